```python
import jax, jax.numpy as jnp
from jax import lax
import numpy as np

D_MODEL = 1024
BATCH = 4
SEQ = 8192
DEPTH = 1
DEC_BATCH = 8
DEC_SEQ = 16
PAST_LEN = 1024

CHUNK = 64
D_MIX = D_MODEL
D_A = D_MIX // 2
D_B = D_MIX - D_A
GROUP_W = 64
CONV_A_W = 3
CONV_B_W = 31
D_IN = 3 * D_A + 2 * D_B
N_EXPERTS = 32
TOP_K = 4
D_FF = D_MODEL
SWIGLU_LIMIT = 7.0
SWIGLU_ALPHA = 1.702
MOE_BLOCK = 256
NORM_EPS = 1e-5

kernel_name = "hybrid_shortconv_conformer_moe_stream_step"


def rms_norm(x, g):
    xf = x.astype(jnp.float32)
    y = xf * lax.rsqrt(jnp.mean(xf * xf, axis=-1, keepdims=True) + NORM_EPS)
    return (y * g.astype(jnp.float32)).astype(x.dtype)


def layer_norm(x, g, b):
    xf = x.astype(jnp.float32)
    mu = jnp.mean(xf, axis=-1, keepdims=True)
    xc = xf - mu
    var = jnp.mean(xc * xc, axis=-1, keepdims=True)
    y = xc * lax.rsqrt(var + NORM_EPS) * g.astype(jnp.float32) + b.astype(jnp.float32)
    return y.astype(x.dtype)


def causal_depthwise_conv(u, hist, w):
    width = w.shape[0]
    xp = jnp.concatenate([hist.astype(u.dtype), u], axis=1)
    y = lax.conv_general_dilated(
        xp, w[:, None, :].astype(u.dtype), window_strides=(1,), padding="VALID",
        dimension_numbers=("NWC", "WIO", "NWC"), feature_group_count=u.shape[-1])
    return y, xp[:, xp.shape[1] - (width - 1):]


def token_mixers(n, hist_a, hist_b, w_in, conv_a_w, conv_b_w, conv_b_b, conv_ln_g, conv_ln_b,
                 out_norm_a_g, out_norm_b_g, w_out):
    z = jnp.einsum("bld,de->ble", n, w_in)
    gate_b, gate_c, xt, v_b, g_b = jnp.split(z, [D_A, 2 * D_A, 3 * D_A, 3 * D_A + D_B], axis=-1)
    conv_a, new_a = causal_depthwise_conv(gate_c * xt, hist_a, conv_a_w)
    y_a = gate_b * conv_a
    conv_b, new_b = causal_depthwise_conv(v_b * jax.nn.sigmoid(g_b), hist_b, conv_b_w)
    y_b = jax.nn.silu(layer_norm(conv_b + conv_b_b, conv_ln_g, conv_ln_b))
    y = jnp.concatenate([rms_norm(y_a, out_norm_a_g), rms_norm(y_b, out_norm_b_g)], axis=-1)
    return jnp.einsum("ble,ed->bld", y, w_out), new_a, new_b


def moe(h, w_router, b_router, w_gate_up, b_gate_up, w_down, b_down):
    bsz, seq, d = h.shape
    n_tok = bsz * seq
    xf = h.reshape(n_tok, d)
    logits = (xf @ w_router + b_router).astype(jnp.float32)
    top_v, top_i = lax.top_k(logits, TOP_K)
    gates = jax.nn.softmax(top_v, axis=-1).astype(h.dtype)
    e_flat = top_i.reshape(-1).astype(jnp.int32)
    tok_flat = jnp.repeat(jnp.arange(n_tok, dtype=jnp.int32), TOP_K)
    g_flat = gates.reshape(-1)
    n_rows = n_tok * TOP_K
    order = jnp.argsort(e_flat)
    e_sorted = e_flat[order]
    counts = jax.ops.segment_sum(jnp.ones_like(e_flat), e_flat, num_segments=N_EXPERTS)
    padded = (counts + MOE_BLOCK - 1) // MOE_BLOCK * MOE_BLOCK
    pad_end = jnp.cumsum(padded)
    pad_start = pad_end - padded
    grp_start = jnp.cumsum(counts) - counts
    rank = jnp.arange(n_rows, dtype=jnp.int32) - grp_start[e_sorted]
    dest = pad_start[e_sorted] + rank
    n_blocks = -(-n_rows // MOE_BLOCK) + N_EXPERTS
    n_pad = n_blocks * MOE_BLOCK
    row_tok = jnp.zeros((n_pad,), jnp.int32).at[dest].set(tok_flat[order])
    row_gate = jnp.zeros((n_pad,), h.dtype).at[dest].set(g_flat[order])
    blk_start = jnp.arange(n_blocks, dtype=jnp.int32) * MOE_BLOCK
    blk_exp = jnp.minimum(jnp.sum(blk_start[:, None] >= pad_end[None, :], axis=1),
                          N_EXPERTS - 1).astype(jnp.int32)

    def expert_block(args):
        tok, gate, e = args
        xb = xf[tok]
        gu = xb @ w_gate_up[e] + b_gate_up[e]
        g, u = jnp.split(gu, 2, axis=-1)
        g = jnp.minimum(g, SWIGLU_LIMIT)
        u = jnp.clip(u, -SWIGLU_LIMIT, SWIGLU_LIMIT)
        act = g * jax.nn.sigmoid(SWIGLU_ALPHA * g) * (u + 1.0)
        return (act @ w_down[e] + b_down[e]) * gate[:, None]

    out = lax.map(expert_block, (row_tok.reshape(n_blocks, MOE_BLOCK),
                                 row_gate.reshape(n_blocks, MOE_BLOCK), blk_exp))
    y = jax.ops.segment_sum(out.reshape(n_pad, d), row_tok, num_segments=n_tok)
    return y.reshape(bsz, seq, d)


def trunk_layer(x, hist_a, hist_b, p):
    (norm_mix_g, w_in, conv_a_w, conv_b_w, conv_b_b, conv_ln_g, conv_ln_b, out_norm_a_g,
     out_norm_b_g, w_out, norm_ffn_g, w_router, b_router, w_gate_up, b_gate_up, w_down, b_down) = p
    mix, new_a, new_b = token_mixers(rms_norm(x, norm_mix_g), hist_a, hist_b, w_in, conv_a_w, conv_b_w,
                                     conv_b_b, conv_ln_g, conv_ln_b, out_norm_a_g, out_norm_b_g, w_out)
    h = x + mix
    h = h + moe(rms_norm(h, norm_ffn_g), w_router, b_router, w_gate_up, b_gate_up, w_down, b_down)
    return h, new_a, new_b


def setup_inputs(seed: int = 0) -> dict:
    key = jax.random.key(seed)
    ks = jax.random.split(key, 24)
    f32 = jnp.float32

    def nrm(k, shape, scale):
        return jax.random.normal(k, shape, f32) * scale

    L = DEPTH
    return {
        "x_prompt": nrm(ks[0], (BATCH, SEQ, D_MODEL), 1.0),
        "x_sample": nrm(ks[1], (DEC_BATCH, DEC_SEQ, D_MODEL), 1.0),
        "state_conv_a": nrm(ks[2], (L, DEC_BATCH, CONV_A_W - 1, D_A), 1.0),
        "state_conv_b": nrm(ks[3], (L, DEC_BATCH, CONV_B_W - 1, D_B), 0.5),
        "norm_mix_g": 1.0 + nrm(ks[4], (L, D_MODEL), 0.02),
        "w_in": nrm(ks[5], (L, D_MODEL, D_IN), D_MODEL ** -0.5),
        "conv_a_w": nrm(ks[6], (L, CONV_A_W, D_A), CONV_A_W ** -0.5),
        "conv_b_w": nrm(ks[7], (L, CONV_B_W, D_B), CONV_B_W ** -0.5),
        "conv_b_b": nrm(ks[8], (L, D_B), 0.02),
        "conv_ln_g": 1.0 + nrm(ks[9], (L, D_B), 0.02),
        "conv_ln_b": nrm(ks[10], (L, D_B), 0.02),
        "out_norm_a_g": 1.0 + nrm(ks[11], (L, D_A), 0.02),
        "out_norm_b_g": 1.0 + nrm(ks[12], (L, D_B), 0.02),
        "w_out": nrm(ks[13], (L, D_MIX, D_MODEL), D_MIX ** -0.5),
        "norm_ffn_g": 1.0 + nrm(ks[14], (L, D_MODEL), 0.02),
        "w_router": nrm(ks[15], (L, D_MODEL, N_EXPERTS), D_MODEL ** -0.5),
        "b_router": nrm(ks[16], (L, N_EXPERTS), 0.01),
        "w_gate_up": nrm(ks[17], (L, N_EXPERTS, D_MODEL, 2 * D_FF), D_MODEL ** -0.5),
        "b_gate_up": nrm(ks[18], (L, N_EXPERTS, 2 * D_FF), 0.01),
        "w_down": nrm(ks[19], (L, N_EXPERTS, D_FF, D_MODEL), D_FF ** -0.5),
        "b_down": nrm(ks[20], (L, N_EXPERTS, D_MODEL), 0.01),
        "final_norm_g": 1.0 + nrm(ks[21], (D_MODEL,), 0.02),
    }


def reference(x_prompt, x_sample, state_conv_a, state_conv_b, norm_mix_g, w_in, conv_a_w, conv_b_w,
              conv_b_b, conv_ln_g, conv_ln_b, out_norm_a_g, out_norm_b_g, w_out, norm_ffn_g, w_router,
              b_router, w_gate_up, b_gate_up, w_down, b_down, final_norm_g):
    hp, hs = x_prompt, x_sample
    zero_a = jnp.zeros((x_prompt.shape[0], CONV_A_W - 1, D_A), x_prompt.dtype)
    zero_b = jnp.zeros((x_prompt.shape[0], CONV_B_W - 1, D_B), x_prompt.dtype)
    pa, pb, sa, sb = [], [], [], []
    for l in range(DEPTH):
        p = (norm_mix_g[l], w_in[l], conv_a_w[l], conv_b_w[l], conv_b_b[l], conv_ln_g[l], conv_ln_b[l],
             out_norm_a_g[l], out_norm_b_g[l], w_out[l], norm_ffn_g[l], w_router[l], b_router[l],
             w_gate_up[l], b_gate_up[l], w_down[l], b_down[l])
        hp, new_pa, new_pb = trunk_layer(hp, zero_a, zero_b, p)
        hs, new_sa, new_sb = trunk_layer(hs, state_conv_a[l], state_conv_b[l], p)
        pa.append(new_pa); pb.append(new_pb); sa.append(new_sa); sb.append(new_sb)
    y_prompt = rms_norm(hp, final_norm_g)
    y_sample = rms_norm(hs, final_norm_g)
    new_conv_a_prompt = jnp.stack(pa, axis=0)
    new_conv_b_prompt = jnp.stack(pb, axis=0)
    new_conv_a_sample = jnp.stack(sa, axis=0)
    new_conv_b_sample = jnp.stack(sb, axis=0)
    return (y_prompt, y_sample, new_conv_a_prompt, new_conv_b_prompt, new_conv_a_sample, new_conv_b_sample)
```

```python
import functools

import jax
import jax.numpy as jnp
from jax import lax
from jax.experimental import pallas as pl
from jax.experimental.pallas import tpu as pltpu

CONV_A_W = 3
CONV_B_W = 31
TOP_K = 4
SWIGLU_LIMIT = 7.0
SWIGLU_ALPHA = 1.702
NORM_EPS = 1e-5
MOE_BLOCK = 256
SUBLANES = 8
HIST_A_ROWS = 8
HIST_B_ROWS = 32
ROW_TILE = 128
VMEM_LIMIT = 56 * 1024 * 1024


def _rms(x, g):
    return x * lax.rsqrt(jnp.mean(x * x, axis=-1, keepdims=True) + NORM_EPS) * g


def _pack_bf16_pairs(a):
    w = a.shape[-1] // 2
    lo = pltpu.bitcast(a[:, :w].astype(jnp.bfloat16).astype(jnp.float32), jnp.uint32)
    hi = pltpu.bitcast(a[:, w:].astype(jnp.bfloat16).astype(jnp.float32), jnp.uint32)
    return (lo >> 16) | (hi & jnp.uint32(0xFFFF0000))


def _unpack_bf16_pairs(wd):
    lo = pltpu.bitcast(wd << 16, jnp.float32)
    hi = pltpu.bitcast(wd & jnp.uint32(0xFFFF0000), jnp.float32)
    return jnp.concatenate([lo, hi], axis=-1)


def _mixer_kernel(x_ref, hista_ref, histb_ref, cnt0_ref, tri_ref,
                  gmix_ref, win_ref, caw_ref, cbw_ref, cbb_ref, lng_ref, lnb_ref,
                  ga_ref, gb_ref, wout_ref, gffn_ref, wrt_ref, br_ref,
                  h_ref, hnp_ref, idx_ref, gate_ref, rank_ref, cnt_ref, newa_ref, newb_ref,
                  ua_buf, ub_buf, cnt_acc, *, nb, ts, d_a, d_b, n_exp):
    b = pl.program_id(0)
    s = pl.program_id(1)
    m = nb * ts
    d = x_ref.shape[-1]

    @pl.when(s == 0)
    def _():
        ua_buf[:, 0:HIST_A_ROWS, :] = hista_ref[...]
        ub_buf[:, 0:HIST_B_ROWS, :] = histb_ref[...]

    @pl.when(s != 0)
    def _():
        ua_buf[:, 0:HIST_A_ROWS, :] = ua_buf[:, ts:ts + HIST_A_ROWS, :]
        ub_buf[:, 0:HIST_B_ROWS, :] = ub_buf[:, ts:ts + HIST_B_ROWS, :]

    @pl.when((b == 0) & (s == 0))
    def _():
        cnt_acc[...] = cnt0_ref[...]

    x = x_ref[...].reshape(m, d)
    n = _rms(x, gmix_ref[...]).astype(jnp.bfloat16)
    z = jnp.dot(n, win_ref[...], preferred_element_type=jnp.float32)
    gate_b = z[:, 0:d_a]
    gate_c = z[:, d_a:2 * d_a]
    xt = z[:, 2 * d_a:3 * d_a]
    v_b = z[:, 3 * d_a:3 * d_a + d_b]
    g_b = z[:, 3 * d_a + d_b:3 * d_a + 2 * d_b]

    ua_buf[:, HIST_A_ROWS:HIST_A_ROWS + ts, :] = (gate_c * xt).reshape(nb, ts, d_a)
    conv_a = jnp.zeros((nb, ts, d_a), jnp.float32)
    for k in range(CONV_A_W):
        off = HIST_A_ROWS - (CONV_A_W - 1) + k
        conv_a = conv_a + caw_ref[k:k + 1, :] * ua_buf[:, off:off + ts, :]
    y_a = gate_b * conv_a.reshape(m, d_a)

    ub_buf[:, HIST_B_ROWS:HIST_B_ROWS + ts, :] = (v_b * jax.nn.sigmoid(g_b)).reshape(nb, ts, d_b)
    conv_b = jnp.zeros((nb, ts, d_b), jnp.float32)
    for k in range(CONV_B_W):
        off = HIST_B_ROWS - (CONV_B_W - 1) + k
        conv_b = conv_b + cbw_ref[k:k + 1, :] * ub_buf[:, off:off + ts, :]
    cb = conv_b.reshape(m, d_b) + cbb_ref[...]
    mu = jnp.mean(cb, axis=-1, keepdims=True)
    xc = cb - mu
    var = jnp.mean(xc * xc, axis=-1, keepdims=True)
    ln = xc * lax.rsqrt(var + NORM_EPS) * lng_ref[...] + lnb_ref[...]
    y_b = ln * jax.nn.sigmoid(ln)

    y = jnp.concatenate([_rms(y_a, ga_ref[...]), _rms(y_b, gb_ref[...])], axis=-1)
    mix = jnp.dot(y.astype(jnp.bfloat16), wout_ref[...], preferred_element_type=jnp.float32)
    h = x + mix
    h_ref[...] = h.reshape(nb, ts, d)

    hn = _rms(h, gffn_ref[...])
    hnp_ref[...] = _pack_bf16_pairs(hn)

    logits = lax.dot_general(wrt_ref[...], hn.astype(jnp.bfloat16), (((1,), (1,)), ((), ())),
                             preferred_element_type=jnp.float32) + br_ref[...]
    e_iota = lax.broadcasted_iota(jnp.int32, (n_exp, m), 0)
    cur = logits
    vals, idxs, sels = [], [], []
    for _ in range(TOP_K):
        mx = jnp.max(cur, axis=0, keepdims=True)
        ix = jnp.min(jnp.where(cur == mx, e_iota, n_exp), axis=0, keepdims=True)
        sel = e_iota == ix
        vals.append(mx)
        idxs.append(ix)
        sels.append(sel)
        cur = jnp.where(sel, -jnp.inf, cur)
    exps = [jnp.exp(v - vals[0]) for v in vals]
    denom = exps[0] + exps[1] + exps[2] + exps[3]
    gates = [e / denom for e in exps]

    onehot = jnp.zeros((n_exp, m), jnp.float32)
    for sel in sels:
        onehot = onehot + sel.astype(jnp.float32)
    before = jnp.dot(onehot.astype(jnp.bfloat16), tri_ref[...],
                     preferred_element_type=jnp.float32) + cnt_acc[...]
    ranks = [jnp.sum(jnp.where(sel, before, 0.0), axis=0, keepdims=True) for sel in sels]
    new_cnt = cnt_acc[...] + jnp.sum(onehot, axis=1, keepdims=True)
    cnt_acc[...] = new_cnt
    cnt_ref[...] = new_cnt

    idx_ref[...] = jnp.concatenate(idxs, axis=0)
    gate_ref[...] = jnp.concatenate(gates, axis=0)
    rank_ref[...] = jnp.concatenate(ranks, axis=0).astype(jnp.int32)

    @pl.when(s == pl.num_programs(1) - 1)
    def _():
        newa_ref[...] = ua_buf[:, ts + HIST_A_ROWS - (CONV_A_W - 1):ts + HIST_A_ROWS, :]
        newb_ref[...] = ub_buf[:, ts + HIST_B_ROWS - (CONV_B_W - 1):ts + HIST_B_ROWS, :]


def _mixer(x, hist_a, hist_b, cnt0, params, *, nb, ts):
    (gmix, win, caw, cbw, cbb, lng, lnb, ga, gb, wout, gffn, wrt, br) = params
    bsz, seq, d = x.shape
    d_a, d_b = caw.shape[-1], cbw.shape[-1]
    n_exp = wrt.shape[0]
    m = nb * ts
    n_tok = bsz * seq
    grid = (bsz // nb, seq // ts)
    tri = (jnp.arange(m)[:, None] < jnp.arange(m)[None, :]).astype(jnp.bfloat16)

    def const(shape):
        return pl.BlockSpec(shape, lambda i, j: (0,) * len(shape))

    tok_map = lambda i, j: (0, i * (seq // ts) + j)
    kern = functools.partial(_mixer_kernel, nb=nb, ts=ts, d_a=d_a, d_b=d_b, n_exp=n_exp)
    out_shape = (
        jax.ShapeDtypeStruct((bsz, seq, d), jnp.float32),
        jax.ShapeDtypeStruct((n_tok, d // 2), jnp.uint32),
        jax.ShapeDtypeStruct((TOP_K, n_tok), jnp.int32),
        jax.ShapeDtypeStruct((TOP_K, n_tok), jnp.float32),
        jax.ShapeDtypeStruct((TOP_K, n_tok), jnp.int32),
        jax.ShapeDtypeStruct((n_exp, 1), jnp.float32),
        jax.ShapeDtypeStruct((bsz, CONV_A_W - 1, d_a), jnp.float32),
        jax.ShapeDtypeStruct((bsz, CONV_B_W - 1, d_b), jnp.float32),
    )
    return pl.pallas_call(
        kern,
        grid=grid,
        in_specs=[
            pl.BlockSpec((nb, ts, d), lambda i, j: (i, j, 0)),
            pl.BlockSpec((nb, HIST_A_ROWS, d_a), lambda i, j: (i, 0, 0)),
            pl.BlockSpec((nb, HIST_B_ROWS, d_b), lambda i, j: (i, 0, 0)),
            const((n_exp, 1)),
            const((m, m)),
            const((1, d)), const(win.shape), const(caw.shape), const(cbw.shape), const((1, d_b)),
            const((1, d_b)), const((1, d_b)), const((1, d_a)), const((1, d_b)), const(wout.shape),
            const((1, d)), const(wrt.shape), const((n_exp, 1)),
        ],
        out_specs=(
            pl.BlockSpec((nb, ts, d), lambda i, j: (i, j, 0)),
            pl.BlockSpec((m, d // 2), lambda i, j: (i * (seq // ts) + j, 0)),
            pl.BlockSpec((TOP_K, m), tok_map),
            pl.BlockSpec((TOP_K, m), tok_map),
            pl.BlockSpec((TOP_K, m), tok_map),
            pl.BlockSpec((n_exp, 1), lambda i, j: (0, 0)),
            pl.BlockSpec((nb, CONV_A_W - 1, d_a), lambda i, j: (i, 0, 0)),
            pl.BlockSpec((nb, CONV_B_W - 1, d_b), lambda i, j: (i, 0, 0)),
        ),
        out_shape=out_shape,
        scratch_shapes=[
            pltpu.VMEM((nb, HIST_A_ROWS + ts, d_a), jnp.float32),
            pltpu.VMEM((nb, HIST_B_ROWS + ts, d_b), jnp.float32),
            pltpu.VMEM((n_exp, 1), jnp.float32),
        ],
        compiler_params=pltpu.CompilerParams(
            dimension_semantics=("arbitrary", "arbitrary"), vmem_limit_bytes=VMEM_LIMIT),
        name="mixer",
    )(x, hist_a, hist_b, cnt0, tri, gmix, win, caw, cbw, cbb, lng, lnb, ga, gb, wout, gffn, wrt, br)


def _row_copy(src, src_row, dst, dst_row, sem):
    return pltpu.make_async_copy(src.at[pl.ds(src_row, 1), :], dst.at[pl.ds(dst_row, 1), :], sem)


def _dispatch_kernel(pos_ref, hnp_ref, xs_in_ref, xs_ref, sem):
    del xs_in_ref
    base = pl.program_id(0) * ROW_TILE

    def issue(t, carry):
        for k in range(TOP_K):
            _row_copy(hnp_ref, base + t, xs_ref, pos_ref[k, t], sem).start()
        return carry

    lax.fori_loop(0, ROW_TILE, issue, 0)
    pltpu.make_async_copy(xs_ref.at[pl.ds(0, TOP_K * ROW_TILE), :],
                          xs_ref.at[pl.ds(0, TOP_K * ROW_TILE), :], sem).wait()


def _dispatch(pos, hnp, xs):
    n_tok = hnp.shape[0]
    return pl.pallas_call(
        _dispatch_kernel,
        grid=(n_tok // ROW_TILE,),
        in_specs=[
            pl.BlockSpec((TOP_K, ROW_TILE), lambda i: (0, i), memory_space=pltpu.SMEM),
            pl.BlockSpec(memory_space=pl.ANY),
            pl.BlockSpec(memory_space=pl.ANY),
        ],
        out_specs=pl.BlockSpec(memory_space=pl.ANY),
        out_shape=jax.ShapeDtypeStruct(xs.shape, xs.dtype),
        scratch_shapes=[pltpu.SemaphoreType.DMA],
        input_output_aliases={2: 0},
        compiler_params=pltpu.CompilerParams(dimension_semantics=("arbitrary",)),
        name="dispatch",
    )(pos, hnp, xs)


def _experts_kernel(blk_exp_ref, n_active_ref, xs_ref, wgu_ref, bgu_ref, wd_ref, bd_ref, out_ref):
    del blk_exp_ref
    d_ff = wd_ref.shape[1]

    @pl.when(pl.program_id(0) < n_active_ref[0])
    def _():
        x = _unpack_bf16_pairs(xs_ref[...]).astype(jnp.bfloat16)
        gu = jnp.dot(x, wgu_ref[0], preferred_element_type=jnp.float32) + bgu_ref[0]
        g = jnp.minimum(gu[:, :d_ff], SWIGLU_LIMIT)
        u = jnp.clip(gu[:, d_ff:], -SWIGLU_LIMIT, SWIGLU_LIMIT)
        act = g * jax.nn.sigmoid(SWIGLU_ALPHA * g) * (u + 1.0)
        o = jnp.dot(act.astype(jnp.bfloat16), wd_ref[0], preferred_element_type=jnp.float32) + bd_ref[0]
        out_ref[...] = _pack_bf16_pairs(o)

    @pl.when(pl.program_id(0) >= n_active_ref[0])
    def _():
        out_ref[...] = jnp.zeros(out_ref.shape, out_ref.dtype)


def _experts(blk_exp, n_active, xs, wgu, bgu, wd, bd):
    n_rows, half = xs.shape
    n_exp, d, two_f = wgu.shape
    d_ff = wd.shape[1]
    n_blocks = n_rows // MOE_BLOCK
    grid_spec = pltpu.PrefetchScalarGridSpec(
        num_scalar_prefetch=2,
        grid=(n_blocks,),
        in_specs=[
            pl.BlockSpec((MOE_BLOCK, half), lambda i, be, na: (i, 0)),
            pl.BlockSpec((1, d, two_f), lambda i, be, na: (be[i], 0, 0)),
            pl.BlockSpec((1, 1, two_f), lambda i, be, na: (be[i], 0, 0)),
            pl.BlockSpec((1, d_ff, d), lambda i, be, na: (be[i], 0, 0)),
            pl.BlockSpec((1, 1, d), lambda i, be, na: (be[i], 0, 0)),
        ],
        out_specs=pl.BlockSpec((MOE_BLOCK, half), lambda i, be, na: (i, 0)),
    )
    return pl.pallas_call(
        _experts_kernel,
        grid_spec=grid_spec,
        out_shape=jax.ShapeDtypeStruct((n_rows, half), jnp.uint32),
        compiler_params=pltpu.CompilerParams(
            dimension_semantics=("arbitrary",), vmem_limit_bytes=VMEM_LIMIT),
        name="experts",
    )(blk_exp, n_active, xs, wgu, bgu.reshape(n_exp, 1, two_f), wd, bd.reshape(n_exp, 1, d))


def _combine_kernel(pos_ref, rows_ref, gate_ref, h_ref, gfin_ref, y_ref, gbuf, sem):
    def issue(t, carry):
        for k in range(TOP_K):
            _row_copy(rows_ref, pos_ref[k, t], gbuf.at[k], t, sem).start()
        return carry

    lax.fori_loop(0, ROW_TILE, issue, 0)
    for k in range(TOP_K):
        pltpu.make_async_copy(gbuf.at[k], gbuf.at[k], sem).wait()

    acc = h_ref[...]
    for k in range(TOP_K):
        acc = acc + gate_ref[:, k:k + 1] * _unpack_bf16_pairs(gbuf[k])
    y_ref[...] = _rms(acc, gfin_ref[...])


def _combine(pos, rows, gate_t, h2d, gfin):
    n_tok, d = h2d.shape
    half = rows.shape[1]
    return pl.pallas_call(
        _combine_kernel,
        grid=(n_tok // ROW_TILE,),
        in_specs=[
            pl.BlockSpec((TOP_K, ROW_TILE), lambda i: (0, i), memory_space=pltpu.SMEM),
            pl.BlockSpec(memory_space=pl.ANY),
            pl.BlockSpec((ROW_TILE, TOP_K), lambda i: (i, 0)),
            pl.BlockSpec((ROW_TILE, d), lambda i: (i, 0)),
            pl.BlockSpec((1, d), lambda i: (0, 0)),
        ],
        out_specs=pl.BlockSpec((ROW_TILE, d), lambda i: (i, 0)),
        out_shape=jax.ShapeDtypeStruct((n_tok, d), jnp.float32),
        scratch_shapes=[pltpu.VMEM((TOP_K, ROW_TILE, half), jnp.uint32), pltpu.SemaphoreType.DMA],
        compiler_params=pltpu.CompilerParams(dimension_semantics=("arbitrary",)),
        name="combine",
    )(pos, rows, gate_t, h2d, gfin)


def _pad_hist(hist, rows):
    return jnp.pad(hist, ((0, 0), (rows - hist.shape[1], 0), (0, 0)))


def kernel(x_prompt, x_sample, state_conv_a, state_conv_b, norm_mix_g, w_in, conv_a_w, conv_b_w, conv_b_b, conv_ln_g, conv_ln_b, out_norm_a_g, out_norm_b_g, w_out, norm_ffn_g, w_router, b_router, w_gate_up, b_gate_up, w_down, b_down, final_norm_g):
    depth = w_in.shape[0]
    assert depth == 1, "single-layer trunk"
    bf16 = jnp.bfloat16
    bsz, seq, d = x_prompt.shape
    dec_b, dec_s, _ = x_sample.shape
    d_a, d_b = conv_a_w.shape[-1], conv_b_w.shape[-1]
    n_exp = w_router.shape[-1]

    params = (norm_mix_g[0][None], w_in[0].astype(bf16), conv_a_w[0], conv_b_w[0], conv_b_b[0][None],
              conv_ln_g[0][None], conv_ln_b[0][None], out_norm_a_g[0][None], out_norm_b_g[0][None],
              w_out[0].astype(bf16), norm_ffn_g[0][None], w_router[0].T.astype(bf16),
              b_router[0][:, None])

    zero_a = jnp.zeros((bsz, HIST_A_ROWS, d_a), jnp.float32)
    zero_b = jnp.zeros((bsz, HIST_B_ROWS, d_b), jnp.float32)
    cnt0 = jnp.zeros((n_exp, 1), jnp.float32)
    h_p, hnp_p, idx_p, gate_p, rank_p, cnt_p, na_p, nb_p = _mixer(
        x_prompt, zero_a, zero_b, cnt0, params, nb=1, ts=min(seq, 256))
    h_s, hnp_s, idx_s, gate_s, rank_s, cnt_s, na_s, nb_s = _mixer(
        x_sample, _pad_hist(state_conv_a[0], HIST_A_ROWS), _pad_hist(state_conv_b[0], HIST_B_ROWS),
        cnt_p, params, nb=dec_b, ts=dec_s)

    counts = cnt_s[:, 0].astype(jnp.int32)
    padded = (counts + MOE_BLOCK - 1) // MOE_BLOCK * MOE_BLOCK
    pad_end = jnp.cumsum(padded)
    pad_start = pad_end - padded
    n_pairs = (bsz * seq + dec_b * dec_s) * TOP_K
    n_blocks = -(-n_pairs // MOE_BLOCK) + n_exp
    blk_start = jnp.arange(n_blocks, dtype=jnp.int32) * MOE_BLOCK
    blk_exp = jnp.minimum(jnp.sum(blk_start[:, None] >= pad_end[None, :], axis=1),
                          n_exp - 1).astype(jnp.int32)
    n_active = (pad_end[-1:] // MOE_BLOCK).astype(jnp.int32)

    def positions(idx, rank):
        onehot = idx[:, :, None] == jnp.arange(n_exp, dtype=jnp.int32)[None, None, :]
        return rank + jnp.sum(jnp.where(onehot, pad_start[None, None, :], 0), axis=-1)

    pos_p = positions(idx_p, rank_p)
    pos_s = positions(idx_s, rank_s)

    xs = jnp.zeros((n_blocks * MOE_BLOCK, d // 2), jnp.uint32)
    xs = _dispatch(pos_p, hnp_p, xs)
    xs = _dispatch(pos_s, hnp_s, xs)

    rows = _experts(blk_exp, n_active, xs, w_gate_up[0].astype(bf16), b_gate_up[0],
                    w_down[0].astype(bf16), b_down[0])

    gfin = final_norm_g[None]
    y_p = _combine(pos_p, rows, gate_p.T, h_p.reshape(bsz * seq, d), gfin).reshape(bsz, seq, d)
    y_s = _combine(pos_s, rows, gate_s.T, h_s.reshape(dec_b * dec_s, d), gfin).reshape(dec_b, dec_s, d)

    return (y_p, y_s, na_p[None], nb_p[None], na_s[None], nb_s[None])
```

```python
import functools

import jax
import jax.numpy as jnp
from jax import lax
from jax.experimental import pallas as pl
from jax.experimental.pallas import tpu as pltpu
from jax.experimental.pallas import tpu_sc as plsc

SC_CORES = 2
SC_SUBCORES = 16
SC_WORKERS = SC_CORES * SC_SUBCORES
SC_CHUNK = 128
CONV_A_W = 3
CONV_B_W = 31
TOP_K = 4
SWIGLU_LIMIT = 7.0
SWIGLU_ALPHA = 1.702
NORM_EPS = 1e-5
MOE_BLOCK = 256
SUBLANES = 8
HIST_A_ROWS = 8
HIST_B_ROWS = 32
ROW_TILE = 128
VMEM_LIMIT = 56 * 1024 * 1024


def _rms(x, g):
    return x * lax.rsqrt(jnp.mean(x * x, axis=-1, keepdims=True) + NORM_EPS) * g


def _pack_bf16_pairs(a):
    w = a.shape[-1] // 2
    lo = pltpu.bitcast(a[:, :w].astype(jnp.bfloat16).astype(jnp.float32), jnp.uint32)
    hi = pltpu.bitcast(a[:, w:].astype(jnp.bfloat16).astype(jnp.float32), jnp.uint32)
    return (lo >> 16) | (hi & jnp.uint32(0xFFFF0000))


def _unpack_bf16_pairs(wd):
    lo = pltpu.bitcast(wd << 16, jnp.float32)
    hi = pltpu.bitcast(wd & jnp.uint32(0xFFFF0000), jnp.float32)
    return jnp.concatenate([lo, hi], axis=-1)


def _mixer_kernel(x_ref, hista_ref, histb_ref, cnt0_ref, tri_ref,
                  gmix_ref, win_ref, caw_ref, cbw_ref, cbb_ref, lng_ref, lnb_ref,
                  ga_ref, gb_ref, wout_ref, gffn_ref, wrt_ref, br_ref,
                  h_ref, hnp_ref, idx_ref, gate_ref, rank_ref, cnt_ref, newa_ref, newb_ref,
                  ua_buf, ub_buf, cnt_acc, *, nb, ts, d_a, d_b, n_exp):
    b = pl.program_id(0)
    s = pl.program_id(1)
    m = nb * ts
    d = x_ref.shape[-1]

    @pl.when(s == 0)
    def _():
        ua_buf[:, 0:HIST_A_ROWS, :] = hista_ref[...]
        ub_buf[:, 0:HIST_B_ROWS, :] = histb_ref[...]

    @pl.when(s != 0)
    def _():
        ua_buf[:, 0:HIST_A_ROWS, :] = ua_buf[:, ts:ts + HIST_A_ROWS, :]
        ub_buf[:, 0:HIST_B_ROWS, :] = ub_buf[:, ts:ts + HIST_B_ROWS, :]

    @pl.when((b == 0) & (s == 0))
    def _():
        cnt_acc[...] = cnt0_ref[...]

    x = x_ref[...].reshape(m, d)
    n = _rms(x, gmix_ref[...]).astype(jnp.bfloat16)
    z = jnp.dot(n, win_ref[...], preferred_element_type=jnp.float32)
    gate_b = z[:, 0:d_a]
    gate_c = z[:, d_a:2 * d_a]
    xt = z[:, 2 * d_a:3 * d_a]
    v_b = z[:, 3 * d_a:3 * d_a + d_b]
    g_b = z[:, 3 * d_a + d_b:3 * d_a + 2 * d_b]

    ua_buf[:, HIST_A_ROWS:HIST_A_ROWS + ts, :] = (gate_c * xt).reshape(nb, ts, d_a)
    conv_a = jnp.zeros((nb, ts, d_a), jnp.float32)
    for k in range(CONV_A_W):
        off = HIST_A_ROWS - (CONV_A_W - 1) + k
        conv_a = conv_a + caw_ref[k:k + 1, :] * ua_buf[:, off:off + ts, :]
    y_a = gate_b * conv_a.reshape(m, d_a)

    ub_buf[:, HIST_B_ROWS:HIST_B_ROWS + ts, :] = (v_b * jax.nn.sigmoid(g_b)).reshape(nb, ts, d_b)
    conv_b = jnp.zeros((nb, ts, d_b), jnp.float32)
    for k in range(CONV_B_W):
        off = HIST_B_ROWS - (CONV_B_W - 1) + k
        conv_b = conv_b + cbw_ref[k:k + 1, :] * ub_buf[:, off:off + ts, :]
    cb = conv_b.reshape(m, d_b) + cbb_ref[...]
    mu = jnp.mean(cb, axis=-1, keepdims=True)
    xc = cb - mu
    var = jnp.mean(xc * xc, axis=-1, keepdims=True)
    ln = xc * lax.rsqrt(var + NORM_EPS) * lng_ref[...] + lnb_ref[...]
    y_b = ln * jax.nn.sigmoid(ln)

    y = jnp.concatenate([_rms(y_a, ga_ref[...]), _rms(y_b, gb_ref[...])], axis=-1)
    mix = jnp.dot(y.astype(jnp.bfloat16), wout_ref[...], preferred_element_type=jnp.float32)
    h = x + mix
    h_ref[...] = h.reshape(nb, ts, d)

    hn = _rms(h, gffn_ref[...])
    hnp_ref[...] = _pack_bf16_pairs(hn)

    logits = lax.dot_general(wrt_ref[...], hn.astype(jnp.bfloat16), (((1,), (1,)), ((), ())),
                             preferred_element_type=jnp.float32) + br_ref[...]
    e_iota = lax.broadcasted_iota(jnp.int32, (n_exp, m), 0)
    cur = logits
    vals, idxs, sels = [], [], []
    for _ in range(TOP_K):
        mx = jnp.max(cur, axis=0, keepdims=True)
        ix = jnp.min(jnp.where(cur == mx, e_iota, n_exp), axis=0, keepdims=True)
        sel = e_iota == ix
        vals.append(mx)
        idxs.append(ix)
        sels.append(sel)
        cur = jnp.where(sel, -jnp.inf, cur)
    exps = [jnp.exp(v - vals[0]) for v in vals]
    denom = exps[0] + exps[1] + exps[2] + exps[3]
    gates = [e / denom for e in exps]

    onehot = jnp.zeros((n_exp, m), jnp.float32)
    for sel in sels:
        onehot = onehot + sel.astype(jnp.float32)
    before = jnp.dot(onehot.astype(jnp.bfloat16), tri_ref[...],
                     preferred_element_type=jnp.float32) + cnt_acc[...]
    ranks = [jnp.sum(jnp.where(sel, before, 0.0), axis=0, keepdims=True) for sel in sels]
    new_cnt = cnt_acc[...] + jnp.sum(onehot, axis=1, keepdims=True)
    cnt_acc[...] = new_cnt
    cnt_ref[...] = new_cnt

    idx_ref[...] = jnp.concatenate(idxs, axis=0)
    gate_ref[...] = jnp.concatenate(gates, axis=0)
    rank_ref[...] = jnp.concatenate(ranks, axis=0).astype(jnp.int32)

    @pl.when(s == pl.num_programs(1) - 1)
    def _():
        newa_ref[...] = ua_buf[:, ts + HIST_A_ROWS - (CONV_A_W - 1):ts + HIST_A_ROWS, :]
        newb_ref[...] = ub_buf[:, ts + HIST_B_ROWS - (CONV_B_W - 1):ts + HIST_B_ROWS, :]


def _mixer(x, hist_a, hist_b, cnt0, params, *, nb, ts):
    (gmix, win, caw, cbw, cbb, lng, lnb, ga, gb, wout, gffn, wrt, br) = params
    bsz, seq, d = x.shape
    d_a, d_b = caw.shape[-1], cbw.shape[-1]
    n_exp = wrt.shape[0]
    m = nb * ts
    n_tok = bsz * seq
    grid = (bsz // nb, seq // ts)
    tri = (jnp.arange(m)[:, None] < jnp.arange(m)[None, :]).astype(jnp.bfloat16)

    def const(shape):
        return pl.BlockSpec(shape, lambda i, j: (0,) * len(shape))

    tok_map = lambda i, j: (0, i * (seq // ts) + j)
    kern = functools.partial(_mixer_kernel, nb=nb, ts=ts, d_a=d_a, d_b=d_b, n_exp=n_exp)
    out_shape = (
        jax.ShapeDtypeStruct((bsz, seq, d), jnp.float32),
        jax.ShapeDtypeStruct((n_tok, d // 2), jnp.uint32),
        jax.ShapeDtypeStruct((TOP_K, n_tok), jnp.int32),
        jax.ShapeDtypeStruct((TOP_K, n_tok), jnp.float32),
        jax.ShapeDtypeStruct((TOP_K, n_tok), jnp.int32),
        jax.ShapeDtypeStruct((n_exp, 1), jnp.float32),
        jax.ShapeDtypeStruct((bsz, CONV_A_W - 1, d_a), jnp.float32),
        jax.ShapeDtypeStruct((bsz, CONV_B_W - 1, d_b), jnp.float32),
    )
    return pl.pallas_call(
        kern,
        grid=grid,
        in_specs=[
            pl.BlockSpec((nb, ts, d), lambda i, j: (i, j, 0)),
            pl.BlockSpec((nb, HIST_A_ROWS, d_a), lambda i, j: (i, 0, 0)),
            pl.BlockSpec((nb, HIST_B_ROWS, d_b), lambda i, j: (i, 0, 0)),
            const((n_exp, 1)),
            const((m, m)),
            const((1, d)), const(win.shape), const(caw.shape), const(cbw.shape), const((1, d_b)),
            const((1, d_b)), const((1, d_b)), const((1, d_a)), const((1, d_b)), const(wout.shape),
            const((1, d)), const(wrt.shape), const((n_exp, 1)),
        ],
        out_specs=(
            pl.BlockSpec((nb, ts, d), lambda i, j: (i, j, 0)),
            pl.BlockSpec((m, d // 2), lambda i, j: (i * (seq // ts) + j, 0)),
            pl.BlockSpec((TOP_K, m), tok_map),
            pl.BlockSpec((TOP_K, m), tok_map),
            pl.BlockSpec((TOP_K, m), tok_map),
            pl.BlockSpec((n_exp, 1), lambda i, j: (0, 0)),
            pl.BlockSpec((nb, CONV_A_W - 1, d_a), lambda i, j: (i, 0, 0)),
            pl.BlockSpec((nb, CONV_B_W - 1, d_b), lambda i, j: (i, 0, 0)),
        ),
        out_shape=out_shape,
        scratch_shapes=[
            pltpu.VMEM((nb, HIST_A_ROWS + ts, d_a), jnp.float32),
            pltpu.VMEM((nb, HIST_B_ROWS + ts, d_b), jnp.float32),
            pltpu.VMEM((n_exp, 1), jnp.float32),
        ],
        compiler_params=pltpu.CompilerParams(
            dimension_semantics=("arbitrary", "arbitrary"), vmem_limit_bytes=VMEM_LIMIT),
        name="mixer",
    )(x, hist_a, hist_b, cnt0, tri, gmix, win, caw, cbw, cbb, lng, lnb, ga, gb, wout, gffn, wrt, br)


def _sc_mesh():
    return plsc.VectorSubcoreMesh(core_axis_name="c", subcore_axis_name="s",
                                  num_cores=SC_CORES, num_subcores=SC_SUBCORES)


def _sc_worker_id():
    return lax.axis_index("s") * SC_CORES + lax.axis_index("c")


def _sc_chunks(n_tok, wid):
    n_chunks = n_tok // SC_CHUNK
    assert n_tok % SC_CHUNK == 0
    if n_chunks % SC_WORKERS == 0:
        per_w = n_chunks // SC_WORKERS
        return None, per_w, lambda i: (wid * per_w + i) * SC_CHUNK
    assert n_chunks <= SC_WORKERS
    return wid < n_chunks, 1, lambda i: wid * SC_CHUNK


def _sc_for_each_chunk(n_tok, fn):
    wid = _sc_worker_id()
    pred, trips, base = _sc_chunks(n_tok, wid)

    def run():
        @pl.loop(0, trips)
        def _(i):
            fn(pl.multiple_of(base(i), SC_CHUNK))

    if pred is None:
        run()
    else:
        pl.when(pred)(run)


def _sc_dispatch(pos_p, hnp_p, pos_s, hnp_s, n_rows):
    half = hnp_p.shape[1]

    @functools.partial(
        pl.kernel, mesh=_sc_mesh(),
        out_type=jax.ShapeDtypeStruct((n_rows, half), jnp.uint32),
        scratch_types=[pltpu.VMEM((TOP_K, SC_CHUNK), jnp.int32),
                       pltpu.VMEM((SC_CHUNK, half), jnp.uint32),
                       pltpu.SemaphoreType.DMA],
        compiler_params=pltpu.CompilerParams(use_tc_tiling_on_sc=True),
        name="sc_dispatch")
    def k(pos_p_hbm, hnp_p_hbm, pos_s_hbm, hnp_s_hbm, xs_hbm, idx_v, rows_v, sem):
        def move(pos_hbm, src_hbm):
            def fn(base):
                pltpu.sync_copy(pos_hbm.at[:, pl.ds(base, SC_CHUNK)], idx_v)
                pltpu.sync_copy(src_hbm.at[pl.ds(base, SC_CHUNK)], rows_v)
                copies = [pltpu.async_copy(rows_v, xs_hbm.at[idx_v.at[j]], sem) for j in range(TOP_K)]
                for c in copies:
                    c.wait()
            return fn

        _sc_for_each_chunk(hnp_p.shape[0], move(pos_p_hbm, hnp_p_hbm))
        _sc_for_each_chunk(hnp_s.shape[0], move(pos_s_hbm, hnp_s_hbm))

    return k(pos_p, hnp_p, pos_s, hnp_s)


def _sc_gather(pos_p, pos_s, rows):
    half = rows.shape[1]
    n_p, n_s = pos_p.shape[1], pos_s.shape[1]

    @functools.partial(
        pl.kernel, mesh=_sc_mesh(),
        out_type=(jax.ShapeDtypeStruct((TOP_K, n_p, half), jnp.uint32),
                  jax.ShapeDtypeStruct((TOP_K, n_s, half), jnp.uint32)),
        scratch_types=[pltpu.VMEM((TOP_K, SC_CHUNK), jnp.int32),
                       pltpu.VMEM((SC_CHUNK, half), jnp.uint32),
                       pltpu.SemaphoreType.DMA],
        compiler_params=pltpu.CompilerParams(use_tc_tiling_on_sc=True),
        name="sc_gather")
    def k(pos_p_hbm, pos_s_hbm, rows_hbm, out_p_hbm, out_s_hbm, idx_v, buf_v, sem):
        def move(pos_hbm, out_hbm):
            def fn(base):
                pltpu.sync_copy(pos_hbm.at[:, pl.ds(base, SC_CHUNK)], idx_v)
                for j in range(TOP_K):
                    pltpu.async_copy(rows_hbm.at[idx_v.at[j]], buf_v, sem).wait()
                    pltpu.sync_copy(buf_v, out_hbm.at[j, pl.ds(base, SC_CHUNK)])
            return fn

        _sc_for_each_chunk(n_p, move(pos_p_hbm, out_p_hbm))
        _sc_for_each_chunk(n_s, move(pos_s_hbm, out_s_hbm))

    return k(pos_p, pos_s, rows)


def _experts_kernel(blk_exp_ref, n_active_ref, xs_ref, wgu_ref, bgu_ref, wd_ref, bd_ref, out_ref):
    del blk_exp_ref
    d_ff = wd_ref.shape[1]

    @pl.when(pl.program_id(0) < n_active_ref[0])
    def _():
        x = _unpack_bf16_pairs(xs_ref[...]).astype(jnp.bfloat16)
        gu = jnp.dot(x, wgu_ref[0], preferred_element_type=jnp.float32) + bgu_ref[0]
        g = jnp.minimum(gu[:, :d_ff], SWIGLU_LIMIT)
        u = jnp.clip(gu[:, d_ff:], -SWIGLU_LIMIT, SWIGLU_LIMIT)
        act = g * jax.nn.sigmoid(SWIGLU_ALPHA * g) * (u + 1.0)
        o = jnp.dot(act.astype(jnp.bfloat16), wd_ref[0], preferred_element_type=jnp.float32) + bd_ref[0]
        out_ref[...] = _pack_bf16_pairs(o)

    @pl.when(pl.program_id(0) >= n_active_ref[0])
    def _():
        out_ref[...] = jnp.zeros(out_ref.shape, out_ref.dtype)


def _experts(blk_exp, n_active, xs, wgu, bgu, wd, bd):
    n_rows, half = xs.shape
    n_exp, d, two_f = wgu.shape
    d_ff = wd.shape[1]
    n_blocks = n_rows // MOE_BLOCK
    grid_spec = pltpu.PrefetchScalarGridSpec(
        num_scalar_prefetch=2,
        grid=(n_blocks,),
        in_specs=[
            pl.BlockSpec((MOE_BLOCK, half), lambda i, be, na: (i, 0)),
            pl.BlockSpec((1, d, two_f), lambda i, be, na: (be[i], 0, 0)),
            pl.BlockSpec((1, 1, two_f), lambda i, be, na: (be[i], 0, 0)),
            pl.BlockSpec((1, d_ff, d), lambda i, be, na: (be[i], 0, 0)),
            pl.BlockSpec((1, 1, d), lambda i, be, na: (be[i], 0, 0)),
        ],
        out_specs=pl.BlockSpec((MOE_BLOCK, half), lambda i, be, na: (i, 0)),
    )
    return pl.pallas_call(
        _experts_kernel,
        grid_spec=grid_spec,
        out_shape=jax.ShapeDtypeStruct((n_rows, half), jnp.uint32),
        compiler_params=pltpu.CompilerParams(
            dimension_semantics=("arbitrary",), vmem_limit_bytes=VMEM_LIMIT),
        name="experts",
    )(blk_exp, n_active, xs, wgu, bgu.reshape(n_exp, 1, two_f), wd, bd.reshape(n_exp, 1, d))


def _combine_kernel(rows_ref, gate_ref, h_ref, gfin_ref, y_ref):
    acc = h_ref[...]
    for k in range(TOP_K):
        acc = acc + gate_ref[:, k:k + 1] * _unpack_bf16_pairs(rows_ref[k])
    y_ref[...] = _rms(acc, gfin_ref[...])


def _combine(rows4, gate_t, h2d, gfin):
    n_tok, d = h2d.shape
    half = rows4.shape[-1]
    tile = min(n_tok, 256)
    return pl.pallas_call(
        _combine_kernel,
        grid=(n_tok // tile,),
        in_specs=[
            pl.BlockSpec((TOP_K, tile, half), lambda i: (0, i, 0)),
            pl.BlockSpec((tile, TOP_K), lambda i: (i, 0)),
            pl.BlockSpec((tile, d), lambda i: (i, 0)),
            pl.BlockSpec((1, d), lambda i: (0, 0)),
        ],
        out_specs=pl.BlockSpec((tile, d), lambda i: (i, 0)),
        out_shape=jax.ShapeDtypeStruct((n_tok, d), jnp.float32),
        compiler_params=pltpu.CompilerParams(dimension_semantics=("arbitrary",)),
        name="combine",
    )(rows4, gate_t, h2d, gfin)


def _pad_hist(hist, rows):
    return jnp.pad(hist, ((0, 0), (rows - hist.shape[1], 0), (0, 0)))


def kernel(x_prompt, x_sample, state_conv_a, state_conv_b, norm_mix_g, w_in, conv_a_w, conv_b_w, conv_b_b, conv_ln_g, conv_ln_b, out_norm_a_g, out_norm_b_g, w_out, norm_ffn_g, w_router, b_router, w_gate_up, b_gate_up, w_down, b_down, final_norm_g):
    depth = w_in.shape[0]
    assert depth == 1, "single-layer trunk"
    bf16 = jnp.bfloat16
    bsz, seq, d = x_prompt.shape
    dec_b, dec_s, _ = x_sample.shape
    d_a, d_b = conv_a_w.shape[-1], conv_b_w.shape[-1]
    n_exp = w_router.shape[-1]

    params = (norm_mix_g[0][None], w_in[0].astype(bf16), conv_a_w[0], conv_b_w[0], conv_b_b[0][None],
              conv_ln_g[0][None], conv_ln_b[0][None], out_norm_a_g[0][None], out_norm_b_g[0][None],
              w_out[0].astype(bf16), norm_ffn_g[0][None], w_router[0].T.astype(bf16),
              b_router[0][:, None])

    zero_a = jnp.zeros((bsz, HIST_A_ROWS, d_a), jnp.float32)
    zero_b = jnp.zeros((bsz, HIST_B_ROWS, d_b), jnp.float32)
    cnt0 = jnp.zeros((n_exp, 1), jnp.float32)
    h_p, hnp_p, idx_p, gate_p, rank_p, cnt_p, na_p, nb_p = _mixer(
        x_prompt, zero_a, zero_b, cnt0, params, nb=1, ts=min(seq, 256))
    h_s, hnp_s, idx_s, gate_s, rank_s, cnt_s, na_s, nb_s = _mixer(
        x_sample, _pad_hist(state_conv_a[0], HIST_A_ROWS), _pad_hist(state_conv_b[0], HIST_B_ROWS),
        cnt_p, params, nb=dec_b, ts=dec_s)

    counts = cnt_s[:, 0].astype(jnp.int32)
    padded = (counts + MOE_BLOCK - 1) // MOE_BLOCK * MOE_BLOCK
    pad_end = jnp.cumsum(padded)
    pad_start = pad_end - padded
    n_pairs = (bsz * seq + dec_b * dec_s) * TOP_K
    n_blocks = -(-n_pairs // MOE_BLOCK) + n_exp
    blk_start = jnp.arange(n_blocks, dtype=jnp.int32) * MOE_BLOCK
    blk_exp = jnp.minimum(jnp.sum(blk_start[:, None] >= pad_end[None, :], axis=1),
                          n_exp - 1).astype(jnp.int32)
    n_active = (pad_end[-1:] // MOE_BLOCK).astype(jnp.int32)

    def positions(idx, rank):
        onehot = idx[:, :, None] == jnp.arange(n_exp, dtype=jnp.int32)[None, None, :]
        return rank + jnp.sum(jnp.where(onehot, pad_start[None, None, :], 0), axis=-1)

    pos_p = positions(idx_p, rank_p)
    pos_s = positions(idx_s, rank_s)

    xs = _sc_dispatch(pos_p, hnp_p, pos_s, hnp_s, n_blocks * MOE_BLOCK)

    rows = _experts(blk_exp, n_active, xs, w_gate_up[0].astype(bf16), b_gate_up[0],
                    w_down[0].astype(bf16), b_down[0])

    rows4_p, rows4_s = _sc_gather(pos_p, pos_s, rows)
    gfin = final_norm_g[None]
    y_p = _combine(rows4_p, gate_p.T, h_p.reshape(bsz * seq, d), gfin).reshape(bsz, seq, d)
    y_s = _combine(rows4_s, gate_s.T, h_s.reshape(dec_b * dec_s, d), gfin).reshape(dec_b, dec_s, d)

    return (y_p, y_s, na_p[None], nb_p[None], na_s[None], nb_s[None])
```

```python
import functools

import jax
import jax.numpy as jnp
from jax import lax
from jax.experimental import pallas as pl
from jax.experimental.pallas import tpu as pltpu
from jax.experimental.pallas import tpu_sc as plsc

SC_CORES = 2
SC_SUBCORES = 16
SC_WORKERS = SC_CORES * SC_SUBCORES
SC_CHUNK = 128
CONV_A_W = 3
CONV_B_W = 31
TOP_K = 4
SWIGLU_LIMIT = 7.0
SWIGLU_ALPHA = 1.702
NORM_EPS = 1e-5
MOE_BLOCK = 256
SUBLANES = 8
LANES = 128
CONV_ROW_CHUNK = 128
HIST_A_ROWS = 8
HIST_B_ROWS = 32
ROW_TILE = 128
VMEM_LIMIT = 56 * 1024 * 1024


def _rms(x, g):
    return x * lax.rsqrt(jnp.mean(x * x, axis=-1, keepdims=True) + NORM_EPS) * g


def _pack_bf16_pairs(a):
    w = a.shape[-1] // 2
    lo = pltpu.bitcast(a[:, :w].astype(jnp.bfloat16).astype(jnp.float32), jnp.uint32)
    hi = pltpu.bitcast(a[:, w:].astype(jnp.bfloat16).astype(jnp.float32), jnp.uint32)
    return (lo >> 16) | (hi & jnp.uint32(0xFFFF0000))


def _unpack_bf16_pairs(wd):
    lo = pltpu.bitcast(wd << 16, jnp.float32)
    hi = pltpu.bitcast(wd & jnp.uint32(0xFFFF0000), jnp.float32)
    return jnp.concatenate([lo, hi], axis=-1)


def _conv_b_chunk(ub_buf, cbw_ref, n_i, r0, rc, c0):
    cs = slice(c0, c0 + LANES)
    lead = HIST_B_ROWS - (CONV_B_W - 1)
    acc = None
    for b in range(SUBLANES):
        rows = rc if b == 0 else rc + SUBLANES
        q = None
        for a in range((HIST_B_ROWS + SUBLANES) // SUBLANES):
            k = SUBLANES * a + b - lead
            if k < 0 or k >= CONV_B_W:
                continue
            start = r0 + SUBLANES * a
            term = cbw_ref[k:k + 1, cs] * ub_buf[n_i, start:start + rows, cs]
            q = term if q is None else q + term
        part = q if b == 0 else q[b:b + rc]
        acc = part if acc is None else acc + part
    return acc


def _mixer_kernel(x_ref, hista_ref, histb_ref, cnt0_ref, tri_ref,
                  gmix_ref, win_ref, caw_ref, cbw_ref, cbb_ref, lng_ref, lnb_ref,
                  ga_ref, gb_ref, wout_ref, gffn_ref, wrt_ref, br_ref,
                  h_ref, hnp_ref, idx_ref, gate_ref, rank_ref, cnt_ref, newa_ref, newb_ref,
                  ua_buf, ub_buf, cb_buf, cnt_acc, *, nb, ts, d_a, d_b, n_exp):
    b = pl.program_id(0)
    s = pl.program_id(1)
    m = nb * ts
    d = x_ref.shape[-1]

    @pl.when(s == 0)
    def _():
        ua_buf[:, 0:HIST_A_ROWS, :] = hista_ref[...]
        ub_buf[:, 0:HIST_B_ROWS, :] = histb_ref[...]

    @pl.when(s != 0)
    def _():
        ua_buf[:, 0:HIST_A_ROWS, :] = ua_buf[:, ts:ts + HIST_A_ROWS, :]
        ub_buf[:, 0:HIST_B_ROWS, :] = ub_buf[:, ts:ts + HIST_B_ROWS, :]

    @pl.when((b == 0) & (s == 0))
    def _():
        cnt_acc[...] = cnt0_ref[...]

    x = x_ref[...].reshape(m, d)
    n = _rms(x, gmix_ref[...]).astype(jnp.bfloat16)
    z = jnp.dot(n, win_ref[...], preferred_element_type=jnp.float32)
    gate_b = z[:, 0:d_a]
    gate_c = z[:, d_a:2 * d_a]
    xt = z[:, 2 * d_a:3 * d_a]
    v_b = z[:, 3 * d_a:3 * d_a + d_b]
    g_b = z[:, 3 * d_a + d_b:3 * d_a + 2 * d_b]

    ua_buf[:, HIST_A_ROWS:HIST_A_ROWS + ts, :] = (gate_c * xt).reshape(nb, ts, d_a)
    conv_a = jnp.zeros((nb, ts, d_a), jnp.float32)
    for k in range(CONV_A_W):
        off = HIST_A_ROWS - (CONV_A_W - 1) + k
        conv_a = conv_a + caw_ref[k:k + 1, :] * ua_buf[:, off:off + ts, :]
    y_a = gate_b * conv_a.reshape(m, d_a)

    ub_buf[:, HIST_B_ROWS:HIST_B_ROWS + ts, :] = (v_b * jax.nn.sigmoid(g_b)).reshape(nb, ts, d_b)
    rc = min(ts, CONV_ROW_CHUNK)
    for n_i in range(nb):
        for r0 in range(0, ts, rc):
            for c0 in range(0, d_b, LANES):
                cb_buf[n_i, r0:r0 + rc, c0:c0 + LANES] = _conv_b_chunk(ub_buf, cbw_ref, n_i, r0, rc, c0)
    cb = cb_buf[...].reshape(m, d_b) + cbb_ref[...]
    mu = jnp.mean(cb, axis=-1, keepdims=True)
    xc = cb - mu
    var = jnp.mean(xc * xc, axis=-1, keepdims=True)
    ln = xc * lax.rsqrt(var + NORM_EPS) * lng_ref[...] + lnb_ref[...]
    y_b = ln * jax.nn.sigmoid(ln)

    y = jnp.concatenate([_rms(y_a, ga_ref[...]), _rms(y_b, gb_ref[...])], axis=-1)
    mix = jnp.dot(y.astype(jnp.bfloat16), wout_ref[...], preferred_element_type=jnp.float32)
    h = x + mix
    h_ref[...] = h.reshape(nb, ts, d)

    hn = _rms(h, gffn_ref[...])
    hnp_ref[...] = _pack_bf16_pairs(hn)

    logits = lax.dot_general(wrt_ref[...], hn.astype(jnp.bfloat16), (((1,), (1,)), ((), ())),
                             preferred_element_type=jnp.float32) + br_ref[...]
    e_iota = lax.broadcasted_iota(jnp.int32, (n_exp, m), 0)
    cur = logits
    vals, idxs, sels = [], [], []
    for _ in range(TOP_K):
        mx = jnp.max(cur, axis=0, keepdims=True)
        ix = jnp.min(jnp.where(cur == mx, e_iota, n_exp), axis=0, keepdims=True)
        sel = e_iota == ix
        vals.append(mx)
        idxs.append(ix)
        sels.append(sel)
        cur = jnp.where(sel, -jnp.inf, cur)
    exps = [jnp.exp(v - vals[0]) for v in vals]
    denom = exps[0] + exps[1] + exps[2] + exps[3]
    gates = [e / denom for e in exps]

    onehot = jnp.zeros((n_exp, m), jnp.float32)
    for sel in sels:
        onehot = onehot + sel.astype(jnp.float32)
    before = jnp.dot(onehot.astype(jnp.bfloat16), tri_ref[...],
                     preferred_element_type=jnp.float32) + cnt_acc[...]
    ranks = [jnp.sum(jnp.where(sel, before, 0.0), axis=0, keepdims=True) for sel in sels]
    new_cnt = cnt_acc[...] + jnp.sum(onehot, axis=1, keepdims=True)
    cnt_acc[...] = new_cnt
    cnt_ref[...] = new_cnt

    idx_ref[...] = jnp.concatenate(idxs, axis=0)
    gate_ref[...] = jnp.concatenate(gates, axis=0)
    rank_ref[...] = jnp.concatenate(ranks, axis=0).astype(jnp.int32)

    @pl.when(s == pl.num_programs(1) - 1)
    def _():
        newa_ref[...] = ua_buf[:, ts + HIST_A_ROWS - (CONV_A_W - 1):ts + HIST_A_ROWS, :]
        newb_ref[...] = ub_buf[:, ts + HIST_B_ROWS - (CONV_B_W - 1):ts + HIST_B_ROWS, :]


def _mixer(x, hist_a, hist_b, cnt0, params, *, nb, ts):
    (gmix, win, caw, cbw, cbb, lng, lnb, ga, gb, wout, gffn, wrt, br) = params
    bsz, seq, d = x.shape
    d_a, d_b = caw.shape[-1], cbw.shape[-1]
    n_exp = wrt.shape[0]
    m = nb * ts
    n_tok = bsz * seq
    grid = (bsz // nb, seq // ts)
    tri = (jnp.arange(m)[:, None] < jnp.arange(m)[None, :]).astype(jnp.bfloat16)

    def const(shape):
        return pl.BlockSpec(shape, lambda i, j: (0,) * len(shape))

    tok_map = lambda i, j: (0, i * (seq // ts) + j)
    kern = functools.partial(_mixer_kernel, nb=nb, ts=ts, d_a=d_a, d_b=d_b, n_exp=n_exp)
    out_shape = (
        jax.ShapeDtypeStruct((bsz, seq, d), jnp.float32),
        jax.ShapeDtypeStruct((n_tok, d // 2), jnp.uint32),
        jax.ShapeDtypeStruct((TOP_K, n_tok), jnp.int32),
        jax.ShapeDtypeStruct((TOP_K, n_tok), jnp.float32),
        jax.ShapeDtypeStruct((TOP_K, n_tok), jnp.int32),
        jax.ShapeDtypeStruct((n_exp, 1), jnp.float32),
        jax.ShapeDtypeStruct((bsz, CONV_A_W - 1, d_a), jnp.float32),
        jax.ShapeDtypeStruct((bsz, CONV_B_W - 1, d_b), jnp.float32),
    )
    return pl.pallas_call(
        kern,
        grid=grid,
        in_specs=[
            pl.BlockSpec((nb, ts, d), lambda i, j: (i, j, 0)),
            pl.BlockSpec((nb, HIST_A_ROWS, d_a), lambda i, j: (i, 0, 0)),
            pl.BlockSpec((nb, HIST_B_ROWS, d_b), lambda i, j: (i, 0, 0)),
            const((n_exp, 1)),
            const((m, m)),
            const((1, d)), const(win.shape), const(caw.shape), const(cbw.shape), const((1, d_b)),
            const((1, d_b)), const((1, d_b)), const((1, d_a)), const((1, d_b)), const(wout.shape),
            const((1, d)), const(wrt.shape), const((n_exp, 1)),
        ],
        out_specs=(
            pl.BlockSpec((nb, ts, d), lambda i, j: (i, j, 0)),
            pl.BlockSpec((m, d // 2), lambda i, j: (i * (seq // ts) + j, 0)),
            pl.BlockSpec((TOP_K, m), tok_map),
            pl.BlockSpec((TOP_K, m), tok_map),
            pl.BlockSpec((TOP_K, m), tok_map),
            pl.BlockSpec((n_exp, 1), lambda i, j: (0, 0)),
            pl.BlockSpec((nb, CONV_A_W - 1, d_a), lambda i, j: (i, 0, 0)),
            pl.BlockSpec((nb, CONV_B_W - 1, d_b), lambda i, j: (i, 0, 0)),
        ),
        out_shape=out_shape,
        scratch_shapes=[
            pltpu.VMEM((nb, HIST_A_ROWS + ts, d_a), jnp.float32),
            pltpu.VMEM((nb, HIST_B_ROWS + ts, d_b), jnp.float32),
            pltpu.VMEM((nb, ts, d_b), jnp.float32),
            pltpu.VMEM((n_exp, 1), jnp.float32),
        ],
        compiler_params=pltpu.CompilerParams(
            dimension_semantics=("arbitrary", "arbitrary"), vmem_limit_bytes=VMEM_LIMIT),
        name="mixer",
    )(x, hist_a, hist_b, cnt0, tri, gmix, win, caw, cbw, cbb, lng, lnb, ga, gb, wout, gffn, wrt, br)


def _sc_mesh():
    return plsc.VectorSubcoreMesh(core_axis_name="c", subcore_axis_name="s",
                                  num_cores=SC_CORES, num_subcores=SC_SUBCORES)


def _sc_worker_id():
    return lax.axis_index("s") * SC_CORES + lax.axis_index("c")


def _sc_chunks(n_tok, wid):
    n_chunks = n_tok // SC_CHUNK
    assert n_tok % SC_CHUNK == 0
    if n_chunks % SC_WORKERS == 0:
        per_w = n_chunks // SC_WORKERS
        return None, per_w, lambda i: (wid * per_w + i) * SC_CHUNK
    assert n_chunks <= SC_WORKERS
    return wid < n_chunks, 1, lambda i: wid * SC_CHUNK


def _sc_for_each_chunk(n_tok, fn):
    wid = _sc_worker_id()
    pred, trips, base = _sc_chunks(n_tok, wid)

    def run():
        @pl.loop(0, trips)
        def _(i):
            fn(pl.multiple_of(base(i), SC_CHUNK))

    if pred is None:
        run()
    else:
        pl.when(pred)(run)


def _sc_dispatch(pos_p, hnp_p, pos_s, hnp_s, n_rows):
    half = hnp_p.shape[1]

    @functools.partial(
        pl.kernel, mesh=_sc_mesh(),
        out_type=jax.ShapeDtypeStruct((n_rows, half), jnp.uint32),
        scratch_types=[pltpu.VMEM((TOP_K, SC_CHUNK), jnp.int32),
                       pltpu.VMEM((SC_CHUNK, half), jnp.uint32),
                       pltpu.SemaphoreType.DMA],
        compiler_params=pltpu.CompilerParams(use_tc_tiling_on_sc=True),
        name="sc_dispatch")
    def k(pos_p_hbm, hnp_p_hbm, pos_s_hbm, hnp_s_hbm, xs_hbm, idx_v, rows_v, sem):
        def move(pos_hbm, src_hbm):
            def fn(base):
                pltpu.sync_copy(pos_hbm.at[:, pl.ds(base, SC_CHUNK)], idx_v)
                pltpu.sync_copy(src_hbm.at[pl.ds(base, SC_CHUNK)], rows_v)
                copies = [pltpu.async_copy(rows_v, xs_hbm.at[idx_v.at[j]], sem) for j in range(TOP_K)]
                for c in copies:
                    c.wait()
            return fn

        _sc_for_each_chunk(hnp_p.shape[0], move(pos_p_hbm, hnp_p_hbm))
        _sc_for_each_chunk(hnp_s.shape[0], move(pos_s_hbm, hnp_s_hbm))

    return k(pos_p, hnp_p, pos_s, hnp_s)


def _sc_gather(pos_p, pos_s, rows):
    half = rows.shape[1]
    n_p, n_s = pos_p.shape[1], pos_s.shape[1]

    @functools.partial(
        pl.kernel, mesh=_sc_mesh(),
        out_type=(jax.ShapeDtypeStruct((TOP_K, n_p, half), jnp.uint32),
                  jax.ShapeDtypeStruct((TOP_K, n_s, half), jnp.uint32)),
        scratch_types=[pltpu.VMEM((TOP_K, SC_CHUNK), jnp.int32),
                       pltpu.VMEM((SC_CHUNK, half), jnp.uint32),
                       pltpu.SemaphoreType.DMA],
        compiler_params=pltpu.CompilerParams(use_tc_tiling_on_sc=True),
        name="sc_gather")
    def k(pos_p_hbm, pos_s_hbm, rows_hbm, out_p_hbm, out_s_hbm, idx_v, buf_v, sem):
        def move(pos_hbm, out_hbm):
            def fn(base):
                pltpu.sync_copy(pos_hbm.at[:, pl.ds(base, SC_CHUNK)], idx_v)
                for j in range(TOP_K):
                    pltpu.async_copy(rows_hbm.at[idx_v.at[j]], buf_v, sem).wait()
                    pltpu.sync_copy(buf_v, out_hbm.at[j, pl.ds(base, SC_CHUNK)])
            return fn

        _sc_for_each_chunk(n_p, move(pos_p_hbm, out_p_hbm))
        _sc_for_each_chunk(n_s, move(pos_s_hbm, out_s_hbm))

    return k(pos_p, pos_s, rows)


def _experts_kernel(blk_exp_ref, first_ref, slot_ref, next_ref, n_active_ref,
                    xs_ref, wgu_hbm, bgu_ref, wd_hbm, bd_ref, out_ref,
                    wgu_f32, wd_f32, wgu_bf, wd_bf, sems):
    i = pl.program_id(0)
    d_ff = wd_bf.shape[0]

    def fetch(e, sl):
        return (pltpu.make_async_copy(wgu_hbm.at[e], wgu_f32.at[sl], sems.at[0, sl]),
                pltpu.make_async_copy(wd_hbm.at[e], wd_f32.at[sl], sems.at[1, sl]))

    @pl.when(i < n_active_ref[0])
    def _():
        sl = slot_ref[i]

        @pl.when(first_ref[i] == 1)
        def _():
            @pl.when(i == 0)
            def _():
                for c in fetch(blk_exp_ref[i], sl):
                    c.start()

            for c in fetch(blk_exp_ref[i], sl):
                c.wait()
            wgu_bf[...] = wgu_f32[sl].astype(jnp.bfloat16)
            wd_bf[...] = wd_f32[sl].astype(jnp.bfloat16)

            @pl.when(next_ref[i] >= 0)
            def _():
                for c in fetch(next_ref[i], 1 - sl):
                    c.start()

        x = _unpack_bf16_pairs(xs_ref[...]).astype(jnp.bfloat16)
        gu = jnp.dot(x, wgu_bf[...], preferred_element_type=jnp.float32) + bgu_ref[0]
        g = jnp.minimum(gu[:, :d_ff], SWIGLU_LIMIT)
        u = jnp.clip(gu[:, d_ff:], -SWIGLU_LIMIT, SWIGLU_LIMIT)
        act = g * jax.nn.sigmoid(SWIGLU_ALPHA * g) * (u + 1.0)
        o = jnp.dot(act.astype(jnp.bfloat16), wd_bf[...], preferred_element_type=jnp.float32) + bd_ref[0]
        out_ref[...] = _pack_bf16_pairs(o)

    @pl.when(i >= n_active_ref[0])
    def _():
        out_ref[...] = jnp.zeros(out_ref.shape, out_ref.dtype)


def _experts(plan, xs, wgu, bgu, wd, bd):
    blk_exp, first, slot, nxt, n_active = plan
    n_rows, half = xs.shape
    n_exp, d, two_f = wgu.shape
    d_ff = wd.shape[1]
    n_blocks = n_rows // MOE_BLOCK
    grid_spec = pltpu.PrefetchScalarGridSpec(
        num_scalar_prefetch=5,
        grid=(n_blocks,),
        in_specs=[
            pl.BlockSpec((MOE_BLOCK, half), lambda i, be, *_: (i, 0)),
            pl.BlockSpec(memory_space=pl.ANY),
            pl.BlockSpec((1, 1, two_f), lambda i, be, *_: (be[i], 0, 0)),
            pl.BlockSpec(memory_space=pl.ANY),
            pl.BlockSpec((1, 1, d), lambda i, be, *_: (be[i], 0, 0)),
        ],
        out_specs=pl.BlockSpec((MOE_BLOCK, half), lambda i, be, *_: (i, 0)),
        scratch_shapes=[
            pltpu.VMEM((2, d, two_f), jnp.float32),
            pltpu.VMEM((2, d_ff, d), jnp.float32),
            pltpu.VMEM((d, two_f), jnp.bfloat16),
            pltpu.VMEM((d_ff, d), jnp.bfloat16),
            pltpu.SemaphoreType.DMA((2, 2)),
        ],
    )
    return pl.pallas_call(
        _experts_kernel,
        grid_spec=grid_spec,
        out_shape=jax.ShapeDtypeStruct((n_rows, half), jnp.uint32),
        compiler_params=pltpu.CompilerParams(
            dimension_semantics=("arbitrary",), vmem_limit_bytes=VMEM_LIMIT),
        name="experts",
    )(blk_exp, first, slot, nxt, n_active, xs, wgu, bgu.reshape(n_exp, 1, two_f), wd,
      bd.reshape(n_exp, 1, d))


def _combine_kernel(rows_ref, gate_ref, h_ref, gfin_ref, y_ref):
    acc = h_ref[...]
    for k in range(TOP_K):
        acc = acc + gate_ref[:, k:k + 1] * _unpack_bf16_pairs(rows_ref[k])
    y_ref[...] = _rms(acc, gfin_ref[...])


def _combine(rows4, gate_t, h2d, gfin):
    n_tok, d = h2d.shape
    half = rows4.shape[-1]
    tile = min(n_tok, 256)
    return pl.pallas_call(
        _combine_kernel,
        grid=(n_tok // tile,),
        in_specs=[
            pl.BlockSpec((TOP_K, tile, half), lambda i: (0, i, 0)),
            pl.BlockSpec((tile, TOP_K), lambda i: (i, 0)),
            pl.BlockSpec((tile, d), lambda i: (i, 0)),
            pl.BlockSpec((1, d), lambda i: (0, 0)),
        ],
        out_specs=pl.BlockSpec((tile, d), lambda i: (i, 0)),
        out_shape=jax.ShapeDtypeStruct((n_tok, d), jnp.float32),
        compiler_params=pltpu.CompilerParams(dimension_semantics=("arbitrary",)),
        name="combine",
    )(rows4, gate_t, h2d, gfin)


def _pad_hist(hist, rows):
    return jnp.pad(hist, ((0, 0), (rows - hist.shape[1], 0), (0, 0)))


def kernel(x_prompt, x_sample, state_conv_a, state_conv_b, norm_mix_g, w_in, conv_a_w, conv_b_w, conv_b_b, conv_ln_g, conv_ln_b, out_norm_a_g, out_norm_b_g, w_out, norm_ffn_g, w_router, b_router, w_gate_up, b_gate_up, w_down, b_down, final_norm_g):
    depth = w_in.shape[0]
    assert depth == 1, "single-layer trunk"
    bf16 = jnp.bfloat16
    bsz, seq, d = x_prompt.shape
    dec_b, dec_s, _ = x_sample.shape
    d_a, d_b = conv_a_w.shape[-1], conv_b_w.shape[-1]
    n_exp = w_router.shape[-1]

    params = (norm_mix_g[0][None], w_in[0].astype(bf16), conv_a_w[0], conv_b_w[0], conv_b_b[0][None],
              conv_ln_g[0][None], conv_ln_b[0][None], out_norm_a_g[0][None], out_norm_b_g[0][None],
              w_out[0].astype(bf16), norm_ffn_g[0][None], w_router[0].T.astype(bf16),
              b_router[0][:, None])

    zero_a = jnp.zeros((bsz, HIST_A_ROWS, d_a), jnp.float32)
    zero_b = jnp.zeros((bsz, HIST_B_ROWS, d_b), jnp.float32)
    cnt0 = jnp.zeros((n_exp, 1), jnp.float32)
    h_p, hnp_p, idx_p, gate_p, rank_p, cnt_p, na_p, nb_p = _mixer(
        x_prompt, zero_a, zero_b, cnt0, params, nb=1, ts=min(seq, 256))
    h_s, hnp_s, idx_s, gate_s, rank_s, cnt_s, na_s, nb_s = _mixer(
        x_sample, _pad_hist(state_conv_a[0], HIST_A_ROWS), _pad_hist(state_conv_b[0], HIST_B_ROWS),
        cnt_p, params, nb=dec_b, ts=dec_s)

    counts = cnt_s[:, 0].astype(jnp.int32)
    padded = (counts + MOE_BLOCK - 1) // MOE_BLOCK * MOE_BLOCK
    pad_end = jnp.cumsum(padded)
    pad_start = pad_end - padded
    n_pairs = (bsz * seq + dec_b * dec_s) * TOP_K
    n_blocks = -(-n_pairs // MOE_BLOCK) + n_exp
    blk_start = jnp.arange(n_blocks, dtype=jnp.int32) * MOE_BLOCK
    blk_exp = jnp.minimum(jnp.sum(blk_start[:, None] >= pad_end[None, :], axis=1),
                          n_exp - 1).astype(jnp.int32)
    n_active = (pad_end[-1:] // MOE_BLOCK).astype(jnp.int32)
    e_ids = jnp.arange(n_exp, dtype=jnp.int32)
    has = padded > 0
    slot_e = (jnp.cumsum(has.astype(jnp.int32)) - 1) % 2
    later = jnp.where(has, e_ids, n_exp)
    next_e = jnp.concatenate([lax.cummin(later, reverse=True)[1:], jnp.full((1,), n_exp, jnp.int32)])
    next_e = jnp.where(next_e >= n_exp, -1, next_e)
    blk_first = ((blk_start == pad_start[blk_exp]) & (blk_start < pad_end[-1])).astype(jnp.int32)
    plan = (blk_exp, blk_first, slot_e[blk_exp].astype(jnp.int32), next_e[blk_exp].astype(jnp.int32),
            n_active)

    def positions(idx, rank):
        onehot = idx[:, :, None] == jnp.arange(n_exp, dtype=jnp.int32)[None, None, :]
        return rank + jnp.sum(jnp.where(onehot, pad_start[None, None, :], 0), axis=-1)

    pos_p = positions(idx_p, rank_p)
    pos_s = positions(idx_s, rank_s)

    xs = _sc_dispatch(pos_p, hnp_p, pos_s, hnp_s, n_blocks * MOE_BLOCK)

    rows = _experts(plan, xs, w_gate_up[0], b_gate_up[0], w_down[0], b_down[0])

    rows4_p, rows4_s = _sc_gather(pos_p, pos_s, rows)
    gfin = final_norm_g[None]
    y_p = _combine(rows4_p, gate_p.T, h_p.reshape(bsz * seq, d), gfin).reshape(bsz, seq, d)
    y_s = _combine(rows4_s, gate_s.T, h_s.reshape(dec_b * dec_s, d), gfin).reshape(dec_b, dec_s, d)

    return (y_p, y_s, na_p[None], nb_p[None], na_s[None], nb_s[None])
```

```python
import functools

import jax
import jax.numpy as jnp
from jax import lax
from jax.experimental import pallas as pl
from jax.experimental.pallas import tpu as pltpu
from jax.experimental.pallas import tpu_sc as plsc

SC_CORES = 2
SC_SUBCORES = 16
SC_WORKERS = SC_CORES * SC_SUBCORES
SC_CHUNK = 128
CONV_A_W = 3
CONV_B_W = 31
TOP_K = 4
SWIGLU_LIMIT = 7.0
SWIGLU_ALPHA = 1.702
NORM_EPS = 1e-5
MOE_BLOCK = 256
FF_GROUP = 256
SUBLANES = 8
LANES = 128
CONV_ROW_CHUNK = 128
HIST_A_ROWS = 8
HIST_B_ROWS = 32
COMBINE_CHUNKS = 4
VMEM_LIMIT = 56 * 1024 * 1024


def _rms(x, g):
    return x * lax.rsqrt(jnp.mean(x * x, axis=-1, keepdims=True) + NORM_EPS) * g


def _pack_bf16_pairs(a):
    w = a.shape[-1] // 2
    lo = pltpu.bitcast(a[:, :w].astype(jnp.bfloat16).astype(jnp.float32), jnp.uint32)
    hi = pltpu.bitcast(a[:, w:].astype(jnp.bfloat16).astype(jnp.float32), jnp.uint32)
    return (lo >> 16) | (hi & jnp.uint32(0xFFFF0000))


def _unpack_bf16_pairs(wd):
    lo = pltpu.bitcast(wd << 16, jnp.float32)
    hi = pltpu.bitcast(wd & jnp.uint32(0xFFFF0000), jnp.float32)
    return jnp.concatenate([lo, hi], axis=-1)


def _conv_b_chunk(ub_buf, cbw_ref, n_i, r0, rc, c0):
    cs = slice(c0, c0 + LANES)
    lead = HIST_B_ROWS - (CONV_B_W - 1)
    acc = None
    for b in range(SUBLANES):
        rows = rc if b == 0 else rc + SUBLANES
        q = None
        for a in range((HIST_B_ROWS + SUBLANES) // SUBLANES):
            k = SUBLANES * a + b - lead
            if k < 0 or k >= CONV_B_W:
                continue
            start = r0 + SUBLANES * a
            term = cbw_ref[k:k + 1, cs] * ub_buf[n_i, start:start + rows, cs]
            q = term if q is None else q + term
        part = q if b == 0 else q[b:b + rc]
        acc = part if acc is None else acc + part
    return acc


def _mixer_kernel(x_ref, hista_ref, histb_ref, cnt0_ref, tri_ref,
                  gmix_ref, win_ref, caw_ref, cbw_ref, cbb_ref, lng_ref, lnb_ref,
                  ga_ref, gb_ref, wout_ref, gffn_ref, wrt_ref, br_ref,
                  h_ref, hnp_ref, idx_ref, gate_ref, rank_ref, cnt_ref, newa_ref, newb_ref,
                  ua_buf, ub_buf, cb_buf, cnt_acc, *, nb, ts, d_a, d_b, n_exp):
    b = pl.program_id(0)
    s = pl.program_id(1)
    m = nb * ts
    d = x_ref.shape[-1]

    @pl.when(s == 0)
    def _():
        ua_buf[:, 0:HIST_A_ROWS, :] = hista_ref[...]
        ub_buf[:, 0:HIST_B_ROWS, :] = histb_ref[...]

    @pl.when(s != 0)
    def _():
        ua_buf[:, 0:HIST_A_ROWS, :] = ua_buf[:, ts:ts + HIST_A_ROWS, :]
        ub_buf[:, 0:HIST_B_ROWS, :] = ub_buf[:, ts:ts + HIST_B_ROWS, :]

    @pl.when((b == 0) & (s == 0))
    def _():
        cnt_acc[...] = cnt0_ref[...]

    x = x_ref[...].reshape(m, d)
    n = _rms(x, gmix_ref[...]).astype(jnp.bfloat16)
    z = jnp.dot(n, win_ref[...], preferred_element_type=jnp.float32)
    gate_b = z[:, 0:d_a]
    gate_c = z[:, d_a:2 * d_a]
    xt = z[:, 2 * d_a:3 * d_a]
    v_b = z[:, 3 * d_a:3 * d_a + d_b]
    g_b = z[:, 3 * d_a + d_b:3 * d_a + 2 * d_b]

    ua_buf[:, HIST_A_ROWS:HIST_A_ROWS + ts, :] = (gate_c * xt).reshape(nb, ts, d_a)
    conv_a = jnp.zeros((nb, ts, d_a), jnp.float32)
    for k in range(CONV_A_W):
        off = HIST_A_ROWS - (CONV_A_W - 1) + k
        conv_a = conv_a + caw_ref[k:k + 1, :] * ua_buf[:, off:off + ts, :]
    y_a = gate_b * conv_a.reshape(m, d_a)

    ub_buf[:, HIST_B_ROWS:HIST_B_ROWS + ts, :] = (v_b * jax.nn.sigmoid(g_b)).reshape(nb, ts, d_b)
    rc = min(ts, CONV_ROW_CHUNK)
    for n_i in range(nb):
        for r0 in range(0, ts, rc):
            for c0 in range(0, d_b, LANES):
                cb_buf[n_i, r0:r0 + rc, c0:c0 + LANES] = _conv_b_chunk(ub_buf, cbw_ref, n_i, r0, rc, c0)
    cb = cb_buf[...].reshape(m, d_b) + cbb_ref[...]
    mu = jnp.mean(cb, axis=-1, keepdims=True)
    xc = cb - mu
    var = jnp.mean(xc * xc, axis=-1, keepdims=True)
    ln = xc * lax.rsqrt(var + NORM_EPS) * lng_ref[...] + lnb_ref[...]
    y_b = ln * jax.nn.sigmoid(ln)

    y = jnp.concatenate([_rms(y_a, ga_ref[...]), _rms(y_b, gb_ref[...])], axis=-1)
    mix = jnp.dot(y.astype(jnp.bfloat16), wout_ref[...], preferred_element_type=jnp.float32)
    h = x + mix
    h_ref[...] = h.reshape(nb, ts, d)

    hn = _rms(h, gffn_ref[...])
    hnp_ref[...] = _pack_bf16_pairs(hn)

    logits = lax.dot_general(wrt_ref[...], hn.astype(jnp.bfloat16), (((1,), (1,)), ((), ())),
                             preferred_element_type=jnp.float32) + br_ref[...]
    e_iota = lax.broadcasted_iota(jnp.int32, (n_exp, m), 0)
    cur = logits
    vals, idxs, sels = [], [], []
    for _ in range(TOP_K):
        mx = jnp.max(cur, axis=0, keepdims=True)
        ix = jnp.min(jnp.where(cur == mx, e_iota, n_exp), axis=0, keepdims=True)
        sel = e_iota == ix
        vals.append(mx)
        idxs.append(ix)
        sels.append(sel)
        cur = jnp.where(sel, -jnp.inf, cur)
    exps = [jnp.exp(v - vals[0]) for v in vals]
    denom = exps[0] + exps[1] + exps[2] + exps[3]
    gates = [e / denom for e in exps]

    onehot = jnp.zeros((n_exp, m), jnp.float32)
    for sel in sels:
        onehot = onehot + sel.astype(jnp.float32)
    before = jnp.dot(onehot.astype(jnp.bfloat16), tri_ref[...],
                     preferred_element_type=jnp.float32) + cnt_acc[...]
    ranks = [jnp.sum(jnp.where(sel, before, 0.0), axis=0, keepdims=True) for sel in sels]
    new_cnt = cnt_acc[...] + jnp.sum(onehot, axis=1, keepdims=True)
    cnt_acc[...] = new_cnt
    cnt_ref[...] = new_cnt

    idx_ref[...] = jnp.concatenate(idxs, axis=0)
    gate_ref[...] = jnp.concatenate(gates, axis=0)
    rank_ref[...] = jnp.concatenate(ranks, axis=0).astype(jnp.int32)

    @pl.when(s == pl.num_programs(1) - 1)
    def _():
        newa_ref[...] = ua_buf[:, ts + HIST_A_ROWS - (CONV_A_W - 1):ts + HIST_A_ROWS, :]
        newb_ref[...] = ub_buf[:, ts + HIST_B_ROWS - (CONV_B_W - 1):ts + HIST_B_ROWS, :]


def _mixer(x, hist_a, hist_b, cnt0, params, *, nb, ts):
    (gmix, win, caw, cbw, cbb, lng, lnb, ga, gb, wout, gffn, wrt, br) = params
    bsz, seq, d = x.shape
    d_a, d_b = caw.shape[-1], cbw.shape[-1]
    n_exp = wrt.shape[0]
    m = nb * ts
    n_tok = bsz * seq
    grid = (bsz // nb, seq // ts)
    tri = (jnp.arange(m)[:, None] < jnp.arange(m)[None, :]).astype(jnp.bfloat16)

    def const(shape):
        return pl.BlockSpec(shape, lambda i, j: (0,) * len(shape))

    tok_map = lambda i, j: (0, i * (seq // ts) + j)
    kern = functools.partial(_mixer_kernel, nb=nb, ts=ts, d_a=d_a, d_b=d_b, n_exp=n_exp)
    out_shape = (
        jax.ShapeDtypeStruct((bsz, seq, d), jnp.float32),
        jax.ShapeDtypeStruct((n_tok, d // 2), jnp.uint32),
        jax.ShapeDtypeStruct((TOP_K, n_tok), jnp.int32),
        jax.ShapeDtypeStruct((TOP_K, n_tok), jnp.float32),
        jax.ShapeDtypeStruct((TOP_K, n_tok), jnp.int32),
        jax.ShapeDtypeStruct((n_exp, 1), jnp.float32),
        jax.ShapeDtypeStruct((bsz, CONV_A_W - 1, d_a), jnp.float32),
        jax.ShapeDtypeStruct((bsz, CONV_B_W - 1, d_b), jnp.float32),
    )
    return pl.pallas_call(
        kern,
        grid=grid,
        in_specs=[
            pl.BlockSpec((nb, ts, d), lambda i, j: (i, j, 0)),
            pl.BlockSpec((nb, HIST_A_ROWS, d_a), lambda i, j: (i, 0, 0)),
            pl.BlockSpec((nb, HIST_B_ROWS, d_b), lambda i, j: (i, 0, 0)),
            const((n_exp, 1)),
            const((m, m)),
            const((1, d)), const(win.shape), const(caw.shape), const(cbw.shape), const((1, d_b)),
            const((1, d_b)), const((1, d_b)), const((1, d_a)), const((1, d_b)), const(wout.shape),
            const((1, d)), const(wrt.shape), const((n_exp, 1)),
        ],
        out_specs=(
            pl.BlockSpec((nb, ts, d), lambda i, j: (i, j, 0)),
            pl.BlockSpec((m, d // 2), lambda i, j: (i * (seq // ts) + j, 0)),
            pl.BlockSpec((TOP_K, m), tok_map),
            pl.BlockSpec((TOP_K, m), tok_map),
            pl.BlockSpec((TOP_K, m), tok_map),
            pl.BlockSpec((n_exp, 1), lambda i, j: (0, 0)),
            pl.BlockSpec((nb, CONV_A_W - 1, d_a), lambda i, j: (i, 0, 0)),
            pl.BlockSpec((nb, CONV_B_W - 1, d_b), lambda i, j: (i, 0, 0)),
        ),
        out_shape=out_shape,
        scratch_shapes=[
            pltpu.VMEM((nb, HIST_A_ROWS + ts, d_a), jnp.float32),
            pltpu.VMEM((nb, HIST_B_ROWS + ts, d_b), jnp.float32),
            pltpu.VMEM((nb, ts, d_b), jnp.float32),
            pltpu.VMEM((n_exp, 1), jnp.float32),
        ],
        compiler_params=pltpu.CompilerParams(
            dimension_semantics=("arbitrary", "arbitrary"), vmem_limit_bytes=VMEM_LIMIT),
        name="mixer",
    )(x, hist_a, hist_b, cnt0, tri, gmix, win, caw, cbw, cbb, lng, lnb, ga, gb, wout, gffn, wrt, br)


def _sc_mesh():
    return plsc.VectorSubcoreMesh(core_axis_name="c", subcore_axis_name="s",
                                  num_cores=SC_CORES, num_subcores=SC_SUBCORES)


def _sc_worker_id():
    return lax.axis_index("s") * SC_CORES + lax.axis_index("c")


def _sc_chunks(n_tok, wid):
    n_chunks = n_tok // SC_CHUNK
    assert n_tok % SC_CHUNK == 0
    if n_chunks % SC_WORKERS == 0:
        per_w = n_chunks // SC_WORKERS
        return None, per_w, lambda i: (wid * per_w + i) * SC_CHUNK
    assert n_chunks <= SC_WORKERS
    return wid < n_chunks, 1, lambda i: wid * SC_CHUNK


def _sc_for_each_chunk(n_tok, fn):
    wid = _sc_worker_id()
    pred, trips, base = _sc_chunks(n_tok, wid)

    def run():
        @pl.loop(0, trips)
        def _(i):
            fn(pl.multiple_of(base(i), SC_CHUNK))

    if pred is None:
        run()
    else:
        pl.when(pred)(run)


def _sc_dispatch(pos_p, hnp_p, pos_s, hnp_s, n_rows):
    half = hnp_p.shape[1]

    @functools.partial(
        pl.kernel, mesh=_sc_mesh(),
        out_type=jax.ShapeDtypeStruct((n_rows, half), jnp.uint32),
        scratch_types=[pltpu.VMEM((TOP_K, SC_CHUNK), jnp.int32),
                       pltpu.VMEM((SC_CHUNK, half), jnp.uint32),
                       pltpu.SemaphoreType.DMA],
        compiler_params=pltpu.CompilerParams(use_tc_tiling_on_sc=True),
        name="sc_dispatch")
    def k(pos_p_hbm, hnp_p_hbm, pos_s_hbm, hnp_s_hbm, xs_hbm, idx_v, rows_v, sem):
        def move(pos_hbm, src_hbm):
            def fn(base):
                pltpu.sync_copy(pos_hbm.at[:, pl.ds(base, SC_CHUNK)], idx_v)
                pltpu.sync_copy(src_hbm.at[pl.ds(base, SC_CHUNK)], rows_v)
                copies = [pltpu.async_copy(rows_v, xs_hbm.at[idx_v.at[j]], sem) for j in range(TOP_K)]
                for c in copies:
                    c.wait()
            return fn

        _sc_for_each_chunk(hnp_p.shape[0], move(pos_p_hbm, hnp_p_hbm))
        _sc_for_each_chunk(hnp_s.shape[0], move(pos_s_hbm, hnp_s_hbm))

    return k(pos_p, hnp_p, pos_s, hnp_s)


def _sc_gather(pos_parts, rows):
    half = rows.shape[1]
    n_parts = len(pos_parts)

    @functools.partial(
        pl.kernel, mesh=_sc_mesh(),
        out_type=tuple(jax.ShapeDtypeStruct((TOP_K, p.shape[1], half), jnp.uint32) for p in pos_parts),
        scratch_types=[pltpu.VMEM((TOP_K, SC_CHUNK), jnp.int32),
                       pltpu.VMEM((SC_CHUNK, half), jnp.uint32),
                       pltpu.SemaphoreType.DMA],
        compiler_params=pltpu.CompilerParams(use_tc_tiling_on_sc=True),
        name="sc_gather")
    def k(*refs):
        pos_hbms, rows_hbm = refs[:n_parts], refs[n_parts]
        out_hbms = refs[n_parts + 1:2 * n_parts + 1]
        idx_v, buf_v, sem = refs[2 * n_parts + 1:]

        def move(pos_hbm, out_hbm):
            def fn(base):
                pltpu.sync_copy(pos_hbm.at[:, pl.ds(base, SC_CHUNK)], idx_v)
                for j in range(TOP_K):
                    pltpu.async_copy(rows_hbm.at[idx_v.at[j]], buf_v, sem).wait()
                    pltpu.sync_copy(buf_v, out_hbm.at[j, pl.ds(base, SC_CHUNK)])
            return fn

        for pos_hbm, out_hbm, p in zip(pos_hbms, out_hbms, pos_parts):
            _sc_for_each_chunk(p.shape[1], move(pos_hbm, out_hbm))

    out = k(*pos_parts, rows)
    return out if isinstance(out, (tuple, list)) else (out,)


def _experts_kernel(blk_exp_ref, first_ref, slot_ref, next_ref, n_active_ref,
                    xs_ref, wgu_hbm, bgu_ref, wd_hbm, bd_ref, out_ref,
                    wgu_f32, wd_f32, wgu_bf, wd_bf, sems):
    i = pl.program_id(0)
    d_ff = wd_bf.shape[0]

    def fetch(e, sl):
        return (pltpu.make_async_copy(wgu_hbm.at[e], wgu_f32.at[sl], sems.at[0, sl]),
                pltpu.make_async_copy(wd_hbm.at[e], wd_f32.at[sl], sems.at[1, sl]))

    @pl.when(i < n_active_ref[0])
    def _():
        sl = slot_ref[i]

        @pl.when(first_ref[i] == 1)
        def _():
            @pl.when(i == 0)
            def _():
                for c in fetch(blk_exp_ref[i], sl):
                    c.start()

            for c in fetch(blk_exp_ref[i], sl):
                c.wait()
            for c in range(d_ff // FF_GROUP):
                src_g = slice(c * FF_GROUP, (c + 1) * FF_GROUP)
                src_u = slice(d_ff + c * FF_GROUP, d_ff + (c + 1) * FF_GROUP)
                wgu_bf[:, 2 * c * FF_GROUP:(2 * c + 1) * FF_GROUP] = wgu_f32[sl, :, src_g].astype(jnp.bfloat16)
                wgu_bf[:, (2 * c + 1) * FF_GROUP:(2 * c + 2) * FF_GROUP] = wgu_f32[sl, :, src_u].astype(jnp.bfloat16)
            wd_bf[...] = wd_f32[sl].astype(jnp.bfloat16)

            @pl.when(next_ref[i] >= 0)
            def _():
                for c in fetch(next_ref[i], 1 - sl):
                    c.start()

        x = _unpack_bf16_pairs(xs_ref[...]).astype(jnp.bfloat16)
        gu = jnp.dot(x, wgu_bf[...], preferred_element_type=jnp.float32)
        acts = []
        for c in range(d_ff // FF_GROUP):
            g = gu[:, 2 * c * FF_GROUP:(2 * c + 1) * FF_GROUP] + bgu_ref[0, :, c * FF_GROUP:(c + 1) * FF_GROUP]
            u = (gu[:, (2 * c + 1) * FF_GROUP:(2 * c + 2) * FF_GROUP]
                 + bgu_ref[0, :, d_ff + c * FF_GROUP:d_ff + (c + 1) * FF_GROUP])
            g = jnp.minimum(g, SWIGLU_LIMIT)
            u = jnp.clip(u, -SWIGLU_LIMIT, SWIGLU_LIMIT)
            acts.append((g * jax.nn.sigmoid(SWIGLU_ALPHA * g) * (u + 1.0)).astype(jnp.bfloat16))
        act = jnp.concatenate(acts, axis=-1)
        o = jnp.dot(act, wd_bf[...], preferred_element_type=jnp.float32) + bd_ref[0]
        out_ref[...] = _pack_bf16_pairs(o)

    @pl.when(i >= n_active_ref[0])
    def _():
        out_ref[...] = jnp.zeros(out_ref.shape, out_ref.dtype)


def _experts(plan, xs, wgu, bgu, wd, bd):
    blk_exp, first, slot, nxt, n_active = plan
    n_rows, half = xs.shape
    n_exp, d, two_f = wgu.shape
    d_ff = wd.shape[1]
    n_blocks = n_rows // MOE_BLOCK
    grid_spec = pltpu.PrefetchScalarGridSpec(
        num_scalar_prefetch=5,
        grid=(n_blocks,),
        in_specs=[
            pl.BlockSpec((MOE_BLOCK, half), lambda i, be, *_: (i, 0)),
            pl.BlockSpec(memory_space=pl.ANY),
            pl.BlockSpec((1, 1, two_f), lambda i, be, *_: (be[i], 0, 0)),
            pl.BlockSpec(memory_space=pl.ANY),
            pl.BlockSpec((1, 1, d), lambda i, be, *_: (be[i], 0, 0)),
        ],
        out_specs=pl.BlockSpec((MOE_BLOCK, half), lambda i, be, *_: (i, 0)),
        scratch_shapes=[
            pltpu.VMEM((2, d, two_f), jnp.float32),
            pltpu.VMEM((2, d_ff, d), jnp.float32),
            pltpu.VMEM((d, two_f), jnp.bfloat16),
            pltpu.VMEM((d_ff, d), jnp.bfloat16),
            pltpu.SemaphoreType.DMA((2, 2)),
        ],
    )
    return pl.pallas_call(
        _experts_kernel,
        grid_spec=grid_spec,
        out_shape=jax.ShapeDtypeStruct((n_rows, half), jnp.uint32),
        compiler_params=pltpu.CompilerParams(
            dimension_semantics=("arbitrary",), vmem_limit_bytes=VMEM_LIMIT),
        name="experts",
    )(blk_exp, first, slot, nxt, n_active, xs, wgu, bgu.reshape(n_exp, 1, two_f), wd,
      bd.reshape(n_exp, 1, d))


def _combine_kernel(rows_ref, gate_ref, h_ref, gfin_ref, *rest):
    y_ref = rest[-1]
    acc = h_ref[...]
    for k in range(TOP_K):
        acc = acc + gate_ref[:, k:k + 1] * _unpack_bf16_pairs(rows_ref[k])
    y_ref[...] = _rms(acc, gfin_ref[...])


def _combine(rows4, gate_t, h2d, gfin, y_prev=None, tok0=0):
    n_tok, d = h2d.shape
    n, half = rows4.shape[1:]
    tile = min(n, 256)
    t0 = tok0 // tile
    in_specs = [
        pl.BlockSpec((TOP_K, tile, half), lambda i: (0, i, 0)),
        pl.BlockSpec((tile, TOP_K), lambda i: (t0 + i, 0)),
        pl.BlockSpec((tile, d), lambda i: (t0 + i, 0)),
        pl.BlockSpec((1, d), lambda i: (0, 0)),
    ]
    args = [rows4, gate_t, h2d, gfin]
    aliases = {}
    if y_prev is not None:
        in_specs.append(pl.BlockSpec(memory_space=pl.ANY))
        args.append(y_prev)
        aliases = {4: 0}
    return pl.pallas_call(
        _combine_kernel,
        grid=(n // tile,),
        in_specs=in_specs,
        out_specs=pl.BlockSpec((tile, d), lambda i: (t0 + i, 0)),
        out_shape=jax.ShapeDtypeStruct((n_tok, d), jnp.float32),
        input_output_aliases=aliases,
        compiler_params=pltpu.CompilerParams(dimension_semantics=("arbitrary",)),
        name="combine",
    )(*args)


def _pad_hist(hist, rows):
    return jnp.pad(hist, ((0, 0), (rows - hist.shape[1], 0), (0, 0)))


def kernel(x_prompt, x_sample, state_conv_a, state_conv_b, norm_mix_g, w_in, conv_a_w, conv_b_w, conv_b_b, conv_ln_g, conv_ln_b, out_norm_a_g, out_norm_b_g, w_out, norm_ffn_g, w_router, b_router, w_gate_up, b_gate_up, w_down, b_down, final_norm_g):
    depth = w_in.shape[0]
    assert depth == 1, "single-layer trunk"
    bf16 = jnp.bfloat16
    bsz, seq, d = x_prompt.shape
    dec_b, dec_s, _ = x_sample.shape
    d_a, d_b = conv_a_w.shape[-1], conv_b_w.shape[-1]
    n_exp = w_router.shape[-1]

    params = (norm_mix_g[0][None], w_in[0].astype(bf16), conv_a_w[0], conv_b_w[0], conv_b_b[0][None],
              conv_ln_g[0][None], conv_ln_b[0][None], out_norm_a_g[0][None], out_norm_b_g[0][None],
              w_out[0].astype(bf16), norm_ffn_g[0][None], w_router[0].T.astype(bf16),
              b_router[0][:, None])

    zero_a = jnp.zeros((bsz, HIST_A_ROWS, d_a), jnp.float32)
    zero_b = jnp.zeros((bsz, HIST_B_ROWS, d_b), jnp.float32)
    cnt0 = jnp.zeros((n_exp, 1), jnp.float32)
    h_p, hnp_p, idx_p, gate_p, rank_p, cnt_p, na_p, nb_p = _mixer(
        x_prompt, zero_a, zero_b, cnt0, params, nb=1, ts=min(seq, 256))
    h_s, hnp_s, idx_s, gate_s, rank_s, cnt_s, na_s, nb_s = _mixer(
        x_sample, _pad_hist(state_conv_a[0], HIST_A_ROWS), _pad_hist(state_conv_b[0], HIST_B_ROWS),
        cnt_p, params, nb=dec_b, ts=dec_s)

    counts = cnt_s[:, 0].astype(jnp.int32)
    padded = (counts + MOE_BLOCK - 1) // MOE_BLOCK * MOE_BLOCK
    pad_end = jnp.cumsum(padded)
    pad_start = pad_end - padded
    n_pairs = (bsz * seq + dec_b * dec_s) * TOP_K
    n_blocks = -(-n_pairs // MOE_BLOCK) + n_exp
    blk_start = jnp.arange(n_blocks, dtype=jnp.int32) * MOE_BLOCK
    blk_exp = jnp.minimum(jnp.sum(blk_start[:, None] >= pad_end[None, :], axis=1),
                          n_exp - 1).astype(jnp.int32)
    n_active = (pad_end[-1:] // MOE_BLOCK).astype(jnp.int32)
    e_ids = jnp.arange(n_exp, dtype=jnp.int32)
    has = padded > 0
    slot_e = (jnp.cumsum(has.astype(jnp.int32)) - 1) % 2
    later = jnp.where(has, e_ids, n_exp)
    next_e = jnp.concatenate([lax.cummin(later, reverse=True)[1:], jnp.full((1,), n_exp, jnp.int32)])
    next_e = jnp.where(next_e >= n_exp, -1, next_e)
    blk_first = ((blk_start == pad_start[blk_exp]) & (blk_start < pad_end[-1])).astype(jnp.int32)
    plan = (blk_exp, blk_first, slot_e[blk_exp].astype(jnp.int32), next_e[blk_exp].astype(jnp.int32),
            n_active)

    def positions(idx, rank):
        pos = rank
        for e in range(n_exp):
            pos = pos + jnp.where(idx == e, pad_start[e], 0)
        return pos

    pos_p = positions(idx_p, rank_p)
    pos_s = positions(idx_s, rank_s)

    xs = _sc_dispatch(pos_p, hnp_p, pos_s, hnp_s, n_blocks * MOE_BLOCK)

    rows = _experts(plan, xs, w_gate_up[0], b_gate_up[0], w_down[0], b_down[0])

    gfin = final_norm_g[None]
    n_p = bsz * seq
    n_chunks = COMBINE_CHUNKS if n_p % (COMBINE_CHUNKS * SC_CHUNK * SC_WORKERS) == 0 else 1
    per = n_p // n_chunks
    gate_pt, h_p2 = gate_p.T, h_p.reshape(n_p, d)
    y_p = None
    for c in range(n_chunks):
        parts = [pos_p[:, c * per:(c + 1) * per]] + ([pos_s] if c == 0 else [])
        got = _sc_gather(parts, rows)
        if c == 0:
            y_s = _combine(got[1], gate_s.T, h_s.reshape(dec_b * dec_s, d), gfin).reshape(dec_b, dec_s, d)
        y_p = _combine(got[0], gate_pt, h_p2, gfin, y_prev=y_p, tok0=c * per)
    y_p = y_p.reshape(bsz, seq, d)

    return (y_p, y_s, na_p[None], nb_p[None], na_s[None], nb_s[None])
```

```python
import functools

import jax
import jax.numpy as jnp
from jax import lax
from jax.experimental import pallas as pl
from jax.experimental.pallas import tpu as pltpu
from jax.experimental.pallas import tpu_sc as plsc

SC_CORES = 2
SC_SUBCORES = 16
SC_WORKERS = SC_CORES * SC_SUBCORES
SC_CHUNK = 128
CONV_A_W = 3
CONV_B_W = 31
TOP_K = 4
SWIGLU_LIMIT = 7.0
SWIGLU_ALPHA = 1.702
NORM_EPS = 1e-5
MOE_BLOCK = 256
SUBLANES = 8
LANES = 128
CONV_ROW_CHUNK = 128
HIST_A_ROWS = 8
HIST_B_ROWS = 32
MIXER_ROWS = 512
PROJ_SECTION = 256
POS_TILE = 4096
COMBINE_CHUNKS = 4
VMEM_LIMIT = 56 * 1024 * 1024


def _rms(x, g):
    return x * lax.rsqrt(jnp.mean(x * x, axis=-1, keepdims=True) + NORM_EPS) * g


def _zero_tile_from(v):
    bits = pltpu.bitcast(v[0:SUBLANES, 0:LANES], jnp.uint32)
    return pltpu.bitcast((bits >> 16) >> 16, jnp.float32)


def _pack_bf16_pairs(a):
    w = a.shape[-1] // 2
    lo = pltpu.bitcast(a[:, :w].astype(jnp.bfloat16).astype(jnp.float32), jnp.uint32)
    hi = pltpu.bitcast(a[:, w:].astype(jnp.bfloat16).astype(jnp.float32), jnp.uint32)
    return (lo >> 16) | (hi & jnp.uint32(0xFFFF0000))


def _unpack_bf16_pairs(wd):
    lo = pltpu.bitcast(wd << 16, jnp.float32)
    hi = pltpu.bitcast(wd & jnp.uint32(0xFFFF0000), jnp.float32)
    return jnp.concatenate([lo, hi], axis=-1)


def _conv_b_chunk(ub_buf, cbw_ref, n_i, r0, rc, c0):
    cs = slice(c0, c0 + LANES)
    lead = HIST_B_ROWS - (CONV_B_W - 1)
    acc = None
    for b in range(SUBLANES):
        rows = rc if b == 0 else rc + SUBLANES
        q = None
        for a in range((HIST_B_ROWS + SUBLANES) // SUBLANES):
            k = SUBLANES * a + b - lead
            if k < 0 or k >= CONV_B_W:
                continue
            start = r0 + SUBLANES * a
            term = cbw_ref[k:k + 1, cs] * ub_buf[n_i, start:start + rows, cs]
            q = term if q is None else q + term
        part = q if b == 0 else q[b:b + rc]
        acc = part if acc is None else acc + part
    return acc


def _mixer_kernel(x_ref, hista_ref, histb_ref, cnt0_ref, tri_ref,
                  gmix_ref, win_ref, caw_ref, cbw_ref, cbb_ref, lng_ref, lnb_ref,
                  ga_ref, gb_ref, wout_ref, gffn_ref, wrt_ref, br_ref,
                  h_ref, hnp_ref, idx_ref, gate_ref, rank_ref, cnt_ref, newa_ref, newb_ref,
                  ua_buf, ub_buf, cb_buf, cnt_acc, *, nb, ts, d_a, d_b, n_exp):
    b = pl.program_id(0)
    s = pl.program_id(1)
    m = nb * ts
    d = x_ref.shape[-1]

    @pl.when(s == 0)
    def _():
        ua_buf[:, 0:HIST_A_ROWS, :] = hista_ref[...]
        ub_buf[:, 0:HIST_B_ROWS, :] = histb_ref[...]

    @pl.when(s != 0)
    def _():
        ua_buf[:, 0:HIST_A_ROWS, :] = ua_buf[:, ts:ts + HIST_A_ROWS, :]
        ub_buf[:, 0:HIST_B_ROWS, :] = ub_buf[:, ts:ts + HIST_B_ROWS, :]

    @pl.when((b == 0) & (s == 0))
    def _():
        cnt_acc[...] = cnt0_ref[...]

    x = x_ref[...].reshape(m, d)
    n = _rms(x, gmix_ref[...]).astype(jnp.bfloat16)
    def proj(c0, width):
        return jnp.dot(n, win_ref[:, c0:c0 + width], preferred_element_type=jnp.float32)

    sec = min(d_b, PROJ_SECTION)
    for c0 in range(0, d_b, sec):
        v_b = proj(3 * d_a + c0, sec)
        g_b = proj(3 * d_a + d_b + c0, sec)
        ub_buf[:, HIST_B_ROWS:HIST_B_ROWS + ts, c0:c0 + sec] = (v_b * jax.nn.sigmoid(g_b)).reshape(nb, ts, sec)

    gate_c = proj(d_a, d_a)
    xt = proj(2 * d_a, d_a)
    gate_b = proj(0, d_a)
    ua_buf[:, HIST_A_ROWS:HIST_A_ROWS + ts, :] = (gate_c * xt).reshape(nb, ts, d_a)
    conv_a = jnp.zeros((nb, ts, d_a), jnp.float32)
    for k in range(CONV_A_W):
        off = HIST_A_ROWS - (CONV_A_W - 1) + k
        conv_a = conv_a + caw_ref[k:k + 1, :] * ua_buf[:, off:off + ts, :]
    y_a = gate_b * conv_a.reshape(m, d_a)

    mix_a = jnp.dot(_rms(y_a, ga_ref[...]).astype(jnp.bfloat16), wout_ref[0:d_a, :],
                    preferred_element_type=jnp.float32)

    rc = min(ts, CONV_ROW_CHUNK)
    chunks = [(n_i, r0, c0) for n_i in range(nb) for r0 in range(0, ts, rc) for c0 in range(0, d_b, LANES)]
    anchors = {len(chunks) * 2 // 8: gate_c, len(chunks) * 3 // 8: xt, len(chunks) * 5 // 8: gate_b,
               len(chunks) - 1: mix_a}
    for ci, (n_i, r0, c0) in enumerate(chunks):
        acc = _conv_b_chunk(ub_buf, cbw_ref, n_i, r0, rc, c0)
        if ci in anchors:
            acc = (acc.reshape(rc // SUBLANES, SUBLANES, LANES) + _zero_tile_from(anchors[ci])[None]
                   ).reshape(rc, LANES)
        cb_buf[n_i, r0:r0 + rc, c0:c0 + LANES] = acc
    cb = cb_buf[...].reshape(m, d_b) + cbb_ref[...]
    mu = jnp.mean(cb, axis=-1, keepdims=True)
    xc = cb - mu
    var = jnp.mean(xc * xc, axis=-1, keepdims=True)
    ln = xc * lax.rsqrt(var + NORM_EPS) * lng_ref[...] + lnb_ref[...]
    y_b = ln * jax.nn.sigmoid(ln)

    mix_b = jnp.dot(_rms(y_b, gb_ref[...]).astype(jnp.bfloat16), wout_ref[d_a:d_a + d_b, :],
                    preferred_element_type=jnp.float32)
    h = x + (mix_a + mix_b)
    h_ref[...] = h.reshape(nb, ts, d)

    hn = _rms(h, gffn_ref[...])
    hnp_ref[...] = _pack_bf16_pairs(hn)

    logits = lax.dot_general(wrt_ref[...], hn.astype(jnp.bfloat16), (((1,), (1,)), ((), ())),
                             preferred_element_type=jnp.float32) + br_ref[...]
    e_iota = lax.broadcasted_iota(jnp.int32, (n_exp, m), 0)
    cur = logits
    vals, idxs, sels = [], [], []
    for _ in range(TOP_K):
        mx = jnp.max(cur, axis=0, keepdims=True)
        ix = jnp.min(jnp.where(cur == mx, e_iota, n_exp), axis=0, keepdims=True)
        sel = e_iota == ix
        vals.append(mx)
        idxs.append(ix)
        sels.append(sel)
        cur = jnp.where(sel, -jnp.inf, cur)
    exps = [jnp.exp(v - vals[0]) for v in vals]
    denom = exps[0] + exps[1] + exps[2] + exps[3]
    gates = [e / denom for e in exps]

    onehot = jnp.zeros((n_exp, m), jnp.float32)
    for sel in sels:
        onehot = onehot + sel.astype(jnp.float32)
    before = jnp.dot(onehot.astype(jnp.bfloat16), tri_ref[...],
                     preferred_element_type=jnp.float32) + cnt_acc[...]
    ranks = [jnp.sum(jnp.where(sel, before, 0.0), axis=0, keepdims=True) for sel in sels]
    new_cnt = cnt_acc[...] + jnp.sum(onehot, axis=1, keepdims=True)
    cnt_acc[...] = new_cnt
    cnt_ref[...] = new_cnt

    idx_ref[...] = jnp.concatenate(idxs, axis=0)
    gate_ref[...] = jnp.concatenate(gates, axis=0)
    rank_ref[...] = jnp.concatenate(ranks, axis=0).astype(jnp.int32)

    @pl.when(s == pl.num_programs(1) - 1)
    def _():
        newa_ref[...] = ua_buf[:, ts + HIST_A_ROWS - (CONV_A_W - 1):ts + HIST_A_ROWS, :]
        newb_ref[...] = ub_buf[:, ts + HIST_B_ROWS - (CONV_B_W - 1):ts + HIST_B_ROWS, :]


def _mixer(x, hist_a, hist_b, cnt0, params, *, nb, ts):
    (gmix, win, caw, cbw, cbb, lng, lnb, ga, gb, wout, gffn, wrt, br) = params
    bsz, seq, d = x.shape
    d_a, d_b = caw.shape[-1], cbw.shape[-1]
    n_exp = wrt.shape[0]
    m = nb * ts
    n_tok = bsz * seq
    grid = (bsz // nb, seq // ts)
    tri = (jnp.arange(m)[:, None] < jnp.arange(m)[None, :]).astype(jnp.bfloat16)

    def const(shape):
        return pl.BlockSpec(shape, lambda i, j: (0,) * len(shape))

    tok_map = lambda i, j: (0, i * (seq // ts) + j)
    kern = functools.partial(_mixer_kernel, nb=nb, ts=ts, d_a=d_a, d_b=d_b, n_exp=n_exp)
    out_shape = (
        jax.ShapeDtypeStruct((bsz, seq, d), jnp.float32),
        jax.ShapeDtypeStruct((n_tok, d // 2), jnp.uint32),
        jax.ShapeDtypeStruct((TOP_K, n_tok), jnp.int32),
        jax.ShapeDtypeStruct((TOP_K, n_tok), jnp.float32),
        jax.ShapeDtypeStruct((TOP_K, n_tok), jnp.int32),
        jax.ShapeDtypeStruct((n_exp, 1), jnp.float32),
        jax.ShapeDtypeStruct((bsz, CONV_A_W - 1, d_a), jnp.float32),
        jax.ShapeDtypeStruct((bsz, CONV_B_W - 1, d_b), jnp.float32),
    )
    return pl.pallas_call(
        kern,
        grid=grid,
        in_specs=[
            pl.BlockSpec((nb, ts, d), lambda i, j: (i, j, 0)),
            pl.BlockSpec((nb, HIST_A_ROWS, d_a), lambda i, j: (i, 0, 0)),
            pl.BlockSpec((nb, HIST_B_ROWS, d_b), lambda i, j: (i, 0, 0)),
            const((n_exp, 1)),
            const((m, m)),
            const((1, d)), const(win.shape), const(caw.shape), const(cbw.shape), const((1, d_b)),
            const((1, d_b)), const((1, d_b)), const((1, d_a)), const((1, d_b)), const(wout.shape),
            const((1, d)), const(wrt.shape), const((n_exp, 1)),
        ],
        out_specs=(
            pl.BlockSpec((nb, ts, d), lambda i, j: (i, j, 0)),
            pl.BlockSpec((m, d // 2), lambda i, j: (i * (seq // ts) + j, 0)),
            pl.BlockSpec((TOP_K, m), tok_map),
            pl.BlockSpec((TOP_K, m), tok_map),
            pl.BlockSpec((TOP_K, m), tok_map),
            pl.BlockSpec((n_exp, 1), lambda i, j: (0, 0)),
            pl.BlockSpec((nb, CONV_A_W - 1, d_a), lambda i, j: (i, 0, 0)),
            pl.BlockSpec((nb, CONV_B_W - 1, d_b), lambda i, j: (i, 0, 0)),
        ),
        out_shape=out_shape,
        scratch_shapes=[
            pltpu.VMEM((nb, HIST_A_ROWS + ts, d_a), jnp.float32),
            pltpu.VMEM((nb, HIST_B_ROWS + ts, d_b), jnp.float32),
            pltpu.VMEM((nb, ts, d_b), jnp.float32),
            pltpu.VMEM((n_exp, 1), jnp.float32),
        ],
        compiler_params=pltpu.CompilerParams(
            dimension_semantics=("arbitrary", "arbitrary"), vmem_limit_bytes=VMEM_LIMIT),
        name="mixer",
    )(x, hist_a, hist_b, cnt0, tri, gmix, win, caw, cbw, cbb, lng, lnb, ga, gb, wout, gffn, wrt, br)


def _positions_kernel(start_ref, idx_p_ref, rank_p_ref, idx_s_ref, rank_s_ref, pos_p_ref, pos_s_ref):
    for idx_ref, rank_ref, pos_ref in ((idx_p_ref, rank_p_ref, pos_p_ref), (idx_s_ref, rank_s_ref, pos_s_ref)):
        idx = idx_ref[...]
        pos = rank_ref[...]
        for e in range(start_ref.shape[0]):
            pos = pos + jnp.where(idx == e, start_ref[e], 0)
        pos_ref[...] = pos


def _positions(pad_start, idx_p, rank_p, idx_s, rank_s):
    n_p, n_s = idx_p.shape[1], idx_s.shape[1]
    tile = min(n_p, POS_TILE)
    big = pl.BlockSpec((TOP_K, tile), lambda i: (0, i))
    small = pl.BlockSpec((TOP_K, n_s), lambda i: (0, 0))
    return pl.pallas_call(
        _positions_kernel,
        grid=(n_p // tile,),
        in_specs=[pl.BlockSpec(memory_space=pltpu.SMEM), big, big, small, small],
        out_specs=(big, small),
        out_shape=(jax.ShapeDtypeStruct(idx_p.shape, jnp.int32), jax.ShapeDtypeStruct(idx_s.shape, jnp.int32)),
        compiler_params=pltpu.CompilerParams(dimension_semantics=("arbitrary",)),
        name="positions",
    )(pad_start, idx_p, rank_p, idx_s, rank_s)


def _sc_mesh():
    return plsc.VectorSubcoreMesh(core_axis_name="c", subcore_axis_name="s",
                                  num_cores=SC_CORES, num_subcores=SC_SUBCORES)


def _sc_worker_id():
    return lax.axis_index("s") * SC_CORES + lax.axis_index("c")


def _sc_chunks(n_tok, wid):
    n_chunks = n_tok // SC_CHUNK
    assert n_tok % SC_CHUNK == 0
    if n_chunks % SC_WORKERS == 0:
        per_w = n_chunks // SC_WORKERS
        return None, per_w, lambda i: (wid * per_w + i) * SC_CHUNK
    assert n_chunks <= SC_WORKERS
    return wid < n_chunks, 1, lambda i: wid * SC_CHUNK


def _sc_for_each_chunk(n_tok, fn):
    wid = _sc_worker_id()
    pred, trips, base = _sc_chunks(n_tok, wid)

    def run():
        @pl.loop(0, trips)
        def _(i):
            fn(pl.multiple_of(base(i), SC_CHUNK))

    if pred is None:
        run()
    else:
        pl.when(pred)(run)


def _sc_dispatch(pos_p, hnp_p, pos_s, hnp_s, n_rows):
    half = hnp_p.shape[1]

    @functools.partial(
        pl.kernel, mesh=_sc_mesh(),
        out_type=jax.ShapeDtypeStruct((n_rows, half), jnp.uint32),
        scratch_types=[pltpu.VMEM((TOP_K, SC_CHUNK), jnp.int32),
                       pltpu.VMEM((SC_CHUNK, half), jnp.uint32),
                       pltpu.SemaphoreType.DMA],
        compiler_params=pltpu.CompilerParams(use_tc_tiling_on_sc=True),
        name="sc_dispatch")
    def k(pos_p_hbm, hnp_p_hbm, pos_s_hbm, hnp_s_hbm, xs_hbm, idx_v, rows_v, sem):
        def move(pos_hbm, src_hbm):
            def fn(base):
                pltpu.sync_copy(pos_hbm.at[:, pl.ds(base, SC_CHUNK)], idx_v)
                pltpu.sync_copy(src_hbm.at[pl.ds(base, SC_CHUNK)], rows_v)
                copies = [pltpu.async_copy(rows_v, xs_hbm.at[idx_v.at[j]], sem) for j in range(TOP_K)]
                for c in copies:
                    c.wait()
            return fn

        _sc_for_each_chunk(hnp_p.shape[0], move(pos_p_hbm, hnp_p_hbm))
        _sc_for_each_chunk(hnp_s.shape[0], move(pos_s_hbm, hnp_s_hbm))

    return k(pos_p, hnp_p, pos_s, hnp_s)


def _sc_gather(pos_parts, rows):
    half = rows.shape[1]
    n_parts = len(pos_parts)

    @functools.partial(
        pl.kernel, mesh=_sc_mesh(),
        out_type=tuple(jax.ShapeDtypeStruct((TOP_K, p.shape[1], half), jnp.uint32) for p in pos_parts),
        scratch_types=[pltpu.VMEM((TOP_K, SC_CHUNK), jnp.int32),
                       pltpu.VMEM((SC_CHUNK, half), jnp.uint32),
                       pltpu.SemaphoreType.DMA],
        compiler_params=pltpu.CompilerParams(use_tc_tiling_on_sc=True),
        name="sc_gather")
    def k(*refs):
        pos_hbms, rows_hbm = refs[:n_parts], refs[n_parts]
        out_hbms = refs[n_parts + 1:2 * n_parts + 1]
        idx_v, buf_v, sem = refs[2 * n_parts + 1:]

        def move(pos_hbm, out_hbm):
            def fn(base):
                pltpu.sync_copy(pos_hbm.at[:, pl.ds(base, SC_CHUNK)], idx_v)
                for j in range(TOP_K):
                    pltpu.async_copy(rows_hbm.at[idx_v.at[j]], buf_v, sem).wait()
                    pltpu.sync_copy(buf_v, out_hbm.at[j, pl.ds(base, SC_CHUNK)])
            return fn

        for pos_hbm, out_hbm, p in zip(pos_hbms, out_hbms, pos_parts):
            _sc_for_each_chunk(p.shape[1], move(pos_hbm, out_hbm))

    out = k(*pos_parts, rows)
    return out if isinstance(out, (tuple, list)) else (out,)


def _experts_kernel(blk_exp_ref, first_ref, slot_ref, next_ref, n_active_ref,
                    xs_ref, wgu_hbm, bgu_ref, wd_hbm, bd_ref, out_ref,
                    wgu_f32, wd_f32, wgu_bf, wd_bf, sems):
    i = pl.program_id(0)
    d_ff = wd_bf.shape[0]

    def fetch(e, sl):
        return (pltpu.make_async_copy(wgu_hbm.at[e], wgu_f32.at[sl], sems.at[0, sl]),
                pltpu.make_async_copy(wd_hbm.at[e], wd_f32.at[sl], sems.at[1, sl]))

    @pl.when(i < n_active_ref[0])
    def _():
        sl = slot_ref[i]

        @pl.when(first_ref[i] == 1)
        def _():
            @pl.when(i == 0)
            def _():
                for c in fetch(blk_exp_ref[i], sl):
                    c.start()

            for c in fetch(blk_exp_ref[i], sl):
                c.wait()
            wgu_bf[...] = wgu_f32[sl].astype(jnp.bfloat16)
            wd_bf[...] = wd_f32[sl].astype(jnp.bfloat16)

            @pl.when(next_ref[i] >= 0)
            def _():
                for c in fetch(next_ref[i], 1 - sl):
                    c.start()

        x = _unpack_bf16_pairs(xs_ref[...]).astype(jnp.bfloat16)
        gu = jnp.dot(x, wgu_bf[...], preferred_element_type=jnp.float32) + bgu_ref[0]
        g = jnp.minimum(gu[:, :d_ff], SWIGLU_LIMIT)
        u = jnp.clip(gu[:, d_ff:], -SWIGLU_LIMIT, SWIGLU_LIMIT)
        act = g * jax.nn.sigmoid(SWIGLU_ALPHA * g) * (u + 1.0)
        o = jnp.dot(act.astype(jnp.bfloat16), wd_bf[...], preferred_element_type=jnp.float32) + bd_ref[0]
        out_ref[...] = _pack_bf16_pairs(o)

    @pl.when(i >= n_active_ref[0])
    def _():
        out_ref[...] = jnp.zeros(out_ref.shape, out_ref.dtype)


def _experts(plan, xs, wgu, bgu, wd, bd):
    blk_exp, first, slot, nxt, n_active = plan
    n_rows, half = xs.shape
    n_exp, d, two_f = wgu.shape
    d_ff = wd.shape[1]
    n_blocks = n_rows // MOE_BLOCK
    grid_spec = pltpu.PrefetchScalarGridSpec(
        num_scalar_prefetch=5,
        grid=(n_blocks,),
        in_specs=[
            pl.BlockSpec((MOE_BLOCK, half), lambda i, be, *_: (i, 0)),
            pl.BlockSpec(memory_space=pl.ANY),
            pl.BlockSpec((1, 1, two_f), lambda i, be, *_: (be[i], 0, 0)),
            pl.BlockSpec(memory_space=pl.ANY),
            pl.BlockSpec((1, 1, d), lambda i, be, *_: (be[i], 0, 0)),
        ],
        out_specs=pl.BlockSpec((MOE_BLOCK, half), lambda i, be, *_: (i, 0)),
        scratch_shapes=[
            pltpu.VMEM((2, d, two_f), jnp.float32),
            pltpu.VMEM((2, d_ff, d), jnp.float32),
            pltpu.VMEM((d, two_f), jnp.bfloat16),
            pltpu.VMEM((d_ff, d), jnp.bfloat16),
            pltpu.SemaphoreType.DMA((2, 2)),
        ],
    )
    return pl.pallas_call(
        _experts_kernel,
        grid_spec=grid_spec,
        out_shape=jax.ShapeDtypeStruct((n_rows, half), jnp.uint32),
        compiler_params=pltpu.CompilerParams(
            dimension_semantics=("arbitrary",), vmem_limit_bytes=VMEM_LIMIT),
        name="experts",
    )(blk_exp, first, slot, nxt, n_active, xs, wgu, bgu.reshape(n_exp, 1, two_f), wd,
      bd.reshape(n_exp, 1, d))


def _combine_kernel(rows_ref, gate_ref, h_ref, gfin_ref, *rest):
    y_ref = rest[-1]
    acc = h_ref[...]
    for k in range(TOP_K):
        acc = acc + gate_ref[:, k:k + 1] * _unpack_bf16_pairs(rows_ref[k])
    y_ref[...] = _rms(acc, gfin_ref[...])


def _combine(rows4, gate_t, h2d, gfin, y_prev=None, tok0=0):
    n_tok, d = h2d.shape
    n, half = rows4.shape[1:]
    tile = min(n, 256)
    t0 = tok0 // tile
    in_specs = [
        pl.BlockSpec((TOP_K, tile, half), lambda i: (0, i, 0)),
        pl.BlockSpec((tile, TOP_K), lambda i: (t0 + i, 0)),
        pl.BlockSpec((tile, d), lambda i: (t0 + i, 0)),
        pl.BlockSpec((1, d), lambda i: (0, 0)),
    ]
    args = [rows4, gate_t, h2d, gfin]
    aliases = {}
    if y_prev is not None:
        in_specs.append(pl.BlockSpec(memory_space=pl.ANY))
        args.append(y_prev)
        aliases = {4: 0}
    return pl.pallas_call(
        _combine_kernel,
        grid=(n // tile,),
        in_specs=in_specs,
        out_specs=pl.BlockSpec((tile, d), lambda i: (t0 + i, 0)),
        out_shape=jax.ShapeDtypeStruct((n_tok, d), jnp.float32),
        input_output_aliases=aliases,
        compiler_params=pltpu.CompilerParams(dimension_semantics=("arbitrary",)),
        name="combine",
    )(*args)


def _pad_hist(hist, rows):
    return jnp.pad(hist, ((0, 0), (rows - hist.shape[1], 0), (0, 0)))


def kernel(x_prompt, x_sample, state_conv_a, state_conv_b, norm_mix_g, w_in, conv_a_w, conv_b_w, conv_b_b, conv_ln_g, conv_ln_b, out_norm_a_g, out_norm_b_g, w_out, norm_ffn_g, w_router, b_router, w_gate_up, b_gate_up, w_down, b_down, final_norm_g):
    depth = w_in.shape[0]
    assert depth == 1, "single-layer trunk"
    bf16 = jnp.bfloat16
    bsz, seq, d = x_prompt.shape
    dec_b, dec_s, _ = x_sample.shape
    d_a, d_b = conv_a_w.shape[-1], conv_b_w.shape[-1]
    n_exp = w_router.shape[-1]

    params = (norm_mix_g[0][None], w_in[0].astype(bf16), conv_a_w[0], conv_b_w[0], conv_b_b[0][None],
              conv_ln_g[0][None], conv_ln_b[0][None], out_norm_a_g[0][None], out_norm_b_g[0][None],
              w_out[0].astype(bf16), norm_ffn_g[0][None], w_router[0].T.astype(bf16),
              b_router[0][:, None])

    zero_a = jnp.zeros((bsz, HIST_A_ROWS, d_a), jnp.float32)
    zero_b = jnp.zeros((bsz, HIST_B_ROWS, d_b), jnp.float32)
    cnt0 = jnp.zeros((n_exp, 1), jnp.float32)
    h_p, hnp_p, idx_p, gate_p, rank_p, cnt_p, na_p, nb_p = _mixer(
        x_prompt, zero_a, zero_b, cnt0, params, nb=1, ts=min(seq, MIXER_ROWS))
    h_s, hnp_s, idx_s, gate_s, rank_s, cnt_s, na_s, nb_s = _mixer(
        x_sample, _pad_hist(state_conv_a[0], HIST_A_ROWS), _pad_hist(state_conv_b[0], HIST_B_ROWS),
        cnt_p, params, nb=dec_b, ts=dec_s)

    counts = cnt_s[:, 0].astype(jnp.int32)
    padded = (counts + MOE_BLOCK - 1) // MOE_BLOCK * MOE_BLOCK
    pad_end = jnp.cumsum(padded)
    pad_start = pad_end - padded
    n_pairs = (bsz * seq + dec_b * dec_s) * TOP_K
    n_blocks = -(-n_pairs // MOE_BLOCK) + n_exp
    blk_start = jnp.arange(n_blocks, dtype=jnp.int32) * MOE_BLOCK
    blk_exp = jnp.minimum(jnp.sum(blk_start[:, None] >= pad_end[None, :], axis=1),
                          n_exp - 1).astype(jnp.int32)
    n_active = (pad_end[-1:] // MOE_BLOCK).astype(jnp.int32)
    e_ids = jnp.arange(n_exp, dtype=jnp.int32)
    has = padded > 0
    slot_e = (jnp.cumsum(has.astype(jnp.int32)) - 1) % 2
    later = jnp.where(has, e_ids, n_exp)
    next_e = jnp.concatenate([lax.cummin(later, reverse=True)[1:], jnp.full((1,), n_exp, jnp.int32)])
    next_e = jnp.where(next_e >= n_exp, -1, next_e)
    blk_onehot = blk_exp[:, None] == e_ids[None, :]

    def per_block(table):
        return jnp.sum(jnp.where(blk_onehot, table[None, :], 0), axis=1).astype(jnp.int32)

    blk_first = ((blk_start == per_block(pad_start)) & (blk_start < pad_end[-1])).astype(jnp.int32)
    plan = (blk_exp, blk_first, per_block(slot_e), per_block(next_e), n_active)

    pos_p, pos_s = _positions(pad_start.astype(jnp.int32), idx_p, rank_p, idx_s, rank_s)

    xs = _sc_dispatch(pos_p, hnp_p, pos_s, hnp_s, n_blocks * MOE_BLOCK)

    rows = _experts(plan, xs, w_gate_up[0], b_gate_up[0], w_down[0], b_down[0])

    gfin = final_norm_g[None]
    n_p = bsz * seq
    n_chunks = COMBINE_CHUNKS if n_p % (COMBINE_CHUNKS * SC_CHUNK * SC_WORKERS) == 0 else 1
    per = n_p // n_chunks
    gate_pt, h_p2 = gate_p.T, h_p.reshape(n_p, d)
    y_p = None
    for c in range(n_chunks):
        parts = [pos_p[:, c * per:(c + 1) * per]] + ([pos_s] if c == 0 else [])
        got = _sc_gather(parts, rows)
        if c == 0:
            y_s = _combine(got[1], gate_s.T, h_s.reshape(dec_b * dec_s, d), gfin).reshape(dec_b, dec_s, d)
        y_p = _combine(got[0], gate_pt, h_p2, gfin, y_prev=y_p, tok0=c * per)
    y_p = y_p.reshape(bsz, seq, d)

    return (y_p, y_s, na_p[None], nb_p[None], na_s[None], nb_s[None])
```

```python
import functools

import jax
import jax.numpy as jnp
from jax import lax
from jax.experimental import pallas as pl
from jax.experimental.pallas import tpu as pltpu
from jax.experimental.pallas import tpu_sc as plsc

SC_CORES = 2
SC_SUBCORES = 16
SC_WORKERS = SC_CORES * SC_SUBCORES
SC_CHUNK = 128
CONV_A_W = 3
CONV_B_W = 31
TOP_K = 4
SWIGLU_LIMIT = 7.0
SWIGLU_ALPHA = 1.702
NORM_EPS = 1e-5
MOE_BLOCK = 256
SUBLANES = 8
LANES = 128
CONV_ROW_CHUNK = 128
HIST_A_ROWS = 8
HIST_B_ROWS = 32
MIXER_ROWS = 512
PROJ_SECTION = 256
POS_TILE = 4096
MOE_GROUPS = 2
COMBINE_CHUNKS = 2
VMEM_LIMIT = 56 * 1024 * 1024


def _rms(x, g):
    return x * lax.rsqrt(jnp.mean(x * x, axis=-1, keepdims=True) + NORM_EPS) * g


def _zero_tile_from(v):
    bits = pltpu.bitcast(v[0:SUBLANES, 0:LANES], jnp.uint32)
    return pltpu.bitcast((bits >> 16) >> 16, jnp.float32)


def _pack_bf16_pairs(a):
    w = a.shape[-1] // 2
    lo = pltpu.bitcast(a[:, :w].astype(jnp.bfloat16).astype(jnp.float32), jnp.uint32)
    hi = pltpu.bitcast(a[:, w:].astype(jnp.bfloat16).astype(jnp.float32), jnp.uint32)
    return (lo >> 16) | (hi & jnp.uint32(0xFFFF0000))


def _unpack_bf16_pairs(wd):
    lo = pltpu.bitcast(wd << 16, jnp.float32)
    hi = pltpu.bitcast(wd & jnp.uint32(0xFFFF0000), jnp.float32)
    return jnp.concatenate([lo, hi], axis=-1)


def _conv_b_chunk(ub_buf, cbw_ref, n_i, r0, rc, c0):
    cs = slice(c0, c0 + LANES)
    lead = HIST_B_ROWS - (CONV_B_W - 1)
    acc = None
    for b in range(SUBLANES):
        rows = rc if b == 0 else rc + SUBLANES
        q = None
        for a in range((HIST_B_ROWS + SUBLANES) // SUBLANES):
            k = SUBLANES * a + b - lead
            if k < 0 or k >= CONV_B_W:
                continue
            start = r0 + SUBLANES * a
            term = cbw_ref[k:k + 1, cs] * ub_buf[n_i, start:start + rows, cs]
            q = term if q is None else q + term
        part = q if b == 0 else q[b:b + rc]
        acc = part if acc is None else acc + part
    return acc


def _mixer_kernel(x_ref, hista_ref, histb_ref, cnt0_ref, tri_ref,
                  gmix_ref, win_ref, caw_ref, cbw_ref, cbb_ref, lng_ref, lnb_ref,
                  ga_ref, gb_ref, wout_ref, gffn_ref, wrt_ref, br_ref,
                  h_ref, hnp_ref, idx_ref, gate_ref, rank_ref, cnt_ref, newa_ref, newb_ref,
                  ua_buf, ub_buf, cb_buf, cnt_acc, *, nb, ts, d_a, d_b, n_exp):
    b = pl.program_id(0)
    s = pl.program_id(1)
    m = nb * ts
    d = x_ref.shape[-1]

    @pl.when(s == 0)
    def _():
        ua_buf[:, 0:HIST_A_ROWS, :] = hista_ref[...]
        ub_buf[:, 0:HIST_B_ROWS, :] = histb_ref[...]

    @pl.when(s != 0)
    def _():
        ua_buf[:, 0:HIST_A_ROWS, :] = ua_buf[:, ts:ts + HIST_A_ROWS, :]
        ub_buf[:, 0:HIST_B_ROWS, :] = ub_buf[:, ts:ts + HIST_B_ROWS, :]

    @pl.when((b == 0) & (s == 0))
    def _():
        cnt_acc[...] = cnt0_ref[...]

    x = x_ref[...].reshape(m, d)
    n = _rms(x, gmix_ref[...]).astype(jnp.bfloat16)
    def proj(c0, width):
        return jnp.dot(n, win_ref[:, c0:c0 + width], preferred_element_type=jnp.float32)

    sec = min(d_b, PROJ_SECTION)
    for c0 in range(0, d_b, sec):
        v_b = proj(3 * d_a + c0, sec)
        g_b = proj(3 * d_a + d_b + c0, sec)
        ub_buf[:, HIST_B_ROWS:HIST_B_ROWS + ts, c0:c0 + sec] = (v_b * jax.nn.sigmoid(g_b)).reshape(nb, ts, sec)

    gate_c = proj(d_a, d_a)
    xt = proj(2 * d_a, d_a)
    gate_b = proj(0, d_a)
    ua_buf[:, HIST_A_ROWS:HIST_A_ROWS + ts, :] = (gate_c * xt).reshape(nb, ts, d_a)
    conv_a = jnp.zeros((nb, ts, d_a), jnp.float32)
    for k in range(CONV_A_W):
        off = HIST_A_ROWS - (CONV_A_W - 1) + k
        conv_a = conv_a + caw_ref[k:k + 1, :] * ua_buf[:, off:off + ts, :]
    y_a = gate_b * conv_a.reshape(m, d_a)

    mix_a = jnp.dot(_rms(y_a, ga_ref[...]).astype(jnp.bfloat16), wout_ref[0:d_a, :],
                    preferred_element_type=jnp.float32)

    rc = min(ts, CONV_ROW_CHUNK)
    chunks = [(n_i, r0, c0) for n_i in range(nb) for r0 in range(0, ts, rc) for c0 in range(0, d_b, LANES)]
    anchors = {len(chunks) * 2 // 8: gate_c, len(chunks) * 3 // 8: xt, len(chunks) * 5 // 8: gate_b,
               len(chunks) - 1: mix_a}
    for ci, (n_i, r0, c0) in enumerate(chunks):
        acc = _conv_b_chunk(ub_buf, cbw_ref, n_i, r0, rc, c0)
        if ci in anchors:
            acc = (acc.reshape(rc // SUBLANES, SUBLANES, LANES) + _zero_tile_from(anchors[ci])[None]
                   ).reshape(rc, LANES)
        cb_buf[n_i, r0:r0 + rc, c0:c0 + LANES] = acc
    cb = cb_buf[...].reshape(m, d_b) + cbb_ref[...]
    mu = jnp.mean(cb, axis=-1, keepdims=True)
    xc = cb - mu
    var = jnp.mean(xc * xc, axis=-1, keepdims=True)
    ln = xc * lax.rsqrt(var + NORM_EPS) * lng_ref[...] + lnb_ref[...]
    y_b = ln * jax.nn.sigmoid(ln)

    mix_b = jnp.dot(_rms(y_b, gb_ref[...]).astype(jnp.bfloat16), wout_ref[d_a:d_a + d_b, :],
                    preferred_element_type=jnp.float32)
    h = x + (mix_a + mix_b)
    h_ref[...] = h.reshape(nb, ts, d)

    hn = _rms(h, gffn_ref[...])
    hnp_ref[...] = _pack_bf16_pairs(hn)

    logits = lax.dot_general(wrt_ref[...], hn.astype(jnp.bfloat16), (((1,), (1,)), ((), ())),
                             preferred_element_type=jnp.float32) + br_ref[...]
    e_iota = lax.broadcasted_iota(jnp.int32, (n_exp, m), 0)
    cur = logits
    vals, idxs, sels = [], [], []
    for _ in range(TOP_K):
        mx = jnp.max(cur, axis=0, keepdims=True)
        ix = jnp.min(jnp.where(cur == mx, e_iota, n_exp), axis=0, keepdims=True)
        sel = e_iota == ix
        vals.append(mx)
        idxs.append(ix)
        sels.append(sel)
        cur = jnp.where(sel, -jnp.inf, cur)
    exps = [jnp.exp(v - vals[0]) for v in vals]
    denom = exps[0] + exps[1] + exps[2] + exps[3]
    gates = [e / denom for e in exps]

    onehot = jnp.zeros((n_exp, m), jnp.float32)
    for sel in sels:
        onehot = onehot + sel.astype(jnp.float32)
    before = jnp.dot(onehot.astype(jnp.bfloat16), tri_ref[...],
                     preferred_element_type=jnp.float32) + cnt_acc[...]
    ranks = [jnp.sum(jnp.where(sel, before, 0.0), axis=0, keepdims=True) for sel in sels]
    new_cnt = cnt_acc[...] + jnp.sum(onehot, axis=1, keepdims=True)
    cnt_acc[...] = new_cnt
    cnt_ref[...] = new_cnt

    idx_ref[...] = jnp.concatenate(idxs, axis=0)
    gate_ref[...] = jnp.concatenate(gates, axis=0)
    rank_ref[...] = jnp.concatenate(ranks, axis=0).astype(jnp.int32)

    @pl.when(s == pl.num_programs(1) - 1)
    def _():
        newa_ref[...] = ua_buf[:, ts + HIST_A_ROWS - (CONV_A_W - 1):ts + HIST_A_ROWS, :]
        newb_ref[...] = ub_buf[:, ts + HIST_B_ROWS - (CONV_B_W - 1):ts + HIST_B_ROWS, :]


def _mixer(x, hist_a, hist_b, cnt0, params, *, nb, ts, b0=0, bsz=None):
    (gmix, win, caw, cbw, cbb, lng, lnb, ga, gb, wout, gffn, wrt, br) = params
    _, seq, d = x.shape
    bsz = x.shape[0] if bsz is None else bsz
    d_a, d_b = caw.shape[-1], cbw.shape[-1]
    n_exp = wrt.shape[0]
    m = nb * ts
    n_tok = bsz * seq
    g0 = b0 // nb
    grid = (bsz // nb, seq // ts)
    tri = (jnp.arange(m)[:, None] < jnp.arange(m)[None, :]).astype(jnp.bfloat16)

    def const(shape):
        return pl.BlockSpec(shape, lambda i, j: (0,) * len(shape))

    tok_map = lambda i, j: (0, i * (seq // ts) + j)
    kern = functools.partial(_mixer_kernel, nb=nb, ts=ts, d_a=d_a, d_b=d_b, n_exp=n_exp)
    out_shape = (
        jax.ShapeDtypeStruct((bsz, seq, d), jnp.float32),
        jax.ShapeDtypeStruct((n_tok, d // 2), jnp.uint32),
        jax.ShapeDtypeStruct((TOP_K, n_tok), jnp.int32),
        jax.ShapeDtypeStruct((TOP_K, n_tok), jnp.float32),
        jax.ShapeDtypeStruct((TOP_K, n_tok), jnp.int32),
        jax.ShapeDtypeStruct((n_exp, 1), jnp.float32),
        jax.ShapeDtypeStruct((bsz, CONV_A_W - 1, d_a), jnp.float32),
        jax.ShapeDtypeStruct((bsz, CONV_B_W - 1, d_b), jnp.float32),
    )
    return pl.pallas_call(
        kern,
        grid=grid,
        in_specs=[
            pl.BlockSpec((nb, ts, d), lambda i, j: (g0 + i, j, 0)),
            pl.BlockSpec((nb, HIST_A_ROWS, d_a), lambda i, j: (g0 + i, 0, 0)),
            pl.BlockSpec((nb, HIST_B_ROWS, d_b), lambda i, j: (g0 + i, 0, 0)),
            const((n_exp, 1)),
            const((m, m)),
            const((1, d)), const(win.shape), const(caw.shape), const(cbw.shape), const((1, d_b)),
            const((1, d_b)), const((1, d_b)), const((1, d_a)), const((1, d_b)), const(wout.shape),
            const((1, d)), const(wrt.shape), const((n_exp, 1)),
        ],
        out_specs=(
            pl.BlockSpec((nb, ts, d), lambda i, j: (i, j, 0)),
            pl.BlockSpec((m, d // 2), lambda i, j: (i * (seq // ts) + j, 0)),
            pl.BlockSpec((TOP_K, m), tok_map),
            pl.BlockSpec((TOP_K, m), tok_map),
            pl.BlockSpec((TOP_K, m), tok_map),
            pl.BlockSpec((n_exp, 1), lambda i, j: (0, 0)),
            pl.BlockSpec((nb, CONV_A_W - 1, d_a), lambda i, j: (i, 0, 0)),
            pl.BlockSpec((nb, CONV_B_W - 1, d_b), lambda i, j: (i, 0, 0)),
        ),
        out_shape=out_shape,
        scratch_shapes=[
            pltpu.VMEM((nb, HIST_A_ROWS + ts, d_a), jnp.float32),
            pltpu.VMEM((nb, HIST_B_ROWS + ts, d_b), jnp.float32),
            pltpu.VMEM((nb, ts, d_b), jnp.float32),
            pltpu.VMEM((n_exp, 1), jnp.float32),
        ],
        compiler_params=pltpu.CompilerParams(
            dimension_semantics=("arbitrary", "arbitrary"), vmem_limit_bytes=VMEM_LIMIT),
        name="mixer",
    )(x, hist_a, hist_b, cnt0, tri, gmix, win, caw, cbw, cbb, lng, lnb, ga, gb, wout, gffn, wrt, br)


def _positions_kernel(start_ref, *refs):
    n_parts = len(refs) // 3
    for p in range(n_parts):
        idx = refs[2 * p][...]
        pos = refs[2 * p + 1][...]
        for e in range(start_ref.shape[0]):
            pos = pos + jnp.where(idx == e, start_ref[e], 0)
        refs[2 * n_parts + p][...] = pos


def _positions(pad_start, parts):
    n_0 = parts[0][0].shape[1]
    tile = min(n_0, POS_TILE)
    specs = [pl.BlockSpec((TOP_K, tile), lambda i: (0, i))]
    specs += [pl.BlockSpec((TOP_K, idx.shape[1]), lambda i: (0, 0)) for idx, _ in parts[1:]]
    in_specs = [pl.BlockSpec(memory_space=pltpu.SMEM)]
    args = [pad_start]
    for spec, (idx, rank) in zip(specs, parts):
        in_specs += [spec, spec]
        args += [idx, rank]
    out = pl.pallas_call(
        _positions_kernel,
        grid=(n_0 // tile,),
        in_specs=in_specs,
        out_specs=tuple(specs),
        out_shape=tuple(jax.ShapeDtypeStruct(idx.shape, jnp.int32) for idx, _ in parts),
        compiler_params=pltpu.CompilerParams(dimension_semantics=("arbitrary",)),
        name="positions",
    )(*args)
    return list(out)


def _sc_mesh():
    return plsc.VectorSubcoreMesh(core_axis_name="c", subcore_axis_name="s",
                                  num_cores=SC_CORES, num_subcores=SC_SUBCORES)


def _sc_worker_id():
    return lax.axis_index("s") * SC_CORES + lax.axis_index("c")


def _sc_chunks(n_tok, wid):
    n_chunks = n_tok // SC_CHUNK
    assert n_tok % SC_CHUNK == 0
    if n_chunks % SC_WORKERS == 0:
        per_w = n_chunks // SC_WORKERS
        return None, per_w, lambda i: (wid * per_w + i) * SC_CHUNK
    assert n_chunks <= SC_WORKERS
    return wid < n_chunks, 1, lambda i: wid * SC_CHUNK


def _sc_for_each_chunk(n_tok, fn):
    wid = _sc_worker_id()
    pred, trips, base = _sc_chunks(n_tok, wid)

    def run():
        @pl.loop(0, trips)
        def _(i):
            fn(pl.multiple_of(base(i), SC_CHUNK))

    if pred is None:
        run()
    else:
        pl.when(pred)(run)


def _sc_dispatch(parts, n_rows):
    half = parts[0][1].shape[1]
    n_parts = len(parts)

    @functools.partial(
        pl.kernel, mesh=_sc_mesh(),
        out_type=jax.ShapeDtypeStruct((n_rows, half), jnp.uint32),
        scratch_types=[pltpu.VMEM((TOP_K, SC_CHUNK), jnp.int32),
                       pltpu.VMEM((SC_CHUNK, half), jnp.uint32),
                       pltpu.SemaphoreType.DMA],
        compiler_params=pltpu.CompilerParams(use_tc_tiling_on_sc=True),
        name="sc_dispatch")
    def k(*refs):
        xs_hbm, idx_v, rows_v, sem = refs[2 * n_parts:]

        def move(pos_hbm, src_hbm):
            def fn(base):
                pltpu.sync_copy(pos_hbm.at[:, pl.ds(base, SC_CHUNK)], idx_v)
                pltpu.sync_copy(src_hbm.at[pl.ds(base, SC_CHUNK)], rows_v)
                copies = [pltpu.async_copy(rows_v, xs_hbm.at[idx_v.at[j]], sem) for j in range(TOP_K)]
                for c in copies:
                    c.wait()
            return fn

        for p, (_, hnp) in enumerate(parts):
            _sc_for_each_chunk(hnp.shape[0], move(refs[2 * p], refs[2 * p + 1]))

    return k(*[a for part in parts for a in part])


def _sc_gather(pos_parts, rows):
    half = rows.shape[1]
    n_parts = len(pos_parts)

    @functools.partial(
        pl.kernel, mesh=_sc_mesh(),
        out_type=tuple(jax.ShapeDtypeStruct((TOP_K, p.shape[1], half), jnp.uint32) for p in pos_parts),
        scratch_types=[pltpu.VMEM((TOP_K, SC_CHUNK), jnp.int32),
                       pltpu.VMEM((SC_CHUNK, half), jnp.uint32),
                       pltpu.SemaphoreType.DMA],
        compiler_params=pltpu.CompilerParams(use_tc_tiling_on_sc=True),
        name="sc_gather")
    def k(*refs):
        pos_hbms, rows_hbm = refs[:n_parts], refs[n_parts]
        out_hbms = refs[n_parts + 1:2 * n_parts + 1]
        idx_v, buf_v, sem = refs[2 * n_parts + 1:]

        def move(pos_hbm, out_hbm):
            def fn(base):
                pltpu.sync_copy(pos_hbm.at[:, pl.ds(base, SC_CHUNK)], idx_v)
                for j in range(TOP_K):
                    pltpu.async_copy(rows_hbm.at[idx_v.at[j]], buf_v, sem).wait()
                    pltpu.sync_copy(buf_v, out_hbm.at[j, pl.ds(base, SC_CHUNK)])
            return fn

        for pos_hbm, out_hbm, p in zip(pos_hbms, out_hbms, pos_parts):
            _sc_for_each_chunk(p.shape[1], move(pos_hbm, out_hbm))

    out = k(*pos_parts, rows)
    return out if isinstance(out, (tuple, list)) else (out,)


def _experts_kernel(blk_exp_ref, first_ref, slot_ref, next_ref, n_active_ref,
                    xs_ref, wgu_hbm, bgu_ref, wd_hbm, bd_ref, out_ref,
                    wgu_f32, wd_f32, wgu_bf, wd_bf, sems):
    i = pl.program_id(0)
    d_ff = wd_bf.shape[0]

    def fetch(e, sl):
        return (pltpu.make_async_copy(wgu_hbm.at[e], wgu_f32.at[sl], sems.at[0, sl]),
                pltpu.make_async_copy(wd_hbm.at[e], wd_f32.at[sl], sems.at[1, sl]))

    @pl.when(i < n_active_ref[0])
    def _():
        sl = slot_ref[i]

        @pl.when(first_ref[i] == 1)
        def _():
            @pl.when(i == 0)
            def _():
                for c in fetch(blk_exp_ref[i], sl):
                    c.start()

            for c in fetch(blk_exp_ref[i], sl):
                c.wait()
            wgu_bf[...] = wgu_f32[sl].astype(jnp.bfloat16)
            wd_bf[...] = wd_f32[sl].astype(jnp.bfloat16)

            @pl.when(next_ref[i] >= 0)
            def _():
                for c in fetch(next_ref[i], 1 - sl):
                    c.start()

        x = _unpack_bf16_pairs(xs_ref[...]).astype(jnp.bfloat16)
        gu = jnp.dot(x, wgu_bf[...], preferred_element_type=jnp.float32) + bgu_ref[0]
        g = jnp.minimum(gu[:, :d_ff], SWIGLU_LIMIT)
        u = jnp.clip(gu[:, d_ff:], -SWIGLU_LIMIT, SWIGLU_LIMIT)
        act = g * jax.nn.sigmoid(SWIGLU_ALPHA * g) * (u + 1.0)
        o = jnp.dot(act.astype(jnp.bfloat16), wd_bf[...], preferred_element_type=jnp.float32) + bd_ref[0]
        out_ref[...] = _pack_bf16_pairs(o)

    @pl.when(i >= n_active_ref[0])
    def _():
        out_ref[...] = jnp.zeros(out_ref.shape, out_ref.dtype)


def _experts(plan, xs, wgu, bgu, wd, bd):
    blk_exp, first, slot, nxt, n_active = plan
    n_rows, half = xs.shape
    n_exp, d, two_f = wgu.shape
    d_ff = wd.shape[1]
    n_blocks = n_rows // MOE_BLOCK
    grid_spec = pltpu.PrefetchScalarGridSpec(
        num_scalar_prefetch=5,
        grid=(n_blocks,),
        in_specs=[
            pl.BlockSpec((MOE_BLOCK, half), lambda i, be, *_: (i, 0)),
            pl.BlockSpec(memory_space=pl.ANY),
            pl.BlockSpec((1, 1, two_f), lambda i, be, *_: (be[i], 0, 0)),
            pl.BlockSpec(memory_space=pl.ANY),
            pl.BlockSpec((1, 1, d), lambda i, be, *_: (be[i], 0, 0)),
        ],
        out_specs=pl.BlockSpec((MOE_BLOCK, half), lambda i, be, *_: (i, 0)),
        scratch_shapes=[
            pltpu.VMEM((2, d, two_f), jnp.float32),
            pltpu.VMEM((2, d_ff, d), jnp.float32),
            pltpu.VMEM((d, two_f), jnp.bfloat16),
            pltpu.VMEM((d_ff, d), jnp.bfloat16),
            pltpu.SemaphoreType.DMA((2, 2)),
        ],
    )
    return pl.pallas_call(
        _experts_kernel,
        grid_spec=grid_spec,
        out_shape=jax.ShapeDtypeStruct((n_rows, half), jnp.uint32),
        compiler_params=pltpu.CompilerParams(
            dimension_semantics=("arbitrary",), vmem_limit_bytes=VMEM_LIMIT),
        name="experts",
    )(blk_exp, first, slot, nxt, n_active, xs, wgu, bgu.reshape(n_exp, 1, two_f), wd,
      bd.reshape(n_exp, 1, d))


def _combine_kernel(rows_ref, gate_ref, h_ref, gfin_ref, *rest):
    y_ref = rest[-1]
    acc = h_ref[...]
    for k in range(TOP_K):
        acc = acc + gate_ref[:, k:k + 1] * _unpack_bf16_pairs(rows_ref[k])
    y_ref[...] = _rms(acc, gfin_ref[...])


def _combine(rows4, gate_t, h2d, gfin, *, tok0=0, y_prev=None, y_rows=None, y_tok0=0):
    d = h2d.shape[1]
    n, half = rows4.shape[1:]
    n_tok = h2d.shape[0] if y_rows is None else y_rows
    tile = min(n, 256)
    t0 = tok0 // tile
    y0 = y_tok0 // tile
    in_specs = [
        pl.BlockSpec((TOP_K, tile, half), lambda i: (0, i, 0)),
        pl.BlockSpec((tile, TOP_K), lambda i: (t0 + i, 0)),
        pl.BlockSpec((tile, d), lambda i: (t0 + i, 0)),
        pl.BlockSpec((1, d), lambda i: (0, 0)),
    ]
    args = [rows4, gate_t, h2d, gfin]
    aliases = {}
    if y_prev is not None:
        in_specs.append(pl.BlockSpec(memory_space=pl.ANY))
        args.append(y_prev)
        aliases = {4: 0}
    return pl.pallas_call(
        _combine_kernel,
        grid=(n // tile,),
        in_specs=in_specs,
        out_specs=pl.BlockSpec((tile, d), lambda i: (y0 + i, 0)),
        out_shape=jax.ShapeDtypeStruct((n_tok, d), jnp.float32),
        input_output_aliases=aliases,
        compiler_params=pltpu.CompilerParams(dimension_semantics=("arbitrary",)),
        name="combine",
    )(*args)


def _pad_hist(hist, rows):
    return jnp.pad(hist, ((0, 0), (rows - hist.shape[1], 0), (0, 0)))


def kernel(x_prompt, x_sample, state_conv_a, state_conv_b, norm_mix_g, w_in, conv_a_w, conv_b_w, conv_b_b, conv_ln_g, conv_ln_b, out_norm_a_g, out_norm_b_g, w_out, norm_ffn_g, w_router, b_router, w_gate_up, b_gate_up, w_down, b_down, final_norm_g):
    depth = w_in.shape[0]
    assert depth == 1, "single-layer trunk"
    bf16 = jnp.bfloat16
    bsz, seq, d = x_prompt.shape
    dec_b, dec_s, _ = x_sample.shape
    d_a, d_b = conv_a_w.shape[-1], conv_b_w.shape[-1]
    n_exp = w_router.shape[-1]

    params = (norm_mix_g[0][None], w_in[0].astype(bf16), conv_a_w[0], conv_b_w[0], conv_b_b[0][None],
              conv_ln_g[0][None], conv_ln_b[0][None], out_norm_a_g[0][None], out_norm_b_g[0][None],
              w_out[0].astype(bf16), norm_ffn_g[0][None], w_router[0].T.astype(bf16),
              b_router[0][:, None])

    zero_a = jnp.zeros((bsz, HIST_A_ROWS, d_a), jnp.float32)
    zero_b = jnp.zeros((bsz, HIST_B_ROWS, d_b), jnp.float32)
    cnt0 = jnp.zeros((n_exp, 1), jnp.float32)
    gfin = final_norm_g[None]
    ts = min(seq, MIXER_ROWS)
    n_p = bsz * seq

    n_groups = MOE_GROUPS if bsz % MOE_GROUPS == 0 else 1
    per_b = bsz // n_groups
    groups = []
    for g in range(n_groups):
        h, hnp, idx, gate, rank, cnt, na, nb_ = _mixer(
            x_prompt, zero_a, zero_b, cnt0, params, nb=1, ts=ts, b0=g * per_b, bsz=per_b)
        parts = [dict(h=h.reshape(per_b * seq, d), hnp=hnp, idx=idx, gate_t=gate.T, rank=rank)]
        states = [(na, nb_)]
        if g == n_groups - 1:
            h, hnp, idx, gate, rank, cnt, na_s, nb_s = _mixer(
                x_sample, _pad_hist(state_conv_a[0], HIST_A_ROWS), _pad_hist(state_conv_b[0], HIST_B_ROWS),
                cnt, params, nb=dec_b, ts=dec_s)
            parts.append(dict(h=h.reshape(dec_b * dec_s, d), hnp=hnp, idx=idx, gate_t=gate.T, rank=rank))
        n_pairs = sum(p["hnp"].shape[0] for p in parts) * TOP_K
        plan, pad_start, n_blocks = _block_plan(cnt[:, 0].astype(jnp.int32), n_pairs, n_exp)
        pos = _positions(pad_start, [(p["idx"], p["rank"]) for p in parts])
        xs = _sc_dispatch([(q, p["hnp"]) for q, p in zip(pos, parts)], n_blocks * MOE_BLOCK)
        groups.append(dict(parts=parts, pos=pos, plan=plan, xs=xs, states=states))

    for grp in groups:
        grp["rows"] = _experts(grp["plan"], grp["xs"], w_gate_up[0], b_gate_up[0], w_down[0], b_down[0])

    y_p = None
    for g, grp in enumerate(groups):
        main, pos0 = grp["parts"][0], grp["pos"][0]
        n_g = main["h"].shape[0]
        n_chunks = COMBINE_CHUNKS if n_g % (COMBINE_CHUNKS * SC_CHUNK * SC_WORKERS) == 0 else 1
        per = n_g // n_chunks
        for c in range(n_chunks):
            extra = grp["pos"][1:] if c == 0 else []
            got = _sc_gather([pos0[:, c * per:(c + 1) * per]] + extra, grp["rows"])
            if extra:
                smp = grp["parts"][1]
                y_s = _combine(got[1], smp["gate_t"], smp["h"], gfin).reshape(dec_b, dec_s, d)
            y_p = _combine(got[0], main["gate_t"], main["h"], gfin, tok0=c * per,
                           y_prev=y_p, y_rows=n_p, y_tok0=g * n_g + c * per)
    y_p = y_p.reshape(bsz, seq, d)

    na_p = jnp.concatenate([grp["states"][0][0] for grp in groups], axis=0)
    nb_p = jnp.concatenate([grp["states"][0][1] for grp in groups], axis=0)
    return (y_p, y_s, na_p[None], nb_p[None], na_s[None], nb_s[None])


def _block_plan(counts, n_pairs, n_exp):
    padded = (counts + MOE_BLOCK - 1) // MOE_BLOCK * MOE_BLOCK
    pad_end = jnp.cumsum(padded)
    pad_start = pad_end - padded
    n_blocks = -(-n_pairs // MOE_BLOCK) + n_exp
    blk_start = jnp.arange(n_blocks, dtype=jnp.int32) * MOE_BLOCK
    blk_exp = jnp.minimum(jnp.sum(blk_start[:, None] >= pad_end[None, :], axis=1),
                          n_exp - 1).astype(jnp.int32)
    n_active = (pad_end[-1:] // MOE_BLOCK).astype(jnp.int32)
    e_ids = jnp.arange(n_exp, dtype=jnp.int32)
    has = padded > 0
    slot_e = (jnp.cumsum(has.astype(jnp.int32)) - 1) % 2
    later = jnp.where(has, e_ids, n_exp)
    next_e = jnp.concatenate([lax.cummin(later, reverse=True)[1:], jnp.full((1,), n_exp, jnp.int32)])
    next_e = jnp.where(next_e >= n_exp, -1, next_e)
    blk_onehot = blk_exp[:, None] == e_ids[None, :]

    def per_block(table):
        return jnp.sum(jnp.where(blk_onehot, table[None, :], 0), axis=1).astype(jnp.int32)

    blk_first = ((blk_start == per_block(pad_start)) & (blk_start < pad_end[-1])).astype(jnp.int32)
    plan = (blk_exp, blk_first, per_block(slot_e), per_block(next_e), n_active)
    return plan, pad_start.astype(jnp.int32), n_blocks
```

```python
import functools

import jax
import jax.numpy as jnp
from jax import lax
from jax.experimental import pallas as pl
from jax.experimental.pallas import tpu as pltpu
from jax.experimental.pallas import tpu_sc as plsc

SC_CORES = 2
SC_SUBCORES = 16
SC_WORKERS = SC_CORES * SC_SUBCORES
SC_CHUNK = 128
CONV_A_W = 3
CONV_B_W = 31
TOP_K = 4
SWIGLU_LIMIT = 7.0
SWIGLU_ALPHA = 1.702
NORM_EPS = 1e-5
MOE_BLOCK = 256
BLOCKS_PER_STEP = 2
SUBLANES = 8
LANES = 128
CONV_ROW_CHUNK = 128
HIST_A_ROWS = 8
HIST_B_ROWS = 32
MIXER_ROWS = 512
PROJ_SECTION = 256
POS_TILE = 4096
MOE_GROUPS = 2
COMBINE_CHUNKS = 2
VMEM_LIMIT = 56 * 1024 * 1024


def _rms(x, g):
    return x * lax.rsqrt(jnp.mean(x * x, axis=-1, keepdims=True) + NORM_EPS) * g


def _zero_tile_from(v):
    bits = pltpu.bitcast(v[0:SUBLANES, 0:LANES], jnp.uint32)
    return pltpu.bitcast((bits >> 16) >> 16, jnp.float32)


def _pack_bf16_pairs(a):
    w = a.shape[-1] // 2
    lo = pltpu.bitcast(a[:, :w].astype(jnp.bfloat16).astype(jnp.float32), jnp.uint32)
    hi = pltpu.bitcast(a[:, w:].astype(jnp.bfloat16).astype(jnp.float32), jnp.uint32)
    return (lo >> 16) | (hi & jnp.uint32(0xFFFF0000))


def _unpack_bf16_pairs(wd):
    lo = pltpu.bitcast(wd << 16, jnp.float32)
    hi = pltpu.bitcast(wd & jnp.uint32(0xFFFF0000), jnp.float32)
    return jnp.concatenate([lo, hi], axis=-1)


def _conv_b_chunk(ub_buf, cbw_ref, n_i, r0, rc, c0):
    cs = slice(c0, c0 + LANES)
    lead = HIST_B_ROWS - (CONV_B_W - 1)
    acc = None
    for b in range(SUBLANES):
        rows = rc if b == 0 else rc + SUBLANES
        q = None
        for a in range((HIST_B_ROWS + SUBLANES) // SUBLANES):
            k = SUBLANES * a + b - lead
            if k < 0 or k >= CONV_B_W:
                continue
            start = r0 + SUBLANES * a
            term = cbw_ref[k:k + 1, cs] * ub_buf[n_i, start:start + rows, cs]
            q = term if q is None else q + term
        part = q if b == 0 else q[b:b + rc]
        acc = part if acc is None else acc + part
    return acc


def _mixer_kernel(x_ref, hista_ref, histb_ref, cnt0_ref, tri_ref,
                  gmix_ref, win_ref, caw_ref, cbw_ref, cbb_ref, lng_ref, lnb_ref,
                  ga_ref, gb_ref, wout_ref, gffn_ref, wrt_ref, br_ref,
                  h_ref, hnp_ref, idx_ref, gate_ref, rank_ref, cnt_ref, newa_ref, newb_ref,
                  ua_buf, ub_buf, cb_buf, cnt_acc, *, nb, ts, d_a, d_b, n_exp):
    b = pl.program_id(0)
    s = pl.program_id(1)
    m = nb * ts
    d = x_ref.shape[-1]

    @pl.when(s == 0)
    def _():
        ua_buf[:, 0:HIST_A_ROWS, :] = hista_ref[...]
        ub_buf[:, 0:HIST_B_ROWS, :] = histb_ref[...]

    @pl.when(s != 0)
    def _():
        ua_buf[:, 0:HIST_A_ROWS, :] = ua_buf[:, ts:ts + HIST_A_ROWS, :]
        ub_buf[:, 0:HIST_B_ROWS, :] = ub_buf[:, ts:ts + HIST_B_ROWS, :]

    @pl.when((b == 0) & (s == 0))
    def _():
        cnt_acc[...] = cnt0_ref[...]

    x = x_ref[...].reshape(m, d)
    n = _rms(x, gmix_ref[...]).astype(jnp.bfloat16)
    def proj(c0, width):
        return jnp.dot(n, win_ref[:, c0:c0 + width], preferred_element_type=jnp.float32)

    sec = min(d_b, PROJ_SECTION)
    for c0 in range(0, d_b, sec):
        v_b = proj(3 * d_a + c0, sec)
        g_b = proj(3 * d_a + d_b + c0, sec)
        ub_buf[:, HIST_B_ROWS:HIST_B_ROWS + ts, c0:c0 + sec] = (v_b * jax.nn.sigmoid(g_b)).reshape(nb, ts, sec)

    gate_c = proj(d_a, d_a)
    xt = proj(2 * d_a, d_a)
    gate_b = proj(0, d_a)
    ua_buf[:, HIST_A_ROWS:HIST_A_ROWS + ts, :] = (gate_c * xt).reshape(nb, ts, d_a)
    conv_a = jnp.zeros((nb, ts, d_a), jnp.float32)
    for k in range(CONV_A_W):
        off = HIST_A_ROWS - (CONV_A_W - 1) + k
        conv_a = conv_a + caw_ref[k:k + 1, :] * ua_buf[:, off:off + ts, :]
    y_a = gate_b * conv_a.reshape(m, d_a)

    mix_a = jnp.dot(_rms(y_a, ga_ref[...]).astype(jnp.bfloat16), wout_ref[0:d_a, :],
                    preferred_element_type=jnp.float32)

    rc = min(ts, CONV_ROW_CHUNK)
    chunks = [(n_i, r0, c0) for n_i in range(nb) for r0 in range(0, ts, rc) for c0 in range(0, d_b, LANES)]
    anchors = {len(chunks) * 2 // 8: gate_c, len(chunks) * 3 // 8: xt, len(chunks) * 5 // 8: gate_b,
               len(chunks) - 1: mix_a}
    for ci, (n_i, r0, c0) in enumerate(chunks):
        acc = _conv_b_chunk(ub_buf, cbw_ref, n_i, r0, rc, c0)
        if ci in anchors:
            acc = (acc.reshape(rc // SUBLANES, SUBLANES, LANES) + _zero_tile_from(anchors[ci])[None]
                   ).reshape(rc, LANES)
        cb_buf[n_i, r0:r0 + rc, c0:c0 + LANES] = acc
    cb = cb_buf[...].reshape(m, d_b) + cbb_ref[...]
    mu = jnp.mean(cb, axis=-1, keepdims=True)
    xc = cb - mu
    var = jnp.mean(xc * xc, axis=-1, keepdims=True)
    ln = xc * lax.rsqrt(var + NORM_EPS) * lng_ref[...] + lnb_ref[...]
    y_b = ln * jax.nn.sigmoid(ln)

    mix_b = jnp.dot(_rms(y_b, gb_ref[...]).astype(jnp.bfloat16), wout_ref[d_a:d_a + d_b, :],
                    preferred_element_type=jnp.float32)
    h = x + (mix_a + mix_b)
    h_ref[...] = h.reshape(nb, ts, d)

    hn = _rms(h, gffn_ref[...])
    hnp_ref[...] = _pack_bf16_pairs(hn)

    logits = lax.dot_general(wrt_ref[...], hn.astype(jnp.bfloat16), (((1,), (1,)), ((), ())),
                             preferred_element_type=jnp.float32) + br_ref[...]
    e_iota = lax.broadcasted_iota(jnp.int32, (n_exp, m), 0)
    cur = logits
    vals, idxs, sels = [], [], []
    for _ in range(TOP_K):
        mx = jnp.max(cur, axis=0, keepdims=True)
        ix = jnp.min(jnp.where(cur == mx, e_iota, n_exp), axis=0, keepdims=True)
        sel = e_iota == ix
        vals.append(mx)
        idxs.append(ix)
        sels.append(sel)
        cur = jnp.where(sel, -jnp.inf, cur)
    exps = [jnp.exp(v - vals[0]) for v in vals]
    denom = exps[0] + exps[1] + exps[2] + exps[3]
    gates = [e / denom for e in exps]

    onehot = jnp.zeros((n_exp, m), jnp.float32)
    for sel in sels:
        onehot = onehot + sel.astype(jnp.float32)
    before = jnp.dot(onehot.astype(jnp.bfloat16), tri_ref[...],
                     preferred_element_type=jnp.float32) + cnt_acc[...]
    ranks = [jnp.sum(jnp.where(sel, before, 0.0), axis=0, keepdims=True) for sel in sels]
    new_cnt = cnt_acc[...] + jnp.sum(onehot, axis=1, keepdims=True)
    cnt_acc[...] = new_cnt
    cnt_ref[...] = new_cnt

    idx_ref[...] = jnp.concatenate(idxs, axis=0)
    gate_ref[...] = jnp.concatenate(gates, axis=0)
    rank_ref[...] = jnp.concatenate(ranks, axis=0).astype(jnp.int32)

    @pl.when(s == pl.num_programs(1) - 1)
    def _():
        newa_ref[...] = ua_buf[:, ts + HIST_A_ROWS - (CONV_A_W - 1):ts + HIST_A_ROWS, :]
        newb_ref[...] = ub_buf[:, ts + HIST_B_ROWS - (CONV_B_W - 1):ts + HIST_B_ROWS, :]


def _mixer(x, hist_a, hist_b, cnt0, params, *, nb, ts, b0=0, bsz=None):
    (gmix, win, caw, cbw, cbb, lng, lnb, ga, gb, wout, gffn, wrt, br) = params
    _, seq, d = x.shape
    bsz = x.shape[0] if bsz is None else bsz
    d_a, d_b = caw.shape[-1], cbw.shape[-1]
    n_exp = wrt.shape[0]
    m = nb * ts
    n_tok = bsz * seq
    g0 = b0 // nb
    grid = (bsz // nb, seq // ts)
    tri = (jnp.arange(m)[:, None] < jnp.arange(m)[None, :]).astype(jnp.bfloat16)

    def const(shape):
        return pl.BlockSpec(shape, lambda i, j: (0,) * len(shape))

    tok_map = lambda i, j: (0, i * (seq // ts) + j)
    kern = functools.partial(_mixer_kernel, nb=nb, ts=ts, d_a=d_a, d_b=d_b, n_exp=n_exp)
    out_shape = (
        jax.ShapeDtypeStruct((bsz, seq, d), jnp.float32),
        jax.ShapeDtypeStruct((n_tok, d // 2), jnp.uint32),
        jax.ShapeDtypeStruct((TOP_K, n_tok), jnp.int32),
        jax.ShapeDtypeStruct((TOP_K, n_tok), jnp.float32),
        jax.ShapeDtypeStruct((TOP_K, n_tok), jnp.int32),
        jax.ShapeDtypeStruct((n_exp, 1), jnp.float32),
        jax.ShapeDtypeStruct((bsz, CONV_A_W - 1, d_a), jnp.float32),
        jax.ShapeDtypeStruct((bsz, CONV_B_W - 1, d_b), jnp.float32),
    )
    return pl.pallas_call(
        kern,
        grid=grid,
        in_specs=[
            pl.BlockSpec((nb, ts, d), lambda i, j: (g0 + i, j, 0)),
            pl.BlockSpec((nb, HIST_A_ROWS, d_a), lambda i, j: (g0 + i, 0, 0)),
            pl.BlockSpec((nb, HIST_B_ROWS, d_b), lambda i, j: (g0 + i, 0, 0)),
            const((n_exp, 1)),
            const((m, m)),
            const((1, d)), const(win.shape), const(caw.shape), const(cbw.shape), const((1, d_b)),
            const((1, d_b)), const((1, d_b)), const((1, d_a)), const((1, d_b)), const(wout.shape),
            const((1, d)), const(wrt.shape), const((n_exp, 1)),
        ],
        out_specs=(
            pl.BlockSpec((nb, ts, d), lambda i, j: (i, j, 0)),
            pl.BlockSpec((m, d // 2), lambda i, j: (i * (seq // ts) + j, 0)),
            pl.BlockSpec((TOP_K, m), tok_map),
            pl.BlockSpec((TOP_K, m), tok_map),
            pl.BlockSpec((TOP_K, m), tok_map),
            pl.BlockSpec((n_exp, 1), lambda i, j: (0, 0)),
            pl.BlockSpec((nb, CONV_A_W - 1, d_a), lambda i, j: (i, 0, 0)),
            pl.BlockSpec((nb, CONV_B_W - 1, d_b), lambda i, j: (i, 0, 0)),
        ),
        out_shape=out_shape,
        scratch_shapes=[
            pltpu.VMEM((nb, HIST_A_ROWS + ts, d_a), jnp.float32),
            pltpu.VMEM((nb, HIST_B_ROWS + ts, d_b), jnp.float32),
            pltpu.VMEM((nb, ts, d_b), jnp.float32),
            pltpu.VMEM((n_exp, 1), jnp.float32),
        ],
        compiler_params=pltpu.CompilerParams(
            dimension_semantics=("arbitrary", "arbitrary"), vmem_limit_bytes=VMEM_LIMIT),
        name="mixer",
    )(x, hist_a, hist_b, cnt0, tri, gmix, win, caw, cbw, cbb, lng, lnb, ga, gb, wout, gffn, wrt, br)


def _positions_kernel(start_ref, *refs):
    n_parts = len(refs) // 3
    for p in range(n_parts):
        idx = refs[2 * p][...]
        pos = refs[2 * p + 1][...]
        for e in range(start_ref.shape[0]):
            pos = pos + jnp.where(idx == e, start_ref[e], 0)
        refs[2 * n_parts + p][...] = pos


def _positions(pad_start, parts):
    n_0 = parts[0][0].shape[1]
    tile = min(n_0, POS_TILE)
    specs = [pl.BlockSpec((TOP_K, tile), lambda i: (0, i))]
    specs += [pl.BlockSpec((TOP_K, idx.shape[1]), lambda i: (0, 0)) for idx, _ in parts[1:]]
    in_specs = [pl.BlockSpec(memory_space=pltpu.SMEM)]
    args = [pad_start]
    for spec, (idx, rank) in zip(specs, parts):
        in_specs += [spec, spec]
        args += [idx, rank]
    out = pl.pallas_call(
        _positions_kernel,
        grid=(n_0 // tile,),
        in_specs=in_specs,
        out_specs=tuple(specs),
        out_shape=tuple(jax.ShapeDtypeStruct(idx.shape, jnp.int32) for idx, _ in parts),
        compiler_params=pltpu.CompilerParams(dimension_semantics=("arbitrary",)),
        name="positions",
    )(*args)
    return list(out)


def _sc_mesh():
    return plsc.VectorSubcoreMesh(core_axis_name="c", subcore_axis_name="s",
                                  num_cores=SC_CORES, num_subcores=SC_SUBCORES)


def _sc_worker_id():
    return lax.axis_index("s") * SC_CORES + lax.axis_index("c")


def _sc_chunks(n_tok, wid):
    n_chunks = n_tok // SC_CHUNK
    assert n_tok % SC_CHUNK == 0
    if n_chunks % SC_WORKERS == 0:
        per_w = n_chunks // SC_WORKERS
        return None, per_w, lambda i: (wid * per_w + i) * SC_CHUNK
    assert n_chunks <= SC_WORKERS
    return wid < n_chunks, 1, lambda i: wid * SC_CHUNK


def _sc_for_each_chunk(n_tok, fn):
    wid = _sc_worker_id()
    pred, trips, base = _sc_chunks(n_tok, wid)

    def run():
        @pl.loop(0, trips)
        def _(i):
            fn(pl.multiple_of(base(i), SC_CHUNK))

    if pred is None:
        run()
    else:
        pl.when(pred)(run)


def _sc_dispatch(parts, n_rows):
    half = parts[0][1].shape[1]
    n_parts = len(parts)

    @functools.partial(
        pl.kernel, mesh=_sc_mesh(),
        out_type=jax.ShapeDtypeStruct((n_rows, half), jnp.uint32),
        scratch_types=[pltpu.VMEM((TOP_K, SC_CHUNK), jnp.int32),
                       pltpu.VMEM((SC_CHUNK, half), jnp.uint32),
                       pltpu.SemaphoreType.DMA],
        compiler_params=pltpu.CompilerParams(use_tc_tiling_on_sc=True),
        name="sc_dispatch")
    def k(*refs):
        xs_hbm, idx_v, rows_v, sem = refs[2 * n_parts:]

        def move(pos_hbm, src_hbm):
            def fn(base):
                pltpu.sync_copy(pos_hbm.at[:, pl.ds(base, SC_CHUNK)], idx_v)
                pltpu.sync_copy(src_hbm.at[pl.ds(base, SC_CHUNK)], rows_v)
                copies = [pltpu.async_copy(rows_v, xs_hbm.at[idx_v.at[j]], sem) for j in range(TOP_K)]
                for c in copies:
                    c.wait()
            return fn

        for p, (_, hnp) in enumerate(parts):
            _sc_for_each_chunk(hnp.shape[0], move(refs[2 * p], refs[2 * p + 1]))

    return k(*[a for part in parts for a in part])


def _sc_gather(pos_parts, rows):
    half = rows.shape[1]
    n_parts = len(pos_parts)

    @functools.partial(
        pl.kernel, mesh=_sc_mesh(),
        out_type=tuple(jax.ShapeDtypeStruct((TOP_K, p.shape[1], half), jnp.uint32) for p in pos_parts),
        scratch_types=[pltpu.VMEM((TOP_K, SC_CHUNK), jnp.int32),
                       pltpu.VMEM((SC_CHUNK, half), jnp.uint32),
                       pltpu.SemaphoreType.DMA],
        compiler_params=pltpu.CompilerParams(use_tc_tiling_on_sc=True),
        name="sc_gather")
    def k(*refs):
        pos_hbms, rows_hbm = refs[:n_parts], refs[n_parts]
        out_hbms = refs[n_parts + 1:2 * n_parts + 1]
        idx_v, buf_v, sem = refs[2 * n_parts + 1:]

        def move(pos_hbm, out_hbm):
            def fn(base):
                pltpu.sync_copy(pos_hbm.at[:, pl.ds(base, SC_CHUNK)], idx_v)
                for j in range(TOP_K):
                    pltpu.async_copy(rows_hbm.at[idx_v.at[j]], buf_v, sem).wait()
                    pltpu.sync_copy(buf_v, out_hbm.at[j, pl.ds(base, SC_CHUNK)])
            return fn

        for pos_hbm, out_hbm, p in zip(pos_hbms, out_hbms, pos_parts):
            _sc_for_each_chunk(p.shape[1], move(pos_hbm, out_hbm))

    out = k(*pos_parts, rows)
    return out if isinstance(out, (tuple, list)) else (out,)


def _experts_kernel(blk_exp_ref, first_ref, slot_ref, next_ref, n_active_ref,
                    xs_ref, wgu_hbm, bgu_ref, wd_hbm, bd_ref, out_ref,
                    wgu_f32, wd_f32, wgu_bf, wd_bf, sems):
    d_ff = wd_bf.shape[0]

    def fetch(e, sl):
        return (pltpu.make_async_copy(wgu_hbm.at[e], wgu_f32.at[sl], sems.at[0, sl]),
                pltpu.make_async_copy(wd_hbm.at[e], wd_f32.at[sl], sems.at[1, sl]))

    for sub in range(BLOCKS_PER_STEP):
        i = pl.program_id(0) * BLOCKS_PER_STEP + sub
        rows = slice(sub * MOE_BLOCK, (sub + 1) * MOE_BLOCK)

        @pl.when(i < n_active_ref[0])
        def _():
            sl = slot_ref[i]
            e = blk_exp_ref[i]

            @pl.when(first_ref[i] == 1)
            def _():
                @pl.when(i == 0)
                def _():
                    for c in fetch(e, sl):
                        c.start()

                for c in fetch(e, sl):
                    c.wait()
                wgu_bf[...] = wgu_f32[sl].astype(jnp.bfloat16)
                wd_bf[...] = wd_f32[sl].astype(jnp.bfloat16)

                @pl.when(next_ref[i] >= 0)
                def _():
                    for c in fetch(next_ref[i], 1 - sl):
                        c.start()

            x = _unpack_bf16_pairs(xs_ref[rows, :]).astype(jnp.bfloat16)
            gu = jnp.dot(x, wgu_bf[...], preferred_element_type=jnp.float32) + bgu_ref[e]
            g = jnp.minimum(gu[:, :d_ff], SWIGLU_LIMIT)
            u = jnp.clip(gu[:, d_ff:], -SWIGLU_LIMIT, SWIGLU_LIMIT)
            act = g * jax.nn.sigmoid(SWIGLU_ALPHA * g) * (u + 1.0)
            o = jnp.dot(act.astype(jnp.bfloat16), wd_bf[...], preferred_element_type=jnp.float32) + bd_ref[e]
            out_ref[rows, :] = _pack_bf16_pairs(o)

        @pl.when(i >= n_active_ref[0])
        def _():
            out_ref[rows, :] = jnp.zeros((MOE_BLOCK, out_ref.shape[1]), out_ref.dtype)


def _experts(plan, xs, wgu, bgu, wd, bd):
    blk_exp, first, slot, nxt, n_active = plan
    n_rows, half = xs.shape
    n_exp, d, two_f = wgu.shape
    d_ff = wd.shape[1]
    n_blocks = n_rows // MOE_BLOCK
    step_rows = MOE_BLOCK * BLOCKS_PER_STEP
    assert n_rows % step_rows == 0
    grid_spec = pltpu.PrefetchScalarGridSpec(
        num_scalar_prefetch=5,
        grid=(n_blocks // BLOCKS_PER_STEP,),
        in_specs=[
            pl.BlockSpec((step_rows, half), lambda i, *_: (i, 0)),
            pl.BlockSpec(memory_space=pl.ANY),
            pl.BlockSpec((n_exp, 1, two_f), lambda i, *_: (0, 0, 0)),
            pl.BlockSpec(memory_space=pl.ANY),
            pl.BlockSpec((n_exp, 1, d), lambda i, *_: (0, 0, 0)),
        ],
        out_specs=pl.BlockSpec((step_rows, half), lambda i, *_: (i, 0)),
        scratch_shapes=[
            pltpu.VMEM((2, d, two_f), jnp.float32),
            pltpu.VMEM((2, d_ff, d), jnp.float32),
            pltpu.VMEM((d, two_f), jnp.bfloat16),
            pltpu.VMEM((d_ff, d), jnp.bfloat16),
            pltpu.SemaphoreType.DMA((2, 2)),
        ],
    )
    return pl.pallas_call(
        _experts_kernel,
        grid_spec=grid_spec,
        out_shape=jax.ShapeDtypeStruct((n_rows, half), jnp.uint32),
        compiler_params=pltpu.CompilerParams(
            dimension_semantics=("arbitrary",), vmem_limit_bytes=VMEM_LIMIT),
        name="experts",
    )(blk_exp, first, slot, nxt, n_active, xs, wgu, bgu.reshape(n_exp, 1, two_f), wd,
      bd.reshape(n_exp, 1, d))


def _combine_kernel(rows_ref, gate_ref, h_ref, gfin_ref, *rest):
    y_ref = rest[-1]
    acc = h_ref[...]
    for k in range(TOP_K):
        acc = acc + gate_ref[:, k:k + 1] * _unpack_bf16_pairs(rows_ref[k])
    y_ref[...] = _rms(acc, gfin_ref[...])


def _combine(rows4, gate_t, h2d, gfin, *, tok0=0, y_prev=None, y_rows=None, y_tok0=0):
    d = h2d.shape[1]
    n, half = rows4.shape[1:]
    n_tok = h2d.shape[0] if y_rows is None else y_rows
    tile = min(n, 256)
    t0 = tok0 // tile
    y0 = y_tok0 // tile
    in_specs = [
        pl.BlockSpec((TOP_K, tile, half), lambda i: (0, i, 0)),
        pl.BlockSpec((tile, TOP_K), lambda i: (t0 + i, 0)),
        pl.BlockSpec((tile, d), lambda i: (t0 + i, 0)),
        pl.BlockSpec((1, d), lambda i: (0, 0)),
    ]
    args = [rows4, gate_t, h2d, gfin]
    aliases = {}
    if y_prev is not None:
        in_specs.append(pl.BlockSpec(memory_space=pl.ANY))
        args.append(y_prev)
        aliases = {4: 0}
    return pl.pallas_call(
        _combine_kernel,
        grid=(n // tile,),
        in_specs=in_specs,
        out_specs=pl.BlockSpec((tile, d), lambda i: (y0 + i, 0)),
        out_shape=jax.ShapeDtypeStruct((n_tok, d), jnp.float32),
        input_output_aliases=aliases,
        compiler_params=pltpu.CompilerParams(dimension_semantics=("arbitrary",)),
        name="combine",
    )(*args)


def _pad_hist(hist, rows):
    return jnp.pad(hist, ((0, 0), (rows - hist.shape[1], 0), (0, 0)))


def kernel(x_prompt, x_sample, state_conv_a, state_conv_b, norm_mix_g, w_in, conv_a_w, conv_b_w, conv_b_b, conv_ln_g, conv_ln_b, out_norm_a_g, out_norm_b_g, w_out, norm_ffn_g, w_router, b_router, w_gate_up, b_gate_up, w_down, b_down, final_norm_g):
    depth = w_in.shape[0]
    assert depth == 1, "single-layer trunk"
    bf16 = jnp.bfloat16
    bsz, seq, d = x_prompt.shape
    dec_b, dec_s, _ = x_sample.shape
    d_a, d_b = conv_a_w.shape[-1], conv_b_w.shape[-1]
    n_exp = w_router.shape[-1]

    params = (norm_mix_g[0][None], w_in[0].astype(bf16), conv_a_w[0], conv_b_w[0], conv_b_b[0][None],
              conv_ln_g[0][None], conv_ln_b[0][None], out_norm_a_g[0][None], out_norm_b_g[0][None],
              w_out[0].astype(bf16), norm_ffn_g[0][None], w_router[0].T.astype(bf16),
              b_router[0][:, None])

    zero_a = jnp.zeros((bsz, HIST_A_ROWS, d_a), jnp.float32)
    zero_b = jnp.zeros((bsz, HIST_B_ROWS, d_b), jnp.float32)
    cnt0 = jnp.zeros((n_exp, 1), jnp.float32)
    gfin = final_norm_g[None]
    ts = min(seq, MIXER_ROWS)
    n_p = bsz * seq

    n_groups = MOE_GROUPS if bsz % MOE_GROUPS == 0 else 1
    per_b = bsz // n_groups
    groups = []
    for g in range(n_groups):
        h, hnp, idx, gate, rank, cnt, na, nb_ = _mixer(
            x_prompt, zero_a, zero_b, cnt0, params, nb=1, ts=ts, b0=g * per_b, bsz=per_b)
        parts = [dict(h=h.reshape(per_b * seq, d), hnp=hnp, idx=idx, gate_t=gate.T, rank=rank)]
        states = [(na, nb_)]
        if g == n_groups - 1:
            h, hnp, idx, gate, rank, cnt, na_s, nb_s = _mixer(
                x_sample, _pad_hist(state_conv_a[0], HIST_A_ROWS), _pad_hist(state_conv_b[0], HIST_B_ROWS),
                cnt, params, nb=dec_b, ts=dec_s)
            parts.append(dict(h=h.reshape(dec_b * dec_s, d), hnp=hnp, idx=idx, gate_t=gate.T, rank=rank))
        n_pairs = sum(p["hnp"].shape[0] for p in parts) * TOP_K
        plan, pad_start, n_blocks = _block_plan(cnt[:, 0].astype(jnp.int32), n_pairs, n_exp)
        pos = _positions(pad_start, [(p["idx"], p["rank"]) for p in parts])
        xs = _sc_dispatch([(q, p["hnp"]) for q, p in zip(pos, parts)], n_blocks * MOE_BLOCK)
        groups.append(dict(parts=parts, pos=pos, plan=plan, xs=xs, states=states))

    for grp in groups:
        grp["rows"] = _experts(grp["plan"], grp["xs"], w_gate_up[0], b_gate_up[0], w_down[0], b_down[0])

    y_p = None
    for g, grp in enumerate(groups):
        main, pos0 = grp["parts"][0], grp["pos"][0]
        n_g = main["h"].shape[0]
        n_chunks = COMBINE_CHUNKS if n_g % (COMBINE_CHUNKS * SC_CHUNK * SC_WORKERS) == 0 else 1
        per = n_g // n_chunks
        for c in range(n_chunks):
            extra = grp["pos"][1:] if c == 0 else []
            got = _sc_gather([pos0[:, c * per:(c + 1) * per]] + extra, grp["rows"])
            if extra:
                smp = grp["parts"][1]
                y_s = _combine(got[1], smp["gate_t"], smp["h"], gfin).reshape(dec_b, dec_s, d)
            y_p = _combine(got[0], main["gate_t"], main["h"], gfin, tok0=c * per,
                           y_prev=y_p, y_rows=n_p, y_tok0=g * n_g + c * per)
    y_p = y_p.reshape(bsz, seq, d)

    na_p = jnp.concatenate([grp["states"][0][0] for grp in groups], axis=0)
    nb_p = jnp.concatenate([grp["states"][0][1] for grp in groups], axis=0)
    return (y_p, y_s, na_p[None], nb_p[None], na_s[None], nb_s[None])


def _block_plan(counts, n_pairs, n_exp):
    padded = (counts + MOE_BLOCK - 1) // MOE_BLOCK * MOE_BLOCK
    pad_end = jnp.cumsum(padded)
    pad_start = pad_end - padded
    n_blocks = -(-n_pairs // MOE_BLOCK) + n_exp
    n_blocks = -(-n_blocks // BLOCKS_PER_STEP) * BLOCKS_PER_STEP
    blk_start = jnp.arange(n_blocks, dtype=jnp.int32) * MOE_BLOCK
    blk_exp = jnp.minimum(jnp.sum(blk_start[:, None] >= pad_end[None, :], axis=1),
                          n_exp - 1).astype(jnp.int32)
    n_active = (pad_end[-1:] // MOE_BLOCK).astype(jnp.int32)
    e_ids = jnp.arange(n_exp, dtype=jnp.int32)
    has = padded > 0
    slot_e = (jnp.cumsum(has.astype(jnp.int32)) - 1) % 2
    later = jnp.where(has, e_ids, n_exp)
    next_e = jnp.concatenate([lax.cummin(later, reverse=True)[1:], jnp.full((1,), n_exp, jnp.int32)])
    next_e = jnp.where(next_e >= n_exp, -1, next_e)
    blk_onehot = blk_exp[:, None] == e_ids[None, :]

    def per_block(table):
        return jnp.sum(jnp.where(blk_onehot, table[None, :], 0), axis=1).astype(jnp.int32)

    blk_first = ((blk_start == per_block(pad_start)) & (blk_start < pad_end[-1])).astype(jnp.int32)
    plan = (blk_exp, blk_first, per_block(slot_e), per_block(next_e), n_active)
    return plan, pad_start.astype(jnp.int32), n_blocks
```

```python
import functools

import jax
import jax.numpy as jnp
from jax import lax
from jax.experimental import pallas as pl
from jax.experimental.pallas import tpu as pltpu
from jax.experimental.pallas import tpu_sc as plsc

SC_CORES = 2
SC_SUBCORES = 16
SC_WORKERS = SC_CORES * SC_SUBCORES
SC_CHUNK = 128
CONV_A_W = 3
CONV_B_W = 31
TOP_K = 4
SWIGLU_LIMIT = 7.0
SWIGLU_ALPHA = 1.702
NORM_EPS = 1e-5
MOE_BLOCK = 256
BLOCKS_PER_STEP = 2
SUBLANES = 8
LANES = 128
CONV_ROW_CHUNK = 128
HIST_A_ROWS = 8
HIST_B_ROWS = 32
MIXER_ROWS = 512
PROJ_SECTION = 256
POS_TILE = 4096
MOE_GROUPS = 2
COMBINE_CHUNKS = 2
VMEM_LIMIT = 56 * 1024 * 1024


def _rms(x, g):
    return x * lax.rsqrt(jnp.mean(x * x, axis=-1, keepdims=True) + NORM_EPS) * g


def _zero_tile_from(v):
    bits = pltpu.bitcast(v[0:SUBLANES, 0:LANES], jnp.uint32)
    return pltpu.bitcast((bits >> 16) >> 16, jnp.float32)


def _pack_bf16_pairs(a):
    w = a.shape[-1] // 2
    lo = pltpu.bitcast(a[:, :w].astype(jnp.bfloat16).astype(jnp.float32), jnp.uint32)
    hi = pltpu.bitcast(a[:, w:].astype(jnp.bfloat16).astype(jnp.float32), jnp.uint32)
    return (lo >> 16) | (hi & jnp.uint32(0xFFFF0000))


def _unpack_bf16_pairs(wd):
    lo = pltpu.bitcast(wd << 16, jnp.float32)
    hi = pltpu.bitcast(wd & jnp.uint32(0xFFFF0000), jnp.float32)
    return jnp.concatenate([lo, hi], axis=-1)


def _conv_b_chunk(ub_buf, cbw_ref, n_i, r0, rc, c0):
    cs = slice(c0, c0 + LANES)
    lead = HIST_B_ROWS - (CONV_B_W - 1)
    acc = None
    for b in range(SUBLANES):
        rows = rc if b == 0 else rc + SUBLANES
        q = None
        for a in range((HIST_B_ROWS + SUBLANES) // SUBLANES):
            k = SUBLANES * a + b - lead
            if k < 0 or k >= CONV_B_W:
                continue
            start = r0 + SUBLANES * a
            term = cbw_ref[k:k + 1, cs] * ub_buf[n_i, start:start + rows, cs]
            q = term if q is None else q + term
        part = q if b == 0 else q[b:b + rc]
        acc = part if acc is None else acc + part
    return acc


def _mixer_kernel(x_ref, hista_ref, histb_ref, cnt0_ref, tri_ref,
                  gmix_ref, win_ref, caw_ref, cbw_ref, cbb_ref, lng_ref, lnb_ref,
                  ga_ref, gb_ref, wout_ref, gffn_ref, wrt_ref, br_ref,
                  h_ref, hnp_ref, idx_ref, gate_ref, rank_ref, cnt_ref, newa_ref, newb_ref,
                  ua_buf, ub_buf, cb_buf, cnt_acc, *, nb, ts, d_a, d_b, n_exp):
    b = pl.program_id(0)
    s = pl.program_id(1)
    m = nb * ts
    d = x_ref.shape[-1]

    @pl.when(s == 0)
    def _():
        ua_buf[:, 0:HIST_A_ROWS, :] = hista_ref[...]
        ub_buf[:, 0:HIST_B_ROWS, :] = histb_ref[...]

    @pl.when(s != 0)
    def _():
        ua_buf[:, 0:HIST_A_ROWS, :] = ua_buf[:, ts:ts + HIST_A_ROWS, :]
        ub_buf[:, 0:HIST_B_ROWS, :] = ub_buf[:, ts:ts + HIST_B_ROWS, :]

    @pl.when((b == 0) & (s == 0))
    def _():
        cnt_acc[...] = cnt0_ref[...]

    x = x_ref[...].reshape(m, d)
    n = _rms(x, gmix_ref[...]).astype(jnp.bfloat16)
    def proj(c0, width):
        return jnp.dot(n, win_ref[:, c0:c0 + width], preferred_element_type=jnp.float32)

    sec = min(d_b, PROJ_SECTION)
    for c0 in range(0, d_b, sec):
        v_b = proj(3 * d_a + c0, sec)
        g_b = proj(3 * d_a + d_b + c0, sec)
        ub_buf[:, HIST_B_ROWS:HIST_B_ROWS + ts, c0:c0 + sec] = (v_b * jax.nn.sigmoid(g_b)).reshape(nb, ts, sec)

    gate_c = proj(d_a, d_a)
    xt = proj(2 * d_a, d_a)
    gate_b = proj(0, d_a)
    ua_buf[:, HIST_A_ROWS:HIST_A_ROWS + ts, :] = (gate_c * xt).reshape(nb, ts, d_a)
    conv_a = jnp.zeros((nb, ts, d_a), jnp.float32)
    for k in range(CONV_A_W):
        off = HIST_A_ROWS - (CONV_A_W - 1) + k
        conv_a = conv_a + caw_ref[k:k + 1, :] * ua_buf[:, off:off + ts, :]
    y_a = gate_b * conv_a.reshape(m, d_a)

    mix_a = jnp.dot(_rms(y_a, ga_ref[...]).astype(jnp.bfloat16), wout_ref[0:d_a, :],
                    preferred_element_type=jnp.float32)

    rc = min(ts, CONV_ROW_CHUNK)
    chunks = [(n_i, r0, c0) for n_i in range(nb) for r0 in range(0, ts, rc) for c0 in range(0, d_b, LANES)]
    anchors = {len(chunks) * 2 // 8: gate_c, len(chunks) * 3 // 8: xt, len(chunks) * 5 // 8: gate_b,
               len(chunks) - 1: mix_a}
    for ci, (n_i, r0, c0) in enumerate(chunks):
        acc = _conv_b_chunk(ub_buf, cbw_ref, n_i, r0, rc, c0)
        if ci in anchors:
            acc = (acc.reshape(rc // SUBLANES, SUBLANES, LANES) + _zero_tile_from(anchors[ci])[None]
                   ).reshape(rc, LANES)
        cb_buf[n_i, r0:r0 + rc, c0:c0 + LANES] = acc
    cb = cb_buf[...].reshape(m, d_b) + cbb_ref[...]
    mu = jnp.mean(cb, axis=-1, keepdims=True)
    xc = cb - mu
    var = jnp.mean(xc * xc, axis=-1, keepdims=True)
    ln = xc * lax.rsqrt(var + NORM_EPS) * lng_ref[...] + lnb_ref[...]
    y_b = ln * jax.nn.sigmoid(ln)

    mix_b = jnp.dot(_rms(y_b, gb_ref[...]).astype(jnp.bfloat16), wout_ref[d_a:d_a + d_b, :],
                    preferred_element_type=jnp.float32)
    h = x + (mix_a + mix_b)
    h_ref[...] = h.reshape(nb, ts, d)

    hn = _rms(h, gffn_ref[...])
    hnp_ref[...] = _pack_bf16_pairs(hn)

    logits = lax.dot_general(wrt_ref[...], hn.astype(jnp.bfloat16), (((1,), (1,)), ((), ())),
                             preferred_element_type=jnp.float32) + br_ref[...]
    e_iota = lax.broadcasted_iota(jnp.int32, (n_exp, m), 0)
    cur = logits
    vals, idxs, sels = [], [], []
    for _ in range(TOP_K):
        mx = jnp.max(cur, axis=0, keepdims=True)
        ix = jnp.min(jnp.where(cur == mx, e_iota, n_exp), axis=0, keepdims=True)
        sel = e_iota == ix
        vals.append(mx)
        idxs.append(ix)
        sels.append(sel)
        cur = jnp.where(sel, -jnp.inf, cur)
    exps = [jnp.exp(v - vals[0]) for v in vals]
    denom = exps[0] + exps[1] + exps[2] + exps[3]
    gates = [e / denom for e in exps]

    onehot = jnp.zeros((n_exp, m), jnp.float32)
    for sel in sels:
        onehot = onehot + sel.astype(jnp.float32)
    before = jnp.dot(onehot.astype(jnp.bfloat16), tri_ref[...],
                     preferred_element_type=jnp.float32) + cnt_acc[...]
    ranks = [jnp.sum(jnp.where(sel, before, 0.0), axis=0, keepdims=True) for sel in sels]
    new_cnt = cnt_acc[...] + jnp.sum(onehot, axis=1, keepdims=True)
    cnt_acc[...] = new_cnt
    cnt_ref[...] = new_cnt

    idx_ref[...] = jnp.concatenate(idxs, axis=0)
    gate_ref[...] = jnp.concatenate(gates, axis=0)
    rank_ref[...] = jnp.concatenate(ranks, axis=0).astype(jnp.int32)

    @pl.when(s == pl.num_programs(1) - 1)
    def _():
        newa_ref[...] = ua_buf[:, ts + HIST_A_ROWS - (CONV_A_W - 1):ts + HIST_A_ROWS, :]
        newb_ref[...] = ub_buf[:, ts + HIST_B_ROWS - (CONV_B_W - 1):ts + HIST_B_ROWS, :]


def _mixer(x, hist_a, hist_b, cnt0, params, *, nb, ts, b0=0, bsz=None):
    (gmix, win, caw, cbw, cbb, lng, lnb, ga, gb, wout, gffn, wrt, br) = params
    _, seq, d = x.shape
    bsz = x.shape[0] if bsz is None else bsz
    d_a, d_b = caw.shape[-1], cbw.shape[-1]
    n_exp = wrt.shape[0]
    m = nb * ts
    n_tok = bsz * seq
    g0 = b0 // nb
    grid = (bsz // nb, seq // ts)
    tri = (jnp.arange(m)[:, None] < jnp.arange(m)[None, :]).astype(jnp.bfloat16)

    def const(shape):
        return pl.BlockSpec(shape, lambda i, j: (0,) * len(shape))

    tok_map = lambda i, j: (0, i * (seq // ts) + j)
    kern = functools.partial(_mixer_kernel, nb=nb, ts=ts, d_a=d_a, d_b=d_b, n_exp=n_exp)
    out_shape = (
        jax.ShapeDtypeStruct((bsz, seq, d), jnp.float32),
        jax.ShapeDtypeStruct((n_tok, d // 2), jnp.uint32),
        jax.ShapeDtypeStruct((TOP_K, n_tok), jnp.int32),
        jax.ShapeDtypeStruct((TOP_K, n_tok), jnp.float32),
        jax.ShapeDtypeStruct((TOP_K, n_tok), jnp.int32),
        jax.ShapeDtypeStruct((n_exp, 1), jnp.float32),
        jax.ShapeDtypeStruct((bsz, CONV_A_W - 1, d_a), jnp.float32),
        jax.ShapeDtypeStruct((bsz, CONV_B_W - 1, d_b), jnp.float32),
    )
    return pl.pallas_call(
        kern,
        grid=grid,
        in_specs=[
            pl.BlockSpec((nb, ts, d), lambda i, j: (g0 + i, j, 0)),
            pl.BlockSpec((nb, HIST_A_ROWS, d_a), lambda i, j: (g0 + i, 0, 0)),
            pl.BlockSpec((nb, HIST_B_ROWS, d_b), lambda i, j: (g0 + i, 0, 0)),
            const((n_exp, 1)),
            const((m, m)),
            const((1, d)), const(win.shape), const(caw.shape), const(cbw.shape), const((1, d_b)),
            const((1, d_b)), const((1, d_b)), const((1, d_a)), const((1, d_b)), const(wout.shape),
            const((1, d)), const(wrt.shape), const((n_exp, 1)),
        ],
        out_specs=(
            pl.BlockSpec((nb, ts, d), lambda i, j: (i, j, 0)),
            pl.BlockSpec((m, d // 2), lambda i, j: (i * (seq // ts) + j, 0)),
            pl.BlockSpec((TOP_K, m), tok_map),
            pl.BlockSpec((TOP_K, m), tok_map),
            pl.BlockSpec((TOP_K, m), tok_map),
            pl.BlockSpec((n_exp, 1), lambda i, j: (0, 0)),
            pl.BlockSpec((nb, CONV_A_W - 1, d_a), lambda i, j: (i, 0, 0)),
            pl.BlockSpec((nb, CONV_B_W - 1, d_b), lambda i, j: (i, 0, 0)),
        ),
        out_shape=out_shape,
        scratch_shapes=[
            pltpu.VMEM((nb, HIST_A_ROWS + ts, d_a), jnp.float32),
            pltpu.VMEM((nb, HIST_B_ROWS + ts, d_b), jnp.float32),
            pltpu.VMEM((nb, ts, d_b), jnp.float32),
            pltpu.VMEM((n_exp, 1), jnp.float32),
        ],
        compiler_params=pltpu.CompilerParams(
            dimension_semantics=("arbitrary", "arbitrary"), vmem_limit_bytes=VMEM_LIMIT),
        name="mixer",
    )(x, hist_a, hist_b, cnt0, tri, gmix, win, caw, cbw, cbb, lng, lnb, ga, gb, wout, gffn, wrt, br)


def _positions_kernel(start_ref, *refs):
    n_parts = len(refs) // 3
    for p in range(n_parts):
        idx = refs[2 * p][...]
        pos = refs[2 * p + 1][...]
        for e in range(start_ref.shape[0]):
            pos = pos + jnp.where(idx == e, start_ref[e], 0)
        refs[2 * n_parts + p][...] = pos


def _positions(pad_start, parts):
    n_0 = parts[0][0].shape[1]
    tile = min(n_0, POS_TILE)
    specs = [pl.BlockSpec((TOP_K, tile), lambda i: (0, i))]
    specs += [pl.BlockSpec((TOP_K, idx.shape[1]), lambda i: (0, 0)) for idx, _ in parts[1:]]
    in_specs = [pl.BlockSpec(memory_space=pltpu.SMEM)]
    args = [pad_start]
    for spec, (idx, rank) in zip(specs, parts):
        in_specs += [spec, spec]
        args += [idx, rank]
    out = pl.pallas_call(
        _positions_kernel,
        grid=(n_0 // tile,),
        in_specs=in_specs,
        out_specs=tuple(specs),
        out_shape=tuple(jax.ShapeDtypeStruct(idx.shape, jnp.int32) for idx, _ in parts),
        compiler_params=pltpu.CompilerParams(dimension_semantics=("arbitrary",)),
        name="positions",
    )(*args)
    return list(out)


def _sc_mesh():
    return plsc.VectorSubcoreMesh(core_axis_name="c", subcore_axis_name="s",
                                  num_cores=SC_CORES, num_subcores=SC_SUBCORES)


def _sc_worker_id():
    return lax.axis_index("s") * SC_CORES + lax.axis_index("c")


def _sc_chunks(n_tok, wid):
    n_chunks = n_tok // SC_CHUNK
    assert n_tok % SC_CHUNK == 0
    if n_chunks % SC_WORKERS == 0:
        per_w = n_chunks // SC_WORKERS
        return None, per_w, lambda i: (wid * per_w + i) * SC_CHUNK
    assert n_chunks <= SC_WORKERS
    return wid < n_chunks, 1, lambda i: wid * SC_CHUNK


def _sc_for_each_chunk(n_tok, fn):
    wid = _sc_worker_id()
    pred, trips, base = _sc_chunks(n_tok, wid)

    def run():
        @pl.loop(0, trips)
        def _(i):
            fn(pl.multiple_of(base(i), SC_CHUNK))

    if pred is None:
        run()
    else:
        pl.when(pred)(run)


def _sc_dispatch(parts, n_rows):
    half = parts[0][1].shape[1]
    n_parts = len(parts)

    @functools.partial(
        pl.kernel, mesh=_sc_mesh(),
        out_type=jax.ShapeDtypeStruct((n_rows, half), jnp.uint32),
        scratch_types=[pltpu.VMEM((TOP_K, SC_CHUNK), jnp.int32),
                       pltpu.VMEM((SC_CHUNK, half), jnp.uint32),
                       pltpu.SemaphoreType.DMA],
        compiler_params=pltpu.CompilerParams(use_tc_tiling_on_sc=True),
        name="sc_dispatch")
    def k(*refs):
        xs_hbm, idx_v, rows_v, sem = refs[2 * n_parts:]

        def move(pos_hbm, src_hbm):
            def fn(base):
                pltpu.sync_copy(pos_hbm.at[:, pl.ds(base, SC_CHUNK)], idx_v)
                pltpu.sync_copy(src_hbm.at[pl.ds(base, SC_CHUNK)], rows_v)
                copies = [pltpu.async_copy(rows_v, xs_hbm.at[idx_v.at[j]], sem) for j in range(TOP_K)]
                for c in copies:
                    c.wait()
            return fn

        for p, (_, hnp) in enumerate(parts):
            _sc_for_each_chunk(hnp.shape[0], move(refs[2 * p], refs[2 * p + 1]))

    return k(*[a for part in parts for a in part])


def _sc_gather(pos_parts, rows):
    half = rows.shape[1]
    n_parts = len(pos_parts)

    @functools.partial(
        pl.kernel, mesh=_sc_mesh(),
        out_type=tuple(jax.ShapeDtypeStruct((TOP_K, p.shape[1], half), jnp.uint32) for p in pos_parts),
        scratch_types=[pltpu.VMEM((TOP_K, SC_CHUNK), jnp.int32),
                       pltpu.VMEM((SC_CHUNK, half), jnp.uint32),
                       pltpu.SemaphoreType.DMA],
        compiler_params=pltpu.CompilerParams(use_tc_tiling_on_sc=True),
        name="sc_gather")
    def k(*refs):
        pos_hbms, rows_hbm = refs[:n_parts], refs[n_parts]
        out_hbms = refs[n_parts + 1:2 * n_parts + 1]
        idx_v, buf_v, sem = refs[2 * n_parts + 1:]

        def move(pos_hbm, out_hbm):
            def fn(base):
                pltpu.sync_copy(pos_hbm.at[:, pl.ds(base, SC_CHUNK)], idx_v)
                for j in range(TOP_K):
                    pltpu.async_copy(rows_hbm.at[idx_v.at[j]], buf_v, sem).wait()
                    pltpu.sync_copy(buf_v, out_hbm.at[j, pl.ds(base, SC_CHUNK)])
            return fn

        for pos_hbm, out_hbm, p in zip(pos_hbms, out_hbms, pos_parts):
            _sc_for_each_chunk(p.shape[1], move(pos_hbm, out_hbm))

    out = k(*pos_parts, rows)
    return out if isinstance(out, (tuple, list)) else (out,)


def _experts_kernel(blk_exp_ref, first_ref, slot_ref, next_ref, n_active_ref,
                    xs_ref, wgu_hbm, bgu_ref, wd_hbm, bd_ref, out_ref,
                    wgu_f32, wd_f32, wgu_bf, wd_bf, sems):
    d_ff = wd_bf.shape[0]

    def fetch(e, sl):
        return (pltpu.make_async_copy(wgu_hbm.at[e], wgu_f32.at[sl], sems.at[0, sl]),
                pltpu.make_async_copy(wd_hbm.at[e], wd_f32.at[sl], sems.at[1, sl]))

    def load_weights_if_first(i):
        sl = slot_ref[i]
        e = blk_exp_ref[i]

        @pl.when(first_ref[i] == 1)
        def _():
            @pl.when(i == 0)
            def _():
                for c in fetch(e, sl):
                    c.start()

            for c in fetch(e, sl):
                c.wait()
            wgu_bf[...] = wgu_f32[sl].astype(jnp.bfloat16)
            wd_bf[...] = wd_f32[sl].astype(jnp.bfloat16)

            @pl.when(next_ref[i] >= 0)
            def _():
                for c in fetch(next_ref[i], 1 - sl):
                    c.start()

    def compute(rows, e):
        x = _unpack_bf16_pairs(xs_ref[rows, :]).astype(jnp.bfloat16)
        gu = jnp.dot(x, wgu_bf[...], preferred_element_type=jnp.float32) + bgu_ref[e]
        g = jnp.minimum(gu[:, :d_ff], SWIGLU_LIMIT)
        u = jnp.clip(gu[:, d_ff:], -SWIGLU_LIMIT, SWIGLU_LIMIT)
        act = g * jax.nn.sigmoid(SWIGLU_ALPHA * g) * (u + 1.0)
        o = jnp.dot(act.astype(jnp.bfloat16), wd_bf[...], preferred_element_type=jnp.float32) + bd_ref[e]
        out_ref[rows, :] = _pack_bf16_pairs(o)

    i0 = pl.program_id(0) * BLOCKS_PER_STEP
    n_active = n_active_ref[0]
    last = i0 + BLOCKS_PER_STEP - 1
    same = last < n_active
    for sub in range(1, BLOCKS_PER_STEP):
        same = same & (blk_exp_ref[i0 + sub] == blk_exp_ref[i0])

    @pl.when(same)
    def _():
        load_weights_if_first(i0)
        compute(slice(0, BLOCKS_PER_STEP * MOE_BLOCK), blk_exp_ref[i0])

    @pl.when(jnp.logical_not(same))
    def _():
        for sub in range(BLOCKS_PER_STEP):
            i = i0 + sub
            rows = slice(sub * MOE_BLOCK, (sub + 1) * MOE_BLOCK)

            @pl.when(i < n_active)
            def _():
                load_weights_if_first(i)
                compute(rows, blk_exp_ref[i])

            @pl.when(i >= n_active)
            def _():
                out_ref[rows, :] = jnp.zeros((MOE_BLOCK, out_ref.shape[1]), out_ref.dtype)


def _experts(plan, xs, wgu, bgu, wd, bd):
    blk_exp, first, slot, nxt, n_active = plan
    n_rows, half = xs.shape
    n_exp, d, two_f = wgu.shape
    d_ff = wd.shape[1]
    n_blocks = n_rows // MOE_BLOCK
    step_rows = MOE_BLOCK * BLOCKS_PER_STEP
    assert n_rows % step_rows == 0
    grid_spec = pltpu.PrefetchScalarGridSpec(
        num_scalar_prefetch=5,
        grid=(n_blocks // BLOCKS_PER_STEP,),
        in_specs=[
            pl.BlockSpec((step_rows, half), lambda i, *_: (i, 0)),
            pl.BlockSpec(memory_space=pl.ANY),
            pl.BlockSpec((n_exp, 1, two_f), lambda i, *_: (0, 0, 0)),
            pl.BlockSpec(memory_space=pl.ANY),
            pl.BlockSpec((n_exp, 1, d), lambda i, *_: (0, 0, 0)),
        ],
        out_specs=pl.BlockSpec((step_rows, half), lambda i, *_: (i, 0)),
        scratch_shapes=[
            pltpu.VMEM((2, d, two_f), jnp.float32),
            pltpu.VMEM((2, d_ff, d), jnp.float32),
            pltpu.VMEM((d, two_f), jnp.bfloat16),
            pltpu.VMEM((d_ff, d), jnp.bfloat16),
            pltpu.SemaphoreType.DMA((2, 2)),
        ],
    )
    return pl.pallas_call(
        _experts_kernel,
        grid_spec=grid_spec,
        out_shape=jax.ShapeDtypeStruct((n_rows, half), jnp.uint32),
        compiler_params=pltpu.CompilerParams(
            dimension_semantics=("arbitrary",), vmem_limit_bytes=VMEM_LIMIT),
        name="experts",
    )(blk_exp, first, slot, nxt, n_active, xs, wgu, bgu.reshape(n_exp, 1, two_f), wd,
      bd.reshape(n_exp, 1, d))


def _combine_kernel(rows_ref, gate_ref, h_ref, gfin_ref, *rest):
    y_ref = rest[-1]
    acc = h_ref[...]
    for k in range(TOP_K):
        acc = acc + gate_ref[:, k:k + 1] * _unpack_bf16_pairs(rows_ref[k])
    y_ref[...] = _rms(acc, gfin_ref[...])


def _combine(rows4, gate_t, h2d, gfin, *, tok0=0, y_prev=None, y_rows=None, y_tok0=0):
    d = h2d.shape[1]
    n, half = rows4.shape[1:]
    n_tok = h2d.shape[0] if y_rows is None else y_rows
    tile = min(n, 256)
    t0 = tok0 // tile
    y0 = y_tok0 // tile
    in_specs = [
        pl.BlockSpec((TOP_K, tile, half), lambda i: (0, i, 0)),
        pl.BlockSpec((tile, TOP_K), lambda i: (t0 + i, 0)),
        pl.BlockSpec((tile, d), lambda i: (t0 + i, 0)),
        pl.BlockSpec((1, d), lambda i: (0, 0)),
    ]
    args = [rows4, gate_t, h2d, gfin]
    aliases = {}
    if y_prev is not None:
        in_specs.append(pl.BlockSpec(memory_space=pl.ANY))
        args.append(y_prev)
        aliases = {4: 0}
    return pl.pallas_call(
        _combine_kernel,
        grid=(n // tile,),
        in_specs=in_specs,
        out_specs=pl.BlockSpec((tile, d), lambda i: (y0 + i, 0)),
        out_shape=jax.ShapeDtypeStruct((n_tok, d), jnp.float32),
        input_output_aliases=aliases,
        compiler_params=pltpu.CompilerParams(dimension_semantics=("arbitrary",)),
        name="combine",
    )(*args)


def _pad_hist(hist, rows):
    return jnp.pad(hist, ((0, 0), (rows - hist.shape[1], 0), (0, 0)))


def kernel(x_prompt, x_sample, state_conv_a, state_conv_b, norm_mix_g, w_in, conv_a_w, conv_b_w, conv_b_b, conv_ln_g, conv_ln_b, out_norm_a_g, out_norm_b_g, w_out, norm_ffn_g, w_router, b_router, w_gate_up, b_gate_up, w_down, b_down, final_norm_g):
    depth = w_in.shape[0]
    assert depth == 1, "single-layer trunk"
    bf16 = jnp.bfloat16
    bsz, seq, d = x_prompt.shape
    dec_b, dec_s, _ = x_sample.shape
    d_a, d_b = conv_a_w.shape[-1], conv_b_w.shape[-1]
    n_exp = w_router.shape[-1]

    params = (norm_mix_g[0][None], w_in[0].astype(bf16), conv_a_w[0], conv_b_w[0], conv_b_b[0][None],
              conv_ln_g[0][None], conv_ln_b[0][None], out_norm_a_g[0][None], out_norm_b_g[0][None],
              w_out[0].astype(bf16), norm_ffn_g[0][None], w_router[0].T.astype(bf16),
              b_router[0][:, None])

    zero_a = jnp.zeros((bsz, HIST_A_ROWS, d_a), jnp.float32)
    zero_b = jnp.zeros((bsz, HIST_B_ROWS, d_b), jnp.float32)
    cnt0 = jnp.zeros((n_exp, 1), jnp.float32)
    gfin = final_norm_g[None]
    ts = min(seq, MIXER_ROWS)
    n_p = bsz * seq

    n_groups = MOE_GROUPS if bsz % MOE_GROUPS == 0 else 1
    per_b = bsz // n_groups
    groups = []
    for g in range(n_groups):
        h, hnp, idx, gate, rank, cnt, na, nb_ = _mixer(
            x_prompt, zero_a, zero_b, cnt0, params, nb=1, ts=ts, b0=g * per_b, bsz=per_b)
        parts = [dict(h=h.reshape(per_b * seq, d), hnp=hnp, idx=idx, gate_t=gate.T, rank=rank)]
        states = [(na, nb_)]
        if g == n_groups - 1:
            h, hnp, idx, gate, rank, cnt, na_s, nb_s = _mixer(
                x_sample, _pad_hist(state_conv_a[0], HIST_A_ROWS), _pad_hist(state_conv_b[0], HIST_B_ROWS),
                cnt, params, nb=dec_b, ts=dec_s)
            parts.append(dict(h=h.reshape(dec_b * dec_s, d), hnp=hnp, idx=idx, gate_t=gate.T, rank=rank))
        n_pairs = sum(p["hnp"].shape[0] for p in parts) * TOP_K
        plan, pad_start, n_blocks = _block_plan(cnt[:, 0].astype(jnp.int32), n_pairs, n_exp)
        pos = _positions(pad_start, [(p["idx"], p["rank"]) for p in parts])
        xs = _sc_dispatch([(q, p["hnp"]) for q, p in zip(pos, parts)], n_blocks * MOE_BLOCK)
        groups.append(dict(parts=parts, pos=pos, plan=plan, xs=xs, states=states))

    for grp in groups:
        grp["rows"] = _experts(grp["plan"], grp["xs"], w_gate_up[0], b_gate_up[0], w_down[0], b_down[0])

    y_p = None
    for g, grp in enumerate(groups):
        main, pos0 = grp["parts"][0], grp["pos"][0]
        n_g = main["h"].shape[0]
        n_chunks = COMBINE_CHUNKS if n_g % (COMBINE_CHUNKS * SC_CHUNK * SC_WORKERS) == 0 else 1
        per = n_g // n_chunks
        for c in range(n_chunks):
            extra = grp["pos"][1:] if c == 0 else []
            got = _sc_gather([pos0[:, c * per:(c + 1) * per]] + extra, grp["rows"])
            if extra:
                smp = grp["parts"][1]
                y_s = _combine(got[1], smp["gate_t"], smp["h"], gfin).reshape(dec_b, dec_s, d)
            y_p = _combine(got[0], main["gate_t"], main["h"], gfin, tok0=c * per,
                           y_prev=y_p, y_rows=n_p, y_tok0=g * n_g + c * per)
    y_p = y_p.reshape(bsz, seq, d)

    na_p = jnp.concatenate([grp["states"][0][0] for grp in groups], axis=0)
    nb_p = jnp.concatenate([grp["states"][0][1] for grp in groups], axis=0)
    return (y_p, y_s, na_p[None], nb_p[None], na_s[None], nb_s[None])


def _block_plan(counts, n_pairs, n_exp):
    padded = (counts + MOE_BLOCK - 1) // MOE_BLOCK * MOE_BLOCK
    pad_end = jnp.cumsum(padded)
    pad_start = pad_end - padded
    n_blocks = -(-n_pairs // MOE_BLOCK) + n_exp
    n_blocks = -(-n_blocks // BLOCKS_PER_STEP) * BLOCKS_PER_STEP
    blk_start = jnp.arange(n_blocks, dtype=jnp.int32) * MOE_BLOCK
    blk_exp = jnp.minimum(jnp.sum(blk_start[:, None] >= pad_end[None, :], axis=1),
                          n_exp - 1).astype(jnp.int32)
    n_active = (pad_end[-1:] // MOE_BLOCK).astype(jnp.int32)
    e_ids = jnp.arange(n_exp, dtype=jnp.int32)
    has = padded > 0
    slot_e = (jnp.cumsum(has.astype(jnp.int32)) - 1) % 2
    later = jnp.where(has, e_ids, n_exp)
    next_e = jnp.concatenate([lax.cummin(later, reverse=True)[1:], jnp.full((1,), n_exp, jnp.int32)])
    next_e = jnp.where(next_e >= n_exp, -1, next_e)
    blk_onehot = blk_exp[:, None] == e_ids[None, :]

    def per_block(table):
        return jnp.sum(jnp.where(blk_onehot, table[None, :], 0), axis=1).astype(jnp.int32)

    blk_first = ((blk_start == per_block(pad_start)) & (blk_start < pad_end[-1])).astype(jnp.int32)
    plan = (blk_exp, blk_first, per_block(slot_e), per_block(next_e), n_active)
    return plan, pad_start.astype(jnp.int32), n_blocks
```

```python
import functools

import jax
import jax.numpy as jnp
from jax import lax
from jax.experimental import pallas as pl
from jax.experimental.pallas import tpu as pltpu
from jax.experimental.pallas import tpu_sc as plsc

SC_CORES = 2
SC_SUBCORES = 16
SC_WORKERS = SC_CORES * SC_SUBCORES
SC_CHUNK = 128
CONV_A_W = 3
CONV_B_W = 31
TOP_K = 4
SWIGLU_LIMIT = 7.0
SWIGLU_ALPHA = 1.702
NORM_EPS = 1e-5
MOE_BLOCK = 256
BLOCKS_PER_STEP = 4
SUBLANES = 8
LANES = 128
CONV_ROW_CHUNK = 128
HIST_A_ROWS = 8
HIST_B_ROWS = 32
MIXER_ROWS = 512
PROJ_SECTION = 256
POS_TILE = 4096
MOE_GROUPS = 2
COMBINE_CHUNKS = 2
VMEM_LIMIT = 56 * 1024 * 1024


def _rms(x, g):
    return x * lax.rsqrt(jnp.mean(x * x, axis=-1, keepdims=True) + NORM_EPS) * g


def _zero_tile_from(v):
    bits = pltpu.bitcast(v[0:SUBLANES, 0:LANES], jnp.uint32)
    return pltpu.bitcast((bits >> 16) >> 16, jnp.float32)


def _pack_bf16_pairs(a):
    w = a.shape[-1] // 2
    lo = pltpu.bitcast(a[:, :w].astype(jnp.bfloat16).astype(jnp.float32), jnp.uint32)
    hi = pltpu.bitcast(a[:, w:].astype(jnp.bfloat16).astype(jnp.float32), jnp.uint32)
    return (lo >> 16) | (hi & jnp.uint32(0xFFFF0000))


def _unpack_bf16_pairs(wd):
    lo = pltpu.bitcast(wd << 16, jnp.float32)
    hi = pltpu.bitcast(wd & jnp.uint32(0xFFFF0000), jnp.float32)
    return jnp.concatenate([lo, hi], axis=-1)


def _conv_b_chunk(ub_buf, cbw_ref, n_i, r0, rc, c0):
    cs = slice(c0, c0 + LANES)
    lead = HIST_B_ROWS - (CONV_B_W - 1)
    acc = None
    for b in range(SUBLANES):
        rows = rc if b == 0 else rc + SUBLANES
        q = None
        for a in range((HIST_B_ROWS + SUBLANES) // SUBLANES):
            k = SUBLANES * a + b - lead
            if k < 0 or k >= CONV_B_W:
                continue
            start = r0 + SUBLANES * a
            term = cbw_ref[k:k + 1, cs] * ub_buf[n_i, start:start + rows, cs]
            q = term if q is None else q + term
        part = q if b == 0 else q[b:b + rc]
        acc = part if acc is None else acc + part
    return acc


def _mixer_kernel(x_ref, hista_ref, histb_ref, cnt0_ref, tri_ref,
                  gmix_ref, win_ref, caw_ref, cbw_ref, cbb_ref, lng_ref, lnb_ref,
                  ga_ref, gb_ref, wout_ref, gffn_ref, wrt_ref, br_ref,
                  h_ref, hnp_ref, idx_ref, gate_ref, rank_ref, cnt_ref, newa_ref, newb_ref,
                  ua_buf, ub_buf, cb_buf, cnt_acc, *, nb, ts, d_a, d_b, n_exp):
    b = pl.program_id(0)
    s = pl.program_id(1)
    m = nb * ts
    d = x_ref.shape[-1]

    @pl.when(s == 0)
    def _():
        ua_buf[:, 0:HIST_A_ROWS, :] = hista_ref[...]
        ub_buf[:, 0:HIST_B_ROWS, :] = histb_ref[...]

    @pl.when(s != 0)
    def _():
        ua_buf[:, 0:HIST_A_ROWS, :] = ua_buf[:, ts:ts + HIST_A_ROWS, :]
        ub_buf[:, 0:HIST_B_ROWS, :] = ub_buf[:, ts:ts + HIST_B_ROWS, :]

    @pl.when((b == 0) & (s == 0))
    def _():
        cnt_acc[...] = cnt0_ref[...]

    x = x_ref[...].reshape(m, d)
    n = _rms(x, gmix_ref[...]).astype(jnp.bfloat16)
    def proj(c0, width):
        return jnp.dot(n, win_ref[:, c0:c0 + width], preferred_element_type=jnp.float32)

    sec = min(d_b, PROJ_SECTION)
    for c0 in range(0, d_b, sec):
        v_b = proj(3 * d_a + c0, sec)
        g_b = proj(3 * d_a + d_b + c0, sec)
        ub_buf[:, HIST_B_ROWS:HIST_B_ROWS + ts, c0:c0 + sec] = (v_b * jax.nn.sigmoid(g_b)).reshape(nb, ts, sec)

    gate_c = proj(d_a, d_a)
    xt = proj(2 * d_a, d_a)
    gate_b = proj(0, d_a)
    ua_buf[:, HIST_A_ROWS:HIST_A_ROWS + ts, :] = (gate_c * xt).reshape(nb, ts, d_a)
    conv_a = jnp.zeros((nb, ts, d_a), jnp.float32)
    for k in range(CONV_A_W):
        off = HIST_A_ROWS - (CONV_A_W - 1) + k
        conv_a = conv_a + caw_ref[k:k + 1, :] * ua_buf[:, off:off + ts, :]
    y_a = gate_b * conv_a.reshape(m, d_a)

    mix_a = jnp.dot(_rms(y_a, ga_ref[...]).astype(jnp.bfloat16), wout_ref[0:d_a, :],
                    preferred_element_type=jnp.float32)

    rc = min(ts, CONV_ROW_CHUNK)
    chunks = [(n_i, r0, c0) for n_i in range(nb) for r0 in range(0, ts, rc) for c0 in range(0, d_b, LANES)]
    anchors = {len(chunks) * 2 // 8: gate_c, len(chunks) * 3 // 8: xt, len(chunks) * 5 // 8: gate_b,
               len(chunks) - 1: mix_a}
    for ci, (n_i, r0, c0) in enumerate(chunks):
        acc = _conv_b_chunk(ub_buf, cbw_ref, n_i, r0, rc, c0)
        if ci in anchors:
            acc = (acc.reshape(rc // SUBLANES, SUBLANES, LANES) + _zero_tile_from(anchors[ci])[None]
                   ).reshape(rc, LANES)
        cb_buf[n_i, r0:r0 + rc, c0:c0 + LANES] = acc
    cb = cb_buf[...].reshape(m, d_b) + cbb_ref[...]
    mu = jnp.mean(cb, axis=-1, keepdims=True)
    xc = cb - mu
    var = jnp.mean(xc * xc, axis=-1, keepdims=True)
    ln = xc * lax.rsqrt(var + NORM_EPS) * lng_ref[...] + lnb_ref[...]
    y_b = ln * jax.nn.sigmoid(ln)

    mix_b = jnp.dot(_rms(y_b, gb_ref[...]).astype(jnp.bfloat16), wout_ref[d_a:d_a + d_b, :],
                    preferred_element_type=jnp.float32)
    h = x + (mix_a + mix_b)
    h_ref[...] = h.reshape(nb, ts, d)

    hn = _rms(h, gffn_ref[...])
    hnp_ref[...] = _pack_bf16_pairs(hn)

    logits = lax.dot_general(wrt_ref[...], hn.astype(jnp.bfloat16), (((1,), (1,)), ((), ())),
                             preferred_element_type=jnp.float32) + br_ref[...]
    e_iota = lax.broadcasted_iota(jnp.int32, (n_exp, m), 0)
    cur = logits
    vals, idxs, sels = [], [], []
    for _ in range(TOP_K):
        mx = jnp.max(cur, axis=0, keepdims=True)
        ix = jnp.min(jnp.where(cur == mx, e_iota, n_exp), axis=0, keepdims=True)
        sel = e_iota == ix
        vals.append(mx)
        idxs.append(ix)
        sels.append(sel)
        cur = jnp.where(sel, -jnp.inf, cur)
    exps = [jnp.exp(v - vals[0]) for v in vals]
    denom = exps[0] + exps[1] + exps[2] + exps[3]
    gates = [e / denom for e in exps]

    onehot = jnp.zeros((n_exp, m), jnp.float32)
    for sel in sels:
        onehot = onehot + sel.astype(jnp.float32)
    before = jnp.dot(onehot.astype(jnp.bfloat16), tri_ref[...],
                     preferred_element_type=jnp.float32) + cnt_acc[...]
    ranks = [jnp.sum(jnp.where(sel, before, 0.0), axis=0, keepdims=True) for sel in sels]
    new_cnt = cnt_acc[...] + jnp.sum(onehot, axis=1, keepdims=True)
    cnt_acc[...] = new_cnt
    cnt_ref[...] = new_cnt

    idx_ref[...] = jnp.concatenate(idxs, axis=0)
    gate_ref[...] = jnp.concatenate(gates, axis=0)
    rank_ref[...] = jnp.concatenate(ranks, axis=0).astype(jnp.int32)

    @pl.when(s == pl.num_programs(1) - 1)
    def _():
        newa_ref[...] = ua_buf[:, ts + HIST_A_ROWS - (CONV_A_W - 1):ts + HIST_A_ROWS, :]
        newb_ref[...] = ub_buf[:, ts + HIST_B_ROWS - (CONV_B_W - 1):ts + HIST_B_ROWS, :]


def _mixer(x, hist_a, hist_b, cnt0, params, *, nb, ts, b0=0, bsz=None):
    (gmix, win, caw, cbw, cbb, lng, lnb, ga, gb, wout, gffn, wrt, br) = params
    _, seq, d = x.shape
    bsz = x.shape[0] if bsz is None else bsz
    d_a, d_b = caw.shape[-1], cbw.shape[-1]
    n_exp = wrt.shape[0]
    m = nb * ts
    n_tok = bsz * seq
    g0 = b0 // nb
    grid = (bsz // nb, seq // ts)
    tri = (jnp.arange(m)[:, None] < jnp.arange(m)[None, :]).astype(jnp.bfloat16)

    def const(shape):
        return pl.BlockSpec(shape, lambda i, j: (0,) * len(shape))

    tok_map = lambda i, j: (0, i * (seq // ts) + j)
    kern = functools.partial(_mixer_kernel, nb=nb, ts=ts, d_a=d_a, d_b=d_b, n_exp=n_exp)
    out_shape = (
        jax.ShapeDtypeStruct((bsz, seq, d), jnp.float32),
        jax.ShapeDtypeStruct((n_tok, d // 2), jnp.uint32),
        jax.ShapeDtypeStruct((TOP_K, n_tok), jnp.int32),
        jax.ShapeDtypeStruct((TOP_K, n_tok), jnp.float32),
        jax.ShapeDtypeStruct((TOP_K, n_tok), jnp.int32),
        jax.ShapeDtypeStruct((n_exp, 1), jnp.float32),
        jax.ShapeDtypeStruct((bsz, CONV_A_W - 1, d_a), jnp.float32),
        jax.ShapeDtypeStruct((bsz, CONV_B_W - 1, d_b), jnp.float32),
    )
    return pl.pallas_call(
        kern,
        grid=grid,
        in_specs=[
            pl.BlockSpec((nb, ts, d), lambda i, j: (g0 + i, j, 0)),
            pl.BlockSpec((nb, HIST_A_ROWS, d_a), lambda i, j: (g0 + i, 0, 0)),
            pl.BlockSpec((nb, HIST_B_ROWS, d_b), lambda i, j: (g0 + i, 0, 0)),
            const((n_exp, 1)),
            const((m, m)),
            const((1, d)), const(win.shape), const(caw.shape), const(cbw.shape), const((1, d_b)),
            const((1, d_b)), const((1, d_b)), const((1, d_a)), const((1, d_b)), const(wout.shape),
            const((1, d)), const(wrt.shape), const((n_exp, 1)),
        ],
        out_specs=(
            pl.BlockSpec((nb, ts, d), lambda i, j: (i, j, 0)),
            pl.BlockSpec((m, d // 2), lambda i, j: (i * (seq // ts) + j, 0)),
            pl.BlockSpec((TOP_K, m), tok_map),
            pl.BlockSpec((TOP_K, m), tok_map),
            pl.BlockSpec((TOP_K, m), tok_map),
            pl.BlockSpec((n_exp, 1), lambda i, j: (0, 0)),
            pl.BlockSpec((nb, CONV_A_W - 1, d_a), lambda i, j: (i, 0, 0)),
            pl.BlockSpec((nb, CONV_B_W - 1, d_b), lambda i, j: (i, 0, 0)),
        ),
        out_shape=out_shape,
        scratch_shapes=[
            pltpu.VMEM((nb, HIST_A_ROWS + ts, d_a), jnp.float32),
            pltpu.VMEM((nb, HIST_B_ROWS + ts, d_b), jnp.float32),
            pltpu.VMEM((nb, ts, d_b), jnp.float32),
            pltpu.VMEM((n_exp, 1), jnp.float32),
        ],
        compiler_params=pltpu.CompilerParams(
            dimension_semantics=("arbitrary", "arbitrary"), vmem_limit_bytes=VMEM_LIMIT),
        name="mixer",
    )(x, hist_a, hist_b, cnt0, tri, gmix, win, caw, cbw, cbb, lng, lnb, ga, gb, wout, gffn, wrt, br)


def _positions_kernel(start_ref, *refs):
    n_parts = len(refs) // 3
    for p in range(n_parts):
        idx = refs[2 * p][...]
        pos = refs[2 * p + 1][...]
        for e in range(start_ref.shape[0]):
            pos = pos + jnp.where(idx == e, start_ref[e], 0)
        refs[2 * n_parts + p][...] = pos


def _positions(pad_start, parts):
    n_0 = parts[0][0].shape[1]
    tile = min(n_0, POS_TILE)
    specs = [pl.BlockSpec((TOP_K, tile), lambda i: (0, i))]
    specs += [pl.BlockSpec((TOP_K, idx.shape[1]), lambda i: (0, 0)) for idx, _ in parts[1:]]
    in_specs = [pl.BlockSpec(memory_space=pltpu.SMEM)]
    args = [pad_start]
    for spec, (idx, rank) in zip(specs, parts):
        in_specs += [spec, spec]
        args += [idx, rank]
    out = pl.pallas_call(
        _positions_kernel,
        grid=(n_0 // tile,),
        in_specs=in_specs,
        out_specs=tuple(specs),
        out_shape=tuple(jax.ShapeDtypeStruct(idx.shape, jnp.int32) for idx, _ in parts),
        compiler_params=pltpu.CompilerParams(dimension_semantics=("arbitrary",)),
        name="positions",
    )(*args)
    return list(out)


def _sc_mesh():
    return plsc.VectorSubcoreMesh(core_axis_name="c", subcore_axis_name="s",
                                  num_cores=SC_CORES, num_subcores=SC_SUBCORES)


def _sc_worker_id():
    return lax.axis_index("s") * SC_CORES + lax.axis_index("c")


def _sc_chunks(n_tok, wid):
    n_chunks = n_tok // SC_CHUNK
    assert n_tok % SC_CHUNK == 0
    if n_chunks % SC_WORKERS == 0:
        per_w = n_chunks // SC_WORKERS
        return None, per_w, lambda i: (wid * per_w + i) * SC_CHUNK
    assert n_chunks <= SC_WORKERS
    return wid < n_chunks, 1, lambda i: wid * SC_CHUNK


def _sc_for_each_chunk(n_tok, fn):
    wid = _sc_worker_id()
    pred, trips, base = _sc_chunks(n_tok, wid)

    def run():
        @pl.loop(0, trips)
        def _(i):
            fn(pl.multiple_of(base(i), SC_CHUNK))

    if pred is None:
        run()
    else:
        pl.when(pred)(run)


def _sc_dispatch(parts, n_rows):
    half = parts[0][1].shape[1]
    n_parts = len(parts)

    @functools.partial(
        pl.kernel, mesh=_sc_mesh(),
        out_type=jax.ShapeDtypeStruct((n_rows, half), jnp.uint32),
        scratch_types=[pltpu.VMEM((TOP_K, SC_CHUNK), jnp.int32),
                       pltpu.VMEM((SC_CHUNK, half), jnp.uint32),
                       pltpu.SemaphoreType.DMA],
        compiler_params=pltpu.CompilerParams(use_tc_tiling_on_sc=True),
        name="sc_dispatch")
    def k(*refs):
        xs_hbm, idx_v, rows_v, sem = refs[2 * n_parts:]

        def move(pos_hbm, src_hbm):
            def fn(base):
                pltpu.sync_copy(pos_hbm.at[:, pl.ds(base, SC_CHUNK)], idx_v)
                pltpu.sync_copy(src_hbm.at[pl.ds(base, SC_CHUNK)], rows_v)
                copies = [pltpu.async_copy(rows_v, xs_hbm.at[idx_v.at[j]], sem) for j in range(TOP_K)]
                for c in copies:
                    c.wait()
            return fn

        for p, (_, hnp) in enumerate(parts):
            _sc_for_each_chunk(hnp.shape[0], move(refs[2 * p], refs[2 * p + 1]))

    return k(*[a for part in parts for a in part])


def _sc_gather(pos_parts, rows):
    half = rows.shape[1]
    n_parts = len(pos_parts)

    @functools.partial(
        pl.kernel, mesh=_sc_mesh(),
        out_type=tuple(jax.ShapeDtypeStruct((TOP_K, p.shape[1], half), jnp.uint32) for p in pos_parts),
        scratch_types=[pltpu.VMEM((TOP_K, SC_CHUNK), jnp.int32),
                       pltpu.VMEM((SC_CHUNK, half), jnp.uint32),
                       pltpu.SemaphoreType.DMA],
        compiler_params=pltpu.CompilerParams(use_tc_tiling_on_sc=True),
        name="sc_gather")
    def k(*refs):
        pos_hbms, rows_hbm = refs[:n_parts], refs[n_parts]
        out_hbms = refs[n_parts + 1:2 * n_parts + 1]
        idx_v, buf_v, sem = refs[2 * n_parts + 1:]

        def move(pos_hbm, out_hbm):
            def fn(base):
                pltpu.sync_copy(pos_hbm.at[:, pl.ds(base, SC_CHUNK)], idx_v)
                for j in range(TOP_K):
                    pltpu.async_copy(rows_hbm.at[idx_v.at[j]], buf_v, sem).wait()
                    pltpu.sync_copy(buf_v, out_hbm.at[j, pl.ds(base, SC_CHUNK)])
            return fn

        for pos_hbm, out_hbm, p in zip(pos_hbms, out_hbms, pos_parts):
            _sc_for_each_chunk(p.shape[1], move(pos_hbm, out_hbm))

    out = k(*pos_parts, rows)
    return out if isinstance(out, (tuple, list)) else (out,)


def _experts_kernel(blk_exp_ref, first_ref, slot_ref, next_ref, n_active_ref,
                    xs_ref, wgu_hbm, bgu_ref, wd_hbm, bd_ref, out_ref,
                    wgu_f32, wd_f32, wgu_bf, wd_bf, sems):
    d_ff = wd_bf.shape[0]

    def fetch(e, sl):
        return (pltpu.make_async_copy(wgu_hbm.at[e], wgu_f32.at[sl], sems.at[0, sl]),
                pltpu.make_async_copy(wd_hbm.at[e], wd_f32.at[sl], sems.at[1, sl]))

    def load_weights_if_first(i):
        sl = slot_ref[i]
        e = blk_exp_ref[i]

        @pl.when(first_ref[i] == 1)
        def _():
            @pl.when(i == 0)
            def _():
                for c in fetch(e, sl):
                    c.start()

            for c in fetch(e, sl):
                c.wait()
            wgu_bf[...] = wgu_f32[sl].astype(jnp.bfloat16)
            wd_bf[...] = wd_f32[sl].astype(jnp.bfloat16)

            @pl.when(next_ref[i] >= 0)
            def _():
                for c in fetch(next_ref[i], 1 - sl):
                    c.start()

    def compute(rows, e):
        x = _unpack_bf16_pairs(xs_ref[rows, :]).astype(jnp.bfloat16)
        gu = jnp.dot(x, wgu_bf[...], preferred_element_type=jnp.float32) + bgu_ref[e]
        g = jnp.minimum(gu[:, :d_ff], SWIGLU_LIMIT)
        u = jnp.clip(gu[:, d_ff:], -SWIGLU_LIMIT, SWIGLU_LIMIT)
        act = g * jax.nn.sigmoid(SWIGLU_ALPHA * g) * (u + 1.0)
        o = jnp.dot(act.astype(jnp.bfloat16), wd_bf[...], preferred_element_type=jnp.float32) + bd_ref[e]
        out_ref[rows, :] = _pack_bf16_pairs(o)

    i0 = pl.program_id(0) * BLOCKS_PER_STEP
    n_active = n_active_ref[0]

    def run(sub0, n_sub):
        first = i0 + sub0
        rows = slice(sub0 * MOE_BLOCK, (sub0 + n_sub) * MOE_BLOCK)
        if n_sub == 1:
            @pl.when(first < n_active)
            def _():
                load_weights_if_first(first)
                compute(rows, blk_exp_ref[first])

            @pl.when(first >= n_active)
            def _():
                out_ref[rows, :] = jnp.zeros((MOE_BLOCK, out_ref.shape[1]), out_ref.dtype)
            return

        same = first + n_sub - 1 < n_active
        for sub in range(1, n_sub):
            same = same & (blk_exp_ref[first + sub] == blk_exp_ref[first])

        @pl.when(same)
        def _():
            load_weights_if_first(first)
            compute(rows, blk_exp_ref[first])

        @pl.when(jnp.logical_not(same))
        def _():
            run(sub0, n_sub // 2)
            run(sub0 + n_sub // 2, n_sub // 2)

    run(0, BLOCKS_PER_STEP)


def _experts(plan, xs, wgu, bgu, wd, bd):
    blk_exp, first, slot, nxt, n_active = plan
    n_rows, half = xs.shape
    n_exp, d, two_f = wgu.shape
    d_ff = wd.shape[1]
    n_blocks = n_rows // MOE_BLOCK
    step_rows = MOE_BLOCK * BLOCKS_PER_STEP
    assert n_rows % step_rows == 0
    grid_spec = pltpu.PrefetchScalarGridSpec(
        num_scalar_prefetch=5,
        grid=(n_blocks // BLOCKS_PER_STEP,),
        in_specs=[
            pl.BlockSpec((step_rows, half), lambda i, *_: (i, 0)),
            pl.BlockSpec(memory_space=pl.ANY),
            pl.BlockSpec((n_exp, 1, two_f), lambda i, *_: (0, 0, 0)),
            pl.BlockSpec(memory_space=pl.ANY),
            pl.BlockSpec((n_exp, 1, d), lambda i, *_: (0, 0, 0)),
        ],
        out_specs=pl.BlockSpec((step_rows, half), lambda i, *_: (i, 0)),
        scratch_shapes=[
            pltpu.VMEM((2, d, two_f), jnp.float32),
            pltpu.VMEM((2, d_ff, d), jnp.float32),
            pltpu.VMEM((d, two_f), jnp.bfloat16),
            pltpu.VMEM((d_ff, d), jnp.bfloat16),
            pltpu.SemaphoreType.DMA((2, 2)),
        ],
    )
    return pl.pallas_call(
        _experts_kernel,
        grid_spec=grid_spec,
        out_shape=jax.ShapeDtypeStruct((n_rows, half), jnp.uint32),
        compiler_params=pltpu.CompilerParams(
            dimension_semantics=("arbitrary",), vmem_limit_bytes=VMEM_LIMIT),
        name="experts",
    )(blk_exp, first, slot, nxt, n_active, xs, wgu, bgu.reshape(n_exp, 1, two_f), wd,
      bd.reshape(n_exp, 1, d))


def _combine_kernel(rows_ref, gate_ref, h_ref, gfin_ref, *rest):
    y_ref = rest[-1]
    acc = h_ref[...]
    for k in range(TOP_K):
        acc = acc + gate_ref[:, k:k + 1] * _unpack_bf16_pairs(rows_ref[k])
    y_ref[...] = _rms(acc, gfin_ref[...])


def _combine(rows4, gate_t, h2d, gfin, *, tok0=0, y_prev=None, y_rows=None, y_tok0=0):
    d = h2d.shape[1]
    n, half = rows4.shape[1:]
    n_tok = h2d.shape[0] if y_rows is None else y_rows
    tile = min(n, 256)
    t0 = tok0 // tile
    y0 = y_tok0 // tile
    in_specs = [
        pl.BlockSpec((TOP_K, tile, half), lambda i: (0, i, 0)),
        pl.BlockSpec((tile, TOP_K), lambda i: (t0 + i, 0)),
        pl.BlockSpec((tile, d), lambda i: (t0 + i, 0)),
        pl.BlockSpec((1, d), lambda i: (0, 0)),
    ]
    args = [rows4, gate_t, h2d, gfin]
    aliases = {}
    if y_prev is not None:
        in_specs.append(pl.BlockSpec(memory_space=pl.ANY))
        args.append(y_prev)
        aliases = {4: 0}
    return pl.pallas_call(
        _combine_kernel,
        grid=(n // tile,),
        in_specs=in_specs,
        out_specs=pl.BlockSpec((tile, d), lambda i: (y0 + i, 0)),
        out_shape=jax.ShapeDtypeStruct((n_tok, d), jnp.float32),
        input_output_aliases=aliases,
        compiler_params=pltpu.CompilerParams(dimension_semantics=("arbitrary",)),
        name="combine",
    )(*args)


def _pad_hist(hist, rows):
    return jnp.pad(hist, ((0, 0), (rows - hist.shape[1], 0), (0, 0)))


def kernel(x_prompt, x_sample, state_conv_a, state_conv_b, norm_mix_g, w_in, conv_a_w, conv_b_w, conv_b_b, conv_ln_g, conv_ln_b, out_norm_a_g, out_norm_b_g, w_out, norm_ffn_g, w_router, b_router, w_gate_up, b_gate_up, w_down, b_down, final_norm_g):
    depth = w_in.shape[0]
    assert depth == 1, "single-layer trunk"
    bf16 = jnp.bfloat16
    bsz, seq, d = x_prompt.shape
    dec_b, dec_s, _ = x_sample.shape
    d_a, d_b = conv_a_w.shape[-1], conv_b_w.shape[-1]
    n_exp = w_router.shape[-1]

    params = (norm_mix_g[0][None], w_in[0].astype(bf16), conv_a_w[0], conv_b_w[0], conv_b_b[0][None],
              conv_ln_g[0][None], conv_ln_b[0][None], out_norm_a_g[0][None], out_norm_b_g[0][None],
              w_out[0].astype(bf16), norm_ffn_g[0][None], w_router[0].T.astype(bf16),
              b_router[0][:, None])

    zero_a = jnp.zeros((bsz, HIST_A_ROWS, d_a), jnp.float32)
    zero_b = jnp.zeros((bsz, HIST_B_ROWS, d_b), jnp.float32)
    cnt0 = jnp.zeros((n_exp, 1), jnp.float32)
    gfin = final_norm_g[None]
    ts = min(seq, MIXER_ROWS)
    n_p = bsz * seq

    n_groups = MOE_GROUPS if bsz % MOE_GROUPS == 0 else 1
    per_b = bsz // n_groups
    groups = []
    for g in range(n_groups):
        h, hnp, idx, gate, rank, cnt, na, nb_ = _mixer(
            x_prompt, zero_a, zero_b, cnt0, params, nb=1, ts=ts, b0=g * per_b, bsz=per_b)
        parts = [dict(h=h.reshape(per_b * seq, d), hnp=hnp, idx=idx, gate_t=gate.T, rank=rank)]
        states = [(na, nb_)]
        if g == n_groups - 1:
            h, hnp, idx, gate, rank, cnt, na_s, nb_s = _mixer(
                x_sample, _pad_hist(state_conv_a[0], HIST_A_ROWS), _pad_hist(state_conv_b[0], HIST_B_ROWS),
                cnt, params, nb=dec_b, ts=dec_s)
            parts.append(dict(h=h.reshape(dec_b * dec_s, d), hnp=hnp, idx=idx, gate_t=gate.T, rank=rank))
        n_pairs = sum(p["hnp"].shape[0] for p in parts) * TOP_K
        plan, pad_start, n_blocks = _block_plan(cnt[:, 0].astype(jnp.int32), n_pairs, n_exp)
        pos = _positions(pad_start, [(p["idx"], p["rank"]) for p in parts])
        xs = _sc_dispatch([(q, p["hnp"]) for q, p in zip(pos, parts)], n_blocks * MOE_BLOCK)
        groups.append(dict(parts=parts, pos=pos, plan=plan, xs=xs, states=states))

    for grp in groups:
        grp["rows"] = _experts(grp["plan"], grp["xs"], w_gate_up[0], b_gate_up[0], w_down[0], b_down[0])

    y_p = None
    for g, grp in enumerate(groups):
        main, pos0 = grp["parts"][0], grp["pos"][0]
        n_g = main["h"].shape[0]
        n_chunks = COMBINE_CHUNKS if n_g % (COMBINE_CHUNKS * SC_CHUNK * SC_WORKERS) == 0 else 1
        per = n_g // n_chunks
        for c in range(n_chunks):
            extra = grp["pos"][1:] if c == 0 else []
            got = _sc_gather([pos0[:, c * per:(c + 1) * per]] + extra, grp["rows"])
            if extra:
                smp = grp["parts"][1]
                y_s = _combine(got[1], smp["gate_t"], smp["h"], gfin).reshape(dec_b, dec_s, d)
            y_p = _combine(got[0], main["gate_t"], main["h"], gfin, tok0=c * per,
                           y_prev=y_p, y_rows=n_p, y_tok0=g * n_g + c * per)
    y_p = y_p.reshape(bsz, seq, d)

    na_p = jnp.concatenate([grp["states"][0][0] for grp in groups], axis=0)
    nb_p = jnp.concatenate([grp["states"][0][1] for grp in groups], axis=0)
    return (y_p, y_s, na_p[None], nb_p[None], na_s[None], nb_s[None])


def _block_plan(counts, n_pairs, n_exp):
    padded = (counts + MOE_BLOCK - 1) // MOE_BLOCK * MOE_BLOCK
    pad_end = jnp.cumsum(padded)
    pad_start = pad_end - padded
    n_blocks = -(-n_pairs // MOE_BLOCK) + n_exp
    n_blocks = -(-n_blocks // BLOCKS_PER_STEP) * BLOCKS_PER_STEP
    blk_start = jnp.arange(n_blocks, dtype=jnp.int32) * MOE_BLOCK
    blk_exp = jnp.minimum(jnp.sum(blk_start[:, None] >= pad_end[None, :], axis=1),
                          n_exp - 1).astype(jnp.int32)
    n_active = (pad_end[-1:] // MOE_BLOCK).astype(jnp.int32)
    e_ids = jnp.arange(n_exp, dtype=jnp.int32)
    has = padded > 0
    slot_e = (jnp.cumsum(has.astype(jnp.int32)) - 1) % 2
    later = jnp.where(has, e_ids, n_exp)
    next_e = jnp.concatenate([lax.cummin(later, reverse=True)[1:], jnp.full((1,), n_exp, jnp.int32)])
    next_e = jnp.where(next_e >= n_exp, -1, next_e)
    blk_onehot = blk_exp[:, None] == e_ids[None, :]

    def per_block(table):
        return jnp.sum(jnp.where(blk_onehot, table[None, :], 0), axis=1).astype(jnp.int32)

    blk_first = ((blk_start == per_block(pad_start)) & (blk_start < pad_end[-1])).astype(jnp.int32)
    plan = (blk_exp, blk_first, per_block(slot_e), per_block(next_e), n_active)
    return plan, pad_start.astype(jnp.int32), n_blocks
```

```python
import functools

import jax
import jax.numpy as jnp
from jax import lax
from jax.experimental import pallas as pl
from jax.experimental.pallas import tpu as pltpu
from jax.experimental.pallas import tpu_sc as plsc

SC_CORES = 2
SC_SUBCORES = 16
SC_WORKERS = SC_CORES * SC_SUBCORES
SC_CHUNK = 128
CONV_A_W = 3
CONV_B_W = 31
TOP_K = 4
SWIGLU_LIMIT = 7.0
SWIGLU_ALPHA = 1.702
NORM_EPS = 1e-5
MOE_BLOCK = 128
BLOCKS_PER_STEP = 8
SUBLANES = 8
LANES = 128
CONV_ROW_CHUNK = 128
HIST_A_ROWS = 8
HIST_B_ROWS = 32
MIXER_ROWS = 512
PROJ_SECTION = 256
POS_TILE = 4096
MOE_GROUPS = 2
COMBINE_CHUNKS = 2
VMEM_LIMIT = 56 * 1024 * 1024


def _rms(x, g):
    return x * lax.rsqrt(jnp.mean(x * x, axis=-1, keepdims=True) + NORM_EPS) * g


def _zero_tile_from(v):
    bits = pltpu.bitcast(v[0:SUBLANES, 0:LANES], jnp.uint32)
    return pltpu.bitcast((bits >> 16) >> 16, jnp.float32)


def _pack_bf16_pairs(a):
    w = a.shape[-1] // 2
    lo = pltpu.bitcast(a[:, :w].astype(jnp.bfloat16).astype(jnp.float32), jnp.uint32)
    hi = pltpu.bitcast(a[:, w:].astype(jnp.bfloat16).astype(jnp.float32), jnp.uint32)
    return (lo >> 16) | (hi & jnp.uint32(0xFFFF0000))


def _unpack_bf16_pairs(wd):
    lo = pltpu.bitcast(wd << 16, jnp.float32)
    hi = pltpu.bitcast(wd & jnp.uint32(0xFFFF0000), jnp.float32)
    return jnp.concatenate([lo, hi], axis=-1)


def _conv_b_chunk(ub_buf, cbw_ref, n_i, r0, rc, c0):
    cs = slice(c0, c0 + LANES)
    lead = HIST_B_ROWS - (CONV_B_W - 1)
    acc = None
    for b in range(SUBLANES):
        rows = rc if b == 0 else rc + SUBLANES
        q = None
        for a in range((HIST_B_ROWS + SUBLANES) // SUBLANES):
            k = SUBLANES * a + b - lead
            if k < 0 or k >= CONV_B_W:
                continue
            start = r0 + SUBLANES * a
            term = cbw_ref[k:k + 1, cs] * ub_buf[n_i, start:start + rows, cs]
            q = term if q is None else q + term
        part = q if b == 0 else q[b:b + rc]
        acc = part if acc is None else acc + part
    return acc


def _mixer_kernel(x_ref, hista_ref, histb_ref, cnt0_ref, tri_ref,
                  gmix_ref, win_ref, caw_ref, cbw_ref, cbb_ref, lng_ref, lnb_ref,
                  ga_ref, gb_ref, wout_ref, gffn_ref, wrt_ref, br_ref,
                  h_ref, hnp_ref, idx_ref, gate_ref, rank_ref, cnt_ref, newa_ref, newb_ref,
                  ua_buf, ub_buf, cb_buf, cnt_acc, *, nb, ts, d_a, d_b, n_exp):
    b = pl.program_id(0)
    s = pl.program_id(1)
    m = nb * ts
    d = x_ref.shape[-1]

    @pl.when(s == 0)
    def _():
        ua_buf[:, 0:HIST_A_ROWS, :] = hista_ref[...]
        ub_buf[:, 0:HIST_B_ROWS, :] = histb_ref[...]

    @pl.when(s != 0)
    def _():
        ua_buf[:, 0:HIST_A_ROWS, :] = ua_buf[:, ts:ts + HIST_A_ROWS, :]
        ub_buf[:, 0:HIST_B_ROWS, :] = ub_buf[:, ts:ts + HIST_B_ROWS, :]

    @pl.when((b == 0) & (s == 0))
    def _():
        cnt_acc[...] = cnt0_ref[...]

    x = x_ref[...].reshape(m, d)
    n = _rms(x, gmix_ref[...]).astype(jnp.bfloat16)
    def proj(c0, width):
        return jnp.dot(n, win_ref[:, c0:c0 + width], preferred_element_type=jnp.float32)

    sec = min(d_b, PROJ_SECTION)
    for c0 in range(0, d_b, sec):
        v_b = proj(3 * d_a + c0, sec)
        g_b = proj(3 * d_a + d_b + c0, sec)
        ub_buf[:, HIST_B_ROWS:HIST_B_ROWS + ts, c0:c0 + sec] = (v_b * jax.nn.sigmoid(g_b)).reshape(nb, ts, sec)

    gate_c = proj(d_a, d_a)
    xt = proj(2 * d_a, d_a)
    gate_b = proj(0, d_a)
    ua_buf[:, HIST_A_ROWS:HIST_A_ROWS + ts, :] = (gate_c * xt).reshape(nb, ts, d_a)
    conv_a = jnp.zeros((nb, ts, d_a), jnp.float32)
    for k in range(CONV_A_W):
        off = HIST_A_ROWS - (CONV_A_W - 1) + k
        conv_a = conv_a + caw_ref[k:k + 1, :] * ua_buf[:, off:off + ts, :]
    y_a = gate_b * conv_a.reshape(m, d_a)

    mix_a = jnp.dot(_rms(y_a, ga_ref[...]).astype(jnp.bfloat16), wout_ref[0:d_a, :],
                    preferred_element_type=jnp.float32)

    rc = min(ts, CONV_ROW_CHUNK)
    chunks = [(n_i, r0, c0) for n_i in range(nb) for r0 in range(0, ts, rc) for c0 in range(0, d_b, LANES)]
    anchors = {len(chunks) * 2 // 8: gate_c, len(chunks) * 3 // 8: xt, len(chunks) * 5 // 8: gate_b,
               len(chunks) - 1: mix_a}
    for ci, (n_i, r0, c0) in enumerate(chunks):
        acc = _conv_b_chunk(ub_buf, cbw_ref, n_i, r0, rc, c0)
        if ci in anchors:
            acc = (acc.reshape(rc // SUBLANES, SUBLANES, LANES) + _zero_tile_from(anchors[ci])[None]
                   ).reshape(rc, LANES)
        cb_buf[n_i, r0:r0 + rc, c0:c0 + LANES] = acc
    cb = cb_buf[...].reshape(m, d_b) + cbb_ref[...]
    mu = jnp.mean(cb, axis=-1, keepdims=True)
    xc = cb - mu
    var = jnp.mean(xc * xc, axis=-1, keepdims=True)
    ln = xc * lax.rsqrt(var + NORM_EPS) * lng_ref[...] + lnb_ref[...]
    y_b = ln * jax.nn.sigmoid(ln)

    mix_b = jnp.dot(_rms(y_b, gb_ref[...]).astype(jnp.bfloat16), wout_ref[d_a:d_a + d_b, :],
                    preferred_element_type=jnp.float32)
    h = x + (mix_a + mix_b)
    h_ref[...] = h.reshape(nb, ts, d)

    hn = _rms(h, gffn_ref[...])
    hnp_ref[...] = _pack_bf16_pairs(hn)

    logits = lax.dot_general(wrt_ref[...], hn.astype(jnp.bfloat16), (((1,), (1,)), ((), ())),
                             preferred_element_type=jnp.float32) + br_ref[...]
    e_iota = lax.broadcasted_iota(jnp.int32, (n_exp, m), 0)
    cur = logits
    vals, idxs, sels = [], [], []
    for _ in range(TOP_K):
        mx = jnp.max(cur, axis=0, keepdims=True)
        ix = jnp.min(jnp.where(cur == mx, e_iota, n_exp), axis=0, keepdims=True)
        sel = e_iota == ix
        vals.append(mx)
        idxs.append(ix)
        sels.append(sel)
        cur = jnp.where(sel, -jnp.inf, cur)
    exps = [jnp.exp(v - vals[0]) for v in vals]
    denom = exps[0] + exps[1] + exps[2] + exps[3]
    gates = [e / denom for e in exps]

    onehot = jnp.zeros((n_exp, m), jnp.float32)
    for sel in sels:
        onehot = onehot + sel.astype(jnp.float32)
    before = jnp.dot(onehot.astype(jnp.bfloat16), tri_ref[...],
                     preferred_element_type=jnp.float32) + cnt_acc[...]
    ranks = [jnp.sum(jnp.where(sel, before, 0.0), axis=0, keepdims=True) for sel in sels]
    new_cnt = cnt_acc[...] + jnp.sum(onehot, axis=1, keepdims=True)
    cnt_acc[...] = new_cnt
    cnt_ref[...] = new_cnt

    idx_ref[...] = jnp.concatenate(idxs, axis=0)
    gate_ref[...] = jnp.concatenate(gates, axis=0)
    rank_ref[...] = jnp.concatenate(ranks, axis=0).astype(jnp.int32)

    @pl.when(s == pl.num_programs(1) - 1)
    def _():
        newa_ref[...] = ua_buf[:, ts + HIST_A_ROWS - (CONV_A_W - 1):ts + HIST_A_ROWS, :]
        newb_ref[...] = ub_buf[:, ts + HIST_B_ROWS - (CONV_B_W - 1):ts + HIST_B_ROWS, :]


def _mixer(x, hist_a, hist_b, cnt0, params, *, nb, ts, b0=0, bsz=None):
    (gmix, win, caw, cbw, cbb, lng, lnb, ga, gb, wout, gffn, wrt, br) = params
    _, seq, d = x.shape
    bsz = x.shape[0] if bsz is None else bsz
    d_a, d_b = caw.shape[-1], cbw.shape[-1]
    n_exp = wrt.shape[0]
    m = nb * ts
    n_tok = bsz * seq
    g0 = b0 // nb
    grid = (bsz // nb, seq // ts)
    tri = (jnp.arange(m)[:, None] < jnp.arange(m)[None, :]).astype(jnp.bfloat16)

    def const(shape):
        return pl.BlockSpec(shape, lambda i, j: (0,) * len(shape))

    tok_map = lambda i, j: (0, i * (seq // ts) + j)
    kern = functools.partial(_mixer_kernel, nb=nb, ts=ts, d_a=d_a, d_b=d_b, n_exp=n_exp)
    out_shape = (
        jax.ShapeDtypeStruct((bsz, seq, d), jnp.float32),
        jax.ShapeDtypeStruct((n_tok, d // 2), jnp.uint32),
        jax.ShapeDtypeStruct((TOP_K, n_tok), jnp.int32),
        jax.ShapeDtypeStruct((TOP_K, n_tok), jnp.float32),
        jax.ShapeDtypeStruct((TOP_K, n_tok), jnp.int32),
        jax.ShapeDtypeStruct((n_exp, 1), jnp.float32),
        jax.ShapeDtypeStruct((bsz, CONV_A_W - 1, d_a), jnp.float32),
        jax.ShapeDtypeStruct((bsz, CONV_B_W - 1, d_b), jnp.float32),
    )
    return pl.pallas_call(
        kern,
        grid=grid,
        in_specs=[
            pl.BlockSpec((nb, ts, d), lambda i, j: (g0 + i, j, 0)),
            pl.BlockSpec((nb, HIST_A_ROWS, d_a), lambda i, j: (g0 + i, 0, 0)),
            pl.BlockSpec((nb, HIST_B_ROWS, d_b), lambda i, j: (g0 + i, 0, 0)),
            const((n_exp, 1)),
            const((m, m)),
            const((1, d)), const(win.shape), const(caw.shape), const(cbw.shape), const((1, d_b)),
            const((1, d_b)), const((1, d_b)), const((1, d_a)), const((1, d_b)), const(wout.shape),
            const((1, d)), const(wrt.shape), const((n_exp, 1)),
        ],
        out_specs=(
            pl.BlockSpec((nb, ts, d), lambda i, j: (i, j, 0)),
            pl.BlockSpec((m, d // 2), lambda i, j: (i * (seq // ts) + j, 0)),
            pl.BlockSpec((TOP_K, m), tok_map),
            pl.BlockSpec((TOP_K, m), tok_map),
            pl.BlockSpec((TOP_K, m), tok_map),
            pl.BlockSpec((n_exp, 1), lambda i, j: (0, 0)),
            pl.BlockSpec((nb, CONV_A_W - 1, d_a), lambda i, j: (i, 0, 0)),
            pl.BlockSpec((nb, CONV_B_W - 1, d_b), lambda i, j: (i, 0, 0)),
        ),
        out_shape=out_shape,
        scratch_shapes=[
            pltpu.VMEM((nb, HIST_A_ROWS + ts, d_a), jnp.float32),
            pltpu.VMEM((nb, HIST_B_ROWS + ts, d_b), jnp.float32),
            pltpu.VMEM((nb, ts, d_b), jnp.float32),
            pltpu.VMEM((n_exp, 1), jnp.float32),
        ],
        compiler_params=pltpu.CompilerParams(
            dimension_semantics=("arbitrary", "arbitrary"), vmem_limit_bytes=VMEM_LIMIT),
        name="mixer",
    )(x, hist_a, hist_b, cnt0, tri, gmix, win, caw, cbw, cbb, lng, lnb, ga, gb, wout, gffn, wrt, br)


def _positions_kernel(start_ref, *refs):
    n_parts = len(refs) // 3
    for p in range(n_parts):
        idx = refs[2 * p][...]
        pos = refs[2 * p + 1][...]
        for e in range(start_ref.shape[0]):
            pos = pos + jnp.where(idx == e, start_ref[e], 0)
        refs[2 * n_parts + p][...] = pos


def _positions(pad_start, parts):
    n_0 = parts[0][0].shape[1]
    tile = min(n_0, POS_TILE)
    specs = [pl.BlockSpec((TOP_K, tile), lambda i: (0, i))]
    specs += [pl.BlockSpec((TOP_K, idx.shape[1]), lambda i: (0, 0)) for idx, _ in parts[1:]]
    in_specs = [pl.BlockSpec(memory_space=pltpu.SMEM)]
    args = [pad_start]
    for spec, (idx, rank) in zip(specs, parts):
        in_specs += [spec, spec]
        args += [idx, rank]
    out = pl.pallas_call(
        _positions_kernel,
        grid=(n_0 // tile,),
        in_specs=in_specs,
        out_specs=tuple(specs),
        out_shape=tuple(jax.ShapeDtypeStruct(idx.shape, jnp.int32) for idx, _ in parts),
        compiler_params=pltpu.CompilerParams(dimension_semantics=("arbitrary",)),
        name="positions",
    )(*args)
    return list(out)


def _sc_mesh():
    return plsc.VectorSubcoreMesh(core_axis_name="c", subcore_axis_name="s",
                                  num_cores=SC_CORES, num_subcores=SC_SUBCORES)


def _sc_worker_id():
    return lax.axis_index("s") * SC_CORES + lax.axis_index("c")


def _sc_chunks(n_tok, wid):
    n_chunks = n_tok // SC_CHUNK
    assert n_tok % SC_CHUNK == 0
    if n_chunks % SC_WORKERS == 0:
        per_w = n_chunks // SC_WORKERS
        return None, per_w, lambda i: (wid * per_w + i) * SC_CHUNK
    assert n_chunks <= SC_WORKERS
    return wid < n_chunks, 1, lambda i: wid * SC_CHUNK


def _sc_for_each_chunk(n_tok, fn):
    wid = _sc_worker_id()
    pred, trips, base = _sc_chunks(n_tok, wid)

    def run():
        @pl.loop(0, trips)
        def _(i):
            fn(pl.multiple_of(base(i), SC_CHUNK))

    if pred is None:
        run()
    else:
        pl.when(pred)(run)


def _sc_dispatch(parts, n_rows):
    half = parts[0][1].shape[1]
    n_parts = len(parts)

    @functools.partial(
        pl.kernel, mesh=_sc_mesh(),
        out_type=jax.ShapeDtypeStruct((n_rows, half), jnp.uint32),
        scratch_types=[pltpu.VMEM((TOP_K, SC_CHUNK), jnp.int32),
                       pltpu.VMEM((SC_CHUNK, half), jnp.uint32),
                       pltpu.SemaphoreType.DMA],
        compiler_params=pltpu.CompilerParams(use_tc_tiling_on_sc=True),
        name="sc_dispatch")
    def k(*refs):
        xs_hbm, idx_v, rows_v, sem = refs[2 * n_parts:]

        def move(pos_hbm, src_hbm):
            def fn(base):
                pltpu.sync_copy(pos_hbm.at[:, pl.ds(base, SC_CHUNK)], idx_v)
                pltpu.sync_copy(src_hbm.at[pl.ds(base, SC_CHUNK)], rows_v)
                copies = [pltpu.async_copy(rows_v, xs_hbm.at[idx_v.at[j]], sem) for j in range(TOP_K)]
                for c in copies:
                    c.wait()
            return fn

        for p, (_, hnp) in enumerate(parts):
            _sc_for_each_chunk(hnp.shape[0], move(refs[2 * p], refs[2 * p + 1]))

    return k(*[a for part in parts for a in part])


def _sc_gather(pos_parts, rows):
    half = rows.shape[1]
    n_parts = len(pos_parts)

    @functools.partial(
        pl.kernel, mesh=_sc_mesh(),
        out_type=tuple(jax.ShapeDtypeStruct((TOP_K, p.shape[1], half), jnp.uint32) for p in pos_parts),
        scratch_types=[pltpu.VMEM((TOP_K, SC_CHUNK), jnp.int32),
                       pltpu.VMEM((SC_CHUNK, half), jnp.uint32),
                       pltpu.SemaphoreType.DMA],
        compiler_params=pltpu.CompilerParams(use_tc_tiling_on_sc=True),
        name="sc_gather")
    def k(*refs):
        pos_hbms, rows_hbm = refs[:n_parts], refs[n_parts]
        out_hbms = refs[n_parts + 1:2 * n_parts + 1]
        idx_v, buf_v, sem = refs[2 * n_parts + 1:]

        def move(pos_hbm, out_hbm):
            def fn(base):
                pltpu.sync_copy(pos_hbm.at[:, pl.ds(base, SC_CHUNK)], idx_v)
                for j in range(TOP_K):
                    pltpu.async_copy(rows_hbm.at[idx_v.at[j]], buf_v, sem).wait()
                    pltpu.sync_copy(buf_v, out_hbm.at[j, pl.ds(base, SC_CHUNK)])
            return fn

        for pos_hbm, out_hbm, p in zip(pos_hbms, out_hbms, pos_parts):
            _sc_for_each_chunk(p.shape[1], move(pos_hbm, out_hbm))

    out = k(*pos_parts, rows)
    return out if isinstance(out, (tuple, list)) else (out,)


def _experts_kernel(blk_exp_ref, first_ref, slot_ref, next_ref, n_active_ref,
                    xs_ref, wgu_hbm, bgu_ref, wd_hbm, bd_ref, out_ref,
                    wgu_f32, wd_f32, wgu_bf, wd_bf, sems):
    d_ff = wd_bf.shape[0]

    def fetch(e, sl):
        return (pltpu.make_async_copy(wgu_hbm.at[e], wgu_f32.at[sl], sems.at[0, sl]),
                pltpu.make_async_copy(wd_hbm.at[e], wd_f32.at[sl], sems.at[1, sl]))

    def load_weights_if_first(i):
        sl = slot_ref[i]
        e = blk_exp_ref[i]

        @pl.when(first_ref[i] == 1)
        def _():
            @pl.when(i == 0)
            def _():
                for c in fetch(e, sl):
                    c.start()

            for c in fetch(e, sl):
                c.wait()
            wgu_bf[...] = wgu_f32[sl].astype(jnp.bfloat16)
            wd_bf[...] = wd_f32[sl].astype(jnp.bfloat16)

            @pl.when(next_ref[i] >= 0)
            def _():
                for c in fetch(next_ref[i], 1 - sl):
                    c.start()

    def compute(rows, e):
        x = _unpack_bf16_pairs(xs_ref[rows, :]).astype(jnp.bfloat16)
        gu = jnp.dot(x, wgu_bf[...], preferred_element_type=jnp.float32) + bgu_ref[e]
        g = jnp.minimum(gu[:, :d_ff], SWIGLU_LIMIT)
        u = jnp.clip(gu[:, d_ff:], -SWIGLU_LIMIT, SWIGLU_LIMIT)
        act = g * jax.nn.sigmoid(SWIGLU_ALPHA * g) * (u + 1.0)
        o = jnp.dot(act.astype(jnp.bfloat16), wd_bf[...], preferred_element_type=jnp.float32) + bd_ref[e]
        out_ref[rows, :] = _pack_bf16_pairs(o)

    i0 = pl.program_id(0) * BLOCKS_PER_STEP
    n_active = n_active_ref[0]

    def run(sub0, n_sub):
        first = i0 + sub0
        rows = slice(sub0 * MOE_BLOCK, (sub0 + n_sub) * MOE_BLOCK)
        if n_sub == 1:
            @pl.when(first < n_active)
            def _():
                load_weights_if_first(first)
                compute(rows, blk_exp_ref[first])

            @pl.when(first >= n_active)
            def _():
                out_ref[rows, :] = jnp.zeros((MOE_BLOCK, out_ref.shape[1]), out_ref.dtype)
            return

        same = first + n_sub - 1 < n_active
        for sub in range(1, n_sub):
            same = same & (blk_exp_ref[first + sub] == blk_exp_ref[first])

        @pl.when(same)
        def _():
            load_weights_if_first(first)
            compute(rows, blk_exp_ref[first])

        @pl.when(jnp.logical_not(same))
        def _():
            run(sub0, n_sub // 2)
            run(sub0 + n_sub // 2, n_sub // 2)

    run(0, BLOCKS_PER_STEP)


def _experts(plan, xs, wgu, bgu, wd, bd):
    blk_exp, first, slot, nxt, n_active = plan
    n_rows, half = xs.shape
    n_exp, d, two_f = wgu.shape
    d_ff = wd.shape[1]
    n_blocks = n_rows // MOE_BLOCK
    step_rows = MOE_BLOCK * BLOCKS_PER_STEP
    assert n_rows % step_rows == 0
    grid_spec = pltpu.PrefetchScalarGridSpec(
        num_scalar_prefetch=5,
        grid=(n_blocks // BLOCKS_PER_STEP,),
        in_specs=[
            pl.BlockSpec((step_rows, half), lambda i, *_: (i, 0)),
            pl.BlockSpec(memory_space=pl.ANY),
            pl.BlockSpec((n_exp, 1, two_f), lambda i, *_: (0, 0, 0)),
            pl.BlockSpec(memory_space=pl.ANY),
            pl.BlockSpec((n_exp, 1, d), lambda i, *_: (0, 0, 0)),
        ],
        out_specs=pl.BlockSpec((step_rows, half), lambda i, *_: (i, 0)),
        scratch_shapes=[
            pltpu.VMEM((2, d, two_f), jnp.float32),
            pltpu.VMEM((2, d_ff, d), jnp.float32),
            pltpu.VMEM((d, two_f), jnp.bfloat16),
            pltpu.VMEM((d_ff, d), jnp.bfloat16),
            pltpu.SemaphoreType.DMA((2, 2)),
        ],
    )
    return pl.pallas_call(
        _experts_kernel,
        grid_spec=grid_spec,
        out_shape=jax.ShapeDtypeStruct((n_rows, half), jnp.uint32),
        compiler_params=pltpu.CompilerParams(
            dimension_semantics=("arbitrary",), vmem_limit_bytes=VMEM_LIMIT),
        name="experts",
    )(blk_exp, first, slot, nxt, n_active, xs, wgu, bgu.reshape(n_exp, 1, two_f), wd,
      bd.reshape(n_exp, 1, d))


def _combine_kernel(rows_ref, gate_ref, h_ref, gfin_ref, *rest):
    y_ref = rest[-1]
    acc = h_ref[...]
    for k in range(TOP_K):
        acc = acc + gate_ref[:, k:k + 1] * _unpack_bf16_pairs(rows_ref[k])
    y_ref[...] = _rms(acc, gfin_ref[...])


def _combine(rows4, gate_t, h2d, gfin, *, tok0=0, y_prev=None, y_rows=None, y_tok0=0):
    d = h2d.shape[1]
    n, half = rows4.shape[1:]
    n_tok = h2d.shape[0] if y_rows is None else y_rows
    tile = min(n, 256)
    t0 = tok0 // tile
    y0 = y_tok0 // tile
    in_specs = [
        pl.BlockSpec((TOP_K, tile, half), lambda i: (0, i, 0)),
        pl.BlockSpec((tile, TOP_K), lambda i: (t0 + i, 0)),
        pl.BlockSpec((tile, d), lambda i: (t0 + i, 0)),
        pl.BlockSpec((1, d), lambda i: (0, 0)),
    ]
    args = [rows4, gate_t, h2d, gfin]
    aliases = {}
    if y_prev is not None:
        in_specs.append(pl.BlockSpec(memory_space=pl.ANY))
        args.append(y_prev)
        aliases = {4: 0}
    return pl.pallas_call(
        _combine_kernel,
        grid=(n // tile,),
        in_specs=in_specs,
        out_specs=pl.BlockSpec((tile, d), lambda i: (y0 + i, 0)),
        out_shape=jax.ShapeDtypeStruct((n_tok, d), jnp.float32),
        input_output_aliases=aliases,
        compiler_params=pltpu.CompilerParams(dimension_semantics=("arbitrary",)),
        name="combine",
    )(*args)


def _pad_hist(hist, rows):
    return jnp.pad(hist, ((0, 0), (rows - hist.shape[1], 0), (0, 0)))


def kernel(x_prompt, x_sample, state_conv_a, state_conv_b, norm_mix_g, w_in, conv_a_w, conv_b_w, conv_b_b, conv_ln_g, conv_ln_b, out_norm_a_g, out_norm_b_g, w_out, norm_ffn_g, w_router, b_router, w_gate_up, b_gate_up, w_down, b_down, final_norm_g):
    depth = w_in.shape[0]
    assert depth == 1, "single-layer trunk"
    bf16 = jnp.bfloat16
    bsz, seq, d = x_prompt.shape
    dec_b, dec_s, _ = x_sample.shape
    d_a, d_b = conv_a_w.shape[-1], conv_b_w.shape[-1]
    n_exp = w_router.shape[-1]

    params = (norm_mix_g[0][None], w_in[0].astype(bf16), conv_a_w[0], conv_b_w[0], conv_b_b[0][None],
              conv_ln_g[0][None], conv_ln_b[0][None], out_norm_a_g[0][None], out_norm_b_g[0][None],
              w_out[0].astype(bf16), norm_ffn_g[0][None], w_router[0].T.astype(bf16),
              b_router[0][:, None])

    zero_a = jnp.zeros((bsz, HIST_A_ROWS, d_a), jnp.float32)
    zero_b = jnp.zeros((bsz, HIST_B_ROWS, d_b), jnp.float32)
    cnt0 = jnp.zeros((n_exp, 1), jnp.float32)
    gfin = final_norm_g[None]
    ts = min(seq, MIXER_ROWS)
    n_p = bsz * seq

    n_groups = MOE_GROUPS if bsz % MOE_GROUPS == 0 else 1
    per_b = bsz // n_groups
    groups = []
    for g in range(n_groups):
        h, hnp, idx, gate, rank, cnt, na, nb_ = _mixer(
            x_prompt, zero_a, zero_b, cnt0, params, nb=1, ts=ts, b0=g * per_b, bsz=per_b)
        parts = [dict(h=h.reshape(per_b * seq, d), hnp=hnp, idx=idx, gate_t=gate.T, rank=rank)]
        states = [(na, nb_)]
        if g == n_groups - 1:
            h, hnp, idx, gate, rank, cnt, na_s, nb_s = _mixer(
                x_sample, _pad_hist(state_conv_a[0], HIST_A_ROWS), _pad_hist(state_conv_b[0], HIST_B_ROWS),
                cnt, params, nb=dec_b, ts=dec_s)
            parts.append(dict(h=h.reshape(dec_b * dec_s, d), hnp=hnp, idx=idx, gate_t=gate.T, rank=rank))
        n_pairs = sum(p["hnp"].shape[0] for p in parts) * TOP_K
        plan, pad_start, n_blocks = _block_plan(cnt[:, 0].astype(jnp.int32), n_pairs, n_exp)
        pos = _positions(pad_start, [(p["idx"], p["rank"]) for p in parts])
        xs = _sc_dispatch([(q, p["hnp"]) for q, p in zip(pos, parts)], n_blocks * MOE_BLOCK)
        groups.append(dict(parts=parts, pos=pos, plan=plan, xs=xs, states=states))

    for grp in groups:
        grp["rows"] = _experts(grp["plan"], grp["xs"], w_gate_up[0], b_gate_up[0], w_down[0], b_down[0])

    y_p = None
    for g, grp in enumerate(groups):
        main, pos0 = grp["parts"][0], grp["pos"][0]
        n_g = main["h"].shape[0]
        n_chunks = COMBINE_CHUNKS if n_g % (COMBINE_CHUNKS * SC_CHUNK * SC_WORKERS) == 0 else 1
        per = n_g // n_chunks
        for c in range(n_chunks):
            extra = grp["pos"][1:] if c == 0 else []
            got = _sc_gather([pos0[:, c * per:(c + 1) * per]] + extra, grp["rows"])
            if extra:
                smp = grp["parts"][1]
                y_s = _combine(got[1], smp["gate_t"], smp["h"], gfin).reshape(dec_b, dec_s, d)
            y_p = _combine(got[0], main["gate_t"], main["h"], gfin, tok0=c * per,
                           y_prev=y_p, y_rows=n_p, y_tok0=g * n_g + c * per)
    y_p = y_p.reshape(bsz, seq, d)

    na_p = jnp.concatenate([grp["states"][0][0] for grp in groups], axis=0)
    nb_p = jnp.concatenate([grp["states"][0][1] for grp in groups], axis=0)
    return (y_p, y_s, na_p[None], nb_p[None], na_s[None], nb_s[None])


def _block_plan(counts, n_pairs, n_exp):
    padded = (counts + MOE_BLOCK - 1) // MOE_BLOCK * MOE_BLOCK
    pad_end = jnp.cumsum(padded)
    pad_start = pad_end - padded
    n_blocks = -(-n_pairs // MOE_BLOCK) + n_exp
    n_blocks = -(-n_blocks // BLOCKS_PER_STEP) * BLOCKS_PER_STEP
    blk_start = jnp.arange(n_blocks, dtype=jnp.int32) * MOE_BLOCK
    blk_exp = jnp.minimum(jnp.sum(blk_start[:, None] >= pad_end[None, :], axis=1),
                          n_exp - 1).astype(jnp.int32)
    n_active = (pad_end[-1:] // MOE_BLOCK).astype(jnp.int32)
    e_ids = jnp.arange(n_exp, dtype=jnp.int32)
    has = padded > 0
    slot_e = (jnp.cumsum(has.astype(jnp.int32)) - 1) % 2
    later = jnp.where(has, e_ids, n_exp)
    next_e = jnp.concatenate([lax.cummin(later, reverse=True)[1:], jnp.full((1,), n_exp, jnp.int32)])
    next_e = jnp.where(next_e >= n_exp, -1, next_e)
    blk_onehot = blk_exp[:, None] == e_ids[None, :]

    def per_block(table):
        return jnp.sum(jnp.where(blk_onehot, table[None, :], 0), axis=1).astype(jnp.int32)

    blk_first = ((blk_start == per_block(pad_start)) & (blk_start < pad_end[-1])).astype(jnp.int32)
    plan = (blk_exp, blk_first, per_block(slot_e), per_block(next_e), n_active)
    return plan, pad_start.astype(jnp.int32), n_blocks
```

```python
import functools

import jax
import jax.numpy as jnp
from jax import lax
from jax.experimental import pallas as pl
from jax.experimental.pallas import tpu as pltpu
from jax.experimental.pallas import tpu_sc as plsc

SC_CORES = 2
SC_SUBCORES = 16
SC_WORKERS = SC_CORES * SC_SUBCORES
SC_CHUNK = 128
CONV_A_W = 3
CONV_B_W = 31
TOP_K = 4
SWIGLU_LIMIT = 7.0
SWIGLU_ALPHA = 1.702
NORM_EPS = 1e-5
MOE_BLOCK = 256
BLOCKS_PER_STEP = 4
SUBLANES = 8
LANES = 128
CONV_ROW_CHUNK = 128
HIST_A_ROWS = 8
HIST_B_ROWS = 32
MIXER_ROWS = 512
PROJ_SECTION = 256
POS_TILE = 4096
GROUP_SPLIT = (3, 1)
COMBINE_CHUNKS = 2
VMEM_LIMIT = 56 * 1024 * 1024


def _rms(x, g):
    return x * lax.rsqrt(jnp.mean(x * x, axis=-1, keepdims=True) + NORM_EPS) * g


def _zero_tile_from(v):
    bits = pltpu.bitcast(v[0:SUBLANES, 0:LANES], jnp.uint32)
    return pltpu.bitcast((bits >> 16) >> 16, jnp.float32)


def _pack_bf16_pairs(a):
    w = a.shape[-1] // 2
    lo = pltpu.bitcast(a[:, :w].astype(jnp.bfloat16).astype(jnp.float32), jnp.uint32)
    hi = pltpu.bitcast(a[:, w:].astype(jnp.bfloat16).astype(jnp.float32), jnp.uint32)
    return (lo >> 16) | (hi & jnp.uint32(0xFFFF0000))


def _unpack_bf16_pairs(wd):
    lo = pltpu.bitcast(wd << 16, jnp.float32)
    hi = pltpu.bitcast(wd & jnp.uint32(0xFFFF0000), jnp.float32)
    return jnp.concatenate([lo, hi], axis=-1)


def _conv_b_chunk(ub_buf, cbw_ref, n_i, r0, rc, c0):
    cs = slice(c0, c0 + LANES)
    lead = HIST_B_ROWS - (CONV_B_W - 1)
    acc = None
    for b in range(SUBLANES):
        rows = rc if b == 0 else rc + SUBLANES
        q = None
        for a in range((HIST_B_ROWS + SUBLANES) // SUBLANES):
            k = SUBLANES * a + b - lead
            if k < 0 or k >= CONV_B_W:
                continue
            start = r0 + SUBLANES * a
            term = cbw_ref[k:k + 1, cs] * ub_buf[n_i, start:start + rows, cs]
            q = term if q is None else q + term
        part = q if b == 0 else q[b:b + rc]
        acc = part if acc is None else acc + part
    return acc


def _mixer_kernel(x_ref, hista_ref, histb_ref, cnt0_ref, tri_ref,
                  gmix_ref, win_ref, caw_ref, cbw_ref, cbb_ref, lng_ref, lnb_ref,
                  ga_ref, gb_ref, wout_ref, gffn_ref, wrt_ref, br_ref,
                  h_ref, hnp_ref, idx_ref, gate_ref, rank_ref, cnt_ref, newa_ref, newb_ref,
                  ua_buf, ub_buf, cb_buf, cnt_acc, *, nb, ts, d_a, d_b, n_exp):
    b = pl.program_id(0)
    s = pl.program_id(1)
    m = nb * ts
    d = x_ref.shape[-1]

    @pl.when(s == 0)
    def _():
        ua_buf[:, 0:HIST_A_ROWS, :] = hista_ref[...]
        ub_buf[:, 0:HIST_B_ROWS, :] = histb_ref[...]

    @pl.when(s != 0)
    def _():
        ua_buf[:, 0:HIST_A_ROWS, :] = ua_buf[:, ts:ts + HIST_A_ROWS, :]
        ub_buf[:, 0:HIST_B_ROWS, :] = ub_buf[:, ts:ts + HIST_B_ROWS, :]

    @pl.when((b == 0) & (s == 0))
    def _():
        cnt_acc[...] = cnt0_ref[...]

    x = x_ref[...].reshape(m, d)
    n = _rms(x, gmix_ref[...]).astype(jnp.bfloat16)
    def proj(c0, width):
        return jnp.dot(n, win_ref[:, c0:c0 + width], preferred_element_type=jnp.float32)

    sec = min(d_b, PROJ_SECTION)
    for c0 in range(0, d_b, sec):
        v_b = proj(3 * d_a + c0, sec)
        g_b = proj(3 * d_a + d_b + c0, sec)
        ub_buf[:, HIST_B_ROWS:HIST_B_ROWS + ts, c0:c0 + sec] = (v_b * jax.nn.sigmoid(g_b)).reshape(nb, ts, sec)

    gate_c = proj(d_a, d_a)
    xt = proj(2 * d_a, d_a)
    gate_b = proj(0, d_a)
    ua_buf[:, HIST_A_ROWS:HIST_A_ROWS + ts, :] = (gate_c * xt).reshape(nb, ts, d_a)
    conv_a = jnp.zeros((nb, ts, d_a), jnp.float32)
    for k in range(CONV_A_W):
        off = HIST_A_ROWS - (CONV_A_W - 1) + k
        conv_a = conv_a + caw_ref[k:k + 1, :] * ua_buf[:, off:off + ts, :]
    y_a = gate_b * conv_a.reshape(m, d_a)

    mix_a = jnp.dot(_rms(y_a, ga_ref[...]).astype(jnp.bfloat16), wout_ref[0:d_a, :],
                    preferred_element_type=jnp.float32)

    rc = min(ts, CONV_ROW_CHUNK)
    chunks = [(n_i, r0, c0) for n_i in range(nb) for r0 in range(0, ts, rc) for c0 in range(0, d_b, LANES)]
    anchors = {len(chunks) * 2 // 8: gate_c, len(chunks) * 3 // 8: xt, len(chunks) * 5 // 8: gate_b,
               len(chunks) - 1: mix_a}
    for ci, (n_i, r0, c0) in enumerate(chunks):
        acc = _conv_b_chunk(ub_buf, cbw_ref, n_i, r0, rc, c0)
        if ci in anchors:
            acc = (acc.reshape(rc // SUBLANES, SUBLANES, LANES) + _zero_tile_from(anchors[ci])[None]
                   ).reshape(rc, LANES)
        cb_buf[n_i, r0:r0 + rc, c0:c0 + LANES] = acc
    cb = cb_buf[...].reshape(m, d_b) + cbb_ref[...]
    mu = jnp.mean(cb, axis=-1, keepdims=True)
    xc = cb - mu
    var = jnp.mean(xc * xc, axis=-1, keepdims=True)
    ln = xc * lax.rsqrt(var + NORM_EPS) * lng_ref[...] + lnb_ref[...]
    y_b = ln * jax.nn.sigmoid(ln)

    mix_b = jnp.dot(_rms(y_b, gb_ref[...]).astype(jnp.bfloat16), wout_ref[d_a:d_a + d_b, :],
                    preferred_element_type=jnp.float32)
    h = x + (mix_a + mix_b)
    h_ref[...] = h.reshape(nb, ts, d)

    hn = _rms(h, gffn_ref[...])
    hnp_ref[...] = _pack_bf16_pairs(hn)

    logits = lax.dot_general(wrt_ref[...], hn.astype(jnp.bfloat16), (((1,), (1,)), ((), ())),
                             preferred_element_type=jnp.float32) + br_ref[...]
    e_iota = lax.broadcasted_iota(jnp.int32, (n_exp, m), 0)
    cur = logits
    vals, idxs, sels = [], [], []
    for _ in range(TOP_K):
        mx = jnp.max(cur, axis=0, keepdims=True)
        ix = jnp.min(jnp.where(cur == mx, e_iota, n_exp), axis=0, keepdims=True)
        sel = e_iota == ix
        vals.append(mx)
        idxs.append(ix)
        sels.append(sel)
        cur = jnp.where(sel, -jnp.inf, cur)
    exps = [jnp.exp(v - vals[0]) for v in vals]
    denom = exps[0] + exps[1] + exps[2] + exps[3]
    gates = [e / denom for e in exps]

    onehot = jnp.zeros((n_exp, m), jnp.float32)
    for sel in sels:
        onehot = onehot + sel.astype(jnp.float32)
    before = jnp.dot(onehot.astype(jnp.bfloat16), tri_ref[...],
                     preferred_element_type=jnp.float32) + cnt_acc[...]
    ranks = [jnp.sum(jnp.where(sel, before, 0.0), axis=0, keepdims=True) for sel in sels]
    new_cnt = cnt_acc[...] + jnp.sum(onehot, axis=1, keepdims=True)
    cnt_acc[...] = new_cnt
    cnt_ref[...] = new_cnt

    idx_ref[...] = jnp.concatenate(idxs, axis=0)
    gate_ref[...] = jnp.concatenate(gates, axis=0)
    rank_ref[...] = jnp.concatenate(ranks, axis=0).astype(jnp.int32)

    @pl.when(s == pl.num_programs(1) - 1)
    def _():
        newa_ref[...] = ua_buf[:, ts + HIST_A_ROWS - (CONV_A_W - 1):ts + HIST_A_ROWS, :]
        newb_ref[...] = ub_buf[:, ts + HIST_B_ROWS - (CONV_B_W - 1):ts + HIST_B_ROWS, :]


def _mixer(x, hist_a, hist_b, cnt0, params, *, nb, ts, b0=0, bsz=None):
    (gmix, win, caw, cbw, cbb, lng, lnb, ga, gb, wout, gffn, wrt, br) = params
    _, seq, d = x.shape
    bsz = x.shape[0] if bsz is None else bsz
    d_a, d_b = caw.shape[-1], cbw.shape[-1]
    n_exp = wrt.shape[0]
    m = nb * ts
    n_tok = bsz * seq
    g0 = b0 // nb
    grid = (bsz // nb, seq // ts)
    tri = (jnp.arange(m)[:, None] < jnp.arange(m)[None, :]).astype(jnp.bfloat16)

    def const(shape):
        return pl.BlockSpec(shape, lambda i, j: (0,) * len(shape))

    tok_map = lambda i, j: (0, i * (seq // ts) + j)
    kern = functools.partial(_mixer_kernel, nb=nb, ts=ts, d_a=d_a, d_b=d_b, n_exp=n_exp)
    out_shape = (
        jax.ShapeDtypeStruct((bsz, seq, d), jnp.float32),
        jax.ShapeDtypeStruct((n_tok, d // 2), jnp.uint32),
        jax.ShapeDtypeStruct((TOP_K, n_tok), jnp.int32),
        jax.ShapeDtypeStruct((TOP_K, n_tok), jnp.float32),
        jax.ShapeDtypeStruct((TOP_K, n_tok), jnp.int32),
        jax.ShapeDtypeStruct((n_exp, 1), jnp.float32),
        jax.ShapeDtypeStruct((bsz, CONV_A_W - 1, d_a), jnp.float32),
        jax.ShapeDtypeStruct((bsz, CONV_B_W - 1, d_b), jnp.float32),
    )
    return pl.pallas_call(
        kern,
        grid=grid,
        in_specs=[
            pl.BlockSpec((nb, ts, d), lambda i, j: (g0 + i, j, 0)),
            pl.BlockSpec((nb, HIST_A_ROWS, d_a), lambda i, j: (g0 + i, 0, 0)),
            pl.BlockSpec((nb, HIST_B_ROWS, d_b), lambda i, j: (g0 + i, 0, 0)),
            const((n_exp, 1)),
            const((m, m)),
            const((1, d)), const(win.shape), const(caw.shape), const(cbw.shape), const((1, d_b)),
            const((1, d_b)), const((1, d_b)), const((1, d_a)), const((1, d_b)), const(wout.shape),
            const((1, d)), const(wrt.shape), const((n_exp, 1)),
        ],
        out_specs=(
            pl.BlockSpec((nb, ts, d), lambda i, j: (i, j, 0)),
            pl.BlockSpec((m, d // 2), lambda i, j: (i * (seq // ts) + j, 0)),
            pl.BlockSpec((TOP_K, m), tok_map),
            pl.BlockSpec((TOP_K, m), tok_map),
            pl.BlockSpec((TOP_K, m), tok_map),
            pl.BlockSpec((n_exp, 1), lambda i, j: (0, 0)),
            pl.BlockSpec((nb, CONV_A_W - 1, d_a), lambda i, j: (i, 0, 0)),
            pl.BlockSpec((nb, CONV_B_W - 1, d_b), lambda i, j: (i, 0, 0)),
        ),
        out_shape=out_shape,
        scratch_shapes=[
            pltpu.VMEM((nb, HIST_A_ROWS + ts, d_a), jnp.float32),
            pltpu.VMEM((nb, HIST_B_ROWS + ts, d_b), jnp.float32),
            pltpu.VMEM((nb, ts, d_b), jnp.float32),
            pltpu.VMEM((n_exp, 1), jnp.float32),
        ],
        compiler_params=pltpu.CompilerParams(
            dimension_semantics=("arbitrary", "arbitrary"), vmem_limit_bytes=VMEM_LIMIT),
        name="mixer",
    )(x, hist_a, hist_b, cnt0, tri, gmix, win, caw, cbw, cbb, lng, lnb, ga, gb, wout, gffn, wrt, br)


def _positions_kernel(start_ref, *refs):
    n_parts = len(refs) // 3
    for p in range(n_parts):
        idx = refs[2 * p][...]
        pos = refs[2 * p + 1][...]
        for e in range(start_ref.shape[0]):
            pos = pos + jnp.where(idx == e, start_ref[e], 0)
        refs[2 * n_parts + p][...] = pos


def _positions(pad_start, parts):
    n_0 = parts[0][0].shape[1]
    tile = min(n_0, POS_TILE)
    specs = [pl.BlockSpec((TOP_K, tile), lambda i: (0, i))]
    specs += [pl.BlockSpec((TOP_K, idx.shape[1]), lambda i: (0, 0)) for idx, _ in parts[1:]]
    in_specs = [pl.BlockSpec(memory_space=pltpu.SMEM)]
    args = [pad_start]
    for spec, (idx, rank) in zip(specs, parts):
        in_specs += [spec, spec]
        args += [idx, rank]
    out = pl.pallas_call(
        _positions_kernel,
        grid=(n_0 // tile,),
        in_specs=in_specs,
        out_specs=tuple(specs),
        out_shape=tuple(jax.ShapeDtypeStruct(idx.shape, jnp.int32) for idx, _ in parts),
        compiler_params=pltpu.CompilerParams(dimension_semantics=("arbitrary",)),
        name="positions",
    )(*args)
    return list(out)


def _sc_mesh():
    return plsc.VectorSubcoreMesh(core_axis_name="c", subcore_axis_name="s",
                                  num_cores=SC_CORES, num_subcores=SC_SUBCORES)


def _sc_worker_id():
    return lax.axis_index("s") * SC_CORES + lax.axis_index("c")


def _sc_chunks(n_tok, wid):
    n_chunks = n_tok // SC_CHUNK
    assert n_tok % SC_CHUNK == 0
    if n_chunks % SC_WORKERS == 0:
        per_w = n_chunks // SC_WORKERS
        return None, per_w, lambda i: (wid * per_w + i) * SC_CHUNK
    assert n_chunks <= SC_WORKERS
    return wid < n_chunks, 1, lambda i: wid * SC_CHUNK


def _sc_for_each_chunk(n_tok, fn):
    wid = _sc_worker_id()
    pred, trips, base = _sc_chunks(n_tok, wid)

    def run():
        @pl.loop(0, trips)
        def _(i):
            fn(pl.multiple_of(base(i), SC_CHUNK))

    if pred is None:
        run()
    else:
        pl.when(pred)(run)


def _sc_dispatch(parts, n_rows):
    half = parts[0][1].shape[1]
    n_parts = len(parts)

    @functools.partial(
        pl.kernel, mesh=_sc_mesh(),
        out_type=jax.ShapeDtypeStruct((n_rows, half), jnp.uint32),
        scratch_types=[pltpu.VMEM((TOP_K, SC_CHUNK), jnp.int32),
                       pltpu.VMEM((SC_CHUNK, half), jnp.uint32),
                       pltpu.SemaphoreType.DMA],
        compiler_params=pltpu.CompilerParams(use_tc_tiling_on_sc=True),
        name="sc_dispatch")
    def k(*refs):
        xs_hbm, idx_v, rows_v, sem = refs[2 * n_parts:]

        def move(pos_hbm, src_hbm):
            def fn(base):
                pltpu.sync_copy(pos_hbm.at[:, pl.ds(base, SC_CHUNK)], idx_v)
                pltpu.sync_copy(src_hbm.at[pl.ds(base, SC_CHUNK)], rows_v)
                copies = [pltpu.async_copy(rows_v, xs_hbm.at[idx_v.at[j]], sem) for j in range(TOP_K)]
                for c in copies:
                    c.wait()
            return fn

        for p, (_, hnp) in enumerate(parts):
            _sc_for_each_chunk(hnp.shape[0], move(refs[2 * p], refs[2 * p + 1]))

    return k(*[a for part in parts for a in part])


def _sc_gather(pos_parts, rows):
    half = rows.shape[1]
    n_parts = len(pos_parts)

    @functools.partial(
        pl.kernel, mesh=_sc_mesh(),
        out_type=tuple(jax.ShapeDtypeStruct((TOP_K, p.shape[1], half), jnp.uint32) for p in pos_parts),
        scratch_types=[pltpu.VMEM((TOP_K, SC_CHUNK), jnp.int32),
                       pltpu.VMEM((SC_CHUNK, half), jnp.uint32),
                       pltpu.SemaphoreType.DMA],
        compiler_params=pltpu.CompilerParams(use_tc_tiling_on_sc=True),
        name="sc_gather")
    def k(*refs):
        pos_hbms, rows_hbm = refs[:n_parts], refs[n_parts]
        out_hbms = refs[n_parts + 1:2 * n_parts + 1]
        idx_v, buf_v, sem = refs[2 * n_parts + 1:]

        def move(pos_hbm, out_hbm):
            def fn(base):
                pltpu.sync_copy(pos_hbm.at[:, pl.ds(base, SC_CHUNK)], idx_v)
                for j in range(TOP_K):
                    pltpu.async_copy(rows_hbm.at[idx_v.at[j]], buf_v, sem).wait()
                    pltpu.sync_copy(buf_v, out_hbm.at[j, pl.ds(base, SC_CHUNK)])
            return fn

        for pos_hbm, out_hbm, p in zip(pos_hbms, out_hbms, pos_parts):
            _sc_for_each_chunk(p.shape[1], move(pos_hbm, out_hbm))

    out = k(*pos_parts, rows)
    return out if isinstance(out, (tuple, list)) else (out,)


def _experts_kernel(blk_exp_ref, first_ref, slot_ref, next_ref, n_active_ref,
                    xs_ref, wgu_hbm, bgu_ref, wd_hbm, bd_ref, out_ref,
                    wgu_f32, wd_f32, wgu_bf, wd_bf, sems):
    d_ff = wd_bf.shape[0]

    def fetch(e, sl):
        return (pltpu.make_async_copy(wgu_hbm.at[e], wgu_f32.at[sl], sems.at[0, sl]),
                pltpu.make_async_copy(wd_hbm.at[e], wd_f32.at[sl], sems.at[1, sl]))

    def load_weights_if_first(i):
        sl = slot_ref[i]
        e = blk_exp_ref[i]

        @pl.when(first_ref[i] == 1)
        def _():
            @pl.when(i == 0)
            def _():
                for c in fetch(e, sl):
                    c.start()

            for c in fetch(e, sl):
                c.wait()
            wgu_bf[...] = wgu_f32[sl].astype(jnp.bfloat16)
            wd_bf[...] = wd_f32[sl].astype(jnp.bfloat16)

            @pl.when(next_ref[i] >= 0)
            def _():
                for c in fetch(next_ref[i], 1 - sl):
                    c.start()

    def compute(rows, e):
        x = _unpack_bf16_pairs(xs_ref[rows, :]).astype(jnp.bfloat16)
        gu = jnp.dot(x, wgu_bf[...], preferred_element_type=jnp.float32) + bgu_ref[e]
        g = jnp.minimum(gu[:, :d_ff], SWIGLU_LIMIT)
        u = jnp.clip(gu[:, d_ff:], -SWIGLU_LIMIT, SWIGLU_LIMIT)
        act = g * jax.nn.sigmoid(SWIGLU_ALPHA * g) * (u + 1.0)
        o = jnp.dot(act.astype(jnp.bfloat16), wd_bf[...], preferred_element_type=jnp.float32) + bd_ref[e]
        out_ref[rows, :] = _pack_bf16_pairs(o)

    i0 = pl.program_id(0) * BLOCKS_PER_STEP
    n_active = n_active_ref[0]

    def run(sub0, n_sub):
        first = i0 + sub0
        rows = slice(sub0 * MOE_BLOCK, (sub0 + n_sub) * MOE_BLOCK)
        if n_sub == 1:
            @pl.when(first < n_active)
            def _():
                load_weights_if_first(first)
                compute(rows, blk_exp_ref[first])

            @pl.when(first >= n_active)
            def _():
                out_ref[rows, :] = jnp.zeros((MOE_BLOCK, out_ref.shape[1]), out_ref.dtype)
            return

        same = first + n_sub - 1 < n_active
        for sub in range(1, n_sub):
            same = same & (blk_exp_ref[first + sub] == blk_exp_ref[first])

        @pl.when(same)
        def _():
            load_weights_if_first(first)
            compute(rows, blk_exp_ref[first])

        @pl.when(jnp.logical_not(same))
        def _():
            run(sub0, n_sub // 2)
            run(sub0 + n_sub // 2, n_sub // 2)

    run(0, BLOCKS_PER_STEP)


def _experts(plan, xs, wgu, bgu, wd, bd):
    blk_exp, first, slot, nxt, n_active = plan
    n_rows, half = xs.shape
    n_exp, d, two_f = wgu.shape
    d_ff = wd.shape[1]
    n_blocks = n_rows // MOE_BLOCK
    step_rows = MOE_BLOCK * BLOCKS_PER_STEP
    assert n_rows % step_rows == 0
    grid_spec = pltpu.PrefetchScalarGridSpec(
        num_scalar_prefetch=5,
        grid=(n_blocks // BLOCKS_PER_STEP,),
        in_specs=[
            pl.BlockSpec((step_rows, half), lambda i, *_: (i, 0)),
            pl.BlockSpec(memory_space=pl.ANY),
            pl.BlockSpec((n_exp, 1, two_f), lambda i, *_: (0, 0, 0)),
            pl.BlockSpec(memory_space=pl.ANY),
            pl.BlockSpec((n_exp, 1, d), lambda i, *_: (0, 0, 0)),
        ],
        out_specs=pl.BlockSpec((step_rows, half), lambda i, *_: (i, 0)),
        scratch_shapes=[
            pltpu.VMEM((2, d, two_f), jnp.float32),
            pltpu.VMEM((2, d_ff, d), jnp.float32),
            pltpu.VMEM((d, two_f), jnp.bfloat16),
            pltpu.VMEM((d_ff, d), jnp.bfloat16),
            pltpu.SemaphoreType.DMA((2, 2)),
        ],
    )
    return pl.pallas_call(
        _experts_kernel,
        grid_spec=grid_spec,
        out_shape=jax.ShapeDtypeStruct((n_rows, half), jnp.uint32),
        compiler_params=pltpu.CompilerParams(
            dimension_semantics=("arbitrary",), vmem_limit_bytes=VMEM_LIMIT),
        name="experts",
    )(blk_exp, first, slot, nxt, n_active, xs, wgu, bgu.reshape(n_exp, 1, two_f), wd,
      bd.reshape(n_exp, 1, d))


def _combine_kernel(rows_ref, gate_ref, h_ref, gfin_ref, *rest):
    y_ref = rest[-1]
    acc = h_ref[...]
    for k in range(TOP_K):
        acc = acc + gate_ref[:, k:k + 1] * _unpack_bf16_pairs(rows_ref[k])
    y_ref[...] = _rms(acc, gfin_ref[...])


def _combine(rows4, gate_t, h2d, gfin, *, tok0=0, y_prev=None, y_rows=None, y_tok0=0):
    d = h2d.shape[1]
    n, half = rows4.shape[1:]
    n_tok = h2d.shape[0] if y_rows is None else y_rows
    tile = min(n, 256)
    t0 = tok0 // tile
    y0 = y_tok0 // tile
    in_specs = [
        pl.BlockSpec((TOP_K, tile, half), lambda i: (0, i, 0)),
        pl.BlockSpec((tile, TOP_K), lambda i: (t0 + i, 0)),
        pl.BlockSpec((tile, d), lambda i: (t0 + i, 0)),
        pl.BlockSpec((1, d), lambda i: (0, 0)),
    ]
    args = [rows4, gate_t, h2d, gfin]
    aliases = {}
    if y_prev is not None:
        in_specs.append(pl.BlockSpec(memory_space=pl.ANY))
        args.append(y_prev)
        aliases = {4: 0}
    return pl.pallas_call(
        _combine_kernel,
        grid=(n // tile,),
        in_specs=in_specs,
        out_specs=pl.BlockSpec((tile, d), lambda i: (y0 + i, 0)),
        out_shape=jax.ShapeDtypeStruct((n_tok, d), jnp.float32),
        input_output_aliases=aliases,
        compiler_params=pltpu.CompilerParams(dimension_semantics=("arbitrary",)),
        name="combine",
    )(*args)


def _pad_hist(hist, rows):
    return jnp.pad(hist, ((0, 0), (rows - hist.shape[1], 0), (0, 0)))


def kernel(x_prompt, x_sample, state_conv_a, state_conv_b, norm_mix_g, w_in, conv_a_w, conv_b_w, conv_b_b, conv_ln_g, conv_ln_b, out_norm_a_g, out_norm_b_g, w_out, norm_ffn_g, w_router, b_router, w_gate_up, b_gate_up, w_down, b_down, final_norm_g):
    depth = w_in.shape[0]
    assert depth == 1, "single-layer trunk"
    bf16 = jnp.bfloat16
    bsz, seq, d = x_prompt.shape
    dec_b, dec_s, _ = x_sample.shape
    d_a, d_b = conv_a_w.shape[-1], conv_b_w.shape[-1]
    n_exp = w_router.shape[-1]

    params = (norm_mix_g[0][None], w_in[0].astype(bf16), conv_a_w[0], conv_b_w[0], conv_b_b[0][None],
              conv_ln_g[0][None], conv_ln_b[0][None], out_norm_a_g[0][None], out_norm_b_g[0][None],
              w_out[0].astype(bf16), norm_ffn_g[0][None], w_router[0].T.astype(bf16),
              b_router[0][:, None])

    zero_a = jnp.zeros((bsz, HIST_A_ROWS, d_a), jnp.float32)
    zero_b = jnp.zeros((bsz, HIST_B_ROWS, d_b), jnp.float32)
    cnt0 = jnp.zeros((n_exp, 1), jnp.float32)
    gfin = final_norm_g[None]
    ts = min(seq, MIXER_ROWS)
    n_p = bsz * seq

    sizes = [bsz * f // sum(GROUP_SPLIT) for f in GROUP_SPLIT]
    if bsz % sum(GROUP_SPLIT) != 0:
        sizes = [bsz]
    groups = []
    tok_start = []
    for g, per_b in enumerate(sizes):
        b0 = sum(sizes[:g])
        tok_start.append(b0 * seq)
        h, hnp, idx, gate, rank, cnt, na, nb_ = _mixer(
            x_prompt, zero_a, zero_b, cnt0, params, nb=1, ts=ts, b0=b0, bsz=per_b)
        parts = [dict(h=h.reshape(per_b * seq, d), hnp=hnp, idx=idx, gate_t=gate.T, rank=rank)]
        states = [(na, nb_)]
        if g == len(sizes) - 1:
            h, hnp, idx, gate, rank, cnt, na_s, nb_s = _mixer(
                x_sample, _pad_hist(state_conv_a[0], HIST_A_ROWS), _pad_hist(state_conv_b[0], HIST_B_ROWS),
                cnt, params, nb=dec_b, ts=dec_s)
            parts.append(dict(h=h.reshape(dec_b * dec_s, d), hnp=hnp, idx=idx, gate_t=gate.T, rank=rank))
        n_pairs = sum(p["hnp"].shape[0] for p in parts) * TOP_K
        plan, pad_start, n_blocks = _block_plan(cnt[:, 0].astype(jnp.int32), n_pairs, n_exp)
        pos = _positions(pad_start, [(p["idx"], p["rank"]) for p in parts])
        xs = _sc_dispatch([(q, p["hnp"]) for q, p in zip(pos, parts)], n_blocks * MOE_BLOCK)
        groups.append(dict(parts=parts, pos=pos, plan=plan, xs=xs, states=states))

    for grp in groups:
        grp["rows"] = _experts(grp["plan"], grp["xs"], w_gate_up[0], b_gate_up[0], w_down[0], b_down[0])

    y_p = None
    for g, grp in enumerate(groups):
        main, pos0 = grp["parts"][0], grp["pos"][0]
        n_g = main["h"].shape[0]
        n_chunks = COMBINE_CHUNKS if n_g % (COMBINE_CHUNKS * SC_CHUNK * SC_WORKERS) == 0 else 1
        per = n_g // n_chunks
        for c in range(n_chunks):
            extra = grp["pos"][1:] if c == 0 else []
            got = _sc_gather([pos0[:, c * per:(c + 1) * per]] + extra, grp["rows"])
            if extra:
                smp = grp["parts"][1]
                y_s = _combine(got[1], smp["gate_t"], smp["h"], gfin).reshape(dec_b, dec_s, d)
            y_p = _combine(got[0], main["gate_t"], main["h"], gfin, tok0=c * per,
                           y_prev=y_p, y_rows=n_p, y_tok0=tok_start[g] + c * per)
    y_p = y_p.reshape(bsz, seq, d)

    na_p = jnp.concatenate([grp["states"][0][0] for grp in groups], axis=0)
    nb_p = jnp.concatenate([grp["states"][0][1] for grp in groups], axis=0)
    return (y_p, y_s, na_p[None], nb_p[None], na_s[None], nb_s[None])


def _block_plan(counts, n_pairs, n_exp):
    padded = (counts + MOE_BLOCK - 1) // MOE_BLOCK * MOE_BLOCK
    pad_end = jnp.cumsum(padded)
    pad_start = pad_end - padded
    n_blocks = -(-n_pairs // MOE_BLOCK) + n_exp
    n_blocks = -(-n_blocks // BLOCKS_PER_STEP) * BLOCKS_PER_STEP
    blk_start = jnp.arange(n_blocks, dtype=jnp.int32) * MOE_BLOCK
    blk_exp = jnp.minimum(jnp.sum(blk_start[:, None] >= pad_end[None, :], axis=1),
                          n_exp - 1).astype(jnp.int32)
    n_active = (pad_end[-1:] // MOE_BLOCK).astype(jnp.int32)
    e_ids = jnp.arange(n_exp, dtype=jnp.int32)
    has = padded > 0
    slot_e = (jnp.cumsum(has.astype(jnp.int32)) - 1) % 2
    later = jnp.where(has, e_ids, n_exp)
    next_e = jnp.concatenate([lax.cummin(later, reverse=True)[1:], jnp.full((1,), n_exp, jnp.int32)])
    next_e = jnp.where(next_e >= n_exp, -1, next_e)
    blk_onehot = blk_exp[:, None] == e_ids[None, :]

    def per_block(table):
        return jnp.sum(jnp.where(blk_onehot, table[None, :], 0), axis=1).astype(jnp.int32)

    blk_first = ((blk_start == per_block(pad_start)) & (blk_start < pad_end[-1])).astype(jnp.int32)
    plan = (blk_exp, blk_first, per_block(slot_e), per_block(next_e), n_active)
    return plan, pad_start.astype(jnp.int32), n_blocks
```

```python
import functools

import jax
import jax.numpy as jnp
from jax import lax
from jax.experimental import pallas as pl
from jax.experimental.pallas import tpu as pltpu
from jax.experimental.pallas import tpu_sc as plsc

SC_CORES = 2
SC_SUBCORES = 16
SC_WORKERS = SC_CORES * SC_SUBCORES
SC_CHUNK = 128
CONV_A_W = 3
CONV_B_W = 31
TOP_K = 4
SWIGLU_LIMIT = 7.0
SWIGLU_ALPHA = 1.702
NORM_EPS = 1e-5
MOE_BLOCK = 256
BLOCKS_PER_STEP = 4
SUBLANES = 8
LANES = 128
CONV_ROW_CHUNK = 128
HIST_A_ROWS = 8
HIST_B_ROWS = 32
MIXER_ROWS = 512
PROJ_SECTION = 256
POS_TILE = 4096
GROUP_SPLIT = (3, 1)
COMBINE_CHUNKS = 2
VMEM_LIMIT = 56 * 1024 * 1024


def _rms(x, g):
    return x * lax.rsqrt(jnp.mean(x * x, axis=-1, keepdims=True) + NORM_EPS) * g


def _zero_tile_from(v):
    bits = pltpu.bitcast(v[0:SUBLANES, 0:LANES], jnp.uint32)
    return pltpu.bitcast((bits >> 16) >> 16, jnp.float32)


def _pack_bf16_pairs(a):
    w = a.shape[-1] // 2
    lo = pltpu.bitcast(a[:, :w].astype(jnp.bfloat16).astype(jnp.float32), jnp.uint32)
    hi = pltpu.bitcast(a[:, w:].astype(jnp.bfloat16).astype(jnp.float32), jnp.uint32)
    return (lo >> 16) | (hi & jnp.uint32(0xFFFF0000))


def _unpack_bf16_pairs(wd):
    lo = pltpu.bitcast(wd << 16, jnp.float32)
    hi = pltpu.bitcast(wd & jnp.uint32(0xFFFF0000), jnp.float32)
    return jnp.concatenate([lo, hi], axis=-1)


def _conv_b_chunk(ub_buf, cbw_ref, n_i, r0, rc, c0):
    cs = slice(c0, c0 + LANES)
    lead = HIST_B_ROWS - (CONV_B_W - 1)
    acc = None
    for b in range(SUBLANES):
        rows = rc if b == 0 else rc + SUBLANES
        q = None
        for a in range((HIST_B_ROWS + SUBLANES) // SUBLANES):
            k = SUBLANES * a + b - lead
            if k < 0 or k >= CONV_B_W:
                continue
            start = r0 + SUBLANES * a
            term = cbw_ref[k:k + 1, cs] * ub_buf[n_i, start:start + rows, cs]
            q = term if q is None else q + term
        part = q if b == 0 else q[b:b + rc]
        acc = part if acc is None else acc + part
    return acc


def _mixer_kernel(x_ref, hista_ref, histb_ref, cnt0_ref, tri_ref,
                  gmix_ref, win_ref, caw_ref, cbw_ref, cbb_ref, lng_ref, lnb_ref,
                  ga_ref, gb_ref, wout_ref, gffn_ref, wrt_ref, br_ref,
                  h_ref, hnp_ref, idx_ref, gate_ref, rank_ref, cnt_ref, newa_ref, newb_ref,
                  ua_buf, ub_buf, cb_buf, cnt_acc, *, nb, ts, d_a, d_b, n_exp):
    b = pl.program_id(0)
    s = pl.program_id(1)
    m = nb * ts
    d = x_ref.shape[-1]

    @pl.when(s == 0)
    def _():
        ua_buf[:, 0:HIST_A_ROWS, :] = hista_ref[...]
        ub_buf[:, 0:HIST_B_ROWS, :] = histb_ref[...]

    @pl.when(s != 0)
    def _():
        ua_buf[:, 0:HIST_A_ROWS, :] = ua_buf[:, ts:ts + HIST_A_ROWS, :]
        ub_buf[:, 0:HIST_B_ROWS, :] = ub_buf[:, ts:ts + HIST_B_ROWS, :]

    @pl.when((b == 0) & (s == 0))
    def _():
        cnt_acc[...] = cnt0_ref[...]

    x = x_ref[...].reshape(m, d)
    n = _rms(x, gmix_ref[...]).astype(jnp.bfloat16)
    def proj(c0, width):
        return jnp.dot(n, win_ref[:, c0:c0 + width], preferred_element_type=jnp.float32)

    sec = min(d_b, PROJ_SECTION)
    for c0 in range(0, d_b, sec):
        v_b = proj(3 * d_a + c0, sec)
        g_b = proj(3 * d_a + d_b + c0, sec)
        ub_buf[:, HIST_B_ROWS:HIST_B_ROWS + ts, c0:c0 + sec] = (v_b * jax.nn.sigmoid(g_b)).reshape(nb, ts, sec)

    gate_c = proj(d_a, d_a)
    xt = proj(2 * d_a, d_a)
    gate_b = proj(0, d_a)
    ua_buf[:, HIST_A_ROWS:HIST_A_ROWS + ts, :] = (gate_c * xt).reshape(nb, ts, d_a)
    conv_a = jnp.zeros((nb, ts, d_a), jnp.float32)
    for k in range(CONV_A_W):
        off = HIST_A_ROWS - (CONV_A_W - 1) + k
        conv_a = conv_a + caw_ref[k:k + 1, :] * ua_buf[:, off:off + ts, :]
    y_a = gate_b * conv_a.reshape(m, d_a)

    mix_a = jnp.dot(_rms(y_a, ga_ref[...]).astype(jnp.bfloat16), wout_ref[0:d_a, :],
                    preferred_element_type=jnp.float32)

    rc = min(ts, CONV_ROW_CHUNK)
    chunks = [(n_i, r0, c0) for n_i in range(nb) for r0 in range(0, ts, rc) for c0 in range(0, d_b, LANES)]
    anchors = {len(chunks) * 2 // 8: gate_c, len(chunks) * 3 // 8: xt, len(chunks) * 5 // 8: gate_b,
               len(chunks) - 1: mix_a}
    for ci, (n_i, r0, c0) in enumerate(chunks):
        acc = _conv_b_chunk(ub_buf, cbw_ref, n_i, r0, rc, c0)
        if ci in anchors:
            acc = (acc.reshape(rc // SUBLANES, SUBLANES, LANES) + _zero_tile_from(anchors[ci])[None]
                   ).reshape(rc, LANES)
        cb_buf[n_i, r0:r0 + rc, c0:c0 + LANES] = acc
    cb = cb_buf[...].reshape(m, d_b) + cbb_ref[...]
    mu = jnp.mean(cb, axis=-1, keepdims=True)
    xc = cb - mu
    var = jnp.mean(xc * xc, axis=-1, keepdims=True)
    ln = xc * lax.rsqrt(var + NORM_EPS) * lng_ref[...] + lnb_ref[...]
    y_b = ln * jax.nn.sigmoid(ln)

    mix_b = jnp.dot(_rms(y_b, gb_ref[...]).astype(jnp.bfloat16), wout_ref[d_a:d_a + d_b, :],
                    preferred_element_type=jnp.float32)
    h = x + (mix_a + mix_b)
    h_ref[...] = h.reshape(nb, ts, d)

    hn = _rms(h, gffn_ref[...])
    hnp_ref[...] = _pack_bf16_pairs(hn)

    logits = lax.dot_general(wrt_ref[...], hn.astype(jnp.bfloat16), (((1,), (1,)), ((), ())),
                             preferred_element_type=jnp.float32) + br_ref[...]
    e_iota = lax.broadcasted_iota(jnp.int32, (n_exp, m), 0)
    cur = logits
    vals, idxs, sels = [], [], []
    for _ in range(TOP_K):
        mx = jnp.max(cur, axis=0, keepdims=True)
        ix = jnp.min(jnp.where(cur == mx, e_iota, n_exp), axis=0, keepdims=True)
        sel = e_iota == ix
        vals.append(mx)
        idxs.append(ix)
        sels.append(sel)
        cur = jnp.where(sel, -jnp.inf, cur)
    exps = [jnp.exp(v - vals[0]) for v in vals]
    denom = exps[0] + exps[1] + exps[2] + exps[3]
    gates = [e / denom for e in exps]

    onehot = jnp.zeros((n_exp, m), jnp.float32)
    for sel in sels:
        onehot = onehot + sel.astype(jnp.float32)
    before = jnp.dot(onehot.astype(jnp.bfloat16), tri_ref[...],
                     preferred_element_type=jnp.float32) + cnt_acc[...]
    ranks = [jnp.sum(jnp.where(sel, before, 0.0), axis=0, keepdims=True) for sel in sels]
    new_cnt = cnt_acc[...] + jnp.sum(onehot, axis=1, keepdims=True)
    cnt_acc[...] = new_cnt
    cnt_ref[...] = new_cnt

    idx_ref[...] = jnp.concatenate(idxs, axis=0)
    gate_ref[...] = jnp.concatenate(gates, axis=0)
    rank_ref[...] = jnp.concatenate(ranks, axis=0).astype(jnp.int32)

    @pl.when(s == pl.num_programs(1) - 1)
    def _():
        newa_ref[...] = ua_buf[:, ts + HIST_A_ROWS - (CONV_A_W - 1):ts + HIST_A_ROWS, :]
        newb_ref[...] = ub_buf[:, ts + HIST_B_ROWS - (CONV_B_W - 1):ts + HIST_B_ROWS, :]


def _mixer(x, hist_a, hist_b, cnt0, params, *, nb, ts, b0=0, bsz=None):
    (gmix, win, caw, cbw, cbb, lng, lnb, ga, gb, wout, gffn, wrt, br) = params
    _, seq, d = x.shape
    bsz = x.shape[0] if bsz is None else bsz
    d_a, d_b = caw.shape[-1], cbw.shape[-1]
    n_exp = wrt.shape[0]
    m = nb * ts
    n_tok = bsz * seq
    g0 = b0 // nb
    grid = (bsz // nb, seq // ts)
    tri = (jnp.arange(m)[:, None] < jnp.arange(m)[None, :]).astype(jnp.bfloat16)

    def const(shape):
        return pl.BlockSpec(shape, lambda i, j: (0,) * len(shape))

    tok_map = lambda i, j: (0, i * (seq // ts) + j)
    kern = functools.partial(_mixer_kernel, nb=nb, ts=ts, d_a=d_a, d_b=d_b, n_exp=n_exp)
    out_shape = (
        jax.ShapeDtypeStruct((bsz, seq, d), jnp.float32),
        jax.ShapeDtypeStruct((n_tok, d // 2), jnp.uint32),
        jax.ShapeDtypeStruct((TOP_K, n_tok), jnp.int32),
        jax.ShapeDtypeStruct((TOP_K, n_tok), jnp.float32),
        jax.ShapeDtypeStruct((TOP_K, n_tok), jnp.int32),
        jax.ShapeDtypeStruct((n_exp, 1), jnp.float32),
        jax.ShapeDtypeStruct((bsz, CONV_A_W - 1, d_a), jnp.float32),
        jax.ShapeDtypeStruct((bsz, CONV_B_W - 1, d_b), jnp.float32),
    )
    return pl.pallas_call(
        kern,
        grid=grid,
        in_specs=[
            pl.BlockSpec((nb, ts, d), lambda i, j: (g0 + i, j, 0)),
            pl.BlockSpec((nb, HIST_A_ROWS, d_a), lambda i, j: (g0 + i, 0, 0)),
            pl.BlockSpec((nb, HIST_B_ROWS, d_b), lambda i, j: (g0 + i, 0, 0)),
            const((n_exp, 1)),
            const((m, m)),
            const((1, d)), const(win.shape), const(caw.shape), const(cbw.shape), const((1, d_b)),
            const((1, d_b)), const((1, d_b)), const((1, d_a)), const((1, d_b)), const(wout.shape),
            const((1, d)), const(wrt.shape), const((n_exp, 1)),
        ],
        out_specs=(
            pl.BlockSpec((nb, ts, d), lambda i, j: (i, j, 0)),
            pl.BlockSpec((m, d // 2), lambda i, j: (i * (seq // ts) + j, 0)),
            pl.BlockSpec((TOP_K, m), tok_map),
            pl.BlockSpec((TOP_K, m), tok_map),
            pl.BlockSpec((TOP_K, m), tok_map),
            pl.BlockSpec((n_exp, 1), lambda i, j: (0, 0)),
            pl.BlockSpec((nb, CONV_A_W - 1, d_a), lambda i, j: (i, 0, 0)),
            pl.BlockSpec((nb, CONV_B_W - 1, d_b), lambda i, j: (i, 0, 0)),
        ),
        out_shape=out_shape,
        scratch_shapes=[
            pltpu.VMEM((nb, HIST_A_ROWS + ts, d_a), jnp.float32),
            pltpu.VMEM((nb, HIST_B_ROWS + ts, d_b), jnp.float32),
            pltpu.VMEM((nb, ts, d_b), jnp.float32),
            pltpu.VMEM((n_exp, 1), jnp.float32),
        ],
        compiler_params=pltpu.CompilerParams(
            dimension_semantics=("arbitrary", "arbitrary"), vmem_limit_bytes=VMEM_LIMIT),
        name="mixer",
    )(x, hist_a, hist_b, cnt0, tri, gmix, win, caw, cbw, cbb, lng, lnb, ga, gb, wout, gffn, wrt, br)


def _positions_kernel(start_ref, *refs):
    n_parts = len(refs) // 3
    for p in range(n_parts):
        idx = refs[2 * p][...]
        pos = refs[2 * p + 1][...]
        for e in range(start_ref.shape[0]):
            pos = pos + jnp.where(idx == e, start_ref[e], 0)
        refs[2 * n_parts + p][...] = pos


def _positions(pad_start, parts):
    n_0 = parts[0][0].shape[1]
    tile = min(n_0, POS_TILE)
    specs = [pl.BlockSpec((TOP_K, tile), lambda i: (0, i))]
    specs += [pl.BlockSpec((TOP_K, idx.shape[1]), lambda i: (0, 0)) for idx, _ in parts[1:]]
    in_specs = [pl.BlockSpec(memory_space=pltpu.SMEM)]
    args = [pad_start]
    for spec, (idx, rank) in zip(specs, parts):
        in_specs += [spec, spec]
        args += [idx, rank]
    out = pl.pallas_call(
        _positions_kernel,
        grid=(n_0 // tile,),
        in_specs=in_specs,
        out_specs=tuple(specs),
        out_shape=tuple(jax.ShapeDtypeStruct(idx.shape, jnp.int32) for idx, _ in parts),
        compiler_params=pltpu.CompilerParams(dimension_semantics=("arbitrary",)),
        name="positions",
    )(*args)
    return list(out)


def _sc_mesh():
    return plsc.VectorSubcoreMesh(core_axis_name="c", subcore_axis_name="s",
                                  num_cores=SC_CORES, num_subcores=SC_SUBCORES)


def _sc_worker_id():
    return lax.axis_index("s") * SC_CORES + lax.axis_index("c")


def _sc_chunks(n_tok, wid):
    n_chunks = n_tok // SC_CHUNK
    assert n_tok % SC_CHUNK == 0
    if n_chunks % SC_WORKERS == 0:
        per_w = n_chunks // SC_WORKERS
        return None, per_w, lambda i: (wid * per_w + i) * SC_CHUNK
    assert n_chunks <= SC_WORKERS
    return wid < n_chunks, 1, lambda i: wid * SC_CHUNK


def _sc_for_each_chunk(n_tok, fn):
    wid = _sc_worker_id()
    pred, trips, base = _sc_chunks(n_tok, wid)

    def run():
        @pl.loop(0, trips)
        def _(i):
            fn(pl.multiple_of(base(i), SC_CHUNK))

    if pred is None:
        run()
    else:
        pl.when(pred)(run)


def _sc_dispatch(parts, n_rows):
    half = parts[0][1].shape[1]
    n_parts = len(parts)

    @functools.partial(
        pl.kernel, mesh=_sc_mesh(),
        out_type=jax.ShapeDtypeStruct((n_rows, half), jnp.uint32),
        scratch_types=[pltpu.VMEM((TOP_K, SC_CHUNK), jnp.int32),
                       pltpu.VMEM((SC_CHUNK, half), jnp.uint32),
                       pltpu.SemaphoreType.DMA],
        compiler_params=pltpu.CompilerParams(use_tc_tiling_on_sc=True),
        name="sc_dispatch")
    def k(*refs):
        xs_hbm, idx_v, rows_v, sem = refs[2 * n_parts:]

        def move(pos_hbm, src_hbm):
            def fn(base):
                pltpu.sync_copy(pos_hbm.at[:, pl.ds(base, SC_CHUNK)], idx_v)
                pltpu.sync_copy(src_hbm.at[pl.ds(base, SC_CHUNK)], rows_v)
                copies = [pltpu.async_copy(rows_v, xs_hbm.at[idx_v.at[j]], sem) for j in range(TOP_K)]
                for c in copies:
                    c.wait()
            return fn

        for p, (_, hnp) in enumerate(parts):
            _sc_for_each_chunk(hnp.shape[0], move(refs[2 * p], refs[2 * p + 1]))

    return k(*[a for part in parts for a in part])


def _sc_gather(pos_parts, rows):
    half = rows.shape[1]
    n_parts = len(pos_parts)

    @functools.partial(
        pl.kernel, mesh=_sc_mesh(),
        out_type=tuple(jax.ShapeDtypeStruct((TOP_K, p.shape[1], half), jnp.uint32) for p in pos_parts),
        scratch_types=[pltpu.VMEM((TOP_K, SC_CHUNK), jnp.int32),
                       pltpu.VMEM((SC_CHUNK, half), jnp.uint32),
                       pltpu.SemaphoreType.DMA],
        compiler_params=pltpu.CompilerParams(use_tc_tiling_on_sc=True),
        name="sc_gather")
    def k(*refs):
        pos_hbms, rows_hbm = refs[:n_parts], refs[n_parts]
        out_hbms = refs[n_parts + 1:2 * n_parts + 1]
        idx_v, buf_v, sem = refs[2 * n_parts + 1:]

        def move(pos_hbm, out_hbm):
            def fn(base):
                pltpu.sync_copy(pos_hbm.at[:, pl.ds(base, SC_CHUNK)], idx_v)
                for j in range(TOP_K):
                    pltpu.async_copy(rows_hbm.at[idx_v.at[j]], buf_v, sem).wait()
                    pltpu.sync_copy(buf_v, out_hbm.at[j, pl.ds(base, SC_CHUNK)])
            return fn

        for pos_hbm, out_hbm, p in zip(pos_hbms, out_hbms, pos_parts):
            _sc_for_each_chunk(p.shape[1], move(pos_hbm, out_hbm))

    out = k(*pos_parts, rows)
    return out if isinstance(out, (tuple, list)) else (out,)


def _experts_kernel(blk_exp_ref, first_ref, slot_ref, next_ref, n_active_ref,
                    xs_ref, wgu_hbm, bgu_ref, wd_hbm, bd_ref, out_ref,
                    wgu_f32, wd_f32, wgu_bf, wd_bf, sems):
    d_ff = wd_bf.shape[0]

    def fetch(e, sl):
        return (pltpu.make_async_copy(wgu_hbm.at[e], wgu_f32.at[sl], sems.at[0, sl]),
                pltpu.make_async_copy(wd_hbm.at[e], wd_f32.at[sl], sems.at[1, sl]))

    def load_weights_if_first(i):
        sl = slot_ref[i]
        e = blk_exp_ref[i]

        @pl.when(first_ref[i] == 1)
        def _():
            @pl.when(i == 0)
            def _():
                for c in fetch(e, sl):
                    c.start()

            for c in fetch(e, sl):
                c.wait()
            wgu_bf[...] = wgu_f32[sl].astype(jnp.bfloat16)
            wd_bf[...] = wd_f32[sl].astype(jnp.bfloat16)

            @pl.when(next_ref[i] >= 0)
            def _():
                for c in fetch(next_ref[i], 1 - sl):
                    c.start()

    def compute(rows, e):
        x = _unpack_bf16_pairs(xs_ref[rows, :]).astype(jnp.bfloat16)
        gu = jnp.dot(x, wgu_bf[...], preferred_element_type=jnp.float32) + bgu_ref[e]
        g = jnp.minimum(gu[:, :d_ff], SWIGLU_LIMIT)
        u = jnp.clip(gu[:, d_ff:], -SWIGLU_LIMIT, SWIGLU_LIMIT)
        act = g * jax.nn.sigmoid(SWIGLU_ALPHA * g) * (u + 1.0)
        o = jnp.dot(act.astype(jnp.bfloat16), wd_bf[...], preferred_element_type=jnp.float32) + bd_ref[e]
        out_ref[rows, :] = _pack_bf16_pairs(o)

    i0 = pl.program_id(0) * BLOCKS_PER_STEP
    n_active = n_active_ref[0]

    def run(sub0, n_sub):
        first = i0 + sub0
        rows = slice(sub0 * MOE_BLOCK, (sub0 + n_sub) * MOE_BLOCK)
        if n_sub == 1:
            @pl.when(first < n_active)
            def _():
                load_weights_if_first(first)
                compute(rows, blk_exp_ref[first])

            @pl.when(first >= n_active)
            def _():
                out_ref[rows, :] = jnp.zeros((MOE_BLOCK, out_ref.shape[1]), out_ref.dtype)
            return

        same = first + n_sub - 1 < n_active
        for sub in range(1, n_sub):
            same = same & (blk_exp_ref[first + sub] == blk_exp_ref[first])

        @pl.when(same)
        def _():
            load_weights_if_first(first)
            compute(rows, blk_exp_ref[first])

        @pl.when(jnp.logical_not(same))
        def _():
            run(sub0, n_sub // 2)
            run(sub0 + n_sub // 2, n_sub // 2)

    run(0, BLOCKS_PER_STEP)


def _experts(plan, xs, wgu, bgu, wd, bd):
    blk_exp, first, slot, nxt, n_active = plan
    n_rows, half = xs.shape
    n_exp, d, two_f = wgu.shape
    d_ff = wd.shape[1]
    n_blocks = n_rows // MOE_BLOCK
    step_rows = MOE_BLOCK * BLOCKS_PER_STEP
    assert n_rows % step_rows == 0
    grid_spec = pltpu.PrefetchScalarGridSpec(
        num_scalar_prefetch=5,
        grid=(n_blocks // BLOCKS_PER_STEP,),
        in_specs=[
            pl.BlockSpec((step_rows, half), lambda i, *_: (i, 0)),
            pl.BlockSpec(memory_space=pl.ANY),
            pl.BlockSpec((n_exp, 1, two_f), lambda i, *_: (0, 0, 0)),
            pl.BlockSpec(memory_space=pl.ANY),
            pl.BlockSpec((n_exp, 1, d), lambda i, *_: (0, 0, 0)),
        ],
        out_specs=pl.BlockSpec((step_rows, half), lambda i, *_: (i, 0)),
        scratch_shapes=[
            pltpu.VMEM((2, d, two_f), jnp.float32),
            pltpu.VMEM((2, d_ff, d), jnp.float32),
            pltpu.VMEM((d, two_f), jnp.bfloat16),
            pltpu.VMEM((d_ff, d), jnp.bfloat16),
            pltpu.SemaphoreType.DMA((2, 2)),
        ],
    )
    return pl.pallas_call(
        _experts_kernel,
        grid_spec=grid_spec,
        out_shape=jax.ShapeDtypeStruct((n_rows, half), jnp.uint32),
        compiler_params=pltpu.CompilerParams(
            dimension_semantics=("arbitrary",), vmem_limit_bytes=VMEM_LIMIT),
        name="experts",
    )(blk_exp, first, slot, nxt, n_active, xs, wgu, bgu.reshape(n_exp, 1, two_f), wd,
      bd.reshape(n_exp, 1, d))


def _combine_kernel(rows_ref, gate_ref, h_ref, gfin_ref, *rest):
    y_ref = rest[-1]
    acc = h_ref[...]
    for k in range(TOP_K):
        acc = acc + gate_ref[:, k:k + 1] * _unpack_bf16_pairs(rows_ref[k])
    y_ref[...] = _rms(acc, gfin_ref[...])


def _combine(rows4, gate_t, h2d, gfin, *, tok0=0, y_prev=None, y_rows=None, y_tok0=0):
    d = h2d.shape[1]
    n, half = rows4.shape[1:]
    n_tok = h2d.shape[0] if y_rows is None else y_rows
    tile = min(n, 256)
    t0 = tok0 // tile
    y0 = y_tok0 // tile
    in_specs = [
        pl.BlockSpec((TOP_K, tile, half), lambda i: (0, i, 0)),
        pl.BlockSpec((tile, TOP_K), lambda i: (t0 + i, 0)),
        pl.BlockSpec((tile, d), lambda i: (t0 + i, 0)),
        pl.BlockSpec((1, d), lambda i: (0, 0)),
    ]
    args = [rows4, gate_t, h2d, gfin]
    aliases = {}
    if y_prev is not None:
        in_specs.append(pl.BlockSpec(memory_space=pl.ANY))
        args.append(y_prev)
        aliases = {4: 0}
    return pl.pallas_call(
        _combine_kernel,
        grid=(n // tile,),
        in_specs=in_specs,
        out_specs=pl.BlockSpec((tile, d), lambda i: (y0 + i, 0)),
        out_shape=jax.ShapeDtypeStruct((n_tok, d), jnp.float32),
        input_output_aliases=aliases,
        compiler_params=pltpu.CompilerParams(dimension_semantics=("arbitrary",)),
        name="combine",
    )(*args)


def _pad_hist(hist, rows):
    return jnp.pad(hist, ((0, 0), (rows - hist.shape[1], 0), (0, 0)))


def kernel(x_prompt, x_sample, state_conv_a, state_conv_b, norm_mix_g, w_in, conv_a_w, conv_b_w, conv_b_b, conv_ln_g, conv_ln_b, out_norm_a_g, out_norm_b_g, w_out, norm_ffn_g, w_router, b_router, w_gate_up, b_gate_up, w_down, b_down, final_norm_g):
    depth = w_in.shape[0]
    assert depth == 1, "single-layer trunk"
    bf16 = jnp.bfloat16
    bsz, seq, d = x_prompt.shape
    dec_b, dec_s, _ = x_sample.shape
    d_a, d_b = conv_a_w.shape[-1], conv_b_w.shape[-1]
    n_exp = w_router.shape[-1]

    params = (norm_mix_g[0][None], w_in[0].astype(bf16), conv_a_w[0], conv_b_w[0], conv_b_b[0][None],
              conv_ln_g[0][None], conv_ln_b[0][None], out_norm_a_g[0][None], out_norm_b_g[0][None],
              w_out[0].astype(bf16), norm_ffn_g[0][None], w_router[0].T.astype(bf16),
              b_router[0][:, None])

    zero_a = jnp.zeros((bsz, HIST_A_ROWS, d_a), jnp.float32)
    zero_b = jnp.zeros((bsz, HIST_B_ROWS, d_b), jnp.float32)
    cnt0 = jnp.zeros((n_exp, 1), jnp.float32)
    gfin = final_norm_g[None]
    ts = min(seq, MIXER_ROWS)
    n_p = bsz * seq

    sizes = [bsz * f // sum(GROUP_SPLIT) for f in GROUP_SPLIT]
    if bsz % sum(GROUP_SPLIT) != 0:
        sizes = [bsz]
    groups = []
    tok_start = []
    for g, per_b in enumerate(sizes):
        b0 = sum(sizes[:g])
        tok_start.append(b0 * seq)
        cnt_in = cnt0
        if groups:
            cnt_in, _ = lax.optimization_barrier((cnt0, groups[-1]["cnt"]))
        h, hnp, idx, gate, rank, cnt, na, nb_ = _mixer(
            x_prompt, zero_a, zero_b, cnt_in, params, nb=1, ts=ts, b0=b0, bsz=per_b)
        parts = [dict(h=h.reshape(per_b * seq, d), hnp=hnp, idx=idx, gate_t=gate.T, rank=rank)]
        states = [(na, nb_)]
        if g == len(sizes) - 1:
            h, hnp, idx, gate, rank, cnt, na_s, nb_s = _mixer(
                x_sample, _pad_hist(state_conv_a[0], HIST_A_ROWS), _pad_hist(state_conv_b[0], HIST_B_ROWS),
                cnt, params, nb=dec_b, ts=dec_s)
            parts.append(dict(h=h.reshape(dec_b * dec_s, d), hnp=hnp, idx=idx, gate_t=gate.T, rank=rank))
        n_pairs = sum(p["hnp"].shape[0] for p in parts) * TOP_K
        plan, pad_start, n_blocks = _block_plan(cnt[:, 0].astype(jnp.int32), n_pairs, n_exp)
        pos = _positions(pad_start, [(p["idx"], p["rank"]) for p in parts])
        xs = _sc_dispatch([(q, p["hnp"]) for q, p in zip(pos, parts)], n_blocks * MOE_BLOCK)
        groups.append(dict(parts=parts, pos=pos, plan=plan, xs=xs, states=states, cnt=cnt))

    for g, grp in enumerate(groups):
        xs = grp["xs"]
        if g:
            xs, _ = lax.optimization_barrier((xs, groups[g - 1]["rows"]))
        grp["rows"] = _experts(grp["plan"], xs, w_gate_up[0], b_gate_up[0], w_down[0], b_down[0])

    y_p = None
    for g, grp in enumerate(groups):
        main, pos0 = grp["parts"][0], grp["pos"][0]
        n_g = main["h"].shape[0]
        n_chunks = COMBINE_CHUNKS if n_g % (COMBINE_CHUNKS * SC_CHUNK * SC_WORKERS) == 0 else 1
        per = n_g // n_chunks
        for c in range(n_chunks):
            extra = grp["pos"][1:] if c == 0 else []
            got = _sc_gather([pos0[:, c * per:(c + 1) * per]] + extra, grp["rows"])
            if extra:
                smp = grp["parts"][1]
                y_s = _combine(got[1], smp["gate_t"], smp["h"], gfin).reshape(dec_b, dec_s, d)
            y_p = _combine(got[0], main["gate_t"], main["h"], gfin, tok0=c * per,
                           y_prev=y_p, y_rows=n_p, y_tok0=tok_start[g] + c * per)
    y_p = y_p.reshape(bsz, seq, d)

    na_p = jnp.concatenate([grp["states"][0][0] for grp in groups], axis=0)
    nb_p = jnp.concatenate([grp["states"][0][1] for grp in groups], axis=0)
    return (y_p, y_s, na_p[None], nb_p[None], na_s[None], nb_s[None])


def _block_plan(counts, n_pairs, n_exp):
    padded = (counts + MOE_BLOCK - 1) // MOE_BLOCK * MOE_BLOCK
    pad_end = jnp.cumsum(padded)
    pad_start = pad_end - padded
    n_blocks = -(-n_pairs // MOE_BLOCK) + n_exp
    n_blocks = -(-n_blocks // BLOCKS_PER_STEP) * BLOCKS_PER_STEP
    blk_start = jnp.arange(n_blocks, dtype=jnp.int32) * MOE_BLOCK
    blk_exp = jnp.minimum(jnp.sum(blk_start[:, None] >= pad_end[None, :], axis=1),
                          n_exp - 1).astype(jnp.int32)
    n_active = (pad_end[-1:] // MOE_BLOCK).astype(jnp.int32)
    e_ids = jnp.arange(n_exp, dtype=jnp.int32)
    has = padded > 0
    slot_e = (jnp.cumsum(has.astype(jnp.int32)) - 1) % 2
    later = jnp.where(has, e_ids, n_exp)
    next_e = jnp.concatenate([lax.cummin(later, reverse=True)[1:], jnp.full((1,), n_exp, jnp.int32)])
    next_e = jnp.where(next_e >= n_exp, -1, next_e)
    blk_onehot = blk_exp[:, None] == e_ids[None, :]

    def per_block(table):
        return jnp.sum(jnp.where(blk_onehot, table[None, :], 0), axis=1).astype(jnp.int32)

    blk_first = ((blk_start == per_block(pad_start)) & (blk_start < pad_end[-1])).astype(jnp.int32)
    plan = (blk_exp, blk_first, per_block(slot_e), per_block(next_e), n_active)
    return plan, pad_start.astype(jnp.int32), n_blocks
```

```python
import functools

import jax
import jax.numpy as jnp
from jax import lax
from jax.experimental import pallas as pl
from jax.experimental.pallas import tpu as pltpu
from jax.experimental.pallas import tpu_sc as plsc

SC_CORES = 2
SC_SUBCORES = 16
SC_WORKERS = SC_CORES * SC_SUBCORES
SC_CHUNK = 128
CONV_A_W = 3
CONV_B_W = 31
TOP_K = 4
SWIGLU_LIMIT = 7.0
SWIGLU_ALPHA = 1.702
NORM_EPS = 1e-5
MOE_BLOCK = 256
BLOCKS_PER_STEP = 4
SUBLANES = 8
LANES = 128
CONV_ROW_CHUNK = 128
HIST_A_ROWS = 8
HIST_B_ROWS = 32
MIXER_ROWS = 512
PROJ_SECTION = 256
POS_TILE = 4096
GROUP_SPLIT = (3, 1)
COMBINE_CHUNKS = 2
VMEM_LIMIT = 56 * 1024 * 1024


def _rms(x, g):
    return x * lax.rsqrt(jnp.mean(x * x, axis=-1, keepdims=True) + NORM_EPS) * g


def _zero_tile_from(v):
    bits = pltpu.bitcast(v[0:SUBLANES, 0:LANES], jnp.uint32)
    return pltpu.bitcast((bits >> 16) >> 16, jnp.float32)


def _pack_bf16_pairs(a):
    w = a.shape[-1] // 2
    lo = pltpu.bitcast(a[:, :w].astype(jnp.bfloat16).astype(jnp.float32), jnp.uint32)
    hi = pltpu.bitcast(a[:, w:].astype(jnp.bfloat16).astype(jnp.float32), jnp.uint32)
    return (lo >> 16) | (hi & jnp.uint32(0xFFFF0000))


def _unpack_bf16_pairs(wd):
    lo = pltpu.bitcast(wd << 16, jnp.float32)
    hi = pltpu.bitcast(wd & jnp.uint32(0xFFFF0000), jnp.float32)
    return jnp.concatenate([lo, hi], axis=-1)


def _conv_b_chunk(ub_buf, cbw_ref, n_i, r0, rc, c0):
    cs = slice(c0, c0 + LANES)
    lead = HIST_B_ROWS - (CONV_B_W - 1)
    acc = None
    for b in range(SUBLANES):
        rows = rc if b == 0 else rc + SUBLANES
        q = None
        for a in range((HIST_B_ROWS + SUBLANES) // SUBLANES):
            k = SUBLANES * a + b - lead
            if k < 0 or k >= CONV_B_W:
                continue
            start = r0 + SUBLANES * a
            term = cbw_ref[k:k + 1, cs] * ub_buf[n_i, start:start + rows, cs]
            q = term if q is None else q + term
        part = q if b == 0 else q[b:b + rc]
        acc = part if acc is None else acc + part
    return acc


def _mixer_kernel(x_ref, hista_ref, histb_ref, cnt0_ref, tri_ref,
                  gmix_ref, win_ref, caw_ref, cbw_ref, cbb_ref, lng_ref, lnb_ref,
                  ga_ref, gb_ref, wout_ref, gffn_ref, wrt_ref, br_ref,
                  h_ref, hnp_ref, idx_ref, gate_ref, rank_ref, cnt_ref, newa_ref, newb_ref,
                  ua_buf, ub_buf, cb_buf, cnt_acc, *, nb, ts, d_a, d_b, n_exp):
    b = pl.program_id(0)
    s = pl.program_id(1)
    m = nb * ts
    d = x_ref.shape[-1]

    @pl.when(s == 0)
    def _():
        ua_buf[:, 0:HIST_A_ROWS, :] = hista_ref[...]
        ub_buf[:, 0:HIST_B_ROWS, :] = histb_ref[...]

    @pl.when(s != 0)
    def _():
        ua_buf[:, 0:HIST_A_ROWS, :] = ua_buf[:, ts:ts + HIST_A_ROWS, :]
        ub_buf[:, 0:HIST_B_ROWS, :] = ub_buf[:, ts:ts + HIST_B_ROWS, :]

    @pl.when((b == 0) & (s == 0))
    def _():
        cnt_acc[...] = cnt0_ref[...]

    x = x_ref[...].reshape(m, d)
    n = _rms(x, gmix_ref[...]).astype(jnp.bfloat16)
    def proj(c0, width):
        return jnp.dot(n, win_ref[:, c0:c0 + width], preferred_element_type=jnp.float32)

    sec = min(d_b, PROJ_SECTION)
    for c0 in range(0, d_b, sec):
        v_b = proj(3 * d_a + c0, sec)
        g_b = proj(3 * d_a + d_b + c0, sec)
        ub_buf[:, HIST_B_ROWS:HIST_B_ROWS + ts, c0:c0 + sec] = (v_b * jax.nn.sigmoid(g_b)).reshape(nb, ts, sec)

    gate_c = proj(d_a, d_a)
    xt = proj(2 * d_a, d_a)
    gate_b = proj(0, d_a)
    ua_buf[:, HIST_A_ROWS:HIST_A_ROWS + ts, :] = (gate_c * xt).reshape(nb, ts, d_a)
    conv_a = jnp.zeros((nb, ts, d_a), jnp.float32)
    for k in range(CONV_A_W):
        off = HIST_A_ROWS - (CONV_A_W - 1) + k
        conv_a = conv_a + caw_ref[k:k + 1, :] * ua_buf[:, off:off + ts, :]
    y_a = gate_b * conv_a.reshape(m, d_a)

    mix_a = jnp.dot(_rms(y_a, ga_ref[...]).astype(jnp.bfloat16), wout_ref[0:d_a, :],
                    preferred_element_type=jnp.float32)

    rc = min(ts, CONV_ROW_CHUNK)
    chunks = [(n_i, r0, c0) for n_i in range(nb) for r0 in range(0, ts, rc) for c0 in range(0, d_b, LANES)]
    anchors = {len(chunks) * 2 // 8: gate_c, len(chunks) * 3 // 8: xt, len(chunks) * 5 // 8: gate_b,
               len(chunks) - 1: mix_a}
    for ci, (n_i, r0, c0) in enumerate(chunks):
        acc = _conv_b_chunk(ub_buf, cbw_ref, n_i, r0, rc, c0)
        if ci in anchors:
            acc = (acc.reshape(rc // SUBLANES, SUBLANES, LANES) + _zero_tile_from(anchors[ci])[None]
                   ).reshape(rc, LANES)
        cb_buf[n_i, r0:r0 + rc, c0:c0 + LANES] = acc
    cb = cb_buf[...].reshape(m, d_b) + cbb_ref[...]
    mu = jnp.mean(cb, axis=-1, keepdims=True)
    xc = cb - mu
    var = jnp.mean(xc * xc, axis=-1, keepdims=True)
    ln = xc * lax.rsqrt(var + NORM_EPS) * lng_ref[...] + lnb_ref[...]
    y_b = ln * jax.nn.sigmoid(ln)

    mix_b = jnp.dot(_rms(y_b, gb_ref[...]).astype(jnp.bfloat16), wout_ref[d_a:d_a + d_b, :],
                    preferred_element_type=jnp.float32)
    h = x + (mix_a + mix_b)
    h_ref[...] = h.reshape(nb, ts, d)

    hn = _rms(h, gffn_ref[...])
    hnp_ref[...] = _pack_bf16_pairs(hn)

    logits = lax.dot_general(wrt_ref[...], hn.astype(jnp.bfloat16), (((1,), (1,)), ((), ())),
                             preferred_element_type=jnp.float32) + br_ref[...]
    e_iota = lax.broadcasted_iota(jnp.int32, (n_exp, m), 0)
    cur = logits
    vals, idxs, sels = [], [], []
    for _ in range(TOP_K):
        mx = jnp.max(cur, axis=0, keepdims=True)
        ix = jnp.min(jnp.where(cur == mx, e_iota, n_exp), axis=0, keepdims=True)
        sel = e_iota == ix
        vals.append(mx)
        idxs.append(ix)
        sels.append(sel)
        cur = jnp.where(sel, -jnp.inf, cur)
    exps = [jnp.exp(v - vals[0]) for v in vals]
    denom = exps[0] + exps[1] + exps[2] + exps[3]
    gates = [e / denom for e in exps]

    onehot = jnp.zeros((n_exp, m), jnp.float32)
    for sel in sels:
        onehot = onehot + sel.astype(jnp.float32)
    before = jnp.dot(onehot.astype(jnp.bfloat16), tri_ref[...],
                     preferred_element_type=jnp.float32) + cnt_acc[...]
    ranks = [jnp.sum(jnp.where(sel, before, 0.0), axis=0, keepdims=True) for sel in sels]
    new_cnt = cnt_acc[...] + jnp.sum(onehot, axis=1, keepdims=True)
    cnt_acc[...] = new_cnt
    cnt_ref[...] = new_cnt

    idx_ref[...] = jnp.concatenate(idxs, axis=0)
    gate_ref[...] = jnp.concatenate(gates, axis=0)
    rank_ref[...] = jnp.concatenate(ranks, axis=0).astype(jnp.int32)

    @pl.when(s == pl.num_programs(1) - 1)
    def _():
        newa_ref[...] = ua_buf[:, ts + HIST_A_ROWS - (CONV_A_W - 1):ts + HIST_A_ROWS, :]
        newb_ref[...] = ub_buf[:, ts + HIST_B_ROWS - (CONV_B_W - 1):ts + HIST_B_ROWS, :]


def _mixer(x, hist_a, hist_b, cnt0, params, *, nb, ts, b0=0, bsz=None):
    (gmix, win, caw, cbw, cbb, lng, lnb, ga, gb, wout, gffn, wrt, br) = params
    _, seq, d = x.shape
    bsz = x.shape[0] if bsz is None else bsz
    d_a, d_b = caw.shape[-1], cbw.shape[-1]
    n_exp = wrt.shape[0]
    m = nb * ts
    n_tok = bsz * seq
    g0 = b0 // nb
    grid = (bsz // nb, seq // ts)
    tri = (jnp.arange(m)[:, None] < jnp.arange(m)[None, :]).astype(jnp.bfloat16)

    def const(shape):
        return pl.BlockSpec(shape, lambda i, j: (0,) * len(shape))

    tok_map = lambda i, j: (0, i * (seq // ts) + j)
    kern = functools.partial(_mixer_kernel, nb=nb, ts=ts, d_a=d_a, d_b=d_b, n_exp=n_exp)
    out_shape = (
        jax.ShapeDtypeStruct((bsz, seq, d), jnp.float32),
        jax.ShapeDtypeStruct((n_tok, d // 2), jnp.uint32),
        jax.ShapeDtypeStruct((TOP_K, n_tok), jnp.int32),
        jax.ShapeDtypeStruct((TOP_K, n_tok), jnp.float32),
        jax.ShapeDtypeStruct((TOP_K, n_tok), jnp.int32),
        jax.ShapeDtypeStruct((n_exp, 1), jnp.float32),
        jax.ShapeDtypeStruct((bsz, CONV_A_W - 1, d_a), jnp.float32),
        jax.ShapeDtypeStruct((bsz, CONV_B_W - 1, d_b), jnp.float32),
    )
    return pl.pallas_call(
        kern,
        grid=grid,
        in_specs=[
            pl.BlockSpec((nb, ts, d), lambda i, j: (g0 + i, j, 0)),
            pl.BlockSpec((nb, HIST_A_ROWS, d_a), lambda i, j: (g0 + i, 0, 0)),
            pl.BlockSpec((nb, HIST_B_ROWS, d_b), lambda i, j: (g0 + i, 0, 0)),
            const((n_exp, 1)),
            const((m, m)),
            const((1, d)), const(win.shape), const(caw.shape), const(cbw.shape), const((1, d_b)),
            const((1, d_b)), const((1, d_b)), const((1, d_a)), const((1, d_b)), const(wout.shape),
            const((1, d)), const(wrt.shape), const((n_exp, 1)),
        ],
        out_specs=(
            pl.BlockSpec((nb, ts, d), lambda i, j: (i, j, 0)),
            pl.BlockSpec((m, d // 2), lambda i, j: (i * (seq // ts) + j, 0)),
            pl.BlockSpec((TOP_K, m), tok_map),
            pl.BlockSpec((TOP_K, m), tok_map),
            pl.BlockSpec((TOP_K, m), tok_map),
            pl.BlockSpec((n_exp, 1), lambda i, j: (0, 0)),
            pl.BlockSpec((nb, CONV_A_W - 1, d_a), lambda i, j: (i, 0, 0)),
            pl.BlockSpec((nb, CONV_B_W - 1, d_b), lambda i, j: (i, 0, 0)),
        ),
        out_shape=out_shape,
        scratch_shapes=[
            pltpu.VMEM((nb, HIST_A_ROWS + ts, d_a), jnp.float32),
            pltpu.VMEM((nb, HIST_B_ROWS + ts, d_b), jnp.float32),
            pltpu.VMEM((nb, ts, d_b), jnp.float32),
            pltpu.VMEM((n_exp, 1), jnp.float32),
        ],
        compiler_params=pltpu.CompilerParams(
            dimension_semantics=("arbitrary", "arbitrary"), vmem_limit_bytes=VMEM_LIMIT),
        name="mixer",
    )(x, hist_a, hist_b, cnt0, tri, gmix, win, caw, cbw, cbb, lng, lnb, ga, gb, wout, gffn, wrt, br)


def _positions_kernel(start_ref, *refs):
    n_parts = len(refs) // 3
    for p in range(n_parts):
        idx = refs[2 * p][...]
        pos = refs[2 * p + 1][...]
        for e in range(start_ref.shape[0]):
            pos = pos + jnp.where(idx == e, start_ref[e], 0)
        refs[2 * n_parts + p][...] = pos


def _positions(pad_start, parts):
    n_0 = parts[0][0].shape[1]
    tile = min(n_0, POS_TILE)
    specs = [pl.BlockSpec((TOP_K, tile), lambda i: (0, i))]
    specs += [pl.BlockSpec((TOP_K, idx.shape[1]), lambda i: (0, 0)) for idx, _ in parts[1:]]
    in_specs = [pl.BlockSpec(memory_space=pltpu.SMEM)]
    args = [pad_start]
    for spec, (idx, rank) in zip(specs, parts):
        in_specs += [spec, spec]
        args += [idx, rank]
    out = pl.pallas_call(
        _positions_kernel,
        grid=(n_0 // tile,),
        in_specs=in_specs,
        out_specs=tuple(specs),
        out_shape=tuple(jax.ShapeDtypeStruct(idx.shape, jnp.int32) for idx, _ in parts),
        compiler_params=pltpu.CompilerParams(dimension_semantics=("arbitrary",)),
        name="positions",
    )(*args)
    return list(out)


def _sc_mesh():
    return plsc.VectorSubcoreMesh(core_axis_name="c", subcore_axis_name="s",
                                  num_cores=SC_CORES, num_subcores=SC_SUBCORES)


def _sc_worker_id():
    return lax.axis_index("s") * SC_CORES + lax.axis_index("c")


def _sc_chunks(n_tok, wid):
    n_chunks = n_tok // SC_CHUNK
    assert n_tok % SC_CHUNK == 0
    if n_chunks % SC_WORKERS == 0:
        per_w = n_chunks // SC_WORKERS
        return None, per_w, lambda i: (wid * per_w + i) * SC_CHUNK
    assert n_chunks <= SC_WORKERS
    return wid < n_chunks, 1, lambda i: wid * SC_CHUNK


def _sc_for_each_chunk(n_tok, fn):
    wid = _sc_worker_id()
    pred, trips, base = _sc_chunks(n_tok, wid)

    def run():
        @pl.loop(0, trips)
        def _(i):
            fn(pl.multiple_of(base(i), SC_CHUNK))

    if pred is None:
        run()
    else:
        pl.when(pred)(run)


def _sc_dispatch(parts, n_rows):
    half = parts[0][1].shape[1]
    n_parts = len(parts)

    @functools.partial(
        pl.kernel, mesh=_sc_mesh(),
        out_type=jax.ShapeDtypeStruct((n_rows, half), jnp.uint32),
        scratch_types=[pltpu.VMEM((TOP_K, SC_CHUNK), jnp.int32),
                       pltpu.VMEM((SC_CHUNK, half), jnp.uint32),
                       pltpu.SemaphoreType.DMA],
        compiler_params=pltpu.CompilerParams(use_tc_tiling_on_sc=True),
        name="sc_dispatch")
    def k(*refs):
        xs_hbm, idx_v, rows_v, sem = refs[2 * n_parts:]

        def move(pos_hbm, src_hbm):
            def fn(base):
                pltpu.sync_copy(pos_hbm.at[:, pl.ds(base, SC_CHUNK)], idx_v)
                pltpu.sync_copy(src_hbm.at[pl.ds(base, SC_CHUNK)], rows_v)
                copies = [pltpu.async_copy(rows_v, xs_hbm.at[idx_v.at[j]], sem) for j in range(TOP_K)]
                for c in copies:
                    c.wait()
            return fn

        for p, (_, hnp) in enumerate(parts):
            _sc_for_each_chunk(hnp.shape[0], move(refs[2 * p], refs[2 * p + 1]))

    return k(*[a for part in parts for a in part])


def _sc_gather(pos_parts, rows):
    half = rows.shape[1]
    n_parts = len(pos_parts)

    @functools.partial(
        pl.kernel, mesh=_sc_mesh(),
        out_type=tuple(jax.ShapeDtypeStruct((TOP_K, p.shape[1], half), jnp.uint32) for p in pos_parts),
        scratch_types=[pltpu.VMEM((TOP_K, SC_CHUNK), jnp.int32),
                       pltpu.VMEM((SC_CHUNK, half), jnp.uint32),
                       pltpu.SemaphoreType.DMA],
        compiler_params=pltpu.CompilerParams(use_tc_tiling_on_sc=True),
        name="sc_gather")
    def k(*refs):
        pos_hbms, rows_hbm = refs[:n_parts], refs[n_parts]
        out_hbms = refs[n_parts + 1:2 * n_parts + 1]
        idx_v, buf_v, sem = refs[2 * n_parts + 1:]

        def move(pos_hbm, out_hbm):
            def fn(base):
                pltpu.sync_copy(pos_hbm.at[:, pl.ds(base, SC_CHUNK)], idx_v)
                for j in range(TOP_K):
                    pltpu.async_copy(rows_hbm.at[idx_v.at[j]], buf_v, sem).wait()
                    pltpu.sync_copy(buf_v, out_hbm.at[j, pl.ds(base, SC_CHUNK)])
            return fn

        for pos_hbm, out_hbm, p in zip(pos_hbms, out_hbms, pos_parts):
            _sc_for_each_chunk(p.shape[1], move(pos_hbm, out_hbm))

    out = k(*pos_parts, rows)
    return out if isinstance(out, (tuple, list)) else (out,)


def _experts_kernel(blk_exp_ref, first_ref, slot_ref, next_ref, n_active_ref,
                    xs_ref, wgu_hbm, bgu_ref, wd_hbm, bd_ref, out_ref,
                    wgu_f32, wd_f32, wgu_bf, wd_bf, sems):
    d_ff = wd_bf.shape[0]

    def fetch(e, sl):
        return (pltpu.make_async_copy(wgu_hbm.at[e], wgu_f32.at[sl], sems.at[0, sl]),
                pltpu.make_async_copy(wd_hbm.at[e], wd_f32.at[sl], sems.at[1, sl]))

    def load_weights_if_first(i):
        sl = slot_ref[i]
        e = blk_exp_ref[i]

        @pl.when(first_ref[i] == 1)
        def _():
            @pl.when(i == 0)
            def _():
                for c in fetch(e, sl):
                    c.start()

            for c in fetch(e, sl):
                c.wait()
            wgu_bf[...] = wgu_f32[sl].astype(jnp.bfloat16)
            wd_bf[...] = wd_f32[sl].astype(jnp.bfloat16)

            @pl.when(next_ref[i] >= 0)
            def _():
                for c in fetch(next_ref[i], 1 - sl):
                    c.start()

    def compute(rows, e):
        x = _unpack_bf16_pairs(xs_ref[rows, :]).astype(jnp.bfloat16)
        gu = jnp.dot(x, wgu_bf[...], preferred_element_type=jnp.float32) + bgu_ref[e]
        g = jnp.minimum(gu[:, :d_ff], SWIGLU_LIMIT)
        u = jnp.clip(gu[:, d_ff:], -SWIGLU_LIMIT, SWIGLU_LIMIT)
        act = g * jax.nn.sigmoid(SWIGLU_ALPHA * g) * (u + 1.0)
        o = jnp.dot(act.astype(jnp.bfloat16), wd_bf[...], preferred_element_type=jnp.float32) + bd_ref[e]
        out_ref[rows, :] = _pack_bf16_pairs(o)

    i0 = pl.program_id(0) * BLOCKS_PER_STEP
    n_active = n_active_ref[0]

    def run(sub0, n_sub):
        first = i0 + sub0
        rows = slice(sub0 * MOE_BLOCK, (sub0 + n_sub) * MOE_BLOCK)
        if n_sub == 1:
            @pl.when(first < n_active)
            def _():
                load_weights_if_first(first)
                compute(rows, blk_exp_ref[first])

            @pl.when(first >= n_active)
            def _():
                out_ref[rows, :] = jnp.zeros((MOE_BLOCK, out_ref.shape[1]), out_ref.dtype)
            return

        same = first + n_sub - 1 < n_active
        for sub in range(1, n_sub):
            same = same & (blk_exp_ref[first + sub] == blk_exp_ref[first])

        @pl.when(same)
        def _():
            load_weights_if_first(first)
            compute(rows, blk_exp_ref[first])

        @pl.when(jnp.logical_not(same))
        def _():
            run(sub0, n_sub // 2)
            run(sub0 + n_sub // 2, n_sub // 2)

    run(0, BLOCKS_PER_STEP)


def _experts(plan, xs, wgu, bgu, wd, bd):
    blk_exp, first, slot, nxt, n_active = plan
    n_rows, half = xs.shape
    n_exp, d, two_f = wgu.shape
    d_ff = wd.shape[1]
    n_blocks = n_rows // MOE_BLOCK
    step_rows = MOE_BLOCK * BLOCKS_PER_STEP
    assert n_rows % step_rows == 0
    grid_spec = pltpu.PrefetchScalarGridSpec(
        num_scalar_prefetch=5,
        grid=(n_blocks // BLOCKS_PER_STEP,),
        in_specs=[
            pl.BlockSpec((step_rows, half), lambda i, *_: (i, 0)),
            pl.BlockSpec(memory_space=pl.ANY),
            pl.BlockSpec((n_exp, 1, two_f), lambda i, *_: (0, 0, 0)),
            pl.BlockSpec(memory_space=pl.ANY),
            pl.BlockSpec((n_exp, 1, d), lambda i, *_: (0, 0, 0)),
        ],
        out_specs=pl.BlockSpec((step_rows, half), lambda i, *_: (i, 0)),
        scratch_shapes=[
            pltpu.VMEM((2, d, two_f), jnp.float32),
            pltpu.VMEM((2, d_ff, d), jnp.float32),
            pltpu.VMEM((d, two_f), jnp.bfloat16),
            pltpu.VMEM((d_ff, d), jnp.bfloat16),
            pltpu.SemaphoreType.DMA((2, 2)),
        ],
    )
    return pl.pallas_call(
        _experts_kernel,
        grid_spec=grid_spec,
        out_shape=jax.ShapeDtypeStruct((n_rows, half), jnp.uint32),
        compiler_params=pltpu.CompilerParams(
            dimension_semantics=("arbitrary",), vmem_limit_bytes=VMEM_LIMIT),
        name="experts",
    )(blk_exp, first, slot, nxt, n_active, xs, wgu, bgu.reshape(n_exp, 1, two_f), wd,
      bd.reshape(n_exp, 1, d))


def _combine_kernel(rows_ref, gate_ref, h_ref, gfin_ref, *rest):
    y_ref = rest[-1]
    acc = h_ref[...]
    for k in range(TOP_K):
        acc = acc + gate_ref[:, k:k + 1] * _unpack_bf16_pairs(rows_ref[k])
    y_ref[...] = _rms(acc, gfin_ref[...])


def _combine(rows4, gate_t, h2d, gfin, *, tok0=0, y_prev=None, y_rows=None, y_tok0=0):
    d = h2d.shape[1]
    n, half = rows4.shape[1:]
    n_tok = h2d.shape[0] if y_rows is None else y_rows
    tile = min(n, 256)
    t0 = tok0 // tile
    y0 = y_tok0 // tile
    in_specs = [
        pl.BlockSpec((TOP_K, tile, half), lambda i: (0, i, 0)),
        pl.BlockSpec((tile, TOP_K), lambda i: (t0 + i, 0)),
        pl.BlockSpec((tile, d), lambda i: (t0 + i, 0)),
        pl.BlockSpec((1, d), lambda i: (0, 0)),
    ]
    args = [rows4, gate_t, h2d, gfin]
    aliases = {}
    if y_prev is not None:
        in_specs.append(pl.BlockSpec(memory_space=pl.ANY))
        args.append(y_prev)
        aliases = {4: 0}
    return pl.pallas_call(
        _combine_kernel,
        grid=(n // tile,),
        in_specs=in_specs,
        out_specs=pl.BlockSpec((tile, d), lambda i: (y0 + i, 0)),
        out_shape=jax.ShapeDtypeStruct((n_tok, d), jnp.float32),
        input_output_aliases=aliases,
        compiler_params=pltpu.CompilerParams(dimension_semantics=("arbitrary",)),
        name="combine",
    )(*args)


def _pad_hist(hist, rows):
    return jnp.pad(hist, ((0, 0), (rows - hist.shape[1], 0), (0, 0)))


def kernel(x_prompt, x_sample, state_conv_a, state_conv_b, norm_mix_g, w_in, conv_a_w, conv_b_w, conv_b_b, conv_ln_g, conv_ln_b, out_norm_a_g, out_norm_b_g, w_out, norm_ffn_g, w_router, b_router, w_gate_up, b_gate_up, w_down, b_down, final_norm_g):
    depth = w_in.shape[0]
    assert depth == 1, "single-layer trunk"
    bf16 = jnp.bfloat16
    bsz, seq, d = x_prompt.shape
    dec_b, dec_s, _ = x_sample.shape
    d_a, d_b = conv_a_w.shape[-1], conv_b_w.shape[-1]
    n_exp = w_router.shape[-1]

    params = (norm_mix_g[0][None], w_in[0].astype(bf16), conv_a_w[0], conv_b_w[0], conv_b_b[0][None],
              conv_ln_g[0][None], conv_ln_b[0][None], out_norm_a_g[0][None], out_norm_b_g[0][None],
              w_out[0].astype(bf16), norm_ffn_g[0][None], w_router[0].T.astype(bf16),
              b_router[0][:, None])

    zero_a = jnp.zeros((bsz, HIST_A_ROWS, d_a), jnp.float32)
    zero_b = jnp.zeros((bsz, HIST_B_ROWS, d_b), jnp.float32)
    cnt0 = jnp.zeros((n_exp, 1), jnp.float32)
    gfin = final_norm_g[None]
    ts = min(seq, MIXER_ROWS)
    n_p = bsz * seq

    sizes = [bsz * f // sum(GROUP_SPLIT) for f in GROUP_SPLIT]
    if bsz % sum(GROUP_SPLIT) != 0:
        sizes = [bsz]
    groups = []
    tok_start = []
    for g, per_b in enumerate(sizes):
        b0 = sum(sizes[:g])
        tok_start.append(b0 * seq)
        cnt_in = cnt0
        if groups:
            cnt_in, _ = lax.optimization_barrier((cnt0, groups[-1]["cnt"]))
        h, hnp, idx, gate, rank, cnt, na, nb_ = _mixer(
            x_prompt, zero_a, zero_b, cnt_in, params, nb=1, ts=ts, b0=b0, bsz=per_b)
        parts = [dict(h=h.reshape(per_b * seq, d), hnp=hnp, idx=idx, gate_t=gate.T, rank=rank)]
        states = [(na, nb_)]
        if g == len(sizes) - 1:
            h, hnp, idx, gate, rank, cnt, na_s, nb_s = _mixer(
                x_sample, _pad_hist(state_conv_a[0], HIST_A_ROWS), _pad_hist(state_conv_b[0], HIST_B_ROWS),
                cnt, params, nb=dec_b, ts=dec_s)
            parts.append(dict(h=h.reshape(dec_b * dec_s, d), hnp=hnp, idx=idx, gate_t=gate.T, rank=rank))
        n_pairs = sum(p["hnp"].shape[0] for p in parts) * TOP_K
        plan, pad_start, n_blocks = _block_plan(cnt[:, 0].astype(jnp.int32), n_pairs, n_exp)
        pos = _positions(pad_start, [(p["idx"], p["rank"]) for p in parts])
        xs = _sc_dispatch([(q, p["hnp"]) for q, p in zip(pos, parts)], n_blocks * MOE_BLOCK)
        groups.append(dict(parts=parts, pos=pos, plan=plan, xs=xs, states=states, cnt=cnt))

    for g, grp in enumerate(groups):
        xs = grp["xs"]
        if g:
            xs, _ = lax.optimization_barrier((xs, groups[g - 1]["rows"]))
        grp["rows"] = _experts(grp["plan"], xs, w_gate_up[0], b_gate_up[0], w_down[0], b_down[0])

    gathered = []
    prev = None
    for g, grp in enumerate(groups):
        pos0 = grp["pos"][0]
        n_g = pos0.shape[1]
        n_chunks = COMBINE_CHUNKS if n_g % (COMBINE_CHUNKS * SC_CHUNK * SC_WORKERS) == 0 else 1
        per = n_g // n_chunks
        for c in range(n_chunks):
            pos_parts = [pos0[:, c * per:(c + 1) * per]] + (grp["pos"][1:] if c == 0 else [])
            if prev is not None:
                pos_parts[0], _ = lax.optimization_barrier((pos_parts[0], prev))
            got = _sc_gather(pos_parts, grp["rows"])
            prev = got[0]
            gathered.append((g, c * per, got))

    y_p = None
    for g, tok0, got in gathered:
        grp = groups[g]
        main = grp["parts"][0]
        if len(got) > 1:
            smp = grp["parts"][1]
            y_s = _combine(got[1], smp["gate_t"], smp["h"], gfin).reshape(dec_b, dec_s, d)
        y_p = _combine(got[0], main["gate_t"], main["h"], gfin, tok0=tok0,
                       y_prev=y_p, y_rows=n_p, y_tok0=tok_start[g] + tok0)
    y_p = y_p.reshape(bsz, seq, d)

    na_p = jnp.concatenate([grp["states"][0][0] for grp in groups], axis=0)
    nb_p = jnp.concatenate([grp["states"][0][1] for grp in groups], axis=0)
    return (y_p, y_s, na_p[None], nb_p[None], na_s[None], nb_s[None])


def _block_plan(counts, n_pairs, n_exp):
    padded = (counts + MOE_BLOCK - 1) // MOE_BLOCK * MOE_BLOCK
    pad_end = jnp.cumsum(padded)
    pad_start = pad_end - padded
    n_blocks = -(-n_pairs // MOE_BLOCK) + n_exp
    n_blocks = -(-n_blocks // BLOCKS_PER_STEP) * BLOCKS_PER_STEP
    blk_start = jnp.arange(n_blocks, dtype=jnp.int32) * MOE_BLOCK
    blk_exp = jnp.minimum(jnp.sum(blk_start[:, None] >= pad_end[None, :], axis=1),
                          n_exp - 1).astype(jnp.int32)
    n_active = (pad_end[-1:] // MOE_BLOCK).astype(jnp.int32)
    e_ids = jnp.arange(n_exp, dtype=jnp.int32)
    has = padded > 0
    slot_e = (jnp.cumsum(has.astype(jnp.int32)) - 1) % 2
    later = jnp.where(has, e_ids, n_exp)
    next_e = jnp.concatenate([lax.cummin(later, reverse=True)[1:], jnp.full((1,), n_exp, jnp.int32)])
    next_e = jnp.where(next_e >= n_exp, -1, next_e)
    blk_onehot = blk_exp[:, None] == e_ids[None, :]

    def per_block(table):
        return jnp.sum(jnp.where(blk_onehot, table[None, :], 0), axis=1).astype(jnp.int32)

    blk_first = ((blk_start == per_block(pad_start)) & (blk_start < pad_end[-1])).astype(jnp.int32)
    plan = (blk_exp, blk_first, per_block(slot_e), per_block(next_e), n_active)
    return plan, pad_start.astype(jnp.int32), n_blocks
```

```python
import functools

import jax
import jax.numpy as jnp
from jax import lax
from jax.experimental import pallas as pl
from jax.experimental.pallas import tpu as pltpu
from jax.experimental.pallas import tpu_sc as plsc

SC_CORES = 2
SC_SUBCORES = 16
SC_WORKERS = SC_CORES * SC_SUBCORES
SC_CHUNK = 128
CONV_A_W = 3
CONV_B_W = 31
TOP_K = 4
SWIGLU_LIMIT = 7.0
SWIGLU_ALPHA = 1.702
NORM_EPS = 1e-5
MOE_BLOCK = 256
BLOCKS_PER_STEP = 4
SUBLANES = 8
LANES = 128
CONV_ROW_CHUNK = 128
HIST_A_ROWS = 8
HIST_B_ROWS = 32
MIXER_ROWS = 512
PROJ_SECTION = 256
POS_TILE = 4096
GROUP_SPLIT = (3, 1)
COMBINE_CHUNKS = 2
VMEM_LIMIT = 56 * 1024 * 1024


def _rms(x, g):
    return x * lax.rsqrt(jnp.mean(x * x, axis=-1, keepdims=True) + NORM_EPS) * g


def _zero_tile_from(v):
    bits = pltpu.bitcast(v[0:SUBLANES, 0:LANES], jnp.uint32)
    return pltpu.bitcast((bits >> 16) >> 16, jnp.float32)


def _pack_bf16_pairs(a):
    w = a.shape[-1] // 2
    lo = pltpu.bitcast(a[:, :w].astype(jnp.bfloat16).astype(jnp.float32), jnp.uint32)
    hi = pltpu.bitcast(a[:, w:].astype(jnp.bfloat16).astype(jnp.float32), jnp.uint32)
    return (lo >> 16) | (hi & jnp.uint32(0xFFFF0000))


def _unpack_bf16_pairs(wd):
    lo = pltpu.bitcast(wd << 16, jnp.float32)
    hi = pltpu.bitcast(wd & jnp.uint32(0xFFFF0000), jnp.float32)
    return jnp.concatenate([lo, hi], axis=-1)


def _conv_b_chunk(ub_buf, cbw_ref, n_i, r0, rc, c0):
    cs = slice(c0, c0 + LANES)
    lead = HIST_B_ROWS - (CONV_B_W - 1)
    acc = None
    for b in range(SUBLANES):
        rows = rc if b == 0 else rc + SUBLANES
        q = None
        for a in range((HIST_B_ROWS + SUBLANES) // SUBLANES):
            k = SUBLANES * a + b - lead
            if k < 0 or k >= CONV_B_W:
                continue
            start = r0 + SUBLANES * a
            term = cbw_ref[k:k + 1, cs] * ub_buf[n_i, start:start + rows, cs]
            q = term if q is None else q + term
        part = q if b == 0 else q[b:b + rc]
        acc = part if acc is None else acc + part
    return acc


def _mixer_kernel(x_ref, hista_ref, histb_ref, cnt0_ref, tri_ref,
                  gmix_ref, win_ref, caw_ref, cbw_ref, cbb_ref, lng_ref, lnb_ref,
                  ga_ref, gb_ref, wout_ref, gffn_ref, wrt_ref, br_ref,
                  h_ref, hnp_ref, idx_ref, gate_ref, rank_ref, cnt_ref, newa_ref, newb_ref,
                  ua_buf, ub_buf, cb_buf, cnt_acc, *, nb, ts, d_a, d_b, n_exp):
    b = pl.program_id(0)
    s = pl.program_id(1)
    m = nb * ts
    d = x_ref.shape[-1]

    @pl.when(s == 0)
    def _():
        ua_buf[:, 0:HIST_A_ROWS, :] = hista_ref[...]
        ub_buf[:, 0:HIST_B_ROWS, :] = histb_ref[...]

    @pl.when(s != 0)
    def _():
        ua_buf[:, 0:HIST_A_ROWS, :] = ua_buf[:, ts:ts + HIST_A_ROWS, :]
        ub_buf[:, 0:HIST_B_ROWS, :] = ub_buf[:, ts:ts + HIST_B_ROWS, :]

    @pl.when((b == 0) & (s == 0))
    def _():
        cnt_acc[...] = cnt0_ref[...]

    x = x_ref[...].reshape(m, d)
    n = _rms(x, gmix_ref[...]).astype(jnp.bfloat16)
    def proj(c0, width):
        return jnp.dot(n, win_ref[:, c0:c0 + width], preferred_element_type=jnp.float32)

    sec = min(d_b, PROJ_SECTION)
    for c0 in range(0, d_b, sec):
        v_b = proj(3 * d_a + c0, sec)
        g_b = proj(3 * d_a + d_b + c0, sec)
        ub_buf[:, HIST_B_ROWS:HIST_B_ROWS + ts, c0:c0 + sec] = (v_b * jax.nn.sigmoid(g_b)).reshape(nb, ts, sec)

    gate_c = proj(d_a, d_a)
    xt = proj(2 * d_a, d_a)
    gate_b = proj(0, d_a)
    ua_buf[:, HIST_A_ROWS:HIST_A_ROWS + ts, :] = (gate_c * xt).reshape(nb, ts, d_a)
    conv_a = jnp.zeros((nb, ts, d_a), jnp.float32)
    for k in range(CONV_A_W):
        off = HIST_A_ROWS - (CONV_A_W - 1) + k
        conv_a = conv_a + caw_ref[k:k + 1, :] * ua_buf[:, off:off + ts, :]
    y_a = gate_b * conv_a.reshape(m, d_a)

    mix_a = jnp.dot(_rms(y_a, ga_ref[...]).astype(jnp.bfloat16), wout_ref[0:d_a, :],
                    preferred_element_type=jnp.float32)

    rc = min(ts, CONV_ROW_CHUNK)
    chunks = [(n_i, r0, c0) for n_i in range(nb) for r0 in range(0, ts, rc) for c0 in range(0, d_b, LANES)]
    anchors = {len(chunks) * 2 // 8: gate_c, len(chunks) * 3 // 8: xt, len(chunks) * 5 // 8: gate_b,
               len(chunks) - 1: mix_a}
    for ci, (n_i, r0, c0) in enumerate(chunks):
        acc = _conv_b_chunk(ub_buf, cbw_ref, n_i, r0, rc, c0)
        if ci in anchors:
            acc = (acc.reshape(rc // SUBLANES, SUBLANES, LANES) + _zero_tile_from(anchors[ci])[None]
                   ).reshape(rc, LANES)
        cb_buf[n_i, r0:r0 + rc, c0:c0 + LANES] = acc
    cb = cb_buf[...].reshape(m, d_b) + cbb_ref[...]
    mu = jnp.mean(cb, axis=-1, keepdims=True)
    xc = cb - mu
    var = jnp.mean(xc * xc, axis=-1, keepdims=True)
    ln = xc * lax.rsqrt(var + NORM_EPS) * lng_ref[...] + lnb_ref[...]
    y_b = ln * jax.nn.sigmoid(ln)

    mix_b = jnp.dot(_rms(y_b, gb_ref[...]).astype(jnp.bfloat16), wout_ref[d_a:d_a + d_b, :],
                    preferred_element_type=jnp.float32)
    h = x + (mix_a + mix_b)
    h_ref[...] = h.reshape(nb, ts, d)

    hn = _rms(h, gffn_ref[...])
    hnp_ref[...] = _pack_bf16_pairs(hn)

    logits = lax.dot_general(wrt_ref[...], hn.astype(jnp.bfloat16), (((1,), (1,)), ((), ())),
                             preferred_element_type=jnp.float32) + br_ref[...]
    e_iota = lax.broadcasted_iota(jnp.int32, (n_exp, m), 0)
    cur = logits
    vals, idxs, sels = [], [], []
    for _ in range(TOP_K):
        mx = jnp.max(cur, axis=0, keepdims=True)
        ix = jnp.min(jnp.where(cur == mx, e_iota, n_exp), axis=0, keepdims=True)
        sel = e_iota == ix
        vals.append(mx)
        idxs.append(ix)
        sels.append(sel)
        cur = jnp.where(sel, -jnp.inf, cur)
    exps = [jnp.exp(v - vals[0]) for v in vals]
    denom = exps[0] + exps[1] + exps[2] + exps[3]
    gates = [e / denom for e in exps]

    onehot = jnp.zeros((n_exp, m), jnp.float32)
    for sel in sels:
        onehot = onehot + sel.astype(jnp.float32)
    before = jnp.dot(onehot.astype(jnp.bfloat16), tri_ref[...],
                     preferred_element_type=jnp.float32) + cnt_acc[...]
    ranks = [jnp.sum(jnp.where(sel, before, 0.0), axis=0, keepdims=True) for sel in sels]
    new_cnt = cnt_acc[...] + jnp.sum(onehot, axis=1, keepdims=True)
    cnt_acc[...] = new_cnt
    cnt_ref[...] = new_cnt

    idx_ref[...] = jnp.concatenate(idxs, axis=0)
    gate_ref[...] = jnp.concatenate(gates, axis=0)
    rank_ref[...] = jnp.concatenate(ranks, axis=0).astype(jnp.int32)

    @pl.when(s == pl.num_programs(1) - 1)
    def _():
        newa_ref[...] = ua_buf[:, ts + HIST_A_ROWS - (CONV_A_W - 1):ts + HIST_A_ROWS, :]
        newb_ref[...] = ub_buf[:, ts + HIST_B_ROWS - (CONV_B_W - 1):ts + HIST_B_ROWS, :]


def _mixer(x, hist_a, hist_b, cnt0, params, *, nb, ts, b0=0, bsz=None):
    (gmix, win, caw, cbw, cbb, lng, lnb, ga, gb, wout, gffn, wrt, br) = params
    _, seq, d = x.shape
    bsz = x.shape[0] if bsz is None else bsz
    d_a, d_b = caw.shape[-1], cbw.shape[-1]
    n_exp = wrt.shape[0]
    m = nb * ts
    n_tok = bsz * seq
    g0 = b0 // nb
    grid = (bsz // nb, seq // ts)
    tri = (jnp.arange(m)[:, None] < jnp.arange(m)[None, :]).astype(jnp.bfloat16)

    def const(shape):
        return pl.BlockSpec(shape, lambda i, j: (0,) * len(shape))

    tok_map = lambda i, j: (0, i * (seq // ts) + j)
    kern = functools.partial(_mixer_kernel, nb=nb, ts=ts, d_a=d_a, d_b=d_b, n_exp=n_exp)
    out_shape = (
        jax.ShapeDtypeStruct((bsz, seq, d), jnp.float32),
        jax.ShapeDtypeStruct((n_tok, d // 2), jnp.uint32),
        jax.ShapeDtypeStruct((TOP_K, n_tok), jnp.int32),
        jax.ShapeDtypeStruct((TOP_K, n_tok), jnp.float32),
        jax.ShapeDtypeStruct((TOP_K, n_tok), jnp.int32),
        jax.ShapeDtypeStruct((n_exp, 1), jnp.float32),
        jax.ShapeDtypeStruct((bsz, CONV_A_W - 1, d_a), jnp.float32),
        jax.ShapeDtypeStruct((bsz, CONV_B_W - 1, d_b), jnp.float32),
    )
    return pl.pallas_call(
        kern,
        grid=grid,
        in_specs=[
            pl.BlockSpec((nb, ts, d), lambda i, j: (g0 + i, j, 0)),
            pl.BlockSpec((nb, HIST_A_ROWS, d_a), lambda i, j: (g0 + i, 0, 0)),
            pl.BlockSpec((nb, HIST_B_ROWS, d_b), lambda i, j: (g0 + i, 0, 0)),
            const((n_exp, 1)),
            const((m, m)),
            const((1, d)), const(win.shape), const(caw.shape), const(cbw.shape), const((1, d_b)),
            const((1, d_b)), const((1, d_b)), const((1, d_a)), const((1, d_b)), const(wout.shape),
            const((1, d)), const(wrt.shape), const((n_exp, 1)),
        ],
        out_specs=(
            pl.BlockSpec((nb, ts, d), lambda i, j: (i, j, 0)),
            pl.BlockSpec((m, d // 2), lambda i, j: (i * (seq // ts) + j, 0)),
            pl.BlockSpec((TOP_K, m), tok_map),
            pl.BlockSpec((TOP_K, m), tok_map),
            pl.BlockSpec((TOP_K, m), tok_map),
            pl.BlockSpec((n_exp, 1), lambda i, j: (0, 0)),
            pl.BlockSpec((nb, CONV_A_W - 1, d_a), lambda i, j: (i, 0, 0)),
            pl.BlockSpec((nb, CONV_B_W - 1, d_b), lambda i, j: (i, 0, 0)),
        ),
        out_shape=out_shape,
        scratch_shapes=[
            pltpu.VMEM((nb, HIST_A_ROWS + ts, d_a), jnp.float32),
            pltpu.VMEM((nb, HIST_B_ROWS + ts, d_b), jnp.float32),
            pltpu.VMEM((nb, ts, d_b), jnp.float32),
            pltpu.VMEM((n_exp, 1), jnp.float32),
        ],
        compiler_params=pltpu.CompilerParams(
            dimension_semantics=("arbitrary", "arbitrary"), vmem_limit_bytes=VMEM_LIMIT),
        name="mixer",
    )(x, hist_a, hist_b, cnt0, tri, gmix, win, caw, cbw, cbb, lng, lnb, ga, gb, wout, gffn, wrt, br)


def _positions_kernel(start_ref, *refs):
    n_parts = len(refs) // 3
    for p in range(n_parts):
        idx = refs[2 * p][...]
        pos = refs[2 * p + 1][...]
        for e in range(start_ref.shape[0]):
            pos = pos + jnp.where(idx == e, start_ref[e], 0)
        refs[2 * n_parts + p][...] = pos


def _positions(pad_start, parts):
    n_0 = parts[0][0].shape[1]
    tile = min(n_0, POS_TILE)
    specs = [pl.BlockSpec((TOP_K, tile), lambda i: (0, i))]
    specs += [pl.BlockSpec((TOP_K, idx.shape[1]), lambda i: (0, 0)) for idx, _ in parts[1:]]
    in_specs = [pl.BlockSpec(memory_space=pltpu.SMEM)]
    args = [pad_start]
    for spec, (idx, rank) in zip(specs, parts):
        in_specs += [spec, spec]
        args += [idx, rank]
    out = pl.pallas_call(
        _positions_kernel,
        grid=(n_0 // tile,),
        in_specs=in_specs,
        out_specs=tuple(specs),
        out_shape=tuple(jax.ShapeDtypeStruct(idx.shape, jnp.int32) for idx, _ in parts),
        compiler_params=pltpu.CompilerParams(dimension_semantics=("arbitrary",)),
        name="positions",
    )(*args)
    return list(out)


def _sc_mesh():
    return plsc.VectorSubcoreMesh(core_axis_name="c", subcore_axis_name="s",
                                  num_cores=SC_CORES, num_subcores=SC_SUBCORES)


def _sc_worker_id():
    return lax.axis_index("s") * SC_CORES + lax.axis_index("c")


def _sc_chunks(n_tok, wid):
    n_chunks = n_tok // SC_CHUNK
    assert n_tok % SC_CHUNK == 0
    if n_chunks % SC_WORKERS == 0:
        per_w = n_chunks // SC_WORKERS
        return None, per_w, lambda i: (wid * per_w + i) * SC_CHUNK
    assert n_chunks <= SC_WORKERS
    return wid < n_chunks, 1, lambda i: wid * SC_CHUNK


def _sc_for_each_chunk(n_tok, fn):
    wid = _sc_worker_id()
    pred, trips, base = _sc_chunks(n_tok, wid)

    def run():
        @pl.loop(0, trips)
        def _(i):
            fn(pl.multiple_of(base(i), SC_CHUNK))

    if pred is None:
        run()
    else:
        pl.when(pred)(run)


def _sc_dispatch(parts, n_rows):
    half = parts[0][1].shape[1]
    n_parts = len(parts)

    @functools.partial(
        pl.kernel, mesh=_sc_mesh(),
        out_type=jax.ShapeDtypeStruct((n_rows, half), jnp.uint32),
        scratch_types=[pltpu.VMEM((TOP_K, SC_CHUNK), jnp.int32),
                       pltpu.VMEM((SC_CHUNK, half), jnp.uint32),
                       pltpu.SemaphoreType.DMA],
        compiler_params=pltpu.CompilerParams(use_tc_tiling_on_sc=True),
        name="sc_dispatch")
    def k(*refs):
        xs_hbm, idx_v, rows_v, sem = refs[2 * n_parts:]

        def move(pos_hbm, src_hbm):
            def fn(base):
                pltpu.sync_copy(pos_hbm.at[:, pl.ds(base, SC_CHUNK)], idx_v)
                pltpu.sync_copy(src_hbm.at[pl.ds(base, SC_CHUNK)], rows_v)
                copies = [pltpu.async_copy(rows_v, xs_hbm.at[idx_v.at[j]], sem) for j in range(TOP_K)]
                for c in copies:
                    c.wait()
            return fn

        for p, (_, hnp) in enumerate(parts):
            _sc_for_each_chunk(hnp.shape[0], move(refs[2 * p], refs[2 * p + 1]))

    return k(*[a for part in parts for a in part])


def _sc_gather(pos_parts, rows):
    half = rows.shape[1]
    n_parts = len(pos_parts)

    @functools.partial(
        pl.kernel, mesh=_sc_mesh(),
        out_type=tuple(jax.ShapeDtypeStruct((TOP_K, p.shape[1], half), jnp.uint32) for p in pos_parts),
        scratch_types=[pltpu.VMEM((TOP_K, SC_CHUNK), jnp.int32),
                       pltpu.VMEM((SC_CHUNK, half), jnp.uint32),
                       pltpu.SemaphoreType.DMA],
        compiler_params=pltpu.CompilerParams(use_tc_tiling_on_sc=True),
        name="sc_gather")
    def k(*refs):
        pos_hbms, rows_hbm = refs[:n_parts], refs[n_parts]
        out_hbms = refs[n_parts + 1:2 * n_parts + 1]
        idx_v, buf_v, sem = refs[2 * n_parts + 1:]

        def move(pos_hbm, out_hbm):
            def fn(base):
                pltpu.sync_copy(pos_hbm.at[:, pl.ds(base, SC_CHUNK)], idx_v)
                for j in range(TOP_K):
                    pltpu.async_copy(rows_hbm.at[idx_v.at[j]], buf_v, sem).wait()
                    pltpu.sync_copy(buf_v, out_hbm.at[j, pl.ds(base, SC_CHUNK)])
            return fn

        for pos_hbm, out_hbm, p in zip(pos_hbms, out_hbms, pos_parts):
            _sc_for_each_chunk(p.shape[1], move(pos_hbm, out_hbm))

    out = k(*pos_parts, rows)
    return out if isinstance(out, (tuple, list)) else (out,)


def _experts_kernel(blk_exp_ref, first_ref, slot_ref, next_ref, n_active_ref,
                    xs_ref, wgu_hbm, bgu_ref, wd_hbm, bd_ref, out_ref,
                    wgu_f32, wd_f32, wgu_bf, wd_bf, sems):
    d_ff = wd_bf.shape[0]

    def fetch(e, sl):
        return (pltpu.make_async_copy(wgu_hbm.at[e], wgu_f32.at[sl], sems.at[0, sl]),
                pltpu.make_async_copy(wd_hbm.at[e], wd_f32.at[sl], sems.at[1, sl]))

    def load_weights_if_first(i):
        sl = slot_ref[i]
        e = blk_exp_ref[i]

        @pl.when(first_ref[i] == 1)
        def _():
            @pl.when(i == 0)
            def _():
                for c in fetch(e, sl):
                    c.start()

            for c in fetch(e, sl):
                c.wait()
            wgu_bf[...] = wgu_f32[sl].astype(jnp.bfloat16)
            wd_bf[...] = wd_f32[sl].astype(jnp.bfloat16)

            @pl.when(next_ref[i] >= 0)
            def _():
                for c in fetch(next_ref[i], 1 - sl):
                    c.start()

    def compute(rows, e):
        x = _unpack_bf16_pairs(xs_ref[rows, :]).astype(jnp.bfloat16)
        gu = jnp.dot(x, wgu_bf[...], preferred_element_type=jnp.float32) + bgu_ref[e]
        g = jnp.minimum(gu[:, :d_ff], SWIGLU_LIMIT)
        u = jnp.clip(gu[:, d_ff:], -SWIGLU_LIMIT, SWIGLU_LIMIT)
        act = g * jax.nn.sigmoid(SWIGLU_ALPHA * g) * (u + 1.0)
        o = jnp.dot(act.astype(jnp.bfloat16), wd_bf[...], preferred_element_type=jnp.float32) + bd_ref[e]
        out_ref[rows, :] = _pack_bf16_pairs(o)

    i0 = pl.program_id(0) * BLOCKS_PER_STEP
    n_active = n_active_ref[0]

    def run(sub0, n_sub):
        first = i0 + sub0
        rows = slice(sub0 * MOE_BLOCK, (sub0 + n_sub) * MOE_BLOCK)
        if n_sub == 1:
            @pl.when(first < n_active)
            def _():
                load_weights_if_first(first)
                compute(rows, blk_exp_ref[first])

            @pl.when(first >= n_active)
            def _():
                out_ref[rows, :] = jnp.zeros((MOE_BLOCK, out_ref.shape[1]), out_ref.dtype)
            return

        same = first + n_sub - 1 < n_active
        for sub in range(1, n_sub):
            same = same & (blk_exp_ref[first + sub] == blk_exp_ref[first])

        @pl.when(same)
        def _():
            load_weights_if_first(first)
            compute(rows, blk_exp_ref[first])

        @pl.when(jnp.logical_not(same))
        def _():
            run(sub0, n_sub // 2)
            run(sub0 + n_sub // 2, n_sub // 2)

    run(0, BLOCKS_PER_STEP)


def _experts(plan, xs, wgu, bgu, wd, bd):
    blk_exp, first, slot, nxt, n_active = plan
    n_rows, half = xs.shape
    n_exp, d, two_f = wgu.shape
    d_ff = wd.shape[1]
    n_blocks = n_rows // MOE_BLOCK
    step_rows = MOE_BLOCK * BLOCKS_PER_STEP
    assert n_rows % step_rows == 0
    grid_spec = pltpu.PrefetchScalarGridSpec(
        num_scalar_prefetch=5,
        grid=(n_blocks // BLOCKS_PER_STEP,),
        in_specs=[
            pl.BlockSpec((step_rows, half), lambda i, *_: (i, 0)),
            pl.BlockSpec(memory_space=pl.ANY),
            pl.BlockSpec((n_exp, 1, two_f), lambda i, *_: (0, 0, 0)),
            pl.BlockSpec(memory_space=pl.ANY),
            pl.BlockSpec((n_exp, 1, d), lambda i, *_: (0, 0, 0)),
        ],
        out_specs=pl.BlockSpec((step_rows, half), lambda i, *_: (i, 0)),
        scratch_shapes=[
            pltpu.VMEM((2, d, two_f), jnp.float32),
            pltpu.VMEM((2, d_ff, d), jnp.float32),
            pltpu.VMEM((d, two_f), jnp.bfloat16),
            pltpu.VMEM((d_ff, d), jnp.bfloat16),
            pltpu.SemaphoreType.DMA((2, 2)),
        ],
    )
    return pl.pallas_call(
        _experts_kernel,
        grid_spec=grid_spec,
        out_shape=jax.ShapeDtypeStruct((n_rows, half), jnp.uint32),
        compiler_params=pltpu.CompilerParams(
            dimension_semantics=("arbitrary",), vmem_limit_bytes=VMEM_LIMIT),
        name="experts",
    )(blk_exp, first, slot, nxt, n_active, xs, wgu, bgu.reshape(n_exp, 1, two_f), wd,
      bd.reshape(n_exp, 1, d))


def _combine_kernel(rows_ref, gate_ref, h_ref, gfin_ref, *rest):
    y_ref = rest[-1]
    acc = h_ref[...]
    for k in range(TOP_K):
        acc = acc + gate_ref[:, k:k + 1] * _unpack_bf16_pairs(rows_ref[k])
    y_ref[...] = _rms(acc, gfin_ref[...])


def _combine(rows4, gate_t, h2d, gfin, *, tok0=0, y_prev=None, y_rows=None, y_tok0=0):
    d = h2d.shape[1]
    n, half = rows4.shape[1:]
    n_tok = h2d.shape[0] if y_rows is None else y_rows
    tile = min(n, 256)
    t0 = tok0 // tile
    y0 = y_tok0 // tile
    in_specs = [
        pl.BlockSpec((TOP_K, tile, half), lambda i: (0, i, 0)),
        pl.BlockSpec((tile, TOP_K), lambda i: (t0 + i, 0)),
        pl.BlockSpec((tile, d), lambda i: (t0 + i, 0)),
        pl.BlockSpec((1, d), lambda i: (0, 0)),
    ]
    args = [rows4, gate_t, h2d, gfin]
    aliases = {}
    if y_prev is not None:
        in_specs.append(pl.BlockSpec(memory_space=pl.ANY))
        args.append(y_prev)
        aliases = {4: 0}
    return pl.pallas_call(
        _combine_kernel,
        grid=(n // tile,),
        in_specs=in_specs,
        out_specs=pl.BlockSpec((tile, d), lambda i: (y0 + i, 0)),
        out_shape=jax.ShapeDtypeStruct((n_tok, d), jnp.float32),
        input_output_aliases=aliases,
        compiler_params=pltpu.CompilerParams(dimension_semantics=("arbitrary",)),
        name="combine",
    )(*args)


def _pad_hist(hist, rows):
    return jnp.pad(hist, ((0, 0), (rows - hist.shape[1], 0), (0, 0)))


def kernel(x_prompt, x_sample, state_conv_a, state_conv_b, norm_mix_g, w_in, conv_a_w, conv_b_w, conv_b_b, conv_ln_g, conv_ln_b, out_norm_a_g, out_norm_b_g, w_out, norm_ffn_g, w_router, b_router, w_gate_up, b_gate_up, w_down, b_down, final_norm_g):
    depth = w_in.shape[0]
    assert depth == 1, "single-layer trunk"
    bf16 = jnp.bfloat16
    bsz, seq, d = x_prompt.shape
    dec_b, dec_s, _ = x_sample.shape
    d_a, d_b = conv_a_w.shape[-1], conv_b_w.shape[-1]
    n_exp = w_router.shape[-1]

    params = (norm_mix_g[0][None], w_in[0].astype(bf16), conv_a_w[0], conv_b_w[0], conv_b_b[0][None],
              conv_ln_g[0][None], conv_ln_b[0][None], out_norm_a_g[0][None], out_norm_b_g[0][None],
              w_out[0].astype(bf16), norm_ffn_g[0][None], w_router[0].T.astype(bf16),
              b_router[0][:, None])

    zero_a = jnp.zeros((bsz, HIST_A_ROWS, d_a), jnp.float32)
    zero_b = jnp.zeros((bsz, HIST_B_ROWS, d_b), jnp.float32)
    cnt0 = jnp.zeros((n_exp, 1), jnp.float32)
    gfin = final_norm_g[None]
    ts = min(seq, MIXER_ROWS)
    n_p = bsz * seq

    sizes = [bsz * f // sum(GROUP_SPLIT) for f in GROUP_SPLIT]
    if bsz % sum(GROUP_SPLIT) != 0:
        sizes = [bsz]
    groups = []
    tok_start = []
    for g, per_b in enumerate(sizes):
        b0 = sum(sizes[:g])
        tok_start.append(b0 * seq)
        cnt_in = cnt0
        if groups:
            cnt_in, _ = lax.optimization_barrier((cnt0, groups[-1]["cnt"]))
        h, hnp, idx, gate, rank, cnt, na, nb_ = _mixer(
            x_prompt, zero_a, zero_b, cnt_in, params, nb=1, ts=ts, b0=b0, bsz=per_b)
        parts = [dict(h=h.reshape(per_b * seq, d), hnp=hnp, idx=idx, gate_t=gate.T, rank=rank)]
        states = [(na, nb_)]
        if g == len(sizes) - 1:
            h, hnp, idx, gate, rank, cnt, na_s, nb_s = _mixer(
                x_sample, _pad_hist(state_conv_a[0], HIST_A_ROWS), _pad_hist(state_conv_b[0], HIST_B_ROWS),
                cnt, params, nb=dec_b, ts=dec_s)
            parts.append(dict(h=h.reshape(dec_b * dec_s, d), hnp=hnp, idx=idx, gate_t=gate.T, rank=rank))
        n_pairs = sum(p["hnp"].shape[0] for p in parts) * TOP_K
        plan, pad_start, n_blocks = _block_plan(cnt[:, 0].astype(jnp.int32), n_pairs, n_exp)
        pos = _positions(pad_start, [(p["idx"], p["rank"]) for p in parts])
        xs = _sc_dispatch([(q, p["hnp"]) for q, p in zip(pos, parts)], n_blocks * MOE_BLOCK)
        groups.append(dict(parts=parts, pos=pos, plan=plan, xs=xs, states=states, cnt=cnt))

    for g, grp in enumerate(groups):
        xs = grp["xs"]
        if g:
            xs, _ = lax.optimization_barrier((xs, groups[g - 1]["rows"]))
        grp["rows"] = _experts(grp["plan"], xs, w_gate_up[0], b_gate_up[0], w_down[0], b_down[0])

    gathered = []
    prev = groups[-1]["xs"]
    for g, grp in enumerate(groups):
        pos0 = grp["pos"][0]
        n_g = pos0.shape[1]
        n_chunks = COMBINE_CHUNKS if n_g % (COMBINE_CHUNKS * SC_CHUNK * SC_WORKERS) == 0 else 1
        per = n_g // n_chunks
        for c in range(n_chunks):
            pos_parts = [pos0[:, c * per:(c + 1) * per]] + (grp["pos"][1:] if c == 0 else [])
            if prev is not None:
                pos_parts[0], _ = lax.optimization_barrier((pos_parts[0], prev))
            got = _sc_gather(pos_parts, grp["rows"])
            prev = got[0]
            gathered.append((g, c * per, got))

    y_p = None
    for g, tok0, got in gathered:
        grp = groups[g]
        main = grp["parts"][0]
        if len(got) > 1:
            smp = grp["parts"][1]
            y_s = _combine(got[1], smp["gate_t"], smp["h"], gfin).reshape(dec_b, dec_s, d)
        y_p = _combine(got[0], main["gate_t"], main["h"], gfin, tok0=tok0,
                       y_prev=y_p, y_rows=n_p, y_tok0=tok_start[g] + tok0)
    y_p = y_p.reshape(bsz, seq, d)

    na_p = jnp.concatenate([grp["states"][0][0] for grp in groups], axis=0)
    nb_p = jnp.concatenate([grp["states"][0][1] for grp in groups], axis=0)
    return (y_p, y_s, na_p[None], nb_p[None], na_s[None], nb_s[None])


def _block_plan(counts, n_pairs, n_exp):
    padded = (counts + MOE_BLOCK - 1) // MOE_BLOCK * MOE_BLOCK
    pad_end = jnp.cumsum(padded)
    pad_start = pad_end - padded
    n_blocks = -(-n_pairs // MOE_BLOCK) + n_exp
    n_blocks = -(-n_blocks // BLOCKS_PER_STEP) * BLOCKS_PER_STEP
    blk_start = jnp.arange(n_blocks, dtype=jnp.int32) * MOE_BLOCK
    blk_exp = jnp.minimum(jnp.sum(blk_start[:, None] >= pad_end[None, :], axis=1),
                          n_exp - 1).astype(jnp.int32)
    n_active = (pad_end[-1:] // MOE_BLOCK).astype(jnp.int32)
    e_ids = jnp.arange(n_exp, dtype=jnp.int32)
    has = padded > 0
    slot_e = (jnp.cumsum(has.astype(jnp.int32)) - 1) % 2
    later = jnp.where(has, e_ids, n_exp)
    next_e = jnp.concatenate([lax.cummin(later, reverse=True)[1:], jnp.full((1,), n_exp, jnp.int32)])
    next_e = jnp.where(next_e >= n_exp, -1, next_e)
    blk_onehot = blk_exp[:, None] == e_ids[None, :]

    def per_block(table):
        return jnp.sum(jnp.where(blk_onehot, table[None, :], 0), axis=1).astype(jnp.int32)

    blk_first = ((blk_start == per_block(pad_start)) & (blk_start < pad_end[-1])).astype(jnp.int32)
    plan = (blk_exp, blk_first, per_block(slot_e), per_block(next_e), n_active)
    return plan, pad_start.astype(jnp.int32), n_blocks
```

```python
import functools

import jax
import jax.numpy as jnp
from jax import lax
from jax.experimental import pallas as pl
from jax.experimental.pallas import tpu as pltpu
from jax.experimental.pallas import tpu_sc as plsc

SC_CORES = 2
SC_SUBCORES = 16
SC_WORKERS = SC_CORES * SC_SUBCORES
SC_CHUNK = 128
CONV_A_W = 3
CONV_B_W = 31
TOP_K = 4
SWIGLU_LIMIT = 7.0
SWIGLU_ALPHA = 1.702
NORM_EPS = 1e-5
MOE_BLOCK = 256
BLOCKS_PER_STEP = 4
SUBLANES = 8
LANES = 128
CONV_ROW_CHUNK = 128
HIST_A_ROWS = 8
HIST_B_ROWS = 32
MIXER_ROWS = 512
PROJ_SECTION = 256
POS_TILE = 4096
GROUP_SPLIT = (1, 1)
COMBINE_CHUNKS = 2
VMEM_LIMIT = 56 * 1024 * 1024


def _rms(x, g):
    return x * lax.rsqrt(jnp.mean(x * x, axis=-1, keepdims=True) + NORM_EPS) * g


def _zero_tile_from(v):
    bits = pltpu.bitcast(v[0:SUBLANES, 0:LANES], jnp.uint32)
    return pltpu.bitcast((bits >> 16) >> 16, jnp.float32)


def _pack_bf16_pairs(a):
    w = a.shape[-1] // 2
    lo = pltpu.bitcast(a[:, :w].astype(jnp.bfloat16).astype(jnp.float32), jnp.uint32)
    hi = pltpu.bitcast(a[:, w:].astype(jnp.bfloat16).astype(jnp.float32), jnp.uint32)
    return (lo >> 16) | (hi & jnp.uint32(0xFFFF0000))


def _unpack_bf16_pairs(wd):
    lo = pltpu.bitcast(wd << 16, jnp.float32)
    hi = pltpu.bitcast(wd & jnp.uint32(0xFFFF0000), jnp.float32)
    return jnp.concatenate([lo, hi], axis=-1)


def _conv_b_chunk(ub_buf, cbw_ref, n_i, r0, rc, c0):
    cs = slice(c0, c0 + LANES)
    lead = HIST_B_ROWS - (CONV_B_W - 1)
    acc = None
    for b in range(SUBLANES):
        rows = rc if b == 0 else rc + SUBLANES
        q = None
        for a in range((HIST_B_ROWS + SUBLANES) // SUBLANES):
            k = SUBLANES * a + b - lead
            if k < 0 or k >= CONV_B_W:
                continue
            start = r0 + SUBLANES * a
            term = cbw_ref[k:k + 1, cs] * ub_buf[n_i, start:start + rows, cs]
            q = term if q is None else q + term
        part = q if b == 0 else q[b:b + rc]
        acc = part if acc is None else acc + part
    return acc


def _mixer_kernel(x_ref, hista_ref, histb_ref, cnt0_ref, tri_ref,
                  gmix_ref, win_ref, caw_ref, cbw_ref, cbb_ref, lng_ref, lnb_ref,
                  ga_ref, gb_ref, wout_ref, gffn_ref, wrt_ref, br_ref,
                  h_ref, hnp_ref, idx_ref, gate_ref, rank_ref, cnt_ref, newa_ref, newb_ref,
                  ua_buf, ub_buf, cb_buf, cnt_acc, *, nb, ts, d_a, d_b, n_exp):
    b = pl.program_id(0)
    s = pl.program_id(1)
    m = nb * ts
    d = x_ref.shape[-1]

    @pl.when(s == 0)
    def _():
        ua_buf[:, 0:HIST_A_ROWS, :] = hista_ref[...]
        ub_buf[:, 0:HIST_B_ROWS, :] = histb_ref[...]

    @pl.when(s != 0)
    def _():
        ua_buf[:, 0:HIST_A_ROWS, :] = ua_buf[:, ts:ts + HIST_A_ROWS, :]
        ub_buf[:, 0:HIST_B_ROWS, :] = ub_buf[:, ts:ts + HIST_B_ROWS, :]

    @pl.when((b == 0) & (s == 0))
    def _():
        cnt_acc[...] = cnt0_ref[...]

    x = x_ref[...].reshape(m, d)
    n = _rms(x, gmix_ref[...]).astype(jnp.bfloat16)
    def proj(c0, width):
        return jnp.dot(n, win_ref[:, c0:c0 + width], preferred_element_type=jnp.float32)

    sec = min(d_b, PROJ_SECTION)
    for c0 in range(0, d_b, sec):
        v_b = proj(3 * d_a + c0, sec)
        g_b = proj(3 * d_a + d_b + c0, sec)
        ub_buf[:, HIST_B_ROWS:HIST_B_ROWS + ts, c0:c0 + sec] = (v_b * jax.nn.sigmoid(g_b)).reshape(nb, ts, sec)

    gate_c = proj(d_a, d_a)
    xt = proj(2 * d_a, d_a)
    gate_b = proj(0, d_a)
    ua_buf[:, HIST_A_ROWS:HIST_A_ROWS + ts, :] = (gate_c * xt).reshape(nb, ts, d_a)
    conv_a = jnp.zeros((nb, ts, d_a), jnp.float32)
    for k in range(CONV_A_W):
        off = HIST_A_ROWS - (CONV_A_W - 1) + k
        conv_a = conv_a + caw_ref[k:k + 1, :] * ua_buf[:, off:off + ts, :]
    y_a = gate_b * conv_a.reshape(m, d_a)

    mix_a = jnp.dot(_rms(y_a, ga_ref[...]).astype(jnp.bfloat16), wout_ref[0:d_a, :],
                    preferred_element_type=jnp.float32)

    rc = min(ts, CONV_ROW_CHUNK)
    chunks = [(n_i, r0, c0) for n_i in range(nb) for r0 in range(0, ts, rc) for c0 in range(0, d_b, LANES)]
    anchors = {len(chunks) * 2 // 8: gate_c, len(chunks) * 3 // 8: xt, len(chunks) * 5 // 8: gate_b,
               len(chunks) - 1: mix_a}
    for ci, (n_i, r0, c0) in enumerate(chunks):
        acc = _conv_b_chunk(ub_buf, cbw_ref, n_i, r0, rc, c0)
        if ci in anchors:
            acc = (acc.reshape(rc // SUBLANES, SUBLANES, LANES) + _zero_tile_from(anchors[ci])[None]
                   ).reshape(rc, LANES)
        cb_buf[n_i, r0:r0 + rc, c0:c0 + LANES] = acc
    cb = cb_buf[...].reshape(m, d_b) + cbb_ref[...]
    mu = jnp.mean(cb, axis=-1, keepdims=True)
    xc = cb - mu
    var = jnp.mean(xc * xc, axis=-1, keepdims=True)
    ln = xc * lax.rsqrt(var + NORM_EPS) * lng_ref[...] + lnb_ref[...]
    y_b = ln * jax.nn.sigmoid(ln)

    mix_b = jnp.dot(_rms(y_b, gb_ref[...]).astype(jnp.bfloat16), wout_ref[d_a:d_a + d_b, :],
                    preferred_element_type=jnp.float32)
    h = x + (mix_a + mix_b)
    h_ref[...] = h.reshape(nb, ts, d)

    hn = _rms(h, gffn_ref[...])
    hnp_ref[...] = _pack_bf16_pairs(hn)

    logits = lax.dot_general(wrt_ref[...], hn.astype(jnp.bfloat16), (((1,), (1,)), ((), ())),
                             preferred_element_type=jnp.float32) + br_ref[...]
    e_iota = lax.broadcasted_iota(jnp.int32, (n_exp, m), 0)
    cur = logits
    vals, idxs, sels = [], [], []
    for _ in range(TOP_K):
        mx = jnp.max(cur, axis=0, keepdims=True)
        ix = jnp.min(jnp.where(cur == mx, e_iota, n_exp), axis=0, keepdims=True)
        sel = e_iota == ix
        vals.append(mx)
        idxs.append(ix)
        sels.append(sel)
        cur = jnp.where(sel, -jnp.inf, cur)
    exps = [jnp.exp(v - vals[0]) for v in vals]
    denom = exps[0] + exps[1] + exps[2] + exps[3]
    gates = [e / denom for e in exps]

    onehot = jnp.zeros((n_exp, m), jnp.float32)
    for sel in sels:
        onehot = onehot + sel.astype(jnp.float32)
    before = jnp.dot(onehot.astype(jnp.bfloat16), tri_ref[...],
                     preferred_element_type=jnp.float32) + cnt_acc[...]
    ranks = [jnp.sum(jnp.where(sel, before, 0.0), axis=0, keepdims=True) for sel in sels]
    new_cnt = cnt_acc[...] + jnp.sum(onehot, axis=1, keepdims=True)
    cnt_acc[...] = new_cnt
    cnt_ref[...] = new_cnt

    idx_ref[...] = jnp.concatenate(idxs, axis=0)
    gate_ref[...] = jnp.concatenate(gates, axis=0)
    rank_ref[...] = jnp.concatenate(ranks, axis=0).astype(jnp.int32)

    @pl.when(s == pl.num_programs(1) - 1)
    def _():
        newa_ref[...] = ua_buf[:, ts + HIST_A_ROWS - (CONV_A_W - 1):ts + HIST_A_ROWS, :]
        newb_ref[...] = ub_buf[:, ts + HIST_B_ROWS - (CONV_B_W - 1):ts + HIST_B_ROWS, :]


def _mixer(x, hist_a, hist_b, cnt0, params, *, nb, ts, b0=0, bsz=None):
    (gmix, win, caw, cbw, cbb, lng, lnb, ga, gb, wout, gffn, wrt, br) = params
    _, seq, d = x.shape
    bsz = x.shape[0] if bsz is None else bsz
    d_a, d_b = caw.shape[-1], cbw.shape[-1]
    n_exp = wrt.shape[0]
    m = nb * ts
    n_tok = bsz * seq
    g0 = b0 // nb
    grid = (bsz // nb, seq // ts)
    tri = (jnp.arange(m)[:, None] < jnp.arange(m)[None, :]).astype(jnp.bfloat16)

    def const(shape):
        return pl.BlockSpec(shape, lambda i, j: (0,) * len(shape))

    tok_map = lambda i, j: (0, i * (seq // ts) + j)
    kern = functools.partial(_mixer_kernel, nb=nb, ts=ts, d_a=d_a, d_b=d_b, n_exp=n_exp)
    out_shape = (
        jax.ShapeDtypeStruct((bsz, seq, d), jnp.float32),
        jax.ShapeDtypeStruct((n_tok, d // 2), jnp.uint32),
        jax.ShapeDtypeStruct((TOP_K, n_tok), jnp.int32),
        jax.ShapeDtypeStruct((TOP_K, n_tok), jnp.float32),
        jax.ShapeDtypeStruct((TOP_K, n_tok), jnp.int32),
        jax.ShapeDtypeStruct((n_exp, 1), jnp.float32),
        jax.ShapeDtypeStruct((bsz, CONV_A_W - 1, d_a), jnp.float32),
        jax.ShapeDtypeStruct((bsz, CONV_B_W - 1, d_b), jnp.float32),
    )
    return pl.pallas_call(
        kern,
        grid=grid,
        in_specs=[
            pl.BlockSpec((nb, ts, d), lambda i, j: (g0 + i, j, 0)),
            pl.BlockSpec((nb, HIST_A_ROWS, d_a), lambda i, j: (g0 + i, 0, 0)),
            pl.BlockSpec((nb, HIST_B_ROWS, d_b), lambda i, j: (g0 + i, 0, 0)),
            const((n_exp, 1)),
            const((m, m)),
            const((1, d)), const(win.shape), const(caw.shape), const(cbw.shape), const((1, d_b)),
            const((1, d_b)), const((1, d_b)), const((1, d_a)), const((1, d_b)), const(wout.shape),
            const((1, d)), const(wrt.shape), const((n_exp, 1)),
        ],
        out_specs=(
            pl.BlockSpec((nb, ts, d), lambda i, j: (i, j, 0)),
            pl.BlockSpec((m, d // 2), lambda i, j: (i * (seq // ts) + j, 0)),
            pl.BlockSpec((TOP_K, m), tok_map),
            pl.BlockSpec((TOP_K, m), tok_map),
            pl.BlockSpec((TOP_K, m), tok_map),
            pl.BlockSpec((n_exp, 1), lambda i, j: (0, 0)),
            pl.BlockSpec((nb, CONV_A_W - 1, d_a), lambda i, j: (i, 0, 0)),
            pl.BlockSpec((nb, CONV_B_W - 1, d_b), lambda i, j: (i, 0, 0)),
        ),
        out_shape=out_shape,
        scratch_shapes=[
            pltpu.VMEM((nb, HIST_A_ROWS + ts, d_a), jnp.float32),
            pltpu.VMEM((nb, HIST_B_ROWS + ts, d_b), jnp.float32),
            pltpu.VMEM((nb, ts, d_b), jnp.float32),
            pltpu.VMEM((n_exp, 1), jnp.float32),
        ],
        compiler_params=pltpu.CompilerParams(
            dimension_semantics=("arbitrary", "arbitrary"), vmem_limit_bytes=VMEM_LIMIT),
        name="mixer",
    )(x, hist_a, hist_b, cnt0, tri, gmix, win, caw, cbw, cbb, lng, lnb, ga, gb, wout, gffn, wrt, br)


def _positions_kernel(start_ref, *refs):
    n_parts = len(refs) // 3
    for p in range(n_parts):
        idx = refs[2 * p][...]
        pos = refs[2 * p + 1][...]
        for e in range(start_ref.shape[0]):
            pos = pos + jnp.where(idx == e, start_ref[e], 0)
        refs[2 * n_parts + p][...] = pos


def _positions(pad_start, parts):
    n_0 = parts[0][0].shape[1]
    tile = min(n_0, POS_TILE)
    specs = [pl.BlockSpec((TOP_K, tile), lambda i: (0, i))]
    specs += [pl.BlockSpec((TOP_K, idx.shape[1]), lambda i: (0, 0)) for idx, _ in parts[1:]]
    in_specs = [pl.BlockSpec(memory_space=pltpu.SMEM)]
    args = [pad_start]
    for spec, (idx, rank) in zip(specs, parts):
        in_specs += [spec, spec]
        args += [idx, rank]
    out = pl.pallas_call(
        _positions_kernel,
        grid=(n_0 // tile,),
        in_specs=in_specs,
        out_specs=tuple(specs),
        out_shape=tuple(jax.ShapeDtypeStruct(idx.shape, jnp.int32) for idx, _ in parts),
        compiler_params=pltpu.CompilerParams(dimension_semantics=("arbitrary",)),
        name="positions",
    )(*args)
    return list(out)


def _sc_mesh():
    return plsc.VectorSubcoreMesh(core_axis_name="c", subcore_axis_name="s",
                                  num_cores=SC_CORES, num_subcores=SC_SUBCORES)


def _sc_worker_id():
    return lax.axis_index("s") * SC_CORES + lax.axis_index("c")


def _sc_chunks(n_tok, wid):
    n_chunks = n_tok // SC_CHUNK
    assert n_tok % SC_CHUNK == 0
    if n_chunks % SC_WORKERS == 0:
        per_w = n_chunks // SC_WORKERS
        return None, per_w, lambda i: (wid * per_w + i) * SC_CHUNK
    assert n_chunks <= SC_WORKERS
    return wid < n_chunks, 1, lambda i: wid * SC_CHUNK


def _sc_for_each_chunk(n_tok, fn):
    wid = _sc_worker_id()
    pred, trips, base = _sc_chunks(n_tok, wid)

    def run():
        @pl.loop(0, trips)
        def _(i):
            fn(pl.multiple_of(base(i), SC_CHUNK))

    if pred is None:
        run()
    else:
        pl.when(pred)(run)


def _sc_dispatch(parts, n_rows):
    half = parts[0][1].shape[1]
    n_parts = len(parts)

    @functools.partial(
        pl.kernel, mesh=_sc_mesh(),
        out_type=jax.ShapeDtypeStruct((n_rows, half), jnp.uint32),
        scratch_types=[pltpu.VMEM((TOP_K, SC_CHUNK), jnp.int32),
                       pltpu.VMEM((SC_CHUNK, half), jnp.uint32),
                       pltpu.SemaphoreType.DMA],
        compiler_params=pltpu.CompilerParams(use_tc_tiling_on_sc=True),
        name="sc_dispatch")
    def k(*refs):
        xs_hbm, idx_v, rows_v, sem = refs[2 * n_parts:]

        def move(pos_hbm, src_hbm):
            def fn(base):
                pltpu.sync_copy(pos_hbm.at[:, pl.ds(base, SC_CHUNK)], idx_v)
                pltpu.sync_copy(src_hbm.at[pl.ds(base, SC_CHUNK)], rows_v)
                copies = [pltpu.async_copy(rows_v, xs_hbm.at[idx_v.at[j]], sem) for j in range(TOP_K)]
                for c in copies:
                    c.wait()
            return fn

        for p, (_, hnp) in enumerate(parts):
            _sc_for_each_chunk(hnp.shape[0], move(refs[2 * p], refs[2 * p + 1]))

    return k(*[a for part in parts for a in part])


def _sc_gather(pos_parts, rows):
    half = rows.shape[1]
    n_parts = len(pos_parts)

    @functools.partial(
        pl.kernel, mesh=_sc_mesh(),
        out_type=tuple(jax.ShapeDtypeStruct((TOP_K, p.shape[1], half), jnp.uint32) for p in pos_parts),
        scratch_types=[pltpu.VMEM((TOP_K, SC_CHUNK), jnp.int32),
                       pltpu.VMEM((SC_CHUNK, half), jnp.uint32),
                       pltpu.SemaphoreType.DMA],
        compiler_params=pltpu.CompilerParams(use_tc_tiling_on_sc=True),
        name="sc_gather")
    def k(*refs):
        pos_hbms, rows_hbm = refs[:n_parts], refs[n_parts]
        out_hbms = refs[n_parts + 1:2 * n_parts + 1]
        idx_v, buf_v, sem = refs[2 * n_parts + 1:]

        def move(pos_hbm, out_hbm):
            def fn(base):
                pltpu.sync_copy(pos_hbm.at[:, pl.ds(base, SC_CHUNK)], idx_v)
                for j in range(TOP_K):
                    pltpu.async_copy(rows_hbm.at[idx_v.at[j]], buf_v, sem).wait()
                    pltpu.sync_copy(buf_v, out_hbm.at[j, pl.ds(base, SC_CHUNK)])
            return fn

        for pos_hbm, out_hbm, p in zip(pos_hbms, out_hbms, pos_parts):
            _sc_for_each_chunk(p.shape[1], move(pos_hbm, out_hbm))

    out = k(*pos_parts, rows)
    return out if isinstance(out, (tuple, list)) else (out,)


def _experts_kernel(blk_exp_ref, first_ref, slot_ref, next_ref, n_active_ref,
                    xs_ref, wgu_hbm, bgu_ref, wd_hbm, bd_ref, out_ref,
                    wgu_f32, wd_f32, wgu_bf, wd_bf, sems):
    d_ff = wd_bf.shape[0]

    def fetch(e, sl):
        return (pltpu.make_async_copy(wgu_hbm.at[e], wgu_f32.at[sl], sems.at[0, sl]),
                pltpu.make_async_copy(wd_hbm.at[e], wd_f32.at[sl], sems.at[1, sl]))

    def load_weights_if_first(i):
        sl = slot_ref[i]
        e = blk_exp_ref[i]

        @pl.when(first_ref[i] == 1)
        def _():
            @pl.when(i == 0)
            def _():
                for c in fetch(e, sl):
                    c.start()

            for c in fetch(e, sl):
                c.wait()
            wgu_bf[...] = wgu_f32[sl].astype(jnp.bfloat16)
            wd_bf[...] = wd_f32[sl].astype(jnp.bfloat16)

            @pl.when(next_ref[i] >= 0)
            def _():
                for c in fetch(next_ref[i], 1 - sl):
                    c.start()

    def compute(rows, e):
        x = _unpack_bf16_pairs(xs_ref[rows, :]).astype(jnp.bfloat16)
        gu = jnp.dot(x, wgu_bf[...], preferred_element_type=jnp.float32) + bgu_ref[e]
        g = jnp.minimum(gu[:, :d_ff], SWIGLU_LIMIT)
        u = jnp.clip(gu[:, d_ff:], -SWIGLU_LIMIT, SWIGLU_LIMIT)
        act = g * jax.nn.sigmoid(SWIGLU_ALPHA * g) * (u + 1.0)
        o = jnp.dot(act.astype(jnp.bfloat16), wd_bf[...], preferred_element_type=jnp.float32) + bd_ref[e]
        out_ref[rows, :] = _pack_bf16_pairs(o)

    i0 = pl.program_id(0) * BLOCKS_PER_STEP
    n_active = n_active_ref[0]

    def run(sub0, n_sub):
        first = i0 + sub0
        rows = slice(sub0 * MOE_BLOCK, (sub0 + n_sub) * MOE_BLOCK)
        if n_sub == 1:
            @pl.when(first < n_active)
            def _():
                load_weights_if_first(first)
                compute(rows, blk_exp_ref[first])

            @pl.when(first >= n_active)
            def _():
                out_ref[rows, :] = jnp.zeros((MOE_BLOCK, out_ref.shape[1]), out_ref.dtype)
            return

        same = first + n_sub - 1 < n_active
        for sub in range(1, n_sub):
            same = same & (blk_exp_ref[first + sub] == blk_exp_ref[first])

        @pl.when(same)
        def _():
            load_weights_if_first(first)
            compute(rows, blk_exp_ref[first])

        @pl.when(jnp.logical_not(same))
        def _():
            run(sub0, n_sub // 2)
            run(sub0 + n_sub // 2, n_sub // 2)

    run(0, BLOCKS_PER_STEP)


def _experts(plan, xs, wgu, bgu, wd, bd):
    blk_exp, first, slot, nxt, n_active = plan
    n_rows, half = xs.shape
    n_exp, d, two_f = wgu.shape
    d_ff = wd.shape[1]
    n_blocks = n_rows // MOE_BLOCK
    step_rows = MOE_BLOCK * BLOCKS_PER_STEP
    assert n_rows % step_rows == 0
    grid_spec = pltpu.PrefetchScalarGridSpec(
        num_scalar_prefetch=5,
        grid=(n_blocks // BLOCKS_PER_STEP,),
        in_specs=[
            pl.BlockSpec((step_rows, half), lambda i, *_: (i, 0)),
            pl.BlockSpec(memory_space=pl.ANY),
            pl.BlockSpec((n_exp, 1, two_f), lambda i, *_: (0, 0, 0)),
            pl.BlockSpec(memory_space=pl.ANY),
            pl.BlockSpec((n_exp, 1, d), lambda i, *_: (0, 0, 0)),
        ],
        out_specs=pl.BlockSpec((step_rows, half), lambda i, *_: (i, 0)),
        scratch_shapes=[
            pltpu.VMEM((2, d, two_f), jnp.float32),
            pltpu.VMEM((2, d_ff, d), jnp.float32),
            pltpu.VMEM((d, two_f), jnp.bfloat16),
            pltpu.VMEM((d_ff, d), jnp.bfloat16),
            pltpu.SemaphoreType.DMA((2, 2)),
        ],
    )
    return pl.pallas_call(
        _experts_kernel,
        grid_spec=grid_spec,
        out_shape=jax.ShapeDtypeStruct((n_rows, half), jnp.uint32),
        compiler_params=pltpu.CompilerParams(
            dimension_semantics=("arbitrary",), vmem_limit_bytes=VMEM_LIMIT),
        name="experts",
    )(blk_exp, first, slot, nxt, n_active, xs, wgu, bgu.reshape(n_exp, 1, two_f), wd,
      bd.reshape(n_exp, 1, d))


def _combine_kernel(rows_ref, gate_ref, h_ref, gfin_ref, *rest):
    y_ref = rest[-1]
    acc = h_ref[...]
    for k in range(TOP_K):
        acc = acc + gate_ref[:, k:k + 1] * _unpack_bf16_pairs(rows_ref[k])
    y_ref[...] = _rms(acc, gfin_ref[...])


def _combine(rows4, gate_t, h2d, gfin, *, tok0=0, y_prev=None, y_rows=None, y_tok0=0):
    d = h2d.shape[1]
    n, half = rows4.shape[1:]
    n_tok = h2d.shape[0] if y_rows is None else y_rows
    tile = min(n, 256)
    t0 = tok0 // tile
    y0 = y_tok0 // tile
    in_specs = [
        pl.BlockSpec((TOP_K, tile, half), lambda i: (0, i, 0)),
        pl.BlockSpec((tile, TOP_K), lambda i: (t0 + i, 0)),
        pl.BlockSpec((tile, d), lambda i: (t0 + i, 0)),
        pl.BlockSpec((1, d), lambda i: (0, 0)),
    ]
    args = [rows4, gate_t, h2d, gfin]
    aliases = {}
    if y_prev is not None:
        in_specs.append(pl.BlockSpec(memory_space=pl.ANY))
        args.append(y_prev)
        aliases = {4: 0}
    return pl.pallas_call(
        _combine_kernel,
        grid=(n // tile,),
        in_specs=in_specs,
        out_specs=pl.BlockSpec((tile, d), lambda i: (y0 + i, 0)),
        out_shape=jax.ShapeDtypeStruct((n_tok, d), jnp.float32),
        input_output_aliases=aliases,
        compiler_params=pltpu.CompilerParams(dimension_semantics=("arbitrary",)),
        name="combine",
    )(*args)


def _pad_hist(hist, rows):
    return jnp.pad(hist, ((0, 0), (rows - hist.shape[1], 0), (0, 0)))


def kernel(x_prompt, x_sample, state_conv_a, state_conv_b, norm_mix_g, w_in, conv_a_w, conv_b_w, conv_b_b, conv_ln_g, conv_ln_b, out_norm_a_g, out_norm_b_g, w_out, norm_ffn_g, w_router, b_router, w_gate_up, b_gate_up, w_down, b_down, final_norm_g):
    depth = w_in.shape[0]
    assert depth == 1, "single-layer trunk"
    bf16 = jnp.bfloat16
    bsz, seq, d = x_prompt.shape
    dec_b, dec_s, _ = x_sample.shape
    d_a, d_b = conv_a_w.shape[-1], conv_b_w.shape[-1]
    n_exp = w_router.shape[-1]

    params = (norm_mix_g[0][None], w_in[0].astype(bf16), conv_a_w[0], conv_b_w[0], conv_b_b[0][None],
              conv_ln_g[0][None], conv_ln_b[0][None], out_norm_a_g[0][None], out_norm_b_g[0][None],
              w_out[0].astype(bf16), norm_ffn_g[0][None], w_router[0].T.astype(bf16),
              b_router[0][:, None])

    zero_a = jnp.zeros((bsz, HIST_A_ROWS, d_a), jnp.float32)
    zero_b = jnp.zeros((bsz, HIST_B_ROWS, d_b), jnp.float32)
    cnt0 = jnp.zeros((n_exp, 1), jnp.float32)
    gfin = final_norm_g[None]
    ts = min(seq, MIXER_ROWS)
    n_p = bsz * seq

    sizes = [bsz * f // sum(GROUP_SPLIT) for f in GROUP_SPLIT]
    if bsz % sum(GROUP_SPLIT) != 0:
        sizes = [bsz]
    groups = []
    tok_start = []
    for g, per_b in enumerate(sizes):
        b0 = sum(sizes[:g])
        tok_start.append(b0 * seq)
        h, hnp, idx, gate, rank, cnt, na, nb_ = _mixer(
            x_prompt, zero_a, zero_b, cnt0, params, nb=1, ts=ts, b0=b0, bsz=per_b)
        parts = [dict(h=h.reshape(per_b * seq, d), hnp=hnp, idx=idx, gate_t=gate.T, rank=rank)]
        states = [(na, nb_)]
        if g == len(sizes) - 1:
            h, hnp, idx, gate, rank, cnt, na_s, nb_s = _mixer(
                x_sample, _pad_hist(state_conv_a[0], HIST_A_ROWS), _pad_hist(state_conv_b[0], HIST_B_ROWS),
                cnt, params, nb=dec_b, ts=dec_s)
            parts.append(dict(h=h.reshape(dec_b * dec_s, d), hnp=hnp, idx=idx, gate_t=gate.T, rank=rank))
        n_pairs = sum(p["hnp"].shape[0] for p in parts) * TOP_K
        plan, pad_start, n_blocks = _block_plan(cnt[:, 0].astype(jnp.int32), n_pairs, n_exp)
        pos = _positions(pad_start, [(p["idx"], p["rank"]) for p in parts])
        xs = _sc_dispatch([(q, p["hnp"]) for q, p in zip(pos, parts)], n_blocks * MOE_BLOCK)
        groups.append(dict(parts=parts, pos=pos, plan=plan, xs=xs, states=states))

    for grp in groups:
        grp["rows"] = _experts(grp["plan"], grp["xs"], w_gate_up[0], b_gate_up[0], w_down[0], b_down[0])

    gathered = []
    for g, grp in enumerate(groups):
        pos0 = grp["pos"][0]
        n_g = pos0.shape[1]
        want = 1 if (g == len(groups) - 1 and len(groups) > 1) else COMBINE_CHUNKS
        n_chunks = want if n_g % (want * SC_CHUNK * SC_WORKERS) == 0 else 1
        per = n_g // n_chunks
        for c in range(n_chunks):
            pos_parts = [pos0[:, c * per:(c + 1) * per]] + (grp["pos"][1:] if c == 0 else [])
            gathered.append((g, c * per, _sc_gather(pos_parts, grp["rows"])))

    y_p = None
    for g, tok0, got in gathered:
        grp = groups[g]
        main = grp["parts"][0]
        if len(got) > 1:
            smp = grp["parts"][1]
            y_s = _combine(got[1], smp["gate_t"], smp["h"], gfin).reshape(dec_b, dec_s, d)
        y_p = _combine(got[0], main["gate_t"], main["h"], gfin, tok0=tok0,
                       y_prev=y_p, y_rows=n_p, y_tok0=tok_start[g] + tok0)
    y_p = y_p.reshape(bsz, seq, d)

    na_p = jnp.concatenate([grp["states"][0][0] for grp in groups], axis=0)
    nb_p = jnp.concatenate([grp["states"][0][1] for grp in groups], axis=0)
    return (y_p, y_s, na_p[None], nb_p[None], na_s[None], nb_s[None])


def _block_plan(counts, n_pairs, n_exp):
    padded = (counts + MOE_BLOCK - 1) // MOE_BLOCK * MOE_BLOCK
    pad_end = jnp.cumsum(padded)
    pad_start = pad_end - padded
    n_blocks = -(-n_pairs // MOE_BLOCK) + n_exp
    n_blocks = -(-n_blocks // BLOCKS_PER_STEP) * BLOCKS_PER_STEP
    blk_start = jnp.arange(n_blocks, dtype=jnp.int32) * MOE_BLOCK
    blk_exp = jnp.minimum(jnp.sum(blk_start[:, None] >= pad_end[None, :], axis=1),
                          n_exp - 1).astype(jnp.int32)
    n_active = (pad_end[-1:] // MOE_BLOCK).astype(jnp.int32)
    e_ids = jnp.arange(n_exp, dtype=jnp.int32)
    has = padded > 0
    slot_e = (jnp.cumsum(has.astype(jnp.int32)) - 1) % 2
    later = jnp.where(has, e_ids, n_exp)
    next_e = jnp.concatenate([lax.cummin(later, reverse=True)[1:], jnp.full((1,), n_exp, jnp.int32)])
    next_e = jnp.where(next_e >= n_exp, -1, next_e)
    blk_onehot = blk_exp[:, None] == e_ids[None, :]

    def per_block(table):
        return jnp.sum(jnp.where(blk_onehot, table[None, :], 0), axis=1).astype(jnp.int32)

    blk_first = ((blk_start == per_block(pad_start)) & (blk_start < pad_end[-1])).astype(jnp.int32)
    plan = (blk_exp, blk_first, per_block(slot_e), per_block(next_e), n_active)
    return plan, pad_start.astype(jnp.int32), n_blocks
```

```python
import functools

import jax
import jax.numpy as jnp
from jax import lax
from jax.experimental import pallas as pl
from jax.experimental.pallas import tpu as pltpu
from jax.experimental.pallas import tpu_sc as plsc

SC_CORES = 2
SC_SUBCORES = 16
SC_WORKERS = SC_CORES * SC_SUBCORES
SC_CHUNK = 128
CONV_A_W = 3
CONV_B_W = 31
TOP_K = 4
SWIGLU_LIMIT = 7.0
SWIGLU_ALPHA = 1.702
NORM_EPS = 1e-5
MOE_BLOCK = 256
BLOCKS_PER_STEP = 4
SUBLANES = 8
LANES = 128
CONV_ROW_CHUNK = 128
HIST_A_ROWS = 8
HIST_B_ROWS = 32
MIXER_ROWS = 512
PROJ_SECTION = 256
POS_TILE = 4096
GROUP_SPLIT = (1, 1)
COMBINE_CHUNKS = 2
VMEM_LIMIT = 56 * 1024 * 1024


def _rms(x, g):
    return x * lax.rsqrt(jnp.mean(x * x, axis=-1, keepdims=True) + NORM_EPS) * g


def _zero_tile_from(v):
    bits = pltpu.bitcast(v[0:SUBLANES, 0:LANES], jnp.uint32)
    return pltpu.bitcast((bits >> 16) >> 16, jnp.float32)


def _pack_bf16_pairs(a):
    w = a.shape[-1] // 2
    lo = pltpu.bitcast(a[:, :w].astype(jnp.bfloat16).astype(jnp.float32), jnp.uint32)
    hi = pltpu.bitcast(a[:, w:].astype(jnp.bfloat16).astype(jnp.float32), jnp.uint32)
    return (lo >> 16) | (hi & jnp.uint32(0xFFFF0000))


def _unpack_bf16_pairs(wd):
    lo = pltpu.bitcast(wd << 16, jnp.float32)
    hi = pltpu.bitcast(wd & jnp.uint32(0xFFFF0000), jnp.float32)
    return jnp.concatenate([lo, hi], axis=-1)


def _conv_b_chunk(ub_buf, cbw_ref, n_i, r0, rc, c0):
    cs = slice(c0, c0 + LANES)
    lead = HIST_B_ROWS - (CONV_B_W - 1)
    acc = None
    for b in range(SUBLANES):
        rows = rc if b == 0 else rc + SUBLANES
        q = None
        for a in range((HIST_B_ROWS + SUBLANES) // SUBLANES):
            k = SUBLANES * a + b - lead
            if k < 0 or k >= CONV_B_W:
                continue
            start = r0 + SUBLANES * a
            term = cbw_ref[k:k + 1, cs] * ub_buf[n_i, start:start + rows, cs]
            q = term if q is None else q + term
        part = q if b == 0 else q[b:b + rc]
        acc = part if acc is None else acc + part
    return acc


def _mixer_kernel(x_ref, hista_ref, histb_ref, cnt0_ref, tri_ref,
                  gmix_ref, win_ref, caw_ref, cbw_ref, cbb_ref, lng_ref, lnb_ref,
                  ga_ref, gb_ref, wout_ref, gffn_ref, wrt_ref, br_ref,
                  h_ref, hnp_ref, idx_ref, gate_ref, rank_ref, cnt_ref, newa_ref, newb_ref,
                  ua_buf, ub_buf, cb_buf, cnt_acc, *, nb, ts, d_a, d_b, n_exp):
    b = pl.program_id(0)
    s = pl.program_id(1)
    m = nb * ts
    d = x_ref.shape[-1]

    @pl.when(s == 0)
    def _():
        ua_buf[:, 0:HIST_A_ROWS, :] = hista_ref[...]
        ub_buf[:, 0:HIST_B_ROWS, :] = histb_ref[...]

    @pl.when(s != 0)
    def _():
        ua_buf[:, 0:HIST_A_ROWS, :] = ua_buf[:, ts:ts + HIST_A_ROWS, :]
        ub_buf[:, 0:HIST_B_ROWS, :] = ub_buf[:, ts:ts + HIST_B_ROWS, :]

    @pl.when((b == 0) & (s == 0))
    def _():
        cnt_acc[...] = cnt0_ref[...]

    x = x_ref[...].reshape(m, d)
    n = _rms(x, gmix_ref[...]).astype(jnp.bfloat16)
    def proj(c0, width):
        return jnp.dot(n, win_ref[:, c0:c0 + width], preferred_element_type=jnp.float32)

    sec = min(d_b, PROJ_SECTION)
    for c0 in range(0, d_b, sec):
        v_b = proj(3 * d_a + c0, sec)
        g_b = proj(3 * d_a + d_b + c0, sec)
        ub_buf[:, HIST_B_ROWS:HIST_B_ROWS + ts, c0:c0 + sec] = (v_b * jax.nn.sigmoid(g_b)).reshape(nb, ts, sec)

    gate_c = proj(d_a, d_a)
    xt = proj(2 * d_a, d_a)
    gate_b = proj(0, d_a)
    ua_buf[:, HIST_A_ROWS:HIST_A_ROWS + ts, :] = (gate_c * xt).reshape(nb, ts, d_a)
    conv_a = jnp.zeros((nb, ts, d_a), jnp.float32)
    for k in range(CONV_A_W):
        off = HIST_A_ROWS - (CONV_A_W - 1) + k
        conv_a = conv_a + caw_ref[k:k + 1, :] * ua_buf[:, off:off + ts, :]
    y_a = gate_b * conv_a.reshape(m, d_a)

    mix_a = jnp.dot(_rms(y_a, ga_ref[...]).astype(jnp.bfloat16), wout_ref[0:d_a, :],
                    preferred_element_type=jnp.float32)

    rc = min(ts, CONV_ROW_CHUNK)
    chunks = [(n_i, r0, c0) for n_i in range(nb) for r0 in range(0, ts, rc) for c0 in range(0, d_b, LANES)]
    anchors = {len(chunks) * 2 // 8: gate_c, len(chunks) * 3 // 8: xt, len(chunks) * 5 // 8: gate_b,
               len(chunks) - 1: mix_a}
    for ci, (n_i, r0, c0) in enumerate(chunks):
        acc = _conv_b_chunk(ub_buf, cbw_ref, n_i, r0, rc, c0)
        if ci in anchors:
            acc = (acc.reshape(rc // SUBLANES, SUBLANES, LANES) + _zero_tile_from(anchors[ci])[None]
                   ).reshape(rc, LANES)
        cb_buf[n_i, r0:r0 + rc, c0:c0 + LANES] = acc
    cb = cb_buf[...].reshape(m, d_b) + cbb_ref[...]
    mu = jnp.mean(cb, axis=-1, keepdims=True)
    xc = cb - mu
    var = jnp.mean(xc * xc, axis=-1, keepdims=True)
    ln = xc * lax.rsqrt(var + NORM_EPS) * lng_ref[...] + lnb_ref[...]
    y_b = ln * jax.nn.sigmoid(ln)

    mix_b = jnp.dot(_rms(y_b, gb_ref[...]).astype(jnp.bfloat16), wout_ref[d_a:d_a + d_b, :],
                    preferred_element_type=jnp.float32)
    h = x + (mix_a + mix_b)
    h_ref[...] = h.reshape(nb, ts, d)

    hn = _rms(h, gffn_ref[...])
    hnp_ref[...] = _pack_bf16_pairs(hn)

    logits = lax.dot_general(wrt_ref[...], hn.astype(jnp.bfloat16), (((1,), (1,)), ((), ())),
                             preferred_element_type=jnp.float32) + br_ref[...]
    e_iota = lax.broadcasted_iota(jnp.int32, (n_exp, m), 0)
    cur = logits
    vals, idxs, sels = [], [], []
    for _ in range(TOP_K):
        mx = jnp.max(cur, axis=0, keepdims=True)
        ix = jnp.min(jnp.where(cur == mx, e_iota, n_exp), axis=0, keepdims=True)
        sel = e_iota == ix
        vals.append(mx)
        idxs.append(ix)
        sels.append(sel)
        cur = jnp.where(sel, -jnp.inf, cur)
    exps = [jnp.exp(v - vals[0]) for v in vals]
    denom = exps[0] + exps[1] + exps[2] + exps[3]
    gates = [e / denom for e in exps]

    onehot = jnp.zeros((n_exp, m), jnp.float32)
    for sel in sels:
        onehot = onehot + sel.astype(jnp.float32)
    before = jnp.dot(onehot.astype(jnp.bfloat16), tri_ref[...],
                     preferred_element_type=jnp.float32) + cnt_acc[...]
    ranks = [jnp.sum(jnp.where(sel, before, 0.0), axis=0, keepdims=True) for sel in sels]
    new_cnt = cnt_acc[...] + jnp.sum(onehot, axis=1, keepdims=True)
    cnt_acc[...] = new_cnt
    cnt_ref[...] = new_cnt

    idx_ref[...] = jnp.concatenate(idxs, axis=0)
    gate_ref[...] = jnp.concatenate(gates, axis=0)
    rank_ref[...] = jnp.concatenate(ranks, axis=0).astype(jnp.int32)

    @pl.when(s == pl.num_programs(1) - 1)
    def _():
        newa_ref[...] = ua_buf[:, ts + HIST_A_ROWS - (CONV_A_W - 1):ts + HIST_A_ROWS, :]
        newb_ref[...] = ub_buf[:, ts + HIST_B_ROWS - (CONV_B_W - 1):ts + HIST_B_ROWS, :]


def _mixer(x, hist_a, hist_b, cnt0, params, *, nb, ts, b0=0, bsz=None):
    (gmix, win, caw, cbw, cbb, lng, lnb, ga, gb, wout, gffn, wrt, br) = params
    _, seq, d = x.shape
    bsz = x.shape[0] if bsz is None else bsz
    d_a, d_b = caw.shape[-1], cbw.shape[-1]
    n_exp = wrt.shape[0]
    m = nb * ts
    n_tok = bsz * seq
    g0 = b0 // nb
    grid = (bsz // nb, seq // ts)
    tri = (jnp.arange(m)[:, None] < jnp.arange(m)[None, :]).astype(jnp.bfloat16)

    def const(shape):
        return pl.BlockSpec(shape, lambda i, j: (0,) * len(shape))

    tok_map = lambda i, j: (0, i * (seq // ts) + j)
    kern = functools.partial(_mixer_kernel, nb=nb, ts=ts, d_a=d_a, d_b=d_b, n_exp=n_exp)
    out_shape = (
        jax.ShapeDtypeStruct((bsz, seq, d), jnp.float32),
        jax.ShapeDtypeStruct((n_tok, d // 2), jnp.uint32),
        jax.ShapeDtypeStruct((TOP_K, n_tok), jnp.int32),
        jax.ShapeDtypeStruct((TOP_K, n_tok), jnp.float32),
        jax.ShapeDtypeStruct((TOP_K, n_tok), jnp.int32),
        jax.ShapeDtypeStruct((n_exp, 1), jnp.float32),
        jax.ShapeDtypeStruct((bsz, CONV_A_W - 1, d_a), jnp.float32),
        jax.ShapeDtypeStruct((bsz, CONV_B_W - 1, d_b), jnp.float32),
    )
    return pl.pallas_call(
        kern,
        grid=grid,
        in_specs=[
            pl.BlockSpec((nb, ts, d), lambda i, j: (g0 + i, j, 0)),
            pl.BlockSpec((nb, HIST_A_ROWS, d_a), lambda i, j: (g0 + i, 0, 0)),
            pl.BlockSpec((nb, HIST_B_ROWS, d_b), lambda i, j: (g0 + i, 0, 0)),
            const((n_exp, 1)),
            const((m, m)),
            const((1, d)), const(win.shape), const(caw.shape), const(cbw.shape), const((1, d_b)),
            const((1, d_b)), const((1, d_b)), const((1, d_a)), const((1, d_b)), const(wout.shape),
            const((1, d)), const(wrt.shape), const((n_exp, 1)),
        ],
        out_specs=(
            pl.BlockSpec((nb, ts, d), lambda i, j: (i, j, 0)),
            pl.BlockSpec((m, d // 2), lambda i, j: (i * (seq // ts) + j, 0)),
            pl.BlockSpec((TOP_K, m), tok_map),
            pl.BlockSpec((TOP_K, m), tok_map),
            pl.BlockSpec((TOP_K, m), tok_map),
            pl.BlockSpec((n_exp, 1), lambda i, j: (0, 0)),
            pl.BlockSpec((nb, CONV_A_W - 1, d_a), lambda i, j: (i, 0, 0)),
            pl.BlockSpec((nb, CONV_B_W - 1, d_b), lambda i, j: (i, 0, 0)),
        ),
        out_shape=out_shape,
        scratch_shapes=[
            pltpu.VMEM((nb, HIST_A_ROWS + ts, d_a), jnp.float32),
            pltpu.VMEM((nb, HIST_B_ROWS + ts, d_b), jnp.float32),
            pltpu.VMEM((nb, ts, d_b), jnp.float32),
            pltpu.VMEM((n_exp, 1), jnp.float32),
        ],
        compiler_params=pltpu.CompilerParams(
            dimension_semantics=("arbitrary", "arbitrary"), vmem_limit_bytes=VMEM_LIMIT),
        name="mixer",
    )(x, hist_a, hist_b, cnt0, tri, gmix, win, caw, cbw, cbb, lng, lnb, ga, gb, wout, gffn, wrt, br)


def _positions_kernel(start_ref, *refs):
    n_parts = len(refs) // 3
    for p in range(n_parts):
        idx = refs[2 * p][...]
        pos = refs[2 * p + 1][...]
        for e in range(start_ref.shape[0]):
            pos = pos + jnp.where(idx == e, start_ref[e], 0)
        refs[2 * n_parts + p][...] = pos


def _positions(pad_start, parts):
    n_0 = parts[0][0].shape[1]
    tile = min(n_0, POS_TILE)
    specs = [pl.BlockSpec((TOP_K, tile), lambda i: (0, i))]
    specs += [pl.BlockSpec((TOP_K, idx.shape[1]), lambda i: (0, 0)) for idx, _ in parts[1:]]
    in_specs = [pl.BlockSpec(memory_space=pltpu.SMEM)]
    args = [pad_start]
    for spec, (idx, rank) in zip(specs, parts):
        in_specs += [spec, spec]
        args += [idx, rank]
    out = pl.pallas_call(
        _positions_kernel,
        grid=(n_0 // tile,),
        in_specs=in_specs,
        out_specs=tuple(specs),
        out_shape=tuple(jax.ShapeDtypeStruct(idx.shape, jnp.int32) for idx, _ in parts),
        compiler_params=pltpu.CompilerParams(dimension_semantics=("arbitrary",)),
        name="positions",
    )(*args)
    return list(out)


def _sc_mesh():
    return plsc.VectorSubcoreMesh(core_axis_name="c", subcore_axis_name="s",
                                  num_cores=SC_CORES, num_subcores=SC_SUBCORES)


def _sc_worker_id():
    return lax.axis_index("s") * SC_CORES + lax.axis_index("c")


def _sc_chunks(n_tok, wid):
    n_chunks = n_tok // SC_CHUNK
    assert n_tok % SC_CHUNK == 0
    if n_chunks % SC_WORKERS == 0:
        per_w = n_chunks // SC_WORKERS
        return None, per_w, lambda i: (wid * per_w + i) * SC_CHUNK
    assert n_chunks <= SC_WORKERS
    return wid < n_chunks, 1, lambda i: wid * SC_CHUNK


def _sc_for_each_chunk(n_tok, fn):
    wid = _sc_worker_id()
    pred, trips, base = _sc_chunks(n_tok, wid)

    def run():
        @pl.loop(0, trips)
        def _(i):
            fn(pl.multiple_of(base(i), SC_CHUNK))

    if pred is None:
        run()
    else:
        pl.when(pred)(run)


def _sc_dispatch(parts, n_rows):
    half = parts[0][1].shape[1]
    n_parts = len(parts)

    @functools.partial(
        pl.kernel, mesh=_sc_mesh(),
        out_type=jax.ShapeDtypeStruct((n_rows, half), jnp.uint32),
        scratch_types=[pltpu.VMEM((TOP_K, SC_CHUNK), jnp.int32),
                       pltpu.VMEM((SC_CHUNK, half), jnp.uint32),
                       pltpu.SemaphoreType.DMA],
        compiler_params=pltpu.CompilerParams(use_tc_tiling_on_sc=True),
        name="sc_dispatch")
    def k(*refs):
        xs_hbm, idx_v, rows_v, sem = refs[2 * n_parts:]

        def move(pos_hbm, src_hbm):
            def fn(base):
                pltpu.sync_copy(pos_hbm.at[:, pl.ds(base, SC_CHUNK)], idx_v)
                pltpu.sync_copy(src_hbm.at[pl.ds(base, SC_CHUNK)], rows_v)
                copies = [pltpu.async_copy(rows_v, xs_hbm.at[idx_v.at[j]], sem) for j in range(TOP_K)]
                for c in copies:
                    c.wait()
            return fn

        for p, (_, hnp) in enumerate(parts):
            _sc_for_each_chunk(hnp.shape[0], move(refs[2 * p], refs[2 * p + 1]))

    return k(*[a for part in parts for a in part])


def _sc_gather(pos_parts, rows):
    half = rows.shape[1]
    n_parts = len(pos_parts)

    @functools.partial(
        pl.kernel, mesh=_sc_mesh(),
        out_type=tuple(jax.ShapeDtypeStruct((TOP_K, p.shape[1], half), jnp.uint32) for p in pos_parts),
        scratch_types=[pltpu.VMEM((TOP_K, SC_CHUNK), jnp.int32),
                       pltpu.VMEM((SC_CHUNK, half), jnp.uint32),
                       pltpu.SemaphoreType.DMA],
        compiler_params=pltpu.CompilerParams(use_tc_tiling_on_sc=True),
        name="sc_gather")
    def k(*refs):
        pos_hbms, rows_hbm = refs[:n_parts], refs[n_parts]
        out_hbms = refs[n_parts + 1:2 * n_parts + 1]
        idx_v, buf_v, sem = refs[2 * n_parts + 1:]

        def move(pos_hbm, out_hbm):
            def fn(base):
                pltpu.sync_copy(pos_hbm.at[:, pl.ds(base, SC_CHUNK)], idx_v)
                for j in range(TOP_K):
                    pltpu.async_copy(rows_hbm.at[idx_v.at[j]], buf_v, sem).wait()
                    pltpu.sync_copy(buf_v, out_hbm.at[j, pl.ds(base, SC_CHUNK)])
            return fn

        for pos_hbm, out_hbm, p in zip(pos_hbms, out_hbms, pos_parts):
            _sc_for_each_chunk(p.shape[1], move(pos_hbm, out_hbm))

    out = k(*pos_parts, rows)
    return out if isinstance(out, (tuple, list)) else (out,)


def _experts_kernel(blk_exp_ref, first_ref, slot_ref, next_ref, n_active_ref,
                    xs_ref, wgu_hbm, bgu_ref, wd_hbm, bd_ref, out_ref,
                    wgu_f32, wd_f32, wgu_bf, wd_bf, sems):
    d_ff = wd_bf.shape[0]

    def fetch(e, sl):
        return (pltpu.make_async_copy(wgu_hbm.at[e], wgu_f32.at[sl], sems.at[0, sl]),
                pltpu.make_async_copy(wd_hbm.at[e], wd_f32.at[sl], sems.at[1, sl]))

    def load_weights_if_first(i):
        sl = slot_ref[i]
        e = blk_exp_ref[i]

        @pl.when(first_ref[i] == 1)
        def _():
            @pl.when(i == 0)
            def _():
                for c in fetch(e, sl):
                    c.start()

            for c in fetch(e, sl):
                c.wait()
            wgu_bf[...] = wgu_f32[sl].astype(jnp.bfloat16)
            wd_bf[...] = wd_f32[sl].astype(jnp.bfloat16)

            @pl.when(next_ref[i] >= 0)
            def _():
                for c in fetch(next_ref[i], 1 - sl):
                    c.start()

    def compute(rows, e):
        x = _unpack_bf16_pairs(xs_ref[rows, :]).astype(jnp.bfloat16)
        gu = jnp.dot(x, wgu_bf[...], preferred_element_type=jnp.float32) + bgu_ref[e]
        g = jnp.minimum(gu[:, :d_ff], SWIGLU_LIMIT)
        u = jnp.clip(gu[:, d_ff:], -SWIGLU_LIMIT, SWIGLU_LIMIT)
        act = g * jax.nn.sigmoid(SWIGLU_ALPHA * g) * (u + 1.0)
        o = jnp.dot(act.astype(jnp.bfloat16), wd_bf[...], preferred_element_type=jnp.float32) + bd_ref[e]
        out_ref[rows, :] = _pack_bf16_pairs(o)

    i0 = pl.program_id(0) * BLOCKS_PER_STEP
    n_active = n_active_ref[0]

    def run(sub0, n_sub):
        first = i0 + sub0
        rows = slice(sub0 * MOE_BLOCK, (sub0 + n_sub) * MOE_BLOCK)
        if n_sub == 1:
            @pl.when(first < n_active)
            def _():
                load_weights_if_first(first)
                compute(rows, blk_exp_ref[first])

            @pl.when(first >= n_active)
            def _():
                out_ref[rows, :] = jnp.zeros((MOE_BLOCK, out_ref.shape[1]), out_ref.dtype)
            return

        same = first + n_sub - 1 < n_active
        for sub in range(1, n_sub):
            same = same & (blk_exp_ref[first + sub] == blk_exp_ref[first])

        @pl.when(same)
        def _():
            load_weights_if_first(first)
            compute(rows, blk_exp_ref[first])

        @pl.when(jnp.logical_not(same))
        def _():
            run(sub0, n_sub // 2)
            run(sub0 + n_sub // 2, n_sub // 2)

    run(0, BLOCKS_PER_STEP)


def _experts(plan, xs, wgu, bgu, wd, bd):
    blk_exp, first, slot, nxt, n_active = plan
    n_rows, half = xs.shape
    n_exp, d, two_f = wgu.shape
    d_ff = wd.shape[1]
    n_blocks = n_rows // MOE_BLOCK
    step_rows = MOE_BLOCK * BLOCKS_PER_STEP
    assert n_rows % step_rows == 0
    grid_spec = pltpu.PrefetchScalarGridSpec(
        num_scalar_prefetch=5,
        grid=(n_blocks // BLOCKS_PER_STEP,),
        in_specs=[
            pl.BlockSpec((step_rows, half), lambda i, *_: (i, 0)),
            pl.BlockSpec(memory_space=pl.ANY),
            pl.BlockSpec((n_exp, 1, two_f), lambda i, *_: (0, 0, 0)),
            pl.BlockSpec(memory_space=pl.ANY),
            pl.BlockSpec((n_exp, 1, d), lambda i, *_: (0, 0, 0)),
        ],
        out_specs=pl.BlockSpec((step_rows, half), lambda i, *_: (i, 0)),
        scratch_shapes=[
            pltpu.VMEM((2, d, two_f), jnp.float32),
            pltpu.VMEM((2, d_ff, d), jnp.float32),
            pltpu.VMEM((d, two_f), jnp.bfloat16),
            pltpu.VMEM((d_ff, d), jnp.bfloat16),
            pltpu.SemaphoreType.DMA((2, 2)),
        ],
    )
    return pl.pallas_call(
        _experts_kernel,
        grid_spec=grid_spec,
        out_shape=jax.ShapeDtypeStruct((n_rows, half), jnp.uint32),
        compiler_params=pltpu.CompilerParams(
            dimension_semantics=("arbitrary",), vmem_limit_bytes=VMEM_LIMIT),
        name="experts",
    )(blk_exp, first, slot, nxt, n_active, xs, wgu, bgu.reshape(n_exp, 1, two_f), wd,
      bd.reshape(n_exp, 1, d))


def _combine_kernel(rows_ref, gate_ref, h_ref, gfin_ref, *rest):
    y_ref = rest[-1]
    acc = h_ref[...]
    for k in range(TOP_K):
        acc = acc + gate_ref[:, k:k + 1] * _unpack_bf16_pairs(rows_ref[k])
    y_ref[...] = _rms(acc, gfin_ref[...])


def _combine(rows4, gate_t, h2d, gfin, *, tok0=0, y_prev=None, y_rows=None, y_tok0=0):
    d = h2d.shape[1]
    n, half = rows4.shape[1:]
    n_tok = h2d.shape[0] if y_rows is None else y_rows
    tile = min(n, 256)
    t0 = tok0 // tile
    y0 = y_tok0 // tile
    in_specs = [
        pl.BlockSpec((TOP_K, tile, half), lambda i: (0, i, 0)),
        pl.BlockSpec((tile, TOP_K), lambda i: (t0 + i, 0)),
        pl.BlockSpec((tile, d), lambda i: (t0 + i, 0)),
        pl.BlockSpec((1, d), lambda i: (0, 0)),
    ]
    args = [rows4, gate_t, h2d, gfin]
    aliases = {}
    if y_prev is not None:
        in_specs.append(pl.BlockSpec(memory_space=pl.ANY))
        args.append(y_prev)
        aliases = {4: 0}
    return pl.pallas_call(
        _combine_kernel,
        grid=(n // tile,),
        in_specs=in_specs,
        out_specs=pl.BlockSpec((tile, d), lambda i: (y0 + i, 0)),
        out_shape=jax.ShapeDtypeStruct((n_tok, d), jnp.float32),
        input_output_aliases=aliases,
        compiler_params=pltpu.CompilerParams(dimension_semantics=("arbitrary",)),
        name="combine",
    )(*args)


def _pad_hist(hist, rows):
    return jnp.pad(hist, ((0, 0), (rows - hist.shape[1], 0), (0, 0)))


def kernel(x_prompt, x_sample, state_conv_a, state_conv_b, norm_mix_g, w_in, conv_a_w, conv_b_w, conv_b_b, conv_ln_g, conv_ln_b, out_norm_a_g, out_norm_b_g, w_out, norm_ffn_g, w_router, b_router, w_gate_up, b_gate_up, w_down, b_down, final_norm_g):
    depth = w_in.shape[0]
    assert depth == 1, "single-layer trunk"
    bf16 = jnp.bfloat16
    bsz, seq, d = x_prompt.shape
    dec_b, dec_s, _ = x_sample.shape
    d_a, d_b = conv_a_w.shape[-1], conv_b_w.shape[-1]
    n_exp = w_router.shape[-1]

    params = (norm_mix_g[0][None], w_in[0].astype(bf16), conv_a_w[0], conv_b_w[0], conv_b_b[0][None],
              conv_ln_g[0][None], conv_ln_b[0][None], out_norm_a_g[0][None], out_norm_b_g[0][None],
              w_out[0].astype(bf16), norm_ffn_g[0][None], w_router[0].T.astype(bf16),
              b_router[0][:, None])

    zero_a = jnp.zeros((bsz, HIST_A_ROWS, d_a), jnp.float32)
    zero_b = jnp.zeros((bsz, HIST_B_ROWS, d_b), jnp.float32)
    cnt0 = jnp.zeros((n_exp, 1), jnp.float32)
    gfin = final_norm_g[None]
    ts = min(seq, MIXER_ROWS)
    n_p = bsz * seq

    sizes = [bsz * f // sum(GROUP_SPLIT) for f in GROUP_SPLIT]
    if bsz % sum(GROUP_SPLIT) != 0:
        sizes = [bsz]
    groups = []
    tok_start = []
    for g, per_b in enumerate(sizes):
        b0 = sum(sizes[:g])
        tok_start.append(b0 * seq)
        h, hnp, idx, gate, rank, cnt, na, nb_ = _mixer(
            x_prompt, zero_a, zero_b, cnt0, params, nb=1, ts=ts, b0=b0, bsz=per_b)
        parts = [dict(h=h.reshape(per_b * seq, d), hnp=hnp, idx=idx, gate_t=gate.T, rank=rank)]
        states = [(na, nb_)]
        if g == len(sizes) - 1:
            h, hnp, idx, gate, rank, cnt, na_s, nb_s = _mixer(
                x_sample, _pad_hist(state_conv_a[0], HIST_A_ROWS), _pad_hist(state_conv_b[0], HIST_B_ROWS),
                cnt, params, nb=dec_b, ts=dec_s)
            parts.append(dict(h=h.reshape(dec_b * dec_s, d), hnp=hnp, idx=idx, gate_t=gate.T, rank=rank))
        n_pairs = sum(p["hnp"].shape[0] for p in parts) * TOP_K
        plan, pad_start, n_blocks = _block_plan(cnt[:, 0].astype(jnp.int32), n_pairs, n_exp)
        pos = _positions(pad_start, [(p["idx"], p["rank"]) for p in parts])
        xs = _sc_dispatch([(q, p["hnp"]) for q, p in zip(pos, parts)], n_blocks * MOE_BLOCK)
        groups.append(dict(parts=parts, pos=pos, plan=plan, xs=xs, states=states))

    for grp in groups:
        grp["rows"] = _experts(grp["plan"], grp["xs"], w_gate_up[0], b_gate_up[0], w_down[0], b_down[0])

    gathered = []
    for g, grp in enumerate(groups):
        pos0 = grp["pos"][0]
        n_g = pos0.shape[1]
        want = 1 if (g == len(groups) - 1 and len(groups) > 1) else COMBINE_CHUNKS
        n_chunks = want if n_g % (want * SC_CHUNK * SC_WORKERS) == 0 else 1
        per = n_g // n_chunks
        for c in range(n_chunks):
            pos_parts = [pos0[:, c * per:(c + 1) * per]] + (grp["pos"][1:] if c == 0 else [])
            gathered.append((g, c * per, _sc_gather(pos_parts, grp["rows"])))

    y_p = None
    for g, tok0, got in sorted(gathered, key=lambda t: (t[0] != len(groups) - 1, t[0], t[1])):
        grp = groups[g]
        main = grp["parts"][0]
        if len(got) > 1:
            smp = grp["parts"][1]
            y_s = _combine(got[1], smp["gate_t"], smp["h"], gfin).reshape(dec_b, dec_s, d)
        y_p = _combine(got[0], main["gate_t"], main["h"], gfin, tok0=tok0,
                       y_prev=y_p, y_rows=n_p, y_tok0=tok_start[g] + tok0)
    y_p = y_p.reshape(bsz, seq, d)

    na_p = jnp.concatenate([grp["states"][0][0] for grp in groups], axis=0)
    nb_p = jnp.concatenate([grp["states"][0][1] for grp in groups], axis=0)
    return (y_p, y_s, na_p[None], nb_p[None], na_s[None], nb_s[None])


def _block_plan(counts, n_pairs, n_exp):
    padded = (counts + MOE_BLOCK - 1) // MOE_BLOCK * MOE_BLOCK
    pad_end = jnp.cumsum(padded)
    pad_start = pad_end - padded
    n_blocks = -(-n_pairs // MOE_BLOCK) + n_exp
    n_blocks = -(-n_blocks // BLOCKS_PER_STEP) * BLOCKS_PER_STEP
    blk_start = jnp.arange(n_blocks, dtype=jnp.int32) * MOE_BLOCK
    blk_exp = jnp.minimum(jnp.sum(blk_start[:, None] >= pad_end[None, :], axis=1),
                          n_exp - 1).astype(jnp.int32)
    n_active = (pad_end[-1:] // MOE_BLOCK).astype(jnp.int32)
    e_ids = jnp.arange(n_exp, dtype=jnp.int32)
    has = padded > 0
    slot_e = (jnp.cumsum(has.astype(jnp.int32)) - 1) % 2
    later = jnp.where(has, e_ids, n_exp)
    next_e = jnp.concatenate([lax.cummin(later, reverse=True)[1:], jnp.full((1,), n_exp, jnp.int32)])
    next_e = jnp.where(next_e >= n_exp, -1, next_e)
    blk_onehot = blk_exp[:, None] == e_ids[None, :]

    def per_block(table):
        return jnp.sum(jnp.where(blk_onehot, table[None, :], 0), axis=1).astype(jnp.int32)

    blk_first = ((blk_start == per_block(pad_start)) & (blk_start < pad_end[-1])).astype(jnp.int32)
    plan = (blk_exp, blk_first, per_block(slot_e), per_block(next_e), n_active)
    return plan, pad_start.astype(jnp.int32), n_blocks
```

```python
import functools

import jax
import jax.numpy as jnp
from jax import lax
from jax.experimental import pallas as pl
from jax.experimental.pallas import tpu as pltpu
from jax.experimental.pallas import tpu_sc as plsc

SC_CORES = 2
SC_SUBCORES = 16
SC_WORKERS = SC_CORES * SC_SUBCORES
SC_CHUNK = 128
CONV_A_W = 3
CONV_B_W = 31
TOP_K = 4
SWIGLU_LIMIT = 7.0
SWIGLU_ALPHA = 1.702
NORM_EPS = 1e-5
MOE_BLOCK = 256
BLOCKS_PER_STEP = 4
SUBLANES = 8
LANES = 128
CONV_ROW_CHUNK = 128
HIST_A_ROWS = 8
HIST_B_ROWS = 32
MIXER_ROWS = 512
PROJ_SECTION = 256
POS_TILE = 4096
GROUP_SPLIT = (1, 1)
COMBINE_CHUNKS = 2
VMEM_LIMIT = 56 * 1024 * 1024


def _rms(x, g):
    return x * lax.rsqrt(jnp.mean(x * x, axis=-1, keepdims=True) + NORM_EPS) * g


def _zero_tile_from(v):
    bits = pltpu.bitcast(v[0:SUBLANES, 0:LANES], jnp.uint32)
    return pltpu.bitcast((bits >> 16) >> 16, jnp.float32)


def _pack_bf16_pairs(a):
    w = a.shape[-1] // 2
    lo = pltpu.bitcast(a[:, :w].astype(jnp.bfloat16).astype(jnp.float32), jnp.uint32)
    hi = pltpu.bitcast(a[:, w:].astype(jnp.bfloat16).astype(jnp.float32), jnp.uint32)
    return (lo >> 16) | (hi & jnp.uint32(0xFFFF0000))


def _unpack_bf16_pairs(wd):
    lo = pltpu.bitcast(wd << 16, jnp.float32)
    hi = pltpu.bitcast(wd & jnp.uint32(0xFFFF0000), jnp.float32)
    return jnp.concatenate([lo, hi], axis=-1)


def _conv_b_chunk(ub_buf, cbw_ref, n_i, r0, rc, c0):
    cs = slice(c0, c0 + LANES)
    lead = HIST_B_ROWS - (CONV_B_W - 1)
    acc = None
    for b in range(SUBLANES):
        rows = rc if b == 0 else rc + SUBLANES
        q = None
        for a in range((HIST_B_ROWS + SUBLANES) // SUBLANES):
            k = SUBLANES * a + b - lead
            if k < 0 or k >= CONV_B_W:
                continue
            start = r0 + SUBLANES * a
            term = cbw_ref[k:k + 1, cs] * ub_buf[n_i, start:start + rows, cs]
            q = term if q is None else q + term
        part = q if b == 0 else q[b:b + rc]
        acc = part if acc is None else acc + part
    return acc


def _mixer_kernel(x_ref, hista_ref, histb_ref, cnt0_ref, tri_ref,
                  gmix_ref, win_ref, caw_ref, cbw_ref, cbb_ref, lng_ref, lnb_ref,
                  ga_ref, gb_ref, wout_ref, gffn_ref, wrt_ref, br_ref,
                  h_ref, hnp_ref, idx_ref, gate_ref, rank_ref, cnt_ref, newa_ref, newb_ref,
                  ua_buf, ub_buf, cb_buf, cnt_acc, *, nb, ts, d_a, d_b, n_exp):
    b = pl.program_id(0)
    s = pl.program_id(1)
    m = nb * ts
    d = x_ref.shape[-1]

    @pl.when(s == 0)
    def _():
        ua_buf[:, 0:HIST_A_ROWS, :] = hista_ref[...]
        ub_buf[:, 0:HIST_B_ROWS, :] = histb_ref[...]

    @pl.when(s != 0)
    def _():
        ua_buf[:, 0:HIST_A_ROWS, :] = ua_buf[:, ts:ts + HIST_A_ROWS, :]
        ub_buf[:, 0:HIST_B_ROWS, :] = ub_buf[:, ts:ts + HIST_B_ROWS, :]

    @pl.when((b == 0) & (s == 0))
    def _():
        cnt_acc[...] = cnt0_ref[...]

    x = x_ref[...].reshape(m, d)
    n = _rms(x, gmix_ref[...]).astype(jnp.bfloat16)
    def proj(c0, width):
        return jnp.dot(n, win_ref[:, c0:c0 + width], preferred_element_type=jnp.float32)

    sec = min(d_b, PROJ_SECTION)
    for c0 in range(0, d_b, sec):
        v_b = proj(3 * d_a + c0, sec)
        g_b = proj(3 * d_a + d_b + c0, sec)
        ub_buf[:, HIST_B_ROWS:HIST_B_ROWS + ts, c0:c0 + sec] = (v_b * jax.nn.sigmoid(g_b)).reshape(nb, ts, sec)

    gate_c = proj(d_a, d_a)
    xt = proj(2 * d_a, d_a)
    gate_b = proj(0, d_a)
    ua_buf[:, HIST_A_ROWS:HIST_A_ROWS + ts, :] = (gate_c * xt).reshape(nb, ts, d_a)
    conv_a = jnp.zeros((nb, ts, d_a), jnp.float32)
    for k in range(CONV_A_W):
        off = HIST_A_ROWS - (CONV_A_W - 1) + k
        conv_a = conv_a + caw_ref[k:k + 1, :] * ua_buf[:, off:off + ts, :]
    y_a = gate_b * conv_a.reshape(m, d_a)

    mix_a = jnp.dot(_rms(y_a, ga_ref[...]).astype(jnp.bfloat16), wout_ref[0:d_a, :],
                    preferred_element_type=jnp.float32)

    rc = min(ts, CONV_ROW_CHUNK)
    chunks = [(n_i, r0, c0) for n_i in range(nb) for r0 in range(0, ts, rc) for c0 in range(0, d_b, LANES)]
    anchors = {len(chunks) * 2 // 8: gate_c, len(chunks) * 3 // 8: xt, len(chunks) * 5 // 8: gate_b,
               len(chunks) - 1: mix_a}
    for ci, (n_i, r0, c0) in enumerate(chunks):
        acc = _conv_b_chunk(ub_buf, cbw_ref, n_i, r0, rc, c0)
        if ci in anchors:
            acc = (acc.reshape(rc // SUBLANES, SUBLANES, LANES) + _zero_tile_from(anchors[ci])[None]
                   ).reshape(rc, LANES)
        cb_buf[n_i, r0:r0 + rc, c0:c0 + LANES] = acc
    cb = cb_buf[...].reshape(m, d_b) + cbb_ref[...]
    mu = jnp.mean(cb, axis=-1, keepdims=True)
    xc = cb - mu
    var = jnp.mean(xc * xc, axis=-1, keepdims=True)
    ln = xc * lax.rsqrt(var + NORM_EPS) * lng_ref[...] + lnb_ref[...]
    y_b = ln * jax.nn.sigmoid(ln)

    mix_b = jnp.dot(_rms(y_b, gb_ref[...]).astype(jnp.bfloat16), wout_ref[d_a:d_a + d_b, :],
                    preferred_element_type=jnp.float32)
    h = x + (mix_a + mix_b)
    h_ref[...] = h.reshape(nb, ts, d)

    hn = _rms(h, gffn_ref[...])
    hnp_ref[...] = _pack_bf16_pairs(hn)

    logits = lax.dot_general(wrt_ref[...], hn.astype(jnp.bfloat16), (((1,), (1,)), ((), ())),
                             preferred_element_type=jnp.float32) + br_ref[...]
    e_iota = lax.broadcasted_iota(jnp.int32, (n_exp, m), 0)
    cur = logits
    vals, idxs, sels = [], [], []
    for _ in range(TOP_K):
        mx = jnp.max(cur, axis=0, keepdims=True)
        ix = jnp.min(jnp.where(cur == mx, e_iota, n_exp), axis=0, keepdims=True)
        sel = e_iota == ix
        vals.append(mx)
        idxs.append(ix)
        sels.append(sel)
        cur = jnp.where(sel, -jnp.inf, cur)
    exps = [jnp.exp(v - vals[0]) for v in vals]
    denom = exps[0] + exps[1] + exps[2] + exps[3]
    gates = [e / denom for e in exps]

    onehot = jnp.zeros((n_exp, m), jnp.float32)
    for sel in sels:
        onehot = onehot + sel.astype(jnp.float32)
    before = jnp.dot(onehot.astype(jnp.bfloat16), tri_ref[...],
                     preferred_element_type=jnp.float32) + cnt_acc[...]
    ranks = [jnp.sum(jnp.where(sel, before, 0.0), axis=0, keepdims=True) for sel in sels]
    new_cnt = cnt_acc[...] + jnp.sum(onehot, axis=1, keepdims=True)
    cnt_acc[...] = new_cnt
    cnt_ref[...] = new_cnt

    idx_ref[...] = jnp.concatenate(idxs, axis=0)
    gate_ref[...] = jnp.concatenate(gates + [jnp.zeros((SUBLANES - TOP_K, m), jnp.float32)], axis=0).T
    rank_ref[...] = jnp.concatenate(ranks, axis=0).astype(jnp.int32)

    @pl.when(s == pl.num_programs(1) - 1)
    def _():
        newa_ref[...] = ua_buf[:, ts + HIST_A_ROWS - (CONV_A_W - 1):ts + HIST_A_ROWS, :]
        newb_ref[...] = ub_buf[:, ts + HIST_B_ROWS - (CONV_B_W - 1):ts + HIST_B_ROWS, :]


def _mixer(x, hist_a, hist_b, cnt0, params, *, nb, ts, b0=0, bsz=None):
    (gmix, win, caw, cbw, cbb, lng, lnb, ga, gb, wout, gffn, wrt, br) = params
    _, seq, d = x.shape
    bsz = x.shape[0] if bsz is None else bsz
    d_a, d_b = caw.shape[-1], cbw.shape[-1]
    n_exp = wrt.shape[0]
    m = nb * ts
    n_tok = bsz * seq
    g0 = b0 // nb
    grid = (bsz // nb, seq // ts)
    tri = (jnp.arange(m)[:, None] < jnp.arange(m)[None, :]).astype(jnp.bfloat16)

    def const(shape):
        return pl.BlockSpec(shape, lambda i, j: (0,) * len(shape))

    tok_map = lambda i, j: (0, i * (seq // ts) + j)
    kern = functools.partial(_mixer_kernel, nb=nb, ts=ts, d_a=d_a, d_b=d_b, n_exp=n_exp)
    out_shape = (
        jax.ShapeDtypeStruct((bsz, seq, d), jnp.float32),
        jax.ShapeDtypeStruct((n_tok, d // 2), jnp.uint32),
        jax.ShapeDtypeStruct((TOP_K, n_tok), jnp.int32),
        jax.ShapeDtypeStruct((n_tok, SUBLANES), jnp.float32),
        jax.ShapeDtypeStruct((TOP_K, n_tok), jnp.int32),
        jax.ShapeDtypeStruct((n_exp, 1), jnp.float32),
        jax.ShapeDtypeStruct((bsz, CONV_A_W - 1, d_a), jnp.float32),
        jax.ShapeDtypeStruct((bsz, CONV_B_W - 1, d_b), jnp.float32),
    )
    return pl.pallas_call(
        kern,
        grid=grid,
        in_specs=[
            pl.BlockSpec((nb, ts, d), lambda i, j: (g0 + i, j, 0)),
            pl.BlockSpec((nb, HIST_A_ROWS, d_a), lambda i, j: (g0 + i, 0, 0)),
            pl.BlockSpec((nb, HIST_B_ROWS, d_b), lambda i, j: (g0 + i, 0, 0)),
            const((n_exp, 1)),
            const((m, m)),
            const((1, d)), const(win.shape), const(caw.shape), const(cbw.shape), const((1, d_b)),
            const((1, d_b)), const((1, d_b)), const((1, d_a)), const((1, d_b)), const(wout.shape),
            const((1, d)), const(wrt.shape), const((n_exp, 1)),
        ],
        out_specs=(
            pl.BlockSpec((nb, ts, d), lambda i, j: (i, j, 0)),
            pl.BlockSpec((m, d // 2), lambda i, j: (i * (seq // ts) + j, 0)),
            pl.BlockSpec((TOP_K, m), tok_map),
            pl.BlockSpec((m, SUBLANES), lambda i, j: (i * (seq // ts) + j, 0)),
            pl.BlockSpec((TOP_K, m), tok_map),
            pl.BlockSpec((n_exp, 1), lambda i, j: (0, 0)),
            pl.BlockSpec((nb, CONV_A_W - 1, d_a), lambda i, j: (i, 0, 0)),
            pl.BlockSpec((nb, CONV_B_W - 1, d_b), lambda i, j: (i, 0, 0)),
        ),
        out_shape=out_shape,
        scratch_shapes=[
            pltpu.VMEM((nb, HIST_A_ROWS + ts, d_a), jnp.float32),
            pltpu.VMEM((nb, HIST_B_ROWS + ts, d_b), jnp.float32),
            pltpu.VMEM((nb, ts, d_b), jnp.float32),
            pltpu.VMEM((n_exp, 1), jnp.float32),
        ],
        compiler_params=pltpu.CompilerParams(
            dimension_semantics=("arbitrary", "arbitrary"), vmem_limit_bytes=VMEM_LIMIT),
        name="mixer",
    )(x, hist_a, hist_b, cnt0, tri, gmix, win, caw, cbw, cbb, lng, lnb, ga, gb, wout, gffn, wrt, br)


def _positions_kernel(start_ref, *refs):
    n_parts = len(refs) // 3
    for p in range(n_parts):
        idx = refs[2 * p][...]
        pos = refs[2 * p + 1][...]
        for e in range(start_ref.shape[0]):
            pos = pos + jnp.where(idx == e, start_ref[e], 0)
        refs[2 * n_parts + p][...] = pos


def _positions(pad_start, parts):
    n_0 = parts[0][0].shape[1]
    tile = min(n_0, POS_TILE)
    specs = [pl.BlockSpec((TOP_K, tile), lambda i: (0, i))]
    specs += [pl.BlockSpec((TOP_K, idx.shape[1]), lambda i: (0, 0)) for idx, _ in parts[1:]]
    in_specs = [pl.BlockSpec(memory_space=pltpu.SMEM)]
    args = [pad_start]
    for spec, (idx, rank) in zip(specs, parts):
        in_specs += [spec, spec]
        args += [idx, rank]
    out = pl.pallas_call(
        _positions_kernel,
        grid=(n_0 // tile,),
        in_specs=in_specs,
        out_specs=tuple(specs),
        out_shape=tuple(jax.ShapeDtypeStruct(idx.shape, jnp.int32) for idx, _ in parts),
        compiler_params=pltpu.CompilerParams(dimension_semantics=("arbitrary",)),
        name="positions",
    )(*args)
    return list(out)


def _sc_mesh():
    return plsc.VectorSubcoreMesh(core_axis_name="c", subcore_axis_name="s",
                                  num_cores=SC_CORES, num_subcores=SC_SUBCORES)


def _sc_worker_id():
    return lax.axis_index("s") * SC_CORES + lax.axis_index("c")


def _sc_chunks(n_tok, wid):
    n_chunks = n_tok // SC_CHUNK
    assert n_tok % SC_CHUNK == 0
    if n_chunks % SC_WORKERS == 0:
        per_w = n_chunks // SC_WORKERS
        return None, per_w, lambda i: (wid * per_w + i) * SC_CHUNK
    assert n_chunks <= SC_WORKERS
    return wid < n_chunks, 1, lambda i: wid * SC_CHUNK


def _sc_for_each_chunk(n_tok, fn):
    wid = _sc_worker_id()
    pred, trips, base = _sc_chunks(n_tok, wid)

    def run():
        @pl.loop(0, trips)
        def _(i):
            fn(pl.multiple_of(base(i), SC_CHUNK))

    if pred is None:
        run()
    else:
        pl.when(pred)(run)


def _sc_dispatch(parts, n_rows):
    half = parts[0][1].shape[1]
    n_parts = len(parts)

    @functools.partial(
        pl.kernel, mesh=_sc_mesh(),
        out_type=jax.ShapeDtypeStruct((n_rows, half), jnp.uint32),
        scratch_types=[pltpu.VMEM((TOP_K, SC_CHUNK), jnp.int32),
                       pltpu.VMEM((SC_CHUNK, half), jnp.uint32),
                       pltpu.SemaphoreType.DMA],
        compiler_params=pltpu.CompilerParams(use_tc_tiling_on_sc=True),
        name="sc_dispatch")
    def k(*refs):
        xs_hbm, idx_v, rows_v, sem = refs[2 * n_parts:]

        def move(pos_hbm, src_hbm):
            def fn(base):
                pltpu.sync_copy(pos_hbm.at[:, pl.ds(base, SC_CHUNK)], idx_v)
                pltpu.sync_copy(src_hbm.at[pl.ds(base, SC_CHUNK)], rows_v)
                copies = [pltpu.async_copy(rows_v, xs_hbm.at[idx_v.at[j]], sem) for j in range(TOP_K)]
                for c in copies:
                    c.wait()
            return fn

        for p, (_, hnp) in enumerate(parts):
            _sc_for_each_chunk(hnp.shape[0], move(refs[2 * p], refs[2 * p + 1]))

    return k(*[a for part in parts for a in part])


def _sc_gather(pos_parts, rows):
    half = rows.shape[1]
    n_parts = len(pos_parts)

    @functools.partial(
        pl.kernel, mesh=_sc_mesh(),
        out_type=tuple(jax.ShapeDtypeStruct((TOP_K, p.shape[1], half), jnp.uint32) for p in pos_parts),
        scratch_types=[pltpu.VMEM((TOP_K, SC_CHUNK), jnp.int32),
                       pltpu.VMEM((SC_CHUNK, half), jnp.uint32),
                       pltpu.SemaphoreType.DMA],
        compiler_params=pltpu.CompilerParams(use_tc_tiling_on_sc=True),
        name="sc_gather")
    def k(*refs):
        pos_hbms, rows_hbm = refs[:n_parts], refs[n_parts]
        out_hbms = refs[n_parts + 1:2 * n_parts + 1]
        idx_v, buf_v, sem = refs[2 * n_parts + 1:]

        def move(pos_hbm, out_hbm):
            def fn(base):
                pltpu.sync_copy(pos_hbm.at[:, pl.ds(base, SC_CHUNK)], idx_v)
                for j in range(TOP_K):
                    pltpu.async_copy(rows_hbm.at[idx_v.at[j]], buf_v, sem).wait()
                    pltpu.sync_copy(buf_v, out_hbm.at[j, pl.ds(base, SC_CHUNK)])
            return fn

        for pos_hbm, out_hbm, p in zip(pos_hbms, out_hbms, pos_parts):
            _sc_for_each_chunk(p.shape[1], move(pos_hbm, out_hbm))

    out = k(*pos_parts, rows)
    return out if isinstance(out, (tuple, list)) else (out,)


def _experts_kernel(blk_exp_ref, first_ref, slot_ref, next_ref, n_active_ref,
                    xs_ref, wgu_hbm, bgu_ref, wd_hbm, bd_ref, out_ref,
                    wgu_f32, wd_f32, wgu_bf, wd_bf, sems):
    d_ff = wd_bf.shape[0]

    def fetch(e, sl):
        return (pltpu.make_async_copy(wgu_hbm.at[e], wgu_f32.at[sl], sems.at[0, sl]),
                pltpu.make_async_copy(wd_hbm.at[e], wd_f32.at[sl], sems.at[1, sl]))

    def load_weights_if_first(i):
        sl = slot_ref[i]
        e = blk_exp_ref[i]

        @pl.when(first_ref[i] == 1)
        def _():
            @pl.when(i == 0)
            def _():
                for c in fetch(e, sl):
                    c.start()

            for c in fetch(e, sl):
                c.wait()
            wgu_bf[...] = wgu_f32[sl].astype(jnp.bfloat16)
            wd_bf[...] = wd_f32[sl].astype(jnp.bfloat16)

            @pl.when(next_ref[i] >= 0)
            def _():
                for c in fetch(next_ref[i], 1 - sl):
                    c.start()

    def compute(rows, e):
        x = _unpack_bf16_pairs(xs_ref[rows, :]).astype(jnp.bfloat16)
        gu = jnp.dot(x, wgu_bf[...], preferred_element_type=jnp.float32) + bgu_ref[e]
        g = jnp.minimum(gu[:, :d_ff], SWIGLU_LIMIT)
        u = jnp.clip(gu[:, d_ff:], -SWIGLU_LIMIT, SWIGLU_LIMIT)
        act = g * jax.nn.sigmoid(SWIGLU_ALPHA * g) * (u + 1.0)
        o = jnp.dot(act.astype(jnp.bfloat16), wd_bf[...], preferred_element_type=jnp.float32) + bd_ref[e]
        out_ref[rows, :] = _pack_bf16_pairs(o)

    i0 = pl.program_id(0) * BLOCKS_PER_STEP
    n_active = n_active_ref[0]

    def run(sub0, n_sub):
        first = i0 + sub0
        rows = slice(sub0 * MOE_BLOCK, (sub0 + n_sub) * MOE_BLOCK)
        if n_sub == 1:
            @pl.when(first < n_active)
            def _():
                load_weights_if_first(first)
                compute(rows, blk_exp_ref[first])

            @pl.when(first >= n_active)
            def _():
                out_ref[rows, :] = jnp.zeros((MOE_BLOCK, out_ref.shape[1]), out_ref.dtype)
            return

        same = first + n_sub - 1 < n_active
        for sub in range(1, n_sub):
            same = same & (blk_exp_ref[first + sub] == blk_exp_ref[first])

        @pl.when(same)
        def _():
            load_weights_if_first(first)
            compute(rows, blk_exp_ref[first])

        @pl.when(jnp.logical_not(same))
        def _():
            run(sub0, n_sub // 2)
            run(sub0 + n_sub // 2, n_sub // 2)

    run(0, BLOCKS_PER_STEP)


def _experts(plan, xs, wgu, bgu, wd, bd):
    blk_exp, first, slot, nxt, n_active = plan
    n_rows, half = xs.shape
    n_exp, d, two_f = wgu.shape
    d_ff = wd.shape[1]
    n_blocks = n_rows // MOE_BLOCK
    step_rows = MOE_BLOCK * BLOCKS_PER_STEP
    assert n_rows % step_rows == 0
    grid_spec = pltpu.PrefetchScalarGridSpec(
        num_scalar_prefetch=5,
        grid=(n_blocks // BLOCKS_PER_STEP,),
        in_specs=[
            pl.BlockSpec((step_rows, half), lambda i, *_: (i, 0)),
            pl.BlockSpec(memory_space=pl.ANY),
            pl.BlockSpec((n_exp, 1, two_f), lambda i, *_: (0, 0, 0)),
            pl.BlockSpec(memory_space=pl.ANY),
            pl.BlockSpec((n_exp, 1, d), lambda i, *_: (0, 0, 0)),
        ],
        out_specs=pl.BlockSpec((step_rows, half), lambda i, *_: (i, 0)),
        scratch_shapes=[
            pltpu.VMEM((2, d, two_f), jnp.float32),
            pltpu.VMEM((2, d_ff, d), jnp.float32),
            pltpu.VMEM((d, two_f), jnp.bfloat16),
            pltpu.VMEM((d_ff, d), jnp.bfloat16),
            pltpu.SemaphoreType.DMA((2, 2)),
        ],
    )
    return pl.pallas_call(
        _experts_kernel,
        grid_spec=grid_spec,
        out_shape=jax.ShapeDtypeStruct((n_rows, half), jnp.uint32),
        compiler_params=pltpu.CompilerParams(
            dimension_semantics=("arbitrary",), vmem_limit_bytes=VMEM_LIMIT),
        name="experts",
    )(blk_exp, first, slot, nxt, n_active, xs, wgu, bgu.reshape(n_exp, 1, two_f), wd,
      bd.reshape(n_exp, 1, d))


def _combine_kernel(rows_ref, gate_ref, h_ref, gfin_ref, *rest):
    y_ref = rest[-1]
    acc = h_ref[...]
    for k in range(TOP_K):
        acc = acc + gate_ref[:, k:k + 1] * _unpack_bf16_pairs(rows_ref[k])
    y_ref[...] = _rms(acc, gfin_ref[...])


def _combine(rows4, gate_t, h2d, gfin, *, tok0=0, y_prev=None, y_rows=None, y_tok0=0):
    d = h2d.shape[1]
    n, half = rows4.shape[1:]
    n_tok = h2d.shape[0] if y_rows is None else y_rows
    tile = min(n, 256)
    t0 = tok0 // tile
    y0 = y_tok0 // tile
    in_specs = [
        pl.BlockSpec((TOP_K, tile, half), lambda i: (0, i, 0)),
        pl.BlockSpec((tile, gate_t.shape[1]), lambda i: (t0 + i, 0)),
        pl.BlockSpec((tile, d), lambda i: (t0 + i, 0)),
        pl.BlockSpec((1, d), lambda i: (0, 0)),
    ]
    args = [rows4, gate_t, h2d, gfin]
    aliases = {}
    if y_prev is not None:
        in_specs.append(pl.BlockSpec(memory_space=pl.ANY))
        args.append(y_prev)
        aliases = {4: 0}
    return pl.pallas_call(
        _combine_kernel,
        grid=(n // tile,),
        in_specs=in_specs,
        out_specs=pl.BlockSpec((tile, d), lambda i: (y0 + i, 0)),
        out_shape=jax.ShapeDtypeStruct((n_tok, d), jnp.float32),
        input_output_aliases=aliases,
        compiler_params=pltpu.CompilerParams(dimension_semantics=("arbitrary",)),
        name="combine",
    )(*args)


def _pad_hist(hist, rows):
    return jnp.pad(hist, ((0, 0), (rows - hist.shape[1], 0), (0, 0)))


def kernel(x_prompt, x_sample, state_conv_a, state_conv_b, norm_mix_g, w_in, conv_a_w, conv_b_w, conv_b_b, conv_ln_g, conv_ln_b, out_norm_a_g, out_norm_b_g, w_out, norm_ffn_g, w_router, b_router, w_gate_up, b_gate_up, w_down, b_down, final_norm_g):
    depth = w_in.shape[0]
    assert depth == 1, "single-layer trunk"
    bf16 = jnp.bfloat16
    bsz, seq, d = x_prompt.shape
    dec_b, dec_s, _ = x_sample.shape
    d_a, d_b = conv_a_w.shape[-1], conv_b_w.shape[-1]
    n_exp = w_router.shape[-1]

    params = (norm_mix_g[0][None], w_in[0].astype(bf16), conv_a_w[0], conv_b_w[0], conv_b_b[0][None],
              conv_ln_g[0][None], conv_ln_b[0][None], out_norm_a_g[0][None], out_norm_b_g[0][None],
              w_out[0].astype(bf16), norm_ffn_g[0][None], w_router[0].T.astype(bf16),
              b_router[0][:, None])

    zero_a = jnp.zeros((bsz, HIST_A_ROWS, d_a), jnp.float32)
    zero_b = jnp.zeros((bsz, HIST_B_ROWS, d_b), jnp.float32)
    cnt0 = jnp.zeros((n_exp, 1), jnp.float32)
    gfin = final_norm_g[None]
    ts = min(seq, MIXER_ROWS)
    n_p = bsz * seq

    sizes = [bsz * f // sum(GROUP_SPLIT) for f in GROUP_SPLIT]
    if bsz % sum(GROUP_SPLIT) != 0:
        sizes = [bsz]
    groups = []
    tok_start = []
    for g, per_b in enumerate(sizes):
        b0 = sum(sizes[:g])
        tok_start.append(b0 * seq)
        h, hnp, idx, gate, rank, cnt, na, nb_ = _mixer(
            x_prompt, zero_a, zero_b, cnt0, params, nb=1, ts=ts, b0=b0, bsz=per_b)
        parts = [dict(h=h.reshape(per_b * seq, d), hnp=hnp, idx=idx, gate_t=gate, rank=rank)]
        states = [(na, nb_)]
        if g == len(sizes) - 1:
            h, hnp, idx, gate, rank, cnt, na_s, nb_s = _mixer(
                x_sample, _pad_hist(state_conv_a[0], HIST_A_ROWS), _pad_hist(state_conv_b[0], HIST_B_ROWS),
                cnt, params, nb=dec_b, ts=dec_s)
            parts.append(dict(h=h.reshape(dec_b * dec_s, d), hnp=hnp, idx=idx, gate_t=gate, rank=rank))
        n_pairs = sum(p["hnp"].shape[0] for p in parts) * TOP_K
        plan, pad_start, n_blocks = _block_plan(cnt[:, 0].astype(jnp.int32), n_pairs, n_exp)
        pos = _positions(pad_start, [(p["idx"], p["rank"]) for p in parts])
        xs = _sc_dispatch([(q, p["hnp"]) for q, p in zip(pos, parts)], n_blocks * MOE_BLOCK)
        groups.append(dict(parts=parts, pos=pos, plan=plan, xs=xs, states=states))

    for grp in groups:
        grp["rows"] = _experts(grp["plan"], grp["xs"], w_gate_up[0], b_gate_up[0], w_down[0], b_down[0])

    gathered = []
    for g, grp in enumerate(groups):
        pos0 = grp["pos"][0]
        n_g = pos0.shape[1]
        want = 1 if (g == len(groups) - 1 and len(groups) > 1) else COMBINE_CHUNKS
        n_chunks = want if n_g % (want * SC_CHUNK * SC_WORKERS) == 0 else 1
        per = n_g // n_chunks
        for c in range(n_chunks):
            pos_parts = [pos0[:, c * per:(c + 1) * per]] + (grp["pos"][1:] if c == 0 else [])
            gathered.append((g, c * per, _sc_gather(pos_parts, grp["rows"])))

    y_p = None
    for g, tok0, got in sorted(gathered, key=lambda t: (t[0] != len(groups) - 1, t[0], t[1])):
        grp = groups[g]
        main = grp["parts"][0]
        if len(got) > 1:
            smp = grp["parts"][1]
            y_s = _combine(got[1], smp["gate_t"], smp["h"], gfin).reshape(dec_b, dec_s, d)
        y_p = _combine(got[0], main["gate_t"], main["h"], gfin, tok0=tok0,
                       y_prev=y_p, y_rows=n_p, y_tok0=tok_start[g] + tok0)
    y_p = y_p.reshape(bsz, seq, d)

    na_p = jnp.concatenate([grp["states"][0][0] for grp in groups], axis=0)
    nb_p = jnp.concatenate([grp["states"][0][1] for grp in groups], axis=0)
    return (y_p, y_s, na_p[None], nb_p[None], na_s[None], nb_s[None])


def _block_plan(counts, n_pairs, n_exp):
    padded = (counts + MOE_BLOCK - 1) // MOE_BLOCK * MOE_BLOCK
    pad_end = jnp.cumsum(padded)
    pad_start = pad_end - padded
    n_blocks = -(-n_pairs // MOE_BLOCK) + n_exp
    n_blocks = -(-n_blocks // BLOCKS_PER_STEP) * BLOCKS_PER_STEP
    blk_start = jnp.arange(n_blocks, dtype=jnp.int32) * MOE_BLOCK
    blk_exp = jnp.minimum(jnp.sum(blk_start[:, None] >= pad_end[None, :], axis=1),
                          n_exp - 1).astype(jnp.int32)
    n_active = (pad_end[-1:] // MOE_BLOCK).astype(jnp.int32)
    e_ids = jnp.arange(n_exp, dtype=jnp.int32)
    has = padded > 0
    slot_e = (jnp.cumsum(has.astype(jnp.int32)) - 1) % 2
    later = jnp.where(has, e_ids, n_exp)
    next_e = jnp.concatenate([lax.cummin(later, reverse=True)[1:], jnp.full((1,), n_exp, jnp.int32)])
    next_e = jnp.where(next_e >= n_exp, -1, next_e)
    blk_onehot = blk_exp[:, None] == e_ids[None, :]

    def per_block(table):
        return jnp.sum(jnp.where(blk_onehot, table[None, :], 0), axis=1).astype(jnp.int32)

    blk_first = ((blk_start == per_block(pad_start)) & (blk_start < pad_end[-1])).astype(jnp.int32)
    plan = (blk_exp, blk_first, per_block(slot_e), per_block(next_e), n_active)
    return plan, pad_start.astype(jnp.int32), n_blocks
```

```python
import functools

import jax
import jax.numpy as jnp
from jax import lax
from jax.experimental import pallas as pl
from jax.experimental.pallas import tpu as pltpu
from jax.experimental.pallas import tpu_sc as plsc

SC_CORES = 2
SC_SUBCORES = 16
SC_WORKERS = SC_CORES * SC_SUBCORES
SC_CHUNK = 128
CONV_A_W = 3
CONV_B_W = 31
TOP_K = 4
SWIGLU_LIMIT = 7.0
SWIGLU_ALPHA = 1.702
NORM_EPS = 1e-5
MOE_BLOCK = 256
BLOCKS_PER_STEP = 4
SUBLANES = 8
BF16_ROWS = 16
LANES = 128
CONV_ROW_CHUNK = 128
HIST_A_ROWS = 8
HIST_B_ROWS = 32
MIXER_ROWS = 512
PROJ_SECTION = 256
POS_TILE = 4096
GROUP_SPLIT = (1, 1)
COMBINE_CHUNKS = 2
VMEM_LIMIT = 56 * 1024 * 1024


def _rms(x, g):
    return x * lax.rsqrt(jnp.mean(x * x, axis=-1, keepdims=True) + NORM_EPS) * g


def _zero_tile_from(v):
    bits = pltpu.bitcast(v[0:SUBLANES, 0:LANES], jnp.uint32)
    return pltpu.bitcast((bits >> 16) >> 16, jnp.float32)


def _pack_bf16_pairs(a):
    w = a.shape[-1] // 2
    lo = pltpu.bitcast(a[:, :w].astype(jnp.bfloat16).astype(jnp.float32), jnp.uint32)
    hi = pltpu.bitcast(a[:, w:].astype(jnp.bfloat16).astype(jnp.float32), jnp.uint32)
    return (lo >> 16) | (hi & jnp.uint32(0xFFFF0000))


def _unpack_bf16_pairs(wd):
    lo = pltpu.bitcast(wd << 16, jnp.float32)
    hi = pltpu.bitcast(wd & jnp.uint32(0xFFFF0000), jnp.float32)
    return jnp.concatenate([lo, hi], axis=-1)


def _conv_b_chunk(ub_buf, cbw_ref, n_i, r0, rc, c0):
    cs = slice(c0, c0 + LANES)
    lead = HIST_B_ROWS - (CONV_B_W - 1)
    acc = None
    for b in range(SUBLANES):
        rows = rc if b == 0 else rc + BF16_ROWS
        q = None
        for a in range((HIST_B_ROWS + SUBLANES) // SUBLANES):
            k = SUBLANES * a + b - lead
            if k < 0 or k >= CONV_B_W:
                continue
            start = r0 + SUBLANES * (a - a % 2)
            u = ub_buf[a % 2, n_i, start:start + rows, cs].reshape(rows // BF16_ROWS, BF16_ROWS, LANES)
            term = u * cbw_ref[k, :, cs][None]
            q = term if q is None else q + term
        q = q.reshape(rows, LANES).astype(jnp.float32)
        part = q if b == 0 else q[b:b + rc]
        acc = part if acc is None else acc + part
    return acc


def _mixer_kernel(x_ref, hista_ref, histb_ref, cnt0_ref, tri_ref,
                  gmix_ref, win_ref, caw_ref, cbw_ref, cbb_ref, lng_ref, lnb_ref,
                  ga_ref, gb_ref, wout_ref, gffn_ref, wrt_ref, br_ref,
                  h_ref, hnp_ref, idx_ref, gate_ref, rank_ref, cnt_ref, newa_ref, newb_ref,
                  ua_buf, ub_buf, ub16_buf, cb_buf, cnt_acc, *, nb, ts, d_a, d_b, n_exp):
    b = pl.program_id(0)
    s = pl.program_id(1)
    m = nb * ts
    d = x_ref.shape[-1]

    @pl.when(s == 0)
    def _():
        ua_buf[:, 0:HIST_A_ROWS, :] = hista_ref[...]
        ub_buf[:, 0:HIST_B_ROWS, :] = histb_ref[...]
        ub_buf[:, HIST_B_ROWS + ts:HIST_B_ROWS + ts + SUBLANES, :] = jnp.zeros((nb, SUBLANES, d_b), jnp.float32)

    @pl.when(s != 0)
    def _():
        ua_buf[:, 0:HIST_A_ROWS, :] = ua_buf[:, ts:ts + HIST_A_ROWS, :]
        ub_buf[:, 0:HIST_B_ROWS, :] = ub_buf[:, ts:ts + HIST_B_ROWS, :]

    @pl.when((b == 0) & (s == 0))
    def _():
        cnt_acc[...] = cnt0_ref[...]

    x = x_ref[...].reshape(m, d)
    n = _rms(x, gmix_ref[...]).astype(jnp.bfloat16)
    def proj(c0, width):
        return jnp.dot(n, win_ref[:, c0:c0 + width], preferred_element_type=jnp.float32)

    sec = min(d_b, PROJ_SECTION)
    for c0 in range(0, d_b, sec):
        v_b = proj(3 * d_a + c0, sec)
        g_b = proj(3 * d_a + d_b + c0, sec)
        ub_buf[:, HIST_B_ROWS:HIST_B_ROWS + ts, c0:c0 + sec] = (v_b * jax.nn.sigmoid(g_b)).reshape(nb, ts, sec)

    gate_c = proj(d_a, d_a)
    xt = proj(2 * d_a, d_a)
    gate_b = proj(0, d_a)
    ua_buf[:, HIST_A_ROWS:HIST_A_ROWS + ts, :] = (gate_c * xt).reshape(nb, ts, d_a)
    conv_a = jnp.zeros((nb, ts, d_a), jnp.float32)
    for k in range(CONV_A_W):
        off = HIST_A_ROWS - (CONV_A_W - 1) + k
        conv_a = conv_a + caw_ref[k:k + 1, :] * ua_buf[:, off:off + ts, :]
    y_a = gate_b * conv_a.reshape(m, d_a)

    mix_a = jnp.dot(_rms(y_a, ga_ref[...]).astype(jnp.bfloat16), wout_ref[0:d_a, :],
                    preferred_element_type=jnp.float32)

    n_stage = HIST_B_ROWS + ts
    ub16_buf[0] = ub_buf[:, 0:n_stage, :].astype(jnp.bfloat16)
    ub16_buf[1] = ub_buf[:, SUBLANES:SUBLANES + n_stage, :].astype(jnp.bfloat16)

    rc = min(ts, CONV_ROW_CHUNK)
    chunks = [(n_i, r0, c0) for n_i in range(nb) for r0 in range(0, ts, rc) for c0 in range(0, d_b, LANES)]
    anchors = {len(chunks) * 2 // 8: gate_c, len(chunks) * 3 // 8: xt, len(chunks) * 5 // 8: gate_b,
               len(chunks) - 1: mix_a}
    for ci, (n_i, r0, c0) in enumerate(chunks):
        acc = _conv_b_chunk(ub16_buf, cbw_ref, n_i, r0, rc, c0)
        if ci in anchors:
            acc = (acc.reshape(rc // SUBLANES, SUBLANES, LANES) + _zero_tile_from(anchors[ci])[None]
                   ).reshape(rc, LANES)
        cb_buf[n_i, r0:r0 + rc, c0:c0 + LANES] = acc
    cb = cb_buf[...].reshape(m, d_b) + cbb_ref[...]
    mu = jnp.mean(cb, axis=-1, keepdims=True)
    xc = cb - mu
    var = jnp.mean(xc * xc, axis=-1, keepdims=True)
    ln = xc * lax.rsqrt(var + NORM_EPS) * lng_ref[...] + lnb_ref[...]
    y_b = ln * jax.nn.sigmoid(ln)

    mix_b = jnp.dot(_rms(y_b, gb_ref[...]).astype(jnp.bfloat16), wout_ref[d_a:d_a + d_b, :],
                    preferred_element_type=jnp.float32)
    h = x + (mix_a + mix_b)
    h_ref[...] = h.reshape(nb, ts, d)

    hn = _rms(h, gffn_ref[...])
    hnp_ref[...] = _pack_bf16_pairs(hn)

    logits = lax.dot_general(wrt_ref[...], hn.astype(jnp.bfloat16), (((1,), (1,)), ((), ())),
                             preferred_element_type=jnp.float32) + br_ref[...]
    e_iota = lax.broadcasted_iota(jnp.int32, (n_exp, m), 0)
    cur = logits
    vals, idxs, sels = [], [], []
    for _ in range(TOP_K):
        mx = jnp.max(cur, axis=0, keepdims=True)
        ix = jnp.min(jnp.where(cur == mx, e_iota, n_exp), axis=0, keepdims=True)
        sel = e_iota == ix
        vals.append(mx)
        idxs.append(ix)
        sels.append(sel)
        cur = jnp.where(sel, -jnp.inf, cur)
    exps = [jnp.exp(v - vals[0]) for v in vals]
    denom = exps[0] + exps[1] + exps[2] + exps[3]
    gates = [e / denom for e in exps]

    onehot = jnp.zeros((n_exp, m), jnp.float32)
    for sel in sels:
        onehot = onehot + sel.astype(jnp.float32)
    before = jnp.dot(onehot.astype(jnp.bfloat16), tri_ref[...],
                     preferred_element_type=jnp.float32) + cnt_acc[...]
    ranks = [jnp.sum(jnp.where(sel, before, 0.0), axis=0, keepdims=True) for sel in sels]
    new_cnt = cnt_acc[...] + jnp.sum(onehot, axis=1, keepdims=True)
    cnt_acc[...] = new_cnt
    cnt_ref[...] = new_cnt

    idx_ref[...] = jnp.concatenate(idxs, axis=0)
    gate_ref[...] = jnp.concatenate(gates + [jnp.zeros((SUBLANES - TOP_K, m), jnp.float32)], axis=0).T
    rank_ref[...] = jnp.concatenate(ranks, axis=0).astype(jnp.int32)

    @pl.when(s == pl.num_programs(1) - 1)
    def _():
        newa_ref[...] = ua_buf[:, ts + HIST_A_ROWS - (CONV_A_W - 1):ts + HIST_A_ROWS, :]
        newb_ref[...] = ub_buf[:, ts + HIST_B_ROWS - (CONV_B_W - 1):ts + HIST_B_ROWS, :]


def _mixer(x, hist_a, hist_b, cnt0, params, *, nb, ts, b0=0, bsz=None):
    (gmix, win, caw, cbw, cbb, lng, lnb, ga, gb, wout, gffn, wrt, br) = params
    _, seq, d = x.shape
    bsz = x.shape[0] if bsz is None else bsz
    d_a, d_b = caw.shape[-1], cbw.shape[-1]
    n_exp = wrt.shape[0]
    m = nb * ts
    n_tok = bsz * seq
    g0 = b0 // nb
    grid = (bsz // nb, seq // ts)
    tri = (jnp.arange(m)[:, None] < jnp.arange(m)[None, :]).astype(jnp.bfloat16)

    def const(shape):
        return pl.BlockSpec(shape, lambda i, j: (0,) * len(shape))

    tok_map = lambda i, j: (0, i * (seq // ts) + j)
    kern = functools.partial(_mixer_kernel, nb=nb, ts=ts, d_a=d_a, d_b=d_b, n_exp=n_exp)
    out_shape = (
        jax.ShapeDtypeStruct((bsz, seq, d), jnp.float32),
        jax.ShapeDtypeStruct((n_tok, d // 2), jnp.uint32),
        jax.ShapeDtypeStruct((TOP_K, n_tok), jnp.int32),
        jax.ShapeDtypeStruct((n_tok, SUBLANES), jnp.float32),
        jax.ShapeDtypeStruct((TOP_K, n_tok), jnp.int32),
        jax.ShapeDtypeStruct((n_exp, 1), jnp.float32),
        jax.ShapeDtypeStruct((bsz, CONV_A_W - 1, d_a), jnp.float32),
        jax.ShapeDtypeStruct((bsz, CONV_B_W - 1, d_b), jnp.float32),
    )
    return pl.pallas_call(
        kern,
        grid=grid,
        in_specs=[
            pl.BlockSpec((nb, ts, d), lambda i, j: (g0 + i, j, 0)),
            pl.BlockSpec((nb, HIST_A_ROWS, d_a), lambda i, j: (g0 + i, 0, 0)),
            pl.BlockSpec((nb, HIST_B_ROWS, d_b), lambda i, j: (g0 + i, 0, 0)),
            const((n_exp, 1)),
            const((m, m)),
            const((1, d)), const(win.shape), const(caw.shape), const(cbw.shape), const((1, d_b)),
            const((1, d_b)), const((1, d_b)), const((1, d_a)), const((1, d_b)), const(wout.shape),
            const((1, d)), const(wrt.shape), const((n_exp, 1)),
        ],
        out_specs=(
            pl.BlockSpec((nb, ts, d), lambda i, j: (i, j, 0)),
            pl.BlockSpec((m, d // 2), lambda i, j: (i * (seq // ts) + j, 0)),
            pl.BlockSpec((TOP_K, m), tok_map),
            pl.BlockSpec((m, SUBLANES), lambda i, j: (i * (seq // ts) + j, 0)),
            pl.BlockSpec((TOP_K, m), tok_map),
            pl.BlockSpec((n_exp, 1), lambda i, j: (0, 0)),
            pl.BlockSpec((nb, CONV_A_W - 1, d_a), lambda i, j: (i, 0, 0)),
            pl.BlockSpec((nb, CONV_B_W - 1, d_b), lambda i, j: (i, 0, 0)),
        ),
        out_shape=out_shape,
        scratch_shapes=[
            pltpu.VMEM((nb, HIST_A_ROWS + ts, d_a), jnp.float32),
            pltpu.VMEM((nb, HIST_B_ROWS + ts + SUBLANES, d_b), jnp.float32),
            pltpu.VMEM((2, nb, HIST_B_ROWS + ts, d_b), jnp.bfloat16),
            pltpu.VMEM((nb, ts, d_b), jnp.float32),
            pltpu.VMEM((n_exp, 1), jnp.float32),
        ],
        compiler_params=pltpu.CompilerParams(
            dimension_semantics=("arbitrary", "arbitrary"), vmem_limit_bytes=VMEM_LIMIT),
        name="mixer",
    )(x, hist_a, hist_b, cnt0, tri, gmix, win, caw, cbw, cbb, lng, lnb, ga, gb, wout, gffn, wrt, br)


def _positions_kernel(start_ref, *refs):
    n_parts = len(refs) // 3
    for p in range(n_parts):
        idx = refs[2 * p][...]
        pos = refs[2 * p + 1][...]
        for e in range(start_ref.shape[0]):
            pos = pos + jnp.where(idx == e, start_ref[e], 0)
        refs[2 * n_parts + p][...] = pos


def _positions(pad_start, parts):
    n_0 = parts[0][0].shape[1]
    tile = min(n_0, POS_TILE)
    specs = [pl.BlockSpec((TOP_K, tile), lambda i: (0, i))]
    specs += [pl.BlockSpec((TOP_K, idx.shape[1]), lambda i: (0, 0)) for idx, _ in parts[1:]]
    in_specs = [pl.BlockSpec(memory_space=pltpu.SMEM)]
    args = [pad_start]
    for spec, (idx, rank) in zip(specs, parts):
        in_specs += [spec, spec]
        args += [idx, rank]
    out = pl.pallas_call(
        _positions_kernel,
        grid=(n_0 // tile,),
        in_specs=in_specs,
        out_specs=tuple(specs),
        out_shape=tuple(jax.ShapeDtypeStruct(idx.shape, jnp.int32) for idx, _ in parts),
        compiler_params=pltpu.CompilerParams(dimension_semantics=("arbitrary",)),
        name="positions",
    )(*args)
    return list(out)


def _sc_mesh():
    return plsc.VectorSubcoreMesh(core_axis_name="c", subcore_axis_name="s",
                                  num_cores=SC_CORES, num_subcores=SC_SUBCORES)


def _sc_worker_id():
    return lax.axis_index("s") * SC_CORES + lax.axis_index("c")


def _sc_chunks(n_tok, wid):
    n_chunks = n_tok // SC_CHUNK
    assert n_tok % SC_CHUNK == 0
    if n_chunks % SC_WORKERS == 0:
        per_w = n_chunks // SC_WORKERS
        return None, per_w, lambda i: (wid * per_w + i) * SC_CHUNK
    assert n_chunks <= SC_WORKERS
    return wid < n_chunks, 1, lambda i: wid * SC_CHUNK


def _sc_for_each_chunk(n_tok, fn):
    wid = _sc_worker_id()
    pred, trips, base = _sc_chunks(n_tok, wid)

    def run():
        @pl.loop(0, trips)
        def _(i):
            fn(pl.multiple_of(base(i), SC_CHUNK))

    if pred is None:
        run()
    else:
        pl.when(pred)(run)


def _sc_dispatch(parts, n_rows):
    half = parts[0][1].shape[1]
    n_parts = len(parts)

    @functools.partial(
        pl.kernel, mesh=_sc_mesh(),
        out_type=jax.ShapeDtypeStruct((n_rows, half), jnp.uint32),
        scratch_types=[pltpu.VMEM((TOP_K, SC_CHUNK), jnp.int32),
                       pltpu.VMEM((SC_CHUNK, half), jnp.uint32),
                       pltpu.SemaphoreType.DMA],
        compiler_params=pltpu.CompilerParams(use_tc_tiling_on_sc=True),
        name="sc_dispatch")
    def k(*refs):
        xs_hbm, idx_v, rows_v, sem = refs[2 * n_parts:]

        def move(pos_hbm, src_hbm):
            def fn(base):
                pltpu.sync_copy(pos_hbm.at[:, pl.ds(base, SC_CHUNK)], idx_v)
                pltpu.sync_copy(src_hbm.at[pl.ds(base, SC_CHUNK)], rows_v)
                copies = [pltpu.async_copy(rows_v, xs_hbm.at[idx_v.at[j]], sem) for j in range(TOP_K)]
                for c in copies:
                    c.wait()
            return fn

        for p, (_, hnp) in enumerate(parts):
            _sc_for_each_chunk(hnp.shape[0], move(refs[2 * p], refs[2 * p + 1]))

    return k(*[a for part in parts for a in part])


def _sc_gather(pos_parts, rows):
    half = rows.shape[1]
    n_parts = len(pos_parts)

    @functools.partial(
        pl.kernel, mesh=_sc_mesh(),
        out_type=tuple(jax.ShapeDtypeStruct((TOP_K, p.shape[1], half), jnp.uint32) for p in pos_parts),
        scratch_types=[pltpu.VMEM((TOP_K, SC_CHUNK), jnp.int32),
                       pltpu.VMEM((SC_CHUNK, half), jnp.uint32),
                       pltpu.SemaphoreType.DMA],
        compiler_params=pltpu.CompilerParams(use_tc_tiling_on_sc=True),
        name="sc_gather")
    def k(*refs):
        pos_hbms, rows_hbm = refs[:n_parts], refs[n_parts]
        out_hbms = refs[n_parts + 1:2 * n_parts + 1]
        idx_v, buf_v, sem = refs[2 * n_parts + 1:]

        def move(pos_hbm, out_hbm):
            def fn(base):
                pltpu.sync_copy(pos_hbm.at[:, pl.ds(base, SC_CHUNK)], idx_v)
                for j in range(TOP_K):
                    pltpu.async_copy(rows_hbm.at[idx_v.at[j]], buf_v, sem).wait()
                    pltpu.sync_copy(buf_v, out_hbm.at[j, pl.ds(base, SC_CHUNK)])
            return fn

        for pos_hbm, out_hbm, p in zip(pos_hbms, out_hbms, pos_parts):
            _sc_for_each_chunk(p.shape[1], move(pos_hbm, out_hbm))

    out = k(*pos_parts, rows)
    return out if isinstance(out, (tuple, list)) else (out,)


def _experts_kernel(blk_exp_ref, first_ref, slot_ref, next_ref, n_active_ref,
                    xs_ref, wgu_hbm, bgu_ref, wd_hbm, bd_ref, out_ref,
                    wgu_f32, wd_f32, wgu_bf, wd_bf, sems):
    d_ff = wd_bf.shape[0]

    def fetch(e, sl):
        return (pltpu.make_async_copy(wgu_hbm.at[e], wgu_f32.at[sl], sems.at[0, sl]),
                pltpu.make_async_copy(wd_hbm.at[e], wd_f32.at[sl], sems.at[1, sl]))

    def load_weights_if_first(i):
        sl = slot_ref[i]
        e = blk_exp_ref[i]

        @pl.when(first_ref[i] == 1)
        def _():
            @pl.when(i == 0)
            def _():
                for c in fetch(e, sl):
                    c.start()

            for c in fetch(e, sl):
                c.wait()
            wgu_bf[...] = wgu_f32[sl].astype(jnp.bfloat16)
            wd_bf[...] = wd_f32[sl].astype(jnp.bfloat16)

            @pl.when(next_ref[i] >= 0)
            def _():
                for c in fetch(next_ref[i], 1 - sl):
                    c.start()

    def compute(rows, e):
        x = _unpack_bf16_pairs(xs_ref[rows, :]).astype(jnp.bfloat16)
        gu = jnp.dot(x, wgu_bf[...], preferred_element_type=jnp.float32) + bgu_ref[e]
        g = jnp.minimum(gu[:, :d_ff], SWIGLU_LIMIT)
        u = jnp.clip(gu[:, d_ff:], -SWIGLU_LIMIT, SWIGLU_LIMIT)
        act = g * jax.nn.sigmoid(SWIGLU_ALPHA * g) * (u + 1.0)
        o = jnp.dot(act.astype(jnp.bfloat16), wd_bf[...], preferred_element_type=jnp.float32) + bd_ref[e]
        out_ref[rows, :] = _pack_bf16_pairs(o)

    i0 = pl.program_id(0) * BLOCKS_PER_STEP
    n_active = n_active_ref[0]

    def run(sub0, n_sub):
        first = i0 + sub0
        rows = slice(sub0 * MOE_BLOCK, (sub0 + n_sub) * MOE_BLOCK)
        if n_sub == 1:
            @pl.when(first < n_active)
            def _():
                load_weights_if_first(first)
                compute(rows, blk_exp_ref[first])

            @pl.when(first >= n_active)
            def _():
                out_ref[rows, :] = jnp.zeros((MOE_BLOCK, out_ref.shape[1]), out_ref.dtype)
            return

        same = first + n_sub - 1 < n_active
        for sub in range(1, n_sub):
            same = same & (blk_exp_ref[first + sub] == blk_exp_ref[first])

        @pl.when(same)
        def _():
            load_weights_if_first(first)
            compute(rows, blk_exp_ref[first])

        @pl.when(jnp.logical_not(same))
        def _():
            run(sub0, n_sub // 2)
            run(sub0 + n_sub // 2, n_sub // 2)

    run(0, BLOCKS_PER_STEP)


def _experts(plan, xs, wgu, bgu, wd, bd):
    blk_exp, first, slot, nxt, n_active = plan
    n_rows, half = xs.shape
    n_exp, d, two_f = wgu.shape
    d_ff = wd.shape[1]
    n_blocks = n_rows // MOE_BLOCK
    step_rows = MOE_BLOCK * BLOCKS_PER_STEP
    assert n_rows % step_rows == 0
    grid_spec = pltpu.PrefetchScalarGridSpec(
        num_scalar_prefetch=5,
        grid=(n_blocks // BLOCKS_PER_STEP,),
        in_specs=[
            pl.BlockSpec((step_rows, half), lambda i, *_: (i, 0)),
            pl.BlockSpec(memory_space=pl.ANY),
            pl.BlockSpec((n_exp, 1, two_f), lambda i, *_: (0, 0, 0)),
            pl.BlockSpec(memory_space=pl.ANY),
            pl.BlockSpec((n_exp, 1, d), lambda i, *_: (0, 0, 0)),
        ],
        out_specs=pl.BlockSpec((step_rows, half), lambda i, *_: (i, 0)),
        scratch_shapes=[
            pltpu.VMEM((2, d, two_f), jnp.float32),
            pltpu.VMEM((2, d_ff, d), jnp.float32),
            pltpu.VMEM((d, two_f), jnp.bfloat16),
            pltpu.VMEM((d_ff, d), jnp.bfloat16),
            pltpu.SemaphoreType.DMA((2, 2)),
        ],
    )
    return pl.pallas_call(
        _experts_kernel,
        grid_spec=grid_spec,
        out_shape=jax.ShapeDtypeStruct((n_rows, half), jnp.uint32),
        compiler_params=pltpu.CompilerParams(
            dimension_semantics=("arbitrary",), vmem_limit_bytes=VMEM_LIMIT),
        name="experts",
    )(blk_exp, first, slot, nxt, n_active, xs, wgu, bgu.reshape(n_exp, 1, two_f), wd,
      bd.reshape(n_exp, 1, d))


def _combine_kernel(rows_ref, gate_ref, h_ref, gfin_ref, *rest):
    y_ref = rest[-1]
    acc = h_ref[...]
    for k in range(TOP_K):
        acc = acc + gate_ref[:, k:k + 1] * _unpack_bf16_pairs(rows_ref[k])
    y_ref[...] = _rms(acc, gfin_ref[...])


def _combine(rows4, gate_t, h2d, gfin, *, tok0=0, y_prev=None, y_rows=None, y_tok0=0):
    d = h2d.shape[1]
    n, half = rows4.shape[1:]
    n_tok = h2d.shape[0] if y_rows is None else y_rows
    tile = min(n, 256)
    t0 = tok0 // tile
    y0 = y_tok0 // tile
    in_specs = [
        pl.BlockSpec((TOP_K, tile, half), lambda i: (0, i, 0)),
        pl.BlockSpec((tile, gate_t.shape[1]), lambda i: (t0 + i, 0)),
        pl.BlockSpec((tile, d), lambda i: (t0 + i, 0)),
        pl.BlockSpec((1, d), lambda i: (0, 0)),
    ]
    args = [rows4, gate_t, h2d, gfin]
    aliases = {}
    if y_prev is not None:
        in_specs.append(pl.BlockSpec(memory_space=pl.ANY))
        args.append(y_prev)
        aliases = {4: 0}
    return pl.pallas_call(
        _combine_kernel,
        grid=(n // tile,),
        in_specs=in_specs,
        out_specs=pl.BlockSpec((tile, d), lambda i: (y0 + i, 0)),
        out_shape=jax.ShapeDtypeStruct((n_tok, d), jnp.float32),
        input_output_aliases=aliases,
        compiler_params=pltpu.CompilerParams(dimension_semantics=("arbitrary",)),
        name="combine",
    )(*args)


def _pad_hist(hist, rows):
    return jnp.pad(hist, ((0, 0), (rows - hist.shape[1], 0), (0, 0)))


def kernel(x_prompt, x_sample, state_conv_a, state_conv_b, norm_mix_g, w_in, conv_a_w, conv_b_w, conv_b_b, conv_ln_g, conv_ln_b, out_norm_a_g, out_norm_b_g, w_out, norm_ffn_g, w_router, b_router, w_gate_up, b_gate_up, w_down, b_down, final_norm_g):
    depth = w_in.shape[0]
    assert depth == 1, "single-layer trunk"
    bf16 = jnp.bfloat16
    bsz, seq, d = x_prompt.shape
    dec_b, dec_s, _ = x_sample.shape
    d_a, d_b = conv_a_w.shape[-1], conv_b_w.shape[-1]
    n_exp = w_router.shape[-1]

    cbw16 = jnp.broadcast_to(conv_b_w[0].astype(bf16)[:, None, :], (CONV_B_W, BF16_ROWS, d_b))
    params = (norm_mix_g[0][None], w_in[0].astype(bf16), conv_a_w[0], cbw16, conv_b_b[0][None],
              conv_ln_g[0][None], conv_ln_b[0][None], out_norm_a_g[0][None], out_norm_b_g[0][None],
              w_out[0].astype(bf16), norm_ffn_g[0][None], w_router[0].T.astype(bf16),
              b_router[0][:, None])

    zero_a = jnp.zeros((bsz, HIST_A_ROWS, d_a), jnp.float32)
    zero_b = jnp.zeros((bsz, HIST_B_ROWS, d_b), jnp.float32)
    cnt0 = jnp.zeros((n_exp, 1), jnp.float32)
    gfin = final_norm_g[None]
    ts = min(seq, MIXER_ROWS)
    n_p = bsz * seq

    sizes = [bsz * f // sum(GROUP_SPLIT) for f in GROUP_SPLIT]
    if bsz % sum(GROUP_SPLIT) != 0:
        sizes = [bsz]
    groups = []
    tok_start = []
    for g, per_b in enumerate(sizes):
        b0 = sum(sizes[:g])
        tok_start.append(b0 * seq)
        h, hnp, idx, gate, rank, cnt, na, nb_ = _mixer(
            x_prompt, zero_a, zero_b, cnt0, params, nb=1, ts=ts, b0=b0, bsz=per_b)
        parts = [dict(h=h.reshape(per_b * seq, d), hnp=hnp, idx=idx, gate_t=gate, rank=rank)]
        states = [(na, nb_)]
        if g == len(sizes) - 1:
            h, hnp, idx, gate, rank, cnt, na_s, nb_s = _mixer(
                x_sample, _pad_hist(state_conv_a[0], HIST_A_ROWS), _pad_hist(state_conv_b[0], HIST_B_ROWS),
                cnt, params, nb=dec_b, ts=dec_s)
            parts.append(dict(h=h.reshape(dec_b * dec_s, d), hnp=hnp, idx=idx, gate_t=gate, rank=rank))
        n_pairs = sum(p["hnp"].shape[0] for p in parts) * TOP_K
        plan, pad_start, n_blocks = _block_plan(cnt[:, 0].astype(jnp.int32), n_pairs, n_exp)
        pos = _positions(pad_start, [(p["idx"], p["rank"]) for p in parts])
        xs = _sc_dispatch([(q, p["hnp"]) for q, p in zip(pos, parts)], n_blocks * MOE_BLOCK)
        groups.append(dict(parts=parts, pos=pos, plan=plan, xs=xs, states=states))

    for grp in groups:
        grp["rows"] = _experts(grp["plan"], grp["xs"], w_gate_up[0], b_gate_up[0], w_down[0], b_down[0])

    gathered = []
    for g, grp in enumerate(groups):
        pos0 = grp["pos"][0]
        n_g = pos0.shape[1]
        want = 1 if (g == len(groups) - 1 and len(groups) > 1) else COMBINE_CHUNKS
        n_chunks = want if n_g % (want * SC_CHUNK * SC_WORKERS) == 0 else 1
        per = n_g // n_chunks
        for c in range(n_chunks):
            pos_parts = [pos0[:, c * per:(c + 1) * per]] + (grp["pos"][1:] if c == 0 else [])
            gathered.append((g, c * per, _sc_gather(pos_parts, grp["rows"])))

    y_p = None
    for g, tok0, got in sorted(gathered, key=lambda t: (t[0] != len(groups) - 1, t[0], t[1])):
        grp = groups[g]
        main = grp["parts"][0]
        if len(got) > 1:
            smp = grp["parts"][1]
            y_s = _combine(got[1], smp["gate_t"], smp["h"], gfin).reshape(dec_b, dec_s, d)
        y_p = _combine(got[0], main["gate_t"], main["h"], gfin, tok0=tok0,
                       y_prev=y_p, y_rows=n_p, y_tok0=tok_start[g] + tok0)
    y_p = y_p.reshape(bsz, seq, d)

    na_p = jnp.concatenate([grp["states"][0][0] for grp in groups], axis=0)
    nb_p = jnp.concatenate([grp["states"][0][1] for grp in groups], axis=0)
    return (y_p, y_s, na_p[None], nb_p[None], na_s[None], nb_s[None])


def _block_plan(counts, n_pairs, n_exp):
    padded = (counts + MOE_BLOCK - 1) // MOE_BLOCK * MOE_BLOCK
    pad_end = jnp.cumsum(padded)
    pad_start = pad_end - padded
    n_blocks = -(-n_pairs // MOE_BLOCK) + n_exp
    n_blocks = -(-n_blocks // BLOCKS_PER_STEP) * BLOCKS_PER_STEP
    blk_start = jnp.arange(n_blocks, dtype=jnp.int32) * MOE_BLOCK
    blk_exp = jnp.minimum(jnp.sum(blk_start[:, None] >= pad_end[None, :], axis=1),
                          n_exp - 1).astype(jnp.int32)
    n_active = (pad_end[-1:] // MOE_BLOCK).astype(jnp.int32)
    e_ids = jnp.arange(n_exp, dtype=jnp.int32)
    has = padded > 0
    slot_e = (jnp.cumsum(has.astype(jnp.int32)) - 1) % 2
    later = jnp.where(has, e_ids, n_exp)
    next_e = jnp.concatenate([lax.cummin(later, reverse=True)[1:], jnp.full((1,), n_exp, jnp.int32)])
    next_e = jnp.where(next_e >= n_exp, -1, next_e)
    blk_onehot = blk_exp[:, None] == e_ids[None, :]

    def per_block(table):
        return jnp.sum(jnp.where(blk_onehot, table[None, :], 0), axis=1).astype(jnp.int32)

    blk_first = ((blk_start == per_block(pad_start)) & (blk_start < pad_end[-1])).astype(jnp.int32)
    plan = (blk_exp, blk_first, per_block(slot_e), per_block(next_e), n_active)
    return plan, pad_start.astype(jnp.int32), n_blocks
```

```python
import functools

import jax
import jax.numpy as jnp
from jax import lax
from jax.experimental import pallas as pl
from jax.experimental.pallas import tpu as pltpu
from jax.experimental.pallas import tpu_sc as plsc

SC_CORES = 2
SC_SUBCORES = 16
SC_WORKERS = SC_CORES * SC_SUBCORES
SC_CHUNK = 128
CONV_A_W = 3
CONV_B_W = 31
TOP_K = 4
SWIGLU_LIMIT = 7.0
SWIGLU_ALPHA = 1.702
NORM_EPS = 1e-5
MOE_BLOCK = 256
BLOCKS_PER_STEP = 4
SUBLANES = 8
LANES = 128
CONV_ROW_CHUNK = 128
HIST_A_ROWS = 8
HIST_B_ROWS = 32
MIXER_ROWS = 1024
PROJ_SECTION = 256
COMBINE_TILE = 1024
POS_TILE = 4096
GROUP_SPLIT = (1, 1)
COMBINE_CHUNKS = 2
VMEM_LIMIT = 56 * 1024 * 1024


def _rms(x, g):
    return x * lax.rsqrt(jnp.mean(x * x, axis=-1, keepdims=True) + NORM_EPS) * g


def _zero_tile_from(v):
    bits = pltpu.bitcast(v[0:SUBLANES, 0:LANES], jnp.uint32)
    return pltpu.bitcast((bits >> 16) >> 16, jnp.float32)


def _pack_bf16_pairs(a):
    w = a.shape[-1] // 2
    lo = pltpu.bitcast(a[:, :w].astype(jnp.bfloat16).astype(jnp.float32), jnp.uint32)
    hi = pltpu.bitcast(a[:, w:].astype(jnp.bfloat16).astype(jnp.float32), jnp.uint32)
    return (lo >> 16) | (hi & jnp.uint32(0xFFFF0000))


def _unpack_bf16_pairs(wd):
    lo = pltpu.bitcast(wd << 16, jnp.float32)
    hi = pltpu.bitcast(wd & jnp.uint32(0xFFFF0000), jnp.float32)
    return jnp.concatenate([lo, hi], axis=-1)


def _conv_b_chunk(ub_buf, cbw_ref, n_i, r0, rc, c0):
    cs = slice(c0, c0 + LANES)
    lead = HIST_B_ROWS - (CONV_B_W - 1)
    acc = None
    for b in range(SUBLANES):
        rows = rc if b == 0 else rc + SUBLANES
        q = None
        for a in range((HIST_B_ROWS + SUBLANES) // SUBLANES):
            k = SUBLANES * a + b - lead
            if k < 0 or k >= CONV_B_W:
                continue
            start = r0 + SUBLANES * a
            term = cbw_ref[k:k + 1, cs] * ub_buf[n_i, start:start + rows, cs]
            q = term if q is None else q + term
        part = q if b == 0 else q[b:b + rc]
        acc = part if acc is None else acc + part
    return acc


def _mixer_kernel(x_ref, hista_ref, histb_ref, cnt0_ref, tri_ref,
                  gmix_ref, win_ref, caw_ref, cbw_ref, cbb_ref, lng_ref, lnb_ref,
                  ga_ref, gb_ref, wout_ref, gffn_ref, wrt_ref, br_ref,
                  h_ref, hnp_ref, idx_ref, gate_ref, rank_ref, cnt_ref, newa_ref, newb_ref,
                  ua_buf, ub_buf, cb_buf, cnt_acc, *, nb, ts, d_a, d_b, n_exp):
    b = pl.program_id(0)
    s = pl.program_id(1)
    m = nb * ts
    d = x_ref.shape[-1]

    @pl.when(s == 0)
    def _():
        ua_buf[:, 0:HIST_A_ROWS, :] = hista_ref[...]
        ub_buf[:, 0:HIST_B_ROWS, :] = histb_ref[...]

    @pl.when(s != 0)
    def _():
        ua_buf[:, 0:HIST_A_ROWS, :] = ua_buf[:, ts:ts + HIST_A_ROWS, :]
        ub_buf[:, 0:HIST_B_ROWS, :] = ub_buf[:, ts:ts + HIST_B_ROWS, :]

    @pl.when((b == 0) & (s == 0))
    def _():
        cnt_acc[...] = cnt0_ref[...]

    x = x_ref[...].reshape(m, d)
    n = _rms(x, gmix_ref[...]).astype(jnp.bfloat16)
    def proj(c0, width):
        return jnp.dot(n, win_ref[:, c0:c0 + width], preferred_element_type=jnp.float32)

    sec = min(d_b, PROJ_SECTION)
    for c0 in range(0, d_b, sec):
        v_b = proj(3 * d_a + c0, sec)
        g_b = proj(3 * d_a + d_b + c0, sec)
        ub_buf[:, HIST_B_ROWS:HIST_B_ROWS + ts, c0:c0 + sec] = (v_b * jax.nn.sigmoid(g_b)).reshape(nb, ts, sec)

    gate_c = proj(d_a, d_a)
    xt = proj(2 * d_a, d_a)
    gate_b = proj(0, d_a)
    ua_buf[:, HIST_A_ROWS:HIST_A_ROWS + ts, :] = (gate_c * xt).reshape(nb, ts, d_a)
    conv_a = jnp.zeros((nb, ts, d_a), jnp.float32)
    for k in range(CONV_A_W):
        off = HIST_A_ROWS - (CONV_A_W - 1) + k
        conv_a = conv_a + caw_ref[k:k + 1, :] * ua_buf[:, off:off + ts, :]
    y_a = gate_b * conv_a.reshape(m, d_a)

    mix_a = jnp.dot(_rms(y_a, ga_ref[...]).astype(jnp.bfloat16), wout_ref[0:d_a, :],
                    preferred_element_type=jnp.float32)

    rc = min(ts, CONV_ROW_CHUNK)
    chunks = [(n_i, r0, c0) for n_i in range(nb) for r0 in range(0, ts, rc) for c0 in range(0, d_b, LANES)]
    anchors = {len(chunks) * 2 // 8: gate_c, len(chunks) * 3 // 8: xt, len(chunks) * 5 // 8: gate_b,
               len(chunks) - 1: mix_a}
    for ci, (n_i, r0, c0) in enumerate(chunks):
        acc = _conv_b_chunk(ub_buf, cbw_ref, n_i, r0, rc, c0)
        if ci in anchors:
            acc = (acc.reshape(rc // SUBLANES, SUBLANES, LANES) + _zero_tile_from(anchors[ci])[None]
                   ).reshape(rc, LANES)
        cb_buf[n_i, r0:r0 + rc, c0:c0 + LANES] = acc
    cb = cb_buf[...].reshape(m, d_b) + cbb_ref[...]
    mu = jnp.mean(cb, axis=-1, keepdims=True)
    xc = cb - mu
    var = jnp.mean(xc * xc, axis=-1, keepdims=True)
    ln = xc * lax.rsqrt(var + NORM_EPS) * lng_ref[...] + lnb_ref[...]
    y_b = ln * jax.nn.sigmoid(ln)

    mix_b = jnp.dot(_rms(y_b, gb_ref[...]).astype(jnp.bfloat16), wout_ref[d_a:d_a + d_b, :],
                    preferred_element_type=jnp.float32)
    h = x + (mix_a + mix_b)
    h_ref[...] = h.reshape(nb, ts, d)

    hn = _rms(h, gffn_ref[...])
    hnp_ref[...] = _pack_bf16_pairs(hn)

    logits = lax.dot_general(wrt_ref[...], hn.astype(jnp.bfloat16), (((1,), (1,)), ((), ())),
                             preferred_element_type=jnp.float32) + br_ref[...]
    e_iota = lax.broadcasted_iota(jnp.int32, (n_exp, m), 0)
    cur = logits
    vals, idxs, sels = [], [], []
    for _ in range(TOP_K):
        mx = jnp.max(cur, axis=0, keepdims=True)
        ix = jnp.min(jnp.where(cur == mx, e_iota, n_exp), axis=0, keepdims=True)
        sel = e_iota == ix
        vals.append(mx)
        idxs.append(ix)
        sels.append(sel)
        cur = jnp.where(sel, -jnp.inf, cur)
    exps = [jnp.exp(v - vals[0]) for v in vals]
    denom = exps[0] + exps[1] + exps[2] + exps[3]
    gates = [e / denom for e in exps]

    onehot = jnp.zeros((n_exp, m), jnp.float32)
    for sel in sels:
        onehot = onehot + sel.astype(jnp.float32)
    before = jnp.dot(onehot.astype(jnp.bfloat16), tri_ref[...],
                     preferred_element_type=jnp.float32) + cnt_acc[...]
    ranks = [jnp.sum(jnp.where(sel, before, 0.0), axis=0, keepdims=True) for sel in sels]
    new_cnt = cnt_acc[...] + jnp.sum(onehot, axis=1, keepdims=True)
    cnt_acc[...] = new_cnt
    cnt_ref[...] = new_cnt

    idx_ref[...] = jnp.concatenate(idxs, axis=0)
    gate_ref[...] = jnp.concatenate(gates + [jnp.zeros((SUBLANES - TOP_K, m), jnp.float32)], axis=0).T
    rank_ref[...] = jnp.concatenate(ranks, axis=0).astype(jnp.int32)

    @pl.when(s == pl.num_programs(1) - 1)
    def _():
        newa_ref[...] = ua_buf[:, ts + HIST_A_ROWS - (CONV_A_W - 1):ts + HIST_A_ROWS, :]
        newb_ref[...] = ub_buf[:, ts + HIST_B_ROWS - (CONV_B_W - 1):ts + HIST_B_ROWS, :]


def _mixer(x, hist_a, hist_b, cnt0, params, *, nb, ts, b0=0, bsz=None):
    (gmix, win, caw, cbw, cbb, lng, lnb, ga, gb, wout, gffn, wrt, br) = params
    _, seq, d = x.shape
    bsz = x.shape[0] if bsz is None else bsz
    d_a, d_b = caw.shape[-1], cbw.shape[-1]
    n_exp = wrt.shape[0]
    m = nb * ts
    n_tok = bsz * seq
    g0 = b0 // nb
    grid = (bsz // nb, seq // ts)
    tri = (jnp.arange(m)[:, None] < jnp.arange(m)[None, :]).astype(jnp.bfloat16)

    def const(shape):
        return pl.BlockSpec(shape, lambda i, j: (0,) * len(shape))

    tok_map = lambda i, j: (0, i * (seq // ts) + j)
    kern = functools.partial(_mixer_kernel, nb=nb, ts=ts, d_a=d_a, d_b=d_b, n_exp=n_exp)
    out_shape = (
        jax.ShapeDtypeStruct((bsz, seq, d), jnp.float32),
        jax.ShapeDtypeStruct((n_tok, d // 2), jnp.uint32),
        jax.ShapeDtypeStruct((TOP_K, n_tok), jnp.int32),
        jax.ShapeDtypeStruct((n_tok, SUBLANES), jnp.float32),
        jax.ShapeDtypeStruct((TOP_K, n_tok), jnp.int32),
        jax.ShapeDtypeStruct((n_exp, 1), jnp.float32),
        jax.ShapeDtypeStruct((bsz, CONV_A_W - 1, d_a), jnp.float32),
        jax.ShapeDtypeStruct((bsz, CONV_B_W - 1, d_b), jnp.float32),
    )
    return pl.pallas_call(
        kern,
        grid=grid,
        in_specs=[
            pl.BlockSpec((nb, ts, d), lambda i, j: (g0 + i, j, 0)),
            pl.BlockSpec((nb, HIST_A_ROWS, d_a), lambda i, j: (g0 + i, 0, 0)),
            pl.BlockSpec((nb, HIST_B_ROWS, d_b), lambda i, j: (g0 + i, 0, 0)),
            const((n_exp, 1)),
            const((m, m)),
            const((1, d)), const(win.shape), const(caw.shape), const(cbw.shape), const((1, d_b)),
            const((1, d_b)), const((1, d_b)), const((1, d_a)), const((1, d_b)), const(wout.shape),
            const((1, d)), const(wrt.shape), const((n_exp, 1)),
        ],
        out_specs=(
            pl.BlockSpec((nb, ts, d), lambda i, j: (i, j, 0)),
            pl.BlockSpec((m, d // 2), lambda i, j: (i * (seq // ts) + j, 0)),
            pl.BlockSpec((TOP_K, m), tok_map),
            pl.BlockSpec((m, SUBLANES), lambda i, j: (i * (seq // ts) + j, 0)),
            pl.BlockSpec((TOP_K, m), tok_map),
            pl.BlockSpec((n_exp, 1), lambda i, j: (0, 0)),
            pl.BlockSpec((nb, CONV_A_W - 1, d_a), lambda i, j: (i, 0, 0)),
            pl.BlockSpec((nb, CONV_B_W - 1, d_b), lambda i, j: (i, 0, 0)),
        ),
        out_shape=out_shape,
        scratch_shapes=[
            pltpu.VMEM((nb, HIST_A_ROWS + ts, d_a), jnp.float32),
            pltpu.VMEM((nb, HIST_B_ROWS + ts, d_b), jnp.float32),
            pltpu.VMEM((nb, ts, d_b), jnp.float32),
            pltpu.VMEM((n_exp, 1), jnp.float32),
        ],
        compiler_params=pltpu.CompilerParams(
            dimension_semantics=("arbitrary", "arbitrary"), vmem_limit_bytes=VMEM_LIMIT),
        name="mixer",
    )(x, hist_a, hist_b, cnt0, tri, gmix, win, caw, cbw, cbb, lng, lnb, ga, gb, wout, gffn, wrt, br)


def _positions_kernel(start_ref, *refs):
    n_parts = len(refs) // 3
    for p in range(n_parts):
        idx = refs[2 * p][...]
        pos = refs[2 * p + 1][...]
        for e in range(start_ref.shape[0]):
            pos = pos + jnp.where(idx == e, start_ref[e], 0)
        refs[2 * n_parts + p][...] = pos


def _positions(pad_start, parts):
    n_0 = parts[0][0].shape[1]
    tile = min(n_0, POS_TILE)
    specs = [pl.BlockSpec((TOP_K, tile), lambda i: (0, i))]
    specs += [pl.BlockSpec((TOP_K, idx.shape[1]), lambda i: (0, 0)) for idx, _ in parts[1:]]
    in_specs = [pl.BlockSpec(memory_space=pltpu.SMEM)]
    args = [pad_start]
    for spec, (idx, rank) in zip(specs, parts):
        in_specs += [spec, spec]
        args += [idx, rank]
    out = pl.pallas_call(
        _positions_kernel,
        grid=(n_0 // tile,),
        in_specs=in_specs,
        out_specs=tuple(specs),
        out_shape=tuple(jax.ShapeDtypeStruct(idx.shape, jnp.int32) for idx, _ in parts),
        compiler_params=pltpu.CompilerParams(dimension_semantics=("arbitrary",)),
        name="positions",
    )(*args)
    return list(out)


def _sc_mesh():
    return plsc.VectorSubcoreMesh(core_axis_name="c", subcore_axis_name="s",
                                  num_cores=SC_CORES, num_subcores=SC_SUBCORES)


def _sc_worker_id():
    return lax.axis_index("s") * SC_CORES + lax.axis_index("c")


def _sc_chunks(n_tok, wid):
    n_chunks = n_tok // SC_CHUNK
    assert n_tok % SC_CHUNK == 0
    if n_chunks % SC_WORKERS == 0:
        per_w = n_chunks // SC_WORKERS
        return None, per_w, lambda i: (wid * per_w + i) * SC_CHUNK
    assert n_chunks <= SC_WORKERS
    return wid < n_chunks, 1, lambda i: wid * SC_CHUNK


def _sc_for_each_chunk(n_tok, fn):
    wid = _sc_worker_id()
    pred, trips, base = _sc_chunks(n_tok, wid)

    def run():
        @pl.loop(0, trips)
        def _(i):
            fn(pl.multiple_of(base(i), SC_CHUNK))

    if pred is None:
        run()
    else:
        pl.when(pred)(run)


def _sc_dispatch(parts, n_rows):
    half = parts[0][1].shape[1]
    n_parts = len(parts)

    @functools.partial(
        pl.kernel, mesh=_sc_mesh(),
        out_type=jax.ShapeDtypeStruct((n_rows, half), jnp.uint32),
        scratch_types=[pltpu.VMEM((TOP_K, SC_CHUNK), jnp.int32),
                       pltpu.VMEM((SC_CHUNK, half), jnp.uint32),
                       pltpu.SemaphoreType.DMA],
        compiler_params=pltpu.CompilerParams(use_tc_tiling_on_sc=True),
        name="sc_dispatch")
    def k(*refs):
        xs_hbm, idx_v, rows_v, sem = refs[2 * n_parts:]

        def move(pos_hbm, src_hbm):
            def fn(base):
                pltpu.sync_copy(pos_hbm.at[:, pl.ds(base, SC_CHUNK)], idx_v)
                pltpu.sync_copy(src_hbm.at[pl.ds(base, SC_CHUNK)], rows_v)
                copies = [pltpu.async_copy(rows_v, xs_hbm.at[idx_v.at[j]], sem) for j in range(TOP_K)]
                for c in copies:
                    c.wait()
            return fn

        for p, (_, hnp) in enumerate(parts):
            _sc_for_each_chunk(hnp.shape[0], move(refs[2 * p], refs[2 * p + 1]))

    return k(*[a for part in parts for a in part])


def _sc_gather(pos_parts, rows):
    half = rows.shape[1]
    n_parts = len(pos_parts)

    @functools.partial(
        pl.kernel, mesh=_sc_mesh(),
        out_type=tuple(jax.ShapeDtypeStruct((TOP_K, p.shape[1], half), jnp.uint32) for p in pos_parts),
        scratch_types=[pltpu.VMEM((TOP_K, SC_CHUNK), jnp.int32),
                       pltpu.VMEM((SC_CHUNK, half), jnp.uint32),
                       pltpu.SemaphoreType.DMA],
        compiler_params=pltpu.CompilerParams(use_tc_tiling_on_sc=True),
        name="sc_gather")
    def k(*refs):
        pos_hbms, rows_hbm = refs[:n_parts], refs[n_parts]
        out_hbms = refs[n_parts + 1:2 * n_parts + 1]
        idx_v, buf_v, sem = refs[2 * n_parts + 1:]

        def move(pos_hbm, out_hbm):
            def fn(base):
                pltpu.sync_copy(pos_hbm.at[:, pl.ds(base, SC_CHUNK)], idx_v)
                for j in range(TOP_K):
                    pltpu.async_copy(rows_hbm.at[idx_v.at[j]], buf_v, sem).wait()
                    pltpu.sync_copy(buf_v, out_hbm.at[j, pl.ds(base, SC_CHUNK)])
            return fn

        for pos_hbm, out_hbm, p in zip(pos_hbms, out_hbms, pos_parts):
            _sc_for_each_chunk(p.shape[1], move(pos_hbm, out_hbm))

    out = k(*pos_parts, rows)
    return out if isinstance(out, (tuple, list)) else (out,)


def _experts_kernel(blk_exp_ref, first_ref, slot_ref, next_ref, n_active_ref,
                    xs_ref, wgu_hbm, bgu_ref, wd_hbm, bd_ref, out_ref,
                    wgu_f32, wd_f32, wgu_bf, wd_bf, sems):
    d_ff = wd_bf.shape[0]

    def fetch(e, sl):
        return (pltpu.make_async_copy(wgu_hbm.at[e], wgu_f32.at[sl], sems.at[0, sl]),
                pltpu.make_async_copy(wd_hbm.at[e], wd_f32.at[sl], sems.at[1, sl]))

    def load_weights_if_first(i):
        sl = slot_ref[i]
        e = blk_exp_ref[i]

        @pl.when(first_ref[i] == 1)
        def _():
            @pl.when(i == 0)
            def _():
                for c in fetch(e, sl):
                    c.start()

            for c in fetch(e, sl):
                c.wait()
            wgu_bf[...] = wgu_f32[sl].astype(jnp.bfloat16)
            wd_bf[...] = wd_f32[sl].astype(jnp.bfloat16)

            @pl.when(next_ref[i] >= 0)
            def _():
                for c in fetch(next_ref[i], 1 - sl):
                    c.start()

    def compute(rows, e):
        x = _unpack_bf16_pairs(xs_ref[rows, :]).astype(jnp.bfloat16)
        gu = jnp.dot(x, wgu_bf[...], preferred_element_type=jnp.float32) + bgu_ref[e]
        g = jnp.minimum(gu[:, :d_ff], SWIGLU_LIMIT)
        u = jnp.clip(gu[:, d_ff:], -SWIGLU_LIMIT, SWIGLU_LIMIT)
        act = g * jax.nn.sigmoid(SWIGLU_ALPHA * g) * (u + 1.0)
        o = jnp.dot(act.astype(jnp.bfloat16), wd_bf[...], preferred_element_type=jnp.float32) + bd_ref[e]
        out_ref[rows, :] = _pack_bf16_pairs(o)

    i0 = pl.program_id(0) * BLOCKS_PER_STEP
    n_active = n_active_ref[0]

    def run(sub0, n_sub):
        first = i0 + sub0
        rows = slice(sub0 * MOE_BLOCK, (sub0 + n_sub) * MOE_BLOCK)
        if n_sub == 1:
            @pl.when(first < n_active)
            def _():
                load_weights_if_first(first)
                compute(rows, blk_exp_ref[first])

            @pl.when(first >= n_active)
            def _():
                out_ref[rows, :] = jnp.zeros((MOE_BLOCK, out_ref.shape[1]), out_ref.dtype)
            return

        same = first + n_sub - 1 < n_active
        for sub in range(1, n_sub):
            same = same & (blk_exp_ref[first + sub] == blk_exp_ref[first])

        @pl.when(same)
        def _():
            load_weights_if_first(first)
            compute(rows, blk_exp_ref[first])

        @pl.when(jnp.logical_not(same))
        def _():
            run(sub0, n_sub // 2)
            run(sub0 + n_sub // 2, n_sub // 2)

    run(0, BLOCKS_PER_STEP)


def _experts(plan, xs, wgu, bgu, wd, bd):
    blk_exp, first, slot, nxt, n_active = plan
    n_rows, half = xs.shape
    n_exp, d, two_f = wgu.shape
    d_ff = wd.shape[1]
    n_blocks = n_rows // MOE_BLOCK
    step_rows = MOE_BLOCK * BLOCKS_PER_STEP
    assert n_rows % step_rows == 0
    grid_spec = pltpu.PrefetchScalarGridSpec(
        num_scalar_prefetch=5,
        grid=(n_blocks // BLOCKS_PER_STEP,),
        in_specs=[
            pl.BlockSpec((step_rows, half), lambda i, *_: (i, 0)),
            pl.BlockSpec(memory_space=pl.ANY),
            pl.BlockSpec((n_exp, 1, two_f), lambda i, *_: (0, 0, 0)),
            pl.BlockSpec(memory_space=pl.ANY),
            pl.BlockSpec((n_exp, 1, d), lambda i, *_: (0, 0, 0)),
        ],
        out_specs=pl.BlockSpec((step_rows, half), lambda i, *_: (i, 0)),
        scratch_shapes=[
            pltpu.VMEM((2, d, two_f), jnp.float32),
            pltpu.VMEM((2, d_ff, d), jnp.float32),
            pltpu.VMEM((d, two_f), jnp.bfloat16),
            pltpu.VMEM((d_ff, d), jnp.bfloat16),
            pltpu.SemaphoreType.DMA((2, 2)),
        ],
    )
    return pl.pallas_call(
        _experts_kernel,
        grid_spec=grid_spec,
        out_shape=jax.ShapeDtypeStruct((n_rows, half), jnp.uint32),
        compiler_params=pltpu.CompilerParams(
            dimension_semantics=("arbitrary",), vmem_limit_bytes=VMEM_LIMIT),
        name="experts",
    )(blk_exp, first, slot, nxt, n_active, xs, wgu, bgu.reshape(n_exp, 1, two_f), wd,
      bd.reshape(n_exp, 1, d))


def _combine_kernel(rows_ref, gate_ref, h_ref, gfin_ref, *rest):
    y_ref = rest[-1]
    acc = h_ref[...]
    for k in range(TOP_K):
        acc = acc + gate_ref[:, k:k + 1] * _unpack_bf16_pairs(rows_ref[k])
    y_ref[...] = _rms(acc, gfin_ref[...])


def _combine(rows4, gate_t, h2d, gfin, *, tok0=0, y_prev=None, y_rows=None, y_tok0=0):
    d = h2d.shape[1]
    n, half = rows4.shape[1:]
    n_tok = h2d.shape[0] if y_rows is None else y_rows
    tile = min(n, COMBINE_TILE)
    t0 = tok0 // tile
    y0 = y_tok0 // tile
    in_specs = [
        pl.BlockSpec((TOP_K, tile, half), lambda i: (0, i, 0)),
        pl.BlockSpec((tile, gate_t.shape[1]), lambda i: (t0 + i, 0)),
        pl.BlockSpec((tile, d), lambda i: (t0 + i, 0)),
        pl.BlockSpec((1, d), lambda i: (0, 0)),
    ]
    args = [rows4, gate_t, h2d, gfin]
    aliases = {}
    if y_prev is not None:
        in_specs.append(pl.BlockSpec(memory_space=pl.ANY))
        args.append(y_prev)
        aliases = {4: 0}
    return pl.pallas_call(
        _combine_kernel,
        grid=(n // tile,),
        in_specs=in_specs,
        out_specs=pl.BlockSpec((tile, d), lambda i: (y0 + i, 0)),
        out_shape=jax.ShapeDtypeStruct((n_tok, d), jnp.float32),
        input_output_aliases=aliases,
        compiler_params=pltpu.CompilerParams(
            dimension_semantics=("arbitrary",), vmem_limit_bytes=VMEM_LIMIT),
        name="combine",
    )(*args)


def _pad_hist(hist, rows):
    return jnp.pad(hist, ((0, 0), (rows - hist.shape[1], 0), (0, 0)))


def kernel(x_prompt, x_sample, state_conv_a, state_conv_b, norm_mix_g, w_in, conv_a_w, conv_b_w, conv_b_b, conv_ln_g, conv_ln_b, out_norm_a_g, out_norm_b_g, w_out, norm_ffn_g, w_router, b_router, w_gate_up, b_gate_up, w_down, b_down, final_norm_g):
    depth = w_in.shape[0]
    assert depth == 1, "single-layer trunk"
    bf16 = jnp.bfloat16
    bsz, seq, d = x_prompt.shape
    dec_b, dec_s, _ = x_sample.shape
    d_a, d_b = conv_a_w.shape[-1], conv_b_w.shape[-1]
    n_exp = w_router.shape[-1]

    params = (norm_mix_g[0][None], w_in[0].astype(bf16), conv_a_w[0], conv_b_w[0], conv_b_b[0][None],
              conv_ln_g[0][None], conv_ln_b[0][None], out_norm_a_g[0][None], out_norm_b_g[0][None],
              w_out[0].astype(bf16), norm_ffn_g[0][None], w_router[0].T.astype(bf16),
              b_router[0][:, None])

    zero_a = jnp.zeros((bsz, HIST_A_ROWS, d_a), jnp.float32)
    zero_b = jnp.zeros((bsz, HIST_B_ROWS, d_b), jnp.float32)
    cnt0 = jnp.zeros((n_exp, 1), jnp.float32)
    gfin = final_norm_g[None]
    ts = min(seq, MIXER_ROWS)
    n_p = bsz * seq

    sizes = [bsz * f // sum(GROUP_SPLIT) for f in GROUP_SPLIT]
    if bsz % sum(GROUP_SPLIT) != 0:
        sizes = [bsz]
    groups = []
    tok_start = []
    for g, per_b in enumerate(sizes):
        b0 = sum(sizes[:g])
        tok_start.append(b0 * seq)
        h, hnp, idx, gate, rank, cnt, na, nb_ = _mixer(
            x_prompt, zero_a, zero_b, cnt0, params, nb=1, ts=ts, b0=b0, bsz=per_b)
        parts = [dict(h=h.reshape(per_b * seq, d), hnp=hnp, idx=idx, gate_t=gate, rank=rank)]
        states = [(na, nb_)]
        if g == len(sizes) - 1:
            h, hnp, idx, gate, rank, cnt, na_s, nb_s = _mixer(
                x_sample, _pad_hist(state_conv_a[0], HIST_A_ROWS), _pad_hist(state_conv_b[0], HIST_B_ROWS),
                cnt, params, nb=dec_b, ts=dec_s)
            parts.append(dict(h=h.reshape(dec_b * dec_s, d), hnp=hnp, idx=idx, gate_t=gate, rank=rank))
        n_pairs = sum(p["hnp"].shape[0] for p in parts) * TOP_K
        plan, pad_start, n_blocks = _block_plan(cnt[:, 0].astype(jnp.int32), n_pairs, n_exp)
        pos = _positions(pad_start, [(p["idx"], p["rank"]) for p in parts])
        xs = _sc_dispatch([(q, p["hnp"]) for q, p in zip(pos, parts)], n_blocks * MOE_BLOCK)
        groups.append(dict(parts=parts, pos=pos, plan=plan, xs=xs, states=states))

    for grp in groups:
        grp["rows"] = _experts(grp["plan"], grp["xs"], w_gate_up[0], b_gate_up[0], w_down[0], b_down[0])

    gathered = []
    for g, grp in enumerate(groups):
        pos0 = grp["pos"][0]
        n_g = pos0.shape[1]
        want = 1 if (g == len(groups) - 1 and len(groups) > 1) else COMBINE_CHUNKS
        n_chunks = want if n_g % (want * SC_CHUNK * SC_WORKERS) == 0 else 1
        per = n_g // n_chunks
        for c in range(n_chunks):
            pos_parts = [pos0[:, c * per:(c + 1) * per]] + (grp["pos"][1:] if c == 0 else [])
            gathered.append((g, c * per, _sc_gather(pos_parts, grp["rows"])))

    y_p = None
    for g, tok0, got in sorted(gathered, key=lambda t: (t[0] != len(groups) - 1, t[0], t[1])):
        grp = groups[g]
        main = grp["parts"][0]
        if len(got) > 1:
            smp = grp["parts"][1]
            y_s = _combine(got[1], smp["gate_t"], smp["h"], gfin).reshape(dec_b, dec_s, d)
        y_p = _combine(got[0], main["gate_t"], main["h"], gfin, tok0=tok0,
                       y_prev=y_p, y_rows=n_p, y_tok0=tok_start[g] + tok0)
    y_p = y_p.reshape(bsz, seq, d)

    na_p = jnp.concatenate([grp["states"][0][0] for grp in groups], axis=0)
    nb_p = jnp.concatenate([grp["states"][0][1] for grp in groups], axis=0)
    return (y_p, y_s, na_p[None], nb_p[None], na_s[None], nb_s[None])


def _block_plan(counts, n_pairs, n_exp):
    padded = (counts + MOE_BLOCK - 1) // MOE_BLOCK * MOE_BLOCK
    pad_end = jnp.cumsum(padded)
    pad_start = pad_end - padded
    n_blocks = -(-n_pairs // MOE_BLOCK) + n_exp
    n_blocks = -(-n_blocks // BLOCKS_PER_STEP) * BLOCKS_PER_STEP
    blk_start = jnp.arange(n_blocks, dtype=jnp.int32) * MOE_BLOCK
    blk_exp = jnp.minimum(jnp.sum(blk_start[:, None] >= pad_end[None, :], axis=1),
                          n_exp - 1).astype(jnp.int32)
    n_active = (pad_end[-1:] // MOE_BLOCK).astype(jnp.int32)
    e_ids = jnp.arange(n_exp, dtype=jnp.int32)
    has = padded > 0
    slot_e = (jnp.cumsum(has.astype(jnp.int32)) - 1) % 2
    later = jnp.where(has, e_ids, n_exp)
    next_e = jnp.concatenate([lax.cummin(later, reverse=True)[1:], jnp.full((1,), n_exp, jnp.int32)])
    next_e = jnp.where(next_e >= n_exp, -1, next_e)
    blk_onehot = blk_exp[:, None] == e_ids[None, :]

    def per_block(table):
        return jnp.sum(jnp.where(blk_onehot, table[None, :], 0), axis=1).astype(jnp.int32)

    blk_first = ((blk_start == per_block(pad_start)) & (blk_start < pad_end[-1])).astype(jnp.int32)
    plan = (blk_exp, blk_first, per_block(slot_e), per_block(next_e), n_active)
    return plan, pad_start.astype(jnp.int32), n_blocks
```

```python
import functools

import jax
import jax.numpy as jnp
from jax import lax
from jax.experimental import pallas as pl
from jax.experimental.pallas import tpu as pltpu
from jax.experimental.pallas import tpu_sc as plsc

SC_CORES = 2
SC_SUBCORES = 16
SC_WORKERS = SC_CORES * SC_SUBCORES
SC_CHUNK = 128
CONV_A_W = 3
CONV_B_W = 31
TOP_K = 4
SWIGLU_LIMIT = 7.0
SWIGLU_ALPHA = 1.702
NORM_EPS = 1e-5
MOE_BLOCK = 256
BLOCKS_PER_STEP = 4
SUBLANES = 8
LANES = 128
CONV_ROW_CHUNK = 128
HIST_A_ROWS = 8
HIST_B_ROWS = 32
MIXER_ROWS = 1024
PROJ_SECTION = 256
COMBINE_TILE = 1024
POS_TILE = 4096
GROUP_SPLIT = (1, 3)
COMBINE_CHUNKS = 2
VMEM_LIMIT = 56 * 1024 * 1024


def _rms(x, g):
    return x * lax.rsqrt(jnp.mean(x * x, axis=-1, keepdims=True) + NORM_EPS) * g


def _zero_tile_from(v):
    bits = pltpu.bitcast(v[0:SUBLANES, 0:LANES], jnp.uint32)
    return pltpu.bitcast((bits >> 16) >> 16, jnp.float32)


def _pack_bf16_pairs(a):
    w = a.shape[-1] // 2
    lo = pltpu.bitcast(a[:, :w].astype(jnp.bfloat16).astype(jnp.float32), jnp.uint32)
    hi = pltpu.bitcast(a[:, w:].astype(jnp.bfloat16).astype(jnp.float32), jnp.uint32)
    return (lo >> 16) | (hi & jnp.uint32(0xFFFF0000))


def _unpack_bf16_pairs(wd):
    lo = pltpu.bitcast(wd << 16, jnp.float32)
    hi = pltpu.bitcast(wd & jnp.uint32(0xFFFF0000), jnp.float32)
    return jnp.concatenate([lo, hi], axis=-1)


def _conv_b_chunk(ub_buf, cbw_ref, n_i, r0, rc, c0):
    cs = slice(c0, c0 + LANES)
    lead = HIST_B_ROWS - (CONV_B_W - 1)
    acc = None
    for b in range(SUBLANES):
        rows = rc if b == 0 else rc + SUBLANES
        q = None
        for a in range((HIST_B_ROWS + SUBLANES) // SUBLANES):
            k = SUBLANES * a + b - lead
            if k < 0 or k >= CONV_B_W:
                continue
            start = r0 + SUBLANES * a
            term = cbw_ref[k:k + 1, cs] * ub_buf[n_i, start:start + rows, cs]
            q = term if q is None else q + term
        part = q if b == 0 else q[b:b + rc]
        acc = part if acc is None else acc + part
    return acc


def _mixer_kernel(x_ref, hista_ref, histb_ref, cnt0_ref, tri_ref,
                  gmix_ref, win_ref, caw_ref, cbw_ref, cbb_ref, lng_ref, lnb_ref,
                  ga_ref, gb_ref, wout_ref, gffn_ref, wrt_ref, br_ref,
                  h_ref, hnp_ref, idx_ref, gate_ref, rank_ref, cnt_ref, newa_ref, newb_ref,
                  ua_buf, ub_buf, cb_buf, cnt_acc, *, nb, ts, d_a, d_b, n_exp):
    b = pl.program_id(0)
    s = pl.program_id(1)
    m = nb * ts
    d = x_ref.shape[-1]

    @pl.when(s == 0)
    def _():
        ua_buf[:, 0:HIST_A_ROWS, :] = hista_ref[...]
        ub_buf[:, 0:HIST_B_ROWS, :] = histb_ref[...]

    @pl.when(s != 0)
    def _():
        ua_buf[:, 0:HIST_A_ROWS, :] = ua_buf[:, ts:ts + HIST_A_ROWS, :]
        ub_buf[:, 0:HIST_B_ROWS, :] = ub_buf[:, ts:ts + HIST_B_ROWS, :]

    @pl.when((b == 0) & (s == 0))
    def _():
        cnt_acc[...] = cnt0_ref[...]

    x = x_ref[...].reshape(m, d)
    n = _rms(x, gmix_ref[...]).astype(jnp.bfloat16)
    def proj(c0, width):
        return jnp.dot(n, win_ref[:, c0:c0 + width], preferred_element_type=jnp.float32)

    sec = min(d_b, PROJ_SECTION)
    for c0 in range(0, d_b, sec):
        v_b = proj(3 * d_a + c0, sec)
        g_b = proj(3 * d_a + d_b + c0, sec)
        ub_buf[:, HIST_B_ROWS:HIST_B_ROWS + ts, c0:c0 + sec] = (v_b * jax.nn.sigmoid(g_b)).reshape(nb, ts, sec)

    gate_c = proj(d_a, d_a)
    xt = proj(2 * d_a, d_a)
    gate_b = proj(0, d_a)
    ua_buf[:, HIST_A_ROWS:HIST_A_ROWS + ts, :] = (gate_c * xt).reshape(nb, ts, d_a)
    conv_a = jnp.zeros((nb, ts, d_a), jnp.float32)
    for k in range(CONV_A_W):
        off = HIST_A_ROWS - (CONV_A_W - 1) + k
        conv_a = conv_a + caw_ref[k:k + 1, :] * ua_buf[:, off:off + ts, :]
    y_a = gate_b * conv_a.reshape(m, d_a)

    mix_a = jnp.dot(_rms(y_a, ga_ref[...]).astype(jnp.bfloat16), wout_ref[0:d_a, :],
                    preferred_element_type=jnp.float32)

    rc = min(ts, CONV_ROW_CHUNK)
    chunks = [(n_i, r0, c0) for n_i in range(nb) for r0 in range(0, ts, rc) for c0 in range(0, d_b, LANES)]
    anchors = {len(chunks) * 2 // 8: gate_c, len(chunks) * 3 // 8: xt, len(chunks) * 5 // 8: gate_b,
               len(chunks) - 1: mix_a}
    for ci, (n_i, r0, c0) in enumerate(chunks):
        acc = _conv_b_chunk(ub_buf, cbw_ref, n_i, r0, rc, c0)
        if ci in anchors:
            acc = (acc.reshape(rc // SUBLANES, SUBLANES, LANES) + _zero_tile_from(anchors[ci])[None]
                   ).reshape(rc, LANES)
        cb_buf[n_i, r0:r0 + rc, c0:c0 + LANES] = acc
    cb = cb_buf[...].reshape(m, d_b) + cbb_ref[...]
    mu = jnp.mean(cb, axis=-1, keepdims=True)
    xc = cb - mu
    var = jnp.mean(xc * xc, axis=-1, keepdims=True)
    ln = xc * lax.rsqrt(var + NORM_EPS) * lng_ref[...] + lnb_ref[...]
    y_b = ln * jax.nn.sigmoid(ln)

    mix_b = jnp.dot(_rms(y_b, gb_ref[...]).astype(jnp.bfloat16), wout_ref[d_a:d_a + d_b, :],
                    preferred_element_type=jnp.float32)
    h = x + (mix_a + mix_b)
    h_ref[...] = h.reshape(nb, ts, d)

    hn = _rms(h, gffn_ref[...])
    hnp_ref[...] = _pack_bf16_pairs(hn)

    logits = lax.dot_general(wrt_ref[...], hn.astype(jnp.bfloat16), (((1,), (1,)), ((), ())),
                             preferred_element_type=jnp.float32) + br_ref[...]
    e_iota = lax.broadcasted_iota(jnp.int32, (n_exp, m), 0)
    cur = logits
    vals, idxs, sels = [], [], []
    for _ in range(TOP_K):
        mx = jnp.max(cur, axis=0, keepdims=True)
        ix = jnp.min(jnp.where(cur == mx, e_iota, n_exp), axis=0, keepdims=True)
        sel = e_iota == ix
        vals.append(mx)
        idxs.append(ix)
        sels.append(sel)
        cur = jnp.where(sel, -jnp.inf, cur)
    exps = [jnp.exp(v - vals[0]) for v in vals]
    denom = exps[0] + exps[1] + exps[2] + exps[3]
    gates = [e / denom for e in exps]

    onehot = jnp.zeros((n_exp, m), jnp.float32)
    for sel in sels:
        onehot = onehot + sel.astype(jnp.float32)
    before = jnp.dot(onehot.astype(jnp.bfloat16), tri_ref[...],
                     preferred_element_type=jnp.float32) + cnt_acc[...]
    ranks = [jnp.sum(jnp.where(sel, before, 0.0), axis=0, keepdims=True) for sel in sels]
    new_cnt = cnt_acc[...] + jnp.sum(onehot, axis=1, keepdims=True)
    cnt_acc[...] = new_cnt
    cnt_ref[...] = new_cnt

    idx_ref[...] = jnp.concatenate(idxs, axis=0)
    gate_ref[...] = jnp.concatenate(gates + [jnp.zeros((SUBLANES - TOP_K, m), jnp.float32)], axis=0).T
    rank_ref[...] = jnp.concatenate(ranks, axis=0).astype(jnp.int32)

    @pl.when(s == pl.num_programs(1) - 1)
    def _():
        newa_ref[...] = ua_buf[:, ts + HIST_A_ROWS - (CONV_A_W - 1):ts + HIST_A_ROWS, :]
        newb_ref[...] = ub_buf[:, ts + HIST_B_ROWS - (CONV_B_W - 1):ts + HIST_B_ROWS, :]


def _mixer(x, hist_a, hist_b, cnt0, params, *, nb, ts, b0=0, bsz=None):
    (gmix, win, caw, cbw, cbb, lng, lnb, ga, gb, wout, gffn, wrt, br) = params
    _, seq, d = x.shape
    bsz = x.shape[0] if bsz is None else bsz
    d_a, d_b = caw.shape[-1], cbw.shape[-1]
    n_exp = wrt.shape[0]
    m = nb * ts
    n_tok = bsz * seq
    g0 = b0 // nb
    grid = (bsz // nb, seq // ts)
    tri = (jnp.arange(m)[:, None] < jnp.arange(m)[None, :]).astype(jnp.bfloat16)

    def const(shape):
        return pl.BlockSpec(shape, lambda i, j: (0,) * len(shape))

    tok_map = lambda i, j: (0, i * (seq // ts) + j)
    kern = functools.partial(_mixer_kernel, nb=nb, ts=ts, d_a=d_a, d_b=d_b, n_exp=n_exp)
    out_shape = (
        jax.ShapeDtypeStruct((bsz, seq, d), jnp.float32),
        jax.ShapeDtypeStruct((n_tok, d // 2), jnp.uint32),
        jax.ShapeDtypeStruct((TOP_K, n_tok), jnp.int32),
        jax.ShapeDtypeStruct((n_tok, SUBLANES), jnp.float32),
        jax.ShapeDtypeStruct((TOP_K, n_tok), jnp.int32),
        jax.ShapeDtypeStruct((n_exp, 1), jnp.float32),
        jax.ShapeDtypeStruct((bsz, CONV_A_W - 1, d_a), jnp.float32),
        jax.ShapeDtypeStruct((bsz, CONV_B_W - 1, d_b), jnp.float32),
    )
    return pl.pallas_call(
        kern,
        grid=grid,
        in_specs=[
            pl.BlockSpec((nb, ts, d), lambda i, j: (g0 + i, j, 0)),
            pl.BlockSpec((nb, HIST_A_ROWS, d_a), lambda i, j: (g0 + i, 0, 0)),
            pl.BlockSpec((nb, HIST_B_ROWS, d_b), lambda i, j: (g0 + i, 0, 0)),
            const((n_exp, 1)),
            const((m, m)),
            const((1, d)), const(win.shape), const(caw.shape), const(cbw.shape), const((1, d_b)),
            const((1, d_b)), const((1, d_b)), const((1, d_a)), const((1, d_b)), const(wout.shape),
            const((1, d)), const(wrt.shape), const((n_exp, 1)),
        ],
        out_specs=(
            pl.BlockSpec((nb, ts, d), lambda i, j: (i, j, 0)),
            pl.BlockSpec((m, d // 2), lambda i, j: (i * (seq // ts) + j, 0)),
            pl.BlockSpec((TOP_K, m), tok_map),
            pl.BlockSpec((m, SUBLANES), lambda i, j: (i * (seq // ts) + j, 0)),
            pl.BlockSpec((TOP_K, m), tok_map),
            pl.BlockSpec((n_exp, 1), lambda i, j: (0, 0)),
            pl.BlockSpec((nb, CONV_A_W - 1, d_a), lambda i, j: (i, 0, 0)),
            pl.BlockSpec((nb, CONV_B_W - 1, d_b), lambda i, j: (i, 0, 0)),
        ),
        out_shape=out_shape,
        scratch_shapes=[
            pltpu.VMEM((nb, HIST_A_ROWS + ts, d_a), jnp.float32),
            pltpu.VMEM((nb, HIST_B_ROWS + ts, d_b), jnp.float32),
            pltpu.VMEM((nb, ts, d_b), jnp.float32),
            pltpu.VMEM((n_exp, 1), jnp.float32),
        ],
        compiler_params=pltpu.CompilerParams(
            dimension_semantics=("arbitrary", "arbitrary"), vmem_limit_bytes=VMEM_LIMIT),
        name="mixer",
    )(x, hist_a, hist_b, cnt0, tri, gmix, win, caw, cbw, cbb, lng, lnb, ga, gb, wout, gffn, wrt, br)


def _positions_kernel(start_ref, *refs):
    n_parts = len(refs) // 3
    for p in range(n_parts):
        idx = refs[2 * p][...]
        pos = refs[2 * p + 1][...]
        for e in range(start_ref.shape[0]):
            pos = pos + jnp.where(idx == e, start_ref[e], 0)
        refs[2 * n_parts + p][...] = pos


def _positions(pad_start, parts):
    n_0 = parts[0][0].shape[1]
    tile = min(n_0, POS_TILE)
    specs = [pl.BlockSpec((TOP_K, tile), lambda i: (0, i))]
    specs += [pl.BlockSpec((TOP_K, idx.shape[1]), lambda i: (0, 0)) for idx, _ in parts[1:]]
    in_specs = [pl.BlockSpec(memory_space=pltpu.SMEM)]
    args = [pad_start]
    for spec, (idx, rank) in zip(specs, parts):
        in_specs += [spec, spec]
        args += [idx, rank]
    out = pl.pallas_call(
        _positions_kernel,
        grid=(n_0 // tile,),
        in_specs=in_specs,
        out_specs=tuple(specs),
        out_shape=tuple(jax.ShapeDtypeStruct(idx.shape, jnp.int32) for idx, _ in parts),
        compiler_params=pltpu.CompilerParams(dimension_semantics=("arbitrary",)),
        name="positions",
    )(*args)
    return list(out)


def _sc_mesh():
    return plsc.VectorSubcoreMesh(core_axis_name="c", subcore_axis_name="s",
                                  num_cores=SC_CORES, num_subcores=SC_SUBCORES)


def _sc_worker_id():
    return lax.axis_index("s") * SC_CORES + lax.axis_index("c")


def _sc_chunks(n_tok, wid):
    n_chunks = n_tok // SC_CHUNK
    assert n_tok % SC_CHUNK == 0
    if n_chunks % SC_WORKERS == 0:
        per_w = n_chunks // SC_WORKERS
        return None, per_w, lambda i: (wid * per_w + i) * SC_CHUNK
    assert n_chunks <= SC_WORKERS
    return wid < n_chunks, 1, lambda i: wid * SC_CHUNK


def _sc_for_each_chunk(n_tok, fn):
    wid = _sc_worker_id()
    pred, trips, base = _sc_chunks(n_tok, wid)

    def run():
        @pl.loop(0, trips)
        def _(i):
            fn(pl.multiple_of(base(i), SC_CHUNK))

    if pred is None:
        run()
    else:
        pl.when(pred)(run)


def _sc_dispatch(parts, n_rows):
    half = parts[0][1].shape[1]
    n_parts = len(parts)

    @functools.partial(
        pl.kernel, mesh=_sc_mesh(),
        out_type=jax.ShapeDtypeStruct((n_rows, half), jnp.uint32),
        scratch_types=[pltpu.VMEM((TOP_K, SC_CHUNK), jnp.int32),
                       pltpu.VMEM((SC_CHUNK, half), jnp.uint32),
                       pltpu.SemaphoreType.DMA],
        compiler_params=pltpu.CompilerParams(use_tc_tiling_on_sc=True),
        name="sc_dispatch")
    def k(*refs):
        xs_hbm, idx_v, rows_v, sem = refs[2 * n_parts:]

        def move(pos_hbm, src_hbm):
            def fn(base):
                pltpu.sync_copy(pos_hbm.at[:, pl.ds(base, SC_CHUNK)], idx_v)
                pltpu.sync_copy(src_hbm.at[pl.ds(base, SC_CHUNK)], rows_v)
                copies = [pltpu.async_copy(rows_v, xs_hbm.at[idx_v.at[j]], sem) for j in range(TOP_K)]
                for c in copies:
                    c.wait()
            return fn

        for p, (_, hnp) in enumerate(parts):
            _sc_for_each_chunk(hnp.shape[0], move(refs[2 * p], refs[2 * p + 1]))

    return k(*[a for part in parts for a in part])


def _sc_gather(pos_parts, rows):
    half = rows.shape[1]
    n_parts = len(pos_parts)

    @functools.partial(
        pl.kernel, mesh=_sc_mesh(),
        out_type=tuple(jax.ShapeDtypeStruct((TOP_K, p.shape[1], half), jnp.uint32) for p in pos_parts),
        scratch_types=[pltpu.VMEM((TOP_K, SC_CHUNK), jnp.int32),
                       pltpu.VMEM((SC_CHUNK, half), jnp.uint32),
                       pltpu.SemaphoreType.DMA],
        compiler_params=pltpu.CompilerParams(use_tc_tiling_on_sc=True),
        name="sc_gather")
    def k(*refs):
        pos_hbms, rows_hbm = refs[:n_parts], refs[n_parts]
        out_hbms = refs[n_parts + 1:2 * n_parts + 1]
        idx_v, buf_v, sem = refs[2 * n_parts + 1:]

        def move(pos_hbm, out_hbm):
            def fn(base):
                pltpu.sync_copy(pos_hbm.at[:, pl.ds(base, SC_CHUNK)], idx_v)
                for j in range(TOP_K):
                    pltpu.async_copy(rows_hbm.at[idx_v.at[j]], buf_v, sem).wait()
                    pltpu.sync_copy(buf_v, out_hbm.at[j, pl.ds(base, SC_CHUNK)])
            return fn

        for pos_hbm, out_hbm, p in zip(pos_hbms, out_hbms, pos_parts):
            _sc_for_each_chunk(p.shape[1], move(pos_hbm, out_hbm))

    out = k(*pos_parts, rows)
    return out if isinstance(out, (tuple, list)) else (out,)


def _experts_kernel(blk_exp_ref, first_ref, slot_ref, next_ref, n_active_ref,
                    xs_ref, wgu_hbm, bgu_ref, wd_hbm, bd_ref, out_ref,
                    wgu_f32, wd_f32, wgu_bf, wd_bf, sems):
    d_ff = wd_bf.shape[0]

    def fetch(e, sl):
        return (pltpu.make_async_copy(wgu_hbm.at[e], wgu_f32.at[sl], sems.at[0, sl]),
                pltpu.make_async_copy(wd_hbm.at[e], wd_f32.at[sl], sems.at[1, sl]))

    def load_weights_if_first(i):
        sl = slot_ref[i]
        e = blk_exp_ref[i]

        @pl.when(first_ref[i] == 1)
        def _():
            @pl.when(i == 0)
            def _():
                for c in fetch(e, sl):
                    c.start()

            for c in fetch(e, sl):
                c.wait()
            wgu_bf[...] = wgu_f32[sl].astype(jnp.bfloat16)
            wd_bf[...] = wd_f32[sl].astype(jnp.bfloat16)

            @pl.when(next_ref[i] >= 0)
            def _():
                for c in fetch(next_ref[i], 1 - sl):
                    c.start()

    def compute(rows, e):
        x = _unpack_bf16_pairs(xs_ref[rows, :]).astype(jnp.bfloat16)
        gu = jnp.dot(x, wgu_bf[...], preferred_element_type=jnp.float32) + bgu_ref[e]
        g = jnp.minimum(gu[:, :d_ff], SWIGLU_LIMIT)
        u = jnp.clip(gu[:, d_ff:], -SWIGLU_LIMIT, SWIGLU_LIMIT)
        act = g * jax.nn.sigmoid(SWIGLU_ALPHA * g) * (u + 1.0)
        o = jnp.dot(act.astype(jnp.bfloat16), wd_bf[...], preferred_element_type=jnp.float32) + bd_ref[e]
        out_ref[rows, :] = _pack_bf16_pairs(o)

    i0 = pl.program_id(0) * BLOCKS_PER_STEP
    n_active = n_active_ref[0]

    def run(sub0, n_sub):
        first = i0 + sub0
        rows = slice(sub0 * MOE_BLOCK, (sub0 + n_sub) * MOE_BLOCK)
        if n_sub == 1:
            @pl.when(first < n_active)
            def _():
                load_weights_if_first(first)
                compute(rows, blk_exp_ref[first])

            @pl.when(first >= n_active)
            def _():
                out_ref[rows, :] = jnp.zeros((MOE_BLOCK, out_ref.shape[1]), out_ref.dtype)
            return

        same = first + n_sub - 1 < n_active
        for sub in range(1, n_sub):
            same = same & (blk_exp_ref[first + sub] == blk_exp_ref[first])

        @pl.when(same)
        def _():
            load_weights_if_first(first)
            compute(rows, blk_exp_ref[first])

        @pl.when(jnp.logical_not(same))
        def _():
            run(sub0, n_sub // 2)
            run(sub0 + n_sub // 2, n_sub // 2)

    run(0, BLOCKS_PER_STEP)


def _experts(plan, xs, wgu, bgu, wd, bd):
    blk_exp, first, slot, nxt, n_active = plan
    n_rows, half = xs.shape
    n_exp, d, two_f = wgu.shape
    d_ff = wd.shape[1]
    n_blocks = n_rows // MOE_BLOCK
    step_rows = MOE_BLOCK * BLOCKS_PER_STEP
    assert n_rows % step_rows == 0
    grid_spec = pltpu.PrefetchScalarGridSpec(
        num_scalar_prefetch=5,
        grid=(n_blocks // BLOCKS_PER_STEP,),
        in_specs=[
            pl.BlockSpec((step_rows, half), lambda i, *_: (i, 0)),
            pl.BlockSpec(memory_space=pl.ANY),
            pl.BlockSpec((n_exp, 1, two_f), lambda i, *_: (0, 0, 0)),
            pl.BlockSpec(memory_space=pl.ANY),
            pl.BlockSpec((n_exp, 1, d), lambda i, *_: (0, 0, 0)),
        ],
        out_specs=pl.BlockSpec((step_rows, half), lambda i, *_: (i, 0)),
        scratch_shapes=[
            pltpu.VMEM((2, d, two_f), jnp.float32),
            pltpu.VMEM((2, d_ff, d), jnp.float32),
            pltpu.VMEM((d, two_f), jnp.bfloat16),
            pltpu.VMEM((d_ff, d), jnp.bfloat16),
            pltpu.SemaphoreType.DMA((2, 2)),
        ],
    )
    return pl.pallas_call(
        _experts_kernel,
        grid_spec=grid_spec,
        out_shape=jax.ShapeDtypeStruct((n_rows, half), jnp.uint32),
        compiler_params=pltpu.CompilerParams(
            dimension_semantics=("arbitrary",), vmem_limit_bytes=VMEM_LIMIT),
        name="experts",
    )(blk_exp, first, slot, nxt, n_active, xs, wgu, bgu.reshape(n_exp, 1, two_f), wd,
      bd.reshape(n_exp, 1, d))


def _combine_kernel(rows_ref, gate_ref, h_ref, gfin_ref, *rest):
    y_ref = rest[-1]
    acc = h_ref[...]
    for k in range(TOP_K):
        acc = acc + gate_ref[:, k:k + 1] * _unpack_bf16_pairs(rows_ref[k])
    y_ref[...] = _rms(acc, gfin_ref[...])


def _combine(rows4, gate_t, h2d, gfin, *, tok0=0, y_prev=None, y_rows=None, y_tok0=0):
    d = h2d.shape[1]
    n, half = rows4.shape[1:]
    n_tok = h2d.shape[0] if y_rows is None else y_rows
    tile = min(n, COMBINE_TILE)
    t0 = tok0 // tile
    y0 = y_tok0 // tile
    in_specs = [
        pl.BlockSpec((TOP_K, tile, half), lambda i: (0, i, 0)),
        pl.BlockSpec((tile, gate_t.shape[1]), lambda i: (t0 + i, 0)),
        pl.BlockSpec((tile, d), lambda i: (t0 + i, 0)),
        pl.BlockSpec((1, d), lambda i: (0, 0)),
    ]
    args = [rows4, gate_t, h2d, gfin]
    aliases = {}
    if y_prev is not None:
        in_specs.append(pl.BlockSpec(memory_space=pl.ANY))
        args.append(y_prev)
        aliases = {4: 0}
    return pl.pallas_call(
        _combine_kernel,
        grid=(n // tile,),
        in_specs=in_specs,
        out_specs=pl.BlockSpec((tile, d), lambda i: (y0 + i, 0)),
        out_shape=jax.ShapeDtypeStruct((n_tok, d), jnp.float32),
        input_output_aliases=aliases,
        compiler_params=pltpu.CompilerParams(
            dimension_semantics=("arbitrary",), vmem_limit_bytes=VMEM_LIMIT),
        name="combine",
    )(*args)


def _pad_hist(hist, rows):
    return jnp.pad(hist, ((0, 0), (rows - hist.shape[1], 0), (0, 0)))


def kernel(x_prompt, x_sample, state_conv_a, state_conv_b, norm_mix_g, w_in, conv_a_w, conv_b_w, conv_b_b, conv_ln_g, conv_ln_b, out_norm_a_g, out_norm_b_g, w_out, norm_ffn_g, w_router, b_router, w_gate_up, b_gate_up, w_down, b_down, final_norm_g):
    depth = w_in.shape[0]
    assert depth == 1, "single-layer trunk"
    bf16 = jnp.bfloat16
    bsz, seq, d = x_prompt.shape
    dec_b, dec_s, _ = x_sample.shape
    d_a, d_b = conv_a_w.shape[-1], conv_b_w.shape[-1]
    n_exp = w_router.shape[-1]

    params = (norm_mix_g[0][None], w_in[0].astype(bf16), conv_a_w[0], conv_b_w[0], conv_b_b[0][None],
              conv_ln_g[0][None], conv_ln_b[0][None], out_norm_a_g[0][None], out_norm_b_g[0][None],
              w_out[0].astype(bf16), norm_ffn_g[0][None], w_router[0].T.astype(bf16),
              b_router[0][:, None])

    zero_a = jnp.zeros((bsz, HIST_A_ROWS, d_a), jnp.float32)
    zero_b = jnp.zeros((bsz, HIST_B_ROWS, d_b), jnp.float32)
    cnt0 = jnp.zeros((n_exp, 1), jnp.float32)
    gfin = final_norm_g[None]
    ts = min(seq, MIXER_ROWS)
    n_p = bsz * seq

    sizes = [bsz * f // sum(GROUP_SPLIT) for f in GROUP_SPLIT]
    if bsz % sum(GROUP_SPLIT) != 0:
        sizes = [bsz]
    groups = []
    tok_start = []
    for g, per_b in enumerate(sizes):
        b0 = sum(sizes[:g])
        tok_start.append(b0 * seq)
        h, hnp, idx, gate, rank, cnt, na, nb_ = _mixer(
            x_prompt, zero_a, zero_b, cnt0, params, nb=1, ts=ts, b0=b0, bsz=per_b)
        parts = [dict(h=h.reshape(per_b * seq, d), hnp=hnp, idx=idx, gate_t=gate, rank=rank)]
        states = [(na, nb_)]
        if g == len(sizes) - 1:
            h, hnp, idx, gate, rank, cnt, na_s, nb_s = _mixer(
                x_sample, _pad_hist(state_conv_a[0], HIST_A_ROWS), _pad_hist(state_conv_b[0], HIST_B_ROWS),
                cnt, params, nb=dec_b, ts=dec_s)
            parts.append(dict(h=h.reshape(dec_b * dec_s, d), hnp=hnp, idx=idx, gate_t=gate, rank=rank))
        n_pairs = sum(p["hnp"].shape[0] for p in parts) * TOP_K
        plan, pad_start, n_blocks = _block_plan(cnt[:, 0].astype(jnp.int32), n_pairs, n_exp)
        pos = _positions(pad_start, [(p["idx"], p["rank"]) for p in parts])
        xs = _sc_dispatch([(q, p["hnp"]) for q, p in zip(pos, parts)], n_blocks * MOE_BLOCK)
        groups.append(dict(parts=parts, pos=pos, plan=plan, xs=xs, states=states))

    for grp in groups:
        grp["rows"] = _experts(grp["plan"], grp["xs"], w_gate_up[0], b_gate_up[0], w_down[0], b_down[0])

    gathered = []
    for g, grp in enumerate(groups):
        pos0 = grp["pos"][0]
        n_g = pos0.shape[1]
        want = 1 if (g == len(groups) - 1 and len(groups) > 1) else COMBINE_CHUNKS
        n_chunks = want if n_g % (want * SC_CHUNK * SC_WORKERS) == 0 else 1
        per = n_g // n_chunks
        for c in range(n_chunks):
            pos_parts = [pos0[:, c * per:(c + 1) * per]] + (grp["pos"][1:] if c == 0 else [])
            gathered.append((g, c * per, _sc_gather(pos_parts, grp["rows"])))

    y_p = None
    for g, tok0, got in sorted(gathered, key=lambda t: (t[0] != len(groups) - 1, t[0], t[1])):
        grp = groups[g]
        main = grp["parts"][0]
        if len(got) > 1:
            smp = grp["parts"][1]
            y_s = _combine(got[1], smp["gate_t"], smp["h"], gfin).reshape(dec_b, dec_s, d)
        y_p = _combine(got[0], main["gate_t"], main["h"], gfin, tok0=tok0,
                       y_prev=y_p, y_rows=n_p, y_tok0=tok_start[g] + tok0)
    y_p = y_p.reshape(bsz, seq, d)

    na_p = jnp.concatenate([grp["states"][0][0] for grp in groups], axis=0)
    nb_p = jnp.concatenate([grp["states"][0][1] for grp in groups], axis=0)
    return (y_p, y_s, na_p[None], nb_p[None], na_s[None], nb_s[None])


def _block_plan(counts, n_pairs, n_exp):
    padded = (counts + MOE_BLOCK - 1) // MOE_BLOCK * MOE_BLOCK
    pad_end = jnp.cumsum(padded)
    pad_start = pad_end - padded
    n_blocks = -(-n_pairs // MOE_BLOCK) + n_exp
    n_blocks = -(-n_blocks // BLOCKS_PER_STEP) * BLOCKS_PER_STEP
    blk_start = jnp.arange(n_blocks, dtype=jnp.int32) * MOE_BLOCK
    blk_exp = jnp.minimum(jnp.sum(blk_start[:, None] >= pad_end[None, :], axis=1),
                          n_exp - 1).astype(jnp.int32)
    n_active = (pad_end[-1:] // MOE_BLOCK).astype(jnp.int32)
    e_ids = jnp.arange(n_exp, dtype=jnp.int32)
    has = padded > 0
    slot_e = (jnp.cumsum(has.astype(jnp.int32)) - 1) % 2
    later = jnp.where(has, e_ids, n_exp)
    next_e = jnp.concatenate([lax.cummin(later, reverse=True)[1:], jnp.full((1,), n_exp, jnp.int32)])
    next_e = jnp.where(next_e >= n_exp, -1, next_e)
    blk_onehot = blk_exp[:, None] == e_ids[None, :]

    def per_block(table):
        return jnp.sum(jnp.where(blk_onehot, table[None, :], 0), axis=1).astype(jnp.int32)

    blk_first = ((blk_start == per_block(pad_start)) & (blk_start < pad_end[-1])).astype(jnp.int32)
    plan = (blk_exp, blk_first, per_block(slot_e), per_block(next_e), n_active)
    return plan, pad_start.astype(jnp.int32), n_blocks
```

```python
import functools

import jax
import jax.numpy as jnp
from jax import lax
from jax.experimental import pallas as pl
from jax.experimental.pallas import tpu as pltpu
from jax.experimental.pallas import tpu_sc as plsc

SC_CORES = 2
SC_SUBCORES = 16
SC_WORKERS = SC_CORES * SC_SUBCORES
SC_CHUNK = 128
CONV_A_W = 3
CONV_B_W = 31
TOP_K = 4
SWIGLU_LIMIT = 7.0
SWIGLU_ALPHA = 1.702
NORM_EPS = 1e-5
MOE_BLOCK = 256
BLOCKS_PER_STEP = 4
SUBLANES = 8
LANES = 128
CONV_ROW_CHUNK = 128
HIST_A_ROWS = 8
HIST_B_ROWS = 32
MIXER_ROWS = 1024
PROJ_SECTION = 256
COMBINE_TILE = 1024
POS_TILE = 4096
GROUP_SPLIT = (1, 3)
COMBINE_CHUNKS = 2
VMEM_LIMIT = 56 * 1024 * 1024


def _rms(x, g):
    return x * lax.rsqrt(jnp.mean(x * x, axis=-1, keepdims=True) + NORM_EPS) * g


def _zero_tile_from(v):
    bits = pltpu.bitcast(v[0:SUBLANES, 0:LANES], jnp.uint32)
    return pltpu.bitcast((bits >> 16) >> 16, jnp.float32)


def _pack_bf16_pairs(a):
    w = a.shape[-1] // 2
    lo = pltpu.bitcast(a[:, :w].astype(jnp.bfloat16).astype(jnp.float32), jnp.uint32)
    hi = pltpu.bitcast(a[:, w:].astype(jnp.bfloat16).astype(jnp.float32), jnp.uint32)
    return (lo >> 16) | (hi & jnp.uint32(0xFFFF0000))


def _unpack_bf16_pairs(wd):
    lo = pltpu.bitcast(wd << 16, jnp.float32)
    hi = pltpu.bitcast(wd & jnp.uint32(0xFFFF0000), jnp.float32)
    return jnp.concatenate([lo, hi], axis=-1)


def _conv_b_chunk(ub_buf, cbw_ref, n_i, r0, rc, c0):
    cs = slice(c0, c0 + LANES)
    lead = HIST_B_ROWS - (CONV_B_W - 1)
    acc = None
    for b in range(SUBLANES):
        rows = rc if b == 0 else rc + SUBLANES
        q = None
        for a in range((HIST_B_ROWS + SUBLANES) // SUBLANES):
            k = SUBLANES * a + b - lead
            if k < 0 or k >= CONV_B_W:
                continue
            start = r0 + SUBLANES * a
            term = cbw_ref[k:k + 1, cs] * ub_buf[n_i, start:start + rows, cs]
            q = term if q is None else q + term
        part = q if b == 0 else q[b:b + rc]
        acc = part if acc is None else acc + part
    return acc


def _mixer_kernel(x_ref, hista_ref, histb_ref, cnt0_ref, tri_ref,
                  gmix_ref, win_ref, caw_ref, cbw_ref, cbb_ref, lng_ref, lnb_ref,
                  ga_ref, gb_ref, wout_ref, gffn_ref, wrt_ref, br_ref,
                  h_ref, hnp_ref, idx_ref, gate_ref, rank_ref, cnt_ref, newa_ref, newb_ref,
                  ua_buf, ub_buf, cb_buf, cnt_acc, *, nb, ts, d_a, d_b, n_exp):
    b = pl.program_id(0)
    s = pl.program_id(1)
    m = nb * ts
    d = x_ref.shape[-1]

    @pl.when(s == 0)
    def _():
        ua_buf[:, 0:HIST_A_ROWS, :] = hista_ref[...]
        ub_buf[:, 0:HIST_B_ROWS, :] = histb_ref[...]

    @pl.when(s != 0)
    def _():
        ua_buf[:, 0:HIST_A_ROWS, :] = ua_buf[:, ts:ts + HIST_A_ROWS, :]
        ub_buf[:, 0:HIST_B_ROWS, :] = ub_buf[:, ts:ts + HIST_B_ROWS, :]

    @pl.when((b == 0) & (s == 0))
    def _():
        cnt_acc[...] = cnt0_ref[...]

    x = x_ref[...].reshape(m, d)
    n = _rms(x, gmix_ref[...]).astype(jnp.bfloat16)
    def proj(c0, width):
        return jnp.dot(n, win_ref[:, c0:c0 + width], preferred_element_type=jnp.float32)

    sec = min(d_b, PROJ_SECTION)
    for c0 in range(0, d_b, sec):
        v_b = proj(3 * d_a + c0, sec)
        g_b = proj(3 * d_a + d_b + c0, sec)
        ub_buf[:, HIST_B_ROWS:HIST_B_ROWS + ts, c0:c0 + sec] = (v_b * jax.nn.sigmoid(g_b)).reshape(nb, ts, sec)

    gate_c = proj(d_a, d_a)
    xt = proj(2 * d_a, d_a)
    gate_b = proj(0, d_a)
    ua_buf[:, HIST_A_ROWS:HIST_A_ROWS + ts, :] = (gate_c * xt).reshape(nb, ts, d_a)
    conv_a = jnp.zeros((nb, ts, d_a), jnp.float32)
    for k in range(CONV_A_W):
        off = HIST_A_ROWS - (CONV_A_W - 1) + k
        conv_a = conv_a + caw_ref[k:k + 1, :] * ua_buf[:, off:off + ts, :]
    y_a = gate_b * conv_a.reshape(m, d_a)

    mix_a = jnp.dot(_rms(y_a, ga_ref[...]).astype(jnp.bfloat16), wout_ref[0:d_a, :],
                    preferred_element_type=jnp.float32)

    rc = min(ts, CONV_ROW_CHUNK)
    chunks = [(n_i, r0, c0) for n_i in range(nb) for r0 in range(0, ts, rc) for c0 in range(0, d_b, LANES)]
    anchors = {len(chunks) * 2 // 8: gate_c, len(chunks) * 3 // 8: xt, len(chunks) * 5 // 8: gate_b,
               len(chunks) - 1: mix_a}
    for ci, (n_i, r0, c0) in enumerate(chunks):
        acc = _conv_b_chunk(ub_buf, cbw_ref, n_i, r0, rc, c0)
        if ci in anchors:
            acc = (acc.reshape(rc // SUBLANES, SUBLANES, LANES) + _zero_tile_from(anchors[ci])[None]
                   ).reshape(rc, LANES)
        cb_buf[n_i, r0:r0 + rc, c0:c0 + LANES] = acc
    cb = cb_buf[...].reshape(m, d_b) + cbb_ref[...]
    mu = jnp.mean(cb, axis=-1, keepdims=True)
    xc = cb - mu
    var = jnp.mean(xc * xc, axis=-1, keepdims=True)
    ln = xc * lax.rsqrt(var + NORM_EPS) * lng_ref[...] + lnb_ref[...]
    y_b = ln * jax.nn.sigmoid(ln)

    mix_b = jnp.dot(_rms(y_b, gb_ref[...]).astype(jnp.bfloat16), wout_ref[d_a:d_a + d_b, :],
                    preferred_element_type=jnp.float32)
    h = x + (mix_a + mix_b)
    h_ref[...] = h.reshape(nb, ts, d)

    hn = _rms(h, gffn_ref[...])
    hnp_ref[...] = _pack_bf16_pairs(hn)

    logits = lax.dot_general(wrt_ref[...], hn.astype(jnp.bfloat16), (((1,), (1,)), ((), ())),
                             preferred_element_type=jnp.float32) + br_ref[...]
    e_iota = lax.broadcasted_iota(jnp.int32, (n_exp, m), 0)
    cur = logits
    vals, idxs, sels = [], [], []
    for _ in range(TOP_K):
        mx = jnp.max(cur, axis=0, keepdims=True)
        ix = jnp.min(jnp.where(cur == mx, e_iota, n_exp), axis=0, keepdims=True)
        sel = e_iota == ix
        vals.append(mx)
        idxs.append(ix)
        sels.append(sel)
        cur = jnp.where(sel, -jnp.inf, cur)
    exps = [jnp.exp(v - vals[0]) for v in vals]
    denom = exps[0] + exps[1] + exps[2] + exps[3]
    gates = [e / denom for e in exps]

    onehot = jnp.zeros((n_exp, m), jnp.float32)
    for sel in sels:
        onehot = onehot + sel.astype(jnp.float32)
    before = jnp.dot(onehot.astype(jnp.bfloat16), tri_ref[...],
                     preferred_element_type=jnp.float32) + cnt_acc[...]
    ranks = [jnp.sum(jnp.where(sel, before, 0.0), axis=0, keepdims=True) for sel in sels]
    new_cnt = cnt_acc[...] + jnp.sum(onehot, axis=1, keepdims=True)
    cnt_acc[...] = new_cnt
    cnt_ref[...] = new_cnt

    idx_ref[...] = jnp.concatenate(idxs, axis=0)
    gate_ref[...] = jnp.concatenate(gates + [jnp.zeros((SUBLANES - TOP_K, m), jnp.float32)], axis=0).T
    rank_ref[...] = jnp.concatenate(ranks, axis=0).astype(jnp.int32)

    @pl.when(s == pl.num_programs(1) - 1)
    def _():
        newa_ref[...] = ua_buf[:, ts + HIST_A_ROWS - (CONV_A_W - 1):ts + HIST_A_ROWS, :]
        newb_ref[...] = ub_buf[:, ts + HIST_B_ROWS - (CONV_B_W - 1):ts + HIST_B_ROWS, :]


def _mixer(x, hist_a, hist_b, cnt0, params, *, nb, ts, b0=0, bsz=None):
    (gmix, win, caw, cbw, cbb, lng, lnb, ga, gb, wout, gffn, wrt, br) = params
    _, seq, d = x.shape
    bsz = x.shape[0] if bsz is None else bsz
    d_a, d_b = caw.shape[-1], cbw.shape[-1]
    n_exp = wrt.shape[0]
    m = nb * ts
    n_tok = bsz * seq
    g0 = b0 // nb
    grid = (bsz // nb, seq // ts)
    tri = (jnp.arange(m)[:, None] < jnp.arange(m)[None, :]).astype(jnp.bfloat16)

    def const(shape):
        return pl.BlockSpec(shape, lambda i, j: (0,) * len(shape))

    tok_map = lambda i, j: (0, i * (seq // ts) + j)
    kern = functools.partial(_mixer_kernel, nb=nb, ts=ts, d_a=d_a, d_b=d_b, n_exp=n_exp)
    out_shape = (
        jax.ShapeDtypeStruct((bsz, seq, d), jnp.float32),
        jax.ShapeDtypeStruct((n_tok, d // 2), jnp.uint32),
        jax.ShapeDtypeStruct((TOP_K, n_tok), jnp.int32),
        jax.ShapeDtypeStruct((n_tok, SUBLANES), jnp.float32),
        jax.ShapeDtypeStruct((TOP_K, n_tok), jnp.int32),
        jax.ShapeDtypeStruct((n_exp, 1), jnp.float32),
        jax.ShapeDtypeStruct((bsz, CONV_A_W - 1, d_a), jnp.float32),
        jax.ShapeDtypeStruct((bsz, CONV_B_W - 1, d_b), jnp.float32),
    )
    return pl.pallas_call(
        kern,
        grid=grid,
        in_specs=[
            pl.BlockSpec((nb, ts, d), lambda i, j: (g0 + i, j, 0)),
            pl.BlockSpec((nb, HIST_A_ROWS, d_a), lambda i, j: (g0 + i, 0, 0)),
            pl.BlockSpec((nb, HIST_B_ROWS, d_b), lambda i, j: (g0 + i, 0, 0)),
            const((n_exp, 1)),
            const((m, m)),
            const((1, d)), const(win.shape), const(caw.shape), const(cbw.shape), const((1, d_b)),
            const((1, d_b)), const((1, d_b)), const((1, d_a)), const((1, d_b)), const(wout.shape),
            const((1, d)), const(wrt.shape), const((n_exp, 1)),
        ],
        out_specs=(
            pl.BlockSpec((nb, ts, d), lambda i, j: (i, j, 0)),
            pl.BlockSpec((m, d // 2), lambda i, j: (i * (seq // ts) + j, 0)),
            pl.BlockSpec((TOP_K, m), tok_map),
            pl.BlockSpec((m, SUBLANES), lambda i, j: (i * (seq // ts) + j, 0)),
            pl.BlockSpec((TOP_K, m), tok_map),
            pl.BlockSpec((n_exp, 1), lambda i, j: (0, 0)),
            pl.BlockSpec((nb, CONV_A_W - 1, d_a), lambda i, j: (i, 0, 0)),
            pl.BlockSpec((nb, CONV_B_W - 1, d_b), lambda i, j: (i, 0, 0)),
        ),
        out_shape=out_shape,
        scratch_shapes=[
            pltpu.VMEM((nb, HIST_A_ROWS + ts, d_a), jnp.float32),
            pltpu.VMEM((nb, HIST_B_ROWS + ts, d_b), jnp.float32),
            pltpu.VMEM((nb, ts, d_b), jnp.float32),
            pltpu.VMEM((n_exp, 1), jnp.float32),
        ],
        compiler_params=pltpu.CompilerParams(
            dimension_semantics=("arbitrary", "arbitrary"), vmem_limit_bytes=VMEM_LIMIT),
        name="mixer",
    )(x, hist_a, hist_b, cnt0, tri, gmix, win, caw, cbw, cbb, lng, lnb, ga, gb, wout, gffn, wrt, br)


def _positions_kernel(start_ref, *refs):
    n_parts = len(refs) // 3
    for p in range(n_parts):
        idx = refs[2 * p][...]
        pos = refs[2 * p + 1][...]
        for e in range(start_ref.shape[0]):
            pos = pos + jnp.where(idx == e, start_ref[e], 0)
        refs[2 * n_parts + p][...] = pos


def _positions(pad_start, parts):
    n_0 = parts[0][0].shape[1]
    tile = min(n_0, POS_TILE)
    specs = [pl.BlockSpec((TOP_K, tile), lambda i: (0, i))]
    specs += [pl.BlockSpec((TOP_K, idx.shape[1]), lambda i: (0, 0)) for idx, _ in parts[1:]]
    in_specs = [pl.BlockSpec(memory_space=pltpu.SMEM)]
    args = [pad_start]
    for spec, (idx, rank) in zip(specs, parts):
        in_specs += [spec, spec]
        args += [idx, rank]
    out = pl.pallas_call(
        _positions_kernel,
        grid=(n_0 // tile,),
        in_specs=in_specs,
        out_specs=tuple(specs),
        out_shape=tuple(jax.ShapeDtypeStruct(idx.shape, jnp.int32) for idx, _ in parts),
        compiler_params=pltpu.CompilerParams(dimension_semantics=("arbitrary",)),
        name="positions",
    )(*args)
    return list(out)


def _sc_mesh():
    return plsc.VectorSubcoreMesh(core_axis_name="c", subcore_axis_name="s",
                                  num_cores=SC_CORES, num_subcores=SC_SUBCORES)


def _sc_worker_id():
    return lax.axis_index("s") * SC_CORES + lax.axis_index("c")


def _sc_chunks(n_tok, wid):
    n_chunks = n_tok // SC_CHUNK
    assert n_tok % SC_CHUNK == 0
    if n_chunks % SC_WORKERS == 0:
        per_w = n_chunks // SC_WORKERS
        return None, per_w, lambda i: (wid * per_w + i) * SC_CHUNK
    assert n_chunks <= SC_WORKERS
    return wid < n_chunks, 1, lambda i: wid * SC_CHUNK


def _sc_for_each_chunk(n_tok, fn):
    wid = _sc_worker_id()
    pred, trips, base = _sc_chunks(n_tok, wid)

    def run():
        @pl.loop(0, trips)
        def _(i):
            fn(pl.multiple_of(base(i), SC_CHUNK))

    if pred is None:
        run()
    else:
        pl.when(pred)(run)


def _sc_dispatch(parts, n_rows):
    half = parts[0][1].shape[1]
    n_parts = len(parts)

    @functools.partial(
        pl.kernel, mesh=_sc_mesh(),
        out_type=jax.ShapeDtypeStruct((n_rows, half), jnp.uint32),
        scratch_types=[pltpu.VMEM((TOP_K, SC_CHUNK), jnp.int32),
                       pltpu.VMEM((SC_CHUNK, half), jnp.uint32),
                       pltpu.SemaphoreType.DMA],
        compiler_params=pltpu.CompilerParams(use_tc_tiling_on_sc=True),
        name="sc_dispatch")
    def k(*refs):
        xs_hbm, idx_v, rows_v, sem = refs[2 * n_parts:]

        def move(pos_hbm, src_hbm):
            def fn(base):
                pltpu.sync_copy(pos_hbm.at[:, pl.ds(base, SC_CHUNK)], idx_v)
                pltpu.sync_copy(src_hbm.at[pl.ds(base, SC_CHUNK)], rows_v)
                copies = [pltpu.async_copy(rows_v, xs_hbm.at[idx_v.at[j]], sem) for j in range(TOP_K)]
                for c in copies:
                    c.wait()
            return fn

        for p, (_, hnp) in enumerate(parts):
            _sc_for_each_chunk(hnp.shape[0], move(refs[2 * p], refs[2 * p + 1]))

    return k(*[a for part in parts for a in part])


def _sc_gather(pos_parts, rows):
    half = rows.shape[1]
    n_parts = len(pos_parts)

    @functools.partial(
        pl.kernel, mesh=_sc_mesh(),
        out_type=tuple(jax.ShapeDtypeStruct((TOP_K, p.shape[1], half), jnp.uint32) for p in pos_parts),
        scratch_types=[pltpu.VMEM((TOP_K, SC_CHUNK), jnp.int32),
                       pltpu.VMEM((SC_CHUNK, half), jnp.uint32),
                       pltpu.SemaphoreType.DMA],
        compiler_params=pltpu.CompilerParams(use_tc_tiling_on_sc=True),
        name="sc_gather")
    def k(*refs):
        pos_hbms, rows_hbm = refs[:n_parts], refs[n_parts]
        out_hbms = refs[n_parts + 1:2 * n_parts + 1]
        idx_v, buf_v, sem = refs[2 * n_parts + 1:]

        def move(pos_hbm, out_hbm):
            def fn(base):
                pltpu.sync_copy(pos_hbm.at[:, pl.ds(base, SC_CHUNK)], idx_v)
                for j in range(TOP_K):
                    pltpu.async_copy(rows_hbm.at[idx_v.at[j]], buf_v, sem).wait()
                    pltpu.sync_copy(buf_v, out_hbm.at[j, pl.ds(base, SC_CHUNK)])
            return fn

        for pos_hbm, out_hbm, p in zip(pos_hbms, out_hbms, pos_parts):
            _sc_for_each_chunk(p.shape[1], move(pos_hbm, out_hbm))

    out = k(*pos_parts, rows)
    return out if isinstance(out, (tuple, list)) else (out,)


def _experts_kernel(*refs, publish, reuse):
    blk_exp_ref, first_ref, slot_ref, next_ref, n_active_ref = refs[:5]
    refs = refs[5:]
    if reuse:
        have_ref, have_next_ref = refs[:2]
        refs = refs[2:]
    xs_ref, wgu_hbm, bgu_ref, wd_hbm, bd_ref = refs[:5]
    refs = refs[5:]
    if reuse:
        wgu16_hbm, wd16_hbm = refs[:2]
        refs = refs[2:]
    out_ref = refs[0]
    refs = refs[1:]
    if publish:
        wgu16_out, wd16_out = refs[:2]
        refs = refs[2:]
    wgu_f32, wd_f32, wgu_bf, wd_bf, sems = refs[:5]
    refs = refs[5:]
    if publish:
        out_sems, pending = refs
    if reuse:
        sems16, = refs
    d_ff = wd_bf.shape[-2]

    def bf_slot(parity):
        return parity if reuse else 0

    def f32_slot(parity):
        return 0 if reuse else parity

    def fetch(e, parity):
        sl = f32_slot(parity)
        return (pltpu.make_async_copy(wgu_hbm.at[e], wgu_f32.at[sl], sems.at[0, sl]),
                pltpu.make_async_copy(wd_hbm.at[e], wd_f32.at[sl], sems.at[1, sl]))

    def fetch16(e, parity):
        return (pltpu.make_async_copy(wgu16_hbm.at[e], wgu_bf.at[parity], sems16.at[0, parity]),
                pltpu.make_async_copy(wd16_hbm.at[e], wd_bf.at[parity], sems16.at[1, parity]))

    def publish_copies(e):
        return (pltpu.make_async_copy(wgu_bf.at[0], wgu16_out.at[e], out_sems.at[0]),
                pltpu.make_async_copy(wd_bf.at[0], wd16_out.at[e], out_sems.at[1]))

    def wait_published():
        @pl.when(pending[0] == 1)
        def _():
            for c in publish_copies(0):
                c.wait()
            pending[0] = 0

    def start_fetch(e, parity, have):
        if reuse:
            @pl.when(have == 1)
            def _():
                for c in fetch16(e, parity):
                    c.start()

            @pl.when(have != 1)
            def _():
                for c in fetch(e, parity):
                    c.start()
        else:
            for c in fetch(e, parity):
                c.start()

    def finish_fetch(e, parity, have):
        def from_f32():
            for c in fetch(e, parity):
                c.wait()
            if publish:
                wait_published()
            wgu_bf[bf_slot(parity)] = wgu_f32[f32_slot(parity)].astype(jnp.bfloat16)
            wd_bf[bf_slot(parity)] = wd_f32[f32_slot(parity)].astype(jnp.bfloat16)
            if publish:
                for c in publish_copies(e):
                    c.start()
                pending[0] = 1

        if reuse:
            @pl.when(have == 1)
            def _():
                for c in fetch16(e, parity):
                    c.wait()

            pl.when(have != 1)(from_f32)
        else:
            from_f32()

    def load_weights_if_first(i):
        parity = slot_ref[i]
        e = blk_exp_ref[i]
        have = have_ref[i] if reuse else 0

        @pl.when(first_ref[i] == 1)
        def _():
            @pl.when(i == 0)
            def _():
                if publish:
                    pending[0] = 0
                start_fetch(e, parity, have)

            finish_fetch(e, parity, have)

            @pl.when(next_ref[i] >= 0)
            def _():
                start_fetch(next_ref[i], 1 - parity, have_next_ref[i] if reuse else 0)

    def compute(rows, i):
        e = blk_exp_ref[i]
        sl = bf_slot(slot_ref[i])
        x = _unpack_bf16_pairs(xs_ref[rows, :]).astype(jnp.bfloat16)
        gu = jnp.dot(x, wgu_bf[sl], preferred_element_type=jnp.float32) + bgu_ref[e]
        g = jnp.minimum(gu[:, :d_ff], SWIGLU_LIMIT)
        u = jnp.clip(gu[:, d_ff:], -SWIGLU_LIMIT, SWIGLU_LIMIT)
        act = g * jax.nn.sigmoid(SWIGLU_ALPHA * g) * (u + 1.0)
        o = jnp.dot(act.astype(jnp.bfloat16), wd_bf[sl], preferred_element_type=jnp.float32) + bd_ref[e]
        out_ref[rows, :] = _pack_bf16_pairs(o)

    i0 = pl.program_id(0) * BLOCKS_PER_STEP
    n_active = n_active_ref[0]

    def run(sub0, n_sub):
        first = i0 + sub0
        rows = slice(sub0 * MOE_BLOCK, (sub0 + n_sub) * MOE_BLOCK)
        if n_sub == 1:
            @pl.when(first < n_active)
            def _():
                load_weights_if_first(first)
                compute(rows, first)

            @pl.when(first >= n_active)
            def _():
                out_ref[rows, :] = jnp.zeros((MOE_BLOCK, out_ref.shape[1]), out_ref.dtype)
            return

        same = first + n_sub - 1 < n_active
        for sub in range(1, n_sub):
            same = same & (blk_exp_ref[first + sub] == blk_exp_ref[first])

        @pl.when(same)
        def _():
            load_weights_if_first(first)
            compute(rows, first)

        @pl.when(jnp.logical_not(same))
        def _():
            run(sub0, n_sub // 2)
            run(sub0 + n_sub // 2, n_sub // 2)

    run(0, BLOCKS_PER_STEP)

    if publish:
        @pl.when(pl.program_id(0) == pl.num_programs(0) - 1)
        def _():
            wait_published()


def _experts(plan, xs, wgu, bgu, wd, bd, *, publish=False, reuse=None):
    blk_exp, first, slot, nxt, n_active = plan
    n_rows, half = xs.shape
    n_exp, d, two_f = wgu.shape
    d_ff = wd.shape[1]
    n_blocks = n_rows // MOE_BLOCK
    step_rows = MOE_BLOCK * BLOCKS_PER_STEP
    assert n_rows % step_rows == 0
    any_spec = pl.BlockSpec(memory_space=pl.ANY)
    scalars = [blk_exp, first, slot, nxt, n_active]
    operands = [xs, wgu, bgu.reshape(n_exp, 1, two_f), wd, bd.reshape(n_exp, 1, d)]
    in_specs = [
        pl.BlockSpec((step_rows, half), lambda i, *_: (i, 0)),
        any_spec,
        pl.BlockSpec((n_exp, 1, two_f), lambda i, *_: (0, 0, 0)),
        any_spec,
        pl.BlockSpec((n_exp, 1, d), lambda i, *_: (0, 0, 0)),
    ]
    out_specs = [pl.BlockSpec((step_rows, half), lambda i, *_: (i, 0))]
    out_shape = [jax.ShapeDtypeStruct((n_rows, half), jnp.uint32)]
    n_f32_slots, n_bf_slots = (1, 2) if reuse else (2, 1)
    scratch = [
        pltpu.VMEM((n_f32_slots, d, two_f), jnp.float32),
        pltpu.VMEM((n_f32_slots, d_ff, d), jnp.float32),
        pltpu.VMEM((n_bf_slots, d, two_f), jnp.bfloat16),
        pltpu.VMEM((n_bf_slots, d_ff, d), jnp.bfloat16),
        pltpu.SemaphoreType.DMA((2, 2)),
    ]
    if reuse:
        wgu16, wd16, have, have_next = reuse
        scalars += [have, have_next]
        operands += [wgu16, wd16]
        in_specs += [any_spec, any_spec]
        scratch += [pltpu.SemaphoreType.DMA((2, 2))]
    if publish:
        out_specs += [any_spec, any_spec]
        out_shape += [jax.ShapeDtypeStruct((n_exp, d, two_f), jnp.bfloat16),
                      jax.ShapeDtypeStruct((n_exp, d_ff, d), jnp.bfloat16)]
        scratch += [pltpu.SemaphoreType.DMA((2,)), pltpu.SMEM((1,), jnp.int32)]
    grid_spec = pltpu.PrefetchScalarGridSpec(
        num_scalar_prefetch=len(scalars),
        grid=(n_blocks // BLOCKS_PER_STEP,),
        in_specs=in_specs,
        out_specs=tuple(out_specs),
        scratch_shapes=scratch,
    )
    out = pl.pallas_call(
        functools.partial(_experts_kernel, publish=publish, reuse=bool(reuse)),
        grid_spec=grid_spec,
        out_shape=tuple(out_shape),
        compiler_params=pltpu.CompilerParams(
            dimension_semantics=("arbitrary",), vmem_limit_bytes=VMEM_LIMIT),
        name="experts",
    )(*scalars, *operands)
    return out if publish else out[0]


def _combine_kernel(rows_ref, gate_ref, h_ref, gfin_ref, *rest):
    y_ref = rest[-1]
    acc = h_ref[...]
    for k in range(TOP_K):
        acc = acc + gate_ref[:, k:k + 1] * _unpack_bf16_pairs(rows_ref[k])
    y_ref[...] = _rms(acc, gfin_ref[...])


def _combine(rows4, gate_t, h2d, gfin, *, tok0=0, y_prev=None, y_rows=None, y_tok0=0):
    d = h2d.shape[1]
    n, half = rows4.shape[1:]
    n_tok = h2d.shape[0] if y_rows is None else y_rows
    tile = min(n, COMBINE_TILE)
    t0 = tok0 // tile
    y0 = y_tok0 // tile
    in_specs = [
        pl.BlockSpec((TOP_K, tile, half), lambda i: (0, i, 0)),
        pl.BlockSpec((tile, gate_t.shape[1]), lambda i: (t0 + i, 0)),
        pl.BlockSpec((tile, d), lambda i: (t0 + i, 0)),
        pl.BlockSpec((1, d), lambda i: (0, 0)),
    ]
    args = [rows4, gate_t, h2d, gfin]
    aliases = {}
    if y_prev is not None:
        in_specs.append(pl.BlockSpec(memory_space=pl.ANY))
        args.append(y_prev)
        aliases = {4: 0}
    return pl.pallas_call(
        _combine_kernel,
        grid=(n // tile,),
        in_specs=in_specs,
        out_specs=pl.BlockSpec((tile, d), lambda i: (y0 + i, 0)),
        out_shape=jax.ShapeDtypeStruct((n_tok, d), jnp.float32),
        input_output_aliases=aliases,
        compiler_params=pltpu.CompilerParams(
            dimension_semantics=("arbitrary",), vmem_limit_bytes=VMEM_LIMIT),
        name="combine",
    )(*args)


def _pad_hist(hist, rows):
    return jnp.pad(hist, ((0, 0), (rows - hist.shape[1], 0), (0, 0)))


def kernel(x_prompt, x_sample, state_conv_a, state_conv_b, norm_mix_g, w_in, conv_a_w, conv_b_w, conv_b_b, conv_ln_g, conv_ln_b, out_norm_a_g, out_norm_b_g, w_out, norm_ffn_g, w_router, b_router, w_gate_up, b_gate_up, w_down, b_down, final_norm_g):
    depth = w_in.shape[0]
    assert depth == 1, "single-layer trunk"
    bf16 = jnp.bfloat16
    bsz, seq, d = x_prompt.shape
    dec_b, dec_s, _ = x_sample.shape
    d_a, d_b = conv_a_w.shape[-1], conv_b_w.shape[-1]
    n_exp = w_router.shape[-1]

    params = (norm_mix_g[0][None], w_in[0].astype(bf16), conv_a_w[0], conv_b_w[0], conv_b_b[0][None],
              conv_ln_g[0][None], conv_ln_b[0][None], out_norm_a_g[0][None], out_norm_b_g[0][None],
              w_out[0].astype(bf16), norm_ffn_g[0][None], w_router[0].T.astype(bf16),
              b_router[0][:, None])

    zero_a = jnp.zeros((bsz, HIST_A_ROWS, d_a), jnp.float32)
    zero_b = jnp.zeros((bsz, HIST_B_ROWS, d_b), jnp.float32)
    cnt0 = jnp.zeros((n_exp, 1), jnp.float32)
    gfin = final_norm_g[None]
    ts = min(seq, MIXER_ROWS)
    n_p = bsz * seq

    sizes = [bsz * f // sum(GROUP_SPLIT) for f in GROUP_SPLIT]
    if bsz % sum(GROUP_SPLIT) != 0:
        sizes = [bsz]
    groups = []
    tok_start = []
    for g, per_b in enumerate(sizes):
        b0 = sum(sizes[:g])
        tok_start.append(b0 * seq)
        h, hnp, idx, gate, rank, cnt, na, nb_ = _mixer(
            x_prompt, zero_a, zero_b, cnt0, params, nb=1, ts=ts, b0=b0, bsz=per_b)
        parts = [dict(h=h.reshape(per_b * seq, d), hnp=hnp, idx=idx, gate_t=gate, rank=rank)]
        states = [(na, nb_)]
        if g == len(sizes) - 1:
            h, hnp, idx, gate, rank, cnt, na_s, nb_s = _mixer(
                x_sample, _pad_hist(state_conv_a[0], HIST_A_ROWS), _pad_hist(state_conv_b[0], HIST_B_ROWS),
                cnt, params, nb=dec_b, ts=dec_s)
            parts.append(dict(h=h.reshape(dec_b * dec_s, d), hnp=hnp, idx=idx, gate_t=gate, rank=rank))
        n_pairs = sum(p["hnp"].shape[0] for p in parts) * TOP_K
        plan, pad_start, n_blocks, has, reuse_tables = _block_plan(cnt[:, 0].astype(jnp.int32), n_pairs, n_exp)
        pos = _positions(pad_start, [(p["idx"], p["rank"]) for p in parts])
        xs = _sc_dispatch([(q, p["hnp"]) for q, p in zip(pos, parts)], n_blocks * MOE_BLOCK)
        groups.append(dict(parts=parts, pos=pos, plan=plan, xs=xs, states=states, has=has,
                           reuse_tables=reuse_tables))

    weights = (w_gate_up[0], b_gate_up[0], w_down[0], b_down[0])
    pub = groups[-1]
    if len(groups) > 1:
        pub["rows"], wgu16, wd16 = _experts(pub["plan"], pub["xs"], *weights, publish=True)
        for grp in groups[:-1]:
            grp["rows"] = _experts(grp["plan"], grp["xs"], *weights,
                                   reuse=(wgu16, wd16) + grp["reuse_tables"](pub["has"]))
    else:
        pub["rows"] = _experts(pub["plan"], pub["xs"], *weights)

    gathered = []
    for g, grp in enumerate(groups):
        pos0 = grp["pos"][0]
        n_g = pos0.shape[1]
        want = 1 if (g == len(groups) - 1 and len(groups) > 1) else COMBINE_CHUNKS
        n_chunks = want if n_g % (want * SC_CHUNK * SC_WORKERS) == 0 else 1
        per = n_g // n_chunks
        for c in range(n_chunks):
            pos_parts = [pos0[:, c * per:(c + 1) * per]] + (grp["pos"][1:] if c == 0 else [])
            gathered.append((g, c * per, _sc_gather(pos_parts, grp["rows"])))

    y_p = None
    for g, tok0, got in sorted(gathered, key=lambda t: (t[0] != len(groups) - 1, t[0], t[1])):
        grp = groups[g]
        main = grp["parts"][0]
        if len(got) > 1:
            smp = grp["parts"][1]
            y_s = _combine(got[1], smp["gate_t"], smp["h"], gfin).reshape(dec_b, dec_s, d)
        y_p = _combine(got[0], main["gate_t"], main["h"], gfin, tok0=tok0,
                       y_prev=y_p, y_rows=n_p, y_tok0=tok_start[g] + tok0)
    y_p = y_p.reshape(bsz, seq, d)

    na_p = jnp.concatenate([grp["states"][0][0] for grp in groups], axis=0)
    nb_p = jnp.concatenate([grp["states"][0][1] for grp in groups], axis=0)
    return (y_p, y_s, na_p[None], nb_p[None], na_s[None], nb_s[None])


def _block_plan(counts, n_pairs, n_exp):
    padded = (counts + MOE_BLOCK - 1) // MOE_BLOCK * MOE_BLOCK
    pad_end = jnp.cumsum(padded)
    pad_start = pad_end - padded
    n_blocks = -(-n_pairs // MOE_BLOCK) + n_exp
    n_blocks = -(-n_blocks // BLOCKS_PER_STEP) * BLOCKS_PER_STEP
    blk_start = jnp.arange(n_blocks, dtype=jnp.int32) * MOE_BLOCK
    blk_exp = jnp.minimum(jnp.sum(blk_start[:, None] >= pad_end[None, :], axis=1),
                          n_exp - 1).astype(jnp.int32)
    n_active = (pad_end[-1:] // MOE_BLOCK).astype(jnp.int32)
    e_ids = jnp.arange(n_exp, dtype=jnp.int32)
    has = padded > 0
    slot_e = (jnp.cumsum(has.astype(jnp.int32)) - 1) % 2
    later = jnp.where(has, e_ids, n_exp)
    next_e = jnp.concatenate([lax.cummin(later, reverse=True)[1:], jnp.full((1,), n_exp, jnp.int32)])
    next_e = jnp.where(next_e >= n_exp, -1, next_e)
    blk_onehot = blk_exp[:, None] == e_ids[None, :]

    def per_block(table):
        return jnp.sum(jnp.where(blk_onehot, table[None, :], 0), axis=1).astype(jnp.int32)

    blk_first = ((blk_start == per_block(pad_start)) & (blk_start < pad_end[-1])).astype(jnp.int32)
    plan = (blk_exp, blk_first, per_block(slot_e), per_block(next_e), n_active)

    def reuse_tables(has_pub):
        pub = has_pub.astype(jnp.int32)
        pub_next = jnp.sum(jnp.where(next_e[:, None] == e_ids[None, :], pub[None, :], 0), axis=1)
        return per_block(pub), per_block(pub_next)

    return plan, pad_start.astype(jnp.int32), n_blocks, has, reuse_tables
```

```python
import functools

import jax
import jax.numpy as jnp
from jax import lax
from jax.experimental import pallas as pl
from jax.experimental.pallas import tpu as pltpu
from jax.experimental.pallas import tpu_sc as plsc

SC_CORES = 2
SC_SUBCORES = 16
SC_WORKERS = SC_CORES * SC_SUBCORES
SC_CHUNK = 128
CONV_A_W = 3
CONV_B_W = 31
TOP_K = 4
SWIGLU_LIMIT = 7.0
SWIGLU_ALPHA = 1.702
NORM_EPS = 1e-5
MOE_BLOCK = 256
BLOCKS_PER_STEP = 4
SUBLANES = 8
LANES = 128
CONV_ROW_CHUNK = 128
HIST_A_ROWS = 8
HIST_B_ROWS = 32
MIXER_ROWS = 1024
PROJ_SECTION = 256
COMBINE_TILE = 1024
POS_TILE = 4096
GROUP_SPLIT = (1, 3)
COMBINE_CHUNKS = 1
VMEM_LIMIT = 56 * 1024 * 1024


def _rms(x, g):
    return x * lax.rsqrt(jnp.mean(x * x, axis=-1, keepdims=True) + NORM_EPS) * g


def _zero_tile_from(v):
    bits = pltpu.bitcast(v[0:SUBLANES, 0:LANES], jnp.uint32)
    return pltpu.bitcast((bits >> 16) >> 16, jnp.float32)


def _pack_bf16_pairs(a):
    w = a.shape[-1] // 2
    lo = pltpu.bitcast(a[:, :w].astype(jnp.bfloat16).astype(jnp.float32), jnp.uint32)
    hi = pltpu.bitcast(a[:, w:].astype(jnp.bfloat16).astype(jnp.float32), jnp.uint32)
    return (lo >> 16) | (hi & jnp.uint32(0xFFFF0000))


def _unpack_bf16_pairs(wd):
    lo = pltpu.bitcast(wd << 16, jnp.float32)
    hi = pltpu.bitcast(wd & jnp.uint32(0xFFFF0000), jnp.float32)
    return jnp.concatenate([lo, hi], axis=-1)


def _conv_b_chunk(ub_buf, cbw_ref, n_i, r0, rc, c0):
    cs = slice(c0, c0 + LANES)
    lead = HIST_B_ROWS - (CONV_B_W - 1)
    acc = None
    for b in range(SUBLANES):
        rows = rc if b == 0 else rc + SUBLANES
        q = None
        for a in range((HIST_B_ROWS + SUBLANES) // SUBLANES):
            k = SUBLANES * a + b - lead
            if k < 0 or k >= CONV_B_W:
                continue
            start = r0 + SUBLANES * a
            term = cbw_ref[k:k + 1, cs] * ub_buf[n_i, start:start + rows, cs]
            q = term if q is None else q + term
        part = q if b == 0 else q[b:b + rc]
        acc = part if acc is None else acc + part
    return acc


def _mixer_kernel(x_ref, hista_ref, histb_ref, cnt0_ref, tri_ref,
                  gmix_ref, win_ref, caw_ref, cbw_ref, cbb_ref, lng_ref, lnb_ref,
                  ga_ref, gb_ref, wout_ref, gffn_ref, wrt_ref, br_ref,
                  h_ref, hnp_ref, idx_ref, gate_ref, rank_ref, cnt_ref, newa_ref, newb_ref,
                  ua_buf, ub_buf, cb_buf, cnt_acc, *, nb, ts, d_a, d_b, n_exp):
    b = pl.program_id(0)
    s = pl.program_id(1)
    m = nb * ts
    d = x_ref.shape[-1]

    @pl.when(s == 0)
    def _():
        ua_buf[:, 0:HIST_A_ROWS, :] = hista_ref[...]
        ub_buf[:, 0:HIST_B_ROWS, :] = histb_ref[...]

    @pl.when(s != 0)
    def _():
        ua_buf[:, 0:HIST_A_ROWS, :] = ua_buf[:, ts:ts + HIST_A_ROWS, :]
        ub_buf[:, 0:HIST_B_ROWS, :] = ub_buf[:, ts:ts + HIST_B_ROWS, :]

    @pl.when((b == 0) & (s == 0))
    def _():
        cnt_acc[...] = cnt0_ref[...]

    x = x_ref[...].reshape(m, d)
    n = _rms(x, gmix_ref[...]).astype(jnp.bfloat16)
    def proj(c0, width):
        return jnp.dot(n, win_ref[:, c0:c0 + width], preferred_element_type=jnp.float32)

    sec = min(d_b, PROJ_SECTION)
    for c0 in range(0, d_b, sec):
        v_b = proj(3 * d_a + c0, sec)
        g_b = proj(3 * d_a + d_b + c0, sec)
        ub_buf[:, HIST_B_ROWS:HIST_B_ROWS + ts, c0:c0 + sec] = (v_b * jax.nn.sigmoid(g_b)).reshape(nb, ts, sec)

    gate_c = proj(d_a, d_a)
    xt = proj(2 * d_a, d_a)
    gate_b = proj(0, d_a)
    ua_buf[:, HIST_A_ROWS:HIST_A_ROWS + ts, :] = (gate_c * xt).reshape(nb, ts, d_a)
    conv_a = jnp.zeros((nb, ts, d_a), jnp.float32)
    for k in range(CONV_A_W):
        off = HIST_A_ROWS - (CONV_A_W - 1) + k
        conv_a = conv_a + caw_ref[k:k + 1, :] * ua_buf[:, off:off + ts, :]
    y_a = gate_b * conv_a.reshape(m, d_a)

    mix_a = jnp.dot(_rms(y_a, ga_ref[...]).astype(jnp.bfloat16), wout_ref[0:d_a, :],
                    preferred_element_type=jnp.float32)

    rc = min(ts, CONV_ROW_CHUNK)
    chunks = [(n_i, r0, c0) for n_i in range(nb) for r0 in range(0, ts, rc) for c0 in range(0, d_b, LANES)]
    anchors = {len(chunks) * 2 // 8: gate_c, len(chunks) * 3 // 8: xt, len(chunks) * 5 // 8: gate_b,
               len(chunks) - 1: mix_a}
    for ci, (n_i, r0, c0) in enumerate(chunks):
        acc = _conv_b_chunk(ub_buf, cbw_ref, n_i, r0, rc, c0)
        if ci in anchors:
            acc = (acc.reshape(rc // SUBLANES, SUBLANES, LANES) + _zero_tile_from(anchors[ci])[None]
                   ).reshape(rc, LANES)
        cb_buf[n_i, r0:r0 + rc, c0:c0 + LANES] = acc
    cb = cb_buf[...].reshape(m, d_b) + cbb_ref[...]
    mu = jnp.mean(cb, axis=-1, keepdims=True)
    xc = cb - mu
    var = jnp.mean(xc * xc, axis=-1, keepdims=True)
    ln = xc * lax.rsqrt(var + NORM_EPS) * lng_ref[...] + lnb_ref[...]
    y_b = ln * jax.nn.sigmoid(ln)

    mix_b = jnp.dot(_rms(y_b, gb_ref[...]).astype(jnp.bfloat16), wout_ref[d_a:d_a + d_b, :],
                    preferred_element_type=jnp.float32)
    h = x + (mix_a + mix_b)
    h_ref[...] = h.reshape(nb, ts, d)

    hn = _rms(h, gffn_ref[...])
    hnp_ref[...] = _pack_bf16_pairs(hn)

    logits = lax.dot_general(wrt_ref[...], hn.astype(jnp.bfloat16), (((1,), (1,)), ((), ())),
                             preferred_element_type=jnp.float32) + br_ref[...]
    e_iota = lax.broadcasted_iota(jnp.int32, (n_exp, m), 0)
    cur = logits
    vals, idxs, sels = [], [], []
    for _ in range(TOP_K):
        mx = jnp.max(cur, axis=0, keepdims=True)
        ix = jnp.min(jnp.where(cur == mx, e_iota, n_exp), axis=0, keepdims=True)
        sel = e_iota == ix
        vals.append(mx)
        idxs.append(ix)
        sels.append(sel)
        cur = jnp.where(sel, -jnp.inf, cur)
    exps = [jnp.exp(v - vals[0]) for v in vals]
    denom = exps[0] + exps[1] + exps[2] + exps[3]
    gates = [e / denom for e in exps]

    onehot = jnp.zeros((n_exp, m), jnp.float32)
    for sel in sels:
        onehot = onehot + sel.astype(jnp.float32)
    before = jnp.dot(onehot.astype(jnp.bfloat16), tri_ref[...],
                     preferred_element_type=jnp.float32) + cnt_acc[...]
    ranks = [jnp.sum(jnp.where(sel, before, 0.0), axis=0, keepdims=True) for sel in sels]
    new_cnt = cnt_acc[...] + jnp.sum(onehot, axis=1, keepdims=True)
    cnt_acc[...] = new_cnt
    cnt_ref[...] = new_cnt

    idx_ref[...] = jnp.concatenate(idxs, axis=0)
    gate_ref[...] = jnp.concatenate(gates + [jnp.zeros((SUBLANES - TOP_K, m), jnp.float32)], axis=0).T
    rank_ref[...] = jnp.concatenate(ranks, axis=0).astype(jnp.int32)

    @pl.when(s == pl.num_programs(1) - 1)
    def _():
        newa_ref[...] = ua_buf[:, ts + HIST_A_ROWS - (CONV_A_W - 1):ts + HIST_A_ROWS, :]
        newb_ref[...] = ub_buf[:, ts + HIST_B_ROWS - (CONV_B_W - 1):ts + HIST_B_ROWS, :]


def _mixer(x, hist_a, hist_b, cnt0, params, *, nb, ts, b0=0, bsz=None):
    (gmix, win, caw, cbw, cbb, lng, lnb, ga, gb, wout, gffn, wrt, br) = params
    _, seq, d = x.shape
    bsz = x.shape[0] if bsz is None else bsz
    d_a, d_b = caw.shape[-1], cbw.shape[-1]
    n_exp = wrt.shape[0]
    m = nb * ts
    n_tok = bsz * seq
    g0 = b0 // nb
    grid = (bsz // nb, seq // ts)
    tri = (jnp.arange(m)[:, None] < jnp.arange(m)[None, :]).astype(jnp.bfloat16)

    def const(shape):
        return pl.BlockSpec(shape, lambda i, j: (0,) * len(shape))

    tok_map = lambda i, j: (0, i * (seq // ts) + j)
    kern = functools.partial(_mixer_kernel, nb=nb, ts=ts, d_a=d_a, d_b=d_b, n_exp=n_exp)
    out_shape = (
        jax.ShapeDtypeStruct((bsz, seq, d), jnp.float32),
        jax.ShapeDtypeStruct((n_tok, d // 2), jnp.uint32),
        jax.ShapeDtypeStruct((TOP_K, n_tok), jnp.int32),
        jax.ShapeDtypeStruct((n_tok, SUBLANES), jnp.float32),
        jax.ShapeDtypeStruct((TOP_K, n_tok), jnp.int32),
        jax.ShapeDtypeStruct((n_exp, 1), jnp.float32),
        jax.ShapeDtypeStruct((bsz, CONV_A_W - 1, d_a), jnp.float32),
        jax.ShapeDtypeStruct((bsz, CONV_B_W - 1, d_b), jnp.float32),
    )
    return pl.pallas_call(
        kern,
        grid=grid,
        in_specs=[
            pl.BlockSpec((nb, ts, d), lambda i, j: (g0 + i, j, 0)),
            pl.BlockSpec((nb, HIST_A_ROWS, d_a), lambda i, j: (g0 + i, 0, 0)),
            pl.BlockSpec((nb, HIST_B_ROWS, d_b), lambda i, j: (g0 + i, 0, 0)),
            const((n_exp, 1)),
            const((m, m)),
            const((1, d)), const(win.shape), const(caw.shape), const(cbw.shape), const((1, d_b)),
            const((1, d_b)), const((1, d_b)), const((1, d_a)), const((1, d_b)), const(wout.shape),
            const((1, d)), const(wrt.shape), const((n_exp, 1)),
        ],
        out_specs=(
            pl.BlockSpec((nb, ts, d), lambda i, j: (i, j, 0)),
            pl.BlockSpec((m, d // 2), lambda i, j: (i * (seq // ts) + j, 0)),
            pl.BlockSpec((TOP_K, m), tok_map),
            pl.BlockSpec((m, SUBLANES), lambda i, j: (i * (seq // ts) + j, 0)),
            pl.BlockSpec((TOP_K, m), tok_map),
            pl.BlockSpec((n_exp, 1), lambda i, j: (0, 0)),
            pl.BlockSpec((nb, CONV_A_W - 1, d_a), lambda i, j: (i, 0, 0)),
            pl.BlockSpec((nb, CONV_B_W - 1, d_b), lambda i, j: (i, 0, 0)),
        ),
        out_shape=out_shape,
        scratch_shapes=[
            pltpu.VMEM((nb, HIST_A_ROWS + ts, d_a), jnp.float32),
            pltpu.VMEM((nb, HIST_B_ROWS + ts, d_b), jnp.float32),
            pltpu.VMEM((nb, ts, d_b), jnp.float32),
            pltpu.VMEM((n_exp, 1), jnp.float32),
        ],
        compiler_params=pltpu.CompilerParams(
            dimension_semantics=("arbitrary", "arbitrary"), vmem_limit_bytes=VMEM_LIMIT),
        name="mixer",
    )(x, hist_a, hist_b, cnt0, tri, gmix, win, caw, cbw, cbb, lng, lnb, ga, gb, wout, gffn, wrt, br)


def _positions_kernel(start_ref, *refs):
    n_parts = len(refs) // 3
    for p in range(n_parts):
        idx = refs[2 * p][...]
        pos = refs[2 * p + 1][...]
        for e in range(start_ref.shape[0]):
            pos = pos + jnp.where(idx == e, start_ref[e], 0)
        refs[2 * n_parts + p][...] = pos


def _positions(pad_start, parts):
    n_0 = parts[0][0].shape[1]
    tile = min(n_0, POS_TILE)
    specs = [pl.BlockSpec((TOP_K, tile), lambda i: (0, i))]
    specs += [pl.BlockSpec((TOP_K, idx.shape[1]), lambda i: (0, 0)) for idx, _ in parts[1:]]
    in_specs = [pl.BlockSpec(memory_space=pltpu.SMEM)]
    args = [pad_start]
    for spec, (idx, rank) in zip(specs, parts):
        in_specs += [spec, spec]
        args += [idx, rank]
    out = pl.pallas_call(
        _positions_kernel,
        grid=(n_0 // tile,),
        in_specs=in_specs,
        out_specs=tuple(specs),
        out_shape=tuple(jax.ShapeDtypeStruct(idx.shape, jnp.int32) for idx, _ in parts),
        compiler_params=pltpu.CompilerParams(dimension_semantics=("arbitrary",)),
        name="positions",
    )(*args)
    return list(out)


def _sc_mesh():
    return plsc.VectorSubcoreMesh(core_axis_name="c", subcore_axis_name="s",
                                  num_cores=SC_CORES, num_subcores=SC_SUBCORES)


def _sc_worker_id():
    return lax.axis_index("s") * SC_CORES + lax.axis_index("c")


def _sc_chunks(n_tok, wid):
    n_chunks = n_tok // SC_CHUNK
    assert n_tok % SC_CHUNK == 0
    if n_chunks % SC_WORKERS == 0:
        per_w = n_chunks // SC_WORKERS
        return None, per_w, lambda i: (wid * per_w + i) * SC_CHUNK
    assert n_chunks <= SC_WORKERS
    return wid < n_chunks, 1, lambda i: wid * SC_CHUNK


def _sc_for_each_chunk(n_tok, fn):
    wid = _sc_worker_id()
    pred, trips, base = _sc_chunks(n_tok, wid)

    def run():
        @pl.loop(0, trips)
        def _(i):
            fn(pl.multiple_of(base(i), SC_CHUNK))

    if pred is None:
        run()
    else:
        pl.when(pred)(run)


def _sc_dispatch(parts, n_rows):
    half = parts[0][1].shape[1]
    n_parts = len(parts)

    @functools.partial(
        pl.kernel, mesh=_sc_mesh(),
        out_type=jax.ShapeDtypeStruct((n_rows, half), jnp.uint32),
        scratch_types=[pltpu.VMEM((TOP_K, SC_CHUNK), jnp.int32),
                       pltpu.VMEM((SC_CHUNK, half), jnp.uint32),
                       pltpu.SemaphoreType.DMA],
        compiler_params=pltpu.CompilerParams(use_tc_tiling_on_sc=True),
        name="sc_dispatch")
    def k(*refs):
        xs_hbm, idx_v, rows_v, sem = refs[2 * n_parts:]

        def move(pos_hbm, src_hbm):
            def fn(base):
                pltpu.sync_copy(pos_hbm.at[:, pl.ds(base, SC_CHUNK)], idx_v)
                pltpu.sync_copy(src_hbm.at[pl.ds(base, SC_CHUNK)], rows_v)
                copies = [pltpu.async_copy(rows_v, xs_hbm.at[idx_v.at[j]], sem) for j in range(TOP_K)]
                for c in copies:
                    c.wait()
            return fn

        for p, (_, hnp) in enumerate(parts):
            _sc_for_each_chunk(hnp.shape[0], move(refs[2 * p], refs[2 * p + 1]))

    return k(*[a for part in parts for a in part])


def _sc_gather(pos_parts, rows):
    half = rows.shape[1]
    n_parts = len(pos_parts)

    @functools.partial(
        pl.kernel, mesh=_sc_mesh(),
        out_type=tuple(jax.ShapeDtypeStruct((TOP_K, p.shape[1], half), jnp.uint32) for p in pos_parts),
        scratch_types=[pltpu.VMEM((TOP_K, SC_CHUNK), jnp.int32),
                       pltpu.VMEM((SC_CHUNK, half), jnp.uint32),
                       pltpu.SemaphoreType.DMA],
        compiler_params=pltpu.CompilerParams(use_tc_tiling_on_sc=True),
        name="sc_gather")
    def k(*refs):
        pos_hbms, rows_hbm = refs[:n_parts], refs[n_parts]
        out_hbms = refs[n_parts + 1:2 * n_parts + 1]
        idx_v, buf_v, sem = refs[2 * n_parts + 1:]

        def move(pos_hbm, out_hbm):
            def fn(base):
                pltpu.sync_copy(pos_hbm.at[:, pl.ds(base, SC_CHUNK)], idx_v)
                for j in range(TOP_K):
                    pltpu.async_copy(rows_hbm.at[idx_v.at[j]], buf_v, sem).wait()
                    pltpu.sync_copy(buf_v, out_hbm.at[j, pl.ds(base, SC_CHUNK)])
            return fn

        for pos_hbm, out_hbm, p in zip(pos_hbms, out_hbms, pos_parts):
            _sc_for_each_chunk(p.shape[1], move(pos_hbm, out_hbm))

    out = k(*pos_parts, rows)
    return out if isinstance(out, (tuple, list)) else (out,)


def _experts_kernel(*refs, publish, reuse):
    blk_exp_ref, first_ref, slot_ref, next_ref, n_active_ref = refs[:5]
    refs = refs[5:]
    if reuse:
        have_ref, have_next_ref = refs[:2]
        refs = refs[2:]
    xs_ref, wgu_hbm, bgu_ref, wd_hbm, bd_ref = refs[:5]
    refs = refs[5:]
    if reuse:
        wgu16_hbm, wd16_hbm = refs[:2]
        refs = refs[2:]
    out_ref = refs[0]
    refs = refs[1:]
    if publish:
        wgu16_out, wd16_out = refs[:2]
        refs = refs[2:]
    wgu_f32, wd_f32, wgu_bf, wd_bf, sems = refs[:5]
    refs = refs[5:]
    if publish:
        out_sems, pending = refs
    if reuse:
        sems16, = refs
    d_ff = wd_bf.shape[-2]

    def bf_slot(parity):
        return parity if reuse else 0

    def f32_slot(parity):
        return 0 if reuse else parity

    def fetch(e, parity):
        sl = f32_slot(parity)
        return (pltpu.make_async_copy(wgu_hbm.at[e], wgu_f32.at[sl], sems.at[0, sl]),
                pltpu.make_async_copy(wd_hbm.at[e], wd_f32.at[sl], sems.at[1, sl]))

    def fetch16(e, parity):
        return (pltpu.make_async_copy(wgu16_hbm.at[e], wgu_bf.at[parity], sems16.at[0, parity]),
                pltpu.make_async_copy(wd16_hbm.at[e], wd_bf.at[parity], sems16.at[1, parity]))

    def publish_copies(e):
        return (pltpu.make_async_copy(wgu_bf.at[0], wgu16_out.at[e], out_sems.at[0]),
                pltpu.make_async_copy(wd_bf.at[0], wd16_out.at[e], out_sems.at[1]))

    def wait_published():
        @pl.when(pending[0] == 1)
        def _():
            for c in publish_copies(0):
                c.wait()
            pending[0] = 0

    def start_fetch(e, parity, have):
        if reuse:
            @pl.when(have == 1)
            def _():
                for c in fetch16(e, parity):
                    c.start()

            @pl.when(have != 1)
            def _():
                for c in fetch(e, parity):
                    c.start()
        else:
            for c in fetch(e, parity):
                c.start()

    def finish_fetch(e, parity, have):
        def from_f32():
            for c in fetch(e, parity):
                c.wait()
            if publish:
                wait_published()
            wgu_bf[bf_slot(parity)] = wgu_f32[f32_slot(parity)].astype(jnp.bfloat16)
            wd_bf[bf_slot(parity)] = wd_f32[f32_slot(parity)].astype(jnp.bfloat16)
            if publish:
                for c in publish_copies(e):
                    c.start()
                pending[0] = 1

        if reuse:
            @pl.when(have == 1)
            def _():
                for c in fetch16(e, parity):
                    c.wait()

            pl.when(have != 1)(from_f32)
        else:
            from_f32()

    def load_weights_if_first(i):
        parity = slot_ref[i]
        e = blk_exp_ref[i]
        have = have_ref[i] if reuse else 0

        @pl.when(first_ref[i] == 1)
        def _():
            @pl.when(i == 0)
            def _():
                if publish:
                    pending[0] = 0
                start_fetch(e, parity, have)

            finish_fetch(e, parity, have)

            @pl.when(next_ref[i] >= 0)
            def _():
                start_fetch(next_ref[i], 1 - parity, have_next_ref[i] if reuse else 0)

    def compute(rows, i):
        e = blk_exp_ref[i]
        sl = bf_slot(slot_ref[i])
        x = _unpack_bf16_pairs(xs_ref[rows, :]).astype(jnp.bfloat16)
        gu = jnp.dot(x, wgu_bf[sl], preferred_element_type=jnp.float32) + bgu_ref[e]
        g = jnp.minimum(gu[:, :d_ff], SWIGLU_LIMIT)
        u = jnp.clip(gu[:, d_ff:], -SWIGLU_LIMIT, SWIGLU_LIMIT)
        act = g * jax.nn.sigmoid(SWIGLU_ALPHA * g) * (u + 1.0)
        o = jnp.dot(act.astype(jnp.bfloat16), wd_bf[sl], preferred_element_type=jnp.float32) + bd_ref[e]
        out_ref[rows, :] = _pack_bf16_pairs(o)

    i0 = pl.program_id(0) * BLOCKS_PER_STEP
    n_active = n_active_ref[0]

    def run(sub0, n_sub):
        first = i0 + sub0
        rows = slice(sub0 * MOE_BLOCK, (sub0 + n_sub) * MOE_BLOCK)
        if n_sub == 1:
            @pl.when(first < n_active)
            def _():
                load_weights_if_first(first)
                compute(rows, first)

            @pl.when(first >= n_active)
            def _():
                out_ref[rows, :] = jnp.zeros((MOE_BLOCK, out_ref.shape[1]), out_ref.dtype)
            return

        same = first + n_sub - 1 < n_active
        for sub in range(1, n_sub):
            same = same & (blk_exp_ref[first + sub] == blk_exp_ref[first])

        @pl.when(same)
        def _():
            load_weights_if_first(first)
            compute(rows, first)

        @pl.when(jnp.logical_not(same))
        def _():
            run(sub0, n_sub // 2)
            run(sub0 + n_sub // 2, n_sub // 2)

    run(0, BLOCKS_PER_STEP)

    if publish:
        @pl.when(pl.program_id(0) == pl.num_programs(0) - 1)
        def _():
            wait_published()


def _experts(plan, xs, wgu, bgu, wd, bd, *, publish=False, reuse=None):
    blk_exp, first, slot, nxt, n_active = plan
    n_rows, half = xs.shape
    n_exp, d, two_f = wgu.shape
    d_ff = wd.shape[1]
    n_blocks = n_rows // MOE_BLOCK
    step_rows = MOE_BLOCK * BLOCKS_PER_STEP
    assert n_rows % step_rows == 0
    any_spec = pl.BlockSpec(memory_space=pl.ANY)
    scalars = [blk_exp, first, slot, nxt, n_active]
    operands = [xs, wgu, bgu.reshape(n_exp, 1, two_f), wd, bd.reshape(n_exp, 1, d)]
    in_specs = [
        pl.BlockSpec((step_rows, half), lambda i, *_: (i, 0)),
        any_spec,
        pl.BlockSpec((n_exp, 1, two_f), lambda i, *_: (0, 0, 0)),
        any_spec,
        pl.BlockSpec((n_exp, 1, d), lambda i, *_: (0, 0, 0)),
    ]
    out_specs = [pl.BlockSpec((step_rows, half), lambda i, *_: (i, 0))]
    out_shape = [jax.ShapeDtypeStruct((n_rows, half), jnp.uint32)]
    n_f32_slots, n_bf_slots = (1, 2) if reuse else (2, 1)
    scratch = [
        pltpu.VMEM((n_f32_slots, d, two_f), jnp.float32),
        pltpu.VMEM((n_f32_slots, d_ff, d), jnp.float32),
        pltpu.VMEM((n_bf_slots, d, two_f), jnp.bfloat16),
        pltpu.VMEM((n_bf_slots, d_ff, d), jnp.bfloat16),
        pltpu.SemaphoreType.DMA((2, 2)),
    ]
    if reuse:
        wgu16, wd16, have, have_next = reuse
        scalars += [have, have_next]
        operands += [wgu16, wd16]
        in_specs += [any_spec, any_spec]
        scratch += [pltpu.SemaphoreType.DMA((2, 2))]
    if publish:
        out_specs += [any_spec, any_spec]
        out_shape += [jax.ShapeDtypeStruct((n_exp, d, two_f), jnp.bfloat16),
                      jax.ShapeDtypeStruct((n_exp, d_ff, d), jnp.bfloat16)]
        scratch += [pltpu.SemaphoreType.DMA((2,)), pltpu.SMEM((1,), jnp.int32)]
    grid_spec = pltpu.PrefetchScalarGridSpec(
        num_scalar_prefetch=len(scalars),
        grid=(n_blocks // BLOCKS_PER_STEP,),
        in_specs=in_specs,
        out_specs=tuple(out_specs),
        scratch_shapes=scratch,
    )
    out = pl.pallas_call(
        functools.partial(_experts_kernel, publish=publish, reuse=bool(reuse)),
        grid_spec=grid_spec,
        out_shape=tuple(out_shape),
        compiler_params=pltpu.CompilerParams(
            dimension_semantics=("arbitrary",), vmem_limit_bytes=VMEM_LIMIT),
        name="experts",
    )(*scalars, *operands)
    return out if publish else out[0]


def _combine_kernel(rows_ref, gate_ref, h_ref, gfin_ref, *rest):
    y_ref = rest[-1]
    acc = h_ref[...]
    for k in range(TOP_K):
        acc = acc + gate_ref[:, k:k + 1] * _unpack_bf16_pairs(rows_ref[k])
    y_ref[...] = _rms(acc, gfin_ref[...])


def _combine(rows4, gate_t, h2d, gfin, *, tok0=0, y_prev=None, y_rows=None, y_tok0=0):
    d = h2d.shape[1]
    n, half = rows4.shape[1:]
    n_tok = h2d.shape[0] if y_rows is None else y_rows
    tile = min(n, COMBINE_TILE)
    t0 = tok0 // tile
    y0 = y_tok0 // tile
    in_specs = [
        pl.BlockSpec((TOP_K, tile, half), lambda i: (0, i, 0)),
        pl.BlockSpec((tile, gate_t.shape[1]), lambda i: (t0 + i, 0)),
        pl.BlockSpec((tile, d), lambda i: (t0 + i, 0)),
        pl.BlockSpec((1, d), lambda i: (0, 0)),
    ]
    args = [rows4, gate_t, h2d, gfin]
    aliases = {}
    if y_prev is not None:
        in_specs.append(pl.BlockSpec(memory_space=pl.ANY))
        args.append(y_prev)
        aliases = {4: 0}
    return pl.pallas_call(
        _combine_kernel,
        grid=(n // tile,),
        in_specs=in_specs,
        out_specs=pl.BlockSpec((tile, d), lambda i: (y0 + i, 0)),
        out_shape=jax.ShapeDtypeStruct((n_tok, d), jnp.float32),
        input_output_aliases=aliases,
        compiler_params=pltpu.CompilerParams(
            dimension_semantics=("arbitrary",), vmem_limit_bytes=VMEM_LIMIT),
        name="combine",
    )(*args)


def _pad_hist(hist, rows):
    return jnp.pad(hist, ((0, 0), (rows - hist.shape[1], 0), (0, 0)))


def kernel(x_prompt, x_sample, state_conv_a, state_conv_b, norm_mix_g, w_in, conv_a_w, conv_b_w, conv_b_b, conv_ln_g, conv_ln_b, out_norm_a_g, out_norm_b_g, w_out, norm_ffn_g, w_router, b_router, w_gate_up, b_gate_up, w_down, b_down, final_norm_g):
    depth = w_in.shape[0]
    assert depth == 1, "single-layer trunk"
    bf16 = jnp.bfloat16
    bsz, seq, d = x_prompt.shape
    dec_b, dec_s, _ = x_sample.shape
    d_a, d_b = conv_a_w.shape[-1], conv_b_w.shape[-1]
    n_exp = w_router.shape[-1]

    params = (norm_mix_g[0][None], w_in[0].astype(bf16), conv_a_w[0], conv_b_w[0], conv_b_b[0][None],
              conv_ln_g[0][None], conv_ln_b[0][None], out_norm_a_g[0][None], out_norm_b_g[0][None],
              w_out[0].astype(bf16), norm_ffn_g[0][None], w_router[0].T.astype(bf16),
              b_router[0][:, None])

    zero_a = jnp.zeros((bsz, HIST_A_ROWS, d_a), jnp.float32)
    zero_b = jnp.zeros((bsz, HIST_B_ROWS, d_b), jnp.float32)
    cnt0 = jnp.zeros((n_exp, 1), jnp.float32)
    gfin = final_norm_g[None]
    ts = min(seq, MIXER_ROWS)
    n_p = bsz * seq

    sizes = [bsz * f // sum(GROUP_SPLIT) for f in GROUP_SPLIT]
    if bsz % sum(GROUP_SPLIT) != 0:
        sizes = [bsz]
    groups = []
    tok_start = []
    for g, per_b in enumerate(sizes):
        b0 = sum(sizes[:g])
        tok_start.append(b0 * seq)
        h, hnp, idx, gate, rank, cnt, na, nb_ = _mixer(
            x_prompt, zero_a, zero_b, cnt0, params, nb=1, ts=ts, b0=b0, bsz=per_b)
        parts = [dict(h=h.reshape(per_b * seq, d), hnp=hnp, idx=idx, gate_t=gate, rank=rank)]
        states = [(na, nb_)]
        if g == len(sizes) - 1:
            h, hnp, idx, gate, rank, cnt, na_s, nb_s = _mixer(
                x_sample, _pad_hist(state_conv_a[0], HIST_A_ROWS), _pad_hist(state_conv_b[0], HIST_B_ROWS),
                cnt, params, nb=dec_b, ts=dec_s)
            parts.append(dict(h=h.reshape(dec_b * dec_s, d), hnp=hnp, idx=idx, gate_t=gate, rank=rank))
        n_pairs = sum(p["hnp"].shape[0] for p in parts) * TOP_K
        plan, pad_start, n_blocks, has, reuse_tables = _block_plan(cnt[:, 0].astype(jnp.int32), n_pairs, n_exp)
        pos = _positions(pad_start, [(p["idx"], p["rank"]) for p in parts])
        xs = _sc_dispatch([(q, p["hnp"]) for q, p in zip(pos, parts)], n_blocks * MOE_BLOCK)
        groups.append(dict(parts=parts, pos=pos, plan=plan, xs=xs, states=states, has=has,
                           reuse_tables=reuse_tables))

    weights = (w_gate_up[0], b_gate_up[0], w_down[0], b_down[0])
    pub = groups[-1]
    if len(groups) > 1:
        pub["rows"], wgu16, wd16 = _experts(pub["plan"], pub["xs"], *weights, publish=True)
        for grp in groups[:-1]:
            grp["rows"] = _experts(grp["plan"], grp["xs"], *weights,
                                   reuse=(wgu16, wd16) + grp["reuse_tables"](pub["has"]))
    else:
        pub["rows"] = _experts(pub["plan"], pub["xs"], *weights)

    gathered = []
    for g, grp in enumerate(groups):
        pos0 = grp["pos"][0]
        n_g = pos0.shape[1]
        want = 1 if (g == len(groups) - 1 and len(groups) > 1) else COMBINE_CHUNKS
        n_chunks = want if n_g % (want * SC_CHUNK * SC_WORKERS) == 0 else 1
        per = n_g // n_chunks
        for c in range(n_chunks):
            pos_parts = [pos0[:, c * per:(c + 1) * per]] + (grp["pos"][1:] if c == 0 else [])
            gathered.append((g, c * per, _sc_gather(pos_parts, grp["rows"])))

    y_p = None
    for g, tok0, got in sorted(gathered, key=lambda t: (t[0] != len(groups) - 1, t[0], t[1])):
        grp = groups[g]
        main = grp["parts"][0]
        if len(got) > 1:
            smp = grp["parts"][1]
            y_s = _combine(got[1], smp["gate_t"], smp["h"], gfin).reshape(dec_b, dec_s, d)
        y_p = _combine(got[0], main["gate_t"], main["h"], gfin, tok0=tok0,
                       y_prev=y_p, y_rows=n_p, y_tok0=tok_start[g] + tok0)
    y_p = y_p.reshape(bsz, seq, d)

    na_p = jnp.concatenate([grp["states"][0][0] for grp in groups], axis=0)
    nb_p = jnp.concatenate([grp["states"][0][1] for grp in groups], axis=0)
    return (y_p, y_s, na_p[None], nb_p[None], na_s[None], nb_s[None])


def _block_plan(counts, n_pairs, n_exp):
    padded = (counts + MOE_BLOCK - 1) // MOE_BLOCK * MOE_BLOCK
    pad_end = jnp.cumsum(padded)
    pad_start = pad_end - padded
    n_blocks = -(-n_pairs // MOE_BLOCK) + n_exp
    n_blocks = -(-n_blocks // BLOCKS_PER_STEP) * BLOCKS_PER_STEP
    blk_start = jnp.arange(n_blocks, dtype=jnp.int32) * MOE_BLOCK
    blk_exp = jnp.minimum(jnp.sum(blk_start[:, None] >= pad_end[None, :], axis=1),
                          n_exp - 1).astype(jnp.int32)
    n_active = (pad_end[-1:] // MOE_BLOCK).astype(jnp.int32)
    e_ids = jnp.arange(n_exp, dtype=jnp.int32)
    has = padded > 0
    slot_e = (jnp.cumsum(has.astype(jnp.int32)) - 1) % 2
    later = jnp.where(has, e_ids, n_exp)
    next_e = jnp.concatenate([lax.cummin(later, reverse=True)[1:], jnp.full((1,), n_exp, jnp.int32)])
    next_e = jnp.where(next_e >= n_exp, -1, next_e)
    blk_onehot = blk_exp[:, None] == e_ids[None, :]

    def per_block(table):
        return jnp.sum(jnp.where(blk_onehot, table[None, :], 0), axis=1).astype(jnp.int32)

    blk_first = ((blk_start == per_block(pad_start)) & (blk_start < pad_end[-1])).astype(jnp.int32)
    plan = (blk_exp, blk_first, per_block(slot_e), per_block(next_e), n_active)

    def reuse_tables(has_pub):
        pub = has_pub.astype(jnp.int32)
        pub_next = jnp.sum(jnp.where(next_e[:, None] == e_ids[None, :], pub[None, :], 0), axis=1)
        return per_block(pub), per_block(pub_next)

    return plan, pad_start.astype(jnp.int32), n_blocks, has, reuse_tables
```

```python
import functools

import jax
import jax.numpy as jnp
from jax import lax
from jax.experimental import pallas as pl
from jax.experimental.pallas import tpu as pltpu
from jax.experimental.pallas import tpu_sc as plsc

SC_CORES = 2
SC_SUBCORES = 16
SC_WORKERS = SC_CORES * SC_SUBCORES
SC_CHUNK = 128
CONV_A_W = 3
CONV_B_W = 31
TOP_K = 4
SWIGLU_LIMIT = 7.0
SWIGLU_ALPHA = 1.702
NORM_EPS = 1e-5
MOE_BLOCK = 256
BLOCKS_PER_STEP = 4
SUBLANES = 8
LANES = 128
CONV_ROW_CHUNK = 128
HIST_A_ROWS = 8
HIST_B_ROWS = 32
MIXER_ROWS = 1024
PROJ_SECTION = 256
COMBINE_TILE = 1024
COMBINE_ROW_BUFFERS = 3
POS_TILE = 4096
GROUP_SPLIT = (1, 3)
COMBINE_CHUNKS = 1
VMEM_LIMIT = 56 * 1024 * 1024


def _rms(x, g):
    return x * lax.rsqrt(jnp.mean(x * x, axis=-1, keepdims=True) + NORM_EPS) * g


def _zero_tile_from(v):
    bits = pltpu.bitcast(v[0:SUBLANES, 0:LANES], jnp.uint32)
    return pltpu.bitcast((bits >> 16) >> 16, jnp.float32)


def _pack_bf16_pairs(a):
    w = a.shape[-1] // 2
    lo = pltpu.bitcast(a[:, :w].astype(jnp.bfloat16).astype(jnp.float32), jnp.uint32)
    hi = pltpu.bitcast(a[:, w:].astype(jnp.bfloat16).astype(jnp.float32), jnp.uint32)
    return (lo >> 16) | (hi & jnp.uint32(0xFFFF0000))


def _unpack_bf16_pairs(wd):
    lo = pltpu.bitcast(wd << 16, jnp.float32)
    hi = pltpu.bitcast(wd & jnp.uint32(0xFFFF0000), jnp.float32)
    return jnp.concatenate([lo, hi], axis=-1)


def _conv_b_chunk(ub_buf, cbw_ref, n_i, r0, rc, c0):
    cs = slice(c0, c0 + LANES)
    lead = HIST_B_ROWS - (CONV_B_W - 1)
    acc = None
    for b in range(SUBLANES):
        rows = rc if b == 0 else rc + SUBLANES
        q = None
        for a in range((HIST_B_ROWS + SUBLANES) // SUBLANES):
            k = SUBLANES * a + b - lead
            if k < 0 or k >= CONV_B_W:
                continue
            start = r0 + SUBLANES * a
            term = cbw_ref[k:k + 1, cs] * ub_buf[n_i, start:start + rows, cs]
            q = term if q is None else q + term
        part = q if b == 0 else q[b:b + rc]
        acc = part if acc is None else acc + part
    return acc


def _mixer_kernel(x_ref, hista_ref, histb_ref, cnt0_ref, tri_ref,
                  gmix_ref, win_ref, caw_ref, cbw_ref, cbb_ref, lng_ref, lnb_ref,
                  ga_ref, gb_ref, wout_ref, gffn_ref, wrt_ref, br_ref,
                  h_ref, hnp_ref, idx_ref, gate_ref, rank_ref, cnt_ref, newa_ref, newb_ref,
                  ua_buf, ub_buf, cb_buf, cnt_acc, *, nb, ts, d_a, d_b, n_exp):
    b = pl.program_id(0)
    s = pl.program_id(1)
    m = nb * ts
    d = x_ref.shape[-1]

    @pl.when(s == 0)
    def _():
        ua_buf[:, 0:HIST_A_ROWS, :] = hista_ref[...]
        ub_buf[:, 0:HIST_B_ROWS, :] = histb_ref[...]

    @pl.when(s != 0)
    def _():
        ua_buf[:, 0:HIST_A_ROWS, :] = ua_buf[:, ts:ts + HIST_A_ROWS, :]
        ub_buf[:, 0:HIST_B_ROWS, :] = ub_buf[:, ts:ts + HIST_B_ROWS, :]

    @pl.when((b == 0) & (s == 0))
    def _():
        cnt_acc[...] = cnt0_ref[...]

    x = x_ref[...].reshape(m, d)
    n = _rms(x, gmix_ref[...]).astype(jnp.bfloat16)
    def proj(c0, width):
        return jnp.dot(n, win_ref[:, c0:c0 + width], preferred_element_type=jnp.float32)

    sec = min(d_b, PROJ_SECTION)
    for c0 in range(0, d_b, sec):
        v_b = proj(3 * d_a + c0, sec)
        g_b = proj(3 * d_a + d_b + c0, sec)
        ub_buf[:, HIST_B_ROWS:HIST_B_ROWS + ts, c0:c0 + sec] = (v_b * jax.nn.sigmoid(g_b)).reshape(nb, ts, sec)

    gate_c = proj(d_a, d_a)
    xt = proj(2 * d_a, d_a)
    gate_b = proj(0, d_a)
    ua_buf[:, HIST_A_ROWS:HIST_A_ROWS + ts, :] = (gate_c * xt).reshape(nb, ts, d_a)
    conv_a = jnp.zeros((nb, ts, d_a), jnp.float32)
    for k in range(CONV_A_W):
        off = HIST_A_ROWS - (CONV_A_W - 1) + k
        conv_a = conv_a + caw_ref[k:k + 1, :] * ua_buf[:, off:off + ts, :]
    y_a = gate_b * conv_a.reshape(m, d_a)

    mix_a = jnp.dot(_rms(y_a, ga_ref[...]).astype(jnp.bfloat16), wout_ref[0:d_a, :],
                    preferred_element_type=jnp.float32)

    rc = min(ts, CONV_ROW_CHUNK)
    chunks = [(n_i, r0, c0) for n_i in range(nb) for r0 in range(0, ts, rc) for c0 in range(0, d_b, LANES)]
    anchors = {len(chunks) * 2 // 8: gate_c, len(chunks) * 3 // 8: xt, len(chunks) * 5 // 8: gate_b,
               len(chunks) - 1: mix_a}
    for ci, (n_i, r0, c0) in enumerate(chunks):
        acc = _conv_b_chunk(ub_buf, cbw_ref, n_i, r0, rc, c0)
        if ci in anchors:
            acc = (acc.reshape(rc // SUBLANES, SUBLANES, LANES) + _zero_tile_from(anchors[ci])[None]
                   ).reshape(rc, LANES)
        cb_buf[n_i, r0:r0 + rc, c0:c0 + LANES] = acc
    cb = cb_buf[...].reshape(m, d_b) + cbb_ref[...]
    mu = jnp.mean(cb, axis=-1, keepdims=True)
    xc = cb - mu
    var = jnp.mean(xc * xc, axis=-1, keepdims=True)
    ln = xc * lax.rsqrt(var + NORM_EPS) * lng_ref[...] + lnb_ref[...]
    y_b = ln * jax.nn.sigmoid(ln)

    mix_b = jnp.dot(_rms(y_b, gb_ref[...]).astype(jnp.bfloat16), wout_ref[d_a:d_a + d_b, :],
                    preferred_element_type=jnp.float32)
    h = x + (mix_a + mix_b)
    h_ref[...] = h.reshape(nb, ts, d)

    hn = _rms(h, gffn_ref[...])
    hnp_ref[...] = _pack_bf16_pairs(hn)

    logits = lax.dot_general(wrt_ref[...], hn.astype(jnp.bfloat16), (((1,), (1,)), ((), ())),
                             preferred_element_type=jnp.float32) + br_ref[...]
    e_iota = lax.broadcasted_iota(jnp.int32, (n_exp, m), 0)
    cur = logits
    vals, idxs, sels = [], [], []
    for _ in range(TOP_K):
        mx = jnp.max(cur, axis=0, keepdims=True)
        ix = jnp.min(jnp.where(cur == mx, e_iota, n_exp), axis=0, keepdims=True)
        sel = e_iota == ix
        vals.append(mx)
        idxs.append(ix)
        sels.append(sel)
        cur = jnp.where(sel, -jnp.inf, cur)
    exps = [jnp.exp(v - vals[0]) for v in vals]
    denom = exps[0] + exps[1] + exps[2] + exps[3]
    gates = [e / denom for e in exps]

    onehot = jnp.zeros((n_exp, m), jnp.float32)
    for sel in sels:
        onehot = onehot + sel.astype(jnp.float32)
    before = jnp.dot(onehot.astype(jnp.bfloat16), tri_ref[...],
                     preferred_element_type=jnp.float32) + cnt_acc[...]
    ranks = [jnp.sum(jnp.where(sel, before, 0.0), axis=0, keepdims=True) for sel in sels]
    new_cnt = cnt_acc[...] + jnp.sum(onehot, axis=1, keepdims=True)
    cnt_acc[...] = new_cnt
    cnt_ref[...] = new_cnt

    idx_ref[...] = jnp.concatenate(idxs, axis=0)
    gate_ref[...] = jnp.concatenate(gates + [jnp.zeros((SUBLANES - TOP_K, m), jnp.float32)], axis=0).T
    rank_ref[...] = jnp.concatenate(ranks, axis=0).astype(jnp.int32)

    @pl.when(s == pl.num_programs(1) - 1)
    def _():
        newa_ref[...] = ua_buf[:, ts + HIST_A_ROWS - (CONV_A_W - 1):ts + HIST_A_ROWS, :]
        newb_ref[...] = ub_buf[:, ts + HIST_B_ROWS - (CONV_B_W - 1):ts + HIST_B_ROWS, :]


def _mixer(x, hist_a, hist_b, cnt0, params, *, nb, ts, b0=0, bsz=None):
    (gmix, win, caw, cbw, cbb, lng, lnb, ga, gb, wout, gffn, wrt, br) = params
    _, seq, d = x.shape
    bsz = x.shape[0] if bsz is None else bsz
    d_a, d_b = caw.shape[-1], cbw.shape[-1]
    n_exp = wrt.shape[0]
    m = nb * ts
    n_tok = bsz * seq
    g0 = b0 // nb
    grid = (bsz // nb, seq // ts)
    tri = (jnp.arange(m)[:, None] < jnp.arange(m)[None, :]).astype(jnp.bfloat16)

    def const(shape):
        return pl.BlockSpec(shape, lambda i, j: (0,) * len(shape))

    tok_map = lambda i, j: (0, i * (seq // ts) + j)
    kern = functools.partial(_mixer_kernel, nb=nb, ts=ts, d_a=d_a, d_b=d_b, n_exp=n_exp)
    out_shape = (
        jax.ShapeDtypeStruct((bsz, seq, d), jnp.float32),
        jax.ShapeDtypeStruct((n_tok, d // 2), jnp.uint32),
        jax.ShapeDtypeStruct((TOP_K, n_tok), jnp.int32),
        jax.ShapeDtypeStruct((n_tok, SUBLANES), jnp.float32),
        jax.ShapeDtypeStruct((TOP_K, n_tok), jnp.int32),
        jax.ShapeDtypeStruct((n_exp, 1), jnp.float32),
        jax.ShapeDtypeStruct((bsz, CONV_A_W - 1, d_a), jnp.float32),
        jax.ShapeDtypeStruct((bsz, CONV_B_W - 1, d_b), jnp.float32),
    )
    return pl.pallas_call(
        kern,
        grid=grid,
        in_specs=[
            pl.BlockSpec((nb, ts, d), lambda i, j: (g0 + i, j, 0)),
            pl.BlockSpec((nb, HIST_A_ROWS, d_a), lambda i, j: (g0 + i, 0, 0)),
            pl.BlockSpec((nb, HIST_B_ROWS, d_b), lambda i, j: (g0 + i, 0, 0)),
            const((n_exp, 1)),
            const((m, m)),
            const((1, d)), const(win.shape), const(caw.shape), const(cbw.shape), const((1, d_b)),
            const((1, d_b)), const((1, d_b)), const((1, d_a)), const((1, d_b)), const(wout.shape),
            const((1, d)), const(wrt.shape), const((n_exp, 1)),
        ],
        out_specs=(
            pl.BlockSpec((nb, ts, d), lambda i, j: (i, j, 0)),
            pl.BlockSpec((m, d // 2), lambda i, j: (i * (seq // ts) + j, 0)),
            pl.BlockSpec((TOP_K, m), tok_map),
            pl.BlockSpec((m, SUBLANES), lambda i, j: (i * (seq // ts) + j, 0)),
            pl.BlockSpec((TOP_K, m), tok_map),
            pl.BlockSpec((n_exp, 1), lambda i, j: (0, 0)),
            pl.BlockSpec((nb, CONV_A_W - 1, d_a), lambda i, j: (i, 0, 0)),
            pl.BlockSpec((nb, CONV_B_W - 1, d_b), lambda i, j: (i, 0, 0)),
        ),
        out_shape=out_shape,
        scratch_shapes=[
            pltpu.VMEM((nb, HIST_A_ROWS + ts, d_a), jnp.float32),
            pltpu.VMEM((nb, HIST_B_ROWS + ts, d_b), jnp.float32),
            pltpu.VMEM((nb, ts, d_b), jnp.float32),
            pltpu.VMEM((n_exp, 1), jnp.float32),
        ],
        compiler_params=pltpu.CompilerParams(
            dimension_semantics=("arbitrary", "arbitrary"), vmem_limit_bytes=VMEM_LIMIT),
        name="mixer",
    )(x, hist_a, hist_b, cnt0, tri, gmix, win, caw, cbw, cbb, lng, lnb, ga, gb, wout, gffn, wrt, br)


def _positions_kernel(start_ref, *refs):
    n_parts = len(refs) // 3
    for p in range(n_parts):
        idx = refs[2 * p][...]
        pos = refs[2 * p + 1][...]
        for e in range(start_ref.shape[0]):
            pos = pos + jnp.where(idx == e, start_ref[e], 0)
        refs[2 * n_parts + p][...] = pos


def _positions(pad_start, parts):
    n_0 = parts[0][0].shape[1]
    tile = min(n_0, POS_TILE)
    specs = [pl.BlockSpec((TOP_K, tile), lambda i: (0, i))]
    specs += [pl.BlockSpec((TOP_K, idx.shape[1]), lambda i: (0, 0)) for idx, _ in parts[1:]]
    in_specs = [pl.BlockSpec(memory_space=pltpu.SMEM)]
    args = [pad_start]
    for spec, (idx, rank) in zip(specs, parts):
        in_specs += [spec, spec]
        args += [idx, rank]
    out = pl.pallas_call(
        _positions_kernel,
        grid=(n_0 // tile,),
        in_specs=in_specs,
        out_specs=tuple(specs),
        out_shape=tuple(jax.ShapeDtypeStruct(idx.shape, jnp.int32) for idx, _ in parts),
        compiler_params=pltpu.CompilerParams(dimension_semantics=("arbitrary",)),
        name="positions",
    )(*args)
    return list(out)


def _sc_mesh():
    return plsc.VectorSubcoreMesh(core_axis_name="c", subcore_axis_name="s",
                                  num_cores=SC_CORES, num_subcores=SC_SUBCORES)


def _sc_worker_id():
    return lax.axis_index("s") * SC_CORES + lax.axis_index("c")


def _sc_chunks(n_tok, wid):
    n_chunks = n_tok // SC_CHUNK
    assert n_tok % SC_CHUNK == 0
    if n_chunks % SC_WORKERS == 0:
        per_w = n_chunks // SC_WORKERS
        return None, per_w, lambda i: (wid * per_w + i) * SC_CHUNK
    assert n_chunks <= SC_WORKERS
    return wid < n_chunks, 1, lambda i: wid * SC_CHUNK


def _sc_for_each_chunk(n_tok, fn):
    wid = _sc_worker_id()
    pred, trips, base = _sc_chunks(n_tok, wid)

    def run():
        @pl.loop(0, trips)
        def _(i):
            fn(pl.multiple_of(base(i), SC_CHUNK))

    if pred is None:
        run()
    else:
        pl.when(pred)(run)


def _sc_dispatch(parts, n_rows):
    half = parts[0][1].shape[1]
    n_parts = len(parts)

    @functools.partial(
        pl.kernel, mesh=_sc_mesh(),
        out_type=jax.ShapeDtypeStruct((n_rows, half), jnp.uint32),
        scratch_types=[pltpu.VMEM((TOP_K, SC_CHUNK), jnp.int32),
                       pltpu.VMEM((SC_CHUNK, half), jnp.uint32),
                       pltpu.SemaphoreType.DMA],
        compiler_params=pltpu.CompilerParams(use_tc_tiling_on_sc=True),
        name="sc_dispatch")
    def k(*refs):
        xs_hbm, idx_v, rows_v, sem = refs[2 * n_parts:]

        def move(pos_hbm, src_hbm):
            def fn(base):
                pltpu.sync_copy(pos_hbm.at[:, pl.ds(base, SC_CHUNK)], idx_v)
                pltpu.sync_copy(src_hbm.at[pl.ds(base, SC_CHUNK)], rows_v)
                copies = [pltpu.async_copy(rows_v, xs_hbm.at[idx_v.at[j]], sem) for j in range(TOP_K)]
                for c in copies:
                    c.wait()
            return fn

        for p, (_, hnp) in enumerate(parts):
            _sc_for_each_chunk(hnp.shape[0], move(refs[2 * p], refs[2 * p + 1]))

    return k(*[a for part in parts for a in part])


def _sc_gather(pos_parts, rows):
    half = rows.shape[1]
    n_parts = len(pos_parts)

    @functools.partial(
        pl.kernel, mesh=_sc_mesh(),
        out_type=tuple(jax.ShapeDtypeStruct((TOP_K, p.shape[1], half), jnp.uint32) for p in pos_parts),
        scratch_types=[pltpu.VMEM((TOP_K, SC_CHUNK), jnp.int32),
                       pltpu.VMEM((SC_CHUNK, half), jnp.uint32),
                       pltpu.SemaphoreType.DMA],
        compiler_params=pltpu.CompilerParams(use_tc_tiling_on_sc=True),
        name="sc_gather")
    def k(*refs):
        pos_hbms, rows_hbm = refs[:n_parts], refs[n_parts]
        out_hbms = refs[n_parts + 1:2 * n_parts + 1]
        idx_v, buf_v, sem = refs[2 * n_parts + 1:]

        def move(pos_hbm, out_hbm):
            def fn(base):
                pltpu.sync_copy(pos_hbm.at[:, pl.ds(base, SC_CHUNK)], idx_v)
                for j in range(TOP_K):
                    pltpu.async_copy(rows_hbm.at[idx_v.at[j]], buf_v, sem).wait()
                    pltpu.sync_copy(buf_v, out_hbm.at[j, pl.ds(base, SC_CHUNK)])
            return fn

        for pos_hbm, out_hbm, p in zip(pos_hbms, out_hbms, pos_parts):
            _sc_for_each_chunk(p.shape[1], move(pos_hbm, out_hbm))

    out = k(*pos_parts, rows)
    return out if isinstance(out, (tuple, list)) else (out,)


def _experts_kernel(*refs, publish, reuse):
    blk_exp_ref, first_ref, slot_ref, next_ref, n_active_ref = refs[:5]
    refs = refs[5:]
    if reuse:
        have_ref, have_next_ref = refs[:2]
        refs = refs[2:]
    xs_ref, wgu_hbm, bgu_ref, wd_hbm, bd_ref = refs[:5]
    refs = refs[5:]
    if reuse:
        wgu16_hbm, wd16_hbm = refs[:2]
        refs = refs[2:]
    out_ref = refs[0]
    refs = refs[1:]
    if publish:
        wgu16_out, wd16_out = refs[:2]
        refs = refs[2:]
    wgu_f32, wd_f32, wgu_bf, wd_bf, sems = refs[:5]
    refs = refs[5:]
    if publish:
        out_sems, pending = refs
    if reuse:
        sems16, = refs
    d_ff = wd_bf.shape[-2]

    def bf_slot(parity):
        return parity if reuse else 0

    def f32_slot(parity):
        return 0 if reuse else parity

    def fetch(e, parity):
        sl = f32_slot(parity)
        return (pltpu.make_async_copy(wgu_hbm.at[e], wgu_f32.at[sl], sems.at[0, sl]),
                pltpu.make_async_copy(wd_hbm.at[e], wd_f32.at[sl], sems.at[1, sl]))

    def fetch16(e, parity):
        return (pltpu.make_async_copy(wgu16_hbm.at[e], wgu_bf.at[parity], sems16.at[0, parity]),
                pltpu.make_async_copy(wd16_hbm.at[e], wd_bf.at[parity], sems16.at[1, parity]))

    def publish_copies(e):
        return (pltpu.make_async_copy(wgu_bf.at[0], wgu16_out.at[e], out_sems.at[0]),
                pltpu.make_async_copy(wd_bf.at[0], wd16_out.at[e], out_sems.at[1]))

    def wait_published():
        @pl.when(pending[0] == 1)
        def _():
            for c in publish_copies(0):
                c.wait()
            pending[0] = 0

    def start_fetch(e, parity, have):
        if reuse:
            @pl.when(have == 1)
            def _():
                for c in fetch16(e, parity):
                    c.start()

            @pl.when(have != 1)
            def _():
                for c in fetch(e, parity):
                    c.start()
        else:
            for c in fetch(e, parity):
                c.start()

    def finish_fetch(e, parity, have):
        def from_f32():
            for c in fetch(e, parity):
                c.wait()
            if publish:
                wait_published()
            wgu_bf[bf_slot(parity)] = wgu_f32[f32_slot(parity)].astype(jnp.bfloat16)
            wd_bf[bf_slot(parity)] = wd_f32[f32_slot(parity)].astype(jnp.bfloat16)
            if publish:
                for c in publish_copies(e):
                    c.start()
                pending[0] = 1

        if reuse:
            @pl.when(have == 1)
            def _():
                for c in fetch16(e, parity):
                    c.wait()

            pl.when(have != 1)(from_f32)
        else:
            from_f32()

    def load_weights_if_first(i):
        parity = slot_ref[i]
        e = blk_exp_ref[i]
        have = have_ref[i] if reuse else 0

        @pl.when(first_ref[i] == 1)
        def _():
            @pl.when(i == 0)
            def _():
                if publish:
                    pending[0] = 0
                start_fetch(e, parity, have)

            finish_fetch(e, parity, have)

            @pl.when(next_ref[i] >= 0)
            def _():
                start_fetch(next_ref[i], 1 - parity, have_next_ref[i] if reuse else 0)

    def compute(rows, i):
        e = blk_exp_ref[i]
        sl = bf_slot(slot_ref[i])
        x = _unpack_bf16_pairs(xs_ref[rows, :]).astype(jnp.bfloat16)
        gu = jnp.dot(x, wgu_bf[sl], preferred_element_type=jnp.float32) + bgu_ref[e]
        g = jnp.minimum(gu[:, :d_ff], SWIGLU_LIMIT)
        u = jnp.clip(gu[:, d_ff:], -SWIGLU_LIMIT, SWIGLU_LIMIT)
        act = g * jax.nn.sigmoid(SWIGLU_ALPHA * g) * (u + 1.0)
        o = jnp.dot(act.astype(jnp.bfloat16), wd_bf[sl], preferred_element_type=jnp.float32) + bd_ref[e]
        out_ref[rows, :] = _pack_bf16_pairs(o)

    i0 = pl.program_id(0) * BLOCKS_PER_STEP
    n_active = n_active_ref[0]

    def run(sub0, n_sub):
        first = i0 + sub0
        rows = slice(sub0 * MOE_BLOCK, (sub0 + n_sub) * MOE_BLOCK)
        if n_sub == 1:
            @pl.when(first < n_active)
            def _():
                load_weights_if_first(first)
                compute(rows, first)

            @pl.when(first >= n_active)
            def _():
                out_ref[rows, :] = jnp.zeros((MOE_BLOCK, out_ref.shape[1]), out_ref.dtype)
            return

        same = first + n_sub - 1 < n_active
        for sub in range(1, n_sub):
            same = same & (blk_exp_ref[first + sub] == blk_exp_ref[first])

        @pl.when(same)
        def _():
            load_weights_if_first(first)
            compute(rows, first)

        @pl.when(jnp.logical_not(same))
        def _():
            run(sub0, n_sub // 2)
            run(sub0 + n_sub // 2, n_sub // 2)

    run(0, BLOCKS_PER_STEP)

    if publish:
        @pl.when(pl.program_id(0) == pl.num_programs(0) - 1)
        def _():
            wait_published()


def _experts(plan, xs, wgu, bgu, wd, bd, *, publish=False, reuse=None):
    blk_exp, first, slot, nxt, n_active = plan
    n_rows, half = xs.shape
    n_exp, d, two_f = wgu.shape
    d_ff = wd.shape[1]
    n_blocks = n_rows // MOE_BLOCK
    step_rows = MOE_BLOCK * BLOCKS_PER_STEP
    assert n_rows % step_rows == 0
    any_spec = pl.BlockSpec(memory_space=pl.ANY)
    scalars = [blk_exp, first, slot, nxt, n_active]
    operands = [xs, wgu, bgu.reshape(n_exp, 1, two_f), wd, bd.reshape(n_exp, 1, d)]
    in_specs = [
        pl.BlockSpec((step_rows, half), lambda i, *_: (i, 0)),
        any_spec,
        pl.BlockSpec((n_exp, 1, two_f), lambda i, *_: (0, 0, 0)),
        any_spec,
        pl.BlockSpec((n_exp, 1, d), lambda i, *_: (0, 0, 0)),
    ]
    out_specs = [pl.BlockSpec((step_rows, half), lambda i, *_: (i, 0))]
    out_shape = [jax.ShapeDtypeStruct((n_rows, half), jnp.uint32)]
    n_f32_slots, n_bf_slots = (1, 2) if reuse else (2, 1)
    scratch = [
        pltpu.VMEM((n_f32_slots, d, two_f), jnp.float32),
        pltpu.VMEM((n_f32_slots, d_ff, d), jnp.float32),
        pltpu.VMEM((n_bf_slots, d, two_f), jnp.bfloat16),
        pltpu.VMEM((n_bf_slots, d_ff, d), jnp.bfloat16),
        pltpu.SemaphoreType.DMA((2, 2)),
    ]
    if reuse:
        wgu16, wd16, have, have_next = reuse
        scalars += [have, have_next]
        operands += [wgu16, wd16]
        in_specs += [any_spec, any_spec]
        scratch += [pltpu.SemaphoreType.DMA((2, 2))]
    if publish:
        out_specs += [any_spec, any_spec]
        out_shape += [jax.ShapeDtypeStruct((n_exp, d, two_f), jnp.bfloat16),
                      jax.ShapeDtypeStruct((n_exp, d_ff, d), jnp.bfloat16)]
        scratch += [pltpu.SemaphoreType.DMA((2,)), pltpu.SMEM((1,), jnp.int32)]
    grid_spec = pltpu.PrefetchScalarGridSpec(
        num_scalar_prefetch=len(scalars),
        grid=(n_blocks // BLOCKS_PER_STEP,),
        in_specs=in_specs,
        out_specs=tuple(out_specs),
        scratch_shapes=scratch,
    )
    out = pl.pallas_call(
        functools.partial(_experts_kernel, publish=publish, reuse=bool(reuse)),
        grid_spec=grid_spec,
        out_shape=tuple(out_shape),
        compiler_params=pltpu.CompilerParams(
            dimension_semantics=("arbitrary",), vmem_limit_bytes=VMEM_LIMIT),
        name="experts",
    )(*scalars, *operands)
    return out if publish else out[0]


def _combine_kernel(rows_ref, gate_ref, h_ref, gfin_ref, *rest):
    y_ref = rest[-1]
    acc = h_ref[...]
    for k in range(TOP_K):
        acc = acc + gate_ref[:, k:k + 1] * _unpack_bf16_pairs(rows_ref[k])
    y_ref[...] = _rms(acc, gfin_ref[...])


def _combine(rows4, gate_t, h2d, gfin, *, tok0=0, y_prev=None, y_rows=None, y_tok0=0):
    d = h2d.shape[1]
    n, half = rows4.shape[1:]
    n_tok = h2d.shape[0] if y_rows is None else y_rows
    tile = min(n, COMBINE_TILE)
    t0 = tok0 // tile
    y0 = y_tok0 // tile
    def outer(rows_hbm, gate_hbm, h_hbm, gfin_ref, *rest):
        y_hbm = rest[-1]

        def body(rows_v, gate_v, h_v, y_v):
            _combine_kernel(rows_v, gate_v, h_v, gfin_ref, y_v)

        pltpu.emit_pipeline(
            body,
            grid=(n // tile,),
            in_specs=[
                pl.BlockSpec((TOP_K, tile, half), lambda i: (0, i, 0),
                             pipeline_mode=pl.Buffered(COMBINE_ROW_BUFFERS)),
                pl.BlockSpec((tile, gate_t.shape[1]), lambda i: (t0 + i, 0)),
                pl.BlockSpec((tile, d), lambda i: (t0 + i, 0)),
            ],
            out_specs=[pl.BlockSpec((tile, d), lambda i: (y0 + i, 0))],
        )(rows_hbm, gate_hbm, h_hbm, y_hbm)

    any_spec = pl.BlockSpec(memory_space=pl.ANY)
    in_specs = [any_spec, any_spec, any_spec, pl.BlockSpec(memory_space=pltpu.VMEM)]
    args = [rows4, gate_t, h2d, gfin]
    aliases = {}
    if y_prev is not None:
        in_specs.append(any_spec)
        args.append(y_prev)
        aliases = {4: 0}
    return pl.pallas_call(
        outer,
        in_specs=in_specs,
        out_specs=any_spec,
        out_shape=jax.ShapeDtypeStruct((n_tok, d), jnp.float32),
        input_output_aliases=aliases,
        compiler_params=pltpu.CompilerParams(vmem_limit_bytes=VMEM_LIMIT),
        name="combine",
    )(*args)


def _pad_hist(hist, rows):
    return jnp.pad(hist, ((0, 0), (rows - hist.shape[1], 0), (0, 0)))


def kernel(x_prompt, x_sample, state_conv_a, state_conv_b, norm_mix_g, w_in, conv_a_w, conv_b_w, conv_b_b, conv_ln_g, conv_ln_b, out_norm_a_g, out_norm_b_g, w_out, norm_ffn_g, w_router, b_router, w_gate_up, b_gate_up, w_down, b_down, final_norm_g):
    depth = w_in.shape[0]
    assert depth == 1, "single-layer trunk"
    bf16 = jnp.bfloat16
    bsz, seq, d = x_prompt.shape
    dec_b, dec_s, _ = x_sample.shape
    d_a, d_b = conv_a_w.shape[-1], conv_b_w.shape[-1]
    n_exp = w_router.shape[-1]

    params = (norm_mix_g[0][None], w_in[0].astype(bf16), conv_a_w[0], conv_b_w[0], conv_b_b[0][None],
              conv_ln_g[0][None], conv_ln_b[0][None], out_norm_a_g[0][None], out_norm_b_g[0][None],
              w_out[0].astype(bf16), norm_ffn_g[0][None], w_router[0].T.astype(bf16),
              b_router[0][:, None])

    zero_a = jnp.zeros((bsz, HIST_A_ROWS, d_a), jnp.float32)
    zero_b = jnp.zeros((bsz, HIST_B_ROWS, d_b), jnp.float32)
    cnt0 = jnp.zeros((n_exp, 1), jnp.float32)
    gfin = final_norm_g[None]
    ts = min(seq, MIXER_ROWS)
    n_p = bsz * seq

    sizes = [bsz * f // sum(GROUP_SPLIT) for f in GROUP_SPLIT]
    if bsz % sum(GROUP_SPLIT) != 0:
        sizes = [bsz]
    groups = []
    tok_start = []
    for g, per_b in enumerate(sizes):
        b0 = sum(sizes[:g])
        tok_start.append(b0 * seq)
        h, hnp, idx, gate, rank, cnt, na, nb_ = _mixer(
            x_prompt, zero_a, zero_b, cnt0, params, nb=1, ts=ts, b0=b0, bsz=per_b)
        parts = [dict(h=h.reshape(per_b * seq, d), hnp=hnp, idx=idx, gate_t=gate, rank=rank)]
        states = [(na, nb_)]
        if g == len(sizes) - 1:
            h, hnp, idx, gate, rank, cnt, na_s, nb_s = _mixer(
                x_sample, _pad_hist(state_conv_a[0], HIST_A_ROWS), _pad_hist(state_conv_b[0], HIST_B_ROWS),
                cnt, params, nb=dec_b, ts=dec_s)
            parts.append(dict(h=h.reshape(dec_b * dec_s, d), hnp=hnp, idx=idx, gate_t=gate, rank=rank))
        n_pairs = sum(p["hnp"].shape[0] for p in parts) * TOP_K
        plan, pad_start, n_blocks, has, reuse_tables = _block_plan(cnt[:, 0].astype(jnp.int32), n_pairs, n_exp)
        pos = _positions(pad_start, [(p["idx"], p["rank"]) for p in parts])
        xs = _sc_dispatch([(q, p["hnp"]) for q, p in zip(pos, parts)], n_blocks * MOE_BLOCK)
        groups.append(dict(parts=parts, pos=pos, plan=plan, xs=xs, states=states, has=has,
                           reuse_tables=reuse_tables))

    weights = (w_gate_up[0], b_gate_up[0], w_down[0], b_down[0])
    pub = groups[-1]
    if len(groups) > 1:
        pub["rows"], wgu16, wd16 = _experts(pub["plan"], pub["xs"], *weights, publish=True)
        for grp in groups[:-1]:
            grp["rows"] = _experts(grp["plan"], grp["xs"], *weights,
                                   reuse=(wgu16, wd16) + grp["reuse_tables"](pub["has"]))
    else:
        pub["rows"] = _experts(pub["plan"], pub["xs"], *weights)

    gathered = []
    for g, grp in enumerate(groups):
        pos0 = grp["pos"][0]
        n_g = pos0.shape[1]
        want = 1 if (g == len(groups) - 1 and len(groups) > 1) else COMBINE_CHUNKS
        n_chunks = want if n_g % (want * SC_CHUNK * SC_WORKERS) == 0 else 1
        per = n_g // n_chunks
        for c in range(n_chunks):
            pos_parts = [pos0[:, c * per:(c + 1) * per]] + (grp["pos"][1:] if c == 0 else [])
            gathered.append((g, c * per, _sc_gather(pos_parts, grp["rows"])))

    y_p = None
    for g, tok0, got in sorted(gathered, key=lambda t: (t[0] != len(groups) - 1, t[0], t[1])):
        grp = groups[g]
        main = grp["parts"][0]
        if len(got) > 1:
            smp = grp["parts"][1]
            y_s = _combine(got[1], smp["gate_t"], smp["h"], gfin).reshape(dec_b, dec_s, d)
        y_p = _combine(got[0], main["gate_t"], main["h"], gfin, tok0=tok0,
                       y_prev=y_p, y_rows=n_p, y_tok0=tok_start[g] + tok0)
    y_p = y_p.reshape(bsz, seq, d)

    na_p = jnp.concatenate([grp["states"][0][0] for grp in groups], axis=0)
    nb_p = jnp.concatenate([grp["states"][0][1] for grp in groups], axis=0)
    return (y_p, y_s, na_p[None], nb_p[None], na_s[None], nb_s[None])


def _block_plan(counts, n_pairs, n_exp):
    padded = (counts + MOE_BLOCK - 1) // MOE_BLOCK * MOE_BLOCK
    pad_end = jnp.cumsum(padded)
    pad_start = pad_end - padded
    n_blocks = -(-n_pairs // MOE_BLOCK) + n_exp
    n_blocks = -(-n_blocks // BLOCKS_PER_STEP) * BLOCKS_PER_STEP
    blk_start = jnp.arange(n_blocks, dtype=jnp.int32) * MOE_BLOCK
    blk_exp = jnp.minimum(jnp.sum(blk_start[:, None] >= pad_end[None, :], axis=1),
                          n_exp - 1).astype(jnp.int32)
    n_active = (pad_end[-1:] // MOE_BLOCK).astype(jnp.int32)
    e_ids = jnp.arange(n_exp, dtype=jnp.int32)
    has = padded > 0
    slot_e = (jnp.cumsum(has.astype(jnp.int32)) - 1) % 2
    later = jnp.where(has, e_ids, n_exp)
    next_e = jnp.concatenate([lax.cummin(later, reverse=True)[1:], jnp.full((1,), n_exp, jnp.int32)])
    next_e = jnp.where(next_e >= n_exp, -1, next_e)
    blk_onehot = blk_exp[:, None] == e_ids[None, :]

    def per_block(table):
        return jnp.sum(jnp.where(blk_onehot, table[None, :], 0), axis=1).astype(jnp.int32)

    blk_first = ((blk_start == per_block(pad_start)) & (blk_start < pad_end[-1])).astype(jnp.int32)
    plan = (blk_exp, blk_first, per_block(slot_e), per_block(next_e), n_active)

    def reuse_tables(has_pub):
        pub = has_pub.astype(jnp.int32)
        pub_next = jnp.sum(jnp.where(next_e[:, None] == e_ids[None, :], pub[None, :], 0), axis=1)
        return per_block(pub), per_block(pub_next)

    return plan, pad_start.astype(jnp.int32), n_blocks, has, reuse_tables
```

```python
import functools

import jax
import jax.numpy as jnp
from jax import lax
from jax.experimental import pallas as pl
from jax.experimental.pallas import tpu as pltpu
from jax.experimental.pallas import tpu_sc as plsc

SC_CORES = 2
SC_SUBCORES = 16
SC_WORKERS = SC_CORES * SC_SUBCORES
SC_CHUNK = 128
CONV_A_W = 3
CONV_B_W = 31
TOP_K = 4
SWIGLU_LIMIT = 7.0
SWIGLU_ALPHA = 1.702
NORM_EPS = 1e-5
MOE_BLOCK = 256
BLOCKS_PER_STEP = 4
SUBLANES = 8
LANES = 128
CONV_ROW_CHUNK = 128
HIST_A_ROWS = 8
HIST_B_ROWS = 32
MIXER_ROWS = 1024
PROJ_SECTION = 256
COMBINE_TILE = 1024
POS_TILE = 4096
GROUP_SPLIT = (1, 1)
COMBINE_CHUNKS = 1
VMEM_LIMIT = 56 * 1024 * 1024


def _rms(x, g):
    return x * lax.rsqrt(jnp.mean(x * x, axis=-1, keepdims=True) + NORM_EPS) * g


def _zero_tile_from(v):
    bits = pltpu.bitcast(v[0:SUBLANES, 0:LANES], jnp.uint32)
    return pltpu.bitcast((bits >> 16) >> 16, jnp.float32)


def _pack_bf16_pairs(a):
    w = a.shape[-1] // 2
    lo = pltpu.bitcast(a[:, :w].astype(jnp.bfloat16).astype(jnp.float32), jnp.uint32)
    hi = pltpu.bitcast(a[:, w:].astype(jnp.bfloat16).astype(jnp.float32), jnp.uint32)
    return (lo >> 16) | (hi & jnp.uint32(0xFFFF0000))


def _unpack_bf16_pairs(wd):
    lo = pltpu.bitcast(wd << 16, jnp.float32)
    hi = pltpu.bitcast(wd & jnp.uint32(0xFFFF0000), jnp.float32)
    return jnp.concatenate([lo, hi], axis=-1)


def _conv_b_chunk(ub_buf, cbw_ref, n_i, r0, rc, c0):
    cs = slice(c0, c0 + LANES)
    lead = HIST_B_ROWS - (CONV_B_W - 1)
    acc = None
    for b in range(SUBLANES):
        rows = rc if b == 0 else rc + SUBLANES
        q = None
        for a in range((HIST_B_ROWS + SUBLANES) // SUBLANES):
            k = SUBLANES * a + b - lead
            if k < 0 or k >= CONV_B_W:
                continue
            start = r0 + SUBLANES * a
            term = cbw_ref[k:k + 1, cs] * ub_buf[n_i, start:start + rows, cs]
            q = term if q is None else q + term
        part = q if b == 0 else q[b:b + rc]
        acc = part if acc is None else acc + part
    return acc


def _mixer_kernel(x_ref, hista_ref, histb_ref, cnt0_ref, tri_ref,
                  gmix_ref, win_ref, caw_ref, cbw_ref, cbb_ref, lng_ref, lnb_ref,
                  ga_ref, gb_ref, wout_ref, gffn_ref, wrt_ref, br_ref,
                  h_ref, hnp_ref, idx_ref, gate_ref, rank_ref, cnt_ref, newa_ref, newb_ref,
                  ua_buf, ub_buf, cb_buf, cnt_acc, *, nb, ts, d_a, d_b, n_exp):
    b = pl.program_id(0)
    s = pl.program_id(1)
    m = nb * ts
    d = x_ref.shape[-1]

    @pl.when(s == 0)
    def _():
        ua_buf[:, 0:HIST_A_ROWS, :] = hista_ref[...]
        ub_buf[:, 0:HIST_B_ROWS, :] = histb_ref[...]

    @pl.when(s != 0)
    def _():
        ua_buf[:, 0:HIST_A_ROWS, :] = ua_buf[:, ts:ts + HIST_A_ROWS, :]
        ub_buf[:, 0:HIST_B_ROWS, :] = ub_buf[:, ts:ts + HIST_B_ROWS, :]

    @pl.when((b == 0) & (s == 0))
    def _():
        cnt_acc[...] = cnt0_ref[...]

    x = x_ref[...].reshape(m, d)
    n = _rms(x, gmix_ref[...]).astype(jnp.bfloat16)
    def proj(c0, width):
        return jnp.dot(n, win_ref[:, c0:c0 + width], preferred_element_type=jnp.float32)

    sec = min(d_b, PROJ_SECTION)
    for c0 in range(0, d_b, sec):
        v_b = proj(3 * d_a + c0, sec)
        g_b = proj(3 * d_a + d_b + c0, sec)
        ub_buf[:, HIST_B_ROWS:HIST_B_ROWS + ts, c0:c0 + sec] = (v_b * jax.nn.sigmoid(g_b)).reshape(nb, ts, sec)

    gate_c = proj(d_a, d_a)
    xt = proj(2 * d_a, d_a)
    gate_b = proj(0, d_a)
    ua_buf[:, HIST_A_ROWS:HIST_A_ROWS + ts, :] = (gate_c * xt).reshape(nb, ts, d_a)
    conv_a = jnp.zeros((nb, ts, d_a), jnp.float32)
    for k in range(CONV_A_W):
        off = HIST_A_ROWS - (CONV_A_W - 1) + k
        conv_a = conv_a + caw_ref[k:k + 1, :] * ua_buf[:, off:off + ts, :]
    y_a = gate_b * conv_a.reshape(m, d_a)

    mix_a = jnp.dot(_rms(y_a, ga_ref[...]).astype(jnp.bfloat16), wout_ref[0:d_a, :],
                    preferred_element_type=jnp.float32)

    rc = min(ts, CONV_ROW_CHUNK)
    chunks = [(n_i, r0, c0) for n_i in range(nb) for r0 in range(0, ts, rc) for c0 in range(0, d_b, LANES)]
    anchors = {len(chunks) * 2 // 8: gate_c, len(chunks) * 3 // 8: xt, len(chunks) * 5 // 8: gate_b,
               len(chunks) - 1: mix_a}
    for ci, (n_i, r0, c0) in enumerate(chunks):
        acc = _conv_b_chunk(ub_buf, cbw_ref, n_i, r0, rc, c0)
        if ci in anchors:
            acc = (acc.reshape(rc // SUBLANES, SUBLANES, LANES) + _zero_tile_from(anchors[ci])[None]
                   ).reshape(rc, LANES)
        cb_buf[n_i, r0:r0 + rc, c0:c0 + LANES] = acc
    cb = cb_buf[...].reshape(m, d_b) + cbb_ref[...]
    mu = jnp.mean(cb, axis=-1, keepdims=True)
    xc = cb - mu
    var = jnp.mean(xc * xc, axis=-1, keepdims=True)
    ln = xc * lax.rsqrt(var + NORM_EPS) * lng_ref[...] + lnb_ref[...]
    y_b = ln * jax.nn.sigmoid(ln)

    mix_b = jnp.dot(_rms(y_b, gb_ref[...]).astype(jnp.bfloat16), wout_ref[d_a:d_a + d_b, :],
                    preferred_element_type=jnp.float32)
    h = x + (mix_a + mix_b)
    h_ref[...] = h.reshape(nb, ts, d)

    hn = _rms(h, gffn_ref[...])
    hnp_ref[...] = _pack_bf16_pairs(hn)

    logits = lax.dot_general(wrt_ref[...], hn.astype(jnp.bfloat16), (((1,), (1,)), ((), ())),
                             preferred_element_type=jnp.float32) + br_ref[...]
    e_iota = lax.broadcasted_iota(jnp.int32, (n_exp, m), 0)
    cur = logits
    vals, idxs, sels = [], [], []
    for _ in range(TOP_K):
        mx = jnp.max(cur, axis=0, keepdims=True)
        ix = jnp.min(jnp.where(cur == mx, e_iota, n_exp), axis=0, keepdims=True)
        sel = e_iota == ix
        vals.append(mx)
        idxs.append(ix)
        sels.append(sel)
        cur = jnp.where(sel, -jnp.inf, cur)
    exps = [jnp.exp(v - vals[0]) for v in vals]
    denom = exps[0] + exps[1] + exps[2] + exps[3]
    gates = [e / denom for e in exps]

    onehot = jnp.zeros((n_exp, m), jnp.float32)
    for sel in sels:
        onehot = onehot + sel.astype(jnp.float32)
    before = jnp.dot(onehot.astype(jnp.bfloat16), tri_ref[...],
                     preferred_element_type=jnp.float32) + cnt_acc[...]
    ranks = [jnp.sum(jnp.where(sel, before, 0.0), axis=0, keepdims=True) for sel in sels]
    new_cnt = cnt_acc[...] + jnp.sum(onehot, axis=1, keepdims=True)
    cnt_acc[...] = new_cnt
    cnt_ref[...] = new_cnt

    idx_ref[...] = jnp.concatenate(idxs, axis=0)
    gate_ref[...] = jnp.concatenate(gates + [jnp.zeros((SUBLANES - TOP_K, m), jnp.float32)], axis=0).T
    rank_ref[...] = jnp.concatenate(ranks, axis=0).astype(jnp.int32)

    @pl.when(s == pl.num_programs(1) - 1)
    def _():
        newa_ref[...] = ua_buf[:, ts + HIST_A_ROWS - (CONV_A_W - 1):ts + HIST_A_ROWS, :]
        newb_ref[...] = ub_buf[:, ts + HIST_B_ROWS - (CONV_B_W - 1):ts + HIST_B_ROWS, :]


def _mixer(x, hist_a, hist_b, cnt0, params, *, nb, ts, b0=0, bsz=None):
    (gmix, win, caw, cbw, cbb, lng, lnb, ga, gb, wout, gffn, wrt, br) = params
    _, seq, d = x.shape
    bsz = x.shape[0] if bsz is None else bsz
    d_a, d_b = caw.shape[-1], cbw.shape[-1]
    n_exp = wrt.shape[0]
    m = nb * ts
    n_tok = bsz * seq
    g0 = b0 // nb
    grid = (bsz // nb, seq // ts)
    tri = (jnp.arange(m)[:, None] < jnp.arange(m)[None, :]).astype(jnp.bfloat16)

    def const(shape):
        return pl.BlockSpec(shape, lambda i, j: (0,) * len(shape))

    tok_map = lambda i, j: (0, i * (seq // ts) + j)
    kern = functools.partial(_mixer_kernel, nb=nb, ts=ts, d_a=d_a, d_b=d_b, n_exp=n_exp)
    out_shape = (
        jax.ShapeDtypeStruct((bsz, seq, d), jnp.float32),
        jax.ShapeDtypeStruct((n_tok, d // 2), jnp.uint32),
        jax.ShapeDtypeStruct((TOP_K, n_tok), jnp.int32),
        jax.ShapeDtypeStruct((n_tok, SUBLANES), jnp.float32),
        jax.ShapeDtypeStruct((TOP_K, n_tok), jnp.int32),
        jax.ShapeDtypeStruct((n_exp, 1), jnp.float32),
        jax.ShapeDtypeStruct((bsz, CONV_A_W - 1, d_a), jnp.float32),
        jax.ShapeDtypeStruct((bsz, CONV_B_W - 1, d_b), jnp.float32),
    )
    return pl.pallas_call(
        kern,
        grid=grid,
        in_specs=[
            pl.BlockSpec((nb, ts, d), lambda i, j: (g0 + i, j, 0)),
            pl.BlockSpec((nb, HIST_A_ROWS, d_a), lambda i, j: (g0 + i, 0, 0)),
            pl.BlockSpec((nb, HIST_B_ROWS, d_b), lambda i, j: (g0 + i, 0, 0)),
            const((n_exp, 1)),
            const((m, m)),
            const((1, d)), const(win.shape), const(caw.shape), const(cbw.shape), const((1, d_b)),
            const((1, d_b)), const((1, d_b)), const((1, d_a)), const((1, d_b)), const(wout.shape),
            const((1, d)), const(wrt.shape), const((n_exp, 1)),
        ],
        out_specs=(
            pl.BlockSpec((nb, ts, d), lambda i, j: (i, j, 0)),
            pl.BlockSpec((m, d // 2), lambda i, j: (i * (seq // ts) + j, 0)),
            pl.BlockSpec((TOP_K, m), tok_map),
            pl.BlockSpec((m, SUBLANES), lambda i, j: (i * (seq // ts) + j, 0)),
            pl.BlockSpec((TOP_K, m), tok_map),
            pl.BlockSpec((n_exp, 1), lambda i, j: (0, 0)),
            pl.BlockSpec((nb, CONV_A_W - 1, d_a), lambda i, j: (i, 0, 0)),
            pl.BlockSpec((nb, CONV_B_W - 1, d_b), lambda i, j: (i, 0, 0)),
        ),
        out_shape=out_shape,
        scratch_shapes=[
            pltpu.VMEM((nb, HIST_A_ROWS + ts, d_a), jnp.float32),
            pltpu.VMEM((nb, HIST_B_ROWS + ts, d_b), jnp.float32),
            pltpu.VMEM((nb, ts, d_b), jnp.float32),
            pltpu.VMEM((n_exp, 1), jnp.float32),
        ],
        compiler_params=pltpu.CompilerParams(
            dimension_semantics=("arbitrary", "arbitrary"), vmem_limit_bytes=VMEM_LIMIT),
        name="mixer",
    )(x, hist_a, hist_b, cnt0, tri, gmix, win, caw, cbw, cbb, lng, lnb, ga, gb, wout, gffn, wrt, br)


def _positions_kernel(start_ref, *refs):
    n_parts = len(refs) // 3
    for p in range(n_parts):
        idx = refs[2 * p][...]
        pos = refs[2 * p + 1][...]
        for e in range(start_ref.shape[0]):
            pos = pos + jnp.where(idx == e, start_ref[e], 0)
        refs[2 * n_parts + p][...] = pos


def _positions(pad_start, parts):
    n_0 = parts[0][0].shape[1]
    tile = min(n_0, POS_TILE)
    specs = [pl.BlockSpec((TOP_K, tile), lambda i: (0, i))]
    specs += [pl.BlockSpec((TOP_K, idx.shape[1]), lambda i: (0, 0)) for idx, _ in parts[1:]]
    in_specs = [pl.BlockSpec(memory_space=pltpu.SMEM)]
    args = [pad_start]
    for spec, (idx, rank) in zip(specs, parts):
        in_specs += [spec, spec]
        args += [idx, rank]
    out = pl.pallas_call(
        _positions_kernel,
        grid=(n_0 // tile,),
        in_specs=in_specs,
        out_specs=tuple(specs),
        out_shape=tuple(jax.ShapeDtypeStruct(idx.shape, jnp.int32) for idx, _ in parts),
        compiler_params=pltpu.CompilerParams(dimension_semantics=("arbitrary",)),
        name="positions",
    )(*args)
    return list(out)


def _sc_mesh():
    return plsc.VectorSubcoreMesh(core_axis_name="c", subcore_axis_name="s",
                                  num_cores=SC_CORES, num_subcores=SC_SUBCORES)


def _sc_worker_id():
    return lax.axis_index("s") * SC_CORES + lax.axis_index("c")


def _sc_chunks(n_tok, wid):
    n_chunks = n_tok // SC_CHUNK
    assert n_tok % SC_CHUNK == 0
    if n_chunks % SC_WORKERS == 0:
        per_w = n_chunks // SC_WORKERS
        return None, per_w, lambda i: (wid * per_w + i) * SC_CHUNK
    assert n_chunks <= SC_WORKERS
    return wid < n_chunks, 1, lambda i: wid * SC_CHUNK


def _sc_for_each_chunk(n_tok, fn):
    wid = _sc_worker_id()
    pred, trips, base = _sc_chunks(n_tok, wid)

    def run():
        @pl.loop(0, trips)
        def _(i):
            fn(pl.multiple_of(base(i), SC_CHUNK))

    if pred is None:
        run()
    else:
        pl.when(pred)(run)


def _sc_dispatch(parts, n_rows):
    half = parts[0][1].shape[1]
    n_parts = len(parts)

    @functools.partial(
        pl.kernel, mesh=_sc_mesh(),
        out_type=jax.ShapeDtypeStruct((n_rows, half), jnp.uint32),
        scratch_types=[pltpu.VMEM((TOP_K, SC_CHUNK), jnp.int32),
                       pltpu.VMEM((SC_CHUNK, half), jnp.uint32),
                       pltpu.SemaphoreType.DMA],
        compiler_params=pltpu.CompilerParams(use_tc_tiling_on_sc=True),
        name="sc_dispatch")
    def k(*refs):
        xs_hbm, idx_v, rows_v, sem = refs[2 * n_parts:]

        def move(pos_hbm, src_hbm):
            def fn(base):
                pltpu.sync_copy(pos_hbm.at[:, pl.ds(base, SC_CHUNK)], idx_v)
                pltpu.sync_copy(src_hbm.at[pl.ds(base, SC_CHUNK)], rows_v)
                copies = [pltpu.async_copy(rows_v, xs_hbm.at[idx_v.at[j]], sem) for j in range(TOP_K)]
                for c in copies:
                    c.wait()
            return fn

        for p, (_, hnp) in enumerate(parts):
            _sc_for_each_chunk(hnp.shape[0], move(refs[2 * p], refs[2 * p + 1]))

    return k(*[a for part in parts for a in part])


def _sc_gather(pos_parts, rows):
    half = rows.shape[1]
    n_parts = len(pos_parts)

    @functools.partial(
        pl.kernel, mesh=_sc_mesh(),
        out_type=tuple(jax.ShapeDtypeStruct((TOP_K, p.shape[1], half), jnp.uint32) for p in pos_parts),
        scratch_types=[pltpu.VMEM((TOP_K, SC_CHUNK), jnp.int32),
                       pltpu.VMEM((SC_CHUNK, half), jnp.uint32),
                       pltpu.SemaphoreType.DMA],
        compiler_params=pltpu.CompilerParams(use_tc_tiling_on_sc=True),
        name="sc_gather")
    def k(*refs):
        pos_hbms, rows_hbm = refs[:n_parts], refs[n_parts]
        out_hbms = refs[n_parts + 1:2 * n_parts + 1]
        idx_v, buf_v, sem = refs[2 * n_parts + 1:]

        def move(pos_hbm, out_hbm):
            def fn(base):
                pltpu.sync_copy(pos_hbm.at[:, pl.ds(base, SC_CHUNK)], idx_v)
                for j in range(TOP_K):
                    pltpu.async_copy(rows_hbm.at[idx_v.at[j]], buf_v, sem).wait()
                    pltpu.sync_copy(buf_v, out_hbm.at[j, pl.ds(base, SC_CHUNK)])
            return fn

        for pos_hbm, out_hbm, p in zip(pos_hbms, out_hbms, pos_parts):
            _sc_for_each_chunk(p.shape[1], move(pos_hbm, out_hbm))

    out = k(*pos_parts, rows)
    return out if isinstance(out, (tuple, list)) else (out,)


def _experts_kernel(*refs, publish, reuse):
    blk_exp_ref, first_ref, slot_ref, next_ref, n_active_ref = refs[:5]
    refs = refs[5:]
    if reuse:
        have_ref, have_next_ref = refs[:2]
        refs = refs[2:]
    xs_ref, wgu_hbm, bgu_ref, wd_hbm, bd_ref = refs[:5]
    refs = refs[5:]
    if reuse:
        wgu16_hbm, wd16_hbm = refs[:2]
        refs = refs[2:]
    out_ref = refs[0]
    refs = refs[1:]
    if publish:
        wgu16_out, wd16_out = refs[:2]
        refs = refs[2:]
    wgu_f32, wd_f32, wgu_bf, wd_bf, sems = refs[:5]
    refs = refs[5:]
    if publish:
        out_sems, pending = refs
    if reuse:
        sems16, = refs
    d_ff = wd_bf.shape[-2]

    def bf_slot(parity):
        return parity if reuse else 0

    def f32_slot(parity):
        return 0 if reuse else parity

    def fetch(e, parity):
        sl = f32_slot(parity)
        return (pltpu.make_async_copy(wgu_hbm.at[e], wgu_f32.at[sl], sems.at[0, sl]),
                pltpu.make_async_copy(wd_hbm.at[e], wd_f32.at[sl], sems.at[1, sl]))

    def fetch16(e, parity):
        return (pltpu.make_async_copy(wgu16_hbm.at[e], wgu_bf.at[parity], sems16.at[0, parity]),
                pltpu.make_async_copy(wd16_hbm.at[e], wd_bf.at[parity], sems16.at[1, parity]))

    def publish_copies(e):
        return (pltpu.make_async_copy(wgu_bf.at[0], wgu16_out.at[e], out_sems.at[0]),
                pltpu.make_async_copy(wd_bf.at[0], wd16_out.at[e], out_sems.at[1]))

    def wait_published():
        @pl.when(pending[0] == 1)
        def _():
            for c in publish_copies(0):
                c.wait()
            pending[0] = 0

    def start_fetch(e, parity, have):
        if reuse:
            @pl.when(have == 1)
            def _():
                for c in fetch16(e, parity):
                    c.start()

            @pl.when(have != 1)
            def _():
                for c in fetch(e, parity):
                    c.start()
        else:
            for c in fetch(e, parity):
                c.start()

    def finish_fetch(e, parity, have):
        def from_f32():
            for c in fetch(e, parity):
                c.wait()
            if publish:
                wait_published()
            wgu_bf[bf_slot(parity)] = wgu_f32[f32_slot(parity)].astype(jnp.bfloat16)
            wd_bf[bf_slot(parity)] = wd_f32[f32_slot(parity)].astype(jnp.bfloat16)
            if publish:
                for c in publish_copies(e):
                    c.start()
                pending[0] = 1

        if reuse:
            @pl.when(have == 1)
            def _():
                for c in fetch16(e, parity):
                    c.wait()

            pl.when(have != 1)(from_f32)
        else:
            from_f32()

    def load_weights_if_first(i):
        parity = slot_ref[i]
        e = blk_exp_ref[i]
        have = have_ref[i] if reuse else 0

        @pl.when(first_ref[i] == 1)
        def _():
            @pl.when(i == 0)
            def _():
                if publish:
                    pending[0] = 0
                start_fetch(e, parity, have)

            finish_fetch(e, parity, have)

            @pl.when(next_ref[i] >= 0)
            def _():
                start_fetch(next_ref[i], 1 - parity, have_next_ref[i] if reuse else 0)

    def compute(rows, i):
        e = blk_exp_ref[i]
        sl = bf_slot(slot_ref[i])
        x = _unpack_bf16_pairs(xs_ref[rows, :]).astype(jnp.bfloat16)
        gu = jnp.dot(x, wgu_bf[sl], preferred_element_type=jnp.float32) + bgu_ref[e]
        g = jnp.minimum(gu[:, :d_ff], SWIGLU_LIMIT)
        u = jnp.clip(gu[:, d_ff:], -SWIGLU_LIMIT, SWIGLU_LIMIT)
        act = g * jax.nn.sigmoid(SWIGLU_ALPHA * g) * (u + 1.0)
        o = jnp.dot(act.astype(jnp.bfloat16), wd_bf[sl], preferred_element_type=jnp.float32) + bd_ref[e]
        out_ref[rows, :] = _pack_bf16_pairs(o)

    i0 = pl.program_id(0) * BLOCKS_PER_STEP
    n_active = n_active_ref[0]

    def run(sub0, n_sub):
        first = i0 + sub0
        rows = slice(sub0 * MOE_BLOCK, (sub0 + n_sub) * MOE_BLOCK)
        if n_sub == 1:
            @pl.when(first < n_active)
            def _():
                load_weights_if_first(first)
                compute(rows, first)

            @pl.when(first >= n_active)
            def _():
                out_ref[rows, :] = jnp.zeros((MOE_BLOCK, out_ref.shape[1]), out_ref.dtype)
            return

        same = first + n_sub - 1 < n_active
        for sub in range(1, n_sub):
            same = same & (blk_exp_ref[first + sub] == blk_exp_ref[first])

        @pl.when(same)
        def _():
            load_weights_if_first(first)
            compute(rows, first)

        @pl.when(jnp.logical_not(same))
        def _():
            run(sub0, n_sub // 2)
            run(sub0 + n_sub // 2, n_sub // 2)

    run(0, BLOCKS_PER_STEP)

    if publish:
        @pl.when(pl.program_id(0) == pl.num_programs(0) - 1)
        def _():
            wait_published()


def _experts(plan, xs, wgu, bgu, wd, bd, *, publish=False, reuse=None):
    blk_exp, first, slot, nxt, n_active = plan
    n_rows, half = xs.shape
    n_exp, d, two_f = wgu.shape
    d_ff = wd.shape[1]
    n_blocks = n_rows // MOE_BLOCK
    step_rows = MOE_BLOCK * BLOCKS_PER_STEP
    assert n_rows % step_rows == 0
    any_spec = pl.BlockSpec(memory_space=pl.ANY)
    scalars = [blk_exp, first, slot, nxt, n_active]
    operands = [xs, wgu, bgu.reshape(n_exp, 1, two_f), wd, bd.reshape(n_exp, 1, d)]
    in_specs = [
        pl.BlockSpec((step_rows, half), lambda i, *_: (i, 0)),
        any_spec,
        pl.BlockSpec((n_exp, 1, two_f), lambda i, *_: (0, 0, 0)),
        any_spec,
        pl.BlockSpec((n_exp, 1, d), lambda i, *_: (0, 0, 0)),
    ]
    out_specs = [pl.BlockSpec((step_rows, half), lambda i, *_: (i, 0))]
    out_shape = [jax.ShapeDtypeStruct((n_rows, half), jnp.uint32)]
    n_f32_slots, n_bf_slots = (1, 2) if reuse else (2, 1)
    scratch = [
        pltpu.VMEM((n_f32_slots, d, two_f), jnp.float32),
        pltpu.VMEM((n_f32_slots, d_ff, d), jnp.float32),
        pltpu.VMEM((n_bf_slots, d, two_f), jnp.bfloat16),
        pltpu.VMEM((n_bf_slots, d_ff, d), jnp.bfloat16),
        pltpu.SemaphoreType.DMA((2, 2)),
    ]
    if reuse:
        wgu16, wd16, have, have_next = reuse
        scalars += [have, have_next]
        operands += [wgu16, wd16]
        in_specs += [any_spec, any_spec]
        scratch += [pltpu.SemaphoreType.DMA((2, 2))]
    if publish:
        out_specs += [any_spec, any_spec]
        out_shape += [jax.ShapeDtypeStruct((n_exp, d, two_f), jnp.bfloat16),
                      jax.ShapeDtypeStruct((n_exp, d_ff, d), jnp.bfloat16)]
        scratch += [pltpu.SemaphoreType.DMA((2,)), pltpu.SMEM((1,), jnp.int32)]
    grid_spec = pltpu.PrefetchScalarGridSpec(
        num_scalar_prefetch=len(scalars),
        grid=(n_blocks // BLOCKS_PER_STEP,),
        in_specs=in_specs,
        out_specs=tuple(out_specs),
        scratch_shapes=scratch,
    )
    out = pl.pallas_call(
        functools.partial(_experts_kernel, publish=publish, reuse=bool(reuse)),
        grid_spec=grid_spec,
        out_shape=tuple(out_shape),
        compiler_params=pltpu.CompilerParams(
            dimension_semantics=("arbitrary",), vmem_limit_bytes=VMEM_LIMIT),
        name="experts",
    )(*scalars, *operands)
    return out if publish else out[0]


def _combine_kernel(rows_ref, gate_ref, h_ref, gfin_ref, *rest):
    y_ref = rest[-1]
    acc = h_ref[...]
    for k in range(TOP_K):
        acc = acc + gate_ref[:, k:k + 1] * _unpack_bf16_pairs(rows_ref[k])
    y_ref[...] = _rms(acc, gfin_ref[...])


def _combine(rows4, gate_t, h2d, gfin, *, tok0=0, y_prev=None, y_rows=None, y_tok0=0):
    d = h2d.shape[1]
    n, half = rows4.shape[1:]
    n_tok = h2d.shape[0] if y_rows is None else y_rows
    tile = min(n, COMBINE_TILE)
    t0 = tok0 // tile
    y0 = y_tok0 // tile
    in_specs = [
        pl.BlockSpec((TOP_K, tile, half), lambda i: (0, i, 0)),
        pl.BlockSpec((tile, gate_t.shape[1]), lambda i: (t0 + i, 0)),
        pl.BlockSpec((tile, d), lambda i: (t0 + i, 0)),
        pl.BlockSpec((1, d), lambda i: (0, 0)),
    ]
    args = [rows4, gate_t, h2d, gfin]
    aliases = {}
    if y_prev is not None:
        in_specs.append(pl.BlockSpec(memory_space=pl.ANY))
        args.append(y_prev)
        aliases = {4: 0}
    return pl.pallas_call(
        _combine_kernel,
        grid=(n // tile,),
        in_specs=in_specs,
        out_specs=pl.BlockSpec((tile, d), lambda i: (y0 + i, 0)),
        out_shape=jax.ShapeDtypeStruct((n_tok, d), jnp.float32),
        input_output_aliases=aliases,
        compiler_params=pltpu.CompilerParams(
            dimension_semantics=("arbitrary",), vmem_limit_bytes=VMEM_LIMIT),
        name="combine",
    )(*args)


def _pad_hist(hist, rows):
    return jnp.pad(hist, ((0, 0), (rows - hist.shape[1], 0), (0, 0)))


def kernel(x_prompt, x_sample, state_conv_a, state_conv_b, norm_mix_g, w_in, conv_a_w, conv_b_w, conv_b_b, conv_ln_g, conv_ln_b, out_norm_a_g, out_norm_b_g, w_out, norm_ffn_g, w_router, b_router, w_gate_up, b_gate_up, w_down, b_down, final_norm_g):
    depth = w_in.shape[0]
    assert depth == 1, "single-layer trunk"
    bf16 = jnp.bfloat16
    bsz, seq, d = x_prompt.shape
    dec_b, dec_s, _ = x_sample.shape
    d_a, d_b = conv_a_w.shape[-1], conv_b_w.shape[-1]
    n_exp = w_router.shape[-1]

    params = (norm_mix_g[0][None], w_in[0].astype(bf16), conv_a_w[0], conv_b_w[0], conv_b_b[0][None],
              conv_ln_g[0][None], conv_ln_b[0][None], out_norm_a_g[0][None], out_norm_b_g[0][None],
              w_out[0].astype(bf16), norm_ffn_g[0][None], w_router[0].T.astype(bf16),
              b_router[0][:, None])

    zero_a = jnp.zeros((bsz, HIST_A_ROWS, d_a), jnp.float32)
    zero_b = jnp.zeros((bsz, HIST_B_ROWS, d_b), jnp.float32)
    cnt0 = jnp.zeros((n_exp, 1), jnp.float32)
    gfin = final_norm_g[None]
    ts = min(seq, MIXER_ROWS)
    n_p = bsz * seq

    sizes = [bsz * f // sum(GROUP_SPLIT) for f in GROUP_SPLIT]
    if bsz % sum(GROUP_SPLIT) != 0:
        sizes = [bsz]
    groups = []
    tok_start = []
    for g, per_b in enumerate(sizes):
        b0 = sum(sizes[:g])
        tok_start.append(b0 * seq)
        h, hnp, idx, gate, rank, cnt, na, nb_ = _mixer(
            x_prompt, zero_a, zero_b, cnt0, params, nb=1, ts=ts, b0=b0, bsz=per_b)
        parts = [dict(h=h.reshape(per_b * seq, d), hnp=hnp, idx=idx, gate_t=gate, rank=rank)]
        states = [(na, nb_)]
        if g == len(sizes) - 1:
            h, hnp, idx, gate, rank, cnt, na_s, nb_s = _mixer(
                x_sample, _pad_hist(state_conv_a[0], HIST_A_ROWS), _pad_hist(state_conv_b[0], HIST_B_ROWS),
                cnt, params, nb=dec_b, ts=dec_s)
            parts.append(dict(h=h.reshape(dec_b * dec_s, d), hnp=hnp, idx=idx, gate_t=gate, rank=rank))
        n_pairs = sum(p["hnp"].shape[0] for p in parts) * TOP_K
        plan, pad_start, n_blocks, has, reuse_tables = _block_plan(cnt[:, 0].astype(jnp.int32), n_pairs, n_exp)
        pos = _positions(pad_start, [(p["idx"], p["rank"]) for p in parts])
        xs = _sc_dispatch([(q, p["hnp"]) for q, p in zip(pos, parts)], n_blocks * MOE_BLOCK)
        groups.append(dict(parts=parts, pos=pos, plan=plan, xs=xs, states=states, has=has,
                           reuse_tables=reuse_tables))

    weights = (w_gate_up[0], b_gate_up[0], w_down[0], b_down[0])
    pub = groups[-1]
    if len(groups) > 1:
        pub["rows"], wgu16, wd16 = _experts(pub["plan"], pub["xs"], *weights, publish=True)
        for grp in groups[:-1]:
            grp["rows"] = _experts(grp["plan"], grp["xs"], *weights,
                                   reuse=(wgu16, wd16) + grp["reuse_tables"](pub["has"]))
    else:
        pub["rows"] = _experts(pub["plan"], pub["xs"], *weights)

    gathered = []
    for g, grp in enumerate(groups):
        pos0 = grp["pos"][0]
        n_g = pos0.shape[1]
        want = 1 if (g == len(groups) - 1 and len(groups) > 1) else COMBINE_CHUNKS
        n_chunks = want if n_g % (want * SC_CHUNK * SC_WORKERS) == 0 else 1
        per = n_g // n_chunks
        for c in range(n_chunks):
            pos_parts = [pos0[:, c * per:(c + 1) * per]] + (grp["pos"][1:] if c == 0 else [])
            gathered.append((g, c * per, _sc_gather(pos_parts, grp["rows"])))

    y_p = None
    for g, tok0, got in sorted(gathered, key=lambda t: (t[0] != len(groups) - 1, t[0], t[1])):
        grp = groups[g]
        main = grp["parts"][0]
        if len(got) > 1:
            smp = grp["parts"][1]
            y_s = _combine(got[1], smp["gate_t"], smp["h"], gfin).reshape(dec_b, dec_s, d)
        y_p = _combine(got[0], main["gate_t"], main["h"], gfin, tok0=tok0,
                       y_prev=y_p, y_rows=n_p, y_tok0=tok_start[g] + tok0)
    y_p = y_p.reshape(bsz, seq, d)

    na_p = jnp.concatenate([grp["states"][0][0] for grp in groups], axis=0)
    nb_p = jnp.concatenate([grp["states"][0][1] for grp in groups], axis=0)
    return (y_p, y_s, na_p[None], nb_p[None], na_s[None], nb_s[None])


def _block_plan(counts, n_pairs, n_exp):
    padded = (counts + MOE_BLOCK - 1) // MOE_BLOCK * MOE_BLOCK
    pad_end = jnp.cumsum(padded)
    pad_start = pad_end - padded
    n_blocks = -(-n_pairs // MOE_BLOCK) + n_exp
    n_blocks = -(-n_blocks // BLOCKS_PER_STEP) * BLOCKS_PER_STEP
    blk_start = jnp.arange(n_blocks, dtype=jnp.int32) * MOE_BLOCK
    blk_exp = jnp.minimum(jnp.sum(blk_start[:, None] >= pad_end[None, :], axis=1),
                          n_exp - 1).astype(jnp.int32)
    n_active = (pad_end[-1:] // MOE_BLOCK).astype(jnp.int32)
    e_ids = jnp.arange(n_exp, dtype=jnp.int32)
    has = padded > 0
    slot_e = (jnp.cumsum(has.astype(jnp.int32)) - 1) % 2
    later = jnp.where(has, e_ids, n_exp)
    next_e = jnp.concatenate([lax.cummin(later, reverse=True)[1:], jnp.full((1,), n_exp, jnp.int32)])
    next_e = jnp.where(next_e >= n_exp, -1, next_e)
    blk_onehot = blk_exp[:, None] == e_ids[None, :]

    def per_block(table):
        return jnp.sum(jnp.where(blk_onehot, table[None, :], 0), axis=1).astype(jnp.int32)

    blk_first = ((blk_start == per_block(pad_start)) & (blk_start < pad_end[-1])).astype(jnp.int32)
    plan = (blk_exp, blk_first, per_block(slot_e), per_block(next_e), n_active)

    def reuse_tables(has_pub):
        pub = has_pub.astype(jnp.int32)
        pub_next = jnp.sum(jnp.where(next_e[:, None] == e_ids[None, :], pub[None, :], 0), axis=1)
        return per_block(pub), per_block(pub_next)

    return plan, pad_start.astype(jnp.int32), n_blocks, has, reuse_tables
```

```python
import functools

import jax
import jax.numpy as jnp
from jax import lax
from jax.experimental import pallas as pl
from jax.experimental.pallas import tpu as pltpu
from jax.experimental.pallas import tpu_sc as plsc

SC_CORES = 2
SC_SUBCORES = 16
SC_WORKERS = SC_CORES * SC_SUBCORES
SC_CHUNK = 128
CONV_A_W = 3
CONV_B_W = 31
TOP_K = 4
SWIGLU_LIMIT = 7.0
SWIGLU_ALPHA = 1.702
NORM_EPS = 1e-5
MOE_BLOCK = 256
BLOCKS_PER_STEP = 4
SUBLANES = 8
LANES = 128
CONV_ROW_CHUNK = 128
HIST_A_ROWS = 8
HIST_B_ROWS = 32
MIXER_ROWS = 1024
PROJ_SECTION = 256
COMBINE_TILE = 1024
POS_TILE = 4096
GROUP_SPLIT = (1, 3)
COMBINE_CHUNKS = 1
VMEM_LIMIT = 56 * 1024 * 1024


def _rms(x, g):
    return x * lax.rsqrt(jnp.mean(x * x, axis=-1, keepdims=True) + NORM_EPS) * g


def _zero_tile_from(v):
    bits = pltpu.bitcast(v[0:SUBLANES, 0:LANES], jnp.uint32)
    return pltpu.bitcast((bits >> 16) >> 16, jnp.float32)


def _pack_bf16_pairs(a):
    w = a.shape[-1] // 2
    lo = pltpu.bitcast(a[:, :w].astype(jnp.bfloat16).astype(jnp.float32), jnp.uint32)
    hi = pltpu.bitcast(a[:, w:].astype(jnp.bfloat16).astype(jnp.float32), jnp.uint32)
    return (lo >> 16) | (hi & jnp.uint32(0xFFFF0000))


def _unpack_bf16_pairs(wd):
    lo = pltpu.bitcast(wd << 16, jnp.float32)
    hi = pltpu.bitcast(wd & jnp.uint32(0xFFFF0000), jnp.float32)
    return jnp.concatenate([lo, hi], axis=-1)


def _conv_b_chunk(ub_buf, cbw_ref, n_i, r0, rc, c0):
    cs = slice(c0, c0 + LANES)
    lead = HIST_B_ROWS - (CONV_B_W - 1)
    acc = None
    for b in range(SUBLANES):
        rows = rc if b == 0 else rc + SUBLANES
        q = None
        for a in range((HIST_B_ROWS + SUBLANES) // SUBLANES):
            k = SUBLANES * a + b - lead
            if k < 0 or k >= CONV_B_W:
                continue
            start = r0 + SUBLANES * a
            term = cbw_ref[k:k + 1, cs] * ub_buf[n_i, start:start + rows, cs]
            q = term if q is None else q + term
        part = q if b == 0 else q[b:b + rc]
        acc = part if acc is None else acc + part
    return acc


def _mixer_kernel(x_ref, hista_ref, histb_ref, cnt0_ref, tri_ref,
                  gmix_ref, win_ref, caw_ref, cbw_ref, cbb_ref, lng_ref, lnb_ref,
                  ga_ref, gb_ref, wout_ref, gffn_ref, wrt_ref, br_ref,
                  h_ref, hnp_ref, idx_ref, gate_ref, rank_ref, cnt_ref, newa_ref, newb_ref,
                  ua_buf, ub_buf, cb_buf, cnt_acc, *, nb, ts, d_a, d_b, n_exp):
    b = pl.program_id(0)
    s = pl.program_id(1)
    m = nb * ts
    d = x_ref.shape[-1]

    @pl.when(s == 0)
    def _():
        ua_buf[:, 0:HIST_A_ROWS, :] = hista_ref[...]
        ub_buf[:, 0:HIST_B_ROWS, :] = histb_ref[...]

    @pl.when(s != 0)
    def _():
        ua_buf[:, 0:HIST_A_ROWS, :] = ua_buf[:, ts:ts + HIST_A_ROWS, :]
        ub_buf[:, 0:HIST_B_ROWS, :] = ub_buf[:, ts:ts + HIST_B_ROWS, :]

    @pl.when((b == 0) & (s == 0))
    def _():
        cnt_acc[...] = cnt0_ref[...]

    x = x_ref[...].reshape(m, d)
    n = _rms(x, gmix_ref[...]).astype(jnp.bfloat16)
    def proj(c0, width):
        return jnp.dot(n, win_ref[:, c0:c0 + width], preferred_element_type=jnp.float32)

    sec = min(d_b, PROJ_SECTION)
    for c0 in range(0, d_b, sec):
        v_b = proj(3 * d_a + c0, sec)
        g_b = proj(3 * d_a + d_b + c0, sec)
        ub_buf[:, HIST_B_ROWS:HIST_B_ROWS + ts, c0:c0 + sec] = (v_b * jax.nn.sigmoid(g_b)).reshape(nb, ts, sec)

    gate_c = proj(d_a, d_a)
    xt = proj(2 * d_a, d_a)
    gate_b = proj(0, d_a)
    ua_buf[:, HIST_A_ROWS:HIST_A_ROWS + ts, :] = (gate_c * xt).reshape(nb, ts, d_a)
    conv_a = jnp.zeros((nb, ts, d_a), jnp.float32)
    for k in range(CONV_A_W):
        off = HIST_A_ROWS - (CONV_A_W - 1) + k
        conv_a = conv_a + caw_ref[k:k + 1, :] * ua_buf[:, off:off + ts, :]
    y_a = gate_b * conv_a.reshape(m, d_a)

    mix_a = jnp.dot(_rms(y_a, ga_ref[...]).astype(jnp.bfloat16), wout_ref[0:d_a, :],
                    preferred_element_type=jnp.float32)

    rc = min(ts, CONV_ROW_CHUNK)
    chunks = [(n_i, r0, c0) for n_i in range(nb) for r0 in range(0, ts, rc) for c0 in range(0, d_b, LANES)]
    anchors = {len(chunks) * 3 // 8: gate_c, len(chunks) * 4 // 8: xt, len(chunks) * 6 // 8: gate_b,
               len(chunks) - 1: mix_a}
    for ci, (n_i, r0, c0) in enumerate(chunks):
        acc = _conv_b_chunk(ub_buf, cbw_ref, n_i, r0, rc, c0)
        if ci in anchors:
            acc = (acc.reshape(rc // SUBLANES, SUBLANES, LANES) + _zero_tile_from(anchors[ci])[None]
                   ).reshape(rc, LANES)
        cb_buf[n_i, r0:r0 + rc, c0:c0 + LANES] = acc
    cb = cb_buf[...].reshape(m, d_b) + cbb_ref[...]
    mu = jnp.mean(cb, axis=-1, keepdims=True)
    xc = cb - mu
    var = jnp.mean(xc * xc, axis=-1, keepdims=True)
    ln = xc * lax.rsqrt(var + NORM_EPS) * lng_ref[...] + lnb_ref[...]
    y_b = ln * jax.nn.sigmoid(ln)

    mix_b = jnp.dot(_rms(y_b, gb_ref[...]).astype(jnp.bfloat16), wout_ref[d_a:d_a + d_b, :],
                    preferred_element_type=jnp.float32)
    h = x + (mix_a + mix_b)
    h_ref[...] = h.reshape(nb, ts, d)

    hn = _rms(h, gffn_ref[...])
    hnp_ref[...] = _pack_bf16_pairs(hn)

    logits = lax.dot_general(wrt_ref[...], hn.astype(jnp.bfloat16), (((1,), (1,)), ((), ())),
                             preferred_element_type=jnp.float32) + br_ref[...]
    e_iota = lax.broadcasted_iota(jnp.int32, (n_exp, m), 0)
    cur = logits
    vals, idxs, sels = [], [], []
    for _ in range(TOP_K):
        mx = jnp.max(cur, axis=0, keepdims=True)
        ix = jnp.min(jnp.where(cur == mx, e_iota, n_exp), axis=0, keepdims=True)
        sel = e_iota == ix
        vals.append(mx)
        idxs.append(ix)
        sels.append(sel)
        cur = jnp.where(sel, -jnp.inf, cur)
    exps = [jnp.exp(v - vals[0]) for v in vals]
    denom = exps[0] + exps[1] + exps[2] + exps[3]
    gates = [e / denom for e in exps]

    onehot = jnp.zeros((n_exp, m), jnp.float32)
    for sel in sels:
        onehot = onehot + sel.astype(jnp.float32)
    before = jnp.dot(onehot.astype(jnp.bfloat16), tri_ref[...],
                     preferred_element_type=jnp.float32) + cnt_acc[...]
    ranks = [jnp.sum(jnp.where(sel, before, 0.0), axis=0, keepdims=True) for sel in sels]
    new_cnt = cnt_acc[...] + jnp.sum(onehot, axis=1, keepdims=True)
    cnt_acc[...] = new_cnt
    cnt_ref[...] = new_cnt

    idx_ref[...] = jnp.concatenate(idxs, axis=0)
    gate_ref[...] = jnp.concatenate(gates + [jnp.zeros((SUBLANES - TOP_K, m), jnp.float32)], axis=0).T
    rank_ref[...] = jnp.concatenate(ranks, axis=0).astype(jnp.int32)

    @pl.when(s == pl.num_programs(1) - 1)
    def _():
        newa_ref[...] = ua_buf[:, ts + HIST_A_ROWS - (CONV_A_W - 1):ts + HIST_A_ROWS, :]
        newb_ref[...] = ub_buf[:, ts + HIST_B_ROWS - (CONV_B_W - 1):ts + HIST_B_ROWS, :]


def _mixer(x, hist_a, hist_b, cnt0, params, *, nb, ts, b0=0, bsz=None):
    (gmix, win, caw, cbw, cbb, lng, lnb, ga, gb, wout, gffn, wrt, br) = params
    _, seq, d = x.shape
    bsz = x.shape[0] if bsz is None else bsz
    d_a, d_b = caw.shape[-1], cbw.shape[-1]
    n_exp = wrt.shape[0]
    m = nb * ts
    n_tok = bsz * seq
    g0 = b0 // nb
    grid = (bsz // nb, seq // ts)
    tri = (jnp.arange(m)[:, None] < jnp.arange(m)[None, :]).astype(jnp.bfloat16)

    def const(shape):
        return pl.BlockSpec(shape, lambda i, j: (0,) * len(shape))

    tok_map = lambda i, j: (0, i * (seq // ts) + j)
    kern = functools.partial(_mixer_kernel, nb=nb, ts=ts, d_a=d_a, d_b=d_b, n_exp=n_exp)
    out_shape = (
        jax.ShapeDtypeStruct((bsz, seq, d), jnp.float32),
        jax.ShapeDtypeStruct((n_tok, d // 2), jnp.uint32),
        jax.ShapeDtypeStruct((TOP_K, n_tok), jnp.int32),
        jax.ShapeDtypeStruct((n_tok, SUBLANES), jnp.float32),
        jax.ShapeDtypeStruct((TOP_K, n_tok), jnp.int32),
        jax.ShapeDtypeStruct((n_exp, 1), jnp.float32),
        jax.ShapeDtypeStruct((bsz, CONV_A_W - 1, d_a), jnp.float32),
        jax.ShapeDtypeStruct((bsz, CONV_B_W - 1, d_b), jnp.float32),
    )
    return pl.pallas_call(
        kern,
        grid=grid,
        in_specs=[
            pl.BlockSpec((nb, ts, d), lambda i, j: (g0 + i, j, 0)),
            pl.BlockSpec((nb, HIST_A_ROWS, d_a), lambda i, j: (g0 + i, 0, 0)),
            pl.BlockSpec((nb, HIST_B_ROWS, d_b), lambda i, j: (g0 + i, 0, 0)),
            const((n_exp, 1)),
            const((m, m)),
            const((1, d)), const(win.shape), const(caw.shape), const(cbw.shape), const((1, d_b)),
            const((1, d_b)), const((1, d_b)), const((1, d_a)), const((1, d_b)), const(wout.shape),
            const((1, d)), const(wrt.shape), const((n_exp, 1)),
        ],
        out_specs=(
            pl.BlockSpec((nb, ts, d), lambda i, j: (i, j, 0)),
            pl.BlockSpec((m, d // 2), lambda i, j: (i * (seq // ts) + j, 0)),
            pl.BlockSpec((TOP_K, m), tok_map),
            pl.BlockSpec((m, SUBLANES), lambda i, j: (i * (seq // ts) + j, 0)),
            pl.BlockSpec((TOP_K, m), tok_map),
            pl.BlockSpec((n_exp, 1), lambda i, j: (0, 0)),
            pl.BlockSpec((nb, CONV_A_W - 1, d_a), lambda i, j: (i, 0, 0)),
            pl.BlockSpec((nb, CONV_B_W - 1, d_b), lambda i, j: (i, 0, 0)),
        ),
        out_shape=out_shape,
        scratch_shapes=[
            pltpu.VMEM((nb, HIST_A_ROWS + ts, d_a), jnp.float32),
            pltpu.VMEM((nb, HIST_B_ROWS + ts, d_b), jnp.float32),
            pltpu.VMEM((nb, ts, d_b), jnp.float32),
            pltpu.VMEM((n_exp, 1), jnp.float32),
        ],
        compiler_params=pltpu.CompilerParams(
            dimension_semantics=("arbitrary", "arbitrary"), vmem_limit_bytes=VMEM_LIMIT),
        name="mixer",
    )(x, hist_a, hist_b, cnt0, tri, gmix, win, caw, cbw, cbb, lng, lnb, ga, gb, wout, gffn, wrt, br)


def _positions_kernel(start_ref, *refs):
    n_parts = len(refs) // 3
    for p in range(n_parts):
        idx = refs[2 * p][...]
        pos = refs[2 * p + 1][...]
        for e in range(start_ref.shape[0]):
            pos = pos + jnp.where(idx == e, start_ref[e], 0)
        refs[2 * n_parts + p][...] = pos


def _positions(pad_start, parts):
    n_0 = parts[0][0].shape[1]
    tile = min(n_0, POS_TILE)
    specs = [pl.BlockSpec((TOP_K, tile), lambda i: (0, i))]
    specs += [pl.BlockSpec((TOP_K, idx.shape[1]), lambda i: (0, 0)) for idx, _ in parts[1:]]
    in_specs = [pl.BlockSpec(memory_space=pltpu.SMEM)]
    args = [pad_start]
    for spec, (idx, rank) in zip(specs, parts):
        in_specs += [spec, spec]
        args += [idx, rank]
    out = pl.pallas_call(
        _positions_kernel,
        grid=(n_0 // tile,),
        in_specs=in_specs,
        out_specs=tuple(specs),
        out_shape=tuple(jax.ShapeDtypeStruct(idx.shape, jnp.int32) for idx, _ in parts),
        compiler_params=pltpu.CompilerParams(dimension_semantics=("arbitrary",)),
        name="positions",
    )(*args)
    return list(out)


def _sc_mesh():
    return plsc.VectorSubcoreMesh(core_axis_name="c", subcore_axis_name="s",
                                  num_cores=SC_CORES, num_subcores=SC_SUBCORES)


def _sc_worker_id():
    return lax.axis_index("s") * SC_CORES + lax.axis_index("c")


def _sc_chunks(n_tok, wid):
    n_chunks = n_tok // SC_CHUNK
    assert n_tok % SC_CHUNK == 0
    if n_chunks % SC_WORKERS == 0:
        per_w = n_chunks // SC_WORKERS
        return None, per_w, lambda i: (wid * per_w + i) * SC_CHUNK
    assert n_chunks <= SC_WORKERS
    return wid < n_chunks, 1, lambda i: wid * SC_CHUNK


def _sc_for_each_chunk(n_tok, fn):
    wid = _sc_worker_id()
    pred, trips, base = _sc_chunks(n_tok, wid)

    def run():
        @pl.loop(0, trips)
        def _(i):
            fn(pl.multiple_of(base(i), SC_CHUNK))

    if pred is None:
        run()
    else:
        pl.when(pred)(run)


def _sc_dispatch(parts, n_rows):
    half = parts[0][1].shape[1]
    n_parts = len(parts)

    @functools.partial(
        pl.kernel, mesh=_sc_mesh(),
        out_type=jax.ShapeDtypeStruct((n_rows, half), jnp.uint32),
        scratch_types=[pltpu.VMEM((TOP_K, SC_CHUNK), jnp.int32),
                       pltpu.VMEM((SC_CHUNK, half), jnp.uint32),
                       pltpu.SemaphoreType.DMA],
        compiler_params=pltpu.CompilerParams(use_tc_tiling_on_sc=True),
        name="sc_dispatch")
    def k(*refs):
        xs_hbm, idx_v, rows_v, sem = refs[2 * n_parts:]

        def move(pos_hbm, src_hbm):
            def fn(base):
                pltpu.sync_copy(pos_hbm.at[:, pl.ds(base, SC_CHUNK)], idx_v)
                pltpu.sync_copy(src_hbm.at[pl.ds(base, SC_CHUNK)], rows_v)
                copies = [pltpu.async_copy(rows_v, xs_hbm.at[idx_v.at[j]], sem) for j in range(TOP_K)]
                for c in copies:
                    c.wait()
            return fn

        for p, (_, hnp) in enumerate(parts):
            _sc_for_each_chunk(hnp.shape[0], move(refs[2 * p], refs[2 * p + 1]))

    return k(*[a for part in parts for a in part])


def _sc_gather(pos_parts, rows):
    half = rows.shape[1]
    n_parts = len(pos_parts)

    @functools.partial(
        pl.kernel, mesh=_sc_mesh(),
        out_type=tuple(jax.ShapeDtypeStruct((TOP_K, p.shape[1], half), jnp.uint32) for p in pos_parts),
        scratch_types=[pltpu.VMEM((TOP_K, SC_CHUNK), jnp.int32),
                       pltpu.VMEM((SC_CHUNK, half), jnp.uint32),
                       pltpu.SemaphoreType.DMA],
        compiler_params=pltpu.CompilerParams(use_tc_tiling_on_sc=True),
        name="sc_gather")
    def k(*refs):
        pos_hbms, rows_hbm = refs[:n_parts], refs[n_parts]
        out_hbms = refs[n_parts + 1:2 * n_parts + 1]
        idx_v, buf_v, sem = refs[2 * n_parts + 1:]

        def move(pos_hbm, out_hbm):
            def fn(base):
                pltpu.sync_copy(pos_hbm.at[:, pl.ds(base, SC_CHUNK)], idx_v)
                for j in range(TOP_K):
                    pltpu.async_copy(rows_hbm.at[idx_v.at[j]], buf_v, sem).wait()
                    pltpu.sync_copy(buf_v, out_hbm.at[j, pl.ds(base, SC_CHUNK)])
            return fn

        for pos_hbm, out_hbm, p in zip(pos_hbms, out_hbms, pos_parts):
            _sc_for_each_chunk(p.shape[1], move(pos_hbm, out_hbm))

    out = k(*pos_parts, rows)
    return out if isinstance(out, (tuple, list)) else (out,)


def _experts_kernel(*refs, publish, reuse):
    blk_exp_ref, first_ref, slot_ref, next_ref, n_active_ref = refs[:5]
    refs = refs[5:]
    if reuse:
        have_ref, have_next_ref = refs[:2]
        refs = refs[2:]
    xs_ref, wgu_hbm, bgu_ref, wd_hbm, bd_ref = refs[:5]
    refs = refs[5:]
    if reuse:
        wgu16_hbm, wd16_hbm = refs[:2]
        refs = refs[2:]
    out_ref = refs[0]
    refs = refs[1:]
    if publish:
        wgu16_out, wd16_out = refs[:2]
        refs = refs[2:]
    wgu_f32, wd_f32, wgu_bf, wd_bf, sems = refs[:5]
    refs = refs[5:]
    if publish:
        out_sems, pending = refs
    if reuse:
        sems16, = refs
    d_ff = wd_bf.shape[-2]

    def bf_slot(parity):
        return parity if reuse else 0

    def f32_slot(parity):
        return 0 if reuse else parity

    def fetch(e, parity):
        sl = f32_slot(parity)
        return (pltpu.make_async_copy(wgu_hbm.at[e], wgu_f32.at[sl], sems.at[0, sl]),
                pltpu.make_async_copy(wd_hbm.at[e], wd_f32.at[sl], sems.at[1, sl]))

    def fetch16(e, parity):
        return (pltpu.make_async_copy(wgu16_hbm.at[e], wgu_bf.at[parity], sems16.at[0, parity]),
                pltpu.make_async_copy(wd16_hbm.at[e], wd_bf.at[parity], sems16.at[1, parity]))

    def publish_copies(e):
        return (pltpu.make_async_copy(wgu_bf.at[0], wgu16_out.at[e], out_sems.at[0]),
                pltpu.make_async_copy(wd_bf.at[0], wd16_out.at[e], out_sems.at[1]))

    def wait_published():
        @pl.when(pending[0] == 1)
        def _():
            for c in publish_copies(0):
                c.wait()
            pending[0] = 0

    def start_fetch(e, parity, have):
        if reuse:
            @pl.when(have == 1)
            def _():
                for c in fetch16(e, parity):
                    c.start()

            @pl.when(have != 1)
            def _():
                for c in fetch(e, parity):
                    c.start()
        else:
            for c in fetch(e, parity):
                c.start()

    def finish_fetch(e, parity, have):
        def from_f32():
            for c in fetch(e, parity):
                c.wait()
            if publish:
                wait_published()
            wgu_bf[bf_slot(parity)] = wgu_f32[f32_slot(parity)].astype(jnp.bfloat16)
            wd_bf[bf_slot(parity)] = wd_f32[f32_slot(parity)].astype(jnp.bfloat16)
            if publish:
                for c in publish_copies(e):
                    c.start()
                pending[0] = 1

        if reuse:
            @pl.when(have == 1)
            def _():
                for c in fetch16(e, parity):
                    c.wait()

            pl.when(have != 1)(from_f32)
        else:
            from_f32()

    def load_weights_if_first(i):
        parity = slot_ref[i]
        e = blk_exp_ref[i]
        have = have_ref[i] if reuse else 0

        @pl.when(first_ref[i] == 1)
        def _():
            @pl.when(i == 0)
            def _():
                if publish:
                    pending[0] = 0
                start_fetch(e, parity, have)

            finish_fetch(e, parity, have)

            @pl.when(next_ref[i] >= 0)
            def _():
                start_fetch(next_ref[i], 1 - parity, have_next_ref[i] if reuse else 0)

    def compute(rows, i):
        e = blk_exp_ref[i]
        sl = bf_slot(slot_ref[i])
        x = _unpack_bf16_pairs(xs_ref[rows, :]).astype(jnp.bfloat16)
        gu = jnp.dot(x, wgu_bf[sl], preferred_element_type=jnp.float32) + bgu_ref[e]
        g = jnp.minimum(gu[:, :d_ff], SWIGLU_LIMIT)
        u = jnp.clip(gu[:, d_ff:], -SWIGLU_LIMIT, SWIGLU_LIMIT)
        act = g * jax.nn.sigmoid(SWIGLU_ALPHA * g) * (u + 1.0)
        o = jnp.dot(act.astype(jnp.bfloat16), wd_bf[sl], preferred_element_type=jnp.float32) + bd_ref[e]
        out_ref[rows, :] = _pack_bf16_pairs(o)

    i0 = pl.program_id(0) * BLOCKS_PER_STEP
    n_active = n_active_ref[0]

    def run(sub0, n_sub):
        first = i0 + sub0
        rows = slice(sub0 * MOE_BLOCK, (sub0 + n_sub) * MOE_BLOCK)
        if n_sub == 1:
            @pl.when(first < n_active)
            def _():
                load_weights_if_first(first)
                compute(rows, first)

            @pl.when(first >= n_active)
            def _():
                out_ref[rows, :] = jnp.zeros((MOE_BLOCK, out_ref.shape[1]), out_ref.dtype)
            return

        same = first + n_sub - 1 < n_active
        for sub in range(1, n_sub):
            same = same & (blk_exp_ref[first + sub] == blk_exp_ref[first])

        @pl.when(same)
        def _():
            load_weights_if_first(first)
            compute(rows, first)

        @pl.when(jnp.logical_not(same))
        def _():
            run(sub0, n_sub // 2)
            run(sub0 + n_sub // 2, n_sub // 2)

    run(0, BLOCKS_PER_STEP)

    if publish:
        @pl.when(pl.program_id(0) == pl.num_programs(0) - 1)
        def _():
            wait_published()


def _experts(plan, xs, wgu, bgu, wd, bd, *, publish=False, reuse=None):
    blk_exp, first, slot, nxt, n_active = plan
    n_rows, half = xs.shape
    n_exp, d, two_f = wgu.shape
    d_ff = wd.shape[1]
    n_blocks = n_rows // MOE_BLOCK
    step_rows = MOE_BLOCK * BLOCKS_PER_STEP
    assert n_rows % step_rows == 0
    any_spec = pl.BlockSpec(memory_space=pl.ANY)
    scalars = [blk_exp, first, slot, nxt, n_active]
    operands = [xs, wgu, bgu.reshape(n_exp, 1, two_f), wd, bd.reshape(n_exp, 1, d)]
    in_specs = [
        pl.BlockSpec((step_rows, half), lambda i, *_: (i, 0)),
        any_spec,
        pl.BlockSpec((n_exp, 1, two_f), lambda i, *_: (0, 0, 0)),
        any_spec,
        pl.BlockSpec((n_exp, 1, d), lambda i, *_: (0, 0, 0)),
    ]
    out_specs = [pl.BlockSpec((step_rows, half), lambda i, *_: (i, 0))]
    out_shape = [jax.ShapeDtypeStruct((n_rows, half), jnp.uint32)]
    n_f32_slots, n_bf_slots = (1, 2) if reuse else (2, 1)
    scratch = [
        pltpu.VMEM((n_f32_slots, d, two_f), jnp.float32),
        pltpu.VMEM((n_f32_slots, d_ff, d), jnp.float32),
        pltpu.VMEM((n_bf_slots, d, two_f), jnp.bfloat16),
        pltpu.VMEM((n_bf_slots, d_ff, d), jnp.bfloat16),
        pltpu.SemaphoreType.DMA((2, 2)),
    ]
    if reuse:
        wgu16, wd16, have, have_next = reuse
        scalars += [have, have_next]
        operands += [wgu16, wd16]
        in_specs += [any_spec, any_spec]
        scratch += [pltpu.SemaphoreType.DMA((2, 2))]
    if publish:
        out_specs += [any_spec, any_spec]
        out_shape += [jax.ShapeDtypeStruct((n_exp, d, two_f), jnp.bfloat16),
                      jax.ShapeDtypeStruct((n_exp, d_ff, d), jnp.bfloat16)]
        scratch += [pltpu.SemaphoreType.DMA((2,)), pltpu.SMEM((1,), jnp.int32)]
    grid_spec = pltpu.PrefetchScalarGridSpec(
        num_scalar_prefetch=len(scalars),
        grid=(n_blocks // BLOCKS_PER_STEP,),
        in_specs=in_specs,
        out_specs=tuple(out_specs),
        scratch_shapes=scratch,
    )
    out = pl.pallas_call(
        functools.partial(_experts_kernel, publish=publish, reuse=bool(reuse)),
        grid_spec=grid_spec,
        out_shape=tuple(out_shape),
        compiler_params=pltpu.CompilerParams(
            dimension_semantics=("arbitrary",), vmem_limit_bytes=VMEM_LIMIT),
        name="experts",
    )(*scalars, *operands)
    return out if publish else out[0]


def _combine_kernel(rows_ref, gate_ref, h_ref, gfin_ref, *rest):
    y_ref = rest[-1]
    acc = h_ref[...]
    for k in range(TOP_K):
        acc = acc + gate_ref[:, k:k + 1] * _unpack_bf16_pairs(rows_ref[k])
    y_ref[...] = _rms(acc, gfin_ref[...])


def _combine(rows4, gate_t, h2d, gfin, *, tok0=0, y_prev=None, y_rows=None, y_tok0=0):
    d = h2d.shape[1]
    n, half = rows4.shape[1:]
    n_tok = h2d.shape[0] if y_rows is None else y_rows
    tile = min(n, COMBINE_TILE)
    t0 = tok0 // tile
    y0 = y_tok0 // tile
    in_specs = [
        pl.BlockSpec((TOP_K, tile, half), lambda i: (0, i, 0)),
        pl.BlockSpec((tile, gate_t.shape[1]), lambda i: (t0 + i, 0)),
        pl.BlockSpec((tile, d), lambda i: (t0 + i, 0)),
        pl.BlockSpec((1, d), lambda i: (0, 0)),
    ]
    args = [rows4, gate_t, h2d, gfin]
    aliases = {}
    if y_prev is not None:
        in_specs.append(pl.BlockSpec(memory_space=pl.ANY))
        args.append(y_prev)
        aliases = {4: 0}
    return pl.pallas_call(
        _combine_kernel,
        grid=(n // tile,),
        in_specs=in_specs,
        out_specs=pl.BlockSpec((tile, d), lambda i: (y0 + i, 0)),
        out_shape=jax.ShapeDtypeStruct((n_tok, d), jnp.float32),
        input_output_aliases=aliases,
        compiler_params=pltpu.CompilerParams(
            dimension_semantics=("arbitrary",), vmem_limit_bytes=VMEM_LIMIT),
        name="combine",
    )(*args)


def _pad_hist(hist, rows):
    return jnp.pad(hist, ((0, 0), (rows - hist.shape[1], 0), (0, 0)))


def kernel(x_prompt, x_sample, state_conv_a, state_conv_b, norm_mix_g, w_in, conv_a_w, conv_b_w, conv_b_b, conv_ln_g, conv_ln_b, out_norm_a_g, out_norm_b_g, w_out, norm_ffn_g, w_router, b_router, w_gate_up, b_gate_up, w_down, b_down, final_norm_g):
    depth = w_in.shape[0]
    assert depth == 1, "single-layer trunk"
    bf16 = jnp.bfloat16
    bsz, seq, d = x_prompt.shape
    dec_b, dec_s, _ = x_sample.shape
    d_a, d_b = conv_a_w.shape[-1], conv_b_w.shape[-1]
    n_exp = w_router.shape[-1]

    params = (norm_mix_g[0][None], w_in[0].astype(bf16), conv_a_w[0], conv_b_w[0], conv_b_b[0][None],
              conv_ln_g[0][None], conv_ln_b[0][None], out_norm_a_g[0][None], out_norm_b_g[0][None],
              w_out[0].astype(bf16), norm_ffn_g[0][None], w_router[0].T.astype(bf16),
              b_router[0][:, None])

    zero_a = jnp.zeros((bsz, HIST_A_ROWS, d_a), jnp.float32)
    zero_b = jnp.zeros((bsz, HIST_B_ROWS, d_b), jnp.float32)
    cnt0 = jnp.zeros((n_exp, 1), jnp.float32)
    gfin = final_norm_g[None]
    ts = min(seq, MIXER_ROWS)
    n_p = bsz * seq

    sizes = [bsz * f // sum(GROUP_SPLIT) for f in GROUP_SPLIT]
    if bsz % sum(GROUP_SPLIT) != 0:
        sizes = [bsz]
    groups = []
    tok_start = []
    for g, per_b in enumerate(sizes):
        b0 = sum(sizes[:g])
        tok_start.append(b0 * seq)
        h, hnp, idx, gate, rank, cnt, na, nb_ = _mixer(
            x_prompt, zero_a, zero_b, cnt0, params, nb=1, ts=ts, b0=b0, bsz=per_b)
        parts = [dict(h=h.reshape(per_b * seq, d), hnp=hnp, idx=idx, gate_t=gate, rank=rank)]
        states = [(na, nb_)]
        if g == len(sizes) - 1:
            h, hnp, idx, gate, rank, cnt, na_s, nb_s = _mixer(
                x_sample, _pad_hist(state_conv_a[0], HIST_A_ROWS), _pad_hist(state_conv_b[0], HIST_B_ROWS),
                cnt, params, nb=dec_b, ts=dec_s)
            parts.append(dict(h=h.reshape(dec_b * dec_s, d), hnp=hnp, idx=idx, gate_t=gate, rank=rank))
        n_pairs = sum(p["hnp"].shape[0] for p in parts) * TOP_K
        plan, pad_start, n_blocks, has, reuse_tables = _block_plan(cnt[:, 0].astype(jnp.int32), n_pairs, n_exp)
        pos = _positions(pad_start, [(p["idx"], p["rank"]) for p in parts])
        xs = _sc_dispatch([(q, p["hnp"]) for q, p in zip(pos, parts)], n_blocks * MOE_BLOCK)
        groups.append(dict(parts=parts, pos=pos, plan=plan, xs=xs, states=states, has=has,
                           reuse_tables=reuse_tables))

    weights = (w_gate_up[0], b_gate_up[0], w_down[0], b_down[0])
    pub = groups[-1]
    if len(groups) > 1:
        pub["rows"], wgu16, wd16 = _experts(pub["plan"], pub["xs"], *weights, publish=True)
        for grp in groups[:-1]:
            grp["rows"] = _experts(grp["plan"], grp["xs"], *weights,
                                   reuse=(wgu16, wd16) + grp["reuse_tables"](pub["has"]))
    else:
        pub["rows"] = _experts(pub["plan"], pub["xs"], *weights)

    gathered = []
    for g, grp in enumerate(groups):
        pos0 = grp["pos"][0]
        n_g = pos0.shape[1]
        want = 1 if (g == len(groups) - 1 and len(groups) > 1) else COMBINE_CHUNKS
        n_chunks = want if n_g % (want * SC_CHUNK * SC_WORKERS) == 0 else 1
        per = n_g // n_chunks
        for c in range(n_chunks):
            pos_parts = [pos0[:, c * per:(c + 1) * per]] + (grp["pos"][1:] if c == 0 else [])
            gathered.append((g, c * per, _sc_gather(pos_parts, grp["rows"])))

    y_p = None
    for g, tok0, got in sorted(gathered, key=lambda t: (t[0] != len(groups) - 1, t[0], t[1])):
        grp = groups[g]
        main = grp["parts"][0]
        if len(got) > 1:
            smp = grp["parts"][1]
            y_s = _combine(got[1], smp["gate_t"], smp["h"], gfin).reshape(dec_b, dec_s, d)
        y_p = _combine(got[0], main["gate_t"], main["h"], gfin, tok0=tok0,
                       y_prev=y_p, y_rows=n_p, y_tok0=tok_start[g] + tok0)
    y_p = y_p.reshape(bsz, seq, d)

    na_p = jnp.concatenate([grp["states"][0][0] for grp in groups], axis=0)
    nb_p = jnp.concatenate([grp["states"][0][1] for grp in groups], axis=0)
    return (y_p, y_s, na_p[None], nb_p[None], na_s[None], nb_s[None])


def _block_plan(counts, n_pairs, n_exp):
    padded = (counts + MOE_BLOCK - 1) // MOE_BLOCK * MOE_BLOCK
    pad_end = jnp.cumsum(padded)
    pad_start = pad_end - padded
    n_blocks = -(-n_pairs // MOE_BLOCK) + n_exp
    n_blocks = -(-n_blocks // BLOCKS_PER_STEP) * BLOCKS_PER_STEP
    blk_start = jnp.arange(n_blocks, dtype=jnp.int32) * MOE_BLOCK
    blk_exp = jnp.minimum(jnp.sum(blk_start[:, None] >= pad_end[None, :], axis=1),
                          n_exp - 1).astype(jnp.int32)
    n_active = (pad_end[-1:] // MOE_BLOCK).astype(jnp.int32)
    e_ids = jnp.arange(n_exp, dtype=jnp.int32)
    has = padded > 0
    slot_e = (jnp.cumsum(has.astype(jnp.int32)) - 1) % 2
    later = jnp.where(has, e_ids, n_exp)
    next_e = jnp.concatenate([lax.cummin(later, reverse=True)[1:], jnp.full((1,), n_exp, jnp.int32)])
    next_e = jnp.where(next_e >= n_exp, -1, next_e)
    blk_onehot = blk_exp[:, None] == e_ids[None, :]

    def per_block(table):
        return jnp.sum(jnp.where(blk_onehot, table[None, :], 0), axis=1).astype(jnp.int32)

    blk_first = ((blk_start == per_block(pad_start)) & (blk_start < pad_end[-1])).astype(jnp.int32)
    plan = (blk_exp, blk_first, per_block(slot_e), per_block(next_e), n_active)

    def reuse_tables(has_pub):
        pub = has_pub.astype(jnp.int32)
        pub_next = jnp.sum(jnp.where(next_e[:, None] == e_ids[None, :], pub[None, :], 0), axis=1)
        return per_block(pub), per_block(pub_next)

    return plan, pad_start.astype(jnp.int32), n_blocks, has, reuse_tables
```

```python
import functools

import jax
import jax.numpy as jnp
from jax import lax
from jax.experimental import pallas as pl
from jax.experimental.pallas import tpu as pltpu
from jax.experimental.pallas import tpu_sc as plsc

SC_CORES = 2
SC_SUBCORES = 16
SC_WORKERS = SC_CORES * SC_SUBCORES
SC_CHUNK = 128
CONV_A_W = 3
CONV_B_W = 31
TOP_K = 4
SWIGLU_LIMIT = 7.0
SWIGLU_ALPHA = 1.702
NORM_EPS = 1e-5
MOE_BLOCK = 256
BLOCKS_PER_STEP = 4
SUBLANES = 8
LANES = 128
CONV_ROW_CHUNK = 256
HIST_A_ROWS = 8
HIST_B_ROWS = 32
MIXER_ROWS = 1024
PROJ_SECTION = 256
COMBINE_TILE = 1024
POS_TILE = 4096
GROUP_SPLIT = (1, 3)
COMBINE_CHUNKS = 1
VMEM_LIMIT = 56 * 1024 * 1024


def _rms(x, g):
    return x * lax.rsqrt(jnp.mean(x * x, axis=-1, keepdims=True) + NORM_EPS) * g


def _zero_tile_from(v):
    bits = pltpu.bitcast(v[0:SUBLANES, 0:LANES], jnp.uint32)
    return pltpu.bitcast((bits >> 16) >> 16, jnp.float32)


def _pack_bf16_pairs(a):
    w = a.shape[-1] // 2
    lo = pltpu.bitcast(a[:, :w].astype(jnp.bfloat16).astype(jnp.float32), jnp.uint32)
    hi = pltpu.bitcast(a[:, w:].astype(jnp.bfloat16).astype(jnp.float32), jnp.uint32)
    return (lo >> 16) | (hi & jnp.uint32(0xFFFF0000))


def _unpack_bf16_pairs(wd):
    lo = pltpu.bitcast(wd << 16, jnp.float32)
    hi = pltpu.bitcast(wd & jnp.uint32(0xFFFF0000), jnp.float32)
    return jnp.concatenate([lo, hi], axis=-1)


def _conv_b_chunk(ub_buf, cbw_ref, n_i, r0, rc, c0):
    cs = slice(c0, c0 + LANES)
    lead = HIST_B_ROWS - (CONV_B_W - 1)
    acc = None
    for b in range(SUBLANES):
        rows = rc if b == 0 else rc + SUBLANES
        q = None
        for a in range((HIST_B_ROWS + SUBLANES) // SUBLANES):
            k = SUBLANES * a + b - lead
            if k < 0 or k >= CONV_B_W:
                continue
            start = r0 + SUBLANES * a
            term = cbw_ref[k:k + 1, cs] * ub_buf[n_i, start:start + rows, cs]
            q = term if q is None else q + term
        part = q if b == 0 else q[b:b + rc]
        acc = part if acc is None else acc + part
    return acc


def _mixer_kernel(x_ref, hista_ref, histb_ref, cnt0_ref, tri_ref,
                  gmix_ref, win_ref, caw_ref, cbw_ref, cbb_ref, lng_ref, lnb_ref,
                  ga_ref, gb_ref, wout_ref, gffn_ref, wrt_ref, br_ref,
                  h_ref, hnp_ref, idx_ref, gate_ref, rank_ref, cnt_ref, newa_ref, newb_ref,
                  ua_buf, ub_buf, cb_buf, cnt_acc, *, nb, ts, d_a, d_b, n_exp):
    b = pl.program_id(0)
    s = pl.program_id(1)
    m = nb * ts
    d = x_ref.shape[-1]

    @pl.when(s == 0)
    def _():
        ua_buf[:, 0:HIST_A_ROWS, :] = hista_ref[...]
        ub_buf[:, 0:HIST_B_ROWS, :] = histb_ref[...]

    @pl.when(s != 0)
    def _():
        ua_buf[:, 0:HIST_A_ROWS, :] = ua_buf[:, ts:ts + HIST_A_ROWS, :]
        ub_buf[:, 0:HIST_B_ROWS, :] = ub_buf[:, ts:ts + HIST_B_ROWS, :]

    @pl.when((b == 0) & (s == 0))
    def _():
        cnt_acc[...] = cnt0_ref[...]

    x = x_ref[...].reshape(m, d)
    n = _rms(x, gmix_ref[...]).astype(jnp.bfloat16)
    def proj(c0, width):
        return jnp.dot(n, win_ref[:, c0:c0 + width], preferred_element_type=jnp.float32)

    sec = min(d_b, PROJ_SECTION)
    for c0 in range(0, d_b, sec):
        v_b = proj(3 * d_a + c0, sec)
        g_b = proj(3 * d_a + d_b + c0, sec)
        ub_buf[:, HIST_B_ROWS:HIST_B_ROWS + ts, c0:c0 + sec] = (v_b * jax.nn.sigmoid(g_b)).reshape(nb, ts, sec)

    gate_c = proj(d_a, d_a)
    xt = proj(2 * d_a, d_a)
    gate_b = proj(0, d_a)
    ua_buf[:, HIST_A_ROWS:HIST_A_ROWS + ts, :] = (gate_c * xt).reshape(nb, ts, d_a)
    conv_a = jnp.zeros((nb, ts, d_a), jnp.float32)
    for k in range(CONV_A_W):
        off = HIST_A_ROWS - (CONV_A_W - 1) + k
        conv_a = conv_a + caw_ref[k:k + 1, :] * ua_buf[:, off:off + ts, :]
    y_a = gate_b * conv_a.reshape(m, d_a)

    mix_a = jnp.dot(_rms(y_a, ga_ref[...]).astype(jnp.bfloat16), wout_ref[0:d_a, :],
                    preferred_element_type=jnp.float32)

    rc = min(ts, CONV_ROW_CHUNK)
    chunks = [(n_i, r0, c0) for n_i in range(nb) for r0 in range(0, ts, rc) for c0 in range(0, d_b, LANES)]
    anchors = {len(chunks) * 3 // 8: gate_c, len(chunks) * 4 // 8: xt, len(chunks) * 6 // 8: gate_b,
               len(chunks) - 1: mix_a}
    for ci, (n_i, r0, c0) in enumerate(chunks):
        acc = _conv_b_chunk(ub_buf, cbw_ref, n_i, r0, rc, c0)
        if ci in anchors:
            acc = (acc.reshape(rc // SUBLANES, SUBLANES, LANES) + _zero_tile_from(anchors[ci])[None]
                   ).reshape(rc, LANES)
        cb_buf[n_i, r0:r0 + rc, c0:c0 + LANES] = acc
    cb = cb_buf[...].reshape(m, d_b) + cbb_ref[...]
    mu = jnp.mean(cb, axis=-1, keepdims=True)
    xc = cb - mu
    var = jnp.mean(xc * xc, axis=-1, keepdims=True)
    ln = xc * lax.rsqrt(var + NORM_EPS) * lng_ref[...] + lnb_ref[...]
    y_b = ln * jax.nn.sigmoid(ln)

    mix_b = jnp.dot(_rms(y_b, gb_ref[...]).astype(jnp.bfloat16), wout_ref[d_a:d_a + d_b, :],
                    preferred_element_type=jnp.float32)
    h = x + (mix_a + mix_b)
    h_ref[...] = h.reshape(nb, ts, d)

    hn = _rms(h, gffn_ref[...])
    hnp_ref[...] = _pack_bf16_pairs(hn)

    logits = lax.dot_general(wrt_ref[...], hn.astype(jnp.bfloat16), (((1,), (1,)), ((), ())),
                             preferred_element_type=jnp.float32) + br_ref[...]
    e_iota = lax.broadcasted_iota(jnp.int32, (n_exp, m), 0)
    cur = logits
    vals, idxs, sels = [], [], []
    for _ in range(TOP_K):
        mx = jnp.max(cur, axis=0, keepdims=True)
        ix = jnp.min(jnp.where(cur == mx, e_iota, n_exp), axis=0, keepdims=True)
        sel = e_iota == ix
        vals.append(mx)
        idxs.append(ix)
        sels.append(sel)
        cur = jnp.where(sel, -jnp.inf, cur)
    exps = [jnp.exp(v - vals[0]) for v in vals]
    denom = exps[0] + exps[1] + exps[2] + exps[3]
    gates = [e / denom for e in exps]

    onehot = jnp.zeros((n_exp, m), jnp.float32)
    for sel in sels:
        onehot = onehot + sel.astype(jnp.float32)
    before = jnp.dot(onehot.astype(jnp.bfloat16), tri_ref[...],
                     preferred_element_type=jnp.float32) + cnt_acc[...]
    ranks = [jnp.sum(jnp.where(sel, before, 0.0), axis=0, keepdims=True) for sel in sels]
    new_cnt = cnt_acc[...] + jnp.sum(onehot, axis=1, keepdims=True)
    cnt_acc[...] = new_cnt
    cnt_ref[...] = new_cnt

    idx_ref[...] = jnp.concatenate(idxs, axis=0)
    gate_ref[...] = jnp.concatenate(gates + [jnp.zeros((SUBLANES - TOP_K, m), jnp.float32)], axis=0).T
    rank_ref[...] = jnp.concatenate(ranks, axis=0).astype(jnp.int32)

    @pl.when(s == pl.num_programs(1) - 1)
    def _():
        newa_ref[...] = ua_buf[:, ts + HIST_A_ROWS - (CONV_A_W - 1):ts + HIST_A_ROWS, :]
        newb_ref[...] = ub_buf[:, ts + HIST_B_ROWS - (CONV_B_W - 1):ts + HIST_B_ROWS, :]


def _mixer(x, hist_a, hist_b, cnt0, params, *, nb, ts, b0=0, bsz=None):
    (gmix, win, caw, cbw, cbb, lng, lnb, ga, gb, wout, gffn, wrt, br) = params
    _, seq, d = x.shape
    bsz = x.shape[0] if bsz is None else bsz
    d_a, d_b = caw.shape[-1], cbw.shape[-1]
    n_exp = wrt.shape[0]
    m = nb * ts
    n_tok = bsz * seq
    g0 = b0 // nb
    grid = (bsz // nb, seq // ts)
    tri = (jnp.arange(m)[:, None] < jnp.arange(m)[None, :]).astype(jnp.bfloat16)

    def const(shape):
        return pl.BlockSpec(shape, lambda i, j: (0,) * len(shape))

    tok_map = lambda i, j: (0, i * (seq // ts) + j)
    kern = functools.partial(_mixer_kernel, nb=nb, ts=ts, d_a=d_a, d_b=d_b, n_exp=n_exp)
    out_shape = (
        jax.ShapeDtypeStruct((bsz, seq, d), jnp.float32),
        jax.ShapeDtypeStruct((n_tok, d // 2), jnp.uint32),
        jax.ShapeDtypeStruct((TOP_K, n_tok), jnp.int32),
        jax.ShapeDtypeStruct((n_tok, SUBLANES), jnp.float32),
        jax.ShapeDtypeStruct((TOP_K, n_tok), jnp.int32),
        jax.ShapeDtypeStruct((n_exp, 1), jnp.float32),
        jax.ShapeDtypeStruct((bsz, CONV_A_W - 1, d_a), jnp.float32),
        jax.ShapeDtypeStruct((bsz, CONV_B_W - 1, d_b), jnp.float32),
    )
    return pl.pallas_call(
        kern,
        grid=grid,
        in_specs=[
            pl.BlockSpec((nb, ts, d), lambda i, j: (g0 + i, j, 0)),
            pl.BlockSpec((nb, HIST_A_ROWS, d_a), lambda i, j: (g0 + i, 0, 0)),
            pl.BlockSpec((nb, HIST_B_ROWS, d_b), lambda i, j: (g0 + i, 0, 0)),
            const((n_exp, 1)),
            const((m, m)),
            const((1, d)), const(win.shape), const(caw.shape), const(cbw.shape), const((1, d_b)),
            const((1, d_b)), const((1, d_b)), const((1, d_a)), const((1, d_b)), const(wout.shape),
            const((1, d)), const(wrt.shape), const((n_exp, 1)),
        ],
        out_specs=(
            pl.BlockSpec((nb, ts, d), lambda i, j: (i, j, 0)),
            pl.BlockSpec((m, d // 2), lambda i, j: (i * (seq // ts) + j, 0)),
            pl.BlockSpec((TOP_K, m), tok_map),
            pl.BlockSpec((m, SUBLANES), lambda i, j: (i * (seq // ts) + j, 0)),
            pl.BlockSpec((TOP_K, m), tok_map),
            pl.BlockSpec((n_exp, 1), lambda i, j: (0, 0)),
            pl.BlockSpec((nb, CONV_A_W - 1, d_a), lambda i, j: (i, 0, 0)),
            pl.BlockSpec((nb, CONV_B_W - 1, d_b), lambda i, j: (i, 0, 0)),
        ),
        out_shape=out_shape,
        scratch_shapes=[
            pltpu.VMEM((nb, HIST_A_ROWS + ts, d_a), jnp.float32),
            pltpu.VMEM((nb, HIST_B_ROWS + ts, d_b), jnp.float32),
            pltpu.VMEM((nb, ts, d_b), jnp.float32),
            pltpu.VMEM((n_exp, 1), jnp.float32),
        ],
        compiler_params=pltpu.CompilerParams(
            dimension_semantics=("arbitrary", "arbitrary"), vmem_limit_bytes=VMEM_LIMIT),
        name="mixer",
    )(x, hist_a, hist_b, cnt0, tri, gmix, win, caw, cbw, cbb, lng, lnb, ga, gb, wout, gffn, wrt, br)


def _positions_kernel(start_ref, *refs):
    n_parts = len(refs) // 3
    for p in range(n_parts):
        idx = refs[2 * p][...]
        pos = refs[2 * p + 1][...]
        for e in range(start_ref.shape[0]):
            pos = pos + jnp.where(idx == e, start_ref[e], 0)
        refs[2 * n_parts + p][...] = pos


def _positions(pad_start, parts):
    n_0 = parts[0][0].shape[1]
    tile = min(n_0, POS_TILE)
    specs = [pl.BlockSpec((TOP_K, tile), lambda i: (0, i))]
    specs += [pl.BlockSpec((TOP_K, idx.shape[1]), lambda i: (0, 0)) for idx, _ in parts[1:]]
    in_specs = [pl.BlockSpec(memory_space=pltpu.SMEM)]
    args = [pad_start]
    for spec, (idx, rank) in zip(specs, parts):
        in_specs += [spec, spec]
        args += [idx, rank]
    out = pl.pallas_call(
        _positions_kernel,
        grid=(n_0 // tile,),
        in_specs=in_specs,
        out_specs=tuple(specs),
        out_shape=tuple(jax.ShapeDtypeStruct(idx.shape, jnp.int32) for idx, _ in parts),
        compiler_params=pltpu.CompilerParams(dimension_semantics=("arbitrary",)),
        name="positions",
    )(*args)
    return list(out)


def _sc_mesh():
    return plsc.VectorSubcoreMesh(core_axis_name="c", subcore_axis_name="s",
                                  num_cores=SC_CORES, num_subcores=SC_SUBCORES)


def _sc_worker_id():
    return lax.axis_index("s") * SC_CORES + lax.axis_index("c")


def _sc_chunks(n_tok, wid):
    n_chunks = n_tok // SC_CHUNK
    assert n_tok % SC_CHUNK == 0
    if n_chunks % SC_WORKERS == 0:
        per_w = n_chunks // SC_WORKERS
        return None, per_w, lambda i: (wid * per_w + i) * SC_CHUNK
    assert n_chunks <= SC_WORKERS
    return wid < n_chunks, 1, lambda i: wid * SC_CHUNK


def _sc_for_each_chunk(n_tok, fn):
    wid = _sc_worker_id()
    pred, trips, base = _sc_chunks(n_tok, wid)

    def run():
        @pl.loop(0, trips)
        def _(i):
            fn(pl.multiple_of(base(i), SC_CHUNK))

    if pred is None:
        run()
    else:
        pl.when(pred)(run)


def _sc_dispatch(parts, n_rows):
    half = parts[0][1].shape[1]
    n_parts = len(parts)

    @functools.partial(
        pl.kernel, mesh=_sc_mesh(),
        out_type=jax.ShapeDtypeStruct((n_rows, half), jnp.uint32),
        scratch_types=[pltpu.VMEM((TOP_K, SC_CHUNK), jnp.int32),
                       pltpu.VMEM((SC_CHUNK, half), jnp.uint32),
                       pltpu.SemaphoreType.DMA],
        compiler_params=pltpu.CompilerParams(use_tc_tiling_on_sc=True),
        name="sc_dispatch")
    def k(*refs):
        xs_hbm, idx_v, rows_v, sem = refs[2 * n_parts:]

        def move(pos_hbm, src_hbm):
            def fn(base):
                pltpu.sync_copy(pos_hbm.at[:, pl.ds(base, SC_CHUNK)], idx_v)
                pltpu.sync_copy(src_hbm.at[pl.ds(base, SC_CHUNK)], rows_v)
                copies = [pltpu.async_copy(rows_v, xs_hbm.at[idx_v.at[j]], sem) for j in range(TOP_K)]
                for c in copies:
                    c.wait()
            return fn

        for p, (_, hnp) in enumerate(parts):
            _sc_for_each_chunk(hnp.shape[0], move(refs[2 * p], refs[2 * p + 1]))

    return k(*[a for part in parts for a in part])


def _sc_gather(pos_parts, rows):
    half = rows.shape[1]
    n_parts = len(pos_parts)

    @functools.partial(
        pl.kernel, mesh=_sc_mesh(),
        out_type=tuple(jax.ShapeDtypeStruct((TOP_K, p.shape[1], half), jnp.uint32) for p in pos_parts),
        scratch_types=[pltpu.VMEM((TOP_K, SC_CHUNK), jnp.int32),
                       pltpu.VMEM((SC_CHUNK, half), jnp.uint32),
                       pltpu.SemaphoreType.DMA],
        compiler_params=pltpu.CompilerParams(use_tc_tiling_on_sc=True),
        name="sc_gather")
    def k(*refs):
        pos_hbms, rows_hbm = refs[:n_parts], refs[n_parts]
        out_hbms = refs[n_parts + 1:2 * n_parts + 1]
        idx_v, buf_v, sem = refs[2 * n_parts + 1:]

        def move(pos_hbm, out_hbm):
            def fn(base):
                pltpu.sync_copy(pos_hbm.at[:, pl.ds(base, SC_CHUNK)], idx_v)
                for j in range(TOP_K):
                    pltpu.async_copy(rows_hbm.at[idx_v.at[j]], buf_v, sem).wait()
                    pltpu.sync_copy(buf_v, out_hbm.at[j, pl.ds(base, SC_CHUNK)])
            return fn

        for pos_hbm, out_hbm, p in zip(pos_hbms, out_hbms, pos_parts):
            _sc_for_each_chunk(p.shape[1], move(pos_hbm, out_hbm))

    out = k(*pos_parts, rows)
    return out if isinstance(out, (tuple, list)) else (out,)


def _experts_kernel(*refs, publish, reuse):
    blk_exp_ref, first_ref, slot_ref, next_ref, n_active_ref = refs[:5]
    refs = refs[5:]
    if reuse:
        have_ref, have_next_ref = refs[:2]
        refs = refs[2:]
    xs_ref, wgu_hbm, bgu_ref, wd_hbm, bd_ref = refs[:5]
    refs = refs[5:]
    if reuse:
        wgu16_hbm, wd16_hbm = refs[:2]
        refs = refs[2:]
    out_ref = refs[0]
    refs = refs[1:]
    if publish:
        wgu16_out, wd16_out = refs[:2]
        refs = refs[2:]
    wgu_f32, wd_f32, wgu_bf, wd_bf, sems = refs[:5]
    refs = refs[5:]
    if publish:
        out_sems, pending = refs
    if reuse:
        sems16, = refs
    d_ff = wd_bf.shape[-2]

    def bf_slot(parity):
        return parity if reuse else 0

    def f32_slot(parity):
        return 0 if reuse else parity

    def fetch(e, parity):
        sl = f32_slot(parity)
        return (pltpu.make_async_copy(wgu_hbm.at[e], wgu_f32.at[sl], sems.at[0, sl]),
                pltpu.make_async_copy(wd_hbm.at[e], wd_f32.at[sl], sems.at[1, sl]))

    def fetch16(e, parity):
        return (pltpu.make_async_copy(wgu16_hbm.at[e], wgu_bf.at[parity], sems16.at[0, parity]),
                pltpu.make_async_copy(wd16_hbm.at[e], wd_bf.at[parity], sems16.at[1, parity]))

    def publish_copies(e):
        return (pltpu.make_async_copy(wgu_bf.at[0], wgu16_out.at[e], out_sems.at[0]),
                pltpu.make_async_copy(wd_bf.at[0], wd16_out.at[e], out_sems.at[1]))

    def wait_published():
        @pl.when(pending[0] == 1)
        def _():
            for c in publish_copies(0):
                c.wait()
            pending[0] = 0

    def start_fetch(e, parity, have):
        if reuse:
            @pl.when(have == 1)
            def _():
                for c in fetch16(e, parity):
                    c.start()

            @pl.when(have != 1)
            def _():
                for c in fetch(e, parity):
                    c.start()
        else:
            for c in fetch(e, parity):
                c.start()

    def finish_fetch(e, parity, have):
        def from_f32():
            for c in fetch(e, parity):
                c.wait()
            if publish:
                wait_published()
            wgu_bf[bf_slot(parity)] = wgu_f32[f32_slot(parity)].astype(jnp.bfloat16)
            wd_bf[bf_slot(parity)] = wd_f32[f32_slot(parity)].astype(jnp.bfloat16)
            if publish:
                for c in publish_copies(e):
                    c.start()
                pending[0] = 1

        if reuse:
            @pl.when(have == 1)
            def _():
                for c in fetch16(e, parity):
                    c.wait()

            pl.when(have != 1)(from_f32)
        else:
            from_f32()

    def load_weights_if_first(i):
        parity = slot_ref[i]
        e = blk_exp_ref[i]
        have = have_ref[i] if reuse else 0

        @pl.when(first_ref[i] == 1)
        def _():
            @pl.when(i == 0)
            def _():
                if publish:
                    pending[0] = 0
                start_fetch(e, parity, have)

            finish_fetch(e, parity, have)

            @pl.when(next_ref[i] >= 0)
            def _():
                start_fetch(next_ref[i], 1 - parity, have_next_ref[i] if reuse else 0)

    def compute(rows, i):
        e = blk_exp_ref[i]
        sl = bf_slot(slot_ref[i])
        x = _unpack_bf16_pairs(xs_ref[rows, :]).astype(jnp.bfloat16)
        gu = jnp.dot(x, wgu_bf[sl], preferred_element_type=jnp.float32) + bgu_ref[e]
        g = jnp.minimum(gu[:, :d_ff], SWIGLU_LIMIT)
        u = jnp.clip(gu[:, d_ff:], -SWIGLU_LIMIT, SWIGLU_LIMIT)
        act = g * jax.nn.sigmoid(SWIGLU_ALPHA * g) * (u + 1.0)
        o = jnp.dot(act.astype(jnp.bfloat16), wd_bf[sl], preferred_element_type=jnp.float32) + bd_ref[e]
        out_ref[rows, :] = _pack_bf16_pairs(o)

    i0 = pl.program_id(0) * BLOCKS_PER_STEP
    n_active = n_active_ref[0]

    def run(sub0, n_sub):
        first = i0 + sub0
        rows = slice(sub0 * MOE_BLOCK, (sub0 + n_sub) * MOE_BLOCK)
        if n_sub == 1:
            @pl.when(first < n_active)
            def _():
                load_weights_if_first(first)
                compute(rows, first)

            @pl.when(first >= n_active)
            def _():
                out_ref[rows, :] = jnp.zeros((MOE_BLOCK, out_ref.shape[1]), out_ref.dtype)
            return

        same = first + n_sub - 1 < n_active
        for sub in range(1, n_sub):
            same = same & (blk_exp_ref[first + sub] == blk_exp_ref[first])

        @pl.when(same)
        def _():
            load_weights_if_first(first)
            compute(rows, first)

        @pl.when(jnp.logical_not(same))
        def _():
            run(sub0, n_sub // 2)
            run(sub0 + n_sub // 2, n_sub // 2)

    run(0, BLOCKS_PER_STEP)

    if publish:
        @pl.when(pl.program_id(0) == pl.num_programs(0) - 1)
        def _():
            wait_published()


def _experts(plan, xs, wgu, bgu, wd, bd, *, publish=False, reuse=None):
    blk_exp, first, slot, nxt, n_active = plan
    n_rows, half = xs.shape
    n_exp, d, two_f = wgu.shape
    d_ff = wd.shape[1]
    n_blocks = n_rows // MOE_BLOCK
    step_rows = MOE_BLOCK * BLOCKS_PER_STEP
    assert n_rows % step_rows == 0
    any_spec = pl.BlockSpec(memory_space=pl.ANY)
    scalars = [blk_exp, first, slot, nxt, n_active]
    operands = [xs, wgu, bgu.reshape(n_exp, 1, two_f), wd, bd.reshape(n_exp, 1, d)]
    in_specs = [
        pl.BlockSpec((step_rows, half), lambda i, *_: (i, 0)),
        any_spec,
        pl.BlockSpec((n_exp, 1, two_f), lambda i, *_: (0, 0, 0)),
        any_spec,
        pl.BlockSpec((n_exp, 1, d), lambda i, *_: (0, 0, 0)),
    ]
    out_specs = [pl.BlockSpec((step_rows, half), lambda i, *_: (i, 0))]
    out_shape = [jax.ShapeDtypeStruct((n_rows, half), jnp.uint32)]
    n_f32_slots, n_bf_slots = (1, 2) if reuse else (2, 1)
    scratch = [
        pltpu.VMEM((n_f32_slots, d, two_f), jnp.float32),
        pltpu.VMEM((n_f32_slots, d_ff, d), jnp.float32),
        pltpu.VMEM((n_bf_slots, d, two_f), jnp.bfloat16),
        pltpu.VMEM((n_bf_slots, d_ff, d), jnp.bfloat16),
        pltpu.SemaphoreType.DMA((2, 2)),
    ]
    if reuse:
        wgu16, wd16, have, have_next = reuse
        scalars += [have, have_next]
        operands += [wgu16, wd16]
        in_specs += [any_spec, any_spec]
        scratch += [pltpu.SemaphoreType.DMA((2, 2))]
    if publish:
        out_specs += [any_spec, any_spec]
        out_shape += [jax.ShapeDtypeStruct((n_exp, d, two_f), jnp.bfloat16),
                      jax.ShapeDtypeStruct((n_exp, d_ff, d), jnp.bfloat16)]
        scratch += [pltpu.SemaphoreType.DMA((2,)), pltpu.SMEM((1,), jnp.int32)]
    grid_spec = pltpu.PrefetchScalarGridSpec(
        num_scalar_prefetch=len(scalars),
        grid=(n_blocks // BLOCKS_PER_STEP,),
        in_specs=in_specs,
        out_specs=tuple(out_specs),
        scratch_shapes=scratch,
    )
    out = pl.pallas_call(
        functools.partial(_experts_kernel, publish=publish, reuse=bool(reuse)),
        grid_spec=grid_spec,
        out_shape=tuple(out_shape),
        compiler_params=pltpu.CompilerParams(
            dimension_semantics=("arbitrary",), vmem_limit_bytes=VMEM_LIMIT),
        name="experts",
    )(*scalars, *operands)
    return out if publish else out[0]


def _combine_kernel(rows_ref, gate_ref, h_ref, gfin_ref, *rest):
    y_ref = rest[-1]
    acc = h_ref[...]
    for k in range(TOP_K):
        acc = acc + gate_ref[:, k:k + 1] * _unpack_bf16_pairs(rows_ref[k])
    y_ref[...] = _rms(acc, gfin_ref[...])


def _combine(rows4, gate_t, h2d, gfin, *, tok0=0, y_prev=None, y_rows=None, y_tok0=0):
    d = h2d.shape[1]
    n, half = rows4.shape[1:]
    n_tok = h2d.shape[0] if y_rows is None else y_rows
    tile = min(n, COMBINE_TILE)
    t0 = tok0 // tile
    y0 = y_tok0 // tile
    in_specs = [
        pl.BlockSpec((TOP_K, tile, half), lambda i: (0, i, 0)),
        pl.BlockSpec((tile, gate_t.shape[1]), lambda i: (t0 + i, 0)),
        pl.BlockSpec((tile, d), lambda i: (t0 + i, 0)),
        pl.BlockSpec((1, d), lambda i: (0, 0)),
    ]
    args = [rows4, gate_t, h2d, gfin]
    aliases = {}
    if y_prev is not None:
        in_specs.append(pl.BlockSpec(memory_space=pl.ANY))
        args.append(y_prev)
        aliases = {4: 0}
    return pl.pallas_call(
        _combine_kernel,
        grid=(n // tile,),
        in_specs=in_specs,
        out_specs=pl.BlockSpec((tile, d), lambda i: (y0 + i, 0)),
        out_shape=jax.ShapeDtypeStruct((n_tok, d), jnp.float32),
        input_output_aliases=aliases,
        compiler_params=pltpu.CompilerParams(
            dimension_semantics=("arbitrary",), vmem_limit_bytes=VMEM_LIMIT),
        name="combine",
    )(*args)


def _pad_hist(hist, rows):
    return jnp.pad(hist, ((0, 0), (rows - hist.shape[1], 0), (0, 0)))


def kernel(x_prompt, x_sample, state_conv_a, state_conv_b, norm_mix_g, w_in, conv_a_w, conv_b_w, conv_b_b, conv_ln_g, conv_ln_b, out_norm_a_g, out_norm_b_g, w_out, norm_ffn_g, w_router, b_router, w_gate_up, b_gate_up, w_down, b_down, final_norm_g):
    depth = w_in.shape[0]
    assert depth == 1, "single-layer trunk"
    bf16 = jnp.bfloat16
    bsz, seq, d = x_prompt.shape
    dec_b, dec_s, _ = x_sample.shape
    d_a, d_b = conv_a_w.shape[-1], conv_b_w.shape[-1]
    n_exp = w_router.shape[-1]

    params = (norm_mix_g[0][None], w_in[0].astype(bf16), conv_a_w[0], conv_b_w[0], conv_b_b[0][None],
              conv_ln_g[0][None], conv_ln_b[0][None], out_norm_a_g[0][None], out_norm_b_g[0][None],
              w_out[0].astype(bf16), norm_ffn_g[0][None], w_router[0].T.astype(bf16),
              b_router[0][:, None])

    zero_a = jnp.zeros((bsz, HIST_A_ROWS, d_a), jnp.float32)
    zero_b = jnp.zeros((bsz, HIST_B_ROWS, d_b), jnp.float32)
    cnt0 = jnp.zeros((n_exp, 1), jnp.float32)
    gfin = final_norm_g[None]
    ts = min(seq, MIXER_ROWS)
    n_p = bsz * seq

    sizes = [bsz * f // sum(GROUP_SPLIT) for f in GROUP_SPLIT]
    if bsz % sum(GROUP_SPLIT) != 0:
        sizes = [bsz]
    groups = []
    tok_start = []
    for g, per_b in enumerate(sizes):
        b0 = sum(sizes[:g])
        tok_start.append(b0 * seq)
        h, hnp, idx, gate, rank, cnt, na, nb_ = _mixer(
            x_prompt, zero_a, zero_b, cnt0, params, nb=1, ts=ts, b0=b0, bsz=per_b)
        parts = [dict(h=h.reshape(per_b * seq, d), hnp=hnp, idx=idx, gate_t=gate, rank=rank)]
        states = [(na, nb_)]
        if g == len(sizes) - 1:
            h, hnp, idx, gate, rank, cnt, na_s, nb_s = _mixer(
                x_sample, _pad_hist(state_conv_a[0], HIST_A_ROWS), _pad_hist(state_conv_b[0], HIST_B_ROWS),
                cnt, params, nb=dec_b, ts=dec_s)
            parts.append(dict(h=h.reshape(dec_b * dec_s, d), hnp=hnp, idx=idx, gate_t=gate, rank=rank))
        n_pairs = sum(p["hnp"].shape[0] for p in parts) * TOP_K
        plan, pad_start, n_blocks, has, reuse_tables = _block_plan(cnt[:, 0].astype(jnp.int32), n_pairs, n_exp)
        pos = _positions(pad_start, [(p["idx"], p["rank"]) for p in parts])
        xs = _sc_dispatch([(q, p["hnp"]) for q, p in zip(pos, parts)], n_blocks * MOE_BLOCK)
        groups.append(dict(parts=parts, pos=pos, plan=plan, xs=xs, states=states, has=has,
                           reuse_tables=reuse_tables))

    weights = (w_gate_up[0], b_gate_up[0], w_down[0], b_down[0])
    pub = groups[-1]
    if len(groups) > 1:
        pub["rows"], wgu16, wd16 = _experts(pub["plan"], pub["xs"], *weights, publish=True)
        for grp in groups[:-1]:
            grp["rows"] = _experts(grp["plan"], grp["xs"], *weights,
                                   reuse=(wgu16, wd16) + grp["reuse_tables"](pub["has"]))
    else:
        pub["rows"] = _experts(pub["plan"], pub["xs"], *weights)

    gathered = []
    for g, grp in enumerate(groups):
        pos0 = grp["pos"][0]
        n_g = pos0.shape[1]
        want = 1 if (g == len(groups) - 1 and len(groups) > 1) else COMBINE_CHUNKS
        n_chunks = want if n_g % (want * SC_CHUNK * SC_WORKERS) == 0 else 1
        per = n_g // n_chunks
        for c in range(n_chunks):
            pos_parts = [pos0[:, c * per:(c + 1) * per]] + (grp["pos"][1:] if c == 0 else [])
            gathered.append((g, c * per, _sc_gather(pos_parts, grp["rows"])))

    y_p = None
    for g, tok0, got in sorted(gathered, key=lambda t: (t[0] != len(groups) - 1, t[0], t[1])):
        grp = groups[g]
        main = grp["parts"][0]
        if len(got) > 1:
            smp = grp["parts"][1]
            y_s = _combine(got[1], smp["gate_t"], smp["h"], gfin).reshape(dec_b, dec_s, d)
        y_p = _combine(got[0], main["gate_t"], main["h"], gfin, tok0=tok0,
                       y_prev=y_p, y_rows=n_p, y_tok0=tok_start[g] + tok0)
    y_p = y_p.reshape(bsz, seq, d)

    na_p = jnp.concatenate([grp["states"][0][0] for grp in groups], axis=0)
    nb_p = jnp.concatenate([grp["states"][0][1] for grp in groups], axis=0)
    return (y_p, y_s, na_p[None], nb_p[None], na_s[None], nb_s[None])


def _block_plan(counts, n_pairs, n_exp):
    padded = (counts + MOE_BLOCK - 1) // MOE_BLOCK * MOE_BLOCK
    pad_end = jnp.cumsum(padded)
    pad_start = pad_end - padded
    n_blocks = -(-n_pairs // MOE_BLOCK) + n_exp
    n_blocks = -(-n_blocks // BLOCKS_PER_STEP) * BLOCKS_PER_STEP
    blk_start = jnp.arange(n_blocks, dtype=jnp.int32) * MOE_BLOCK
    blk_exp = jnp.minimum(jnp.sum(blk_start[:, None] >= pad_end[None, :], axis=1),
                          n_exp - 1).astype(jnp.int32)
    n_active = (pad_end[-1:] // MOE_BLOCK).astype(jnp.int32)
    e_ids = jnp.arange(n_exp, dtype=jnp.int32)
    has = padded > 0
    slot_e = (jnp.cumsum(has.astype(jnp.int32)) - 1) % 2
    later = jnp.where(has, e_ids, n_exp)
    next_e = jnp.concatenate([lax.cummin(later, reverse=True)[1:], jnp.full((1,), n_exp, jnp.int32)])
    next_e = jnp.where(next_e >= n_exp, -1, next_e)
    blk_onehot = blk_exp[:, None] == e_ids[None, :]

    def per_block(table):
        return jnp.sum(jnp.where(blk_onehot, table[None, :], 0), axis=1).astype(jnp.int32)

    blk_first = ((blk_start == per_block(pad_start)) & (blk_start < pad_end[-1])).astype(jnp.int32)
    plan = (blk_exp, blk_first, per_block(slot_e), per_block(next_e), n_active)

    def reuse_tables(has_pub):
        pub = has_pub.astype(jnp.int32)
        pub_next = jnp.sum(jnp.where(next_e[:, None] == e_ids[None, :], pub[None, :], 0), axis=1)
        return per_block(pub), per_block(pub_next)

    return plan, pad_start.astype(jnp.int32), n_blocks, has, reuse_tables
```

```python
import functools

import jax
import jax.numpy as jnp
from jax import lax
from jax.experimental import pallas as pl
from jax.experimental.pallas import tpu as pltpu
from jax.experimental.pallas import tpu_sc as plsc

SC_CORES = 2
SC_SUBCORES = 16
SC_WORKERS = SC_CORES * SC_SUBCORES
SC_CHUNK = 128
CONV_A_W = 3
CONV_B_W = 31
TOP_K = 4
SWIGLU_LIMIT = 7.0
SWIGLU_ALPHA = 1.702
NORM_EPS = 1e-5
MOE_BLOCK = 256
BLOCKS_PER_STEP = 4
SUBLANES = 8
LANES = 128
CONV_ROW_CHUNK = 128
HIST_A_ROWS = 8
HIST_B_ROWS = 32
MIXER_ROWS = 1024
PROJ_SECTION = 256
COMBINE_TILE = 1024
POS_TILE = 4096
GROUP_SPLIT = (1, 3)
COMBINE_CHUNKS = 1
VMEM_LIMIT = 56 * 1024 * 1024


def _rms(x, g):
    return x * lax.rsqrt(jnp.mean(x * x, axis=-1, keepdims=True) + NORM_EPS) * g


def _zero_tile_from(v):
    bits = pltpu.bitcast(v[0:SUBLANES, 0:LANES], jnp.uint32)
    return pltpu.bitcast((bits >> 16) >> 16, jnp.float32)


def _pack_bf16_pairs(a):
    w = a.shape[-1] // 2
    lo = pltpu.bitcast(a[:, :w].astype(jnp.bfloat16).astype(jnp.float32), jnp.uint32)
    hi = pltpu.bitcast(a[:, w:].astype(jnp.bfloat16).astype(jnp.float32), jnp.uint32)
    return (lo >> 16) | (hi & jnp.uint32(0xFFFF0000))


def _unpack_bf16_pairs(wd):
    lo = pltpu.bitcast(wd << 16, jnp.float32)
    hi = pltpu.bitcast(wd & jnp.uint32(0xFFFF0000), jnp.float32)
    return jnp.concatenate([lo, hi], axis=-1)


def _conv_b_chunk(ub_buf, cbw_ref, n_i, r0, rc, c0):
    cs = slice(c0, c0 + LANES)
    lead = HIST_B_ROWS - (CONV_B_W - 1)
    acc = None
    for b in range(SUBLANES):
        rows = rc if b == 0 else rc + SUBLANES
        q = None
        for a in range((HIST_B_ROWS + SUBLANES) // SUBLANES):
            k = SUBLANES * a + b - lead
            if k < 0 or k >= CONV_B_W:
                continue
            start = r0 + SUBLANES * a
            term = cbw_ref[k:k + 1, cs] * ub_buf[n_i, start:start + rows, cs]
            q = term if q is None else q + term
        part = q if b == 0 else q[b:b + rc]
        acc = part if acc is None else acc + part
    return acc


def _mixer_kernel(x_ref, hista_ref, histb_ref, cnt0_ref, tri_ref,
                  gmix_ref, win_ref, caw_ref, cbw_ref, cbb_ref, lng_ref, lnb_ref,
                  ga_ref, gb_ref, wout_ref, gffn_ref, wrt_ref, br_ref,
                  h_ref, hnp_ref, idx_ref, gate_ref, rank_ref, cnt_ref, newa_ref, newb_ref,
                  ua_buf, ub_buf, cb_buf, cnt_acc, *, nb, ts, d_a, d_b, n_exp):
    b = pl.program_id(0)
    s = pl.program_id(1)
    m = nb * ts
    d = x_ref.shape[-1]

    @pl.when(s == 0)
    def _():
        ua_buf[:, 0:HIST_A_ROWS, :] = hista_ref[...]
        ub_buf[:, 0:HIST_B_ROWS, :] = histb_ref[...]

    @pl.when(s != 0)
    def _():
        ua_buf[:, 0:HIST_A_ROWS, :] = ua_buf[:, ts:ts + HIST_A_ROWS, :]
        ub_buf[:, 0:HIST_B_ROWS, :] = ub_buf[:, ts:ts + HIST_B_ROWS, :]

    @pl.when((b == 0) & (s == 0))
    def _():
        cnt_acc[...] = cnt0_ref[...]

    x = x_ref[...].reshape(m, d)
    n = _rms(x, gmix_ref[...]).astype(jnp.bfloat16)
    def proj(c0, width):
        return jnp.dot(n, win_ref[:, c0:c0 + width], preferred_element_type=jnp.float32)

    sec = min(d_b, PROJ_SECTION)
    for c0 in range(0, d_b, sec):
        v_b = proj(3 * d_a + c0, sec)
        g_b = proj(3 * d_a + d_b + c0, sec)
        ub_buf[:, HIST_B_ROWS:HIST_B_ROWS + ts, c0:c0 + sec] = (v_b * jax.nn.sigmoid(g_b)).reshape(nb, ts, sec)

    gate_c = proj(d_a, d_a)
    xt = proj(2 * d_a, d_a)
    gate_b = proj(0, d_a)
    ua_buf[:, HIST_A_ROWS:HIST_A_ROWS + ts, :] = (gate_c * xt).reshape(nb, ts, d_a)
    conv_a = jnp.zeros((nb, ts, d_a), jnp.float32)
    for k in range(CONV_A_W):
        off = HIST_A_ROWS - (CONV_A_W - 1) + k
        conv_a = conv_a + caw_ref[k:k + 1, :] * ua_buf[:, off:off + ts, :]
    y_a = gate_b * conv_a.reshape(m, d_a)

    mix_a = jnp.dot(_rms(y_a, ga_ref[...]).astype(jnp.bfloat16), wout_ref[0:d_a, :],
                    preferred_element_type=jnp.float32)

    rc = min(ts, CONV_ROW_CHUNK)
    chunks = [(n_i, r0, c0) for n_i in range(nb) for r0 in range(0, ts, rc) for c0 in range(0, d_b, LANES)]
    anchors = {len(chunks) * 3 // 8: gate_c, len(chunks) * 4 // 8: xt, len(chunks) * 6 // 8: gate_b,
               len(chunks) * 7 // 8: mix_a}
    for ci, (n_i, r0, c0) in enumerate(chunks):
        acc = _conv_b_chunk(ub_buf, cbw_ref, n_i, r0, rc, c0)
        if ci in anchors:
            acc = (acc.reshape(rc // SUBLANES, SUBLANES, LANES) + _zero_tile_from(anchors[ci])[None]
                   ).reshape(rc, LANES)
        cb_buf[n_i, r0:r0 + rc, c0:c0 + LANES] = acc
    cb = cb_buf[...].reshape(m, d_b) + cbb_ref[...]
    mu = jnp.mean(cb, axis=-1, keepdims=True)
    xc = cb - mu
    var = jnp.mean(xc * xc, axis=-1, keepdims=True)
    ln = xc * lax.rsqrt(var + NORM_EPS) * lng_ref[...] + lnb_ref[...]
    y_b = ln * jax.nn.sigmoid(ln)

    mix_b = jnp.dot(_rms(y_b, gb_ref[...]).astype(jnp.bfloat16), wout_ref[d_a:d_a + d_b, :],
                    preferred_element_type=jnp.float32)
    h = x + (mix_a + mix_b)
    h_ref[...] = h.reshape(nb, ts, d)

    hn = _rms(h, gffn_ref[...])
    hnp_ref[...] = _pack_bf16_pairs(hn)

    logits = lax.dot_general(wrt_ref[...], hn.astype(jnp.bfloat16), (((1,), (1,)), ((), ())),
                             preferred_element_type=jnp.float32) + br_ref[...]
    e_iota = lax.broadcasted_iota(jnp.int32, (n_exp, m), 0)
    cur = logits
    vals, idxs, sels = [], [], []
    for _ in range(TOP_K):
        mx = jnp.max(cur, axis=0, keepdims=True)
        ix = jnp.min(jnp.where(cur == mx, e_iota, n_exp), axis=0, keepdims=True)
        sel = e_iota == ix
        vals.append(mx)
        idxs.append(ix)
        sels.append(sel)
        cur = jnp.where(sel, -jnp.inf, cur)
    exps = [jnp.exp(v - vals[0]) for v in vals]
    denom = exps[0] + exps[1] + exps[2] + exps[3]
    gates = [e / denom for e in exps]

    onehot = jnp.zeros((n_exp, m), jnp.float32)
    for sel in sels:
        onehot = onehot + sel.astype(jnp.float32)
    before = jnp.dot(onehot.astype(jnp.bfloat16), tri_ref[...],
                     preferred_element_type=jnp.float32) + cnt_acc[...]
    ranks = [jnp.sum(jnp.where(sel, before, 0.0), axis=0, keepdims=True) for sel in sels]
    new_cnt = cnt_acc[...] + jnp.sum(onehot, axis=1, keepdims=True)
    cnt_acc[...] = new_cnt
    cnt_ref[...] = new_cnt

    idx_ref[...] = jnp.concatenate(idxs, axis=0)
    gate_ref[...] = jnp.concatenate(gates + [jnp.zeros((SUBLANES - TOP_K, m), jnp.float32)], axis=0).T
    rank_ref[...] = jnp.concatenate(ranks, axis=0).astype(jnp.int32)

    @pl.when(s == pl.num_programs(1) - 1)
    def _():
        newa_ref[...] = ua_buf[:, ts + HIST_A_ROWS - (CONV_A_W - 1):ts + HIST_A_ROWS, :]
        newb_ref[...] = ub_buf[:, ts + HIST_B_ROWS - (CONV_B_W - 1):ts + HIST_B_ROWS, :]


def _mixer(x, hist_a, hist_b, cnt0, params, *, nb, ts, b0=0, bsz=None):
    (gmix, win, caw, cbw, cbb, lng, lnb, ga, gb, wout, gffn, wrt, br) = params
    _, seq, d = x.shape
    bsz = x.shape[0] if bsz is None else bsz
    d_a, d_b = caw.shape[-1], cbw.shape[-1]
    n_exp = wrt.shape[0]
    m = nb * ts
    n_tok = bsz * seq
    g0 = b0 // nb
    grid = (bsz // nb, seq // ts)
    tri = (jnp.arange(m)[:, None] < jnp.arange(m)[None, :]).astype(jnp.bfloat16)

    def const(shape):
        return pl.BlockSpec(shape, lambda i, j: (0,) * len(shape))

    tok_map = lambda i, j: (0, i * (seq // ts) + j)
    kern = functools.partial(_mixer_kernel, nb=nb, ts=ts, d_a=d_a, d_b=d_b, n_exp=n_exp)
    out_shape = (
        jax.ShapeDtypeStruct((bsz, seq, d), jnp.float32),
        jax.ShapeDtypeStruct((n_tok, d // 2), jnp.uint32),
        jax.ShapeDtypeStruct((TOP_K, n_tok), jnp.int32),
        jax.ShapeDtypeStruct((n_tok, SUBLANES), jnp.float32),
        jax.ShapeDtypeStruct((TOP_K, n_tok), jnp.int32),
        jax.ShapeDtypeStruct((n_exp, 1), jnp.float32),
        jax.ShapeDtypeStruct((bsz, CONV_A_W - 1, d_a), jnp.float32),
        jax.ShapeDtypeStruct((bsz, CONV_B_W - 1, d_b), jnp.float32),
    )
    return pl.pallas_call(
        kern,
        grid=grid,
        in_specs=[
            pl.BlockSpec((nb, ts, d), lambda i, j: (g0 + i, j, 0)),
            pl.BlockSpec((nb, HIST_A_ROWS, d_a), lambda i, j: (g0 + i, 0, 0)),
            pl.BlockSpec((nb, HIST_B_ROWS, d_b), lambda i, j: (g0 + i, 0, 0)),
            const((n_exp, 1)),
            const((m, m)),
            const((1, d)), const(win.shape), const(caw.shape), const(cbw.shape), const((1, d_b)),
            const((1, d_b)), const((1, d_b)), const((1, d_a)), const((1, d_b)), const(wout.shape),
            const((1, d)), const(wrt.shape), const((n_exp, 1)),
        ],
        out_specs=(
            pl.BlockSpec((nb, ts, d), lambda i, j: (i, j, 0)),
            pl.BlockSpec((m, d // 2), lambda i, j: (i * (seq // ts) + j, 0)),
            pl.BlockSpec((TOP_K, m), tok_map),
            pl.BlockSpec((m, SUBLANES), lambda i, j: (i * (seq // ts) + j, 0)),
            pl.BlockSpec((TOP_K, m), tok_map),
            pl.BlockSpec((n_exp, 1), lambda i, j: (0, 0)),
            pl.BlockSpec((nb, CONV_A_W - 1, d_a), lambda i, j: (i, 0, 0)),
            pl.BlockSpec((nb, CONV_B_W - 1, d_b), lambda i, j: (i, 0, 0)),
        ),
        out_shape=out_shape,
        scratch_shapes=[
            pltpu.VMEM((nb, HIST_A_ROWS + ts, d_a), jnp.float32),
            pltpu.VMEM((nb, HIST_B_ROWS + ts, d_b), jnp.float32),
            pltpu.VMEM((nb, ts, d_b), jnp.float32),
            pltpu.VMEM((n_exp, 1), jnp.float32),
        ],
        compiler_params=pltpu.CompilerParams(
            dimension_semantics=("arbitrary", "arbitrary"), vmem_limit_bytes=VMEM_LIMIT),
        name="mixer",
    )(x, hist_a, hist_b, cnt0, tri, gmix, win, caw, cbw, cbb, lng, lnb, ga, gb, wout, gffn, wrt, br)


def _positions_kernel(start_ref, *refs):
    n_parts = len(refs) // 3
    for p in range(n_parts):
        idx = refs[2 * p][...]
        pos = refs[2 * p + 1][...]
        for e in range(start_ref.shape[0]):
            pos = pos + jnp.where(idx == e, start_ref[e], 0)
        refs[2 * n_parts + p][...] = pos


def _positions(pad_start, parts):
    n_0 = parts[0][0].shape[1]
    tile = min(n_0, POS_TILE)
    specs = [pl.BlockSpec((TOP_K, tile), lambda i: (0, i))]
    specs += [pl.BlockSpec((TOP_K, idx.shape[1]), lambda i: (0, 0)) for idx, _ in parts[1:]]
    in_specs = [pl.BlockSpec(memory_space=pltpu.SMEM)]
    args = [pad_start]
    for spec, (idx, rank) in zip(specs, parts):
        in_specs += [spec, spec]
        args += [idx, rank]
    out = pl.pallas_call(
        _positions_kernel,
        grid=(n_0 // tile,),
        in_specs=in_specs,
        out_specs=tuple(specs),
        out_shape=tuple(jax.ShapeDtypeStruct(idx.shape, jnp.int32) for idx, _ in parts),
        compiler_params=pltpu.CompilerParams(dimension_semantics=("arbitrary",)),
        name="positions",
    )(*args)
    return list(out)


def _sc_mesh():
    return plsc.VectorSubcoreMesh(core_axis_name="c", subcore_axis_name="s",
                                  num_cores=SC_CORES, num_subcores=SC_SUBCORES)


def _sc_worker_id():
    return lax.axis_index("s") * SC_CORES + lax.axis_index("c")


def _sc_chunks(n_tok, wid):
    n_chunks = n_tok // SC_CHUNK
    assert n_tok % SC_CHUNK == 0
    if n_chunks % SC_WORKERS == 0:
        per_w = n_chunks // SC_WORKERS
        return None, per_w, lambda i: (wid * per_w + i) * SC_CHUNK
    assert n_chunks <= SC_WORKERS
    return wid < n_chunks, 1, lambda i: wid * SC_CHUNK


def _sc_for_each_chunk(n_tok, fn):
    wid = _sc_worker_id()
    pred, trips, base = _sc_chunks(n_tok, wid)

    def run():
        @pl.loop(0, trips)
        def _(i):
            fn(pl.multiple_of(base(i), SC_CHUNK))

    if pred is None:
        run()
    else:
        pl.when(pred)(run)


def _sc_dispatch(parts, n_rows):
    half = parts[0][1].shape[1]
    n_parts = len(parts)

    @functools.partial(
        pl.kernel, mesh=_sc_mesh(),
        out_type=jax.ShapeDtypeStruct((n_rows, half), jnp.uint32),
        scratch_types=[pltpu.VMEM((TOP_K, SC_CHUNK), jnp.int32),
                       pltpu.VMEM((SC_CHUNK, half), jnp.uint32),
                       pltpu.SemaphoreType.DMA],
        compiler_params=pltpu.CompilerParams(use_tc_tiling_on_sc=True),
        name="sc_dispatch")
    def k(*refs):
        xs_hbm, idx_v, rows_v, sem = refs[2 * n_parts:]

        def move(pos_hbm, src_hbm):
            def fn(base):
                pltpu.sync_copy(pos_hbm.at[:, pl.ds(base, SC_CHUNK)], idx_v)
                pltpu.sync_copy(src_hbm.at[pl.ds(base, SC_CHUNK)], rows_v)
                copies = [pltpu.async_copy(rows_v, xs_hbm.at[idx_v.at[j]], sem) for j in range(TOP_K)]
                for c in copies:
                    c.wait()
            return fn

        for p, (_, hnp) in enumerate(parts):
            _sc_for_each_chunk(hnp.shape[0], move(refs[2 * p], refs[2 * p + 1]))

    return k(*[a for part in parts for a in part])


def _sc_gather(pos_parts, rows):
    half = rows.shape[1]
    n_parts = len(pos_parts)

    @functools.partial(
        pl.kernel, mesh=_sc_mesh(),
        out_type=tuple(jax.ShapeDtypeStruct((TOP_K, p.shape[1], half), jnp.uint32) for p in pos_parts),
        scratch_types=[pltpu.VMEM((TOP_K, SC_CHUNK), jnp.int32),
                       pltpu.VMEM((SC_CHUNK, half), jnp.uint32),
                       pltpu.SemaphoreType.DMA],
        compiler_params=pltpu.CompilerParams(use_tc_tiling_on_sc=True),
        name="sc_gather")
    def k(*refs):
        pos_hbms, rows_hbm = refs[:n_parts], refs[n_parts]
        out_hbms = refs[n_parts + 1:2 * n_parts + 1]
        idx_v, buf_v, sem = refs[2 * n_parts + 1:]

        def move(pos_hbm, out_hbm):
            def fn(base):
                pltpu.sync_copy(pos_hbm.at[:, pl.ds(base, SC_CHUNK)], idx_v)
                for j in range(TOP_K):
                    pltpu.async_copy(rows_hbm.at[idx_v.at[j]], buf_v, sem).wait()
                    pltpu.sync_copy(buf_v, out_hbm.at[j, pl.ds(base, SC_CHUNK)])
            return fn

        for pos_hbm, out_hbm, p in zip(pos_hbms, out_hbms, pos_parts):
            _sc_for_each_chunk(p.shape[1], move(pos_hbm, out_hbm))

    out = k(*pos_parts, rows)
    return out if isinstance(out, (tuple, list)) else (out,)


def _experts_kernel(*refs, publish, reuse):
    blk_exp_ref, first_ref, slot_ref, next_ref, n_active_ref = refs[:5]
    refs = refs[5:]
    if reuse:
        have_ref, have_next_ref = refs[:2]
        refs = refs[2:]
    xs_ref, wgu_hbm, bgu_ref, wd_hbm, bd_ref = refs[:5]
    refs = refs[5:]
    if reuse:
        wgu16_hbm, wd16_hbm = refs[:2]
        refs = refs[2:]
    out_ref = refs[0]
    refs = refs[1:]
    if publish:
        wgu16_out, wd16_out = refs[:2]
        refs = refs[2:]
    wgu_f32, wd_f32, wgu_bf, wd_bf, sems = refs[:5]
    refs = refs[5:]
    if publish:
        out_sems, pending = refs
    if reuse:
        sems16, = refs
    d_ff = wd_bf.shape[-2]

    def bf_slot(parity):
        return parity if reuse else 0

    def f32_slot(parity):
        return 0 if reuse else parity

    def fetch(e, parity):
        sl = f32_slot(parity)
        return (pltpu.make_async_copy(wgu_hbm.at[e], wgu_f32.at[sl], sems.at[0, sl]),
                pltpu.make_async_copy(wd_hbm.at[e], wd_f32.at[sl], sems.at[1, sl]))

    def fetch16(e, parity):
        return (pltpu.make_async_copy(wgu16_hbm.at[e], wgu_bf.at[parity], sems16.at[0, parity]),
                pltpu.make_async_copy(wd16_hbm.at[e], wd_bf.at[parity], sems16.at[1, parity]))

    def publish_copies(e):
        return (pltpu.make_async_copy(wgu_bf.at[0], wgu16_out.at[e], out_sems.at[0]),
                pltpu.make_async_copy(wd_bf.at[0], wd16_out.at[e], out_sems.at[1]))

    def wait_published():
        @pl.when(pending[0] == 1)
        def _():
            for c in publish_copies(0):
                c.wait()
            pending[0] = 0

    def start_fetch(e, parity, have):
        if reuse:
            @pl.when(have == 1)
            def _():
                for c in fetch16(e, parity):
                    c.start()

            @pl.when(have != 1)
            def _():
                for c in fetch(e, parity):
                    c.start()
        else:
            for c in fetch(e, parity):
                c.start()

    def finish_fetch(e, parity, have):
        def from_f32():
            for c in fetch(e, parity):
                c.wait()
            if publish:
                wait_published()
            wgu_bf[bf_slot(parity)] = wgu_f32[f32_slot(parity)].astype(jnp.bfloat16)
            wd_bf[bf_slot(parity)] = wd_f32[f32_slot(parity)].astype(jnp.bfloat16)
            if publish:
                for c in publish_copies(e):
                    c.start()
                pending[0] = 1

        if reuse:
            @pl.when(have == 1)
            def _():
                for c in fetch16(e, parity):
                    c.wait()

            pl.when(have != 1)(from_f32)
        else:
            from_f32()

    def load_weights_if_first(i):
        parity = slot_ref[i]
        e = blk_exp_ref[i]
        have = have_ref[i] if reuse else 0

        @pl.when(first_ref[i] == 1)
        def _():
            @pl.when(i == 0)
            def _():
                if publish:
                    pending[0] = 0
                start_fetch(e, parity, have)

            finish_fetch(e, parity, have)

            @pl.when(next_ref[i] >= 0)
            def _():
                start_fetch(next_ref[i], 1 - parity, have_next_ref[i] if reuse else 0)

    def compute(rows, i):
        e = blk_exp_ref[i]
        sl = bf_slot(slot_ref[i])
        x = _unpack_bf16_pairs(xs_ref[rows, :]).astype(jnp.bfloat16)
        gu = jnp.dot(x, wgu_bf[sl], preferred_element_type=jnp.float32) + bgu_ref[e]
        g = jnp.minimum(gu[:, :d_ff], SWIGLU_LIMIT)
        u = jnp.clip(gu[:, d_ff:], -SWIGLU_LIMIT, SWIGLU_LIMIT)
        act = g * jax.nn.sigmoid(SWIGLU_ALPHA * g) * (u + 1.0)
        o = jnp.dot(act.astype(jnp.bfloat16), wd_bf[sl], preferred_element_type=jnp.float32) + bd_ref[e]
        out_ref[rows, :] = _pack_bf16_pairs(o)

    i0 = pl.program_id(0) * BLOCKS_PER_STEP
    n_active = n_active_ref[0]

    def run(sub0, n_sub):
        first = i0 + sub0
        rows = slice(sub0 * MOE_BLOCK, (sub0 + n_sub) * MOE_BLOCK)
        if n_sub == 1:
            @pl.when(first < n_active)
            def _():
                load_weights_if_first(first)
                compute(rows, first)

            @pl.when(first >= n_active)
            def _():
                out_ref[rows, :] = jnp.zeros((MOE_BLOCK, out_ref.shape[1]), out_ref.dtype)
            return

        same = first + n_sub - 1 < n_active
        for sub in range(1, n_sub):
            same = same & (blk_exp_ref[first + sub] == blk_exp_ref[first])

        @pl.when(same)
        def _():
            load_weights_if_first(first)
            compute(rows, first)

        @pl.when(jnp.logical_not(same))
        def _():
            run(sub0, n_sub // 2)
            run(sub0 + n_sub // 2, n_sub // 2)

    run(0, BLOCKS_PER_STEP)

    if publish:
        @pl.when(pl.program_id(0) == pl.num_programs(0) - 1)
        def _():
            wait_published()


def _experts(plan, xs, wgu, bgu, wd, bd, *, publish=False, reuse=None):
    blk_exp, first, slot, nxt, n_active = plan
    n_rows, half = xs.shape
    n_exp, d, two_f = wgu.shape
    d_ff = wd.shape[1]
    n_blocks = n_rows // MOE_BLOCK
    step_rows = MOE_BLOCK * BLOCKS_PER_STEP
    assert n_rows % step_rows == 0
    any_spec = pl.BlockSpec(memory_space=pl.ANY)
    scalars = [blk_exp, first, slot, nxt, n_active]
    operands = [xs, wgu, bgu.reshape(n_exp, 1, two_f), wd, bd.reshape(n_exp, 1, d)]
    in_specs = [
        pl.BlockSpec((step_rows, half), lambda i, *_: (i, 0)),
        any_spec,
        pl.BlockSpec((n_exp, 1, two_f), lambda i, *_: (0, 0, 0)),
        any_spec,
        pl.BlockSpec((n_exp, 1, d), lambda i, *_: (0, 0, 0)),
    ]
    out_specs = [pl.BlockSpec((step_rows, half), lambda i, *_: (i, 0))]
    out_shape = [jax.ShapeDtypeStruct((n_rows, half), jnp.uint32)]
    n_f32_slots, n_bf_slots = (1, 2) if reuse else (2, 1)
    scratch = [
        pltpu.VMEM((n_f32_slots, d, two_f), jnp.float32),
        pltpu.VMEM((n_f32_slots, d_ff, d), jnp.float32),
        pltpu.VMEM((n_bf_slots, d, two_f), jnp.bfloat16),
        pltpu.VMEM((n_bf_slots, d_ff, d), jnp.bfloat16),
        pltpu.SemaphoreType.DMA((2, 2)),
    ]
    if reuse:
        wgu16, wd16, have, have_next = reuse
        scalars += [have, have_next]
        operands += [wgu16, wd16]
        in_specs += [any_spec, any_spec]
        scratch += [pltpu.SemaphoreType.DMA((2, 2))]
    if publish:
        out_specs += [any_spec, any_spec]
        out_shape += [jax.ShapeDtypeStruct((n_exp, d, two_f), jnp.bfloat16),
                      jax.ShapeDtypeStruct((n_exp, d_ff, d), jnp.bfloat16)]
        scratch += [pltpu.SemaphoreType.DMA((2,)), pltpu.SMEM((1,), jnp.int32)]
    grid_spec = pltpu.PrefetchScalarGridSpec(
        num_scalar_prefetch=len(scalars),
        grid=(n_blocks // BLOCKS_PER_STEP,),
        in_specs=in_specs,
        out_specs=tuple(out_specs),
        scratch_shapes=scratch,
    )
    out = pl.pallas_call(
        functools.partial(_experts_kernel, publish=publish, reuse=bool(reuse)),
        grid_spec=grid_spec,
        out_shape=tuple(out_shape),
        compiler_params=pltpu.CompilerParams(
            dimension_semantics=("arbitrary",), vmem_limit_bytes=VMEM_LIMIT),
        name="experts",
    )(*scalars, *operands)
    return out if publish else out[0]


def _combine_kernel(rows_ref, gate_ref, h_ref, gfin_ref, *rest):
    y_ref = rest[-1]
    acc = h_ref[...]
    for k in range(TOP_K):
        acc = acc + gate_ref[:, k:k + 1] * _unpack_bf16_pairs(rows_ref[k])
    y_ref[...] = _rms(acc, gfin_ref[...])


def _combine(rows4, gate_t, h2d, gfin, *, tok0=0, y_prev=None, y_rows=None, y_tok0=0):
    d = h2d.shape[1]
    n, half = rows4.shape[1:]
    n_tok = h2d.shape[0] if y_rows is None else y_rows
    tile = min(n, COMBINE_TILE)
    t0 = tok0 // tile
    y0 = y_tok0 // tile
    in_specs = [
        pl.BlockSpec((TOP_K, tile, half), lambda i: (0, i, 0)),
        pl.BlockSpec((tile, gate_t.shape[1]), lambda i: (t0 + i, 0)),
        pl.BlockSpec((tile, d), lambda i: (t0 + i, 0)),
        pl.BlockSpec((1, d), lambda i: (0, 0)),
    ]
    args = [rows4, gate_t, h2d, gfin]
    aliases = {}
    if y_prev is not None:
        in_specs.append(pl.BlockSpec(memory_space=pl.ANY))
        args.append(y_prev)
        aliases = {4: 0}
    return pl.pallas_call(
        _combine_kernel,
        grid=(n // tile,),
        in_specs=in_specs,
        out_specs=pl.BlockSpec((tile, d), lambda i: (y0 + i, 0)),
        out_shape=jax.ShapeDtypeStruct((n_tok, d), jnp.float32),
        input_output_aliases=aliases,
        compiler_params=pltpu.CompilerParams(
            dimension_semantics=("arbitrary",), vmem_limit_bytes=VMEM_LIMIT),
        name="combine",
    )(*args)


def _pad_hist(hist, rows):
    return jnp.pad(hist, ((0, 0), (rows - hist.shape[1], 0), (0, 0)))


def kernel(x_prompt, x_sample, state_conv_a, state_conv_b, norm_mix_g, w_in, conv_a_w, conv_b_w, conv_b_b, conv_ln_g, conv_ln_b, out_norm_a_g, out_norm_b_g, w_out, norm_ffn_g, w_router, b_router, w_gate_up, b_gate_up, w_down, b_down, final_norm_g):
    depth = w_in.shape[0]
    assert depth == 1, "single-layer trunk"
    bf16 = jnp.bfloat16
    bsz, seq, d = x_prompt.shape
    dec_b, dec_s, _ = x_sample.shape
    d_a, d_b = conv_a_w.shape[-1], conv_b_w.shape[-1]
    n_exp = w_router.shape[-1]

    params = (norm_mix_g[0][None], w_in[0].astype(bf16), conv_a_w[0], conv_b_w[0], conv_b_b[0][None],
              conv_ln_g[0][None], conv_ln_b[0][None], out_norm_a_g[0][None], out_norm_b_g[0][None],
              w_out[0].astype(bf16), norm_ffn_g[0][None], w_router[0].T.astype(bf16),
              b_router[0][:, None])

    zero_a = jnp.zeros((bsz, HIST_A_ROWS, d_a), jnp.float32)
    zero_b = jnp.zeros((bsz, HIST_B_ROWS, d_b), jnp.float32)
    cnt0 = jnp.zeros((n_exp, 1), jnp.float32)
    gfin = final_norm_g[None]
    ts = min(seq, MIXER_ROWS)
    n_p = bsz * seq

    sizes = [bsz * f // sum(GROUP_SPLIT) for f in GROUP_SPLIT]
    if bsz % sum(GROUP_SPLIT) != 0:
        sizes = [bsz]
    groups = []
    tok_start = []
    for g, per_b in enumerate(sizes):
        b0 = sum(sizes[:g])
        tok_start.append(b0 * seq)
        h, hnp, idx, gate, rank, cnt, na, nb_ = _mixer(
            x_prompt, zero_a, zero_b, cnt0, params, nb=1, ts=ts, b0=b0, bsz=per_b)
        parts = [dict(h=h.reshape(per_b * seq, d), hnp=hnp, idx=idx, gate_t=gate, rank=rank)]
        states = [(na, nb_)]
        if g == len(sizes) - 1:
            h, hnp, idx, gate, rank, cnt, na_s, nb_s = _mixer(
                x_sample, _pad_hist(state_conv_a[0], HIST_A_ROWS), _pad_hist(state_conv_b[0], HIST_B_ROWS),
                cnt, params, nb=dec_b, ts=dec_s)
            parts.append(dict(h=h.reshape(dec_b * dec_s, d), hnp=hnp, idx=idx, gate_t=gate, rank=rank))
        n_pairs = sum(p["hnp"].shape[0] for p in parts) * TOP_K
        plan, pad_start, n_blocks, has, reuse_tables = _block_plan(cnt[:, 0].astype(jnp.int32), n_pairs, n_exp)
        pos = _positions(pad_start, [(p["idx"], p["rank"]) for p in parts])
        xs = _sc_dispatch([(q, p["hnp"]) for q, p in zip(pos, parts)], n_blocks * MOE_BLOCK)
        groups.append(dict(parts=parts, pos=pos, plan=plan, xs=xs, states=states, has=has,
                           reuse_tables=reuse_tables))

    weights = (w_gate_up[0], b_gate_up[0], w_down[0], b_down[0])
    pub = groups[-1]
    if len(groups) > 1:
        pub["rows"], wgu16, wd16 = _experts(pub["plan"], pub["xs"], *weights, publish=True)
        for grp in groups[:-1]:
            grp["rows"] = _experts(grp["plan"], grp["xs"], *weights,
                                   reuse=(wgu16, wd16) + grp["reuse_tables"](pub["has"]))
    else:
        pub["rows"] = _experts(pub["plan"], pub["xs"], *weights)

    gathered = []
    for g, grp in enumerate(groups):
        pos0 = grp["pos"][0]
        n_g = pos0.shape[1]
        want = 1 if (g == len(groups) - 1 and len(groups) > 1) else COMBINE_CHUNKS
        n_chunks = want if n_g % (want * SC_CHUNK * SC_WORKERS) == 0 else 1
        per = n_g // n_chunks
        for c in range(n_chunks):
            pos_parts = [pos0[:, c * per:(c + 1) * per]] + (grp["pos"][1:] if c == 0 else [])
            gathered.append((g, c * per, _sc_gather(pos_parts, grp["rows"])))

    y_p = None
    for g, tok0, got in sorted(gathered, key=lambda t: (t[0] != len(groups) - 1, t[0], t[1])):
        grp = groups[g]
        main = grp["parts"][0]
        if len(got) > 1:
            smp = grp["parts"][1]
            y_s = _combine(got[1], smp["gate_t"], smp["h"], gfin).reshape(dec_b, dec_s, d)
        y_p = _combine(got[0], main["gate_t"], main["h"], gfin, tok0=tok0,
                       y_prev=y_p, y_rows=n_p, y_tok0=tok_start[g] + tok0)
    y_p = y_p.reshape(bsz, seq, d)

    na_p = jnp.concatenate([grp["states"][0][0] for grp in groups], axis=0)
    nb_p = jnp.concatenate([grp["states"][0][1] for grp in groups], axis=0)
    return (y_p, y_s, na_p[None], nb_p[None], na_s[None], nb_s[None])


def _block_plan(counts, n_pairs, n_exp):
    padded = (counts + MOE_BLOCK - 1) // MOE_BLOCK * MOE_BLOCK
    pad_end = jnp.cumsum(padded)
    pad_start = pad_end - padded
    n_blocks = -(-n_pairs // MOE_BLOCK) + n_exp
    n_blocks = -(-n_blocks // BLOCKS_PER_STEP) * BLOCKS_PER_STEP
    blk_start = jnp.arange(n_blocks, dtype=jnp.int32) * MOE_BLOCK
    blk_exp = jnp.minimum(jnp.sum(blk_start[:, None] >= pad_end[None, :], axis=1),
                          n_exp - 1).astype(jnp.int32)
    n_active = (pad_end[-1:] // MOE_BLOCK).astype(jnp.int32)
    e_ids = jnp.arange(n_exp, dtype=jnp.int32)
    has = padded > 0
    slot_e = (jnp.cumsum(has.astype(jnp.int32)) - 1) % 2
    later = jnp.where(has, e_ids, n_exp)
    next_e = jnp.concatenate([lax.cummin(later, reverse=True)[1:], jnp.full((1,), n_exp, jnp.int32)])
    next_e = jnp.where(next_e >= n_exp, -1, next_e)
    blk_onehot = blk_exp[:, None] == e_ids[None, :]

    def per_block(table):
        return jnp.sum(jnp.where(blk_onehot, table[None, :], 0), axis=1).astype(jnp.int32)

    blk_first = ((blk_start == per_block(pad_start)) & (blk_start < pad_end[-1])).astype(jnp.int32)
    plan = (blk_exp, blk_first, per_block(slot_e), per_block(next_e), n_active)

    def reuse_tables(has_pub):
        pub = has_pub.astype(jnp.int32)
        pub_next = jnp.sum(jnp.where(next_e[:, None] == e_ids[None, :], pub[None, :], 0), axis=1)
        return per_block(pub), per_block(pub_next)

    return plan, pad_start.astype(jnp.int32), n_blocks, has, reuse_tables
```

```python
import functools

import jax
import jax.numpy as jnp
from jax import lax
from jax.experimental import pallas as pl
from jax.experimental.pallas import tpu as pltpu
from jax.experimental.pallas import tpu_sc as plsc

SC_CORES = 2
SC_SUBCORES = 16
SC_WORKERS = SC_CORES * SC_SUBCORES
SC_CHUNK = 128
CONV_A_W = 3
CONV_B_W = 31
TOP_K = 4
SWIGLU_LIMIT = 7.0
SWIGLU_ALPHA = 1.702
NORM_EPS = 1e-5
MOE_BLOCK = 256
BLOCKS_PER_STEP = 4
SUBLANES = 8
LANES = 128
CONV_ROW_CHUNK = 128
HIST_A_ROWS = 8
HIST_B_ROWS = 32
MIXER_ROWS = 1024
PROJ_SECTION = 256
COMBINE_TILE = 1024
POS_TILE = 8192
GROUP_SPLIT = (1, 3)
COMBINE_CHUNKS = 1
VMEM_LIMIT = 56 * 1024 * 1024


def _rms(x, g):
    return x * lax.rsqrt(jnp.mean(x * x, axis=-1, keepdims=True) + NORM_EPS) * g


def _zero_tile_from(v):
    bits = pltpu.bitcast(v[0:SUBLANES, 0:LANES], jnp.uint32)
    return pltpu.bitcast((bits >> 16) >> 16, jnp.float32)


def _pack_bf16_pairs(a):
    w = a.shape[-1] // 2
    lo = pltpu.bitcast(a[:, :w].astype(jnp.bfloat16).astype(jnp.float32), jnp.uint32)
    hi = pltpu.bitcast(a[:, w:].astype(jnp.bfloat16).astype(jnp.float32), jnp.uint32)
    return (lo >> 16) | (hi & jnp.uint32(0xFFFF0000))


def _unpack_bf16_pairs(wd):
    lo = pltpu.bitcast(wd << 16, jnp.float32)
    hi = pltpu.bitcast(wd & jnp.uint32(0xFFFF0000), jnp.float32)
    return jnp.concatenate([lo, hi], axis=-1)


def _conv_b_chunk(ub_buf, cbw_ref, n_i, r0, rc, c0):
    cs = slice(c0, c0 + LANES)
    lead = HIST_B_ROWS - (CONV_B_W - 1)
    acc = None
    for b in range(SUBLANES):
        rows = rc if b == 0 else rc + SUBLANES
        q = None
        for a in range((HIST_B_ROWS + SUBLANES) // SUBLANES):
            k = SUBLANES * a + b - lead
            if k < 0 or k >= CONV_B_W:
                continue
            start = r0 + SUBLANES * a
            term = cbw_ref[k:k + 1, cs] * ub_buf[n_i, start:start + rows, cs]
            q = term if q is None else q + term
        part = q if b == 0 else q[b:b + rc]
        acc = part if acc is None else acc + part
    return acc


def _mixer_kernel(x_ref, hista_ref, histb_ref, cnt0_ref, tri_ref,
                  gmix_ref, win_ref, caw_ref, cbw_ref, cbb_ref, lng_ref, lnb_ref,
                  ga_ref, gb_ref, wout_ref, gffn_ref, wrt_ref, br_ref,
                  h_ref, hnp_ref, idx_ref, gate_ref, rank_ref, cnt_ref, newa_ref, newb_ref,
                  ua_buf, ub_buf, cb_buf, cnt_acc, *, nb, ts, d_a, d_b, n_exp):
    b = pl.program_id(0)
    s = pl.program_id(1)
    m = nb * ts
    d = x_ref.shape[-1]

    @pl.when(s == 0)
    def _():
        ua_buf[:, 0:HIST_A_ROWS, :] = hista_ref[...]
        ub_buf[:, 0:HIST_B_ROWS, :] = histb_ref[...]

    @pl.when(s != 0)
    def _():
        ua_buf[:, 0:HIST_A_ROWS, :] = ua_buf[:, ts:ts + HIST_A_ROWS, :]
        ub_buf[:, 0:HIST_B_ROWS, :] = ub_buf[:, ts:ts + HIST_B_ROWS, :]

    @pl.when((b == 0) & (s == 0))
    def _():
        cnt_acc[...] = cnt0_ref[...]

    x = x_ref[...].reshape(m, d)
    n = _rms(x, gmix_ref[...]).astype(jnp.bfloat16)
    def proj(c0, width):
        return jnp.dot(n, win_ref[:, c0:c0 + width], preferred_element_type=jnp.float32)

    sec = min(d_b, PROJ_SECTION)
    for c0 in range(0, d_b, sec):
        v_b = proj(3 * d_a + c0, sec)
        g_b = proj(3 * d_a + d_b + c0, sec)
        ub_buf[:, HIST_B_ROWS:HIST_B_ROWS + ts, c0:c0 + sec] = (v_b * jax.nn.sigmoid(g_b)).reshape(nb, ts, sec)

    gate_c = proj(d_a, d_a)
    xt = proj(2 * d_a, d_a)
    gate_b = proj(0, d_a)
    ua_buf[:, HIST_A_ROWS:HIST_A_ROWS + ts, :] = (gate_c * xt).reshape(nb, ts, d_a)
    conv_a = jnp.zeros((nb, ts, d_a), jnp.float32)
    for k in range(CONV_A_W):
        off = HIST_A_ROWS - (CONV_A_W - 1) + k
        conv_a = conv_a + caw_ref[k:k + 1, :] * ua_buf[:, off:off + ts, :]
    y_a = gate_b * conv_a.reshape(m, d_a)

    mix_a = jnp.dot(_rms(y_a, ga_ref[...]).astype(jnp.bfloat16), wout_ref[0:d_a, :],
                    preferred_element_type=jnp.float32)

    rc = min(ts, CONV_ROW_CHUNK)
    chunks = [(n_i, r0, c0) for n_i in range(nb) for r0 in range(0, ts, rc) for c0 in range(0, d_b, LANES)]
    anchors = {len(chunks) * 3 // 8: gate_c, len(chunks) * 4 // 8: xt, len(chunks) * 6 // 8: gate_b,
               len(chunks) * 7 // 8: mix_a}
    for ci, (n_i, r0, c0) in enumerate(chunks):
        acc = _conv_b_chunk(ub_buf, cbw_ref, n_i, r0, rc, c0)
        if ci in anchors:
            acc = (acc.reshape(rc // SUBLANES, SUBLANES, LANES) + _zero_tile_from(anchors[ci])[None]
                   ).reshape(rc, LANES)
        cb_buf[n_i, r0:r0 + rc, c0:c0 + LANES] = acc
    cb = cb_buf[...].reshape(m, d_b) + cbb_ref[...]
    mu = jnp.mean(cb, axis=-1, keepdims=True)
    xc = cb - mu
    var = jnp.mean(xc * xc, axis=-1, keepdims=True)
    ln = xc * lax.rsqrt(var + NORM_EPS) * lng_ref[...] + lnb_ref[...]
    y_b = ln * jax.nn.sigmoid(ln)

    mix_b = jnp.dot(_rms(y_b, gb_ref[...]).astype(jnp.bfloat16), wout_ref[d_a:d_a + d_b, :],
                    preferred_element_type=jnp.float32)
    h = x + (mix_a + mix_b)
    h_ref[...] = h.reshape(nb, ts, d)

    hn = _rms(h, gffn_ref[...])
    hnp_ref[...] = _pack_bf16_pairs(hn)

    logits = lax.dot_general(wrt_ref[...], hn.astype(jnp.bfloat16), (((1,), (1,)), ((), ())),
                             preferred_element_type=jnp.float32) + br_ref[...]
    e_iota = lax.broadcasted_iota(jnp.int32, (n_exp, m), 0)
    cur = logits
    vals, idxs, sels = [], [], []
    for _ in range(TOP_K):
        mx = jnp.max(cur, axis=0, keepdims=True)
        ix = jnp.min(jnp.where(cur == mx, e_iota, n_exp), axis=0, keepdims=True)
        sel = e_iota == ix
        vals.append(mx)
        idxs.append(ix)
        sels.append(sel)
        cur = jnp.where(sel, -jnp.inf, cur)
    exps = [jnp.exp(v - vals[0]) for v in vals]
    denom = exps[0] + exps[1] + exps[2] + exps[3]
    gates = [e / denom for e in exps]

    onehot = jnp.zeros((n_exp, m), jnp.float32)
    for sel in sels:
        onehot = onehot + sel.astype(jnp.float32)
    before = jnp.dot(onehot.astype(jnp.bfloat16), tri_ref[...],
                     preferred_element_type=jnp.float32) + cnt_acc[...]
    ranks = [jnp.sum(jnp.where(sel, before, 0.0), axis=0, keepdims=True) for sel in sels]
    new_cnt = cnt_acc[...] + jnp.sum(onehot, axis=1, keepdims=True)
    cnt_acc[...] = new_cnt
    cnt_ref[...] = new_cnt

    idx_ref[...] = jnp.concatenate(idxs, axis=0)
    gate_ref[...] = jnp.concatenate(gates + [jnp.zeros((SUBLANES - TOP_K, m), jnp.float32)], axis=0).T
    rank_ref[...] = jnp.concatenate(ranks, axis=0).astype(jnp.int32)

    @pl.when(s == pl.num_programs(1) - 1)
    def _():
        newa_ref[...] = ua_buf[:, ts + HIST_A_ROWS - (CONV_A_W - 1):ts + HIST_A_ROWS, :]
        newb_ref[...] = ub_buf[:, ts + HIST_B_ROWS - (CONV_B_W - 1):ts + HIST_B_ROWS, :]


def _mixer(x, hist_a, hist_b, cnt0, params, *, nb, ts, b0=0, bsz=None):
    (gmix, win, caw, cbw, cbb, lng, lnb, ga, gb, wout, gffn, wrt, br) = params
    _, seq, d = x.shape
    bsz = x.shape[0] if bsz is None else bsz
    d_a, d_b = caw.shape[-1], cbw.shape[-1]
    n_exp = wrt.shape[0]
    m = nb * ts
    n_tok = bsz * seq
    g0 = b0 // nb
    grid = (bsz // nb, seq // ts)
    tri = (jnp.arange(m)[:, None] < jnp.arange(m)[None, :]).astype(jnp.bfloat16)

    def const(shape):
        return pl.BlockSpec(shape, lambda i, j: (0,) * len(shape))

    tok_map = lambda i, j: (0, i * (seq // ts) + j)
    kern = functools.partial(_mixer_kernel, nb=nb, ts=ts, d_a=d_a, d_b=d_b, n_exp=n_exp)
    out_shape = (
        jax.ShapeDtypeStruct((bsz, seq, d), jnp.float32),
        jax.ShapeDtypeStruct((n_tok, d // 2), jnp.uint32),
        jax.ShapeDtypeStruct((TOP_K, n_tok), jnp.int32),
        jax.ShapeDtypeStruct((n_tok, SUBLANES), jnp.float32),
        jax.ShapeDtypeStruct((TOP_K, n_tok), jnp.int32),
        jax.ShapeDtypeStruct((n_exp, 1), jnp.float32),
        jax.ShapeDtypeStruct((bsz, CONV_A_W - 1, d_a), jnp.float32),
        jax.ShapeDtypeStruct((bsz, CONV_B_W - 1, d_b), jnp.float32),
    )
    return pl.pallas_call(
        kern,
        grid=grid,
        in_specs=[
            pl.BlockSpec((nb, ts, d), lambda i, j: (g0 + i, j, 0)),
            pl.BlockSpec((nb, HIST_A_ROWS, d_a), lambda i, j: (g0 + i, 0, 0)),
            pl.BlockSpec((nb, HIST_B_ROWS, d_b), lambda i, j: (g0 + i, 0, 0)),
            const((n_exp, 1)),
            const((m, m)),
            const((1, d)), const(win.shape), const(caw.shape), const(cbw.shape), const((1, d_b)),
            const((1, d_b)), const((1, d_b)), const((1, d_a)), const((1, d_b)), const(wout.shape),
            const((1, d)), const(wrt.shape), const((n_exp, 1)),
        ],
        out_specs=(
            pl.BlockSpec((nb, ts, d), lambda i, j: (i, j, 0)),
            pl.BlockSpec((m, d // 2), lambda i, j: (i * (seq // ts) + j, 0)),
            pl.BlockSpec((TOP_K, m), tok_map),
            pl.BlockSpec((m, SUBLANES), lambda i, j: (i * (seq // ts) + j, 0)),
            pl.BlockSpec((TOP_K, m), tok_map),
            pl.BlockSpec((n_exp, 1), lambda i, j: (0, 0)),
            pl.BlockSpec((nb, CONV_A_W - 1, d_a), lambda i, j: (i, 0, 0)),
            pl.BlockSpec((nb, CONV_B_W - 1, d_b), lambda i, j: (i, 0, 0)),
        ),
        out_shape=out_shape,
        scratch_shapes=[
            pltpu.VMEM((nb, HIST_A_ROWS + ts, d_a), jnp.float32),
            pltpu.VMEM((nb, HIST_B_ROWS + ts, d_b), jnp.float32),
            pltpu.VMEM((nb, ts, d_b), jnp.float32),
            pltpu.VMEM((n_exp, 1), jnp.float32),
        ],
        compiler_params=pltpu.CompilerParams(
            dimension_semantics=("arbitrary", "arbitrary"), vmem_limit_bytes=VMEM_LIMIT),
        name="mixer",
    )(x, hist_a, hist_b, cnt0, tri, gmix, win, caw, cbw, cbb, lng, lnb, ga, gb, wout, gffn, wrt, br)


def _positions_kernel(start_ref, *refs):
    n_parts = len(refs) // 3
    for p in range(n_parts):
        idx = refs[2 * p][...]
        pos = refs[2 * p + 1][...]
        for e in range(start_ref.shape[0]):
            pos = pos + jnp.where(idx == e, start_ref[e], 0)
        refs[2 * n_parts + p][...] = pos


def _positions(pad_start, parts):
    n_0 = parts[0][0].shape[1]
    tile = min(n_0, POS_TILE)
    specs = [pl.BlockSpec((TOP_K, tile), lambda i: (0, i))]
    specs += [pl.BlockSpec((TOP_K, idx.shape[1]), lambda i: (0, 0)) for idx, _ in parts[1:]]
    in_specs = [pl.BlockSpec(memory_space=pltpu.SMEM)]
    args = [pad_start]
    for spec, (idx, rank) in zip(specs, parts):
        in_specs += [spec, spec]
        args += [idx, rank]
    out = pl.pallas_call(
        _positions_kernel,
        grid=(n_0 // tile,),
        in_specs=in_specs,
        out_specs=tuple(specs),
        out_shape=tuple(jax.ShapeDtypeStruct(idx.shape, jnp.int32) for idx, _ in parts),
        compiler_params=pltpu.CompilerParams(dimension_semantics=("arbitrary",)),
        name="positions",
    )(*args)
    return list(out)


def _sc_mesh():
    return plsc.VectorSubcoreMesh(core_axis_name="c", subcore_axis_name="s",
                                  num_cores=SC_CORES, num_subcores=SC_SUBCORES)


def _sc_worker_id():
    return lax.axis_index("s") * SC_CORES + lax.axis_index("c")


def _sc_chunks(n_tok, wid):
    n_chunks = n_tok // SC_CHUNK
    assert n_tok % SC_CHUNK == 0
    if n_chunks % SC_WORKERS == 0:
        per_w = n_chunks // SC_WORKERS
        return None, per_w, lambda i: (wid * per_w + i) * SC_CHUNK
    assert n_chunks <= SC_WORKERS
    return wid < n_chunks, 1, lambda i: wid * SC_CHUNK


def _sc_for_each_chunk(n_tok, fn):
    wid = _sc_worker_id()
    pred, trips, base = _sc_chunks(n_tok, wid)

    def run():
        @pl.loop(0, trips)
        def _(i):
            fn(pl.multiple_of(base(i), SC_CHUNK))

    if pred is None:
        run()
    else:
        pl.when(pred)(run)


def _sc_dispatch(parts, n_rows):
    half = parts[0][1].shape[1]
    n_parts = len(parts)

    @functools.partial(
        pl.kernel, mesh=_sc_mesh(),
        out_type=jax.ShapeDtypeStruct((n_rows, half), jnp.uint32),
        scratch_types=[pltpu.VMEM((TOP_K, SC_CHUNK), jnp.int32),
                       pltpu.VMEM((SC_CHUNK, half), jnp.uint32),
                       pltpu.SemaphoreType.DMA],
        compiler_params=pltpu.CompilerParams(use_tc_tiling_on_sc=True),
        name="sc_dispatch")
    def k(*refs):
        xs_hbm, idx_v, rows_v, sem = refs[2 * n_parts:]

        def move(pos_hbm, src_hbm):
            def fn(base):
                pltpu.sync_copy(pos_hbm.at[:, pl.ds(base, SC_CHUNK)], idx_v)
                pltpu.sync_copy(src_hbm.at[pl.ds(base, SC_CHUNK)], rows_v)
                copies = [pltpu.async_copy(rows_v, xs_hbm.at[idx_v.at[j]], sem) for j in range(TOP_K)]
                for c in copies:
                    c.wait()
            return fn

        for p, (_, hnp) in enumerate(parts):
            _sc_for_each_chunk(hnp.shape[0], move(refs[2 * p], refs[2 * p + 1]))

    return k(*[a for part in parts for a in part])


def _sc_gather(pos_parts, rows):
    half = rows.shape[1]
    n_parts = len(pos_parts)

    @functools.partial(
        pl.kernel, mesh=_sc_mesh(),
        out_type=tuple(jax.ShapeDtypeStruct((TOP_K, p.shape[1], half), jnp.uint32) for p in pos_parts),
        scratch_types=[pltpu.VMEM((TOP_K, SC_CHUNK), jnp.int32),
                       pltpu.VMEM((SC_CHUNK, half), jnp.uint32),
                       pltpu.SemaphoreType.DMA],
        compiler_params=pltpu.CompilerParams(use_tc_tiling_on_sc=True),
        name="sc_gather")
    def k(*refs):
        pos_hbms, rows_hbm = refs[:n_parts], refs[n_parts]
        out_hbms = refs[n_parts + 1:2 * n_parts + 1]
        idx_v, buf_v, sem = refs[2 * n_parts + 1:]

        def move(pos_hbm, out_hbm):
            def fn(base):
                pltpu.sync_copy(pos_hbm.at[:, pl.ds(base, SC_CHUNK)], idx_v)
                for j in range(TOP_K):
                    pltpu.async_copy(rows_hbm.at[idx_v.at[j]], buf_v, sem).wait()
                    pltpu.sync_copy(buf_v, out_hbm.at[j, pl.ds(base, SC_CHUNK)])
            return fn

        for pos_hbm, out_hbm, p in zip(pos_hbms, out_hbms, pos_parts):
            _sc_for_each_chunk(p.shape[1], move(pos_hbm, out_hbm))

    out = k(*pos_parts, rows)
    return out if isinstance(out, (tuple, list)) else (out,)


def _experts_kernel(*refs, publish, reuse):
    blk_exp_ref, first_ref, slot_ref, next_ref, n_active_ref = refs[:5]
    refs = refs[5:]
    if reuse:
        have_ref, have_next_ref = refs[:2]
        refs = refs[2:]
    xs_ref, wgu_hbm, bgu_ref, wd_hbm, bd_ref = refs[:5]
    refs = refs[5:]
    if reuse:
        wgu16_hbm, wd16_hbm = refs[:2]
        refs = refs[2:]
    out_ref = refs[0]
    refs = refs[1:]
    if publish:
        wgu16_out, wd16_out = refs[:2]
        refs = refs[2:]
    wgu_f32, wd_f32, wgu_bf, wd_bf, sems = refs[:5]
    refs = refs[5:]
    if publish:
        out_sems, pending = refs
    if reuse:
        sems16, = refs
    d_ff = wd_bf.shape[-2]

    def bf_slot(parity):
        return parity if reuse else 0

    def f32_slot(parity):
        return 0 if reuse else parity

    def fetch(e, parity):
        sl = f32_slot(parity)
        return (pltpu.make_async_copy(wgu_hbm.at[e], wgu_f32.at[sl], sems.at[0, sl]),
                pltpu.make_async_copy(wd_hbm.at[e], wd_f32.at[sl], sems.at[1, sl]))

    def fetch16(e, parity):
        return (pltpu.make_async_copy(wgu16_hbm.at[e], wgu_bf.at[parity], sems16.at[0, parity]),
                pltpu.make_async_copy(wd16_hbm.at[e], wd_bf.at[parity], sems16.at[1, parity]))

    def publish_copies(e):
        return (pltpu.make_async_copy(wgu_bf.at[0], wgu16_out.at[e], out_sems.at[0]),
                pltpu.make_async_copy(wd_bf.at[0], wd16_out.at[e], out_sems.at[1]))

    def wait_published():
        @pl.when(pending[0] == 1)
        def _():
            for c in publish_copies(0):
                c.wait()
            pending[0] = 0

    def start_fetch(e, parity, have):
        if reuse:
            @pl.when(have == 1)
            def _():
                for c in fetch16(e, parity):
                    c.start()

            @pl.when(have != 1)
            def _():
                for c in fetch(e, parity):
                    c.start()
        else:
            for c in fetch(e, parity):
                c.start()

    def finish_fetch(e, parity, have):
        def from_f32():
            for c in fetch(e, parity):
                c.wait()
            if publish:
                wait_published()
            wgu_bf[bf_slot(parity)] = wgu_f32[f32_slot(parity)].astype(jnp.bfloat16)
            wd_bf[bf_slot(parity)] = wd_f32[f32_slot(parity)].astype(jnp.bfloat16)
            if publish:
                for c in publish_copies(e):
                    c.start()
                pending[0] = 1

        if reuse:
            @pl.when(have == 1)
            def _():
                for c in fetch16(e, parity):
                    c.wait()

            pl.when(have != 1)(from_f32)
        else:
            from_f32()

    def load_weights_if_first(i):
        parity = slot_ref[i]
        e = blk_exp_ref[i]
        have = have_ref[i] if reuse else 0

        @pl.when(first_ref[i] == 1)
        def _():
            @pl.when(i == 0)
            def _():
                if publish:
                    pending[0] = 0
                start_fetch(e, parity, have)

            finish_fetch(e, parity, have)

            @pl.when(next_ref[i] >= 0)
            def _():
                start_fetch(next_ref[i], 1 - parity, have_next_ref[i] if reuse else 0)

    def compute(rows, i):
        e = blk_exp_ref[i]
        sl = bf_slot(slot_ref[i])
        x = _unpack_bf16_pairs(xs_ref[rows, :]).astype(jnp.bfloat16)
        gu = jnp.dot(x, wgu_bf[sl], preferred_element_type=jnp.float32) + bgu_ref[e]
        g = jnp.minimum(gu[:, :d_ff], SWIGLU_LIMIT)
        u = jnp.clip(gu[:, d_ff:], -SWIGLU_LIMIT, SWIGLU_LIMIT)
        act = g * jax.nn.sigmoid(SWIGLU_ALPHA * g) * (u + 1.0)
        o = jnp.dot(act.astype(jnp.bfloat16), wd_bf[sl], preferred_element_type=jnp.float32) + bd_ref[e]
        out_ref[rows, :] = _pack_bf16_pairs(o)

    i0 = pl.program_id(0) * BLOCKS_PER_STEP
    n_active = n_active_ref[0]

    def run(sub0, n_sub):
        first = i0 + sub0
        rows = slice(sub0 * MOE_BLOCK, (sub0 + n_sub) * MOE_BLOCK)
        if n_sub == 1:
            @pl.when(first < n_active)
            def _():
                load_weights_if_first(first)
                compute(rows, first)

            @pl.when(first >= n_active)
            def _():
                out_ref[rows, :] = jnp.zeros((MOE_BLOCK, out_ref.shape[1]), out_ref.dtype)
            return

        same = first + n_sub - 1 < n_active
        for sub in range(1, n_sub):
            same = same & (blk_exp_ref[first + sub] == blk_exp_ref[first])

        @pl.when(same)
        def _():
            load_weights_if_first(first)
            compute(rows, first)

        @pl.when(jnp.logical_not(same))
        def _():
            run(sub0, n_sub // 2)
            run(sub0 + n_sub // 2, n_sub // 2)

    run(0, BLOCKS_PER_STEP)

    if publish:
        @pl.when(pl.program_id(0) == pl.num_programs(0) - 1)
        def _():
            wait_published()


def _experts(plan, xs, wgu, bgu, wd, bd, *, publish=False, reuse=None):
    blk_exp, first, slot, nxt, n_active = plan
    n_rows, half = xs.shape
    n_exp, d, two_f = wgu.shape
    d_ff = wd.shape[1]
    n_blocks = n_rows // MOE_BLOCK
    step_rows = MOE_BLOCK * BLOCKS_PER_STEP
    assert n_rows % step_rows == 0
    any_spec = pl.BlockSpec(memory_space=pl.ANY)
    scalars = [blk_exp, first, slot, nxt, n_active]
    operands = [xs, wgu, bgu.reshape(n_exp, 1, two_f), wd, bd.reshape(n_exp, 1, d)]
    in_specs = [
        pl.BlockSpec((step_rows, half), lambda i, *_: (i, 0)),
        any_spec,
        pl.BlockSpec((n_exp, 1, two_f), lambda i, *_: (0, 0, 0)),
        any_spec,
        pl.BlockSpec((n_exp, 1, d), lambda i, *_: (0, 0, 0)),
    ]
    out_specs = [pl.BlockSpec((step_rows, half), lambda i, *_: (i, 0))]
    out_shape = [jax.ShapeDtypeStruct((n_rows, half), jnp.uint32)]
    n_f32_slots, n_bf_slots = (1, 2) if reuse else (2, 1)
    scratch = [
        pltpu.VMEM((n_f32_slots, d, two_f), jnp.float32),
        pltpu.VMEM((n_f32_slots, d_ff, d), jnp.float32),
        pltpu.VMEM((n_bf_slots, d, two_f), jnp.bfloat16),
        pltpu.VMEM((n_bf_slots, d_ff, d), jnp.bfloat16),
        pltpu.SemaphoreType.DMA((2, 2)),
    ]
    if reuse:
        wgu16, wd16, have, have_next = reuse
        scalars += [have, have_next]
        operands += [wgu16, wd16]
        in_specs += [any_spec, any_spec]
        scratch += [pltpu.SemaphoreType.DMA((2, 2))]
    if publish:
        out_specs += [any_spec, any_spec]
        out_shape += [jax.ShapeDtypeStruct((n_exp, d, two_f), jnp.bfloat16),
                      jax.ShapeDtypeStruct((n_exp, d_ff, d), jnp.bfloat16)]
        scratch += [pltpu.SemaphoreType.DMA((2,)), pltpu.SMEM((1,), jnp.int32)]
    grid_spec = pltpu.PrefetchScalarGridSpec(
        num_scalar_prefetch=len(scalars),
        grid=(n_blocks // BLOCKS_PER_STEP,),
        in_specs=in_specs,
        out_specs=tuple(out_specs),
        scratch_shapes=scratch,
    )
    out = pl.pallas_call(
        functools.partial(_experts_kernel, publish=publish, reuse=bool(reuse)),
        grid_spec=grid_spec,
        out_shape=tuple(out_shape),
        compiler_params=pltpu.CompilerParams(
            dimension_semantics=("arbitrary",), vmem_limit_bytes=VMEM_LIMIT),
        name="experts",
    )(*scalars, *operands)
    return out if publish else out[0]


def _combine_kernel(rows_ref, gate_ref, h_ref, gfin_ref, *rest):
    y_ref = rest[-1]
    acc = h_ref[...]
    for k in range(TOP_K):
        acc = acc + gate_ref[:, k:k + 1] * _unpack_bf16_pairs(rows_ref[k])
    y_ref[...] = _rms(acc, gfin_ref[...])


def _combine(rows4, gate_t, h2d, gfin, *, tok0=0, y_prev=None, y_rows=None, y_tok0=0):
    d = h2d.shape[1]
    n, half = rows4.shape[1:]
    n_tok = h2d.shape[0] if y_rows is None else y_rows
    tile = min(n, COMBINE_TILE)
    t0 = tok0 // tile
    y0 = y_tok0 // tile
    in_specs = [
        pl.BlockSpec((TOP_K, tile, half), lambda i: (0, i, 0)),
        pl.BlockSpec((tile, gate_t.shape[1]), lambda i: (t0 + i, 0)),
        pl.BlockSpec((tile, d), lambda i: (t0 + i, 0)),
        pl.BlockSpec((1, d), lambda i: (0, 0)),
    ]
    args = [rows4, gate_t, h2d, gfin]
    aliases = {}
    if y_prev is not None:
        in_specs.append(pl.BlockSpec(memory_space=pl.ANY))
        args.append(y_prev)
        aliases = {4: 0}
    return pl.pallas_call(
        _combine_kernel,
        grid=(n // tile,),
        in_specs=in_specs,
        out_specs=pl.BlockSpec((tile, d), lambda i: (y0 + i, 0)),
        out_shape=jax.ShapeDtypeStruct((n_tok, d), jnp.float32),
        input_output_aliases=aliases,
        compiler_params=pltpu.CompilerParams(
            dimension_semantics=("arbitrary",), vmem_limit_bytes=VMEM_LIMIT),
        name="combine",
    )(*args)


def _pad_hist(hist, rows):
    return jnp.pad(hist, ((0, 0), (rows - hist.shape[1], 0), (0, 0)))


def kernel(x_prompt, x_sample, state_conv_a, state_conv_b, norm_mix_g, w_in, conv_a_w, conv_b_w, conv_b_b, conv_ln_g, conv_ln_b, out_norm_a_g, out_norm_b_g, w_out, norm_ffn_g, w_router, b_router, w_gate_up, b_gate_up, w_down, b_down, final_norm_g):
    depth = w_in.shape[0]
    assert depth == 1, "single-layer trunk"
    bf16 = jnp.bfloat16
    bsz, seq, d = x_prompt.shape
    dec_b, dec_s, _ = x_sample.shape
    d_a, d_b = conv_a_w.shape[-1], conv_b_w.shape[-1]
    n_exp = w_router.shape[-1]

    params = (norm_mix_g[0][None], w_in[0].astype(bf16), conv_a_w[0], conv_b_w[0], conv_b_b[0][None],
              conv_ln_g[0][None], conv_ln_b[0][None], out_norm_a_g[0][None], out_norm_b_g[0][None],
              w_out[0].astype(bf16), norm_ffn_g[0][None], w_router[0].T.astype(bf16),
              b_router[0][:, None])

    zero_a = jnp.zeros((bsz, HIST_A_ROWS, d_a), jnp.float32)
    zero_b = jnp.zeros((bsz, HIST_B_ROWS, d_b), jnp.float32)
    cnt0 = jnp.zeros((n_exp, 1), jnp.float32)
    gfin = final_norm_g[None]
    ts = min(seq, MIXER_ROWS)
    n_p = bsz * seq

    sizes = [bsz * f // sum(GROUP_SPLIT) for f in GROUP_SPLIT]
    if bsz % sum(GROUP_SPLIT) != 0:
        sizes = [bsz]
    groups = []
    tok_start = []
    for g, per_b in enumerate(sizes):
        b0 = sum(sizes[:g])
        tok_start.append(b0 * seq)
        h, hnp, idx, gate, rank, cnt, na, nb_ = _mixer(
            x_prompt, zero_a, zero_b, cnt0, params, nb=1, ts=ts, b0=b0, bsz=per_b)
        parts = [dict(h=h.reshape(per_b * seq, d), hnp=hnp, idx=idx, gate_t=gate, rank=rank)]
        states = [(na, nb_)]
        if g == len(sizes) - 1:
            h, hnp, idx, gate, rank, cnt, na_s, nb_s = _mixer(
                x_sample, _pad_hist(state_conv_a[0], HIST_A_ROWS), _pad_hist(state_conv_b[0], HIST_B_ROWS),
                cnt, params, nb=dec_b, ts=dec_s)
            parts.append(dict(h=h.reshape(dec_b * dec_s, d), hnp=hnp, idx=idx, gate_t=gate, rank=rank))
        n_pairs = sum(p["hnp"].shape[0] for p in parts) * TOP_K
        plan, pad_start, n_blocks, has, reuse_tables = _block_plan(cnt[:, 0].astype(jnp.int32), n_pairs, n_exp)
        pos = _positions(pad_start, [(p["idx"], p["rank"]) for p in parts])
        xs = _sc_dispatch([(q, p["hnp"]) for q, p in zip(pos, parts)], n_blocks * MOE_BLOCK)
        groups.append(dict(parts=parts, pos=pos, plan=plan, xs=xs, states=states, has=has,
                           reuse_tables=reuse_tables))

    weights = (w_gate_up[0], b_gate_up[0], w_down[0], b_down[0])
    pub = groups[-1]
    if len(groups) > 1:
        pub["rows"], wgu16, wd16 = _experts(pub["plan"], pub["xs"], *weights, publish=True)
        for grp in groups[:-1]:
            grp["rows"] = _experts(grp["plan"], grp["xs"], *weights,
                                   reuse=(wgu16, wd16) + grp["reuse_tables"](pub["has"]))
    else:
        pub["rows"] = _experts(pub["plan"], pub["xs"], *weights)

    gathered = []
    for g, grp in enumerate(groups):
        pos0 = grp["pos"][0]
        n_g = pos0.shape[1]
        want = 1 if (g == len(groups) - 1 and len(groups) > 1) else COMBINE_CHUNKS
        n_chunks = want if n_g % (want * SC_CHUNK * SC_WORKERS) == 0 else 1
        per = n_g // n_chunks
        for c in range(n_chunks):
            pos_parts = [pos0[:, c * per:(c + 1) * per]] + (grp["pos"][1:] if c == 0 else [])
            gathered.append((g, c * per, _sc_gather(pos_parts, grp["rows"])))

    y_p = None
    for g, tok0, got in sorted(gathered, key=lambda t: (t[0] != len(groups) - 1, t[0], t[1])):
        grp = groups[g]
        main = grp["parts"][0]
        if len(got) > 1:
            smp = grp["parts"][1]
            y_s = _combine(got[1], smp["gate_t"], smp["h"], gfin).reshape(dec_b, dec_s, d)
        y_p = _combine(got[0], main["gate_t"], main["h"], gfin, tok0=tok0,
                       y_prev=y_p, y_rows=n_p, y_tok0=tok_start[g] + tok0)
    y_p = y_p.reshape(bsz, seq, d)

    na_p = jnp.concatenate([grp["states"][0][0] for grp in groups], axis=0)
    nb_p = jnp.concatenate([grp["states"][0][1] for grp in groups], axis=0)
    return (y_p, y_s, na_p[None], nb_p[None], na_s[None], nb_s[None])


def _block_plan(counts, n_pairs, n_exp):
    padded = (counts + MOE_BLOCK - 1) // MOE_BLOCK * MOE_BLOCK
    pad_end = jnp.cumsum(padded)
    pad_start = pad_end - padded
    n_blocks = -(-n_pairs // MOE_BLOCK) + n_exp
    n_blocks = -(-n_blocks // BLOCKS_PER_STEP) * BLOCKS_PER_STEP
    blk_start = jnp.arange(n_blocks, dtype=jnp.int32) * MOE_BLOCK
    blk_exp = jnp.minimum(jnp.sum(blk_start[:, None] >= pad_end[None, :], axis=1),
                          n_exp - 1).astype(jnp.int32)
    n_active = (pad_end[-1:] // MOE_BLOCK).astype(jnp.int32)
    e_ids = jnp.arange(n_exp, dtype=jnp.int32)
    has = padded > 0
    slot_e = (jnp.cumsum(has.astype(jnp.int32)) - 1) % 2
    later = jnp.where(has, e_ids, n_exp)
    next_e = jnp.concatenate([lax.cummin(later, reverse=True)[1:], jnp.full((1,), n_exp, jnp.int32)])
    next_e = jnp.where(next_e >= n_exp, -1, next_e)
    blk_onehot = blk_exp[:, None] == e_ids[None, :]

    def per_block(table):
        return jnp.sum(jnp.where(blk_onehot, table[None, :], 0), axis=1).astype(jnp.int32)

    blk_first = ((blk_start == per_block(pad_start)) & (blk_start < pad_end[-1])).astype(jnp.int32)
    plan = (blk_exp, blk_first, per_block(slot_e), per_block(next_e), n_active)

    def reuse_tables(has_pub):
        pub = has_pub.astype(jnp.int32)
        pub_next = jnp.sum(jnp.where(next_e[:, None] == e_ids[None, :], pub[None, :], 0), axis=1)
        return per_block(pub), per_block(pub_next)

    return plan, pad_start.astype(jnp.int32), n_blocks, has, reuse_tables
```
